```python
import jax, jax.numpy as jnp
from jax import lax
import numpy as np

D_MODEL = 2048
BATCH = 8
SEQ = 4096
DEPTH = 2

N_EVEN = (DEPTH + 1) // 2
N_ODD = DEPTH // 2
PAR_WIDTH = D_MODEL
SB_WIDTH = PAR_WIDTH // 2
SB_HEADS = 8
SB_HEAD_DIM = SB_WIDTH // SB_HEADS
SB_BLOCK = 128
HG_WIDTH = PAR_WIDTH - SB_WIDTH
HG_HEADS = 8
HG_VDIM = HG_WIDTH // HG_HEADS
HG_KDIM = 128
HG_KEY_WIDTH = HG_HEADS * HG_KDIM
HG_CHUNK = 64
PAR_IN = 3 * SB_WIDTH + 2 * HG_KEY_WIDTH + 2 * HG_WIDTH
PAR_SPLITS = [SB_WIDTH, 2 * SB_WIDTH, 3 * SB_WIDTH, 3 * SB_WIDTH + HG_KEY_WIDTH,
              3 * SB_WIDTH + 2 * HG_KEY_WIDTH, 3 * SB_WIDTH + 2 * HG_KEY_WIDTH + HG_WIDTH]
SG_WIDTH = D_MODEL
SG_GROUPS = 8
SG_CHUNK = 128
FFN_DIM = 11 * D_MODEL // 4
CONV_WIDTH = 3
NORM_EPS = 1e-6

kernel_name = "hybrid_stickbreak_hgrn2_gmlp_convffn_adaln"


def rms_norm(x, gain):
    x32 = x.astype(jnp.float32)
    y = x32 * lax.rsqrt(jnp.mean(x32 * x32, axis=-1, keepdims=True) + NORM_EPS)
    return (y * gain.astype(jnp.float32)).astype(x.dtype)


def layer_norm(x, gain, bias):
    x32 = x.astype(jnp.float32)
    mu = jnp.mean(x32, axis=-1, keepdims=True)
    xc = x32 - mu
    y = xc * lax.rsqrt(jnp.mean(xc * xc, axis=-1, keepdims=True) + NORM_EPS)
    return (y * gain.astype(jnp.float32) + bias.astype(jnp.float32)).astype(x.dtype)


def stick_breaking_attention(q, k, v):
    S = q.shape[2]
    scale = q.shape[-1] ** -0.5
    outs = []
    for blk in range(S // SB_BLOCK):
        q0, q1 = blk * SB_BLOCK, (blk + 1) * SB_BLOCK
        z = jnp.einsum('bhtd,bhsd->bhts', q[:, :, q0:q1], k[:, :, :q1]).astype(jnp.float32) * scale
        t_idx = q0 + jnp.arange(SB_BLOCK)[:, None]
        s_idx = jnp.arange(q1)[None, :]
        strict = s_idx < t_idx
        log_keep = jnp.where(strict, jax.nn.log_sigmoid(-z), 0.0)
        tail = lax.cumsum(log_keep, axis=3, reverse=True) - log_keep
        w = jnp.where(strict, jnp.exp(jax.nn.log_sigmoid(z) + tail), 0.0)
        outs.append(jnp.einsum('bhts,bhsd->bhtd', w.astype(v.dtype), v[:, :, :q1]))
    return jnp.concatenate(outs, axis=2)


def hgrn2_recurrence(q, f_logit, i, g, lower_bound, out_gain):
    dtype = q.dtype
    Bn, S, H, dk = q.shape
    dv = i.shape[-1]
    n = S // HG_CHUNK
    lb = lower_bound.astype(jnp.float32)
    f = lb + (1.0 - lb) * jax.nn.sigmoid(f_logit.astype(jnp.float32))
    log_f = jnp.log(f)
    k = 1.0 - f
    qf = jax.nn.silu(q.astype(jnp.float32))

    def to_chunks(t):
        return t.transpose(0, 2, 1, 3).reshape(Bn, H, n, HG_CHUNK, t.shape[-1])

    qc, kc, vc, lc = (to_chunks(t) for t in (qf, k, i.astype(jnp.float32), log_f))
    G = jnp.cumsum(lc, axis=3)
    G_last = G[:, :, :, -1:, :]
    q_dec = qc * jnp.exp(G)
    k_inv = kc * jnp.exp(-G)
    k_end = kc * jnp.exp(G_last - G)
    causal = jnp.tril(jnp.ones((HG_CHUNK, HG_CHUNK), dtype=bool))
    scores = jnp.where(causal, jnp.einsum('bhntd,bhnsd->bhnts', q_dec, k_inv), 0.0)
    o_intra = jnp.einsum('bhnts,bhnsv->bhntv', scores, vc)

    def step(state, xs):
        q_d, k_e, v_c, decay = xs
        o = jnp.einsum('bhtd,bhdv->bhtv', q_d, state)
        state = decay[..., None] * state + jnp.einsum('bhtd,bhtv->bhdv', k_e, v_c)
        return state, o

    xs = tuple(jnp.moveaxis(t, 2, 0) for t in (q_dec, k_end, vc, jnp.exp(G_last[:, :, :, 0, :])))
    state0 = jnp.zeros((Bn, H, dk, dv), jnp.float32)
    _, o_inter = lax.scan(step, state0, xs)
    o = o_intra + jnp.moveaxis(o_inter, 0, 2)
    o = o.reshape(Bn, H, S, dv).transpose(0, 2, 1, 3)
    o = o * lax.rsqrt(jnp.mean(o * o, axis=-1, keepdims=True) + NORM_EPS) * out_gain.astype(jnp.float32)
    return (o * jax.nn.silu(g.astype(jnp.float32))).astype(dtype)


def parallel_mixer(h, w_in, w_out, lower_bound, hg_gain):
    Bn, S, _ = h.shape
    proj = h @ w_in
    sb_q, sb_k, sb_v, hg_q, hg_f, hg_i, hg_g = jnp.split(proj, PAR_SPLITS, axis=-1)
    bhsd = lambda t: t.reshape(Bn, S, SB_HEADS, SB_HEAD_DIM).transpose(0, 2, 1, 3)
    o_sb = stick_breaking_attention(bhsd(sb_q), bhsd(sb_k), bhsd(sb_v))
    o_sb = o_sb.transpose(0, 2, 1, 3).reshape(Bn, S, SB_WIDTH)
    bshd = lambda t: t.reshape(Bn, S, HG_HEADS, -1)
    o_hg = hgrn2_recurrence(bshd(hg_q), bshd(hg_f), bshd(hg_i), bshd(hg_g),
                            lower_bound.reshape(HG_HEADS, HG_KDIM), hg_gain)
    o_hg = o_hg.reshape(Bn, S, HG_WIDTH)
    return jnp.concatenate([o_sb, o_hg], axis=-1) @ w_out


def chunked_spatial_gating_mlp(h, w_in, v_gain, v_bias, w_pos, b_pos, w_out):
    Bn, S, _ = h.shape
    z = jax.nn.gelu(h @ w_in, approximate=False)
    u, v = jnp.split(z, 2, axis=-1)
    v = layer_norm(v, v_gain, v_bias)
    n = S // SG_CHUNK
    vg = v.reshape(Bn, n, SG_CHUNK, SG_GROUPS, SG_WIDTH // SG_GROUPS)
    causal = jnp.tril(jnp.ones((SG_CHUNK, SG_CHUNK), dtype=bool))
    w = jnp.where(causal, w_pos, 0.0).astype(v.dtype)
    mixed = jnp.einsum('gts,bnsgc->bntgc', w, vg) + b_pos.T[None, None, :, :, None]
    return (u * mixed.reshape(Bn, S, SG_WIDTH)) @ w_out


def conv_ffn(h, w_up, conv_w, conv_b, w_down):
    S = h.shape[1]
    a = h @ w_up
    ap = jnp.pad(a, ((0, 0), (CONV_WIDTH - 1, 0), (0, 0)))
    a = conv_b + sum(conv_w[j] * ap[:, j:j + S] for j in range(CONV_WIDTH))
    gate, val = jnp.split(a, 2, axis=-1)
    return (jax.nn.silu(gate) * val) @ w_down


def _fwd_setup_inputs(seed: int = 0) -> dict:
    key = jax.random.key(seed)
    ks = jax.random.split(key, 24)
    f32 = jnp.float32
    nrm = lambda k, shape, scale: jax.random.normal(k, shape, f32) * scale
    gain = lambda k, shape: 1.0 + 0.05 * jax.random.normal(k, shape, f32)
    D = D_MODEL
    return {
        "x": nrm(ks[0], (BATCH, SEQ, D), 1.0),
        "c": nrm(ks[1], (BATCH, D), 1.0),
        "ada_w": nrm(ks[2], (DEPTH, D, 6 * D), D ** -0.5),
        "ada_b": nrm(ks[3], (DEPTH, 6 * D), 0.01),
        "mix_norm": gain(ks[4], (DEPTH, D)),
        "ffn_norm": gain(ks[5], (DEPTH, D)),
        "par_w_in": nrm(ks[6], (N_EVEN, D, PAR_IN), D ** -0.5),
        "par_w_out": nrm(ks[7], (N_EVEN, PAR_WIDTH, D), PAR_WIDTH ** -0.5),
        "hg_lb_logits": nrm(ks[8], (N_EVEN + 1, HG_KEY_WIDTH), 0.1),
        "hg_out_norm": gain(ks[9], (N_EVEN, HG_HEADS, HG_VDIM)),
        "sg_w_in": nrm(ks[10], (N_ODD, D, 2 * SG_WIDTH), D ** -0.5),
        "sg_v_gain": gain(ks[11], (N_ODD, SG_WIDTH)),
        "sg_v_bias": nrm(ks[12], (N_ODD, SG_WIDTH), 0.02),
        "sg_w_pos": nrm(ks[13], (N_ODD, SG_GROUPS, SG_CHUNK, SG_CHUNK), SG_CHUNK ** -0.5),
        "sg_b_pos": nrm(ks[14], (N_ODD, SG_GROUPS, SG_CHUNK), 0.02),
        "sg_w_out": nrm(ks[15], (N_ODD, SG_WIDTH, D), SG_WIDTH ** -0.5),
        "ffn_up": nrm(ks[16], (DEPTH, D, 2 * FFN_DIM), D ** -0.5),
        "ffn_conv_w": nrm(ks[17], (DEPTH, CONV_WIDTH, 2 * FFN_DIM), CONV_WIDTH ** -0.5),
        "ffn_conv_b": nrm(ks[18], (DEPTH, 2 * FFN_DIM), 0.02),
        "ffn_down": nrm(ks[19], (DEPTH, FFN_DIM, D), FFN_DIM ** -0.5),
        "final_norm": gain(ks[20], (D,)),
    }


def _fwd_reference(x, c, ada_w, ada_b, mix_norm, ffn_norm, par_w_in, par_w_out, hg_lb_logits,
              hg_out_norm, sg_w_in, sg_v_gain, sg_v_bias, sg_w_pos, sg_b_pos, sg_w_out,
              ffn_up, ffn_conv_w, ffn_conv_b, ffn_down, final_norm):
    cond = jax.nn.silu(c)
    lower_bounds = jnp.cumsum(jax.nn.softmax(hg_lb_logits.astype(jnp.float32), axis=0), axis=0)
    for layer in range(DEPTH):
        j = layer // 2
        mod = cond @ ada_w[layer] + ada_b[layer]
        sh1, sc1, g1, sh2, sc2, g2 = jnp.split(mod[:, None, :], 6, axis=-1)
        h = rms_norm(x, mix_norm[layer]) * (1.0 + sc1) + sh1
        if layer % 2 == 0:
            y = parallel_mixer(h, par_w_in[j], par_w_out[j], lower_bounds[j], hg_out_norm[j])
        else:
            y = chunked_spatial_gating_mlp(h, sg_w_in[j], sg_v_gain[j], sg_v_bias[j],
                                           sg_w_pos[j], sg_b_pos[j], sg_w_out[j])
        x = x + g1 * y
        h = rms_norm(x, ffn_norm[layer]) * (1.0 + sc2) + sh2
        x = x + g2 * conv_ffn(h, ffn_up[layer], ffn_conv_w[layer], ffn_conv_b[layer], ffn_down[layer])
    return rms_norm(x, final_norm)


import jax as _jax
import jax.numpy as _jnp

TWIN_FORMAT = 'train_step'
FWD_PARAMS = ['x', 'c', 'ada_w', 'ada_b', 'mix_norm', 'ffn_norm', 'par_w_in', 'par_w_out', 'hg_lb_logits', 'hg_out_norm', 'sg_w_in', 'sg_v_gain', 'sg_v_bias', 'sg_w_pos', 'sg_b_pos', 'sg_w_out', 'ffn_up', 'ffn_conv_w', 'ffn_conv_b', 'ffn_down', 'final_norm']
TWIN_WEIGHTS = ['ada_w', 'ada_b', 'mix_norm', 'ffn_norm', 'par_w_in', 'par_w_out', 'hg_lb_logits', 'hg_out_norm', 'sg_w_in', 'sg_v_gain', 'sg_v_bias', 'sg_w_pos', 'sg_b_pos', 'sg_w_out', 'ffn_up', 'ffn_conv_w', 'ffn_conv_b', 'ffn_down', 'final_norm']
TWIN_DIFF_INPUT = 'x'
TWIN_INPUTS = ['x', 'c', 'ada_w', 'ada_b', 'mix_norm', 'ffn_norm', 'par_w_in', 'par_w_out', 'hg_lb_logits', 'hg_out_norm', 'sg_w_in', 'sg_v_gain', 'sg_v_bias', 'sg_w_pos', 'sg_b_pos', 'sg_w_out', 'ffn_up', 'ffn_conv_w', 'ffn_conv_b', 'ffn_down', 'final_norm', 'loss_target', 'm_ada_w', 'm_ada_b', 'm_mix_norm', 'm_ffn_norm', 'm_par_w_in', 'm_par_w_out', 'm_hg_lb_logits', 'm_hg_out_norm', 'm_sg_w_in', 'm_sg_v_gain', 'm_sg_v_bias', 'm_sg_w_pos', 'm_sg_b_pos', 'm_sg_w_out', 'm_ffn_up', 'm_ffn_conv_w', 'm_ffn_conv_b', 'm_ffn_down', 'm_final_norm', 'v_ada_w', 'v_ada_b', 'v_mix_norm', 'v_ffn_norm', 'v_par_w_in', 'v_par_w_out', 'v_hg_lb_logits', 'v_hg_out_norm', 'v_sg_w_in', 'v_sg_v_gain', 'v_sg_v_bias', 'v_sg_w_pos', 'v_sg_b_pos', 'v_sg_w_out', 'v_ffn_up', 'v_ffn_conv_w', 'v_ffn_conv_b', 'v_ffn_down', 'v_final_norm']
TWIN_OUTPUTS = ['loss', 'grad_x', 'grad_ada_w', 'grad_ada_b', 'grad_mix_norm', 'grad_ffn_norm', 'grad_par_w_in', 'grad_par_w_out', 'grad_hg_lb_logits', 'grad_hg_out_norm', 'grad_sg_w_in', 'grad_sg_v_gain', 'grad_sg_v_bias', 'grad_sg_w_pos', 'grad_sg_b_pos', 'grad_sg_w_out', 'grad_ffn_up', 'grad_ffn_conv_w', 'grad_ffn_conv_b', 'grad_ffn_down', 'grad_final_norm', 'delta_ada_w', 'delta_ada_b', 'delta_mix_norm', 'delta_ffn_norm', 'delta_par_w_in', 'delta_par_w_out', 'delta_hg_lb_logits', 'delta_hg_out_norm', 'delta_sg_w_in', 'delta_sg_v_gain', 'delta_sg_v_bias', 'delta_sg_w_pos', 'delta_sg_b_pos', 'delta_sg_w_out', 'delta_ffn_up', 'delta_ffn_conv_w', 'delta_ffn_conv_b', 'delta_ffn_down', 'delta_final_norm', 'new_m_ada_w', 'new_m_ada_b', 'new_m_mix_norm', 'new_m_ffn_norm', 'new_m_par_w_in', 'new_m_par_w_out', 'new_m_hg_lb_logits', 'new_m_hg_out_norm', 'new_m_sg_w_in', 'new_m_sg_v_gain', 'new_m_sg_v_bias', 'new_m_sg_w_pos', 'new_m_sg_b_pos', 'new_m_sg_w_out', 'new_m_ffn_up', 'new_m_ffn_conv_w', 'new_m_ffn_conv_b', 'new_m_ffn_down', 'new_m_final_norm', 'new_v_ada_w', 'new_v_ada_b', 'new_v_mix_norm', 'new_v_ffn_norm', 'new_v_par_w_in', 'new_v_par_w_out', 'new_v_hg_lb_logits', 'new_v_hg_out_norm', 'new_v_sg_w_in', 'new_v_sg_v_gain', 'new_v_sg_v_bias', 'new_v_sg_w_pos', 'new_v_sg_b_pos', 'new_v_sg_w_out', 'new_v_ffn_up', 'new_v_ffn_conv_w', 'new_v_ffn_conv_b', 'new_v_ffn_down', 'new_v_final_norm']
TWIN_LEAF_KINDS = {'loss': 'loss', 'grad_x': 'grad_x', 'grad_ada_w': 'grad_w', 'grad_ada_b': 'grad_w', 'grad_mix_norm': 'grad_w', 'grad_ffn_norm': 'grad_w', 'grad_par_w_in': 'grad_w', 'grad_par_w_out': 'grad_w', 'grad_hg_lb_logits': 'grad_w', 'grad_hg_out_norm': 'grad_w', 'grad_sg_w_in': 'grad_w', 'grad_sg_v_gain': 'grad_w', 'grad_sg_v_bias': 'grad_w', 'grad_sg_w_pos': 'grad_w', 'grad_sg_b_pos': 'grad_w', 'grad_sg_w_out': 'grad_w', 'grad_ffn_up': 'grad_w', 'grad_ffn_conv_w': 'grad_w', 'grad_ffn_conv_b': 'grad_w', 'grad_ffn_down': 'grad_w', 'grad_final_norm': 'grad_w', 'delta_ada_w': 'delta_w', 'delta_ada_b': 'delta_w', 'delta_mix_norm': 'delta_w', 'delta_ffn_norm': 'delta_w', 'delta_par_w_in': 'delta_w', 'delta_par_w_out': 'delta_w', 'delta_hg_lb_logits': 'delta_w', 'delta_hg_out_norm': 'delta_w', 'delta_sg_w_in': 'delta_w', 'delta_sg_v_gain': 'delta_w', 'delta_sg_v_bias': 'delta_w', 'delta_sg_w_pos': 'delta_w', 'delta_sg_b_pos': 'delta_w', 'delta_sg_w_out': 'delta_w', 'delta_ffn_up': 'delta_w', 'delta_ffn_conv_w': 'delta_w', 'delta_ffn_conv_b': 'delta_w', 'delta_ffn_down': 'delta_w', 'delta_final_norm': 'delta_w', 'new_m_ada_w': 'new_m', 'new_m_ada_b': 'new_m', 'new_m_mix_norm': 'new_m', 'new_m_ffn_norm': 'new_m', 'new_m_par_w_in': 'new_m', 'new_m_par_w_out': 'new_m', 'new_m_hg_lb_logits': 'new_m', 'new_m_hg_out_norm': 'new_m', 'new_m_sg_w_in': 'new_m', 'new_m_sg_v_gain': 'new_m', 'new_m_sg_v_bias': 'new_m', 'new_m_sg_w_pos': 'new_m', 'new_m_sg_b_pos': 'new_m', 'new_m_sg_w_out': 'new_m', 'new_m_ffn_up': 'new_m', 'new_m_ffn_conv_w': 'new_m', 'new_m_ffn_conv_b': 'new_m', 'new_m_ffn_down': 'new_m', 'new_m_final_norm': 'new_m', 'new_v_ada_w': 'new_v', 'new_v_ada_b': 'new_v', 'new_v_mix_norm': 'new_v', 'new_v_ffn_norm': 'new_v', 'new_v_par_w_in': 'new_v', 'new_v_par_w_out': 'new_v', 'new_v_hg_lb_logits': 'new_v', 'new_v_hg_out_norm': 'new_v', 'new_v_sg_w_in': 'new_v', 'new_v_sg_v_gain': 'new_v', 'new_v_sg_v_bias': 'new_v', 'new_v_sg_w_pos': 'new_v', 'new_v_sg_b_pos': 'new_v', 'new_v_sg_w_out': 'new_v', 'new_v_ffn_up': 'new_v', 'new_v_ffn_conv_w': 'new_v', 'new_v_ffn_conv_b': 'new_v', 'new_v_ffn_down': 'new_v', 'new_v_final_norm': 'new_v'}


def _forward(args):
    return _fwd_reference(*[args[k] for k in FWD_PARAMS])


def _output_shape():
    def fwd():
        inp = _fwd_setup_inputs(0)
        return _fwd_reference(*[inp[k] for k in FWD_PARAMS])
    out = _jax.eval_shape(fwd)
    return out.shape, out.dtype

N_MICROBATCH = 1
ADAM_LR = 0.001
ADAM_B1 = 0.9
ADAM_B2 = 0.999
ADAM_EPS = 1e-08
ADAM_WD = 0.01
ADAM_STEP = 10
PER_EXAMPLE_BATCH_AXIS = {'x': 0, 'c': 0, 'loss_target': 0}
SHARED_INPUTS = []
_WEIGHT_DTYPES = {'ada_w': _jnp.float32, 'ada_b': _jnp.float32, 'mix_norm': _jnp.float32, 'ffn_norm': _jnp.float32, 'par_w_in': _jnp.float32, 'par_w_out': _jnp.float32, 'hg_lb_logits': _jnp.float32, 'hg_out_norm': _jnp.float32, 'sg_w_in': _jnp.float32, 'sg_v_gain': _jnp.float32, 'sg_v_bias': _jnp.float32, 'sg_w_pos': _jnp.float32, 'sg_b_pos': _jnp.float32, 'sg_w_out': _jnp.float32, 'ffn_up': _jnp.float32, 'ffn_conv_w': _jnp.float32, 'ffn_conv_b': _jnp.float32, 'ffn_down': _jnp.float32, 'final_norm': _jnp.float32}
MOMENT_SCALE = {'ada_w': 5.406285e-02, 'ada_b': 1.051994e-01, 'mix_norm': 4.557651e-02, 'ffn_norm': 5.958364e-02, 'par_w_in': 4.062806e-02, 'par_w_out': 6.114083e-02, 'hg_lb_logits': 2.779674e-03, 'hg_out_norm': 4.639407e-02, 'sg_w_in': 2.784653e-02, 'sg_v_gain': 2.551142e-02, 'sg_v_bias': 2.517536e-02, 'sg_w_pos': 3.613076e-02, 'sg_b_pos': 5.247021e-02, 'sg_w_out': 2.556518e-02, 'ffn_up': 2.910031e-02, 'ffn_conv_w': 3.048657e-02, 'ffn_conv_b': 2.623706e-02, 'ffn_down': 4.964160e-02, 'final_norm': 1.626958e+01}


def _to_microbatches(a, axis):
    t = _jnp.moveaxis(a, axis, 0)
    t = t.reshape((N_MICROBATCH, t.shape[0] // N_MICROBATCH) + t.shape[1:])
    return _jnp.moveaxis(t, 1, axis + 1)


def setup_inputs(seed: int = 0) -> dict:
    inp = _fwd_setup_inputs(seed)
    key = _jax.random.fold_in(_jax.random.key(seed), 7919)
    shape, _ = _output_shape()
    out = dict(inp)
    out["loss_target"] = _jax.random.normal(_jax.random.fold_in(key, 0), shape, _jnp.float32)
    for i, name in enumerate(TWIN_WEIGHTS):
        w = inp[name].astype(_jnp.float32)
        if MOMENT_SCALE is None:
            s = _jnp.sqrt(_jnp.mean(_jnp.square(w)) + 1e-30)
        else:
            s = MOMENT_SCALE[name]
        km, kv = _jax.random.split(_jax.random.fold_in(key, i + 1))
        out[name] = w
        out["m_" + name] = s * _jax.random.normal(km, w.shape, _jnp.float32)
        out["v_" + name] = (s * s) * _jax.random.uniform(kv, w.shape, _jnp.float32, 0.5, 1.5)
    if N_MICROBATCH > 1:
        for name, axis in PER_EXAMPLE_BATCH_AXIS.items():
            out[name] = _to_microbatches(out[name], axis)
    return {'x': out['x'], 'c': out['c'], 'ada_w': out['ada_w'], 'ada_b': out['ada_b'], 'mix_norm': out['mix_norm'], 'ffn_norm': out['ffn_norm'], 'par_w_in': out['par_w_in'], 'par_w_out': out['par_w_out'], 'hg_lb_logits': out['hg_lb_logits'], 'hg_out_norm': out['hg_out_norm'], 'sg_w_in': out['sg_w_in'], 'sg_v_gain': out['sg_v_gain'], 'sg_v_bias': out['sg_v_bias'], 'sg_w_pos': out['sg_w_pos'], 'sg_b_pos': out['sg_b_pos'], 'sg_w_out': out['sg_w_out'], 'ffn_up': out['ffn_up'], 'ffn_conv_w': out['ffn_conv_w'], 'ffn_conv_b': out['ffn_conv_b'], 'ffn_down': out['ffn_down'], 'final_norm': out['final_norm'], 'loss_target': out['loss_target'], 'm_ada_w': out['m_ada_w'], 'm_ada_b': out['m_ada_b'], 'm_mix_norm': out['m_mix_norm'], 'm_ffn_norm': out['m_ffn_norm'], 'm_par_w_in': out['m_par_w_in'], 'm_par_w_out': out['m_par_w_out'], 'm_hg_lb_logits': out['m_hg_lb_logits'], 'm_hg_out_norm': out['m_hg_out_norm'], 'm_sg_w_in': out['m_sg_w_in'], 'm_sg_v_gain': out['m_sg_v_gain'], 'm_sg_v_bias': out['m_sg_v_bias'], 'm_sg_w_pos': out['m_sg_w_pos'], 'm_sg_b_pos': out['m_sg_b_pos'], 'm_sg_w_out': out['m_sg_w_out'], 'm_ffn_up': out['m_ffn_up'], 'm_ffn_conv_w': out['m_ffn_conv_w'], 'm_ffn_conv_b': out['m_ffn_conv_b'], 'm_ffn_down': out['m_ffn_down'], 'm_final_norm': out['m_final_norm'], 'v_ada_w': out['v_ada_w'], 'v_ada_b': out['v_ada_b'], 'v_mix_norm': out['v_mix_norm'], 'v_ffn_norm': out['v_ffn_norm'], 'v_par_w_in': out['v_par_w_in'], 'v_par_w_out': out['v_par_w_out'], 'v_hg_lb_logits': out['v_hg_lb_logits'], 'v_hg_out_norm': out['v_hg_out_norm'], 'v_sg_w_in': out['v_sg_w_in'], 'v_sg_v_gain': out['v_sg_v_gain'], 'v_sg_v_bias': out['v_sg_v_bias'], 'v_sg_w_pos': out['v_sg_w_pos'], 'v_sg_b_pos': out['v_sg_b_pos'], 'v_sg_w_out': out['v_sg_w_out'], 'v_ffn_up': out['v_ffn_up'], 'v_ffn_conv_w': out['v_ffn_conv_w'], 'v_ffn_conv_b': out['v_ffn_conv_b'], 'v_ffn_down': out['v_ffn_down'], 'v_final_norm': out['v_final_norm']}


def _loss(weights, diff, rest, loss_target):
    with _jax.named_scope("forward"):
        args = {**rest, TWIN_DIFF_INPUT: diff, **{k: w.astype(_WEIGHT_DTYPES[k]) for k, w in weights.items()}}
        y = _forward(args)
    with _jax.named_scope("loss_head"):
        err = _jnp.square(y.astype(_jnp.float32) - loss_target)
        return 0.5 * _jnp.sum(_jnp.mean(err, axis=-1)) if err.ndim else 0.5 * err


def _adamw(w, g, m, v):
    m = ADAM_B1 * m + (1.0 - ADAM_B1) * g
    v = ADAM_B2 * v + (1.0 - ADAM_B2) * _jnp.square(g)
    m_hat = m / (1.0 - ADAM_B1 ** ADAM_STEP)
    v_hat = v / (1.0 - ADAM_B2 ** ADAM_STEP)
    delta = -ADAM_LR * (m_hat / (_jnp.sqrt(v_hat) + ADAM_EPS) + ADAM_WD * w)
    return delta, m, v


def reference(x, c, ada_w, ada_b, mix_norm, ffn_norm, par_w_in, par_w_out, hg_lb_logits, hg_out_norm, sg_w_in, sg_v_gain, sg_v_bias, sg_w_pos, sg_b_pos, sg_w_out, ffn_up, ffn_conv_w, ffn_conv_b, ffn_down, final_norm, loss_target, m_ada_w, m_ada_b, m_mix_norm, m_ffn_norm, m_par_w_in, m_par_w_out, m_hg_lb_logits, m_hg_out_norm, m_sg_w_in, m_sg_v_gain, m_sg_v_bias, m_sg_w_pos, m_sg_b_pos, m_sg_w_out, m_ffn_up, m_ffn_conv_w, m_ffn_conv_b, m_ffn_down, m_final_norm, v_ada_w, v_ada_b, v_mix_norm, v_ffn_norm, v_par_w_in, v_par_w_out, v_hg_lb_logits, v_hg_out_norm, v_sg_w_in, v_sg_v_gain, v_sg_v_bias, v_sg_w_pos, v_sg_b_pos, v_sg_w_out, v_ffn_up, v_ffn_conv_w, v_ffn_conv_b, v_ffn_down, v_final_norm):
    given = dict(x=x, c=c, ada_w=ada_w, ada_b=ada_b, mix_norm=mix_norm, ffn_norm=ffn_norm, par_w_in=par_w_in, par_w_out=par_w_out, hg_lb_logits=hg_lb_logits, hg_out_norm=hg_out_norm, sg_w_in=sg_w_in, sg_v_gain=sg_v_gain, sg_v_bias=sg_v_bias, sg_w_pos=sg_w_pos, sg_b_pos=sg_b_pos, sg_w_out=sg_w_out, ffn_up=ffn_up, ffn_conv_w=ffn_conv_w, ffn_conv_b=ffn_conv_b, ffn_down=ffn_down, final_norm=final_norm, loss_target=loss_target, m_ada_w=m_ada_w, m_ada_b=m_ada_b, m_mix_norm=m_mix_norm, m_ffn_norm=m_ffn_norm, m_par_w_in=m_par_w_in, m_par_w_out=m_par_w_out, m_hg_lb_logits=m_hg_lb_logits, m_hg_out_norm=m_hg_out_norm, m_sg_w_in=m_sg_w_in, m_sg_v_gain=m_sg_v_gain, m_sg_v_bias=m_sg_v_bias, m_sg_w_pos=m_sg_w_pos, m_sg_b_pos=m_sg_b_pos, m_sg_w_out=m_sg_w_out, m_ffn_up=m_ffn_up, m_ffn_conv_w=m_ffn_conv_w, m_ffn_conv_b=m_ffn_conv_b, m_ffn_down=m_ffn_down, m_final_norm=m_final_norm, v_ada_w=v_ada_w, v_ada_b=v_ada_b, v_mix_norm=v_mix_norm, v_ffn_norm=v_ffn_norm, v_par_w_in=v_par_w_in, v_par_w_out=v_par_w_out, v_hg_lb_logits=v_hg_lb_logits, v_hg_out_norm=v_hg_out_norm, v_sg_w_in=v_sg_w_in, v_sg_v_gain=v_sg_v_gain, v_sg_v_bias=v_sg_v_bias, v_sg_w_pos=v_sg_w_pos, v_sg_b_pos=v_sg_b_pos, v_sg_w_out=v_sg_w_out, v_ffn_up=v_ffn_up, v_ffn_conv_w=v_ffn_conv_w, v_ffn_conv_b=v_ffn_conv_b, v_ffn_down=v_ffn_down, v_final_norm=v_final_norm)
    weights = {n: given[n] for n in TWIN_WEIGHTS}
    shared = {n: given[n] for n in SHARED_INPUTS}
    per_example = {n: given[n] for n in ['x', 'c']}
    grad_fn = _jax.value_and_grad(_loss, argnums=(0, 1))

    def one_microbatch(ex, loss_target):
        ex = dict(ex)
        diff = ex.pop(TWIN_DIFF_INPUT)
        return grad_fn(weights, diff, {**shared, **ex}, loss_target)

    if N_MICROBATCH == 1:
        loss, (grad_w, grad_x) = one_microbatch(per_example, given["loss_target"])
    else:
        def body(carry, xs):
            loss_sum, grad_sum = carry
            l_k, (gw_k, gx_k) = one_microbatch(xs[0], xs[1])
            with _jax.named_scope("update"):
                return (loss_sum + l_k, _jax.tree.map(_jnp.add, grad_sum, gw_k)), gx_k

        init = (_jnp.zeros((), _jnp.float32), _jax.tree.map(_jnp.zeros_like, weights))
        (loss, grad_w), grad_x = _jax.lax.scan(body, init, (per_example, given["loss_target"]))
    with _jax.named_scope("update"):
        delta_w, new_m, new_v = {}, {}, {}
        for n in TWIN_WEIGHTS:
            delta_w[n], new_m[n], new_v[n] = _adamw(weights[n], grad_w[n], given["m_" + n], given["v_" + n])
    return (loss, grad_x, *[grad_w[n] for n in TWIN_WEIGHTS], *[delta_w[n] for n in TWIN_WEIGHTS],
            *[new_m[n] for n in TWIN_WEIGHTS], *[new_v[n] for n in TWIN_WEIGHTS])
```

```python
import functools
import math

import jax
import jax.numpy as jnp
from jax import lax
from jax.experimental import pallas as pl
from jax.experimental.pallas import tpu as pltpu

F32 = jnp.float32
BF16 = jnp.bfloat16
MESH = pl.DeviceIdType.MESH
ANY = pl.BlockSpec(memory_space=pl.ANY)

NORM_EPS = 1e-6
ADAM_LR = 0.001
ADAM_B1 = 0.9
ADAM_B2 = 0.999
ADAM_EPS = 1e-08
ADAM_WD = 0.01
ADAM_STEP = 10
CONV_WIDTH = 3
HEAD = 128
HG_CHUNK = 64
SG_CHUNK = 128
N_DEV = 8
N_CHIP = 4
V7X_VMEM_LIMIT = 56 * 1024 * 1024


def _cp(*sem):
    return pltpu.CompilerParams(dimension_semantics=sem if sem else None, vmem_limit_bytes=V7X_VMEM_LIMIT)


def _pick(n, prefs):
    for p in prefs:
        if p <= n and n % p == 0:
            return p
    return n


def _iota(shape, axis):
    return lax.broadcasted_iota(jnp.int32, shape, axis)


def _rows_within(R, row_bytes, budget):
    if R * row_bytes <= budget:
        return R
    for t in (1024, 512, 256, 128, 64, 32, 16):
        if R % t == 0 and t * row_bytes <= budget:
            return t
    return _pick(R, (16, 8))


def _pack_rows(arrays):
    flat = jnp.concatenate([a.reshape(-1) for a in arrays])
    pad = (-flat.size) % (8 * HEAD)
    return jnp.pad(flat, (0, pad)).reshape(-1, HEAD)


def _dg(a, b, ca, cb):
    return lax.dot_general(a.astype(BF16), b.astype(BF16), (((ca,), (cb,)), ((), ())), preferred_element_type=F32)


@jax.custom_vjp
def mm_nn(a, b):
    return _dg(a, b, 1, 0)


mm_nn.defvjp(lambda a, b: (_dg(a, b, 1, 0), (a, b)),
             lambda r, g: (_dg(g, r[1], 1, 1), _dg(r[0], g, 0, 0)))


@jax.custom_vjp
def mm_nt(a, b):
    return _dg(a, b, 1, 1)


mm_nt.defvjp(lambda a, b: (_dg(a, b, 1, 1), (a, b)),
             lambda r, g: (_dg(g, r[1], 1, 0), _dg(g, r[0], 0, 0)))


@jax.custom_vjp
def mm_tn(a, b):
    return _dg(a, b, 0, 0)


mm_tn.defvjp(lambda a, b: (_dg(a, b, 0, 0), (a, b)),
             lambda r, g: (_dg(r[1], g, 1, 1), _dg(r[0], g, 1, 0)))


def _split(x):
    hi = x.astype(BF16)
    lo = (x - hi.astype(F32)).astype(BF16)
    return hi, lo


def _sum_right(x, m01):
    hi, lo = _split(x)
    return _dg(hi, m01, 1, 0) + _dg(lo, m01, 1, 0)


def _sum_left_impl(m01, x, ca):
    hi, lo = _split(x)
    return _dg(m01, hi, ca, 0) + _dg(m01, lo, ca, 0)


@jax.custom_vjp
def _sum_left(m01, x):
    return _sum_left_impl(m01, x, 1)


_sum_left.defvjp(lambda m, x: (_sum_left_impl(m, x, 1), m),
                 lambda m, g: (None, _sum_left_impl(m, g, 0)))


def _sigmoid(x):
    return 1.0 / (1.0 + jnp.exp(-x))


def _softplus(z):
    return jnp.maximum(z, 0.0) + jnp.log(1.0 + jnp.exp(-jnp.abs(z)))


_INV_SQRT2 = 1.0 / math.sqrt(2.0)
_INV_SQRT2PI = 1.0 / math.sqrt(2.0 * math.pi)


@jax.custom_vjp
def _gelu(x):
    return 0.5 * x * (1.0 + lax.erf(x * _INV_SQRT2))


_gelu.defvjp(lambda x: (0.5 * x * (1.0 + lax.erf(x * _INV_SQRT2)), x),
             lambda x, g: (g * (0.5 * (1.0 + lax.erf(x * _INV_SQRT2)) + x * jnp.exp(-0.5 * x * x) * _INV_SQRT2PI),))


def _rms(x, gain):
    r = lax.rsqrt(jnp.mean(x * x, axis=-1, keepdims=True) + NORM_EPS)
    return x * r * gain


def _normmod(x, gain, sc, sh):
    return _rms(x, gain) * (1.0 + sc) + sh


def _mm_call(name, a, b, out_shape, out_dtype, dims, grid, a_spec, b_spec, o_spec, acc_shape):
    nk = grid[2]

    def body(a_ref, b_ref, o_ref, *scratch):
        part = lax.dot_general(a_ref[...].astype(BF16), b_ref[...].astype(BF16), dims, preferred_element_type=F32)
        if nk == 1:
            o_ref[...] = part.astype(o_ref.dtype)
            return
        acc_ref, = scratch
        k = pl.program_id(2)

        @pl.when(k == 0)
        def _():
            acc_ref[...] = part

        @pl.when(k > 0)
        def _():
            acc_ref[...] += part

        @pl.when(k == nk - 1)
        def _():
            o_ref[...] = acc_ref[...].astype(o_ref.dtype)

    return pl.pallas_call(
        body, name=name, grid=grid, in_specs=[a_spec, b_spec], out_specs=o_spec,
        out_shape=jax.ShapeDtypeStruct(out_shape, out_dtype),
        scratch_shapes=[] if nk == 1 else [pltpu.VMEM(acc_shape, F32)],
        compiler_params=_cp("parallel", "parallel", "arbitrary"),
    )(a, b)


def _mm_nn(name, a, b, out_dtype=F32):
    M, K = a.shape
    chunked = b.ndim == 3
    Nc = b.shape[-1]
    N = Nc * (b.shape[0] if chunked else 1)
    tm = _pick(M, (1024, 512, 256, 128, 64, 32, 16, 8))
    tn = _pick(Nc, (512, 256, 128))
    tk = _pick(K, (2048, 1408, 1024, 512, 256, 128))
    npc = Nc // tn
    if chunked:
        b_spec = pl.BlockSpec((None, tk, tn), lambda i, j, k: (j // npc, k, j % npc))
    else:
        b_spec = pl.BlockSpec((tk, tn), lambda i, j, k: (k, j))
    return _mm_call(name, a, b, (M, N), out_dtype, (((1,), (0,)), ((), ())), (M // tm, N // tn, K // tk),
                    pl.BlockSpec((tm, tk), lambda i, j, k: (i, k)), b_spec,
                    pl.BlockSpec((tm, tn), lambda i, j, k: (i, j)), (tm, tn))


def _mm_nt(name, a, b, out_dtype=F32):
    M, N = a.shape
    chunked = b.ndim == 3
    Nc = b.shape[-1]
    K = b.shape[-2]
    tm = _pick(M, (1024, 512, 256, 128, 64, 32, 16, 8))
    tn = _pick(K, (1024, 512, 256, 128))
    tk = _pick(Nc, (512, 256, 128))
    npc = Nc // tk
    if chunked:
        b_spec = pl.BlockSpec((None, tn, tk), lambda i, j, k: (k // npc, j, k % npc))
    else:
        b_spec = pl.BlockSpec((tn, tk), lambda i, j, k: (j, k))
    return _mm_call(name, a, b, (M, K), out_dtype, (((1,), (1,)), ((), ())), (M // tm, K // tn, N // tk),
                    pl.BlockSpec((tm, tk), lambda i, j, k: (i, k)), b_spec,
                    pl.BlockSpec((tm, tn), lambda i, j, k: (i, j)), (tm, tn))


def _mm_tn(name, a, b, chunks=1, out_dtype=BF16):
    T, K = a.shape
    N = b.shape[1]
    Nc = N // chunks
    tm = _pick(K, (1024, 512, 256, 128))
    tn = _pick(Nc, (1408, 1024, 896, 512, 256, 128))
    tk = _pick(T, (512, 256, 128))
    npc = Nc // tn
    if chunks > 1:
        shape = (chunks, K, Nc)
        o_spec = pl.BlockSpec((None, tm, tn), lambda i, j, k: (j // npc, i, j % npc))
    else:
        shape = (K, N)
        o_spec = pl.BlockSpec((tm, tn), lambda i, j, k: (i, j))
    return _mm_call(name, a, b, shape, out_dtype, (((0,), (0,)), ((), ())), (K // tm, N // tn, T // tk),
                    pl.BlockSpec((tk, tm), lambda i, j, k: (k, i)),
                    pl.BlockSpec((tk, tn), lambda i, j, k: (k, j)), o_spec, (tm, tn))


def _row_tile(T):
    return _pick(T, (256, 128, 64, 32, 16, 8))


def _vec_spec(D):
    return pl.BlockSpec((1, D), lambda i: (0, 0))


def _normmod_fwd(name, x, gain, sc, sh):
    T, D = x.shape
    bt = _row_tile(T)

    def body(x_ref, g_ref, sc_ref, sh_ref, h_ref):
        h_ref[...] = _normmod(x_ref[...], g_ref[...], sc_ref[...], sh_ref[...]).astype(h_ref.dtype)

    rows = pl.BlockSpec((bt, D), lambda i: (i, 0))
    return pl.pallas_call(body, name=name, grid=(T // bt,), in_specs=[rows] + [_vec_spec(D)] * 3, out_specs=rows,
                          out_shape=jax.ShapeDtypeStruct((T, D), BF16), compiler_params=_cp("parallel"))(x, gain, sc, sh)


def _res_normmod_fwd(name, x, y, g, gain, sc, sh):
    T, D = x.shape
    bt = _row_tile(T)

    def body(x_ref, y_ref, gate_ref, g_ref, sc_ref, sh_ref, x1_ref, h_ref):
        x1 = x_ref[...] + gate_ref[...] * y_ref[...]
        x1_ref[...] = x1
        h_ref[...] = _normmod(x1, g_ref[...], sc_ref[...], sh_ref[...]).astype(h_ref.dtype)

    rows = pl.BlockSpec((bt, D), lambda i: (i, 0))
    return pl.pallas_call(body, name=name, grid=(T // bt,), in_specs=[rows, rows] + [_vec_spec(D)] * 4,
                          out_specs=[rows, rows],
                          out_shape=[jax.ShapeDtypeStruct((T, D), F32), jax.ShapeDtypeStruct((T, D), BF16)],
                          compiler_params=_cp("parallel"))(x, y, g, gain, sc, sh)


def _final_fwd_bwd(x, y, g, gain, target):
    T, D = x.shape
    bt = _row_tile(T)

    def body(x_ref, y_ref, gate_ref, g_ref, t_ref, loss_ref, dx_ref, dy_ref, dgate_ref, dgain_ref):
        i = pl.program_id(0)
        yv = y_ref[...]
        gate = gate_ref[...]
        x4 = x_ref[...] + gate * yv
        out, vjp = jax.vjp(_rms, x4, g_ref[...])
        err = out - t_ref[...]
        dx4, dgain = vjp(err * (1.0 / D))
        part = 0.5 * jnp.sum(jnp.mean(err * err, axis=-1, keepdims=True), axis=0, keepdims=True)

        @pl.when(i == 0)
        def _():
            loss_ref[...] = jnp.zeros_like(loss_ref)
            dgate_ref[...] = jnp.zeros_like(dgate_ref)
            dgain_ref[...] = jnp.zeros_like(dgain_ref)

        loss_ref[...] += jnp.broadcast_to(part, loss_ref.shape)
        dx_ref[...] = dx4
        dy_ref[...] = (gate * dx4).astype(dy_ref.dtype)
        dgate_ref[...] += jnp.sum(dx4 * yv, axis=0, keepdims=True)
        dgain_ref[...] += dgain

    rows = pl.BlockSpec((bt, D), lambda i: (i, 0))
    vec = _vec_spec(D)
    return pl.pallas_call(
        body, name="final_loss", grid=(T // bt,), in_specs=[rows, rows, vec, vec, rows],
        out_specs=[pl.BlockSpec((1, HEAD), lambda i: (0, 0)), rows, rows, vec, vec],
        out_shape=[jax.ShapeDtypeStruct((1, HEAD), F32), jax.ShapeDtypeStruct((T, D), F32),
                   jax.ShapeDtypeStruct((T, D), BF16), jax.ShapeDtypeStruct((1, D), F32),
                   jax.ShapeDtypeStruct((1, D), F32)],
        compiler_params=_cp("arbitrary"))(x, y, g, gain, target)


def _block_bwd(name, dx_out, dh, x_in, gain, sc, sh, y_prev=None, g_prev=None):
    T, D = x_in.shape
    bt = _row_tile(T)
    has_prev = y_prev is not None

    def body(*refs):
        if has_prev:
            dxo_ref, dh_ref, x_ref, g_ref, sc_ref, sh_ref, y_ref, gp_ref, dx_ref, dgain_ref, dsc_ref, dsh_ref, dy_ref, dgp_ref = refs
        else:
            dxo_ref, dh_ref, x_ref, g_ref, sc_ref, sh_ref, dx_ref, dgain_ref, dsc_ref, dsh_ref = refs
        i = pl.program_id(0)
        _, vjp = jax.vjp(_normmod, x_ref[...], g_ref[...], sc_ref[...], sh_ref[...])
        dxn, dgain, dsc, dsh = vjp(dh_ref[...])
        dx = dxo_ref[...] + dxn
        dx_ref[...] = dx

        @pl.when(i == 0)
        def _():
            dgain_ref[...] = jnp.zeros_like(dgain_ref)
            dsc_ref[...] = jnp.zeros_like(dsc_ref)
            dsh_ref[...] = jnp.zeros_like(dsh_ref)
            if has_prev:
                dgp_ref[...] = jnp.zeros_like(dgp_ref)

        dgain_ref[...] += dgain
        dsc_ref[...] += dsc
        dsh_ref[...] += dsh
        if has_prev:
            dy_ref[...] = (gp_ref[...] * dx).astype(dy_ref.dtype)
            dgp_ref[...] += jnp.sum(dx * y_ref[...], axis=0, keepdims=True)

    rows = pl.BlockSpec((bt, D), lambda i: (i, 0))
    vec = _vec_spec(D)
    ins = [dx_out, dh, x_in, gain, sc, sh]
    in_specs = [rows, rows, rows, vec, vec, vec]
    out_specs = [rows, vec, vec, vec]
    out_shape = [jax.ShapeDtypeStruct((T, D), F32)] + [jax.ShapeDtypeStruct((1, D), F32)] * 3
    if has_prev:
        ins += [y_prev, g_prev]
        in_specs += [rows, vec]
        out_specs += [rows, vec]
        out_shape += [jax.ShapeDtypeStruct((T, D), BF16), jax.ShapeDtypeStruct((1, D), F32)]
    return pl.pallas_call(body, name=name, grid=(T // bt,), in_specs=in_specs, out_specs=out_specs,
                          out_shape=out_shape, compiler_params=_cp("arbitrary"))(*ins)


def _sb_tiles(T):
    return _pick(T, (256, 128)), HEAD


def _sb_fwd(proj, H):
    T = proj.shape[0]
    tq, tk = _sb_tiles(T)
    scale = HEAD ** -0.5

    def body(q_ref, k_ref, v_ref, o_ref, l_ref):
        i = pl.program_id(1)
        q = q_ref[...].astype(BF16)
        t_idx = i * tq + _iota((tq, tk), 0)
        later = (_iota((tk, tk), 0) > _iota((tk, tk), 1)).astype(BF16)
        nkb = (i + 1) * (tq // tk)

        def step(n, carry):
            c, acc = carry
            off = pl.multiple_of((nkb - 1 - n) * tk, tk)
            k = k_ref[pl.ds(off, tk), :].astype(BF16)
            v = v_ref[pl.ds(off, tk), :].astype(BF16)
            z = _dg(q, k, 1, 1) * scale
            strict = (off + _iota((tq, tk), 1)) < t_idx
            sp = _softplus(z)
            lk = jnp.where(strict, -sp, 0.0)
            tail = _sum_right(lk, later) + c
            w = jnp.where(strict, jnp.exp(z - sp + tail), 0.0)
            acc = acc + _dg(w, v, 1, 0)
            c = c + jnp.sum(lk, axis=1, keepdims=True)
            return c, acc

        c, acc = lax.fori_loop(0, nkb, step, (jnp.zeros((tq, 1), F32), jnp.zeros((tq, HEAD), F32)))
        o_ref[...] = acc.astype(o_ref.dtype)
        l_ref[...] = jnp.broadcast_to(c, (tq, HEAD))

    blk = pl.BlockSpec((tq, HEAD), lambda h, i: (i, h))
    return pl.pallas_call(
        body, name="sb_fwd", grid=(H, T // tq),
        in_specs=[blk, pl.BlockSpec((T, HEAD), lambda h, i: (0, H + h)), pl.BlockSpec((T, HEAD), lambda h, i: (0, 2 * H + h))],
        out_specs=[blk, blk],
        out_shape=[jax.ShapeDtypeStruct((T, H * HEAD), BF16), jax.ShapeDtypeStruct((T, H * HEAD), F32)],
        compiler_params=_cp("parallel", "arbitrary"))(proj, proj, proj)


def _sb_bwd(proj, do, L, H):
    T = proj.shape[0]
    tq, tk = _sb_tiles(T)
    scale = HEAD ** -0.5

    def body(q_ref, k_ref, v_ref, do_ref, l_ref, dq_ref, dk_ref, dv_ref):
        i = pl.program_id(1)

        @pl.when(i == 0)
        def _():
            dk_ref[...] = jnp.zeros_like(dk_ref)
            dv_ref[...] = jnp.zeros_like(dv_ref)

        q = q_ref[...].astype(BF16)
        do_ = do_ref[...].astype(BF16)
        total = l_ref[...]
        t_idx = i * tq + _iota((tq, tk), 0)
        upto = (_iota((tk, tk), 0) <= _iota((tk, tk), 1)).astype(BF16)
        before = (_iota((tk, tk), 0) < _iota((tk, tk), 1)).astype(BF16)
        nkb = (i + 1) * (tq // tk)

        def step(kb, carry):
            cp, ce, dq = carry
            off = pl.multiple_of(kb * tk, tk)
            k = k_ref[pl.ds(off, tk), :].astype(BF16)
            v = v_ref[pl.ds(off, tk), :].astype(BF16)
            z = _dg(q, k, 1, 1) * scale
            strict = (off + _iota((tq, tk), 1)) < t_idx
            sp = _softplus(z)
            lk = jnp.where(strict, -sp, 0.0)
            tail = total - (_sum_right(lk, upto) + cp)
            w = jnp.where(strict, jnp.exp(z - sp + tail), 0.0)
            e = w * _dg(do_, v, 1, 1)
            e_before = _sum_right(e, before) + ce
            sig = jnp.exp(z - sp)
            dz = jnp.where(strict, e * (1.0 - sig) - e_before * sig, 0.0) * scale
            dv_ref[pl.ds(off, tk), :] += _dg(w, do_, 0, 0)
            dk_ref[pl.ds(off, tk), :] += _dg(dz, q, 0, 0)
            dq = dq + _dg(dz, k, 1, 0)
            cp = cp + jnp.sum(lk, axis=1, keepdims=True)
            ce = ce + jnp.sum(e, axis=1, keepdims=True)
            return cp, ce, dq

        zero = jnp.zeros((tq, 1), F32)
        _, _, dq = lax.fori_loop(0, nkb, step, (zero, zero, jnp.zeros((tq, HEAD), F32)))
        dq_ref[...] = dq

    blk = pl.BlockSpec((tq, HEAD), lambda h, i: (i, h))
    full = pl.BlockSpec((T, HEAD), lambda h, i: (0, h))
    shp = jax.ShapeDtypeStruct((T, H * HEAD), F32)
    return pl.pallas_call(
        body, name="sb_bwd", grid=(H, T // tq),
        in_specs=[blk, pl.BlockSpec((T, HEAD), lambda h, i: (0, H + h)), pl.BlockSpec((T, HEAD), lambda h, i: (0, 2 * H + h)),
                  blk, blk],
        out_specs=[blk, full, full], out_shape=[shp, shp, shp],
        compiler_params=_cp("parallel", "arbitrary"))(proj, proj, proj, do, L)


def _hg_tile(q, fl, iv, g, st, l0, l1, gain):
    R = 2 * HG_CHUNK
    row = _iota((R, R), 0)
    col = _iota((R, R), 1)
    first = row < HG_CHUNK
    same = first == (col < HG_CHUNK)
    tri = (row >= col) & same
    lb = _sigmoid(l0 - l1)
    f = lb + (1.0 - lb) * _sigmoid(fl)
    logf = jnp.log(f)
    k = 1.0 - f
    qf = q * _sigmoid(q)
    G = _sum_left(tri.astype(BF16), logf)
    gl_a = jnp.sum(jnp.where(first, logf, 0.0), axis=0, keepdims=True)
    gl_b = jnp.sum(jnp.where(first, 0.0, logf), axis=0, keepdims=True)
    q_dec = qf * jnp.exp(G)
    k_inv = k * jnp.exp(-G)
    k_end = k * jnp.exp(jnp.where(first, gl_a, gl_b) - G)
    scores = jnp.where(tri, mm_nt(q_dec, k_inv), 0.0)
    o = mm_nn(scores, iv)
    o_a = mm_nt(q_dec, st)
    st_mid = st * jnp.exp(gl_a) + mm_tn(jnp.where(first, iv, 0.0), k_end)
    o_b = mm_nt(q_dec, st_mid)
    st_new = st_mid * jnp.exp(gl_b) + mm_tn(jnp.where(first, 0.0, iv), k_end)
    o = o + jnp.where(first, o_a, o_b)
    on = o * lax.rsqrt(jnp.mean(o * o, axis=-1, keepdims=True) + NORM_EPS) * gain
    return on * (g * _sigmoid(g)), st_new


def _hg_specs(H, c0, rev, nt):
    def at(base):
        if rev:
            return pl.BlockSpec((HEAD, HEAD), lambda h, i: (nt - 1 - i, base + h))
        return pl.BlockSpec((HEAD, HEAD), lambda h, i: (i, base + h))
    return [at(c0), at(c0 + H), at(c0 + 2 * H), at(c0 + 3 * H)]


def _hg_fwd(proj, l0, l1, gain, H, c0):
    T = proj.shape[0]
    nt = T // HEAD

    def body(q_ref, f_ref, i_ref, g_ref, l0_ref, l1_ref, gain_ref, o_ref, st_out_ref, st_ref):
        @pl.when(pl.program_id(1) == 0)
        def _():
            st_ref[...] = jnp.zeros_like(st_ref)

        st = st_ref[...]
        st_out_ref[...] = st
        out, st_new = _hg_tile(q_ref[...], f_ref[...], i_ref[...], g_ref[...], st, l0_ref[...], l1_ref[...], gain_ref[...])
        o_ref[...] = out.astype(o_ref.dtype)
        st_ref[...] = st_new

    vec = pl.BlockSpec((1, HEAD), lambda h, i: (0, h))
    return pl.pallas_call(
        body, name="hg_fwd", grid=(H, nt), in_specs=_hg_specs(H, c0, False, nt) + [vec, vec, vec],
        out_specs=[pl.BlockSpec((HEAD, HEAD), lambda h, i: (i, h)),
                   pl.BlockSpec((None, None, HEAD, HEAD), lambda h, i: (h, i, 0, 0))],
        out_shape=[jax.ShapeDtypeStruct((T, H * HEAD), BF16), jax.ShapeDtypeStruct((H, nt, HEAD, HEAD), F32)],
        scratch_shapes=[pltpu.VMEM((HEAD, HEAD), F32)],
        compiler_params=_cp("parallel", "arbitrary"))(proj, proj, proj, proj, l0, l1, gain)


def _hg_bwd(proj, states, do, l0, l1, gain, H, c0, do_c0):
    T = proj.shape[0]
    nt = T // HEAD

    def body(q_ref, f_ref, i_ref, g_ref, st_in_ref, do_ref, l0_ref, l1_ref, gain_ref,
             dq_ref, df_ref, di_ref, dg_ref, dl0_ref, dl1_ref, dgain_ref, dst_ref):
        @pl.when(pl.program_id(1) == 0)
        def _():
            dst_ref[...] = jnp.zeros_like(dst_ref)
            dl0_ref[...] = jnp.zeros_like(dl0_ref)
            dl1_ref[...] = jnp.zeros_like(dl1_ref)
            dgain_ref[...] = jnp.zeros_like(dgain_ref)

        _, vjp = jax.vjp(_hg_tile, q_ref[...], f_ref[...], i_ref[...], g_ref[...], st_in_ref[...],
                         l0_ref[...], l1_ref[...], gain_ref[...])
        dq, df, di, dg, dst, dl0, dl1, dgain = vjp((do_ref[...], dst_ref[...]))
        dq_ref[...] = dq
        df_ref[...] = df
        di_ref[...] = di
        dg_ref[...] = dg
        dst_ref[...] = dst
        dl0_ref[...] += dl0
        dl1_ref[...] += dl1
        dgain_ref[...] += dgain

    vec = pl.BlockSpec((1, HEAD), lambda h, i: (0, h))
    rblk = pl.BlockSpec((HEAD, HEAD), lambda h, i: (nt - 1 - i, h))
    shp = jax.ShapeDtypeStruct((T, H * HEAD), F32)
    vshp = jax.ShapeDtypeStruct((1, H * HEAD), F32)
    return pl.pallas_call(
        body, name="hg_bwd", grid=(H, nt),
        in_specs=_hg_specs(H, c0, True, nt) + [
            pl.BlockSpec((None, None, HEAD, HEAD), lambda h, i: (h, nt - 1 - i, 0, 0)),
            pl.BlockSpec((HEAD, HEAD), lambda h, i: (nt - 1 - i, do_c0 + h)), vec, vec, vec],
        out_specs=[rblk, rblk, rblk, rblk, vec, vec, vec],
        out_shape=[shp, shp, shp, shp, vshp, vshp, vshp],
        scratch_shapes=[pltpu.VMEM((HEAD, HEAD), F32)],
        compiler_params=_cp("parallel", "arbitrary"))(proj, proj, proj, proj, states, do, l0, l1, gain)


def _sg_chunk(u_parts, v_parts, gains, biases, wpos, bpos):
    W = sum(p.shape[1] for p in v_parts)
    C = v_parts[0].shape[0]
    v = [_gelu(p) for p in v_parts]
    mu = sum(jnp.sum(p, axis=-1, keepdims=True) for p in v) * (1.0 / W)
    xc = [p - mu for p in v]
    r = lax.rsqrt(sum(jnp.sum(p * p, axis=-1, keepdims=True) for p in xc) * (1.0 / W) + NORM_EPS)
    causal = _iota((C, C), 0) >= _iota((C, C), 1)
    out = []
    for up, p, gn, bs, w, b in zip(u_parts, xc, gains, biases, wpos, bpos):
        vn = p * r * gn + bs
        mixed = mm_nn(jnp.where(causal, w, 0.0), vn) + b
        out.append(_gelu(up) * mixed)
    return out


def _sg_fwd(zpre, vgain, vbias, wpos, bpos):
    T, W2 = zpre.shape
    W = W2 // 2
    G = wpos.shape[0]
    cg = W // G
    C = SG_CHUNK

    def body(z_ref, gn_ref, bs_ref, w_ref, b_ref, s_ref):
        sl = [slice(g * cg, (g + 1) * cg) for g in range(G)]
        out = _sg_chunk([z_ref[:, s] for s in sl], [z_ref[:, W + s.start:W + s.stop] for s in sl],
                        [gn_ref[:, s] for s in sl], [bs_ref[:, s] for s in sl],
                        [w_ref[g] for g in range(G)], [b_ref[g] for g in range(G)])
        for s, o in zip(sl, out):
            s_ref[:, s] = o.astype(s_ref.dtype)

    return pl.pallas_call(
        body, name="sg_fwd", grid=(T // C,),
        in_specs=[pl.BlockSpec((C, W2), lambda i: (i, 0)), _vec_spec(W), _vec_spec(W),
                  pl.BlockSpec((G, C, C), lambda i: (0, 0, 0)), pl.BlockSpec((G, C, 1), lambda i: (0, 0, 0))],
        out_specs=pl.BlockSpec((C, W), lambda i: (i, 0)),
        out_shape=jax.ShapeDtypeStruct((T, W), BF16), compiler_params=_cp("parallel"))(zpre, vgain, vbias, wpos, bpos)


def _sg_bwd(zpre, ds, vgain, vbias, wpos, bpos):
    T, W2 = zpre.shape
    W = W2 // 2
    G = wpos.shape[0]
    cg = W // G
    C = SG_CHUNK

    def body(z_ref, ds_ref, gn_ref, bs_ref, w_ref, b_ref, dz_ref, dgn_ref, dbs_ref, dw_ref, db_ref):
        @pl.when(pl.program_id(0) == 0)
        def _():
            dgn_ref[...] = jnp.zeros_like(dgn_ref)
            dbs_ref[...] = jnp.zeros_like(dbs_ref)
            dw_ref[...] = jnp.zeros_like(dw_ref)
            db_ref[...] = jnp.zeros_like(db_ref)

        sl = [slice(g * cg, (g + 1) * cg) for g in range(G)]
        _, vjp = jax.vjp(_sg_chunk, [z_ref[:, s] for s in sl], [z_ref[:, W + s.start:W + s.stop] for s in sl],
                         [gn_ref[:, s] for s in sl], [bs_ref[:, s] for s in sl],
                         [w_ref[g] for g in range(G)], [b_ref[g] for g in range(G)])
        du, dv, dgn, dbs, dw, db = vjp([ds_ref[:, s] for s in sl])
        for g, s in enumerate(sl):
            dz_ref[:, s] = du[g].astype(dz_ref.dtype)
            dz_ref[:, W + s.start:W + s.stop] = dv[g].astype(dz_ref.dtype)
            dgn_ref[:, s] += dgn[g]
            dbs_ref[:, s] += dbs[g]
            dw_ref[g] += dw[g]
            db_ref[g] += db[g]

    wspec = pl.BlockSpec((G, C, C), lambda i: (0, 0, 0))
    bspec = pl.BlockSpec((G, C, 1), lambda i: (0, 0, 0))
    return pl.pallas_call(
        body, name="sg_bwd", grid=(T // C,),
        in_specs=[pl.BlockSpec((C, W2), lambda i: (i, 0)), pl.BlockSpec((C, W), lambda i: (i, 0)),
                  _vec_spec(W), _vec_spec(W), wspec, bspec],
        out_specs=[pl.BlockSpec((C, W2), lambda i: (i, 0)), _vec_spec(W), _vec_spec(W), wspec, bspec],
        out_shape=[jax.ShapeDtypeStruct((T, W2), BF16), jax.ShapeDtypeStruct((1, W), F32),
                   jax.ShapeDtypeStruct((1, W), F32), jax.ShapeDtypeStruct((G, C, C), F32),
                   jax.ShapeDtypeStruct((G, C, 1), F32)],
        compiler_params=_cp("arbitrary"))(zpre, ds, vgain, vbias, wpos, bpos)


def _conv_tiles(T, F):
    return _pick(T, (512, 256, 128, 64, 32, 16, 8)), _pick(F, (512, 256, 128))


def _shift_down(cur, prev8, n, first_tile):
    bt = cur.shape[0]
    r = pltpu.roll(cur, n, 0)
    p = pltpu.roll(prev8, n, 0)
    p = jnp.where(first_tile, 0.0, p)
    head = jnp.concatenate([p, r[8:]], axis=0) if bt > 8 else p
    return jnp.where(_iota(cur.shape, 0) < n, head, r)


def _shift_up(cur, next8, n, last_tile):
    bt = cur.shape[0]
    r = pltpu.roll(cur, bt - n, 0)
    p = pltpu.roll(next8, 8 - n, 0)
    p = jnp.where(last_tile, 0.0, p)
    tail = jnp.concatenate([r[:bt - 8], p], axis=0) if bt > 8 else p
    return jnp.where(_iota(cur.shape, 0) >= bt - n, tail, r)


def _conv_apply(cur, prev8, w_ref, b, first_tile):
    return (b + w_ref[0:1, :] * _shift_down(cur, prev8, 2, first_tile)
            + w_ref[1:2, :] * _shift_down(cur, prev8, 1, first_tile) + w_ref[2:3, :] * cur)


def _conv_fwd(name, a, w, b):
    T, F2 = a.shape
    F = F2 // 2
    bt, cw = _conv_tiles(T, F)
    nf = F // cw
    r8 = bt // 8

    def body(g_ref, gp_ref, v_ref, vp_ref, wg_ref, wv_ref, bg_ref, bv_ref, u_ref):
        first = pl.program_id(0) == 0
        gate = _conv_apply(g_ref[...], gp_ref[...], wg_ref, bg_ref[...], first)
        val = _conv_apply(v_ref[...], vp_ref[...], wv_ref, bv_ref[...], first)
        u_ref[...] = (gate * _sigmoid(gate) * val).astype(u_ref.dtype)

    def cur(off):
        return pl.BlockSpec((bt, cw), lambda i, j: (i, j + off))

    def prev(off):
        return pl.BlockSpec((8, cw), lambda i, j: (jnp.maximum(i * r8 - 1, 0), j + off))

    def vec(rows, off):
        return pl.BlockSpec((rows, cw), lambda i, j: (0, j + off))

    return pl.pallas_call(
        body, name=name, grid=(T // bt, nf),
        in_specs=[cur(0), prev(0), cur(nf), prev(nf), vec(3, 0), vec(3, nf), vec(1, 0), vec(1, nf)],
        out_specs=pl.BlockSpec((bt, cw), lambda i, j: (i, j)),
        out_shape=jax.ShapeDtypeStruct((T, F), BF16),
        compiler_params=_cp("parallel", "parallel"))(a, a, a, a, w, w, b, b)


def _conv_bwd_act(name, a, du, w, b):
    T, F2 = a.shape
    F = F2 // 2
    bt, cw = _conv_tiles(T, F)
    nf = F // cw
    r8 = bt // 8

    def body(s_ref, sp_ref, o_ref, op_ref, du_ref, ws_ref, wo_ref, bs_ref, bo_ref, da_ref, dw_ref, db_ref):
        i = pl.program_id(1)
        first = i == 0
        is_gate = pl.program_id(0) < nf
        cur = s_ref[...]
        prev8 = sp_ref[...]
        mine = _conv_apply(cur, prev8, ws_ref, bs_ref[...], first)
        other = _conv_apply(o_ref[...], op_ref[...], wo_ref, bo_ref[...], first)
        gate = jnp.where(is_gate, mine, other)
        val = jnp.where(is_gate, other, mine)
        sg = _sigmoid(gate)
        du_ = du_ref[...]
        d_gate = du_ * val * (sg * (1.0 + gate * (1.0 - sg)))
        d_val = du_ * gate * sg
        da = jnp.where(is_gate, d_gate, d_val)
        da_ref[...] = da

        @pl.when(first)
        def _():
            dw_ref[...] = jnp.zeros_like(dw_ref)
            db_ref[...] = jnp.zeros_like(db_ref)

        rows = [jnp.sum(da * _shift_down(cur, prev8, 2, first), axis=0, keepdims=True),
                jnp.sum(da * _shift_down(cur, prev8, 1, first), axis=0, keepdims=True),
                jnp.sum(da * cur, axis=0, keepdims=True)]
        for t in range(CONV_WIDTH):
            dw_ref[t:t + 1, :] += rows[t]
        db_ref[...] += jnp.sum(da, axis=0, keepdims=True)

    n2 = 2 * nf

    def cur(off):
        return pl.BlockSpec((bt, cw), lambda j, i: (i, (j + off) % n2))

    def prev(off):
        return pl.BlockSpec((8, cw), lambda j, i: (jnp.maximum(i * r8 - 1, 0), (j + off) % n2))

    def vec(rows, off):
        return pl.BlockSpec((rows, cw), lambda j, i: (0, (j + off) % n2))

    return pl.pallas_call(
        body, name=name, grid=(n2, T // bt),
        in_specs=[cur(0), prev(0), cur(nf), prev(nf), pl.BlockSpec((bt, cw), lambda j, i: (i, j % nf)),
                  vec(3, 0), vec(3, nf), vec(1, 0), vec(1, nf)],
        out_specs=[pl.BlockSpec((bt, cw), lambda j, i: (i, j)), vec(3, 0), vec(1, 0)],
        out_shape=[jax.ShapeDtypeStruct((T, F2), F32), jax.ShapeDtypeStruct((3, F2), F32),
                   jax.ShapeDtypeStruct((1, F2), F32)],
        compiler_params=_cp("parallel", "arbitrary"))(a, a, a, a, du, w, w, b, b)


def _conv_bwd_in(name, da_out, w):
    T, F2 = da_out.shape
    bt, cw = _conv_tiles(T, F2 // 2)
    r8 = bt // 8
    last_blk = T // 8 - 1

    def body(d_ref, dn_ref, w_ref, o_ref):
        last = pl.program_id(0) == pl.num_programs(0) - 1
        cur = d_ref[...]
        nxt = dn_ref[...]
        o_ref[...] = (w_ref[2:3, :] * cur + w_ref[1:2, :] * _shift_up(cur, nxt, 1, last)
                      + w_ref[0:1, :] * _shift_up(cur, nxt, 2, last)).astype(o_ref.dtype)

    return pl.pallas_call(
        body, name=name, grid=(T // bt, F2 // cw),
        in_specs=[pl.BlockSpec((bt, cw), lambda i, j: (i, j)),
                  pl.BlockSpec((8, cw), lambda i, j: (jnp.minimum((i + 1) * r8, last_blk), j)),
                  pl.BlockSpec((3, cw), lambda i, j: (0, j))],
        out_specs=pl.BlockSpec((bt, cw), lambda i, j: (i, j)),
        out_shape=jax.ShapeDtypeStruct((T, F2), BF16),
        compiler_params=_cp("parallel", "parallel"))(da_out, da_out, w)


def _ada_fwd(c_all, ada_w, ada_b):
    R, D = c_all.shape
    L, _, Ns = ada_w.shape
    tn = _pick(Ns, (512, 256, 128))

    def body(c_ref, w_ref, b_ref, o_ref):
        cv = c_ref[...]
        cond = cv * _sigmoid(cv)
        o_ref[...] = _dg(cond, w_ref[...], 1, 0) + b_ref[...]

    return pl.pallas_call(
        body, name="ada_fwd", grid=(L, Ns // tn),
        in_specs=[pl.BlockSpec((R, D), lambda l, j: (0, 0)), pl.BlockSpec((None, D, tn), lambda l, j: (l, 0, j)),
                  pl.BlockSpec((None, 1, tn), lambda l, j: (l, 0, j))],
        out_specs=pl.BlockSpec((None, R, tn), lambda l, j: (l, 0, j)),
        out_shape=jax.ShapeDtypeStruct((L, R, Ns), F32), compiler_params=_cp("parallel", "parallel"))(c_all, ada_w, ada_b)


def _adam_math(w, g, m, v):
    m2 = ADAM_B1 * m + (1.0 - ADAM_B1) * g
    v2 = ADAM_B2 * v + (1.0 - ADAM_B2) * (g * g)
    m_hat = m2 / (1.0 - ADAM_B1 ** ADAM_STEP)
    v_hat = v2 / (1.0 - ADAM_B2 ** ADAM_STEP)
    delta = -ADAM_LR * (m_hat / (jnp.sqrt(v_hat) + ADAM_EPS) + ADAM_WD * w)
    return delta, m2, v2


def _ada_grad_adam(c_all_t, dmod, w, m, v):
    D, R = c_all_t.shape
    L, _, Ns = dmod.shape
    tr = _rows_within(D, Ns * 4, 1 << 20)

    def body(c_ref, d_ref, w_ref, m_ref, v_ref, g_ref, dl_ref, m2_ref, v2_ref):
        cv = c_ref[...]
        g = _dg(cv * _sigmoid(cv), d_ref[...], 1, 0)
        g_ref[...] = g
        dl_ref[...], m2_ref[...], v2_ref[...] = _adam_math(w_ref[...], g, m_ref[...], v_ref[...])

    big = pl.BlockSpec((None, tr, Ns), lambda l, i: (l, i, 0))
    shp = jax.ShapeDtypeStruct((L, D, Ns), F32)
    return pl.pallas_call(
        body, name="ada_grad_adam", grid=(L, D // tr),
        in_specs=[pl.BlockSpec((tr, R), lambda l, i: (i, 0)), pl.BlockSpec((None, R, Ns), lambda l, i: (l, 0, 0)), big, big, big],
        out_specs=[big] * 4, out_shape=[shp] * 4, compiler_params=_cp("parallel", "parallel"))(c_all_t, dmod, w, m, v)


def _adam(name, w, g, m, v):
    R, C = w.shape
    tr = _rows_within(R, C * 4, 1 << 21)

    def body(w_ref, g_ref, m_ref, v_ref, dl_ref, m2_ref, v2_ref):
        dl_ref[...], m2_ref[...], v2_ref[...] = _adam_math(w_ref[...], g_ref[...], m_ref[...], v_ref[...])

    blk = pl.BlockSpec((tr, C), lambda i: (i, 0))
    shp = jax.ShapeDtypeStruct((R, C), F32)
    return pl.pallas_call(body, name=name, grid=(R // tr,), in_specs=[blk] * 4, out_specs=[blk] * 3,
                          out_shape=[shp] * 3, compiler_params=_cp("parallel"))(w, g, m, v)


def _cast_bf16(name, w):
    R, C = w.shape
    tr = _rows_within(R, C * 4, 1 << 22)

    def body(w_ref, o_ref):
        o_ref[...] = w_ref[...].astype(BF16)

    blk = pl.BlockSpec((tr, C), lambda i: (i, 0))
    return pl.pallas_call(body, name=name, grid=(R // tr,), in_specs=[blk], out_specs=blk,
                          out_shape=jax.ShapeDtypeStruct((R, C), BF16), compiler_params=_cp("parallel"))(w)


def _add_pairs(name, a, b):
    _, R, C = a.shape
    tr = _rows_within(R, C * 2, 1 << 21)

    def body(a_ref, b_ref, o_ref):
        o_ref[...] = (a_ref[...].astype(F32) + b_ref[...].astype(F32)).astype(o_ref.dtype)

    blk = pl.BlockSpec((None, tr, C), lambda j, i: (j, i, 0))
    return pl.pallas_call(body, name=name, grid=(4, R // tr), in_specs=[blk, blk], out_specs=blk,
                          out_shape=jax.ShapeDtypeStruct(a.shape, BF16), compiler_params=_cp("parallel", "parallel"))(a, b)


def _sum_leading(name, a, out_dtype=F32):
    n, R, C = a.shape
    tr = _rows_within(R, n * C * a.dtype.itemsize, 1 << 23)

    def body(a_ref, o_ref):
        acc = a_ref[0].astype(F32)
        for j in range(1, n):
            acc = acc + a_ref[j].astype(F32)
        o_ref[...] = acc.astype(o_ref.dtype)

    return pl.pallas_call(body, name=name, grid=(R // tr,), in_specs=[pl.BlockSpec((n, tr, C), lambda i: (0, i, 0))],
                          out_specs=pl.BlockSpec((tr, C), lambda i: (i, 0)),
                          out_shape=jax.ShapeDtypeStruct((R, C), out_dtype), compiler_params=_cp("parallel"))(a)


def _place():
    return lax.axis_index("x"), lax.axis_index("y"), lax.axis_index("c")


def _all_gather(name, blocks, halves=False):
    n = len(blocks)
    shapes = [b.shape[1:] if halves else b.shape for b in blocks]

    def body(*refs):
        ins, outs = refs[:n], refs[n:2 * n]
        send_sems, recv_sems, local_sems = refs[2 * n:]
        x, y, c = _place()
        me, sibling = (x, y, c), (x, y, 1 - c)
        chips = [(1 - x, y), (x, 1 - y), (1 - x, 1 - y)]

        def rows(a, px, py, pc):
            return outs[a].at[4 * px + 2 * py + pc]

        def copy(a, k, block, to, src=None):
            return pltpu.make_async_remote_copy(
                src_ref=rows(a, *block) if src is None else src, dst_ref=rows(a, *block),
                send_sem=send_sems.at[7 * a + k], recv_sem=recv_sems.at[7 * a + k],
                device_id=to, device_id_type=MESH)

        started = []
        mine = []
        for a in range(n):
            src = ins[a].at[c] if halves else ins[a]
            mine.append(pltpu.make_async_copy(src, rows(a, *me), local_sems.at[a]))
            mine[-1].start()
            first = [copy(a, 0, me, sibling, src=src)]
            first += [copy(a, 1 + j, me, (*chip, c), src=src) for j, chip in enumerate(chips)]
            for cp in first:
                cp.start()
            started += first
        for j, chip in enumerate(chips):
            for a in range(n):
                copy(a, 1 + j, (*chip, c), me).wait_recv()
                passed = copy(a, 4 + j, (*chip, c), sibling)
                passed.start()
                started.append(passed)
        for a in range(n):
            copy(a, 0, sibling, me).wait_recv()
            for j, chip in enumerate(chips):
                copy(a, 4 + j, (*chip, 1 - c), me).wait_recv()
        for cp in started:
            cp.wait_send()
        for cp in mine:
            cp.wait()

    return pl.pallas_call(
        body, name=name, in_specs=[ANY] * n, out_specs=[ANY] * n,
        out_shape=[jax.ShapeDtypeStruct((N_DEV,) + tuple(s), b.dtype) for s, b in zip(shapes, blocks)],
        scratch_shapes=[pltpu.SemaphoreType.DMA((7 * n,)), pltpu.SemaphoreType.DMA((7 * n,)),
                        pltpu.SemaphoreType.DMA((n,))],
    )(*blocks)


def _to_sibling(name, arrays, pick):
    n = len(arrays)
    per = 4 if pick == "other_half" else 1

    def body(*refs):
        ins, outs = refs[:n], refs[n:2 * n]
        send_sems, recv_sems, local_sems = refs[2 * n:]
        x, y, c = _place()
        sibling = (x, y, 1 - c)
        started = []
        local = []
        for a in range(n):
            for j in range(per):
                if pick == "other_half":
                    src, dst = ins[a].at[2 * j + (1 - c)], outs[a].at[j]
                else:
                    src, dst = ins[a], outs[a].at[c]
                    local.append(pltpu.make_async_copy(src, dst, local_sems.at[a]))
                    local[-1].start()
                cp = pltpu.make_async_remote_copy(src_ref=src, dst_ref=dst, send_sem=send_sems.at[per * a + j],
                                                  recv_sem=recv_sems.at[per * a + j], device_id=sibling, device_id_type=MESH)
                cp.start()
                started.append(cp)
        for cp in started:
            cp.wait()
        for cp in local:
            cp.wait()

    if pick == "other_half":
        out_shape = [jax.ShapeDtypeStruct((4,) + a.shape[1:], a.dtype) for a in arrays]
    else:
        out_shape = [jax.ShapeDtypeStruct((2,) + a.shape, a.dtype) for a in arrays]
    return pl.pallas_call(
        body, name=name, in_specs=[ANY] * n, out_specs=[ANY] * n, out_shape=out_shape,
        scratch_shapes=[pltpu.SemaphoreType.DMA((per * n,)), pltpu.SemaphoreType.DMA((per * n,)),
                        pltpu.SemaphoreType.DMA((n,))],
    )(*arrays)


def _between_chips(name, arrays):
    n = len(arrays)

    def body(*refs):
        ins, outs = refs[:n], refs[n:2 * n]
        send_sems, recv_sems, local_sems = refs[2 * n:]
        x, y, c = _place()
        mine = 2 * x + y
        flips = [(1 - x, y), (x, 1 - y), (1 - x, 1 - y)]
        started = []
        local = []
        for a in range(n):
            local.append(pltpu.make_async_copy(ins[a].at[mine], outs[a].at[mine], local_sems.at[a]))
            local[-1].start()
            for k, (px, py) in enumerate(flips):
                cp = pltpu.make_async_remote_copy(
                    src_ref=ins[a].at[2 * px + py], dst_ref=outs[a].at[mine],
                    send_sem=send_sems.at[3 * a + k], recv_sem=recv_sems.at[3 * a + k],
                    device_id=(px, py, c), device_id_type=MESH)
                cp.start()
                started.append(cp)
        for cp in started:
            cp.wait()
        for cp in local:
            cp.wait()

    return pl.pallas_call(
        body, name=name, in_specs=[ANY] * n, out_specs=[ANY] * n,
        out_shape=[jax.ShapeDtypeStruct(a.shape, a.dtype) for a in arrays],
        scratch_shapes=[pltpu.SemaphoreType.DMA((3 * n,)), pltpu.SemaphoreType.DMA((3 * n,)),
                        pltpu.SemaphoreType.DMA((n,))],
    )(*arrays)


def kernel(x, c, ada_w, ada_b, mix_norm, ffn_norm, par_w_in, par_w_out, hg_lb_logits, hg_out_norm, sg_w_in, sg_v_gain, sg_v_bias, sg_w_pos, sg_b_pos, sg_w_out, ffn_up, ffn_conv_w, ffn_conv_b, ffn_down, final_norm, loss_target, m_ada_w, m_ada_b, m_mix_norm, m_ffn_norm, m_par_w_in, m_par_w_out, m_hg_lb_logits, m_hg_out_norm, m_sg_w_in, m_sg_v_gain, m_sg_v_bias, m_sg_w_pos, m_sg_b_pos, m_sg_w_out, m_ffn_up, m_ffn_conv_w, m_ffn_conv_b, m_ffn_down, m_final_norm, v_ada_w, v_ada_b, v_mix_norm, v_ffn_norm, v_par_w_in, v_par_w_out, v_hg_lb_logits, v_hg_out_norm, v_sg_w_in, v_sg_v_gain, v_sg_v_bias, v_sg_w_pos, v_sg_b_pos, v_sg_w_out, v_ffn_up, v_ffn_conv_w, v_ffn_conv_b, v_ffn_down, v_final_norm):
    names = ["ada_w", "ada_b", "mix_norm", "ffn_norm", "par_w_in", "par_w_out", "hg_lb_logits", "hg_out_norm", "sg_w_in",
             "sg_v_gain", "sg_v_bias", "sg_w_pos", "sg_b_pos", "sg_w_out", "ffn_up", "ffn_conv_w", "ffn_conv_b",
             "ffn_down", "final_norm"]
    W = dict(zip(names, [ada_w, ada_b, mix_norm, ffn_norm, par_w_in, par_w_out, hg_lb_logits, hg_out_norm, sg_w_in,
                         sg_v_gain, sg_v_bias, sg_w_pos, sg_b_pos, sg_w_out, ffn_up, ffn_conv_w, ffn_conv_b, ffn_down,
                         final_norm]))
    M = dict(zip(names, [m_ada_w, m_ada_b, m_mix_norm, m_ffn_norm, m_par_w_in, m_par_w_out, m_hg_lb_logits, m_hg_out_norm,
                         m_sg_w_in, m_sg_v_gain, m_sg_v_bias, m_sg_w_pos, m_sg_b_pos, m_sg_w_out, m_ffn_up, m_ffn_conv_w,
                         m_ffn_conv_b, m_ffn_down, m_final_norm]))
    V = dict(zip(names, [v_ada_w, v_ada_b, v_mix_norm, v_ffn_norm, v_par_w_in, v_par_w_out, v_hg_lb_logits, v_hg_out_norm,
                         v_sg_w_in, v_sg_v_gain, v_sg_v_bias, v_sg_w_pos, v_sg_b_pos, v_sg_w_out, v_ffn_up, v_ffn_conv_w,
                         v_ffn_conv_b, v_ffn_down, v_final_norm]))

    x = x[0]
    target = loss_target[0]
    T, D = x.shape
    ix, iy, ic = _place()
    chip = 2 * ix + iy
    dev = 2 * chip + ic
    H = hg_out_norm.shape[1]
    SBW = H * HEAD
    NA = ada_w.shape[2]
    F2s = ffn_up.shape[2]
    F2 = N_CHIP * F2s
    SGW = sg_w_out.shape[1] * N_CHIP
    G = sg_w_pos.shape[1]

    n_cw = ffn_conv_w.size
    n_sv = sg_v_gain.size
    c_all, small_all = _all_gather("gather_small", [c, _pack_rows([ffn_conv_w, sg_v_gain, sg_v_bias])])
    c_all = c_all.reshape(N_DEV, D)
    small_all = small_all.reshape(N_CHIP, 2, -1)[:, 0]
    conv_w_full = small_all[:, :n_cw].reshape(N_CHIP, 2, CONV_WIDTH, F2s).transpose(1, 2, 0, 3).reshape(2, CONV_WIDTH, F2)
    sg_gain_full = small_all[:, n_cw:n_cw + n_sv].reshape(1, SGW)
    sg_bias_full = small_all[:, n_cw + n_sv:n_cw + 2 * n_sv].reshape(1, SGW)

    c_pad = jnp.pad(c_all, ((0, 16 - N_DEV), (0, 0)))
    ada_b_sh = lax.dynamic_slice(ada_b, (0, chip * NA), (2, NA)).reshape(2, 1, NA)
    mod_sh = _ada_fwd(c_pad, ada_w, ada_b_sh)
    mod_all, = _all_gather("gather_mod", [mod_sh[:, :N_DEV]])
    mod_all = mod_all.reshape(N_CHIP, 2, 2, N_DEV, NA)[:, 0]
    mod = lax.dynamic_index_in_dim(mod_all, dev, axis=2, keepdims=False)
    mod = mod.transpose(1, 0, 2).reshape(2, 6, D)
    mods = [[mod[l, k].reshape(1, D) for k in range(6)] for l in range(2)]

    def halves_of(w2d, kind):
        K, N = w2d.shape
        b = _cast_bf16("cast_w", w2d)
        return b.reshape(2, K // 2, N)

    shards = [par_w_in[0], par_w_out[0], sg_w_in[0], sg_w_out[0], ffn_up[0], ffn_up[1], ffn_down[0], ffn_down[1]]
    kinds = ["col", "row", "col", "row", "col", "col", "row", "row"]
    gathered = _all_gather("gather_weights", [halves_of(w, k) for w, k in zip(shards, kinds)], halves=True)
    full = []
    for g8, w, k in zip(gathered, shards, kinds):
        K, N = w.shape
        full.append(g8.reshape(N_CHIP, K, N) if k == "col" else g8.reshape(N_CHIP * K, N))
    w_in, w_out, wsg_in, wsg_out, wup0, wup1, wdn0, wdn1 = full
    wup = [wup0, wup1]
    wdn = [wdn0, wdn1]

    vec = lambda a: a.reshape(1, -1)
    l0 = vec(hg_lb_logits[0])
    l1 = vec(hg_lb_logits[1])
    hg_gain = vec(hg_out_norm[0])
    wpos = sg_w_pos[0]
    bpos = sg_b_pos[0].reshape(G, SG_CHUNK, 1)
    conv_b = [vec(ffn_conv_b[l]) for l in range(2)]

    sh1, sc1, g1, sh2, sc2, g2 = mods[0]
    h0 = _normmod_fwd("norm_mix0", x, vec(mix_norm[0]), sc1, sh1)
    proj = _mm_nn("mm_par_in", h0, w_in)
    o_sb, sb_tot = _sb_fwd(proj, H)
    o_hg, hg_states = _hg_fwd(proj, l0, l1, hg_gain, H, 3 * H)
    o_cat = jnp.concatenate([o_sb, o_hg], axis=1)
    y0 = _mm_nn("mm_par_out", o_cat, w_out)
    x1, h0f = _res_normmod_fwd("res_norm_ffn0", x, y0, g1, vec(ffn_norm[0]), sc2, sh2)
    a0 = _mm_nn("mm_up0", h0f, wup[0])
    u0 = _conv_fwd("conv_fwd0", a0, conv_w_full[0], conv_b[0])
    f0 = _mm_nn("mm_down0", u0, wdn[0])
    sh1b, sc1b, g1b, sh2b, sc2b, g2b = mods[1]
    x2, h1 = _res_normmod_fwd("res_norm_mix1", x1, f0, g2, vec(mix_norm[1]), sc1b, sh1b)
    zpre = _mm_nn("mm_sg_in", h1, wsg_in)
    s1 = _sg_fwd(zpre, sg_gain_full, sg_bias_full, wpos, bpos)
    y1 = _mm_nn("mm_sg_out", s1, wsg_out)
    x3, h1f = _res_normmod_fwd("res_norm_ffn1", x2, y1, g1b, vec(ffn_norm[1]), sc2b, sh2b)
    a1 = _mm_nn("mm_up1", h1f, wup[1])
    u1 = _conv_fwd("conv_fwd1", a1, conv_w_full[1], conv_b[1])
    f1 = _mm_nn("mm_down1", u1, wdn[1])
    loss_sum, dx, df1, dg2b, d_final = _final_fwd_bwd(x3, f1, g2b, vec(final_norm), target)
    loss = lax.psum(loss_sum[0, 0], ("x", "y", "c"))

    def ffn_bwd(l, dfl, u, a, hf):
        g_dn = _mm_tn("mm_g_down", u, dfl)
        du = _mm_nt("mm_d_u", dfl, wdn[l])
        da_out, dcw, dcb = _conv_bwd_act("conv_bwd_act", a, du, conv_w_full[l], conv_b[l])
        da = _conv_bwd_in("conv_bwd_in", da_out, conv_w_full[l])
        g_up = _mm_tn("mm_g_up", hf, da, chunks=N_CHIP)
        dh = _mm_nt("mm_d_hf", da, wup[l])
        return g_dn, g_up, dcw, dcb, dh

    g_dn1, g_up1, dcw1, dcb1, dh1f = ffn_bwd(1, df1, u1, a1, h1f)
    dx, dgn_f1, dsc2b, dsh2b, dy1, dg1b = _block_bwd("bwd_ffn1", dx, dh1f, x3, vec(ffn_norm[1]), sc2b, sh2b, y1, g1b)
    g_sg_out = _mm_tn("mm_g_sg_out", s1, dy1)
    ds1 = _mm_nt("mm_d_s", dy1, wsg_out)
    dzpre, dsg_gain, dsg_bias, dwpos, dbpos = _sg_bwd(zpre, ds1, sg_gain_full, sg_bias_full, wpos, bpos)
    g_sg_in = _mm_tn("mm_g_sg_in", h1, dzpre, chunks=N_CHIP)
    dh1 = _mm_nt("mm_d_h1", dzpre, wsg_in)
    dx, dgn_m1, dsc1b, dsh1b, df0, dg2 = _block_bwd("bwd_mix1", dx, dh1, x2, vec(mix_norm[1]), sc1b, sh1b, f0, g2)
    g_dn0, g_up0, dcw0, dcb0, dh0f = ffn_bwd(0, df0, u0, a0, h0f)
    dx, dgn_f0, dsc2, dsh2, dy0, dg1 = _block_bwd("bwd_ffn0", dx, dh0f, x1, vec(ffn_norm[0]), sc2, sh2, y0, g1)
    g_out = _mm_tn("mm_g_par_out", o_cat, dy0)
    do = _mm_nt("mm_d_o", dy0, w_out)
    dq, dk, dv = _sb_bwd(proj, do, sb_tot, H)
    dhq, dhf, dhi, dhg, dl0, dl1, dhg_gain = _hg_bwd(proj, hg_states, do, l0, l1, hg_gain, H, 3 * H, H)
    dproj = jnp.concatenate([dq, dk, dv, dhq, dhf, dhi, dhg], axis=1).astype(BF16)
    g_in = _mm_tn("mm_g_par_in", h0, dproj, chunks=N_CHIP)
    dh0 = _mm_nt("mm_d_h0", dproj, w_in)
    grad_x, dgn_m0, dsc1, dsh1 = _block_bwd("bwd_mix0", dx, dh0, x, vec(mix_norm[0]), sc1, sh1)

    dmod = jnp.concatenate([dsh1, dsc1, dg1, dsh2, dsc2, dg2, dsh1b, dsc1b, dg1b, dsh2b, dsc2b, dg2b], axis=1)
    parts = [dmod, dgn_m0, dgn_m1, dgn_f0, dgn_f1, dl0, dl1, dhg_gain, dsg_gain, dsg_bias, dwpos, dbpos,
             dcw0, dcw1, dcb0, dcb1, d_final]
    sizes = [p.size for p in parts]
    packed = _pack_rows(parts)
    packed_all, = _all_gather("gather_small_grads", [packed])
    summed = _sum_leading("sum_small_grads", packed_all).reshape(-1)
    offs = [0]
    for s in sizes:
        offs.append(offs[-1] + s)
    red = [summed[offs[i]:offs[i + 1]] for i in range(len(parts))]
    (r_dmod, r_gm0, r_gm1, r_gf0, r_gf1, r_l0, r_l1, r_hgain, r_sgain, r_sbias, r_wpos, r_bpos,
     r_cw0, r_cw1, r_cb0, r_cb1, r_final) = red
    n_mod = sizes[0]
    dmod_all = packed_all.reshape(N_DEV, -1)[:, :n_mod].reshape(N_DEV, 2, 6 * D)

    G_ = {}
    G_["ada_b"] = r_dmod.reshape(2, 6 * D)
    G_["mix_norm"] = jnp.stack([r_gm0, r_gm1])
    G_["ffn_norm"] = jnp.stack([r_gf0, r_gf1])
    G_["hg_lb_logits"] = jnp.stack([r_l0, r_l1])
    G_["hg_out_norm"] = r_hgain.reshape(hg_out_norm.shape)
    G_["sg_v_gain"] = lax.dynamic_slice(r_sgain, (chip * n_sv,), (n_sv,)).reshape(sg_v_gain.shape)
    G_["sg_v_bias"] = lax.dynamic_slice(r_sbias, (chip * n_sv,), (n_sv,)).reshape(sg_v_bias.shape)
    G_["sg_w_pos"] = r_wpos.reshape(sg_w_pos.shape)
    G_["sg_b_pos"] = r_bpos.reshape(sg_b_pos.shape)
    cw_full = jnp.stack([r_cw0.reshape(CONV_WIDTH, F2), r_cw1.reshape(CONV_WIDTH, F2)])
    G_["ffn_conv_w"] = lax.dynamic_slice(cw_full, (0, 0, chip * F2s), (2, CONV_WIDTH, F2s))
    G_["ffn_conv_b"] = jnp.stack([r_cb0, r_cb1])
    G_["final_norm"] = r_final

    big = [g_in, g_out, g_sg_in, g_sg_out, g_up0, g_up1, g_dn0, g_dn1]
    eights = [g.reshape((N_DEV, -1, g.shape[-1])) for g in big]
    from_sib = _to_sibling("grads_to_sibling", eights, "other_half")
    own = [lax.dynamic_index_in_dim(e.reshape((N_CHIP, 2) + e.shape[1:]), ic, axis=1, keepdims=False) for e in eights]
    pair = [_add_pairs("add_pair", o, r) for o, r in zip(own, from_sib)]
    crossed = _between_chips("grads_between_chips", pair)
    halves = [_sum_leading("sum_chips", cr) for cr in crossed]
    both = _to_sibling("grads_share", halves, "pair")
    g_shards = [b.reshape(w.shape) for b, w in zip(both, shards)]
    G_["par_w_in"] = g_shards[0][None]
    G_["par_w_out"] = g_shards[1][None]
    G_["sg_w_in"] = g_shards[2][None]
    G_["sg_w_out"] = g_shards[3][None]
    G_["ffn_up"] = jnp.stack([g_shards[4], g_shards[5]])
    G_["ffn_down"] = jnp.stack([g_shards[6], g_shards[7]])

    delta, new_m, new_v = {}, {}, {}
    c_t = jnp.pad(c_all, ((0, HEAD - N_DEV), (0, 0))).T
    dmod_sh = lax.dynamic_slice(dmod_all.transpose(1, 0, 2), (0, 0, chip * NA), (2, N_DEV, NA))
    dmod_sh = jnp.pad(dmod_sh, ((0, 0), (0, HEAD - N_DEV), (0, 0)))
    G_["ada_w"], delta["ada_w"], new_m["ada_w"], new_v["ada_w"] = _ada_grad_adam(c_t, dmod_sh, ada_w, m_ada_w, v_ada_w)
    for nme in ["par_w_in", "par_w_out", "sg_w_in", "sg_w_out", "ffn_up", "ffn_down"]:
        w = W[nme]
        shp = w.shape
        r2 = lambda a: a.reshape(-1, shp[-1])
        d_, m_, v_ = _adam("adam_" + nme, r2(w), r2(G_[nme]), r2(M[nme]), r2(V[nme]))
        delta[nme], new_m[nme], new_v[nme] = d_.reshape(shp), m_.reshape(shp), v_.reshape(shp)
    small = [n_ for n_ in names if n_ not in delta]
    pk = lambda dct: _pack_rows([dct[n_] for n_ in small])
    d_, m_, v_ = _adam("adam_small", pk(W), pk(G_), pk(M), pk(V))
    off = 0
    for n_ in small:
        sz = W[n_].size
        for dst, src in ((delta, d_), (new_m, m_), (new_v, v_)):
            dst[n_] = src.reshape(-1)[off:off + sz].reshape(W[n_].shape)
        off += sz

    return (loss, grad_x[None], *[G_[n_] for n_ in names], *[delta[n_] for n_ in names],
            *[new_m[n_] for n_ in names], *[new_v[n_] for n_ in names])
```

```python
import functools
import math

import jax
import jax.numpy as jnp
from jax import lax
from jax.experimental import pallas as pl
from jax.experimental.pallas import tpu as pltpu

F32 = jnp.float32
BF16 = jnp.bfloat16
MESH = pl.DeviceIdType.MESH
ANY = pl.BlockSpec(memory_space=pl.ANY)

NORM_EPS = 1e-6
ADAM_LR = 0.001
ADAM_B1 = 0.9
ADAM_B2 = 0.999
ADAM_EPS = 1e-08
ADAM_WD = 0.01
ADAM_STEP = 10
CONV_WIDTH = 3
HEAD = 128
HG_CHUNK = 64
SG_CHUNK = 128
N_DEV = 8
N_CHIP = 4
V7X_VMEM_LIMIT = 56 * 1024 * 1024


def _cp(*sem):
    return pltpu.CompilerParams(dimension_semantics=sem if sem else None, vmem_limit_bytes=V7X_VMEM_LIMIT)


def _pick(n, prefs):
    for p in prefs:
        if p <= n and n % p == 0:
            return p
    return n


def _iota(shape, axis):
    return lax.broadcasted_iota(jnp.int32, shape, axis)


def _rows_within(R, row_bytes, budget):
    if R * row_bytes <= budget:
        return R
    for t in (1024, 512, 256, 128, 64, 32, 16):
        if R % t == 0 and t * row_bytes <= budget:
            return t
    return _pick(R, (16, 8))


def _pack_rows(arrays):
    flat = jnp.concatenate([a.reshape(-1) for a in arrays])
    pad = (-flat.size) % (8 * HEAD)
    return jnp.pad(flat, (0, pad)).reshape(-1, HEAD)


def _dg(a, b, ca, cb):
    return lax.dot_general(a.astype(BF16), b.astype(BF16), (((ca,), (cb,)), ((), ())), preferred_element_type=F32)


@jax.custom_vjp
def mm_nn(a, b):
    return _dg(a, b, 1, 0)


mm_nn.defvjp(lambda a, b: (_dg(a, b, 1, 0), (a, b)),
             lambda r, g: (_dg(g, r[1], 1, 1), _dg(r[0], g, 0, 0)))


@jax.custom_vjp
def mm_nt(a, b):
    return _dg(a, b, 1, 1)


mm_nt.defvjp(lambda a, b: (_dg(a, b, 1, 1), (a, b)),
             lambda r, g: (_dg(g, r[1], 1, 0), _dg(g, r[0], 0, 0)))


@jax.custom_vjp
def mm_tn(a, b):
    return _dg(a, b, 0, 0)


mm_tn.defvjp(lambda a, b: (_dg(a, b, 0, 0), (a, b)),
             lambda r, g: (_dg(r[1], g, 1, 1), _dg(r[0], g, 1, 0)))


def _split(x):
    hi = x.astype(BF16)
    lo = (x - hi.astype(F32)).astype(BF16)
    return hi, lo


def _sum_right(x, m01):
    hi, lo = _split(x)
    return _dg(hi, m01, 1, 0) + _dg(lo, m01, 1, 0)


def _sum_left_impl(m01, x, ca):
    hi, lo = _split(x)
    return _dg(m01, hi, ca, 0) + _dg(m01, lo, ca, 0)


@jax.custom_vjp
def _sum_left(m01, x):
    return _sum_left_impl(m01, x, 1)


_sum_left.defvjp(lambda m, x: (_sum_left_impl(m, x, 1), m),
                 lambda m, g: (None, _sum_left_impl(m, g, 0)))


def _sigmoid(x):
    return 1.0 / (1.0 + jnp.exp(-x))


def _softplus(z):
    return jnp.maximum(z, 0.0) + jnp.log(1.0 + jnp.exp(-jnp.abs(z)))


_INV_SQRT2 = 1.0 / math.sqrt(2.0)
_INV_SQRT2PI = 1.0 / math.sqrt(2.0 * math.pi)


@jax.custom_vjp
def _gelu(x):
    return 0.5 * x * (1.0 + lax.erf(x * _INV_SQRT2))


_gelu.defvjp(lambda x: (0.5 * x * (1.0 + lax.erf(x * _INV_SQRT2)), x),
             lambda x, g: (g * (0.5 * (1.0 + lax.erf(x * _INV_SQRT2)) + x * jnp.exp(-0.5 * x * x) * _INV_SQRT2PI),))


def _rms(x, gain):
    r = lax.rsqrt(jnp.mean(x * x, axis=-1, keepdims=True) + NORM_EPS)
    return x * r * gain


def _normmod(x, gain, sc, sh):
    return _rms(x, gain) * (1.0 + sc) + sh


def _mm_call(name, a, b, out_shape, out_dtype, dims, grid, a_spec, b_spec, o_spec, acc_shape):
    nk = grid[2]

    def body(a_ref, b_ref, o_ref, *scratch):
        part = lax.dot_general(a_ref[...].astype(BF16), b_ref[...].astype(BF16), dims, preferred_element_type=F32)
        if nk == 1:
            o_ref[...] = part.astype(o_ref.dtype)
            return
        acc_ref, = scratch
        k = pl.program_id(2)

        @pl.when(k == 0)
        def _():
            acc_ref[...] = part

        @pl.when(k > 0)
        def _():
            acc_ref[...] += part

        @pl.when(k == nk - 1)
        def _():
            o_ref[...] = acc_ref[...].astype(o_ref.dtype)

    return pl.pallas_call(
        body, name=name, grid=grid, in_specs=[a_spec, b_spec], out_specs=o_spec,
        out_shape=jax.ShapeDtypeStruct(out_shape, out_dtype),
        scratch_shapes=[] if nk == 1 else [pltpu.VMEM(acc_shape, F32)],
        compiler_params=_cp("parallel", "parallel", "arbitrary"),
    )(a, b)


def _mm_nn(name, a, b, out_dtype=F32):
    M, K = a.shape
    chunked = b.ndim == 3
    Nc = b.shape[-1]
    N = Nc * (b.shape[0] if chunked else 1)
    tm = _pick(M, (1024, 512, 256, 128, 64, 32, 16, 8))
    tn = _pick(Nc, (1408, 1024, 896, 512, 256, 128))
    tk = _pick(K, (2048, 1408, 1024, 512, 256, 128))
    npc = Nc // tn
    if chunked:
        b_spec = pl.BlockSpec((None, tk, tn), lambda i, j, k: (j // npc, k, j % npc))
    else:
        b_spec = pl.BlockSpec((tk, tn), lambda i, j, k: (k, j))
    return _mm_call(name, a, b, (M, N), out_dtype, (((1,), (0,)), ((), ())), (M // tm, N // tn, K // tk),
                    pl.BlockSpec((tm, tk), lambda i, j, k: (i, k)), b_spec,
                    pl.BlockSpec((tm, tn), lambda i, j, k: (i, j)), (tm, tn))


def _mm_nt(name, a, b, out_dtype=F32):
    M, N = a.shape
    chunked = b.ndim == 3
    Nc = b.shape[-1]
    K = b.shape[-2]
    tm = _pick(M, (1024, 512, 256, 128, 64, 32, 16, 8))
    tn = _pick(K, (1408, 1024, 512, 256, 128))
    tk = _pick(Nc, (2048, 1792, 1408, 1024, 896, 512, 256, 128))
    npc = Nc // tk
    if chunked:
        b_spec = pl.BlockSpec((None, tn, tk), lambda i, j, k: (k // npc, j, k % npc))
    else:
        b_spec = pl.BlockSpec((tn, tk), lambda i, j, k: (j, k))
    return _mm_call(name, a, b, (M, K), out_dtype, (((1,), (1,)), ((), ())), (M // tm, K // tn, N // tk),
                    pl.BlockSpec((tm, tk), lambda i, j, k: (i, k)), b_spec,
                    pl.BlockSpec((tm, tn), lambda i, j, k: (i, j)), (tm, tn))


def _mm_tn(name, a, b, chunks=1, out_dtype=BF16):
    T, K = a.shape
    N = b.shape[1]
    Nc = N // chunks
    tm = _pick(K, (1408, 1024, 512, 256, 128))
    tn = _pick(Nc, (1408, 1024, 896, 512, 256, 128))
    tk = _pick(T, (1024, 512, 256, 128))
    npc = Nc // tn
    if chunks > 1:
        shape = (chunks, K, Nc)
        o_spec = pl.BlockSpec((None, tm, tn), lambda i, j, k: (j // npc, i, j % npc))
    else:
        shape = (K, N)
        o_spec = pl.BlockSpec((tm, tn), lambda i, j, k: (i, j))
    return _mm_call(name, a, b, shape, out_dtype, (((0,), (0,)), ((), ())), (K // tm, N // tn, T // tk),
                    pl.BlockSpec((tk, tm), lambda i, j, k: (k, i)),
                    pl.BlockSpec((tk, tn), lambda i, j, k: (k, j)), o_spec, (tm, tn))


def _row_tile(T):
    return _pick(T, (256, 128, 64, 32, 16, 8))


def _vec_spec(D):
    return pl.BlockSpec((1, D), lambda i: (0, 0))


def _normmod_fwd(name, x, gain, sc, sh):
    T, D = x.shape
    bt = _row_tile(T)

    def body(x_ref, g_ref, sc_ref, sh_ref, h_ref):
        h_ref[...] = _normmod(x_ref[...], g_ref[...], sc_ref[...], sh_ref[...]).astype(h_ref.dtype)

    rows = pl.BlockSpec((bt, D), lambda i: (i, 0))
    return pl.pallas_call(body, name=name, grid=(T // bt,), in_specs=[rows] + [_vec_spec(D)] * 3, out_specs=rows,
                          out_shape=jax.ShapeDtypeStruct((T, D), BF16), compiler_params=_cp("parallel"))(x, gain, sc, sh)


def _res_normmod_fwd(name, x, y, g, gain, sc, sh):
    T, D = x.shape
    bt = _row_tile(T)

    def body(x_ref, y_ref, gate_ref, g_ref, sc_ref, sh_ref, x1_ref, h_ref):
        x1 = x_ref[...] + gate_ref[...] * y_ref[...]
        x1_ref[...] = x1
        h_ref[...] = _normmod(x1, g_ref[...], sc_ref[...], sh_ref[...]).astype(h_ref.dtype)

    rows = pl.BlockSpec((bt, D), lambda i: (i, 0))
    return pl.pallas_call(body, name=name, grid=(T // bt,), in_specs=[rows, rows] + [_vec_spec(D)] * 4,
                          out_specs=[rows, rows],
                          out_shape=[jax.ShapeDtypeStruct((T, D), F32), jax.ShapeDtypeStruct((T, D), BF16)],
                          compiler_params=_cp("parallel"))(x, y, g, gain, sc, sh)


def _final_fwd_bwd(x, y, g, gain, target):
    T, D = x.shape
    bt = _row_tile(T)

    def body(x_ref, y_ref, gate_ref, g_ref, t_ref, loss_ref, dx_ref, dy_ref, dgate_ref, dgain_ref):
        i = pl.program_id(0)
        yv = y_ref[...]
        gate = gate_ref[...]
        x4 = x_ref[...] + gate * yv
        out, vjp = jax.vjp(_rms, x4, g_ref[...])
        err = out - t_ref[...]
        dx4, dgain = vjp(err * (1.0 / D))
        part = 0.5 * jnp.sum(jnp.mean(err * err, axis=-1, keepdims=True), axis=0, keepdims=True)

        @pl.when(i == 0)
        def _():
            loss_ref[...] = jnp.zeros_like(loss_ref)
            dgate_ref[...] = jnp.zeros_like(dgate_ref)
            dgain_ref[...] = jnp.zeros_like(dgain_ref)

        loss_ref[...] += jnp.broadcast_to(part, loss_ref.shape)
        dx_ref[...] = dx4
        dy_ref[...] = (gate * dx4).astype(dy_ref.dtype)
        dgate_ref[...] += jnp.sum(dx4 * yv, axis=0, keepdims=True)
        dgain_ref[...] += dgain

    rows = pl.BlockSpec((bt, D), lambda i: (i, 0))
    vec = _vec_spec(D)
    return pl.pallas_call(
        body, name="final_loss", grid=(T // bt,), in_specs=[rows, rows, vec, vec, rows],
        out_specs=[pl.BlockSpec((1, HEAD), lambda i: (0, 0)), rows, rows, vec, vec],
        out_shape=[jax.ShapeDtypeStruct((1, HEAD), F32), jax.ShapeDtypeStruct((T, D), F32),
                   jax.ShapeDtypeStruct((T, D), BF16), jax.ShapeDtypeStruct((1, D), F32),
                   jax.ShapeDtypeStruct((1, D), F32)],
        compiler_params=_cp("arbitrary"))(x, y, g, gain, target)


def _block_bwd(name, dx_out, dh, x_in, gain, sc, sh, y_prev=None, g_prev=None):
    T, D = x_in.shape
    bt = _row_tile(T)
    has_prev = y_prev is not None

    def body(*refs):
        if has_prev:
            dxo_ref, dh_ref, x_ref, g_ref, sc_ref, sh_ref, y_ref, gp_ref, dx_ref, dgain_ref, dsc_ref, dsh_ref, dy_ref, dgp_ref = refs
        else:
            dxo_ref, dh_ref, x_ref, g_ref, sc_ref, sh_ref, dx_ref, dgain_ref, dsc_ref, dsh_ref = refs
        i = pl.program_id(0)
        _, vjp = jax.vjp(_normmod, x_ref[...], g_ref[...], sc_ref[...], sh_ref[...])
        dxn, dgain, dsc, dsh = vjp(dh_ref[...])
        dx = dxo_ref[...] + dxn
        dx_ref[...] = dx

        @pl.when(i == 0)
        def _():
            dgain_ref[...] = jnp.zeros_like(dgain_ref)
            dsc_ref[...] = jnp.zeros_like(dsc_ref)
            dsh_ref[...] = jnp.zeros_like(dsh_ref)
            if has_prev:
                dgp_ref[...] = jnp.zeros_like(dgp_ref)

        dgain_ref[...] += dgain
        dsc_ref[...] += dsc
        dsh_ref[...] += dsh
        if has_prev:
            dy_ref[...] = (gp_ref[...] * dx).astype(dy_ref.dtype)
            dgp_ref[...] += jnp.sum(dx * y_ref[...], axis=0, keepdims=True)

    rows = pl.BlockSpec((bt, D), lambda i: (i, 0))
    vec = _vec_spec(D)
    ins = [dx_out, dh, x_in, gain, sc, sh]
    in_specs = [rows, rows, rows, vec, vec, vec]
    out_specs = [rows, vec, vec, vec]
    out_shape = [jax.ShapeDtypeStruct((T, D), F32)] + [jax.ShapeDtypeStruct((1, D), F32)] * 3
    if has_prev:
        ins += [y_prev, g_prev]
        in_specs += [rows, vec]
        out_specs += [rows, vec]
        out_shape += [jax.ShapeDtypeStruct((T, D), BF16), jax.ShapeDtypeStruct((1, D), F32)]
    return pl.pallas_call(body, name=name, grid=(T // bt,), in_specs=in_specs, out_specs=out_specs,
                          out_shape=out_shape, compiler_params=_cp("arbitrary"))(*ins)


def _sb_tiles(T):
    tq = _pick(T, (512, 256, 128))
    return tq, tq // HEAD


def _sb_fwd(proj, H):
    T = proj.shape[0]
    tq, nsub = _sb_tiles(T)
    scale = HEAD ** -0.5

    def body(q_ref, k_ref, v_ref, o_ref, l_ref, acc_ref):
        i = pl.program_id(1)
        q = q_ref[...].astype(BF16)
        later = (_iota((HEAD, HEAD), 0) > _iota((HEAD, HEAD), 1)).astype(BF16)
        row = _iota((tq, HEAD), 0)
        col = _iota((tq, HEAD), 1)

        def key_step(j, c, diagonal):
            off = pl.multiple_of(j * tq, tq)
            k = k_ref[pl.ds(off, tq), :].astype(BF16)
            v = v_ref[pl.ds(off, tq), :].astype(BF16)
            z = _dg(q, k, 1, 1) * scale
            ws = [None] * nsub
            for s in reversed(range(nsub)):
                zs = z[:, s * HEAD:(s + 1) * HEAD]
                sp = _softplus(zs)
                if diagonal:
                    strict = (s * HEAD + col) < row
                    lk = jnp.where(strict, -sp, 0.0)
                else:
                    lk = -sp
                w = jnp.exp(zs - sp + _sum_right(lk, later) + c)
                if diagonal:
                    w = jnp.where(strict, w, 0.0)
                ws[s] = w.astype(BF16)
                c = c + jnp.sum(lk, axis=1, keepdims=True)
            acc_ref[...] += _dg(jnp.concatenate(ws, axis=1), v, 1, 0)
            return c

        acc_ref[...] = jnp.zeros_like(acc_ref)
        c = key_step(i, jnp.zeros((tq, 1), F32), True)
        c = lax.fori_loop(0, i, lambda n, c: key_step(i - 1 - n, c, False), c)
        o_ref[...] = acc_ref[...].astype(o_ref.dtype)
        l_ref[...] = jnp.broadcast_to(c, (tq, HEAD))

    blk = pl.BlockSpec((tq, HEAD), lambda h, i: (i, h))
    return pl.pallas_call(
        body, name="sb_fwd", grid=(H, T // tq),
        in_specs=[blk, pl.BlockSpec((T, HEAD), lambda h, i: (0, H + h)), pl.BlockSpec((T, HEAD), lambda h, i: (0, 2 * H + h))],
        out_specs=[blk, blk],
        out_shape=[jax.ShapeDtypeStruct((T, H * HEAD), BF16), jax.ShapeDtypeStruct((T, H * HEAD), F32)],
        scratch_shapes=[pltpu.VMEM((tq, HEAD), F32)],
        compiler_params=_cp("parallel", "arbitrary"))(proj, proj, proj)


def _sb_bwd(proj, do, L, H):
    T = proj.shape[0]
    tq, nsub = _sb_tiles(T)
    scale = HEAD ** -0.5

    def body(q_ref, k_ref, v_ref, do_ref, l_ref, dq_ref, dk_ref, dv_ref):
        i = pl.program_id(1)

        @pl.when(i == 0)
        def _():
            dk_ref[...] = jnp.zeros_like(dk_ref)
            dv_ref[...] = jnp.zeros_like(dv_ref)

        dq_ref[...] = jnp.zeros_like(dq_ref)
        q = q_ref[...].astype(BF16)
        do_ = do_ref[...].astype(BF16)
        total = l_ref[...]
        upto = (_iota((HEAD, HEAD), 0) <= _iota((HEAD, HEAD), 1)).astype(BF16)
        before = (_iota((HEAD, HEAD), 0) < _iota((HEAD, HEAD), 1)).astype(BF16)
        row = _iota((tq, HEAD), 0)
        col = _iota((tq, HEAD), 1)

        def key_step(j, carry, diagonal):
            cp, ce = carry
            off = pl.multiple_of(j * tq, tq)
            k = k_ref[pl.ds(off, tq), :].astype(BF16)
            v = v_ref[pl.ds(off, tq), :].astype(BF16)
            z = _dg(q, k, 1, 1) * scale
            dw = _dg(do_, v, 1, 1)
            ws, dzs = [], []
            for s in range(nsub):
                zs = z[:, s * HEAD:(s + 1) * HEAD]
                sp = _softplus(zs)
                if diagonal:
                    strict = (s * HEAD + col) < row
                    lk = jnp.where(strict, -sp, 0.0)
                else:
                    lk = -sp
                tail = total - (_sum_right(lk, upto) + cp)
                w = jnp.exp(zs - sp + tail)
                if diagonal:
                    w = jnp.where(strict, w, 0.0)
                e = w * dw[:, s * HEAD:(s + 1) * HEAD]
                e_before = _sum_right(e, before) + ce
                sig = jnp.exp(zs - sp)
                dz = (e * (1.0 - sig) - e_before * sig) * scale
                if diagonal:
                    dz = jnp.where(strict, dz, 0.0)
                ws.append(w.astype(BF16))
                dzs.append(dz.astype(BF16))
                cp = cp + jnp.sum(lk, axis=1, keepdims=True)
                ce = ce + jnp.sum(e, axis=1, keepdims=True)
            w_all = jnp.concatenate(ws, axis=1)
            dz_all = jnp.concatenate(dzs, axis=1)
            dv_ref[pl.ds(off, tq), :] += _dg(w_all, do_, 0, 0)
            dk_ref[pl.ds(off, tq), :] += _dg(dz_all, q, 0, 0)
            dq_ref[...] += _dg(dz_all, k, 1, 0)
            return cp, ce

        zero = jnp.zeros((tq, 1), F32)
        carry = lax.fori_loop(0, i, lambda j, cr: key_step(j, cr, False), (zero, zero))
        key_step(i, carry, True)

    blk = pl.BlockSpec((tq, HEAD), lambda h, i: (i, h))
    full = pl.BlockSpec((T, HEAD), lambda h, i: (0, h))
    shp = jax.ShapeDtypeStruct((T, H * HEAD), F32)
    return pl.pallas_call(
        body, name="sb_bwd", grid=(H, T // tq),
        in_specs=[blk, pl.BlockSpec((T, HEAD), lambda h, i: (0, H + h)), pl.BlockSpec((T, HEAD), lambda h, i: (0, 2 * H + h)),
                  blk, blk],
        out_specs=[blk, full, full], out_shape=[shp, shp, shp],
        compiler_params=_cp("parallel", "arbitrary"))(proj, proj, proj, do, L)


def _hg_tile(q, fl, iv, g, st, l0, l1, gain):
    R = 2 * HG_CHUNK
    row = _iota((R, R), 0)
    col = _iota((R, R), 1)
    first = row < HG_CHUNK
    same = first == (col < HG_CHUNK)
    tri = (row >= col) & same
    lb = _sigmoid(l0 - l1)
    f = lb + (1.0 - lb) * _sigmoid(fl)
    logf = jnp.log(f)
    k = 1.0 - f
    qf = q * _sigmoid(q)
    G = _sum_left(tri.astype(BF16), logf)
    gl_a = jnp.sum(jnp.where(first, logf, 0.0), axis=0, keepdims=True)
    gl_b = jnp.sum(jnp.where(first, 0.0, logf), axis=0, keepdims=True)
    q_dec = qf * jnp.exp(G)
    k_inv = k * jnp.exp(-G)
    k_end = k * jnp.exp(jnp.where(first, gl_a, gl_b) - G)
    scores = jnp.where(tri, mm_nt(q_dec, k_inv), 0.0)
    o = mm_nn(scores, iv)
    o_a = mm_nt(q_dec, st)
    st_mid = st * jnp.exp(gl_a) + mm_tn(jnp.where(first, iv, 0.0), k_end)
    o_b = mm_nt(q_dec, st_mid)
    st_new = st_mid * jnp.exp(gl_b) + mm_tn(jnp.where(first, 0.0, iv), k_end)
    o = o + jnp.where(first, o_a, o_b)
    on = o * lax.rsqrt(jnp.mean(o * o, axis=-1, keepdims=True) + NORM_EPS) * gain
    return on * (g * _sigmoid(g)), st_new


def _hg_specs(H, c0, rev, nt):
    def at(base):
        if rev:
            return pl.BlockSpec((HEAD, HEAD), lambda h, i: (nt - 1 - i, base + h))
        return pl.BlockSpec((HEAD, HEAD), lambda h, i: (i, base + h))
    return [at(c0), at(c0 + H), at(c0 + 2 * H), at(c0 + 3 * H)]


def _hg_fwd(proj, l0, l1, gain, H, c0):
    T = proj.shape[0]
    nt = T // HEAD

    def body(q_ref, f_ref, i_ref, g_ref, l0_ref, l1_ref, gain_ref, o_ref, st_out_ref, st_ref):
        @pl.when(pl.program_id(1) == 0)
        def _():
            st_ref[...] = jnp.zeros_like(st_ref)

        st = st_ref[...]
        st_out_ref[...] = st
        out, st_new = _hg_tile(q_ref[...], f_ref[...], i_ref[...], g_ref[...], st, l0_ref[...], l1_ref[...], gain_ref[...])
        o_ref[...] = out.astype(o_ref.dtype)
        st_ref[...] = st_new

    vec = pl.BlockSpec((1, HEAD), lambda h, i: (0, h))
    return pl.pallas_call(
        body, name="hg_fwd", grid=(H, nt), in_specs=_hg_specs(H, c0, False, nt) + [vec, vec, vec],
        out_specs=[pl.BlockSpec((HEAD, HEAD), lambda h, i: (i, h)),
                   pl.BlockSpec((None, None, HEAD, HEAD), lambda h, i: (h, i, 0, 0))],
        out_shape=[jax.ShapeDtypeStruct((T, H * HEAD), BF16), jax.ShapeDtypeStruct((H, nt, HEAD, HEAD), F32)],
        scratch_shapes=[pltpu.VMEM((HEAD, HEAD), F32)],
        compiler_params=_cp("parallel", "arbitrary"))(proj, proj, proj, proj, l0, l1, gain)


def _hg_bwd(proj, states, do, l0, l1, gain, H, c0, do_c0):
    T = proj.shape[0]
    nt = T // HEAD

    def body(q_ref, f_ref, i_ref, g_ref, st_in_ref, do_ref, l0_ref, l1_ref, gain_ref,
             dq_ref, df_ref, di_ref, dg_ref, dl0_ref, dl1_ref, dgain_ref, dst_ref):
        @pl.when(pl.program_id(1) == 0)
        def _():
            dst_ref[...] = jnp.zeros_like(dst_ref)
            dl0_ref[...] = jnp.zeros_like(dl0_ref)
            dl1_ref[...] = jnp.zeros_like(dl1_ref)
            dgain_ref[...] = jnp.zeros_like(dgain_ref)

        _, vjp = jax.vjp(_hg_tile, q_ref[...], f_ref[...], i_ref[...], g_ref[...], st_in_ref[...],
                         l0_ref[...], l1_ref[...], gain_ref[...])
        dq, df, di, dg, dst, dl0, dl1, dgain = vjp((do_ref[...], dst_ref[...]))
        dq_ref[...] = dq
        df_ref[...] = df
        di_ref[...] = di
        dg_ref[...] = dg
        dst_ref[...] = dst
        dl0_ref[...] += dl0
        dl1_ref[...] += dl1
        dgain_ref[...] += dgain

    vec = pl.BlockSpec((1, HEAD), lambda h, i: (0, h))
    rblk = pl.BlockSpec((HEAD, HEAD), lambda h, i: (nt - 1 - i, h))
    shp = jax.ShapeDtypeStruct((T, H * HEAD), F32)
    vshp = jax.ShapeDtypeStruct((1, H * HEAD), F32)
    return pl.pallas_call(
        body, name="hg_bwd", grid=(H, nt),
        in_specs=_hg_specs(H, c0, True, nt) + [
            pl.BlockSpec((None, None, HEAD, HEAD), lambda h, i: (h, nt - 1 - i, 0, 0)),
            pl.BlockSpec((HEAD, HEAD), lambda h, i: (nt - 1 - i, do_c0 + h)), vec, vec, vec],
        out_specs=[rblk, rblk, rblk, rblk, vec, vec, vec],
        out_shape=[shp, shp, shp, shp, vshp, vshp, vshp],
        scratch_shapes=[pltpu.VMEM((HEAD, HEAD), F32)],
        compiler_params=_cp("parallel", "arbitrary"))(proj, proj, proj, proj, states, do, l0, l1, gain)


def _sg_chunk(u_parts, v_parts, gains, biases, wpos, bpos):
    W = sum(p.shape[1] for p in v_parts)
    C = v_parts[0].shape[0]
    v = [_gelu(p) for p in v_parts]
    mu = sum(jnp.sum(p, axis=-1, keepdims=True) for p in v) * (1.0 / W)
    xc = [p - mu for p in v]
    r = lax.rsqrt(sum(jnp.sum(p * p, axis=-1, keepdims=True) for p in xc) * (1.0 / W) + NORM_EPS)
    causal = _iota((C, C), 0) >= _iota((C, C), 1)
    out = []
    for up, p, gn, bs, w, b in zip(u_parts, xc, gains, biases, wpos, bpos):
        vn = p * r * gn + bs
        mixed = mm_nn(jnp.where(causal, w, 0.0), vn) + b
        out.append(_gelu(up) * mixed)
    return out


def _sg_fwd(zpre, vgain, vbias, wpos, bpos):
    T, W2 = zpre.shape
    W = W2 // 2
    G = wpos.shape[0]
    cg = W // G
    C = SG_CHUNK

    def body(z_ref, gn_ref, bs_ref, w_ref, b_ref, s_ref):
        sl = [slice(g * cg, (g + 1) * cg) for g in range(G)]
        out = _sg_chunk([z_ref[:, s] for s in sl], [z_ref[:, W + s.start:W + s.stop] for s in sl],
                        [gn_ref[:, s] for s in sl], [bs_ref[:, s] for s in sl],
                        [w_ref[g] for g in range(G)], [b_ref[g] for g in range(G)])
        for s, o in zip(sl, out):
            s_ref[:, s] = o.astype(s_ref.dtype)

    return pl.pallas_call(
        body, name="sg_fwd", grid=(T // C,),
        in_specs=[pl.BlockSpec((C, W2), lambda i: (i, 0)), _vec_spec(W), _vec_spec(W),
                  pl.BlockSpec((G, C, C), lambda i: (0, 0, 0)), pl.BlockSpec((G, C, 1), lambda i: (0, 0, 0))],
        out_specs=pl.BlockSpec((C, W), lambda i: (i, 0)),
        out_shape=jax.ShapeDtypeStruct((T, W), BF16), compiler_params=_cp("parallel"))(zpre, vgain, vbias, wpos, bpos)


def _sg_bwd(zpre, ds, vgain, vbias, wpos, bpos):
    T, W2 = zpre.shape
    W = W2 // 2
    G = wpos.shape[0]
    cg = W // G
    C = SG_CHUNK

    def body(z_ref, ds_ref, gn_ref, bs_ref, w_ref, b_ref, dz_ref, dgn_ref, dbs_ref, dw_ref, db_ref):
        @pl.when(pl.program_id(0) == 0)
        def _():
            dgn_ref[...] = jnp.zeros_like(dgn_ref)
            dbs_ref[...] = jnp.zeros_like(dbs_ref)
            dw_ref[...] = jnp.zeros_like(dw_ref)
            db_ref[...] = jnp.zeros_like(db_ref)

        sl = [slice(g * cg, (g + 1) * cg) for g in range(G)]
        _, vjp = jax.vjp(_sg_chunk, [z_ref[:, s] for s in sl], [z_ref[:, W + s.start:W + s.stop] for s in sl],
                         [gn_ref[:, s] for s in sl], [bs_ref[:, s] for s in sl],
                         [w_ref[g] for g in range(G)], [b_ref[g] for g in range(G)])
        du, dv, dgn, dbs, dw, db = vjp([ds_ref[:, s] for s in sl])
        for g, s in enumerate(sl):
            dz_ref[:, s] = du[g].astype(dz_ref.dtype)
            dz_ref[:, W + s.start:W + s.stop] = dv[g].astype(dz_ref.dtype)
            dgn_ref[:, s] += dgn[g]
            dbs_ref[:, s] += dbs[g]
            dw_ref[g] += dw[g]
            db_ref[g] += db[g]

    wspec = pl.BlockSpec((G, C, C), lambda i: (0, 0, 0))
    bspec = pl.BlockSpec((G, C, 1), lambda i: (0, 0, 0))
    return pl.pallas_call(
        body, name="sg_bwd", grid=(T // C,),
        in_specs=[pl.BlockSpec((C, W2), lambda i: (i, 0)), pl.BlockSpec((C, W), lambda i: (i, 0)),
                  _vec_spec(W), _vec_spec(W), wspec, bspec],
        out_specs=[pl.BlockSpec((C, W2), lambda i: (i, 0)), _vec_spec(W), _vec_spec(W), wspec, bspec],
        out_shape=[jax.ShapeDtypeStruct((T, W2), BF16), jax.ShapeDtypeStruct((1, W), F32),
                   jax.ShapeDtypeStruct((1, W), F32), jax.ShapeDtypeStruct((G, C, C), F32),
                   jax.ShapeDtypeStruct((G, C, 1), F32)],
        compiler_params=_cp("arbitrary"))(zpre, ds, vgain, vbias, wpos, bpos)


def _conv_tiles(T, F):
    return _pick(T, (512, 256, 128, 64, 32, 16, 8)), _pick(F, (512, 256, 128))


def _shift_down(cur, prev8, n, first_tile):
    bt = cur.shape[0]
    r = pltpu.roll(cur, n, 0)
    p = pltpu.roll(prev8, n, 0)
    p = jnp.where(first_tile, 0.0, p)
    head = jnp.concatenate([p, r[8:]], axis=0) if bt > 8 else p
    return jnp.where(_iota(cur.shape, 0) < n, head, r)


def _shift_up(cur, next8, n, last_tile):
    bt = cur.shape[0]
    r = pltpu.roll(cur, bt - n, 0)
    p = pltpu.roll(next8, 8 - n, 0)
    p = jnp.where(last_tile, 0.0, p)
    tail = jnp.concatenate([r[:bt - 8], p], axis=0) if bt > 8 else p
    return jnp.where(_iota(cur.shape, 0) >= bt - n, tail, r)


def _conv_apply(cur, prev8, w_ref, b, first_tile):
    return (b + w_ref[0:1, :] * _shift_down(cur, prev8, 2, first_tile)
            + w_ref[1:2, :] * _shift_down(cur, prev8, 1, first_tile) + w_ref[2:3, :] * cur)


def _conv_fwd(name, a, w, b):
    T, F2 = a.shape
    F = F2 // 2
    bt, cw = _conv_tiles(T, F)
    nf = F // cw
    r8 = bt // 8

    def body(g_ref, gp_ref, v_ref, vp_ref, wg_ref, wv_ref, bg_ref, bv_ref, u_ref):
        first = pl.program_id(0) == 0
        gate = _conv_apply(g_ref[...], gp_ref[...], wg_ref, bg_ref[...], first)
        val = _conv_apply(v_ref[...], vp_ref[...], wv_ref, bv_ref[...], first)
        u_ref[...] = (gate * _sigmoid(gate) * val).astype(u_ref.dtype)

    def cur(off):
        return pl.BlockSpec((bt, cw), lambda i, j: (i, j + off))

    def prev(off):
        return pl.BlockSpec((8, cw), lambda i, j: (jnp.maximum(i * r8 - 1, 0), j + off))

    def vec(rows, off):
        return pl.BlockSpec((rows, cw), lambda i, j: (0, j + off))

    return pl.pallas_call(
        body, name=name, grid=(T // bt, nf),
        in_specs=[cur(0), prev(0), cur(nf), prev(nf), vec(3, 0), vec(3, nf), vec(1, 0), vec(1, nf)],
        out_specs=pl.BlockSpec((bt, cw), lambda i, j: (i, j)),
        out_shape=jax.ShapeDtypeStruct((T, F), BF16),
        compiler_params=_cp("parallel", "parallel"))(a, a, a, a, w, w, b, b)


def _conv_bwd_act(name, a, du, w, b):
    T, F2 = a.shape
    F = F2 // 2
    bt, cw = _conv_tiles(T, F)
    nf = F // cw
    r8 = bt // 8

    def body(s_ref, sp_ref, o_ref, op_ref, du_ref, ws_ref, wo_ref, bs_ref, bo_ref, da_ref, dw_ref, db_ref):
        i = pl.program_id(1)
        first = i == 0
        is_gate = pl.program_id(0) < nf
        cur = s_ref[...]
        prev8 = sp_ref[...]
        mine = _conv_apply(cur, prev8, ws_ref, bs_ref[...], first)
        other = _conv_apply(o_ref[...], op_ref[...], wo_ref, bo_ref[...], first)
        gate = jnp.where(is_gate, mine, other)
        val = jnp.where(is_gate, other, mine)
        sg = _sigmoid(gate)
        du_ = du_ref[...]
        d_gate = du_ * val * (sg * (1.0 + gate * (1.0 - sg)))
        d_val = du_ * gate * sg
        da = jnp.where(is_gate, d_gate, d_val)
        da_ref[...] = da

        @pl.when(first)
        def _():
            dw_ref[...] = jnp.zeros_like(dw_ref)
            db_ref[...] = jnp.zeros_like(db_ref)

        rows = [jnp.sum(da * _shift_down(cur, prev8, 2, first), axis=0, keepdims=True),
                jnp.sum(da * _shift_down(cur, prev8, 1, first), axis=0, keepdims=True),
                jnp.sum(da * cur, axis=0, keepdims=True)]
        for t in range(CONV_WIDTH):
            dw_ref[t:t + 1, :] += rows[t]
        db_ref[...] += jnp.sum(da, axis=0, keepdims=True)

    n2 = 2 * nf

    def cur(off):
        return pl.BlockSpec((bt, cw), lambda j, i: (i, (j + off) % n2))

    def prev(off):
        return pl.BlockSpec((8, cw), lambda j, i: (jnp.maximum(i * r8 - 1, 0), (j + off) % n2))

    def vec(rows, off):
        return pl.BlockSpec((rows, cw), lambda j, i: (0, (j + off) % n2))

    return pl.pallas_call(
        body, name=name, grid=(n2, T // bt),
        in_specs=[cur(0), prev(0), cur(nf), prev(nf), pl.BlockSpec((bt, cw), lambda j, i: (i, j % nf)),
                  vec(3, 0), vec(3, nf), vec(1, 0), vec(1, nf)],
        out_specs=[pl.BlockSpec((bt, cw), lambda j, i: (i, j)), vec(3, 0), vec(1, 0)],
        out_shape=[jax.ShapeDtypeStruct((T, F2), F32), jax.ShapeDtypeStruct((3, F2), F32),
                   jax.ShapeDtypeStruct((1, F2), F32)],
        compiler_params=_cp("parallel", "arbitrary"))(a, a, a, a, du, w, w, b, b)


def _conv_bwd_in(name, da_out, w):
    T, F2 = da_out.shape
    bt, cw = _conv_tiles(T, F2 // 2)
    r8 = bt // 8
    last_blk = T // 8 - 1

    def body(d_ref, dn_ref, w_ref, o_ref):
        last = pl.program_id(0) == pl.num_programs(0) - 1
        cur = d_ref[...]
        nxt = dn_ref[...]
        o_ref[...] = (w_ref[2:3, :] * cur + w_ref[1:2, :] * _shift_up(cur, nxt, 1, last)
                      + w_ref[0:1, :] * _shift_up(cur, nxt, 2, last)).astype(o_ref.dtype)

    return pl.pallas_call(
        body, name=name, grid=(T // bt, F2 // cw),
        in_specs=[pl.BlockSpec((bt, cw), lambda i, j: (i, j)),
                  pl.BlockSpec((8, cw), lambda i, j: (jnp.minimum((i + 1) * r8, last_blk), j)),
                  pl.BlockSpec((3, cw), lambda i, j: (0, j))],
        out_specs=pl.BlockSpec((bt, cw), lambda i, j: (i, j)),
        out_shape=jax.ShapeDtypeStruct((T, F2), BF16),
        compiler_params=_cp("parallel", "parallel"))(da_out, da_out, w)


def _ada_fwd(c_all, ada_w, ada_b):
    R, D = c_all.shape
    L, _, Ns = ada_w.shape
    tn = _pick(Ns, (512, 256, 128))

    def body(c_ref, w_ref, b_ref, o_ref):
        cv = c_ref[...]
        cond = cv * _sigmoid(cv)
        o_ref[...] = _dg(cond, w_ref[...], 1, 0) + b_ref[...]

    return pl.pallas_call(
        body, name="ada_fwd", grid=(L, Ns // tn),
        in_specs=[pl.BlockSpec((R, D), lambda l, j: (0, 0)), pl.BlockSpec((None, D, tn), lambda l, j: (l, 0, j)),
                  pl.BlockSpec((None, 1, tn), lambda l, j: (l, 0, j))],
        out_specs=pl.BlockSpec((None, R, tn), lambda l, j: (l, 0, j)),
        out_shape=jax.ShapeDtypeStruct((L, R, Ns), F32), compiler_params=_cp("parallel", "parallel"))(c_all, ada_w, ada_b)


def _adam_math(w, g, m, v):
    m2 = ADAM_B1 * m + (1.0 - ADAM_B1) * g
    v2 = ADAM_B2 * v + (1.0 - ADAM_B2) * (g * g)
    m_hat = m2 / (1.0 - ADAM_B1 ** ADAM_STEP)
    v_hat = v2 / (1.0 - ADAM_B2 ** ADAM_STEP)
    delta = -ADAM_LR * (m_hat / (jnp.sqrt(v_hat) + ADAM_EPS) + ADAM_WD * w)
    return delta, m2, v2


def _ada_grad_adam(c_all_t, dmod, w, m, v):
    D, R = c_all_t.shape
    L, _, Ns = dmod.shape
    tr = _rows_within(D, Ns * 4, 1 << 20)

    def body(c_ref, d_ref, w_ref, m_ref, v_ref, g_ref, dl_ref, m2_ref, v2_ref):
        cv = c_ref[...]
        g = _dg(cv * _sigmoid(cv), d_ref[...], 1, 0)
        g_ref[...] = g
        dl_ref[...], m2_ref[...], v2_ref[...] = _adam_math(w_ref[...], g, m_ref[...], v_ref[...])

    big = pl.BlockSpec((None, tr, Ns), lambda l, i: (l, i, 0))
    shp = jax.ShapeDtypeStruct((L, D, Ns), F32)
    return pl.pallas_call(
        body, name="ada_grad_adam", grid=(L, D // tr),
        in_specs=[pl.BlockSpec((tr, R), lambda l, i: (i, 0)), pl.BlockSpec((None, R, Ns), lambda l, i: (l, 0, 0)), big, big, big],
        out_specs=[big] * 4, out_shape=[shp] * 4, compiler_params=_cp("parallel", "parallel"))(c_all_t, dmod, w, m, v)


def _adam(name, w, g, m, v):
    R, C = w.shape
    tr = _rows_within(R, C * 4, 1 << 21)

    def body(w_ref, g_ref, m_ref, v_ref, dl_ref, m2_ref, v2_ref):
        dl_ref[...], m2_ref[...], v2_ref[...] = _adam_math(w_ref[...], g_ref[...], m_ref[...], v_ref[...])

    blk = pl.BlockSpec((tr, C), lambda i: (i, 0))
    shp = jax.ShapeDtypeStruct((R, C), F32)
    return pl.pallas_call(body, name=name, grid=(R // tr,), in_specs=[blk] * 4, out_specs=[blk] * 3,
                          out_shape=[shp] * 3, compiler_params=_cp("parallel"))(w, g, m, v)


def _cast_bf16(name, w):
    R, C = w.shape
    tr = _rows_within(R, C * 4, 1 << 22)

    def body(w_ref, o_ref):
        o_ref[...] = w_ref[...].astype(BF16)

    blk = pl.BlockSpec((tr, C), lambda i: (i, 0))
    return pl.pallas_call(body, name=name, grid=(R // tr,), in_specs=[blk], out_specs=blk,
                          out_shape=jax.ShapeDtypeStruct((R, C), BF16), compiler_params=_cp("parallel"))(w)


def _add_pairs(name, a, b):
    _, R, C = a.shape
    tr = _rows_within(R, C * 2, 1 << 21)

    def body(a_ref, b_ref, o_ref):
        o_ref[...] = (a_ref[...].astype(F32) + b_ref[...].astype(F32)).astype(o_ref.dtype)

    blk = pl.BlockSpec((None, tr, C), lambda j, i: (j, i, 0))
    return pl.pallas_call(body, name=name, grid=(4, R // tr), in_specs=[blk, blk], out_specs=blk,
                          out_shape=jax.ShapeDtypeStruct(a.shape, BF16), compiler_params=_cp("parallel", "parallel"))(a, b)


def _sum_into_pair(name, a, slot):
    n, R, C = a.shape
    tr = _rows_within(R, n * C * a.dtype.itemsize, 1 << 23)

    def body(slot_ref, a_ref, o_ref):
        acc = a_ref[0].astype(F32)
        for j in range(1, n):
            acc = acc + a_ref[j].astype(F32)
        o_ref[...] = acc

    grid_spec = pltpu.PrefetchScalarGridSpec(
        num_scalar_prefetch=1, grid=(R // tr,),
        in_specs=[pl.BlockSpec((n, tr, C), lambda i, s: (0, i, 0))],
        out_specs=pl.BlockSpec((None, tr, C), lambda i, s: (s[0], i, 0)))
    return pl.pallas_call(body, name=name, grid_spec=grid_spec, out_shape=jax.ShapeDtypeStruct((2, R, C), F32),
                          compiler_params=_cp("arbitrary"))(slot.reshape(1).astype(jnp.int32), a)


def _sum_leading(name, a, out_dtype=F32):
    n, R, C = a.shape
    tr = _rows_within(R, n * C * a.dtype.itemsize, 1 << 23)

    def body(a_ref, o_ref):
        acc = a_ref[0].astype(F32)
        for j in range(1, n):
            acc = acc + a_ref[j].astype(F32)
        o_ref[...] = acc.astype(o_ref.dtype)

    return pl.pallas_call(body, name=name, grid=(R // tr,), in_specs=[pl.BlockSpec((n, tr, C), lambda i: (0, i, 0))],
                          out_specs=pl.BlockSpec((tr, C), lambda i: (i, 0)),
                          out_shape=jax.ShapeDtypeStruct((R, C), out_dtype), compiler_params=_cp("parallel"))(a)


def _place():
    return lax.axis_index("x"), lax.axis_index("y"), lax.axis_index("c")


def _all_gather(name, blocks, halves=False):
    n = len(blocks)
    shapes = [b.shape[1:] if halves else b.shape for b in blocks]

    def body(*refs):
        ins, outs = refs[:n], refs[n:2 * n]
        send_sems, recv_sems, local_sems = refs[2 * n:]
        x, y, c = _place()
        me, sibling = (x, y, c), (x, y, 1 - c)
        chips = [(1 - x, y), (x, 1 - y), (1 - x, 1 - y)]

        def rows(a, px, py, pc):
            return outs[a].at[4 * px + 2 * py + pc]

        def copy(a, k, block, to, src=None):
            return pltpu.make_async_remote_copy(
                src_ref=rows(a, *block) if src is None else src, dst_ref=rows(a, *block),
                send_sem=send_sems.at[7 * a + k], recv_sem=recv_sems.at[7 * a + k],
                device_id=to, device_id_type=MESH)

        started = []
        mine = []
        for a in range(n):
            src = ins[a].at[c] if halves else ins[a]
            mine.append(pltpu.make_async_copy(src, rows(a, *me), local_sems.at[a]))
            mine[-1].start()
            first = [copy(a, 0, me, sibling, src=src)]
            first += [copy(a, 1 + j, me, (*chip, c), src=src) for j, chip in enumerate(chips)]
            for cp in first:
                cp.start()
            started += first
        for j, chip in enumerate(chips):
            for a in range(n):
                copy(a, 1 + j, (*chip, c), me).wait_recv()
                passed = copy(a, 4 + j, (*chip, c), sibling)
                passed.start()
                started.append(passed)
        for a in range(n):
            copy(a, 0, sibling, me).wait_recv()
            for j, chip in enumerate(chips):
                copy(a, 4 + j, (*chip, 1 - c), me).wait_recv()
        for cp in started:
            cp.wait_send()
        for cp in mine:
            cp.wait()

    return pl.pallas_call(
        body, name=name, in_specs=[ANY] * n, out_specs=[ANY] * n,
        out_shape=[jax.ShapeDtypeStruct((N_DEV,) + tuple(s), b.dtype) for s, b in zip(shapes, blocks)],
        scratch_shapes=[pltpu.SemaphoreType.DMA((7 * n,)), pltpu.SemaphoreType.DMA((7 * n,)),
                        pltpu.SemaphoreType.DMA((n,))],
    )(*blocks)


def _to_sibling(name, arrays, pick):
    n = len(arrays)
    per = 4 if pick == "other_half" else 1

    def body(*refs):
        ins, outs = refs[:n], refs[n:2 * n]
        send_sems, recv_sems, local_sems = refs[2 * n:]
        x, y, c = _place()
        sibling = (x, y, 1 - c)
        started = []
        local = []
        for a in range(n):
            for j in range(per):
                if pick == "other_half":
                    src, dst = ins[a].at[2 * j + (1 - c)], outs[a].at[j]
                else:
                    src, dst = ins[a], outs[a].at[c]
                    local.append(pltpu.make_async_copy(src, dst, local_sems.at[a]))
                    local[-1].start()
                cp = pltpu.make_async_remote_copy(src_ref=src, dst_ref=dst, send_sem=send_sems.at[per * a + j],
                                                  recv_sem=recv_sems.at[per * a + j], device_id=sibling, device_id_type=MESH)
                cp.start()
                started.append(cp)
        for cp in started:
            cp.wait()
        for cp in local:
            cp.wait()

    if pick == "other_half":
        out_shape = [jax.ShapeDtypeStruct((4,) + a.shape[1:], a.dtype) for a in arrays]
    else:
        out_shape = [jax.ShapeDtypeStruct((2,) + a.shape, a.dtype) for a in arrays]
    return pl.pallas_call(
        body, name=name, in_specs=[ANY] * n, out_specs=[ANY] * n, out_shape=out_shape,
        scratch_shapes=[pltpu.SemaphoreType.DMA((per * n,)), pltpu.SemaphoreType.DMA((per * n,)),
                        pltpu.SemaphoreType.DMA((n,))],
    )(*arrays)


def _share_halves(name, arrays):
    n = len(arrays)

    def body(*refs):
        ins, outs = refs[:n], refs[n:2 * n]
        send_sems, recv_sems = refs[2 * n:]
        x, y, c = _place()
        started = []
        for a in range(n):
            cp = pltpu.make_async_remote_copy(src_ref=ins[a].at[c], dst_ref=outs[a].at[c], send_sem=send_sems.at[a],
                                              recv_sem=recv_sems.at[a], device_id=(x, y, 1 - c), device_id_type=MESH)
            cp.start()
            started.append(cp)
        for a in range(n):
            started[a].wait_send()
            pltpu.make_async_remote_copy(src_ref=ins[a].at[1 - c], dst_ref=outs[a].at[1 - c], send_sem=send_sems.at[a],
                                         recv_sem=recv_sems.at[a], device_id=(x, y, 1 - c), device_id_type=MESH).wait_recv()

    return pl.pallas_call(
        body, name=name, in_specs=[ANY] * n, out_specs=[ANY] * n,
        out_shape=[jax.ShapeDtypeStruct(a.shape, a.dtype) for a in arrays],
        input_output_aliases={a: a for a in range(n)},
        scratch_shapes=[pltpu.SemaphoreType.DMA((n,)), pltpu.SemaphoreType.DMA((n,))],
    )(*arrays)


def _between_chips(name, arrays):
    n = len(arrays)

    def body(*refs):
        ins, outs = refs[:n], refs[n:2 * n]
        send_sems, recv_sems, local_sems = refs[2 * n:]
        x, y, c = _place()
        mine = 2 * x + y
        flips = [(1 - x, y), (x, 1 - y), (1 - x, 1 - y)]
        started = []
        local = []
        for a in range(n):
            local.append(pltpu.make_async_copy(ins[a].at[mine], outs[a].at[mine], local_sems.at[a]))
            local[-1].start()
            for k, (px, py) in enumerate(flips):
                cp = pltpu.make_async_remote_copy(
                    src_ref=ins[a].at[2 * px + py], dst_ref=outs[a].at[mine],
                    send_sem=send_sems.at[3 * a + k], recv_sem=recv_sems.at[3 * a + k],
                    device_id=(px, py, c), device_id_type=MESH)
                cp.start()
                started.append(cp)
        for cp in started:
            cp.wait()
        for cp in local:
            cp.wait()

    return pl.pallas_call(
        body, name=name, in_specs=[ANY] * n, out_specs=[ANY] * n,
        out_shape=[jax.ShapeDtypeStruct(a.shape, a.dtype) for a in arrays],
        scratch_shapes=[pltpu.SemaphoreType.DMA((3 * n,)), pltpu.SemaphoreType.DMA((3 * n,)),
                        pltpu.SemaphoreType.DMA((n,))],
    )(*arrays)


def kernel(x, c, ada_w, ada_b, mix_norm, ffn_norm, par_w_in, par_w_out, hg_lb_logits, hg_out_norm, sg_w_in, sg_v_gain, sg_v_bias, sg_w_pos, sg_b_pos, sg_w_out, ffn_up, ffn_conv_w, ffn_conv_b, ffn_down, final_norm, loss_target, m_ada_w, m_ada_b, m_mix_norm, m_ffn_norm, m_par_w_in, m_par_w_out, m_hg_lb_logits, m_hg_out_norm, m_sg_w_in, m_sg_v_gain, m_sg_v_bias, m_sg_w_pos, m_sg_b_pos, m_sg_w_out, m_ffn_up, m_ffn_conv_w, m_ffn_conv_b, m_ffn_down, m_final_norm, v_ada_w, v_ada_b, v_mix_norm, v_ffn_norm, v_par_w_in, v_par_w_out, v_hg_lb_logits, v_hg_out_norm, v_sg_w_in, v_sg_v_gain, v_sg_v_bias, v_sg_w_pos, v_sg_b_pos, v_sg_w_out, v_ffn_up, v_ffn_conv_w, v_ffn_conv_b, v_ffn_down, v_final_norm):
    names = ["ada_w", "ada_b", "mix_norm", "ffn_norm", "par_w_in", "par_w_out", "hg_lb_logits", "hg_out_norm", "sg_w_in",
             "sg_v_gain", "sg_v_bias", "sg_w_pos", "sg_b_pos", "sg_w_out", "ffn_up", "ffn_conv_w", "ffn_conv_b",
             "ffn_down", "final_norm"]
    W = dict(zip(names, [ada_w, ada_b, mix_norm, ffn_norm, par_w_in, par_w_out, hg_lb_logits, hg_out_norm, sg_w_in,
                         sg_v_gain, sg_v_bias, sg_w_pos, sg_b_pos, sg_w_out, ffn_up, ffn_conv_w, ffn_conv_b, ffn_down,
                         final_norm]))
    M = dict(zip(names, [m_ada_w, m_ada_b, m_mix_norm, m_ffn_norm, m_par_w_in, m_par_w_out, m_hg_lb_logits, m_hg_out_norm,
                         m_sg_w_in, m_sg_v_gain, m_sg_v_bias, m_sg_w_pos, m_sg_b_pos, m_sg_w_out, m_ffn_up, m_ffn_conv_w,
                         m_ffn_conv_b, m_ffn_down, m_final_norm]))
    V = dict(zip(names, [v_ada_w, v_ada_b, v_mix_norm, v_ffn_norm, v_par_w_in, v_par_w_out, v_hg_lb_logits, v_hg_out_norm,
                         v_sg_w_in, v_sg_v_gain, v_sg_v_bias, v_sg_w_pos, v_sg_b_pos, v_sg_w_out, v_ffn_up, v_ffn_conv_w,
                         v_ffn_conv_b, v_ffn_down, v_final_norm]))

    x = x[0]
    target = loss_target[0]
    T, D = x.shape
    ix, iy, ic = _place()
    chip = 2 * ix + iy
    dev = 2 * chip + ic
    H = hg_out_norm.shape[1]
    SBW = H * HEAD
    NA = ada_w.shape[2]
    F2s = ffn_up.shape[2]
    F2 = N_CHIP * F2s
    SGW = sg_w_out.shape[1] * N_CHIP
    G = sg_w_pos.shape[1]

    n_cw = ffn_conv_w.size
    n_sv = sg_v_gain.size
    c_all, small_all = _all_gather("gather_small", [c, _pack_rows([ffn_conv_w, sg_v_gain, sg_v_bias])])
    c_all = c_all.reshape(N_DEV, D)
    small_all = small_all.reshape(N_CHIP, 2, -1)[:, 0]
    conv_w_full = small_all[:, :n_cw].reshape(N_CHIP, 2, CONV_WIDTH, F2s).transpose(1, 2, 0, 3).reshape(2, CONV_WIDTH, F2)
    sg_gain_full = small_all[:, n_cw:n_cw + n_sv].reshape(1, SGW)
    sg_bias_full = small_all[:, n_cw + n_sv:n_cw + 2 * n_sv].reshape(1, SGW)

    c_pad = jnp.pad(c_all, ((0, 16 - N_DEV), (0, 0)))
    ada_b_sh = lax.dynamic_slice(ada_b, (0, chip * NA), (2, NA)).reshape(2, 1, NA)
    mod_sh = _ada_fwd(c_pad, ada_w, ada_b_sh)
    mod_all, = _all_gather("gather_mod", [mod_sh[:, :N_DEV]])
    mod_all = mod_all.reshape(N_CHIP, 2, 2, N_DEV, NA)[:, 0]
    mod = lax.dynamic_index_in_dim(mod_all, dev, axis=2, keepdims=False)
    mod = mod.transpose(1, 0, 2).reshape(2, 6, D)
    mods = [[mod[l, k].reshape(1, D) for k in range(6)] for l in range(2)]

    def halves_of(w2d, kind):
        K, N = w2d.shape
        b = _cast_bf16("cast_w", w2d)
        return b.reshape(2, K // 2, N)

    shards = [par_w_in[0], par_w_out[0], sg_w_in[0], sg_w_out[0], ffn_up[0], ffn_up[1], ffn_down[0], ffn_down[1]]
    kinds = ["col", "row", "col", "row", "col", "col", "row", "row"]
    gathered = _all_gather("gather_weights", [halves_of(w, k) for w, k in zip(shards, kinds)], halves=True)
    full = []
    for g8, w, k in zip(gathered, shards, kinds):
        K, N = w.shape
        full.append(g8.reshape(N_CHIP, K, N) if k == "col" else g8.reshape(N_CHIP * K, N))
    w_in, w_out, wsg_in, wsg_out, wup0, wup1, wdn0, wdn1 = full
    wup = [wup0, wup1]
    wdn = [wdn0, wdn1]

    vec = lambda a: a.reshape(1, -1)
    l0 = vec(hg_lb_logits[0])
    l1 = vec(hg_lb_logits[1])
    hg_gain = vec(hg_out_norm[0])
    wpos = sg_w_pos[0]
    bpos = sg_b_pos[0].reshape(G, SG_CHUNK, 1)
    conv_b = [vec(ffn_conv_b[l]) for l in range(2)]

    sh1, sc1, g1, sh2, sc2, g2 = mods[0]
    h0 = _normmod_fwd("norm_mix0", x, vec(mix_norm[0]), sc1, sh1)
    proj = _mm_nn("mm_par_in", h0, w_in)
    o_sb, sb_tot = _sb_fwd(proj, H)
    o_hg, hg_states = _hg_fwd(proj, l0, l1, hg_gain, H, 3 * H)
    o_cat = jnp.concatenate([o_sb, o_hg], axis=1)
    y0 = _mm_nn("mm_par_out", o_cat, w_out)
    x1, h0f = _res_normmod_fwd("res_norm_ffn0", x, y0, g1, vec(ffn_norm[0]), sc2, sh2)
    a0 = _mm_nn("mm_up0", h0f, wup[0])
    u0 = _conv_fwd("conv_fwd0", a0, conv_w_full[0], conv_b[0])
    f0 = _mm_nn("mm_down0", u0, wdn[0])
    sh1b, sc1b, g1b, sh2b, sc2b, g2b = mods[1]
    x2, h1 = _res_normmod_fwd("res_norm_mix1", x1, f0, g2, vec(mix_norm[1]), sc1b, sh1b)
    zpre = _mm_nn("mm_sg_in", h1, wsg_in)
    s1 = _sg_fwd(zpre, sg_gain_full, sg_bias_full, wpos, bpos)
    y1 = _mm_nn("mm_sg_out", s1, wsg_out)
    x3, h1f = _res_normmod_fwd("res_norm_ffn1", x2, y1, g1b, vec(ffn_norm[1]), sc2b, sh2b)
    a1 = _mm_nn("mm_up1", h1f, wup[1])
    u1 = _conv_fwd("conv_fwd1", a1, conv_w_full[1], conv_b[1])
    f1 = _mm_nn("mm_down1", u1, wdn[1])
    loss_sum, dx, df1, dg2b, d_final = _final_fwd_bwd(x3, f1, g2b, vec(final_norm), target)
    loss = lax.psum(loss_sum[0, 0], ("x", "y", "c"))

    def ffn_bwd(l, dfl, u, a, hf):
        g_dn = _mm_tn("mm_g_down", u, dfl)
        du = _mm_nt("mm_d_u", dfl, wdn[l])
        da_out, dcw, dcb = _conv_bwd_act("conv_bwd_act", a, du, conv_w_full[l], conv_b[l])
        da = _conv_bwd_in("conv_bwd_in", da_out, conv_w_full[l])
        g_up = _mm_tn("mm_g_up", hf, da, chunks=N_CHIP)
        dh = _mm_nt("mm_d_hf", da, wup[l])
        return g_dn, g_up, dcw, dcb, dh

    g_dn1, g_up1, dcw1, dcb1, dh1f = ffn_bwd(1, df1, u1, a1, h1f)
    dx, dgn_f1, dsc2b, dsh2b, dy1, dg1b = _block_bwd("bwd_ffn1", dx, dh1f, x3, vec(ffn_norm[1]), sc2b, sh2b, y1, g1b)
    g_sg_out = _mm_tn("mm_g_sg_out", s1, dy1)
    ds1 = _mm_nt("mm_d_s", dy1, wsg_out)
    dzpre, dsg_gain, dsg_bias, dwpos, dbpos = _sg_bwd(zpre, ds1, sg_gain_full, sg_bias_full, wpos, bpos)
    g_sg_in = _mm_tn("mm_g_sg_in", h1, dzpre, chunks=N_CHIP)
    dh1 = _mm_nt("mm_d_h1", dzpre, wsg_in)
    dx, dgn_m1, dsc1b, dsh1b, df0, dg2 = _block_bwd("bwd_mix1", dx, dh1, x2, vec(mix_norm[1]), sc1b, sh1b, f0, g2)
    g_dn0, g_up0, dcw0, dcb0, dh0f = ffn_bwd(0, df0, u0, a0, h0f)
    dx, dgn_f0, dsc2, dsh2, dy0, dg1 = _block_bwd("bwd_ffn0", dx, dh0f, x1, vec(ffn_norm[0]), sc2, sh2, y0, g1)
    g_out = _mm_tn("mm_g_par_out", o_cat, dy0)
    do = _mm_nt("mm_d_o", dy0, w_out)
    dq, dk, dv = _sb_bwd(proj, do, sb_tot, H)
    dhq, dhf, dhi, dhg, dl0, dl1, dhg_gain = _hg_bwd(proj, hg_states, do, l0, l1, hg_gain, H, 3 * H, H)
    dproj = jnp.concatenate([dq, dk, dv, dhq, dhf, dhi, dhg], axis=1).astype(BF16)
    g_in = _mm_tn("mm_g_par_in", h0, dproj, chunks=N_CHIP)
    dh0 = _mm_nt("mm_d_h0", dproj, w_in)
    grad_x, dgn_m0, dsc1, dsh1 = _block_bwd("bwd_mix0", dx, dh0, x, vec(mix_norm[0]), sc1, sh1)

    dmod = jnp.concatenate([dsh1, dsc1, dg1, dsh2, dsc2, dg2, dsh1b, dsc1b, dg1b, dsh2b, dsc2b, dg2b], axis=1)
    parts = [dmod, dgn_m0, dgn_m1, dgn_f0, dgn_f1, dl0, dl1, dhg_gain, dsg_gain, dsg_bias, dwpos, dbpos,
             dcw0, dcw1, dcb0, dcb1, d_final]
    sizes = [p.size for p in parts]
    packed = _pack_rows(parts)
    packed_all, = _all_gather("gather_small_grads", [packed])
    summed = _sum_leading("sum_small_grads", packed_all).reshape(-1)
    offs = [0]
    for s in sizes:
        offs.append(offs[-1] + s)
    red = [summed[offs[i]:offs[i + 1]] for i in range(len(parts))]
    (r_dmod, r_gm0, r_gm1, r_gf0, r_gf1, r_l0, r_l1, r_hgain, r_sgain, r_sbias, r_wpos, r_bpos,
     r_cw0, r_cw1, r_cb0, r_cb1, r_final) = red
    n_mod = sizes[0]
    dmod_all = packed_all.reshape(N_DEV, -1)[:, :n_mod].reshape(N_DEV, 2, 6 * D)

    G_ = {}
    G_["ada_b"] = r_dmod.reshape(2, 6 * D)
    G_["mix_norm"] = jnp.stack([r_gm0, r_gm1])
    G_["ffn_norm"] = jnp.stack([r_gf0, r_gf1])
    G_["hg_lb_logits"] = jnp.stack([r_l0, r_l1])
    G_["hg_out_norm"] = r_hgain.reshape(hg_out_norm.shape)
    G_["sg_v_gain"] = lax.dynamic_slice(r_sgain, (chip * n_sv,), (n_sv,)).reshape(sg_v_gain.shape)
    G_["sg_v_bias"] = lax.dynamic_slice(r_sbias, (chip * n_sv,), (n_sv,)).reshape(sg_v_bias.shape)
    G_["sg_w_pos"] = r_wpos.reshape(sg_w_pos.shape)
    G_["sg_b_pos"] = r_bpos.reshape(sg_b_pos.shape)
    cw_full = jnp.stack([r_cw0.reshape(CONV_WIDTH, F2), r_cw1.reshape(CONV_WIDTH, F2)])
    G_["ffn_conv_w"] = lax.dynamic_slice(cw_full, (0, 0, chip * F2s), (2, CONV_WIDTH, F2s))
    G_["ffn_conv_b"] = jnp.stack([r_cb0, r_cb1])
    G_["final_norm"] = r_final

    big = [g_in, g_out, g_sg_in, g_sg_out, g_up0, g_up1, g_dn0, g_dn1]
    eights = [g.reshape((N_DEV, -1, g.shape[-1])) for g in big]
    from_sib = _to_sibling("grads_to_sibling", eights, "other_half")
    own = [lax.dynamic_index_in_dim(e.reshape((N_CHIP, 2) + e.shape[1:]), ic, axis=1, keepdims=False) for e in eights]
    pair = [_add_pairs("add_pair", o, r) for o, r in zip(own, from_sib)]
    crossed = _between_chips("grads_between_chips", pair)
    halves = [_sum_into_pair("sum_chips", cr, ic) for cr in crossed]
    both = _share_halves("grads_share", halves)
    g_shards = [b.reshape(w.shape) for b, w in zip(both, shards)]
    G_["par_w_in"] = g_shards[0][None]
    G_["par_w_out"] = g_shards[1][None]
    G_["sg_w_in"] = g_shards[2][None]
    G_["sg_w_out"] = g_shards[3][None]
    G_["ffn_up"] = jnp.stack([g_shards[4], g_shards[5]])
    G_["ffn_down"] = jnp.stack([g_shards[6], g_shards[7]])

    delta, new_m, new_v = {}, {}, {}
    c_t = jnp.pad(c_all, ((0, HEAD - N_DEV), (0, 0))).T
    dmod_sh = lax.dynamic_slice(dmod_all.transpose(1, 0, 2), (0, 0, chip * NA), (2, N_DEV, NA))
    dmod_sh = jnp.pad(dmod_sh, ((0, 0), (0, HEAD - N_DEV), (0, 0)))
    G_["ada_w"], delta["ada_w"], new_m["ada_w"], new_v["ada_w"] = _ada_grad_adam(c_t, dmod_sh, ada_w, m_ada_w, v_ada_w)
    for nme in ["par_w_in", "par_w_out", "sg_w_in", "sg_w_out", "ffn_up", "ffn_down"]:
        w = W[nme]
        shp = w.shape
        r2 = lambda a: a.reshape(-1, shp[-1])
        d_, m_, v_ = _adam("adam_" + nme, r2(w), r2(G_[nme]), r2(M[nme]), r2(V[nme]))
        delta[nme], new_m[nme], new_v[nme] = d_.reshape(shp), m_.reshape(shp), v_.reshape(shp)
    small = [n_ for n_ in names if n_ not in delta]
    pk = lambda dct: _pack_rows([dct[n_] for n_ in small])
    d_, m_, v_ = _adam("adam_small", pk(W), pk(G_), pk(M), pk(V))
    off = 0
    for n_ in small:
        sz = W[n_].size
        for dst, src in ((delta, d_), (new_m, m_), (new_v, v_)):
            dst[n_] = src.reshape(-1)[off:off + sz].reshape(W[n_].shape)
        off += sz

    return (loss, grad_x[None], *[G_[n_] for n_ in names], *[delta[n_] for n_ in names],
            *[new_m[n_] for n_ in names], *[new_v[n_] for n_ in names])
```

```python
import functools
import math

import jax
import jax.numpy as jnp
from jax import lax
from jax.experimental import pallas as pl
from jax.experimental.pallas import tpu as pltpu

F32 = jnp.float32
BF16 = jnp.bfloat16
MESH = pl.DeviceIdType.MESH
ANY = pl.BlockSpec(memory_space=pl.ANY)

NORM_EPS = 1e-6
ADAM_LR = 0.001
ADAM_B1 = 0.9
ADAM_B2 = 0.999
ADAM_EPS = 1e-08
ADAM_WD = 0.01
ADAM_STEP = 10
CONV_WIDTH = 3
HEAD = 128
HG_CHUNK = 64
SG_CHUNK = 128
N_DEV = 8
N_CHIP = 4
V7X_VMEM_LIMIT = 56 * 1024 * 1024


def _cp(*sem):
    return pltpu.CompilerParams(dimension_semantics=sem if sem else None, vmem_limit_bytes=V7X_VMEM_LIMIT)


def _pick(n, prefs):
    for p in prefs:
        if p <= n and n % p == 0:
            return p
    return n


def _iota(shape, axis):
    return lax.broadcasted_iota(jnp.int32, shape, axis)


def _rows_within(R, row_bytes, budget):
    if R * row_bytes <= budget:
        return R
    for t in (1024, 512, 256, 128, 64, 32, 16):
        if R % t == 0 and t * row_bytes <= budget:
            return t
    return _pick(R, (16, 8))


def _pack_rows(arrays):
    flat = jnp.concatenate([a.reshape(-1) for a in arrays])
    pad = (-flat.size) % (8 * HEAD)
    return jnp.pad(flat, (0, pad)).reshape(-1, HEAD)


def _dg(a, b, ca, cb):
    return lax.dot_general(a.astype(BF16), b.astype(BF16), (((ca,), (cb,)), ((), ())), preferred_element_type=F32)


@jax.custom_vjp
def mm_nn(a, b):
    return _dg(a, b, 1, 0)


mm_nn.defvjp(lambda a, b: (_dg(a, b, 1, 0), (a, b)),
             lambda r, g: (_dg(g, r[1], 1, 1), _dg(r[0], g, 0, 0)))


@jax.custom_vjp
def mm_nt(a, b):
    return _dg(a, b, 1, 1)


mm_nt.defvjp(lambda a, b: (_dg(a, b, 1, 1), (a, b)),
             lambda r, g: (_dg(g, r[1], 1, 0), _dg(g, r[0], 0, 0)))


@jax.custom_vjp
def mm_tn(a, b):
    return _dg(a, b, 0, 0)


mm_tn.defvjp(lambda a, b: (_dg(a, b, 0, 0), (a, b)),
             lambda r, g: (_dg(r[1], g, 1, 1), _dg(r[0], g, 1, 0)))


def _split(x):
    hi = x.astype(BF16)
    lo = (x - hi.astype(F32)).astype(BF16)
    return hi, lo


def _sum_right(x, m01):
    hi, lo = _split(x)
    return _dg(hi, m01, 1, 0) + _dg(lo, m01, 1, 0)


def _sum_left_impl(m01, x, ca):
    hi, lo = _split(x)
    return _dg(m01, hi, ca, 0) + _dg(m01, lo, ca, 0)


@jax.custom_vjp
def _sum_left(m01, x):
    return _sum_left_impl(m01, x, 1)


_sum_left.defvjp(lambda m, x: (_sum_left_impl(m, x, 1), m),
                 lambda m, g: (None, _sum_left_impl(m, g, 0)))


def _sigmoid(x):
    return 1.0 / (1.0 + jnp.exp(-x))


def _softplus(z):
    return jnp.maximum(z, 0.0) + jnp.log(1.0 + jnp.exp(-jnp.abs(z)))


_INV_SQRT2 = 1.0 / math.sqrt(2.0)
_INV_SQRT2PI = 1.0 / math.sqrt(2.0 * math.pi)


@jax.custom_vjp
def _gelu(x):
    return 0.5 * x * (1.0 + lax.erf(x * _INV_SQRT2))


_gelu.defvjp(lambda x: (0.5 * x * (1.0 + lax.erf(x * _INV_SQRT2)), x),
             lambda x, g: (g * (0.5 * (1.0 + lax.erf(x * _INV_SQRT2)) + x * jnp.exp(-0.5 * x * x) * _INV_SQRT2PI),))


def _rms(x, gain):
    r = lax.rsqrt(jnp.mean(x * x, axis=-1, keepdims=True) + NORM_EPS)
    return x * r * gain


def _normmod(x, gain, sc, sh):
    return _rms(x, gain) * (1.0 + sc) + sh


def _mm_call(name, a, b, out_shape, out_dtype, dims, grid, a_spec, b_spec, o_spec, acc_shape):
    nk = grid[2]

    def body(a_ref, b_ref, o_ref, *scratch):
        part = lax.dot_general(a_ref[...].astype(BF16), b_ref[...].astype(BF16), dims, preferred_element_type=F32)
        if nk == 1:
            o_ref[...] = part.astype(o_ref.dtype)
            return
        acc_ref, = scratch
        k = pl.program_id(2)

        @pl.when(k == 0)
        def _():
            acc_ref[...] = part

        @pl.when(k > 0)
        def _():
            acc_ref[...] += part

        @pl.when(k == nk - 1)
        def _():
            o_ref[...] = acc_ref[...].astype(o_ref.dtype)

    return pl.pallas_call(
        body, name=name, grid=grid, in_specs=[a_spec, b_spec], out_specs=o_spec,
        out_shape=jax.ShapeDtypeStruct(out_shape, out_dtype),
        scratch_shapes=[] if nk == 1 else [pltpu.VMEM(acc_shape, F32)],
        compiler_params=_cp("parallel", "parallel", "arbitrary"),
    )(a, b)


def _mm_nn(name, a, b, out_dtype=F32):
    M, K = a.shape
    chunked = b.ndim == 3
    Nc = b.shape[-1]
    N = Nc * (b.shape[0] if chunked else 1)
    tm = _pick(M, (1024, 512, 256, 128, 64, 32, 16, 8))
    tn = _pick(Nc, (1408, 1024, 896, 512, 256, 128))
    tk = _pick(K, (2048, 1408, 1024, 512, 256, 128))
    npc = Nc // tn
    if chunked:
        b_spec = pl.BlockSpec((None, tk, tn), lambda i, j, k: (j // npc, k, j % npc))
    else:
        b_spec = pl.BlockSpec((tk, tn), lambda i, j, k: (k, j))
    return _mm_call(name, a, b, (M, N), out_dtype, (((1,), (0,)), ((), ())), (M // tm, N // tn, K // tk),
                    pl.BlockSpec((tm, tk), lambda i, j, k: (i, k)), b_spec,
                    pl.BlockSpec((tm, tn), lambda i, j, k: (i, j)), (tm, tn))


def _mm_nt(name, a, b, out_dtype=F32):
    M, N = a.shape
    chunked = b.ndim == 3
    Nc = b.shape[-1]
    K = b.shape[-2]
    tm = _pick(M, (1024, 512, 256, 128, 64, 32, 16, 8))
    tn = _pick(K, (1408, 1024, 512, 256, 128))
    tk = _pick(Nc, (2048, 1792, 1408, 1024, 896, 512, 256, 128))
    npc = Nc // tk
    if chunked:
        b_spec = pl.BlockSpec((None, tn, tk), lambda i, j, k: (k // npc, j, k % npc))
    else:
        b_spec = pl.BlockSpec((tn, tk), lambda i, j, k: (j, k))
    return _mm_call(name, a, b, (M, K), out_dtype, (((1,), (1,)), ((), ())), (M // tm, K // tn, N // tk),
                    pl.BlockSpec((tm, tk), lambda i, j, k: (i, k)), b_spec,
                    pl.BlockSpec((tm, tn), lambda i, j, k: (i, j)), (tm, tn))


def _mm_tn(name, a, b, chunks=1, out_dtype=BF16):
    T, K = a.shape
    N = b.shape[1]
    Nc = N // chunks
    tm = _pick(K, (1408, 1024, 512, 256, 128))
    tn = _pick(Nc, (1408, 1024, 896, 512, 256, 128))
    tk = _pick(T, (1024, 512, 256, 128))
    npc = Nc // tn
    if chunks > 1:
        shape = (chunks, K, Nc)
        o_spec = pl.BlockSpec((None, tm, tn), lambda i, j, k: (j // npc, i, j % npc))
    else:
        shape = (K, N)
        o_spec = pl.BlockSpec((tm, tn), lambda i, j, k: (i, j))
    return _mm_call(name, a, b, shape, out_dtype, (((0,), (0,)), ((), ())), (K // tm, N // tn, T // tk),
                    pl.BlockSpec((tk, tm), lambda i, j, k: (k, i)),
                    pl.BlockSpec((tk, tn), lambda i, j, k: (k, j)), o_spec, (tm, tn))


def _row_tile(T):
    return _pick(T, (256, 128, 64, 32, 16, 8))


def _vec_spec(D):
    return pl.BlockSpec((1, D), lambda i: (0, 0))


def _normmod_fwd(name, x, gain, sc, sh):
    T, D = x.shape
    bt = _row_tile(T)

    def body(x_ref, g_ref, sc_ref, sh_ref, h_ref):
        h_ref[...] = _normmod(x_ref[...], g_ref[...], sc_ref[...], sh_ref[...]).astype(h_ref.dtype)

    rows = pl.BlockSpec((bt, D), lambda i: (i, 0))
    return pl.pallas_call(body, name=name, grid=(T // bt,), in_specs=[rows] + [_vec_spec(D)] * 3, out_specs=rows,
                          out_shape=jax.ShapeDtypeStruct((T, D), BF16), compiler_params=_cp("parallel"))(x, gain, sc, sh)


def _res_normmod_fwd(name, x, y, g, gain, sc, sh):
    T, D = x.shape
    bt = _row_tile(T)

    def body(x_ref, y_ref, gate_ref, g_ref, sc_ref, sh_ref, x1_ref, h_ref):
        x1 = x_ref[...] + gate_ref[...] * y_ref[...]
        x1_ref[...] = x1
        h_ref[...] = _normmod(x1, g_ref[...], sc_ref[...], sh_ref[...]).astype(h_ref.dtype)

    rows = pl.BlockSpec((bt, D), lambda i: (i, 0))
    return pl.pallas_call(body, name=name, grid=(T // bt,), in_specs=[rows, rows] + [_vec_spec(D)] * 4,
                          out_specs=[rows, rows],
                          out_shape=[jax.ShapeDtypeStruct((T, D), F32), jax.ShapeDtypeStruct((T, D), BF16)],
                          compiler_params=_cp("parallel"))(x, y, g, gain, sc, sh)


def _final_fwd_bwd(x, y, g, gain, target):
    T, D = x.shape
    bt = _row_tile(T)

    def body(x_ref, y_ref, gate_ref, g_ref, t_ref, loss_ref, dx_ref, dy_ref, dgate_ref, dgain_ref):
        i = pl.program_id(0)
        yv = y_ref[...]
        gate = gate_ref[...]
        x4 = x_ref[...] + gate * yv
        out, vjp = jax.vjp(_rms, x4, g_ref[...])
        err = out - t_ref[...]
        dx4, dgain = vjp(err * (1.0 / D))
        part = 0.5 * jnp.sum(jnp.mean(err * err, axis=-1, keepdims=True), axis=0, keepdims=True)

        @pl.when(i == 0)
        def _():
            loss_ref[...] = jnp.zeros_like(loss_ref)
            dgate_ref[...] = jnp.zeros_like(dgate_ref)
            dgain_ref[...] = jnp.zeros_like(dgain_ref)

        loss_ref[...] += jnp.broadcast_to(part, loss_ref.shape)
        dx_ref[...] = dx4
        dy_ref[...] = (gate * dx4).astype(dy_ref.dtype)
        dgate_ref[...] += jnp.sum(dx4 * yv, axis=0, keepdims=True)
        dgain_ref[...] += dgain

    rows = pl.BlockSpec((bt, D), lambda i: (i, 0))
    vec = _vec_spec(D)
    return pl.pallas_call(
        body, name="final_loss", grid=(T // bt,), in_specs=[rows, rows, vec, vec, rows],
        out_specs=[pl.BlockSpec((1, HEAD), lambda i: (0, 0)), rows, rows, vec, vec],
        out_shape=[jax.ShapeDtypeStruct((1, HEAD), F32), jax.ShapeDtypeStruct((T, D), F32),
                   jax.ShapeDtypeStruct((T, D), BF16), jax.ShapeDtypeStruct((1, D), F32),
                   jax.ShapeDtypeStruct((1, D), F32)],
        compiler_params=_cp("arbitrary"))(x, y, g, gain, target)


def _block_bwd(name, dx_out, dh, x_in, gain, sc, sh, y_prev=None, g_prev=None):
    T, D = x_in.shape
    bt = _row_tile(T)
    has_prev = y_prev is not None

    def body(*refs):
        if has_prev:
            dxo_ref, dh_ref, x_ref, g_ref, sc_ref, sh_ref, y_ref, gp_ref, dx_ref, dgain_ref, dsc_ref, dsh_ref, dy_ref, dgp_ref = refs
        else:
            dxo_ref, dh_ref, x_ref, g_ref, sc_ref, sh_ref, dx_ref, dgain_ref, dsc_ref, dsh_ref = refs
        i = pl.program_id(0)
        _, vjp = jax.vjp(_normmod, x_ref[...], g_ref[...], sc_ref[...], sh_ref[...])
        dxn, dgain, dsc, dsh = vjp(dh_ref[...])
        dx = dxo_ref[...] + dxn
        dx_ref[...] = dx

        @pl.when(i == 0)
        def _():
            dgain_ref[...] = jnp.zeros_like(dgain_ref)
            dsc_ref[...] = jnp.zeros_like(dsc_ref)
            dsh_ref[...] = jnp.zeros_like(dsh_ref)
            if has_prev:
                dgp_ref[...] = jnp.zeros_like(dgp_ref)

        dgain_ref[...] += dgain
        dsc_ref[...] += dsc
        dsh_ref[...] += dsh
        if has_prev:
            dy_ref[...] = (gp_ref[...] * dx).astype(dy_ref.dtype)
            dgp_ref[...] += jnp.sum(dx * y_ref[...], axis=0, keepdims=True)

    rows = pl.BlockSpec((bt, D), lambda i: (i, 0))
    vec = _vec_spec(D)
    ins = [dx_out, dh, x_in, gain, sc, sh]
    in_specs = [rows, rows, rows, vec, vec, vec]
    out_specs = [rows, vec, vec, vec]
    out_shape = [jax.ShapeDtypeStruct((T, D), F32)] + [jax.ShapeDtypeStruct((1, D), F32)] * 3
    if has_prev:
        ins += [y_prev, g_prev]
        in_specs += [rows, vec]
        out_specs += [rows, vec]
        out_shape += [jax.ShapeDtypeStruct((T, D), BF16), jax.ShapeDtypeStruct((1, D), F32)]
    return pl.pallas_call(body, name=name, grid=(T // bt,), in_specs=in_specs, out_specs=out_specs,
                          out_shape=out_shape, compiler_params=_cp("arbitrary"))(*ins)


def _sb_tiles(T):
    tq = _pick(T, (512, 256, 128))
    return tq, tq // HEAD


def _sb_fwd(proj, H):
    T = proj.shape[0]
    tq, nsub = _sb_tiles(T)
    scale = HEAD ** -0.5

    def body(q_ref, k_ref, v_ref, o_ref, l_ref, acc_ref):
        i = pl.program_id(1)
        q = q_ref[...].astype(BF16)
        later = (_iota((HEAD, HEAD), 0) > _iota((HEAD, HEAD), 1)).astype(BF16)
        row = _iota((tq, HEAD), 0)
        col = _iota((tq, HEAD), 1)

        def key_step(j, c, diagonal):
            off = pl.multiple_of(j * tq, tq)
            k = k_ref[pl.ds(off, tq), :].astype(BF16)
            v = v_ref[pl.ds(off, tq), :].astype(BF16)
            z = _dg(q, k, 1, 1) * scale
            ws = [None] * nsub
            for s in reversed(range(nsub)):
                zs = z[:, s * HEAD:(s + 1) * HEAD]
                sp = _softplus(zs)
                if diagonal:
                    strict = (s * HEAD + col) < row
                    lk = jnp.where(strict, -sp, 0.0)
                else:
                    lk = -sp
                w = jnp.exp(zs - sp + _sum_right(lk, later) + c)
                if diagonal:
                    w = jnp.where(strict, w, 0.0)
                ws[s] = w.astype(BF16)
                c = c + jnp.sum(lk, axis=1, keepdims=True)
            acc_ref[...] += _dg(jnp.concatenate(ws, axis=1), v, 1, 0)
            return c

        acc_ref[...] = jnp.zeros_like(acc_ref)
        c = key_step(i, jnp.zeros((tq, 1), F32), True)
        c = lax.fori_loop(0, i, lambda n, c: key_step(i - 1 - n, c, False), c)
        o_ref[...] = acc_ref[...].astype(o_ref.dtype)
        l_ref[...] = jnp.broadcast_to(c, (tq, HEAD))

    blk = pl.BlockSpec((tq, HEAD), lambda h, i: (i, h))
    return pl.pallas_call(
        body, name="sb_fwd", grid=(H, T // tq),
        in_specs=[blk, pl.BlockSpec((T, HEAD), lambda h, i: (0, H + h)), pl.BlockSpec((T, HEAD), lambda h, i: (0, 2 * H + h))],
        out_specs=[blk, blk],
        out_shape=[jax.ShapeDtypeStruct((T, H * HEAD), BF16), jax.ShapeDtypeStruct((T, H * HEAD), F32)],
        scratch_shapes=[pltpu.VMEM((tq, HEAD), F32)],
        compiler_params=_cp("parallel", "arbitrary"))(proj, proj, proj)


def _sb_bwd(proj, do, L, H):
    T = proj.shape[0]
    tq, nsub = _sb_tiles(T)
    scale = HEAD ** -0.5

    def body(q_ref, k_ref, v_ref, do_ref, l_ref, dq_ref, dk_ref, dv_ref):
        i = pl.program_id(1)

        @pl.when(i == 0)
        def _():
            dk_ref[...] = jnp.zeros_like(dk_ref)
            dv_ref[...] = jnp.zeros_like(dv_ref)

        dq_ref[...] = jnp.zeros_like(dq_ref)
        q = q_ref[...].astype(BF16)
        do_ = do_ref[...].astype(BF16)
        total = l_ref[...]
        upto = (_iota((HEAD, HEAD), 0) <= _iota((HEAD, HEAD), 1)).astype(BF16)
        before = (_iota((HEAD, HEAD), 0) < _iota((HEAD, HEAD), 1)).astype(BF16)
        row = _iota((tq, HEAD), 0)
        col = _iota((tq, HEAD), 1)

        def key_step(j, carry, diagonal):
            cp, ce = carry
            off = pl.multiple_of(j * tq, tq)
            k = k_ref[pl.ds(off, tq), :].astype(BF16)
            v = v_ref[pl.ds(off, tq), :].astype(BF16)
            z = _dg(q, k, 1, 1) * scale
            dw = _dg(do_, v, 1, 1)
            ws, dzs = [], []
            for s in range(nsub):
                zs = z[:, s * HEAD:(s + 1) * HEAD]
                sp = _softplus(zs)
                if diagonal:
                    strict = (s * HEAD + col) < row
                    lk = jnp.where(strict, -sp, 0.0)
                else:
                    lk = -sp
                tail = total - (_sum_right(lk, upto) + cp)
                w = jnp.exp(zs - sp + tail)
                if diagonal:
                    w = jnp.where(strict, w, 0.0)
                e = w * dw[:, s * HEAD:(s + 1) * HEAD]
                e_before = _sum_right(e, before) + ce
                sig = jnp.exp(zs - sp)
                dz = (e * (1.0 - sig) - e_before * sig) * scale
                if diagonal:
                    dz = jnp.where(strict, dz, 0.0)
                ws.append(w.astype(BF16))
                dzs.append(dz.astype(BF16))
                cp = cp + jnp.sum(lk, axis=1, keepdims=True)
                ce = ce + jnp.sum(e, axis=1, keepdims=True)
            w_all = jnp.concatenate(ws, axis=1)
            dz_all = jnp.concatenate(dzs, axis=1)
            dv_ref[pl.ds(off, tq), :] += _dg(w_all, do_, 0, 0)
            dk_ref[pl.ds(off, tq), :] += _dg(dz_all, q, 0, 0)
            dq_ref[...] += _dg(dz_all, k, 1, 0)
            return cp, ce

        zero = jnp.zeros((tq, 1), F32)
        carry = lax.fori_loop(0, i, lambda j, cr: key_step(j, cr, False), (zero, zero))
        key_step(i, carry, True)

    blk = pl.BlockSpec((tq, HEAD), lambda h, i: (i, h))
    full = pl.BlockSpec((T, HEAD), lambda h, i: (0, h))
    shp = jax.ShapeDtypeStruct((T, H * HEAD), F32)
    return pl.pallas_call(
        body, name="sb_bwd", grid=(H, T // tq),
        in_specs=[blk, pl.BlockSpec((T, HEAD), lambda h, i: (0, H + h)), pl.BlockSpec((T, HEAD), lambda h, i: (0, 2 * H + h)),
                  blk, blk],
        out_specs=[blk, full, full], out_shape=[shp, shp, shp],
        compiler_params=_cp("parallel", "arbitrary"))(proj, proj, proj, do, L)


def _hg_tile(q, fl, iv, g, st, l0, l1, gain):
    R = 2 * HG_CHUNK
    row = _iota((R, R), 0)
    col = _iota((R, R), 1)
    first = row < HG_CHUNK
    same = first == (col < HG_CHUNK)
    tri = (row >= col) & same
    lb = _sigmoid(l0 - l1)
    f = lb + (1.0 - lb) * _sigmoid(fl)
    logf = jnp.log(f)
    k = 1.0 - f
    qf = q * _sigmoid(q)
    G = _sum_left(tri.astype(BF16), logf)
    gl_a = jnp.sum(jnp.where(first, logf, 0.0), axis=0, keepdims=True)
    gl_b = jnp.sum(jnp.where(first, 0.0, logf), axis=0, keepdims=True)
    q_dec = qf * jnp.exp(G)
    k_inv = k * jnp.exp(-G)
    k_end = k * jnp.exp(jnp.where(first, gl_a, gl_b) - G)
    scores = jnp.where(tri, mm_nt(q_dec, k_inv), 0.0)
    o = mm_nn(scores, iv)
    o_a = mm_nt(q_dec, st)
    st_mid = st * jnp.exp(gl_a) + mm_tn(jnp.where(first, iv, 0.0), k_end)
    o_b = mm_nt(q_dec, st_mid)
    st_new = st_mid * jnp.exp(gl_b) + mm_tn(jnp.where(first, 0.0, iv), k_end)
    o = o + jnp.where(first, o_a, o_b)
    on = o * lax.rsqrt(jnp.mean(o * o, axis=-1, keepdims=True) + NORM_EPS) * gain
    return on * (g * _sigmoid(g)), st_new


def _hg_specs(H, c0, rev, nt):
    def at(base):
        if rev:
            return pl.BlockSpec((HEAD, HEAD), lambda h, i: (nt - 1 - i, base + h))
        return pl.BlockSpec((HEAD, HEAD), lambda h, i: (i, base + h))
    return [at(c0), at(c0 + H), at(c0 + 2 * H), at(c0 + 3 * H)]


def _hg_fwd(proj, l0, l1, gain, H, c0):
    T = proj.shape[0]
    nt = T // HEAD

    def body(q_ref, f_ref, i_ref, g_ref, l0_ref, l1_ref, gain_ref, o_ref, st_out_ref, st_ref):
        @pl.when(pl.program_id(1) == 0)
        def _():
            st_ref[...] = jnp.zeros_like(st_ref)

        st = st_ref[...]
        st_out_ref[...] = st
        out, st_new = _hg_tile(q_ref[...], f_ref[...], i_ref[...], g_ref[...], st, l0_ref[...], l1_ref[...], gain_ref[...])
        o_ref[...] = out.astype(o_ref.dtype)
        st_ref[...] = st_new

    vec = pl.BlockSpec((1, HEAD), lambda h, i: (0, h))
    return pl.pallas_call(
        body, name="hg_fwd", grid=(H, nt), in_specs=_hg_specs(H, c0, False, nt) + [vec, vec, vec],
        out_specs=[pl.BlockSpec((HEAD, HEAD), lambda h, i: (i, h)),
                   pl.BlockSpec((None, None, HEAD, HEAD), lambda h, i: (h, i, 0, 0))],
        out_shape=[jax.ShapeDtypeStruct((T, H * HEAD), BF16), jax.ShapeDtypeStruct((H, nt, HEAD, HEAD), F32)],
        scratch_shapes=[pltpu.VMEM((HEAD, HEAD), F32)],
        compiler_params=_cp("parallel", "arbitrary"))(proj, proj, proj, proj, l0, l1, gain)


def _hg_bwd(proj, states, do, l0, l1, gain, H, c0, do_c0):
    T = proj.shape[0]
    nt = T // HEAD

    def body(q_ref, f_ref, i_ref, g_ref, st_in_ref, do_ref, l0_ref, l1_ref, gain_ref,
             dq_ref, df_ref, di_ref, dg_ref, dl0_ref, dl1_ref, dgain_ref, dst_ref):
        @pl.when(pl.program_id(1) == 0)
        def _():
            dst_ref[...] = jnp.zeros_like(dst_ref)
            dl0_ref[...] = jnp.zeros_like(dl0_ref)
            dl1_ref[...] = jnp.zeros_like(dl1_ref)
            dgain_ref[...] = jnp.zeros_like(dgain_ref)

        _, vjp = jax.vjp(_hg_tile, q_ref[...], f_ref[...], i_ref[...], g_ref[...], st_in_ref[...],
                         l0_ref[...], l1_ref[...], gain_ref[...])
        dq, df, di, dg, dst, dl0, dl1, dgain = vjp((do_ref[...], dst_ref[...]))
        dq_ref[...] = dq
        df_ref[...] = df
        di_ref[...] = di
        dg_ref[...] = dg
        dst_ref[...] = dst
        dl0_ref[...] += dl0
        dl1_ref[...] += dl1
        dgain_ref[...] += dgain

    vec = pl.BlockSpec((1, HEAD), lambda h, i: (0, h))
    rblk = pl.BlockSpec((HEAD, HEAD), lambda h, i: (nt - 1 - i, h))
    shp = jax.ShapeDtypeStruct((T, H * HEAD), F32)
    vshp = jax.ShapeDtypeStruct((1, H * HEAD), F32)
    return pl.pallas_call(
        body, name="hg_bwd", grid=(H, nt),
        in_specs=_hg_specs(H, c0, True, nt) + [
            pl.BlockSpec((None, None, HEAD, HEAD), lambda h, i: (h, nt - 1 - i, 0, 0)),
            pl.BlockSpec((HEAD, HEAD), lambda h, i: (nt - 1 - i, do_c0 + h)), vec, vec, vec],
        out_specs=[rblk, rblk, rblk, rblk, vec, vec, vec],
        out_shape=[shp, shp, shp, shp, vshp, vshp, vshp],
        scratch_shapes=[pltpu.VMEM((HEAD, HEAD), F32)],
        compiler_params=_cp("parallel", "arbitrary"))(proj, proj, proj, proj, states, do, l0, l1, gain)


def _sg_chunk(u_parts, v_parts, gains, biases, wpos, bpos):
    W = sum(p.shape[1] for p in v_parts)
    C = v_parts[0].shape[0]
    v = [_gelu(p) for p in v_parts]
    mu = sum(jnp.sum(p, axis=-1, keepdims=True) for p in v) * (1.0 / W)
    xc = [p - mu for p in v]
    r = lax.rsqrt(sum(jnp.sum(p * p, axis=-1, keepdims=True) for p in xc) * (1.0 / W) + NORM_EPS)
    causal = _iota((C, C), 0) >= _iota((C, C), 1)
    out = []
    for up, p, gn, bs, w, b in zip(u_parts, xc, gains, biases, wpos, bpos):
        vn = p * r * gn + bs
        mixed = mm_nn(jnp.where(causal, w, 0.0), vn) + b
        out.append(_gelu(up) * mixed)
    return out


def _sg_fwd(zpre, vgain, vbias, wpos, bpos):
    T, W2 = zpre.shape
    W = W2 // 2
    G = wpos.shape[0]
    cg = W // G
    C = SG_CHUNK

    def body(z_ref, gn_ref, bs_ref, w_ref, b_ref, s_ref):
        sl = [slice(g * cg, (g + 1) * cg) for g in range(G)]
        out = _sg_chunk([z_ref[:, s] for s in sl], [z_ref[:, W + s.start:W + s.stop] for s in sl],
                        [gn_ref[:, s] for s in sl], [bs_ref[:, s] for s in sl],
                        [w_ref[g] for g in range(G)], [b_ref[g] for g in range(G)])
        for s, o in zip(sl, out):
            s_ref[:, s] = o.astype(s_ref.dtype)

    return pl.pallas_call(
        body, name="sg_fwd", grid=(T // C,),
        in_specs=[pl.BlockSpec((C, W2), lambda i: (i, 0)), _vec_spec(W), _vec_spec(W),
                  pl.BlockSpec((G, C, C), lambda i: (0, 0, 0)), pl.BlockSpec((G, C, 1), lambda i: (0, 0, 0))],
        out_specs=pl.BlockSpec((C, W), lambda i: (i, 0)),
        out_shape=jax.ShapeDtypeStruct((T, W), BF16), compiler_params=_cp("parallel"))(zpre, vgain, vbias, wpos, bpos)


def _sg_bwd(zpre, ds, vgain, vbias, wpos, bpos):
    T, W2 = zpre.shape
    W = W2 // 2
    G = wpos.shape[0]
    cg = W // G
    C = SG_CHUNK

    def body(z_ref, ds_ref, gn_ref, bs_ref, w_ref, b_ref, dz_ref, dgn_ref, dbs_ref, dw_ref, db_ref):
        @pl.when(pl.program_id(0) == 0)
        def _():
            dgn_ref[...] = jnp.zeros_like(dgn_ref)
            dbs_ref[...] = jnp.zeros_like(dbs_ref)
            dw_ref[...] = jnp.zeros_like(dw_ref)
            db_ref[...] = jnp.zeros_like(db_ref)

        sl = [slice(g * cg, (g + 1) * cg) for g in range(G)]
        _, vjp = jax.vjp(_sg_chunk, [z_ref[:, s] for s in sl], [z_ref[:, W + s.start:W + s.stop] for s in sl],
                         [gn_ref[:, s] for s in sl], [bs_ref[:, s] for s in sl],
                         [w_ref[g] for g in range(G)], [b_ref[g] for g in range(G)])
        du, dv, dgn, dbs, dw, db = vjp([ds_ref[:, s] for s in sl])
        for g, s in enumerate(sl):
            dz_ref[:, s] = du[g].astype(dz_ref.dtype)
            dz_ref[:, W + s.start:W + s.stop] = dv[g].astype(dz_ref.dtype)
            dgn_ref[:, s] += dgn[g]
            dbs_ref[:, s] += dbs[g]
            dw_ref[g] += dw[g]
            db_ref[g] += db[g]

    wspec = pl.BlockSpec((G, C, C), lambda i: (0, 0, 0))
    bspec = pl.BlockSpec((G, C, 1), lambda i: (0, 0, 0))
    return pl.pallas_call(
        body, name="sg_bwd", grid=(T // C,),
        in_specs=[pl.BlockSpec((C, W2), lambda i: (i, 0)), pl.BlockSpec((C, W), lambda i: (i, 0)),
                  _vec_spec(W), _vec_spec(W), wspec, bspec],
        out_specs=[pl.BlockSpec((C, W2), lambda i: (i, 0)), _vec_spec(W), _vec_spec(W), wspec, bspec],
        out_shape=[jax.ShapeDtypeStruct((T, W2), BF16), jax.ShapeDtypeStruct((1, W), F32),
                   jax.ShapeDtypeStruct((1, W), F32), jax.ShapeDtypeStruct((G, C, C), F32),
                   jax.ShapeDtypeStruct((G, C, 1), F32)],
        compiler_params=_cp("arbitrary"))(zpre, ds, vgain, vbias, wpos, bpos)


def _conv_tiles(T, F):
    return _pick(T, (512, 256, 128, 64, 32, 16, 8)), _pick(F, (512, 256, 128))


def _shift_down(cur, prev8, n, first_tile):
    bt = cur.shape[0]
    r = pltpu.roll(cur, n, 0)
    p = pltpu.roll(prev8, n, 0)
    p = jnp.where(first_tile, 0.0, p)
    head = jnp.concatenate([p, r[8:]], axis=0) if bt > 8 else p
    return jnp.where(_iota(cur.shape, 0) < n, head, r)


def _shift_up(cur, next8, n, last_tile):
    bt = cur.shape[0]
    r = pltpu.roll(cur, bt - n, 0)
    p = pltpu.roll(next8, 8 - n, 0)
    p = jnp.where(last_tile, 0.0, p)
    tail = jnp.concatenate([r[:bt - 8], p], axis=0) if bt > 8 else p
    return jnp.where(_iota(cur.shape, 0) >= bt - n, tail, r)


def _conv_apply(cur, prev8, w_ref, b, first_tile):
    return (b + w_ref[0:1, :] * _shift_down(cur, prev8, 2, first_tile)
            + w_ref[1:2, :] * _shift_down(cur, prev8, 1, first_tile) + w_ref[2:3, :] * cur)


def _conv_fwd(name, a, w, b):
    T, F2 = a.shape
    F = F2 // 2
    bt, cw = _conv_tiles(T, F)
    nf = F // cw
    r8 = bt // 8

    def body(g_ref, gp_ref, v_ref, vp_ref, wg_ref, wv_ref, bg_ref, bv_ref, u_ref):
        first = pl.program_id(0) == 0
        gate = _conv_apply(g_ref[...], gp_ref[...], wg_ref, bg_ref[...], first)
        val = _conv_apply(v_ref[...], vp_ref[...], wv_ref, bv_ref[...], first)
        u_ref[...] = (gate * _sigmoid(gate) * val).astype(u_ref.dtype)

    def cur(off):
        return pl.BlockSpec((bt, cw), lambda i, j: (i, j + off))

    def prev(off):
        return pl.BlockSpec((8, cw), lambda i, j: (jnp.maximum(i * r8 - 1, 0), j + off))

    def vec(rows, off):
        return pl.BlockSpec((rows, cw), lambda i, j: (0, j + off))

    return pl.pallas_call(
        body, name=name, grid=(T // bt, nf),
        in_specs=[cur(0), prev(0), cur(nf), prev(nf), vec(3, 0), vec(3, nf), vec(1, 0), vec(1, nf)],
        out_specs=pl.BlockSpec((bt, cw), lambda i, j: (i, j)),
        out_shape=jax.ShapeDtypeStruct((T, F), BF16),
        compiler_params=_cp("parallel", "parallel"))(a, a, a, a, w, w, b, b)


def _conv_bwd_act(name, a, du, w, b):
    T, F2 = a.shape
    F = F2 // 2
    bt, cw = _conv_tiles(T, F)
    nf = F // cw
    r8 = bt // 8

    def body(s_ref, sp_ref, o_ref, op_ref, du_ref, ws_ref, wo_ref, bs_ref, bo_ref, da_ref, dw_ref, db_ref):
        i = pl.program_id(1)
        first = i == 0
        is_gate = pl.program_id(0) < nf
        cur = s_ref[...]
        prev8 = sp_ref[...]
        mine = _conv_apply(cur, prev8, ws_ref, bs_ref[...], first)
        other = _conv_apply(o_ref[...], op_ref[...], wo_ref, bo_ref[...], first)
        gate = jnp.where(is_gate, mine, other)
        val = jnp.where(is_gate, other, mine)
        sg = _sigmoid(gate)
        du_ = du_ref[...]
        d_gate = du_ * val * (sg * (1.0 + gate * (1.0 - sg)))
        d_val = du_ * gate * sg
        da = jnp.where(is_gate, d_gate, d_val)
        da_ref[...] = da

        @pl.when(first)
        def _():
            dw_ref[...] = jnp.zeros_like(dw_ref)
            db_ref[...] = jnp.zeros_like(db_ref)

        rows = [jnp.sum(da * _shift_down(cur, prev8, 2, first), axis=0, keepdims=True),
                jnp.sum(da * _shift_down(cur, prev8, 1, first), axis=0, keepdims=True),
                jnp.sum(da * cur, axis=0, keepdims=True)]
        for t in range(CONV_WIDTH):
            dw_ref[t:t + 1, :] += rows[t]
        db_ref[...] += jnp.sum(da, axis=0, keepdims=True)

    n2 = 2 * nf

    def cur(off):
        return pl.BlockSpec((bt, cw), lambda j, i: (i, (j + off) % n2))

    def prev(off):
        return pl.BlockSpec((8, cw), lambda j, i: (jnp.maximum(i * r8 - 1, 0), (j + off) % n2))

    def vec(rows, off):
        return pl.BlockSpec((rows, cw), lambda j, i: (0, (j + off) % n2))

    return pl.pallas_call(
        body, name=name, grid=(n2, T // bt),
        in_specs=[cur(0), prev(0), cur(nf), prev(nf), pl.BlockSpec((bt, cw), lambda j, i: (i, j % nf)),
                  vec(3, 0), vec(3, nf), vec(1, 0), vec(1, nf)],
        out_specs=[pl.BlockSpec((bt, cw), lambda j, i: (i, j)), vec(3, 0), vec(1, 0)],
        out_shape=[jax.ShapeDtypeStruct((T, F2), F32), jax.ShapeDtypeStruct((3, F2), F32),
                   jax.ShapeDtypeStruct((1, F2), F32)],
        compiler_params=_cp("parallel", "arbitrary"))(a, a, a, a, du, w, w, b, b)


def _conv_bwd_in(name, da_out, w):
    T, F2 = da_out.shape
    bt, cw = _conv_tiles(T, F2 // 2)
    r8 = bt // 8
    last_blk = T // 8 - 1

    def body(d_ref, dn_ref, w_ref, o_ref):
        last = pl.program_id(0) == pl.num_programs(0) - 1
        cur = d_ref[...]
        nxt = dn_ref[...]
        o_ref[...] = (w_ref[2:3, :] * cur + w_ref[1:2, :] * _shift_up(cur, nxt, 1, last)
                      + w_ref[0:1, :] * _shift_up(cur, nxt, 2, last)).astype(o_ref.dtype)

    return pl.pallas_call(
        body, name=name, grid=(T // bt, F2 // cw),
        in_specs=[pl.BlockSpec((bt, cw), lambda i, j: (i, j)),
                  pl.BlockSpec((8, cw), lambda i, j: (jnp.minimum((i + 1) * r8, last_blk), j)),
                  pl.BlockSpec((3, cw), lambda i, j: (0, j))],
        out_specs=pl.BlockSpec((bt, cw), lambda i, j: (i, j)),
        out_shape=jax.ShapeDtypeStruct((T, F2), BF16),
        compiler_params=_cp("parallel", "parallel"))(da_out, da_out, w)


def _ada_fwd(c_all, ada_w, ada_b):
    R, D = c_all.shape
    L, _, Ns = ada_w.shape
    tn = _pick(Ns, (512, 256, 128))

    def body(c_ref, w_ref, b_ref, o_ref):
        cv = c_ref[...]
        cond = cv * _sigmoid(cv)
        o_ref[...] = _dg(cond, w_ref[...], 1, 0) + b_ref[...]

    return pl.pallas_call(
        body, name="ada_fwd", grid=(L, Ns // tn),
        in_specs=[pl.BlockSpec((R, D), lambda l, j: (0, 0)), pl.BlockSpec((None, D, tn), lambda l, j: (l, 0, j)),
                  pl.BlockSpec((None, 1, tn), lambda l, j: (l, 0, j))],
        out_specs=pl.BlockSpec((None, R, tn), lambda l, j: (l, 0, j)),
        out_shape=jax.ShapeDtypeStruct((L, R, Ns), F32), compiler_params=_cp("parallel", "parallel"))(c_all, ada_w, ada_b)


def _adam_math(w, g, m, v):
    m2 = ADAM_B1 * m + (1.0 - ADAM_B1) * g
    v2 = ADAM_B2 * v + (1.0 - ADAM_B2) * (g * g)
    m_hat = m2 / (1.0 - ADAM_B1 ** ADAM_STEP)
    v_hat = v2 / (1.0 - ADAM_B2 ** ADAM_STEP)
    delta = -ADAM_LR * (m_hat / (jnp.sqrt(v_hat) + ADAM_EPS) + ADAM_WD * w)
    return delta, m2, v2


def _ada_grad_adam(c_all_t, dmod, w, m, v):
    D, R = c_all_t.shape
    L, _, Ns = dmod.shape
    tr = _rows_within(D, Ns * 4, 1 << 20)

    def body(c_ref, d_ref, w_ref, m_ref, v_ref, g_ref, dl_ref, m2_ref, v2_ref):
        cv = c_ref[...]
        g = _dg(cv * _sigmoid(cv), d_ref[...], 1, 0)
        g_ref[...] = g
        dl_ref[...], m2_ref[...], v2_ref[...] = _adam_math(w_ref[...], g, m_ref[...], v_ref[...])

    big = pl.BlockSpec((None, tr, Ns), lambda l, i: (l, i, 0))
    shp = jax.ShapeDtypeStruct((L, D, Ns), F32)
    return pl.pallas_call(
        body, name="ada_grad_adam", grid=(L, D // tr),
        in_specs=[pl.BlockSpec((tr, R), lambda l, i: (i, 0)), pl.BlockSpec((None, R, Ns), lambda l, i: (l, 0, 0)), big, big, big],
        out_specs=[big] * 4, out_shape=[shp] * 4, compiler_params=_cp("parallel", "parallel"))(c_all_t, dmod, w, m, v)


def _adam(name, w, g, m, v):
    R, C = w.shape
    tr = _rows_within(R, C * 4, 1 << 21)

    def body(w_ref, g_ref, m_ref, v_ref, dl_ref, m2_ref, v2_ref):
        dl_ref[...], m2_ref[...], v2_ref[...] = _adam_math(w_ref[...], g_ref[...], m_ref[...], v_ref[...])

    blk = pl.BlockSpec((tr, C), lambda i: (i, 0))
    shp = jax.ShapeDtypeStruct((R, C), F32)
    return pl.pallas_call(body, name=name, grid=(R // tr,), in_specs=[blk] * 4, out_specs=[blk] * 3,
                          out_shape=[shp] * 3, compiler_params=_cp("parallel"))(w, g, m, v)


def _cast_into_rows(name, w, chip):
    _, R, C = w.shape
    tr = _rows_within(R, C * 4, 1 << 22)

    def body(chip_ref, w_ref, o_ref):
        o_ref[...] = w_ref[...].astype(BF16)

    grid_spec = pltpu.PrefetchScalarGridSpec(
        num_scalar_prefetch=1, grid=(2, R // tr),
        in_specs=[pl.BlockSpec((None, tr, C), lambda h, i, s: (h, i, 0))],
        out_specs=pl.BlockSpec((None, tr, C), lambda h, i, s: (2 * s[0] + h, i, 0)))
    return pl.pallas_call(body, name=name, grid_spec=grid_spec, out_shape=jax.ShapeDtypeStruct((N_DEV, R, C), BF16),
                          compiler_params=_cp("arbitrary", "arbitrary"))(chip.reshape(1).astype(jnp.int32), w)


def _add_pairs(name, a, b):
    _, R, C = a.shape
    tr = _rows_within(R, C * 2, 1 << 21)

    def body(a_ref, b_ref, o_ref):
        o_ref[...] = (a_ref[...].astype(F32) + b_ref[...].astype(F32)).astype(o_ref.dtype)

    blk = pl.BlockSpec((None, tr, C), lambda j, i: (j, i, 0))
    return pl.pallas_call(body, name=name, grid=(4, R // tr), in_specs=[blk, blk], out_specs=blk,
                          out_shape=jax.ShapeDtypeStruct(a.shape, BF16), compiler_params=_cp("parallel", "parallel"))(a, b)


def _sum_into_pair(name, own, landed, slot, chip):
    n, R, C = landed.shape
    tr = _rows_within(R, (n + 1) * C * landed.dtype.itemsize, 1 << 23)

    def body(idx_ref, own_ref, x_ref, o_ref):
        mine = idx_ref[1]
        acc = None
        for j in range(n):
            part = jnp.where(mine == j, own_ref[...], x_ref[j]).astype(F32)
            acc = part if acc is None else acc + part
        o_ref[...] = acc

    grid_spec = pltpu.PrefetchScalarGridSpec(
        num_scalar_prefetch=1, grid=(R // tr,),
        in_specs=[pl.BlockSpec((None, tr, C), lambda i, s: (s[1], i, 0)), pl.BlockSpec((n, tr, C), lambda i, s: (0, i, 0))],
        out_specs=pl.BlockSpec((None, tr, C), lambda i, s: (s[0], i, 0)))
    idx = jnp.stack([slot, chip]).astype(jnp.int32)
    return pl.pallas_call(body, name=name, grid_spec=grid_spec, out_shape=jax.ShapeDtypeStruct((2, R, C), F32),
                          compiler_params=_cp("arbitrary"))(idx, own, landed)


def _sum_leading(name, a, out_dtype=F32):
    n, R, C = a.shape
    tr = _rows_within(R, n * C * a.dtype.itemsize, 1 << 23)

    def body(a_ref, o_ref):
        acc = a_ref[0].astype(F32)
        for j in range(1, n):
            acc = acc + a_ref[j].astype(F32)
        o_ref[...] = acc.astype(o_ref.dtype)

    return pl.pallas_call(body, name=name, grid=(R // tr,), in_specs=[pl.BlockSpec((n, tr, C), lambda i: (0, i, 0))],
                          out_specs=pl.BlockSpec((tr, C), lambda i: (i, 0)),
                          out_shape=jax.ShapeDtypeStruct((R, C), out_dtype), compiler_params=_cp("parallel"))(a)


def _place():
    return lax.axis_index("x"), lax.axis_index("y"), lax.axis_index("c")


def _all_gather(name, blocks, halves=False):
    n = len(blocks)
    shapes = [b.shape[1:] if halves else b.shape for b in blocks]

    def body(*refs):
        ins, outs = refs[:n], refs[n:2 * n]
        send_sems, recv_sems, local_sems = refs[2 * n:]
        x, y, c = _place()
        me, sibling = (x, y, c), (x, y, 1 - c)
        chips = [(1 - x, y), (x, 1 - y), (1 - x, 1 - y)]

        def rows(a, px, py, pc):
            return outs[a].at[4 * px + 2 * py + pc]

        def copy(a, k, block, to, src=None):
            return pltpu.make_async_remote_copy(
                src_ref=rows(a, *block) if src is None else src, dst_ref=rows(a, *block),
                send_sem=send_sems.at[7 * a + k], recv_sem=recv_sems.at[7 * a + k],
                device_id=to, device_id_type=MESH)

        started = []
        mine = []
        for a in range(n):
            src = ins[a].at[c] if halves else ins[a]
            mine.append(pltpu.make_async_copy(src, rows(a, *me), local_sems.at[a]))
            mine[-1].start()
            first = [copy(a, 0, me, sibling, src=src)]
            first += [copy(a, 1 + j, me, (*chip, c), src=src) for j, chip in enumerate(chips)]
            for cp in first:
                cp.start()
            started += first
        for j, chip in enumerate(chips):
            for a in range(n):
                copy(a, 1 + j, (*chip, c), me).wait_recv()
                passed = copy(a, 4 + j, (*chip, c), sibling)
                passed.start()
                started.append(passed)
        for a in range(n):
            copy(a, 0, sibling, me).wait_recv()
            for j, chip in enumerate(chips):
                copy(a, 4 + j, (*chip, 1 - c), me).wait_recv()
        for cp in started:
            cp.wait_send()
        for cp in mine:
            cp.wait()

    return pl.pallas_call(
        body, name=name, in_specs=[ANY] * n, out_specs=[ANY] * n,
        out_shape=[jax.ShapeDtypeStruct((N_DEV,) + tuple(s), b.dtype) for s, b in zip(shapes, blocks)],
        scratch_shapes=[pltpu.SemaphoreType.DMA((7 * n,)), pltpu.SemaphoreType.DMA((7 * n,)),
                        pltpu.SemaphoreType.DMA((n,))],
    )(*blocks)


def _to_sibling(name, arrays, pick):
    n = len(arrays)
    per = 4 if pick == "other_half" else 1

    def body(*refs):
        ins, outs = refs[:n], refs[n:2 * n]
        send_sems, recv_sems, local_sems = refs[2 * n:]
        x, y, c = _place()
        sibling = (x, y, 1 - c)
        started = []
        local = []
        for a in range(n):
            for j in range(per):
                if pick == "other_half":
                    src, dst = ins[a].at[2 * j + (1 - c)], outs[a].at[j]
                else:
                    src, dst = ins[a], outs[a].at[c]
                    local.append(pltpu.make_async_copy(src, dst, local_sems.at[a]))
                    local[-1].start()
                cp = pltpu.make_async_remote_copy(src_ref=src, dst_ref=dst, send_sem=send_sems.at[per * a + j],
                                                  recv_sem=recv_sems.at[per * a + j], device_id=sibling, device_id_type=MESH)
                cp.start()
                started.append(cp)
        for cp in started:
            cp.wait()
        for cp in local:
            cp.wait()

    if pick == "other_half":
        out_shape = [jax.ShapeDtypeStruct((4,) + a.shape[1:], a.dtype) for a in arrays]
    else:
        out_shape = [jax.ShapeDtypeStruct((2,) + a.shape, a.dtype) for a in arrays]
    return pl.pallas_call(
        body, name=name, in_specs=[ANY] * n, out_specs=[ANY] * n, out_shape=out_shape,
        scratch_shapes=[pltpu.SemaphoreType.DMA((per * n,)), pltpu.SemaphoreType.DMA((per * n,)),
                        pltpu.SemaphoreType.DMA((n,))],
    )(*arrays)


def _share_halves(name, arrays):
    n = len(arrays)

    def body(*refs):
        ins, outs = refs[:n], refs[n:2 * n]
        send_sems, recv_sems = refs[2 * n:]
        x, y, c = _place()
        started = []
        for a in range(n):
            cp = pltpu.make_async_remote_copy(src_ref=ins[a].at[c], dst_ref=outs[a].at[c], send_sem=send_sems.at[a],
                                              recv_sem=recv_sems.at[a], device_id=(x, y, 1 - c), device_id_type=MESH)
            cp.start()
            started.append(cp)
        for a in range(n):
            started[a].wait_send()
            pltpu.make_async_remote_copy(src_ref=ins[a].at[1 - c], dst_ref=outs[a].at[1 - c], send_sem=send_sems.at[a],
                                         recv_sem=recv_sems.at[a], device_id=(x, y, 1 - c), device_id_type=MESH).wait_recv()

    return pl.pallas_call(
        body, name=name, in_specs=[ANY] * n, out_specs=[ANY] * n,
        out_shape=[jax.ShapeDtypeStruct(a.shape, a.dtype) for a in arrays],
        input_output_aliases={a: a for a in range(n)},
        scratch_shapes=[pltpu.SemaphoreType.DMA((n,)), pltpu.SemaphoreType.DMA((n,))],
    )(*arrays)


HBM = pl.BlockSpec(memory_space=pltpu.HBM)
SEM = pl.BlockSpec(memory_space=pltpu.SEMAPHORE)
EFFECT = pltpu.SideEffectType.DATAFLOW_SIDE_EFFECTING


def _chip_copies(kind, srcs, dsts, send_sems, recv_sems):
    x, y, c = _place()
    mine = 2 * x + y
    sends, arrivals = [], []
    for a in range(len(srcs)):
        for k, (px, py) in enumerate([(1 - x, y), (x, 1 - y), (1 - x, 1 - y)]):
            other = 2 * px + py
            if kind == "rows":
                src, dst, lands = srcs[a].at[2 * mine + c], dsts[a].at[2 * mine + c], dsts[a].at[2 * other + c]
            else:
                src, dst, lands = srcs[a].at[other], dsts[a].at[mine], dsts[a].at[other]
            sem = dict(send_sem=send_sems.at[3 * a + k], recv_sem=recv_sems.at[3 * a + k], device_id=(px, py, c),
                       device_id_type=MESH)
            sends.append(pltpu.make_async_remote_copy(src_ref=src, dst_ref=dst, **sem))
            arrivals.append(pltpu.make_async_remote_copy(src_ref=src, dst_ref=lands, **sem))
    return sends, arrivals


def _chips_start(name, kind, srcs, dsts=None):
    n = len(srcs)
    bufs = list(srcs) + (list(dsts) if dsts is not None else [])
    nb = len(bufs)

    def body(*refs):
        ins = refs[:nb]
        send_sems, recv_sems = refs[nb], refs[nb + 1]
        token = refs[-1]
        sends, _ = _chip_copies(kind, ins[:n], ins[n:] if dsts is not None else ins[:n], send_sems, recv_sems)
        for cp in sends:
            cp.start()
        token[...] = jnp.zeros_like(token)

    out = pl.pallas_call(
        body, name=name,
        out_shape=(pltpu.SemaphoreType.DMA((3 * n,)), pltpu.SemaphoreType.DMA((3 * n,)),
                   *[pltpu.HBM(b.shape, b.dtype) for b in bufs], jax.ShapeDtypeStruct((8, HEAD), F32)),
        in_specs=(HBM,) * nb, out_specs=(SEM, SEM) + (HBM,) * nb + (pl.BlockSpec(memory_space=pltpu.VMEM),),
        input_output_aliases={i: 2 + i for i in range(nb)},
        compiler_params=pltpu.CompilerParams(has_side_effects=EFFECT),
    )(*[pltpu.with_memory_space_constraint(b, pltpu.HBM) for b in bufs])
    return out[0], out[1], list(out[2:2 + nb]), out[-1]


def _chips_wait(name, kind, n, send_sems, recv_sems, bufs, after):
    nb = len(bufs)

    def body(*refs):
        ins = refs[:nb]
        s_sems, r_sems = refs[nb], refs[nb + 1]
        sends, arrivals = _chip_copies(kind, ins[:n], ins[n:] if nb > n else ins[:n], s_sems, r_sems)
        for cp in sends:
            cp.wait_send()
        for cp in arrivals:
            cp.wait_recv()

    return list(pl.pallas_call(
        body, name=name, out_shape=tuple(pltpu.HBM(b.shape, b.dtype) for b in bufs),
        in_specs=(HBM,) * nb + (SEM, SEM, pl.BlockSpec(memory_space=pl.ANY)), out_specs=(HBM,) * nb,
        input_output_aliases={i: i for i in range(nb)},
        compiler_params=pltpu.CompilerParams(has_side_effects=EFFECT),
    )(*bufs, send_sems, recv_sems, after))


def _fill_from_sibling(name, arrays):
    n = len(arrays)

    def body(*refs):
        ins, outs = refs[:n], refs[n:2 * n]
        send_sems, recv_sems = refs[2 * n:]
        x, y, c = _place()
        sends, arrivals = [], []
        for a in range(n):
            for k, (px, py) in enumerate([(1 - x, y), (x, 1 - y), (1 - x, 1 - y)]):
                sem = dict(send_sem=send_sems.at[3 * a + k], recv_sem=recv_sems.at[3 * a + k], device_id=(x, y, 1 - c),
                           device_id_type=MESH)
                row = 2 * (2 * px + py)
                sends.append(pltpu.make_async_remote_copy(src_ref=ins[a].at[row + c], dst_ref=outs[a].at[row + c], **sem))
                arrivals.append(pltpu.make_async_remote_copy(src_ref=ins[a].at[row + c], dst_ref=outs[a].at[row + 1 - c], **sem))
        for cp in sends:
            cp.start()
        for cp in sends:
            cp.wait_send()
        for cp in arrivals:
            cp.wait_recv()

    return pl.pallas_call(
        body, name=name, in_specs=[ANY] * n, out_specs=[ANY] * n,
        out_shape=[jax.ShapeDtypeStruct(a.shape, a.dtype) for a in arrays],
        input_output_aliases={a: a for a in range(n)},
        scratch_shapes=[pltpu.SemaphoreType.DMA((3 * n,)), pltpu.SemaphoreType.DMA((3 * n,))],
    )(*arrays)


def kernel(x, c, ada_w, ada_b, mix_norm, ffn_norm, par_w_in, par_w_out, hg_lb_logits, hg_out_norm, sg_w_in, sg_v_gain, sg_v_bias, sg_w_pos, sg_b_pos, sg_w_out, ffn_up, ffn_conv_w, ffn_conv_b, ffn_down, final_norm, loss_target, m_ada_w, m_ada_b, m_mix_norm, m_ffn_norm, m_par_w_in, m_par_w_out, m_hg_lb_logits, m_hg_out_norm, m_sg_w_in, m_sg_v_gain, m_sg_v_bias, m_sg_w_pos, m_sg_b_pos, m_sg_w_out, m_ffn_up, m_ffn_conv_w, m_ffn_conv_b, m_ffn_down, m_final_norm, v_ada_w, v_ada_b, v_mix_norm, v_ffn_norm, v_par_w_in, v_par_w_out, v_hg_lb_logits, v_hg_out_norm, v_sg_w_in, v_sg_v_gain, v_sg_v_bias, v_sg_w_pos, v_sg_b_pos, v_sg_w_out, v_ffn_up, v_ffn_conv_w, v_ffn_conv_b, v_ffn_down, v_final_norm):
    names = ["ada_w", "ada_b", "mix_norm", "ffn_norm", "par_w_in", "par_w_out", "hg_lb_logits", "hg_out_norm", "sg_w_in",
             "sg_v_gain", "sg_v_bias", "sg_w_pos", "sg_b_pos", "sg_w_out", "ffn_up", "ffn_conv_w", "ffn_conv_b",
             "ffn_down", "final_norm"]
    W = dict(zip(names, [ada_w, ada_b, mix_norm, ffn_norm, par_w_in, par_w_out, hg_lb_logits, hg_out_norm, sg_w_in,
                         sg_v_gain, sg_v_bias, sg_w_pos, sg_b_pos, sg_w_out, ffn_up, ffn_conv_w, ffn_conv_b, ffn_down,
                         final_norm]))
    M = dict(zip(names, [m_ada_w, m_ada_b, m_mix_norm, m_ffn_norm, m_par_w_in, m_par_w_out, m_hg_lb_logits, m_hg_out_norm,
                         m_sg_w_in, m_sg_v_gain, m_sg_v_bias, m_sg_w_pos, m_sg_b_pos, m_sg_w_out, m_ffn_up, m_ffn_conv_w,
                         m_ffn_conv_b, m_ffn_down, m_final_norm]))
    V = dict(zip(names, [v_ada_w, v_ada_b, v_mix_norm, v_ffn_norm, v_par_w_in, v_par_w_out, v_hg_lb_logits, v_hg_out_norm,
                         v_sg_w_in, v_sg_v_gain, v_sg_v_bias, v_sg_w_pos, v_sg_b_pos, v_sg_w_out, v_ffn_up, v_ffn_conv_w,
                         v_ffn_conv_b, v_ffn_down, v_final_norm]))

    x = x[0]
    target = loss_target[0]
    T, D = x.shape
    ix, iy, ic = _place()
    chip = 2 * ix + iy
    dev = 2 * chip + ic
    H = hg_out_norm.shape[1]
    SBW = H * HEAD
    NA = ada_w.shape[2]
    F2s = ffn_up.shape[2]
    F2 = N_CHIP * F2s
    SGW = sg_w_out.shape[1] * N_CHIP
    G = sg_w_pos.shape[1]

    shards = [par_w_in[0], par_w_out[0], sg_w_in[0], sg_w_out[0], ffn_up[0], ffn_up[1], ffn_down[0], ffn_down[1]]
    kinds = ["col", "row", "col", "row", "col", "col", "row", "row"]
    rows8 = [_cast_into_rows("cast_w", w.reshape(2, w.shape[0] // 2, w.shape[1]), chip) for w in shards]
    groups = {"a": [0], "b": [1, 4, 6], "c": [2, 3, 5, 7]}
    started = {g: _chips_start("gather_start_" + g, "rows", [rows8[i] for i in idx]) for g, idx in groups.items()}
    start_token = sum(st[3][0, 0] for st in started.values())

    def weights_of(g, after):
        send_sems, recv_sems, bufs, _ = started[g]
        bufs = _chips_wait("gather_wait_" + g, "rows", len(bufs), send_sems, recv_sems, bufs, after)
        out = {}
        for i, g8 in zip(groups[g], _fill_from_sibling("gather_fill_" + g, bufs)):
            K, N = shards[i].shape
            out[i] = g8.reshape(N_CHIP, K, N) if kinds[i] == "col" else g8.reshape(N_CHIP * K, N)
        return out

    n_cw = ffn_conv_w.size
    n_sv = sg_v_gain.size
    c_all, small_all = _all_gather("gather_small", [c, _pack_rows([ffn_conv_w, sg_v_gain, sg_v_bias])])
    c_all = c_all.reshape(N_DEV, D)
    small_all = small_all.reshape(N_CHIP, 2, -1)[:, 0]
    conv_w_full = small_all[:, :n_cw].reshape(N_CHIP, 2, CONV_WIDTH, F2s).transpose(1, 2, 0, 3).reshape(2, CONV_WIDTH, F2)
    sg_gain_full = small_all[:, n_cw:n_cw + n_sv].reshape(1, SGW)
    sg_bias_full = small_all[:, n_cw + n_sv:n_cw + 2 * n_sv].reshape(1, SGW)

    c_pad = jnp.pad(c_all, ((0, 16 - N_DEV), (0, 0)))
    ada_b_sh = lax.dynamic_slice(ada_b, (0, chip * NA), (2, NA)).reshape(2, 1, NA)
    mod_sh = _ada_fwd(c_pad, ada_w, ada_b_sh)
    mod_all, = _all_gather("gather_mod", [mod_sh[:, :N_DEV]])
    mod_all = mod_all.reshape(N_CHIP, 2, 2, N_DEV, NA)[:, 0]
    mod = lax.dynamic_index_in_dim(mod_all, dev, axis=2, keepdims=False)
    mod = mod.transpose(1, 0, 2).reshape(2, 6, D)
    mods = [[mod[l, k].reshape(1, D) for k in range(6)] for l in range(2)]

    vec = lambda a: a.reshape(1, -1)
    l0 = vec(hg_lb_logits[0])
    l1 = vec(hg_lb_logits[1])
    hg_gain = vec(hg_out_norm[0])
    wpos = sg_w_pos[0]
    bpos = sg_b_pos[0].reshape(G, SG_CHUNK, 1)
    conv_b = [vec(ffn_conv_b[l]) for l in range(2)]

    sh1, sc1, g1, sh2, sc2, g2 = mods[0]
    w_in = weights_of("a", mod)[0]
    h0 = _normmod_fwd("norm_mix0", x, vec(mix_norm[0]) + start_token, sc1, sh1)
    proj = _mm_nn("mm_par_in", h0, w_in)
    o_sb, sb_tot = _sb_fwd(proj, H)
    o_hg, hg_states = _hg_fwd(proj, l0, l1, hg_gain, H, 3 * H)
    o_cat = jnp.concatenate([o_sb, o_hg], axis=1)
    wb = weights_of("b", o_cat)
    w_out, wup, wdn = wb[1], [wb[4], None], [wb[6], None]
    y0 = _mm_nn("mm_par_out", o_cat, w_out)
    x1, h0f = _res_normmod_fwd("res_norm_ffn0", x, y0, g1, vec(ffn_norm[0]), sc2, sh2)
    a0 = _mm_nn("mm_up0", h0f, wup[0])
    u0 = _conv_fwd("conv_fwd0", a0, conv_w_full[0], conv_b[0])
    f0 = _mm_nn("mm_down0", u0, wdn[0])
    sh1b, sc1b, g1b, sh2b, sc2b, g2b = mods[1]
    x2, h1 = _res_normmod_fwd("res_norm_mix1", x1, f0, g2, vec(mix_norm[1]), sc1b, sh1b)
    wc = weights_of("c", h1)
    wsg_in, wsg_out, wup[1], wdn[1] = wc[2], wc[3], wc[5], wc[7]
    zpre = _mm_nn("mm_sg_in", h1, wsg_in)
    s1 = _sg_fwd(zpre, sg_gain_full, sg_bias_full, wpos, bpos)
    y1 = _mm_nn("mm_sg_out", s1, wsg_out)
    x3, h1f = _res_normmod_fwd("res_norm_ffn1", x2, y1, g1b, vec(ffn_norm[1]), sc2b, sh2b)
    a1 = _mm_nn("mm_up1", h1f, wup[1])
    u1 = _conv_fwd("conv_fwd1", a1, conv_w_full[1], conv_b[1])
    f1 = _mm_nn("mm_down1", u1, wdn[1])
    loss_sum, dx, df1, dg2b, d_final = _final_fwd_bwd(x3, f1, g2b, vec(final_norm), target)
    loss = lax.psum(loss_sum[0, 0], ("x", "y", "c"))

    def reduce_start(tag, idx, grads):
        eights = [g.reshape((N_DEV, -1, g.shape[-1])) for g in grads]
        from_sib = _to_sibling("grads_to_sibling_" + tag, eights, "other_half")
        own = [lax.dynamic_index_in_dim(e.reshape((N_CHIP, 2) + e.shape[1:]), ic, axis=1, keepdims=False) for e in eights]
        pair = [_add_pairs("add_pair", o, r) for o, r in zip(own, from_sib)]
        landing = [lax.empty(p.shape, p.dtype) for p in pair]
        send_sems, recv_sems, bufs, token = _chips_start("grads_start_" + tag, "parts", pair, landing)
        return (tag, idx, send_sems, recv_sems, bufs), token[0:1, 0:1]

    def reduce_finish(state, after):
        tag, idx, send_sems, recv_sems, bufs = state
        n = len(idx)
        bufs = _chips_wait("grads_wait_" + tag, "parts", n, send_sems, recv_sems, bufs, after)
        halves = [_sum_into_pair("sum_chips", p, x_, ic, chip) for p, x_ in zip(bufs[:n], bufs[n:])]
        both = _share_halves("grads_share_" + tag, halves)
        return {i: b.reshape(shards[i].shape) for i, b in zip(idx, both)}

    def ffn_bwd(l, dfl, u, a, hf):
        g_dn = _mm_tn("mm_g_down", u, dfl)
        du = _mm_nt("mm_d_u", dfl, wdn[l])
        da_out, dcw, dcb = _conv_bwd_act("conv_bwd_act", a, du, conv_w_full[l], conv_b[l])
        da = _conv_bwd_in("conv_bwd_in", da_out, conv_w_full[l])
        g_up = _mm_tn("mm_g_up", hf, da, chunks=N_CHIP)
        dh = _mm_nt("mm_d_hf", da, wup[l])
        return g_dn, g_up, dcw, dcb, dh

    g_dn1, g_up1, dcw1, dcb1, dh1f = ffn_bwd(1, df1, u1, a1, h1f)
    red1, tok = reduce_start("1", [5, 7], [g_up1, g_dn1])
    dx, dgn_f1, dsc2b, dsh2b, dy1, dg1b = _block_bwd("bwd_ffn1", dx, dh1f, x3, vec(ffn_norm[1]) + tok, sc2b, sh2b, y1, g1b)
    g_sg_out = _mm_tn("mm_g_sg_out", s1, dy1)
    ds1 = _mm_nt("mm_d_s", dy1, wsg_out)
    dzpre, dsg_gain, dsg_bias, dwpos, dbpos = _sg_bwd(zpre, ds1, sg_gain_full, sg_bias_full, wpos, bpos)
    g_sg_in = _mm_tn("mm_g_sg_in", h1, dzpre, chunks=N_CHIP)
    dh1 = _mm_nt("mm_d_h1", dzpre, wsg_in)
    red2, tok = reduce_start("2", [2, 3], [g_sg_in, g_sg_out])
    dx, dgn_m1, dsc1b, dsh1b, df0, dg2 = _block_bwd("bwd_mix1", dx, dh1, x2, vec(mix_norm[1]) + tok, sc1b, sh1b, f0, g2)
    g_dn0, g_up0, dcw0, dcb0, dh0f = ffn_bwd(0, df0, u0, a0, h0f)
    red3, tok = reduce_start("3", [4, 6], [g_up0, g_dn0])
    dx, dgn_f0, dsc2, dsh2, dy0, dg1 = _block_bwd("bwd_ffn0", dx, dh0f, x1, vec(ffn_norm[0]) + tok, sc2, sh2, y0, g1)
    g_out = _mm_tn("mm_g_par_out", o_cat, dy0)
    do = _mm_nt("mm_d_o", dy0, w_out)
    dq, dk, dv = _sb_bwd(proj, do, sb_tot, H)
    dhq, dhf, dhi, dhg, dl0, dl1, dhg_gain = _hg_bwd(proj, hg_states, do, l0, l1, hg_gain, H, 3 * H, H)
    dproj = jnp.concatenate([dq, dk, dv, dhq, dhf, dhi, dhg], axis=1).astype(BF16)
    g_in = _mm_tn("mm_g_par_in", h0, dproj, chunks=N_CHIP)
    dh0 = _mm_nt("mm_d_h0", dproj, w_in)
    red4, tok = reduce_start("4", [0, 1], [g_in, g_out])
    grad_x, dgn_m0, dsc1, dsh1 = _block_bwd("bwd_mix0", dx, dh0, x, vec(mix_norm[0]) + tok, sc1, sh1)

    dmod = jnp.concatenate([dsh1, dsc1, dg1, dsh2, dsc2, dg2, dsh1b, dsc1b, dg1b, dsh2b, dsc2b, dg2b], axis=1)
    parts = [dmod, dgn_m0, dgn_m1, dgn_f0, dgn_f1, dl0, dl1, dhg_gain, dsg_gain, dsg_bias, dwpos, dbpos,
             dcw0, dcw1, dcb0, dcb1, d_final]
    sizes = [p.size for p in parts]
    packed = _pack_rows(parts)
    packed_all, = _all_gather("gather_small_grads", [packed])
    summed = _sum_leading("sum_small_grads", packed_all).reshape(-1)
    offs = [0]
    for s in sizes:
        offs.append(offs[-1] + s)
    red = [summed[offs[i]:offs[i + 1]] for i in range(len(parts))]
    (r_dmod, r_gm0, r_gm1, r_gf0, r_gf1, r_l0, r_l1, r_hgain, r_sgain, r_sbias, r_wpos, r_bpos,
     r_cw0, r_cw1, r_cb0, r_cb1, r_final) = red
    n_mod = sizes[0]
    dmod_all = packed_all.reshape(N_DEV, -1)[:, :n_mod].reshape(N_DEV, 2, 6 * D)

    G_ = {}
    G_["ada_b"] = r_dmod.reshape(2, 6 * D)
    G_["mix_norm"] = jnp.stack([r_gm0, r_gm1])
    G_["ffn_norm"] = jnp.stack([r_gf0, r_gf1])
    G_["hg_lb_logits"] = jnp.stack([r_l0, r_l1])
    G_["hg_out_norm"] = r_hgain.reshape(hg_out_norm.shape)
    G_["sg_v_gain"] = lax.dynamic_slice(r_sgain, (chip * n_sv,), (n_sv,)).reshape(sg_v_gain.shape)
    G_["sg_v_bias"] = lax.dynamic_slice(r_sbias, (chip * n_sv,), (n_sv,)).reshape(sg_v_bias.shape)
    G_["sg_w_pos"] = r_wpos.reshape(sg_w_pos.shape)
    G_["sg_b_pos"] = r_bpos.reshape(sg_b_pos.shape)
    cw_full = jnp.stack([r_cw0.reshape(CONV_WIDTH, F2), r_cw1.reshape(CONV_WIDTH, F2)])
    G_["ffn_conv_w"] = lax.dynamic_slice(cw_full, (0, 0, chip * F2s), (2, CONV_WIDTH, F2s))
    G_["ffn_conv_b"] = jnp.stack([r_cb0, r_cb1])
    G_["final_norm"] = r_final

    g_shards = {}
    for state in (red1, red2, red3, red4):
        g_shards.update(reduce_finish(state, summed))
    G_["par_w_in"] = g_shards[0][None]
    G_["par_w_out"] = g_shards[1][None]
    G_["sg_w_in"] = g_shards[2][None]
    G_["sg_w_out"] = g_shards[3][None]
    G_["ffn_up"] = jnp.stack([g_shards[4], g_shards[5]])
    G_["ffn_down"] = jnp.stack([g_shards[6], g_shards[7]])

    delta, new_m, new_v = {}, {}, {}
    c_t = jnp.pad(c_all, ((0, HEAD - N_DEV), (0, 0))).T
    dmod_sh = lax.dynamic_slice(dmod_all.transpose(1, 0, 2), (0, 0, chip * NA), (2, N_DEV, NA))
    dmod_sh = jnp.pad(dmod_sh, ((0, 0), (0, HEAD - N_DEV), (0, 0)))
    G_["ada_w"], delta["ada_w"], new_m["ada_w"], new_v["ada_w"] = _ada_grad_adam(c_t, dmod_sh, ada_w, m_ada_w, v_ada_w)
    for nme in ["par_w_in", "par_w_out", "sg_w_in", "sg_w_out", "ffn_up", "ffn_down"]:
        w = W[nme]
        shp = w.shape
        r2 = lambda a: a.reshape(-1, shp[-1])
        d_, m_, v_ = _adam("adam_" + nme, r2(w), r2(G_[nme]), r2(M[nme]), r2(V[nme]))
        delta[nme], new_m[nme], new_v[nme] = d_.reshape(shp), m_.reshape(shp), v_.reshape(shp)
    small = [n_ for n_ in names if n_ not in delta]
    pk = lambda dct: _pack_rows([dct[n_] for n_ in small])
    d_, m_, v_ = _adam("adam_small", pk(W), pk(G_), pk(M), pk(V))
    off = 0
    for n_ in small:
        sz = W[n_].size
        for dst, src in ((delta, d_), (new_m, m_), (new_v, v_)):
            dst[n_] = src.reshape(-1)[off:off + sz].reshape(W[n_].shape)
        off += sz

    return (loss, grad_x[None], *[G_[n_] for n_ in names], *[delta[n_] for n_ in names],
            *[new_m[n_] for n_ in names], *[new_v[n_] for n_ in names])
```

```python
import functools
import math

import jax
import jax.numpy as jnp
from jax import lax
from jax.experimental import pallas as pl
from jax.experimental.pallas import tpu as pltpu

F32 = jnp.float32
BF16 = jnp.bfloat16
MESH = pl.DeviceIdType.MESH
ANY = pl.BlockSpec(memory_space=pl.ANY)

NORM_EPS = 1e-6
ADAM_LR = 0.001
ADAM_B1 = 0.9
ADAM_B2 = 0.999
ADAM_EPS = 1e-08
ADAM_WD = 0.01
ADAM_STEP = 10
CONV_WIDTH = 3
HEAD = 128
HG_CHUNK = 64
SG_CHUNK = 128
N_DEV = 8
N_CHIP = 4
V7X_VMEM_LIMIT = 56 * 1024 * 1024


def _cp(*sem):
    return pltpu.CompilerParams(dimension_semantics=sem if sem else None, vmem_limit_bytes=V7X_VMEM_LIMIT)


def _pick(n, prefs):
    for p in prefs:
        if p <= n and n % p == 0:
            return p
    return n


def _iota(shape, axis):
    return lax.broadcasted_iota(jnp.int32, shape, axis)


def _rows_within(R, row_bytes, budget):
    if R * row_bytes <= budget:
        return R
    for t in (1024, 512, 256, 128, 64, 32, 16):
        if R % t == 0 and t * row_bytes <= budget:
            return t
    return _pick(R, (16, 8))


def _pack_rows(arrays):
    flat = jnp.concatenate([a.reshape(-1) for a in arrays])
    pad = (-flat.size) % (8 * HEAD)
    return jnp.pad(flat, (0, pad)).reshape(-1, HEAD)


def _dg(a, b, ca, cb):
    return lax.dot_general(a.astype(BF16), b.astype(BF16), (((ca,), (cb,)), ((), ())), preferred_element_type=F32)


@jax.custom_vjp
def mm_nn(a, b):
    return _dg(a, b, 1, 0)


mm_nn.defvjp(lambda a, b: (_dg(a, b, 1, 0), (a, b)),
             lambda r, g: (_dg(g, r[1], 1, 1), _dg(r[0], g, 0, 0)))


@jax.custom_vjp
def mm_nt(a, b):
    return _dg(a, b, 1, 1)


mm_nt.defvjp(lambda a, b: (_dg(a, b, 1, 1), (a, b)),
             lambda r, g: (_dg(g, r[1], 1, 0), _dg(g, r[0], 0, 0)))


@jax.custom_vjp
def mm_tn(a, b):
    return _dg(a, b, 0, 0)


mm_tn.defvjp(lambda a, b: (_dg(a, b, 0, 0), (a, b)),
             lambda r, g: (_dg(r[1], g, 1, 1), _dg(r[0], g, 1, 0)))


def _split(x):
    hi = x.astype(BF16)
    lo = (x - hi.astype(F32)).astype(BF16)
    return hi, lo


def _sum_right(x, m01):
    hi, lo = _split(x)
    return _dg(hi, m01, 1, 0) + _dg(lo, m01, 1, 0)


def _sum_left_impl(m01, x, ca):
    hi, lo = _split(x)
    return _dg(m01, hi, ca, 0) + _dg(m01, lo, ca, 0)


@jax.custom_vjp
def _sum_left(m01, x):
    return _sum_left_impl(m01, x, 1)


_sum_left.defvjp(lambda m, x: (_sum_left_impl(m, x, 1), m),
                 lambda m, g: (None, _sum_left_impl(m, g, 0)))


def _sigmoid(x):
    return 1.0 / (1.0 + jnp.exp(-x))


def _softplus(z):
    return jnp.maximum(z, 0.0) + jnp.log(1.0 + jnp.exp(-jnp.abs(z)))


_INV_SQRT2 = 1.0 / math.sqrt(2.0)
_INV_SQRT2PI = 1.0 / math.sqrt(2.0 * math.pi)


@jax.custom_vjp
def _gelu(x):
    return 0.5 * x * (1.0 + lax.erf(x * _INV_SQRT2))


_gelu.defvjp(lambda x: (0.5 * x * (1.0 + lax.erf(x * _INV_SQRT2)), x),
             lambda x, g: (g * (0.5 * (1.0 + lax.erf(x * _INV_SQRT2)) + x * jnp.exp(-0.5 * x * x) * _INV_SQRT2PI),))


def _rms(x, gain):
    r = lax.rsqrt(jnp.mean(x * x, axis=-1, keepdims=True) + NORM_EPS)
    return x * r * gain


def _normmod(x, gain, sc, sh):
    return _rms(x, gain) * (1.0 + sc) + sh


def _mm_call(name, a, b, out_shape, out_dtype, dims, grid, a_spec, b_spec, o_spec, acc_shape):
    nk = grid[2]

    def body(a_ref, b_ref, o_ref, *scratch):
        part = lax.dot_general(a_ref[...].astype(BF16), b_ref[...].astype(BF16), dims, preferred_element_type=F32)
        if nk == 1:
            o_ref[...] = part.astype(o_ref.dtype)
            return
        acc_ref, = scratch
        k = pl.program_id(2)

        @pl.when(k == 0)
        def _():
            acc_ref[...] = part

        @pl.when(k > 0)
        def _():
            acc_ref[...] += part

        @pl.when(k == nk - 1)
        def _():
            o_ref[...] = acc_ref[...].astype(o_ref.dtype)

    return pl.pallas_call(
        body, name=name, grid=grid, in_specs=[a_spec, b_spec], out_specs=o_spec,
        out_shape=jax.ShapeDtypeStruct(out_shape, out_dtype),
        scratch_shapes=[] if nk == 1 else [pltpu.VMEM(acc_shape, F32)],
        compiler_params=_cp("parallel", "parallel", "arbitrary"),
    )(a, b)


def _mm_nn(name, a, b, out_dtype=F32):
    M, K = a.shape
    chunked = b.ndim == 3
    Nc = b.shape[-1]
    N = Nc * (b.shape[0] if chunked else 1)
    tm = _pick(M, (1024, 512, 256, 128, 64, 32, 16, 8))
    tn = _pick(Nc, (1408, 1024, 896, 512, 256, 128))
    tk = _pick(K, (2048, 1408, 1024, 512, 256, 128))
    npc = Nc // tn
    if chunked:
        b_spec = pl.BlockSpec((None, tk, tn), lambda i, j, k: (j // npc, k, j % npc))
    else:
        b_spec = pl.BlockSpec((tk, tn), lambda i, j, k: (k, j))
    return _mm_call(name, a, b, (M, N), out_dtype, (((1,), (0,)), ((), ())), (M // tm, N // tn, K // tk),
                    pl.BlockSpec((tm, tk), lambda i, j, k: (i, k)), b_spec,
                    pl.BlockSpec((tm, tn), lambda i, j, k: (i, j)), (tm, tn))


def _mm_nt(name, a, b, out_dtype=F32):
    M, N = a.shape
    chunked = b.ndim == 3
    Nc = b.shape[-1]
    K = b.shape[-2]
    tm = _pick(M, (1024, 512, 256, 128, 64, 32, 16, 8))
    tn = _pick(K, (1408, 1024, 512, 256, 128))
    tk = _pick(Nc, (2048, 1792, 1408, 1024, 896, 512, 256, 128))
    npc = Nc // tk
    if chunked:
        b_spec = pl.BlockSpec((None, tn, tk), lambda i, j, k: (k // npc, j, k % npc))
    else:
        b_spec = pl.BlockSpec((tn, tk), lambda i, j, k: (j, k))
    return _mm_call(name, a, b, (M, K), out_dtype, (((1,), (1,)), ((), ())), (M // tm, K // tn, N // tk),
                    pl.BlockSpec((tm, tk), lambda i, j, k: (i, k)), b_spec,
                    pl.BlockSpec((tm, tn), lambda i, j, k: (i, j)), (tm, tn))


def _mm_tn(name, a, b, chunks=1, out_dtype=BF16):
    T, K = a.shape
    N = b.shape[1]
    Nc = N // chunks
    tm = _pick(K, (1408, 1024, 512, 256, 128))
    tn = _pick(Nc, (1408, 1024, 896, 512, 256, 128))
    tk = _pick(T, (1024, 512, 256, 128))
    npc = Nc // tn
    if chunks > 1:
        shape = (chunks, K, Nc)
        o_spec = pl.BlockSpec((None, tm, tn), lambda i, j, k: (j // npc, i, j % npc))
    else:
        shape = (K, N)
        o_spec = pl.BlockSpec((tm, tn), lambda i, j, k: (i, j))
    return _mm_call(name, a, b, shape, out_dtype, (((0,), (0,)), ((), ())), (K // tm, N // tn, T // tk),
                    pl.BlockSpec((tk, tm), lambda i, j, k: (k, i)),
                    pl.BlockSpec((tk, tn), lambda i, j, k: (k, j)), o_spec, (tm, tn))


def _row_tile(T):
    return _pick(T, (256, 128, 64, 32, 16, 8))


def _vec_spec(D):
    return pl.BlockSpec((1, D), lambda i: (0, 0))


def _normmod_fwd(name, x, gain, sc, sh):
    T, D = x.shape
    bt = _row_tile(T)

    def body(x_ref, g_ref, sc_ref, sh_ref, h_ref):
        h_ref[...] = _normmod(x_ref[...], g_ref[...], sc_ref[...], sh_ref[...]).astype(h_ref.dtype)

    rows = pl.BlockSpec((bt, D), lambda i: (i, 0))
    return pl.pallas_call(body, name=name, grid=(T // bt,), in_specs=[rows] + [_vec_spec(D)] * 3, out_specs=rows,
                          out_shape=jax.ShapeDtypeStruct((T, D), BF16), compiler_params=_cp("parallel"))(x, gain, sc, sh)


def _res_normmod_fwd(name, x, y, g, gain, sc, sh):
    T, D = x.shape
    bt = _row_tile(T)

    def body(x_ref, y_ref, gate_ref, g_ref, sc_ref, sh_ref, x1_ref, h_ref):
        x1 = x_ref[...] + gate_ref[...] * y_ref[...]
        x1_ref[...] = x1
        h_ref[...] = _normmod(x1, g_ref[...], sc_ref[...], sh_ref[...]).astype(h_ref.dtype)

    rows = pl.BlockSpec((bt, D), lambda i: (i, 0))
    return pl.pallas_call(body, name=name, grid=(T // bt,), in_specs=[rows, rows] + [_vec_spec(D)] * 4,
                          out_specs=[rows, rows],
                          out_shape=[jax.ShapeDtypeStruct((T, D), F32), jax.ShapeDtypeStruct((T, D), BF16)],
                          compiler_params=_cp("parallel"))(x, y, g, gain, sc, sh)


def _final_fwd_bwd(x, y, g, gain, target):
    T, D = x.shape
    bt = _row_tile(T)

    def body(x_ref, y_ref, gate_ref, g_ref, t_ref, loss_ref, dx_ref, dy_ref, dgate_ref, dgain_ref):
        i = pl.program_id(0)
        yv = y_ref[...]
        gate = gate_ref[...]
        x4 = x_ref[...] + gate * yv
        out, vjp = jax.vjp(_rms, x4, g_ref[...])
        err = out - t_ref[...]
        dx4, dgain = vjp(err * (1.0 / D))
        part = 0.5 * jnp.sum(jnp.mean(err * err, axis=-1, keepdims=True), axis=0, keepdims=True)

        @pl.when(i == 0)
        def _():
            loss_ref[...] = jnp.zeros_like(loss_ref)
            dgate_ref[...] = jnp.zeros_like(dgate_ref)
            dgain_ref[...] = jnp.zeros_like(dgain_ref)

        loss_ref[...] += jnp.broadcast_to(part, loss_ref.shape)
        dx_ref[...] = dx4
        dy_ref[...] = (gate * dx4).astype(dy_ref.dtype)
        dgate_ref[...] += jnp.sum(dx4 * yv, axis=0, keepdims=True)
        dgain_ref[...] += dgain

    rows = pl.BlockSpec((bt, D), lambda i: (i, 0))
    vec = _vec_spec(D)
    return pl.pallas_call(
        body, name="final_loss", grid=(T // bt,), in_specs=[rows, rows, vec, vec, rows],
        out_specs=[pl.BlockSpec((1, HEAD), lambda i: (0, 0)), rows, rows, vec, vec],
        out_shape=[jax.ShapeDtypeStruct((1, HEAD), F32), jax.ShapeDtypeStruct((T, D), F32),
                   jax.ShapeDtypeStruct((T, D), BF16), jax.ShapeDtypeStruct((1, D), F32),
                   jax.ShapeDtypeStruct((1, D), F32)],
        compiler_params=_cp("arbitrary"))(x, y, g, gain, target)


def _block_bwd(name, dx_out, dh, x_in, gain, sc, sh, y_prev=None, g_prev=None):
    T, D = x_in.shape
    bt = _row_tile(T)
    has_prev = y_prev is not None

    def body(*refs):
        if has_prev:
            dxo_ref, dh_ref, x_ref, g_ref, sc_ref, sh_ref, y_ref, gp_ref, dx_ref, dgain_ref, dsc_ref, dsh_ref, dy_ref, dgp_ref = refs
        else:
            dxo_ref, dh_ref, x_ref, g_ref, sc_ref, sh_ref, dx_ref, dgain_ref, dsc_ref, dsh_ref = refs
        i = pl.program_id(0)
        _, vjp = jax.vjp(_normmod, x_ref[...], g_ref[...], sc_ref[...], sh_ref[...])
        dxn, dgain, dsc, dsh = vjp(dh_ref[...])
        dx = dxo_ref[...] + dxn
        dx_ref[...] = dx

        @pl.when(i == 0)
        def _():
            dgain_ref[...] = jnp.zeros_like(dgain_ref)
            dsc_ref[...] = jnp.zeros_like(dsc_ref)
            dsh_ref[...] = jnp.zeros_like(dsh_ref)
            if has_prev:
                dgp_ref[...] = jnp.zeros_like(dgp_ref)

        dgain_ref[...] += dgain
        dsc_ref[...] += dsc
        dsh_ref[...] += dsh
        if has_prev:
            dy_ref[...] = (gp_ref[...] * dx).astype(dy_ref.dtype)
            dgp_ref[...] += jnp.sum(dx * y_ref[...], axis=0, keepdims=True)

    rows = pl.BlockSpec((bt, D), lambda i: (i, 0))
    vec = _vec_spec(D)
    ins = [dx_out, dh, x_in, gain, sc, sh]
    in_specs = [rows, rows, rows, vec, vec, vec]
    out_specs = [rows, vec, vec, vec]
    out_shape = [jax.ShapeDtypeStruct((T, D), F32)] + [jax.ShapeDtypeStruct((1, D), F32)] * 3
    if has_prev:
        ins += [y_prev, g_prev]
        in_specs += [rows, vec]
        out_specs += [rows, vec]
        out_shape += [jax.ShapeDtypeStruct((T, D), BF16), jax.ShapeDtypeStruct((1, D), F32)]
    return pl.pallas_call(body, name=name, grid=(T // bt,), in_specs=in_specs, out_specs=out_specs,
                          out_shape=out_shape, compiler_params=_cp("arbitrary"))(*ins)


def _sb_tiles(T):
    tq = _pick(T, (512, 256, 128))
    return tq, tq // HEAD


def _sb_fwd(proj, H):
    T = proj.shape[0]
    tq, nsub = _sb_tiles(T)
    scale = HEAD ** -0.5

    def body(q_ref, k_ref, v_ref, o_ref, l_ref, acc_ref):
        i = pl.program_id(1)
        q = q_ref[...].astype(BF16)
        later = (_iota((HEAD, HEAD), 0) > _iota((HEAD, HEAD), 1)).astype(BF16)
        row = _iota((tq, HEAD), 0)
        col = _iota((tq, HEAD), 1)

        def key_step(j, c, diagonal):
            off = pl.multiple_of(j * tq, tq)
            k = k_ref[pl.ds(off, tq), :].astype(BF16)
            v = v_ref[pl.ds(off, tq), :].astype(BF16)
            z = _dg(q, k, 1, 1) * scale
            ws = [None] * nsub
            for s in reversed(range(nsub)):
                zs = z[:, s * HEAD:(s + 1) * HEAD]
                sp = _softplus(zs)
                if diagonal:
                    strict = (s * HEAD + col) < row
                    lk = jnp.where(strict, -sp, 0.0)
                else:
                    lk = -sp
                w = jnp.exp(zs - sp + _sum_right(lk, later) + c)
                if diagonal:
                    w = jnp.where(strict, w, 0.0)
                ws[s] = w.astype(BF16)
                c = c + jnp.sum(lk, axis=1, keepdims=True)
            acc_ref[...] += _dg(jnp.concatenate(ws, axis=1), v, 1, 0)
            return c

        acc_ref[...] = jnp.zeros_like(acc_ref)
        c = key_step(i, jnp.zeros((tq, 1), F32), True)
        c = lax.fori_loop(0, i, lambda n, c: key_step(i - 1 - n, c, False), c)
        o_ref[...] = acc_ref[...].astype(o_ref.dtype)
        l_ref[...] = jnp.broadcast_to(c, (tq, HEAD))

    blk = pl.BlockSpec((tq, HEAD), lambda h, i: (i, h))
    return pl.pallas_call(
        body, name="sb_fwd", grid=(H, T // tq),
        in_specs=[blk, pl.BlockSpec((T, HEAD), lambda h, i: (0, H + h)), pl.BlockSpec((T, HEAD), lambda h, i: (0, 2 * H + h))],
        out_specs=[blk, blk],
        out_shape=[jax.ShapeDtypeStruct((T, H * HEAD), BF16), jax.ShapeDtypeStruct((T, H * HEAD), F32)],
        scratch_shapes=[pltpu.VMEM((tq, HEAD), F32)],
        compiler_params=_cp("parallel", "arbitrary"))(proj, proj, proj)


def _sb_bwd(proj, do, L, H):
    T = proj.shape[0]
    tq, nsub = _sb_tiles(T)
    scale = HEAD ** -0.5

    def body(q_ref, k_ref, v_ref, do_ref, l_ref, dq_ref, dk_ref, dv_ref):
        i = pl.program_id(1)

        @pl.when(i == 0)
        def _():
            dk_ref[...] = jnp.zeros_like(dk_ref)
            dv_ref[...] = jnp.zeros_like(dv_ref)

        dq_ref[...] = jnp.zeros_like(dq_ref)
        q = q_ref[...].astype(BF16)
        do_ = do_ref[...].astype(BF16)
        total = l_ref[...]
        upto = (_iota((HEAD, HEAD), 0) <= _iota((HEAD, HEAD), 1)).astype(BF16)
        before = (_iota((HEAD, HEAD), 0) < _iota((HEAD, HEAD), 1)).astype(BF16)
        row = _iota((tq, HEAD), 0)
        col = _iota((tq, HEAD), 1)

        def key_step(j, carry, diagonal):
            cp, ce = carry
            off = pl.multiple_of(j * tq, tq)
            k = k_ref[pl.ds(off, tq), :].astype(BF16)
            v = v_ref[pl.ds(off, tq), :].astype(BF16)
            z = _dg(q, k, 1, 1) * scale
            dw = _dg(do_, v, 1, 1)
            ws, dzs = [], []
            for s in range(nsub):
                zs = z[:, s * HEAD:(s + 1) * HEAD]
                sp = _softplus(zs)
                if diagonal:
                    strict = (s * HEAD + col) < row
                    lk = jnp.where(strict, -sp, 0.0)
                else:
                    lk = -sp
                tail = total - (_sum_right(lk, upto) + cp)
                w = jnp.exp(zs - sp + tail)
                if diagonal:
                    w = jnp.where(strict, w, 0.0)
                e = w * dw[:, s * HEAD:(s + 1) * HEAD]
                e_before = _sum_right(e, before) + ce
                sig = jnp.exp(zs - sp)
                dz = (e * (1.0 - sig) - e_before * sig) * scale
                if diagonal:
                    dz = jnp.where(strict, dz, 0.0)
                ws.append(w.astype(BF16))
                dzs.append(dz.astype(BF16))
                cp = cp + jnp.sum(lk, axis=1, keepdims=True)
                ce = ce + jnp.sum(e, axis=1, keepdims=True)
            w_all = jnp.concatenate(ws, axis=1)
            dz_all = jnp.concatenate(dzs, axis=1)
            dv_ref[pl.ds(off, tq), :] += _dg(w_all, do_, 0, 0)
            dk_ref[pl.ds(off, tq), :] += _dg(dz_all, q, 0, 0)
            dq_ref[...] += _dg(dz_all, k, 1, 0)
            return cp, ce

        zero = jnp.zeros((tq, 1), F32)
        carry = lax.fori_loop(0, i, lambda j, cr: key_step(j, cr, False), (zero, zero))
        key_step(i, carry, True)

    blk = pl.BlockSpec((tq, HEAD), lambda h, i: (i, h))
    full = pl.BlockSpec((T, HEAD), lambda h, i: (0, h))
    shp = jax.ShapeDtypeStruct((T, H * HEAD), F32)
    return pl.pallas_call(
        body, name="sb_bwd", grid=(H, T // tq),
        in_specs=[blk, pl.BlockSpec((T, HEAD), lambda h, i: (0, H + h)), pl.BlockSpec((T, HEAD), lambda h, i: (0, 2 * H + h)),
                  blk, blk],
        out_specs=[blk, full, full], out_shape=[shp, shp, shp],
        compiler_params=_cp("parallel", "arbitrary"))(proj, proj, proj, do, L)


def _hg_tile(q, fl, iv, g, st, l0, l1, gain):
    R = 2 * HG_CHUNK
    row = _iota((R, R), 0)
    col = _iota((R, R), 1)
    first = row < HG_CHUNK
    same = first == (col < HG_CHUNK)
    tri = (row >= col) & same
    lb = _sigmoid(l0 - l1)
    f = lb + (1.0 - lb) * _sigmoid(fl)
    logf = jnp.log(f)
    k = 1.0 - f
    qf = q * _sigmoid(q)
    G = _sum_left(tri.astype(BF16), logf)
    gl_a = jnp.sum(jnp.where(first, logf, 0.0), axis=0, keepdims=True)
    gl_b = jnp.sum(jnp.where(first, 0.0, logf), axis=0, keepdims=True)
    q_dec = qf * jnp.exp(G)
    k_inv = k * jnp.exp(-G)
    k_end = k * jnp.exp(jnp.where(first, gl_a, gl_b) - G)
    scores = jnp.where(tri, mm_nt(q_dec, k_inv), 0.0)
    o = mm_nn(scores, iv)
    o_a = mm_nt(q_dec, st)
    st_mid = st * jnp.exp(gl_a) + mm_tn(jnp.where(first, iv, 0.0), k_end)
    o_b = mm_nt(q_dec, st_mid)
    st_new = st_mid * jnp.exp(gl_b) + mm_tn(jnp.where(first, 0.0, iv), k_end)
    o = o + jnp.where(first, o_a, o_b)
    on = o * lax.rsqrt(jnp.mean(o * o, axis=-1, keepdims=True) + NORM_EPS) * gain
    return on * (g * _sigmoid(g)), st_new


def _hg_specs(H, c0, rev, nt):
    def at(base):
        if rev:
            return pl.BlockSpec((HEAD, HEAD), lambda h, i: (nt - 1 - i, base + h))
        return pl.BlockSpec((HEAD, HEAD), lambda h, i: (i, base + h))
    return [at(c0), at(c0 + H), at(c0 + 2 * H), at(c0 + 3 * H)]


def _hg_fwd(proj, l0, l1, gain, H, c0):
    T = proj.shape[0]
    nt = T // HEAD

    def body(q_ref, f_ref, i_ref, g_ref, l0_ref, l1_ref, gain_ref, o_ref, st_out_ref, st_ref):
        @pl.when(pl.program_id(1) == 0)
        def _():
            st_ref[...] = jnp.zeros_like(st_ref)

        st = st_ref[...]
        st_out_ref[...] = st
        out, st_new = _hg_tile(q_ref[...], f_ref[...], i_ref[...], g_ref[...], st, l0_ref[...], l1_ref[...], gain_ref[...])
        o_ref[...] = out.astype(o_ref.dtype)
        st_ref[...] = st_new

    vec = pl.BlockSpec((1, HEAD), lambda h, i: (0, h))
    return pl.pallas_call(
        body, name="hg_fwd", grid=(H, nt), in_specs=_hg_specs(H, c0, False, nt) + [vec, vec, vec],
        out_specs=[pl.BlockSpec((HEAD, HEAD), lambda h, i: (i, h)),
                   pl.BlockSpec((None, None, HEAD, HEAD), lambda h, i: (h, i, 0, 0))],
        out_shape=[jax.ShapeDtypeStruct((T, H * HEAD), BF16), jax.ShapeDtypeStruct((H, nt, HEAD, HEAD), F32)],
        scratch_shapes=[pltpu.VMEM((HEAD, HEAD), F32)],
        compiler_params=_cp("parallel", "arbitrary"))(proj, proj, proj, proj, l0, l1, gain)


def _hg_bwd(proj, states, do, l0, l1, gain, H, c0, do_c0):
    T = proj.shape[0]
    nt = T // HEAD

    def body(q_ref, f_ref, i_ref, g_ref, st_in_ref, do_ref, l0_ref, l1_ref, gain_ref,
             dq_ref, df_ref, di_ref, dg_ref, dl0_ref, dl1_ref, dgain_ref, dst_ref):
        @pl.when(pl.program_id(1) == 0)
        def _():
            dst_ref[...] = jnp.zeros_like(dst_ref)
            dl0_ref[...] = jnp.zeros_like(dl0_ref)
            dl1_ref[...] = jnp.zeros_like(dl1_ref)
            dgain_ref[...] = jnp.zeros_like(dgain_ref)

        _, vjp = jax.vjp(_hg_tile, q_ref[...], f_ref[...], i_ref[...], g_ref[...], st_in_ref[...],
                         l0_ref[...], l1_ref[...], gain_ref[...])
        dq, df, di, dg, dst, dl0, dl1, dgain = vjp((do_ref[...], dst_ref[...]))
        dq_ref[...] = dq
        df_ref[...] = df
        di_ref[...] = di
        dg_ref[...] = dg
        dst_ref[...] = dst
        dl0_ref[...] += dl0
        dl1_ref[...] += dl1
        dgain_ref[...] += dgain

    vec = pl.BlockSpec((1, HEAD), lambda h, i: (0, h))
    rblk = pl.BlockSpec((HEAD, HEAD), lambda h, i: (nt - 1 - i, h))
    shp = jax.ShapeDtypeStruct((T, H * HEAD), F32)
    vshp = jax.ShapeDtypeStruct((1, H * HEAD), F32)
    return pl.pallas_call(
        body, name="hg_bwd", grid=(H, nt),
        in_specs=_hg_specs(H, c0, True, nt) + [
            pl.BlockSpec((None, None, HEAD, HEAD), lambda h, i: (h, nt - 1 - i, 0, 0)),
            pl.BlockSpec((HEAD, HEAD), lambda h, i: (nt - 1 - i, do_c0 + h)), vec, vec, vec],
        out_specs=[rblk, rblk, rblk, rblk, vec, vec, vec],
        out_shape=[shp, shp, shp, shp, vshp, vshp, vshp],
        scratch_shapes=[pltpu.VMEM((HEAD, HEAD), F32)],
        compiler_params=_cp("parallel", "arbitrary"))(proj, proj, proj, proj, states, do, l0, l1, gain)


def _sg_chunk(u_parts, v_parts, gains, biases, wpos, bpos):
    W = sum(p.shape[1] for p in v_parts)
    C = v_parts[0].shape[0]
    v = [_gelu(p) for p in v_parts]
    mu = sum(jnp.sum(p, axis=-1, keepdims=True) for p in v) * (1.0 / W)
    xc = [p - mu for p in v]
    r = lax.rsqrt(sum(jnp.sum(p * p, axis=-1, keepdims=True) for p in xc) * (1.0 / W) + NORM_EPS)
    causal = _iota((C, C), 0) >= _iota((C, C), 1)
    out = []
    for up, p, gn, bs, w, b in zip(u_parts, xc, gains, biases, wpos, bpos):
        vn = p * r * gn + bs
        mixed = mm_nn(jnp.where(causal, w, 0.0), vn) + b
        out.append(_gelu(up) * mixed)
    return out


def _sg_fwd(zpre, vgain, vbias, wpos, bpos):
    T, W2 = zpre.shape
    W = W2 // 2
    G = wpos.shape[0]
    cg = W // G
    C = SG_CHUNK

    def body(z_ref, gn_ref, bs_ref, w_ref, b_ref, s_ref):
        sl = [slice(g * cg, (g + 1) * cg) for g in range(G)]
        out = _sg_chunk([z_ref[:, s] for s in sl], [z_ref[:, W + s.start:W + s.stop] for s in sl],
                        [gn_ref[:, s] for s in sl], [bs_ref[:, s] for s in sl],
                        [w_ref[g] for g in range(G)], [b_ref[g] for g in range(G)])
        for s, o in zip(sl, out):
            s_ref[:, s] = o.astype(s_ref.dtype)

    return pl.pallas_call(
        body, name="sg_fwd", grid=(T // C,),
        in_specs=[pl.BlockSpec((C, W2), lambda i: (i, 0)), _vec_spec(W), _vec_spec(W),
                  pl.BlockSpec((G, C, C), lambda i: (0, 0, 0)), pl.BlockSpec((G, C, 1), lambda i: (0, 0, 0))],
        out_specs=pl.BlockSpec((C, W), lambda i: (i, 0)),
        out_shape=jax.ShapeDtypeStruct((T, W), BF16), compiler_params=_cp("parallel"))(zpre, vgain, vbias, wpos, bpos)


def _sg_bwd(zpre, ds, vgain, vbias, wpos, bpos):
    T, W2 = zpre.shape
    W = W2 // 2
    G = wpos.shape[0]
    cg = W // G
    C = SG_CHUNK

    def body(z_ref, ds_ref, gn_ref, bs_ref, w_ref, b_ref, dz_ref, dgn_ref, dbs_ref, dw_ref, db_ref):
        @pl.when(pl.program_id(0) == 0)
        def _():
            dgn_ref[...] = jnp.zeros_like(dgn_ref)
            dbs_ref[...] = jnp.zeros_like(dbs_ref)
            dw_ref[...] = jnp.zeros_like(dw_ref)
            db_ref[...] = jnp.zeros_like(db_ref)

        sl = [slice(g * cg, (g + 1) * cg) for g in range(G)]
        _, vjp = jax.vjp(_sg_chunk, [z_ref[:, s] for s in sl], [z_ref[:, W + s.start:W + s.stop] for s in sl],
                         [gn_ref[:, s] for s in sl], [bs_ref[:, s] for s in sl],
                         [w_ref[g] for g in range(G)], [b_ref[g] for g in range(G)])
        du, dv, dgn, dbs, dw, db = vjp([ds_ref[:, s] for s in sl])
        for g, s in enumerate(sl):
            dz_ref[:, s] = du[g].astype(dz_ref.dtype)
            dz_ref[:, W + s.start:W + s.stop] = dv[g].astype(dz_ref.dtype)
            dgn_ref[:, s] += dgn[g]
            dbs_ref[:, s] += dbs[g]
            dw_ref[g] += dw[g]
            db_ref[g] += db[g]

    wspec = pl.BlockSpec((G, C, C), lambda i: (0, 0, 0))
    bspec = pl.BlockSpec((G, C, 1), lambda i: (0, 0, 0))
    return pl.pallas_call(
        body, name="sg_bwd", grid=(T // C,),
        in_specs=[pl.BlockSpec((C, W2), lambda i: (i, 0)), pl.BlockSpec((C, W), lambda i: (i, 0)),
                  _vec_spec(W), _vec_spec(W), wspec, bspec],
        out_specs=[pl.BlockSpec((C, W2), lambda i: (i, 0)), _vec_spec(W), _vec_spec(W), wspec, bspec],
        out_shape=[jax.ShapeDtypeStruct((T, W2), BF16), jax.ShapeDtypeStruct((1, W), F32),
                   jax.ShapeDtypeStruct((1, W), F32), jax.ShapeDtypeStruct((G, C, C), F32),
                   jax.ShapeDtypeStruct((G, C, 1), F32)],
        compiler_params=_cp("arbitrary"))(zpre, ds, vgain, vbias, wpos, bpos)


def _conv_tiles(T, F):
    return _pick(T, (512, 256, 128, 64, 32, 16, 8)), _pick(F, (512, 256, 128))


def _shift_down(cur, prev8, n, first_tile):
    bt = cur.shape[0]
    r = pltpu.roll(cur, n, 0)
    p = pltpu.roll(prev8, n, 0)
    p = jnp.where(first_tile, 0.0, p)
    head = jnp.concatenate([p, r[8:]], axis=0) if bt > 8 else p
    return jnp.where(_iota(cur.shape, 0) < n, head, r)


def _shift_up(cur, next8, n, last_tile):
    bt = cur.shape[0]
    r = pltpu.roll(cur, bt - n, 0)
    p = pltpu.roll(next8, 8 - n, 0)
    p = jnp.where(last_tile, 0.0, p)
    tail = jnp.concatenate([r[:bt - 8], p], axis=0) if bt > 8 else p
    return jnp.where(_iota(cur.shape, 0) >= bt - n, tail, r)


def _conv_apply(cur, prev8, w_ref, b, first_tile):
    return (b + w_ref[0:1, :] * _shift_down(cur, prev8, 2, first_tile)
            + w_ref[1:2, :] * _shift_down(cur, prev8, 1, first_tile) + w_ref[2:3, :] * cur)


def _conv_fwd(name, a, w, b):
    T, F2 = a.shape
    F = F2 // 2
    bt, cw = _conv_tiles(T, F)
    nf = F // cw
    r8 = bt // 8

    def body(g_ref, gp_ref, v_ref, vp_ref, wg_ref, wv_ref, bg_ref, bv_ref, u_ref):
        first = pl.program_id(0) == 0
        gate = _conv_apply(g_ref[...], gp_ref[...], wg_ref, bg_ref[...], first)
        val = _conv_apply(v_ref[...], vp_ref[...], wv_ref, bv_ref[...], first)
        u_ref[...] = (gate * _sigmoid(gate) * val).astype(u_ref.dtype)

    def cur(off):
        return pl.BlockSpec((bt, cw), lambda i, j: (i, j + off))

    def prev(off):
        return pl.BlockSpec((8, cw), lambda i, j: (jnp.maximum(i * r8 - 1, 0), j + off))

    def vec(rows, off):
        return pl.BlockSpec((rows, cw), lambda i, j: (0, j + off))

    return pl.pallas_call(
        body, name=name, grid=(T // bt, nf),
        in_specs=[cur(0), prev(0), cur(nf), prev(nf), vec(3, 0), vec(3, nf), vec(1, 0), vec(1, nf)],
        out_specs=pl.BlockSpec((bt, cw), lambda i, j: (i, j)),
        out_shape=jax.ShapeDtypeStruct((T, F), BF16),
        compiler_params=_cp("parallel", "parallel"))(a, a, a, a, w, w, b, b)


def _conv_bwd(name, a, du, w, b):
    T, F2 = a.shape
    F = F2 // 2
    bt, cw = _conv_tiles(T, F)
    nf = F // cw
    r8 = bt // 8
    last_blk = T // 8 - 1

    def body(g_ref, gp_ref, gn_ref, v_ref, vp_ref, vn_ref, du_ref, dun_ref, wg_ref, wv_ref, bg_ref, bv_ref,
             dag_ref, dav_ref, dwg_ref, dwv_ref, dbg_ref, dbv_ref):
        i = pl.program_id(1)
        first = i == 0
        last = i == pl.num_programs(1) - 1

        def taps(cur, prev8, at_start):
            return _shift_down(cur, prev8, 2, at_start), _shift_down(cur, prev8, 1, at_start), cur

        def conv(t, w_ref, b_ref):
            return b_ref[...] + w_ref[0:1, :] * t[0] + w_ref[1:2, :] * t[1] + w_ref[2:3, :] * t[2]

        def act_bwd(gate, val, du_):
            sg = _sigmoid(gate)
            return du_ * val * (sg * (1.0 + gate * (1.0 - sg))), du_ * gate * sg

        g_cur, v_cur = g_ref[...], v_ref[...]
        tg = taps(g_cur, gp_ref[...], first)
        tv = taps(v_cur, vp_ref[...], first)
        dg, dv = act_bwd(conv(tg, wg_ref, bg_ref), conv(tv, wv_ref, bv_ref), du_ref[...])
        tgn = taps(gn_ref[...], g_cur[bt - 8:, :], False)
        tvn = taps(vn_ref[...], v_cur[bt - 8:, :], False)
        dgn, dvn = act_bwd(conv(tgn, wg_ref, bg_ref), conv(tvn, wv_ref, bv_ref), dun_ref[...])

        def conv_t(d, dn, w_ref):
            return w_ref[2:3, :] * d + w_ref[1:2, :] * _shift_up(d, dn, 1, last) + w_ref[0:1, :] * _shift_up(d, dn, 2, last)

        dag_ref[...] = conv_t(dg, dgn, wg_ref).astype(dag_ref.dtype)
        dav_ref[...] = conv_t(dv, dvn, wv_ref).astype(dav_ref.dtype)

        @pl.when(first)
        def _():
            dwg_ref[...] = jnp.zeros_like(dwg_ref)
            dwv_ref[...] = jnp.zeros_like(dwv_ref)
            dbg_ref[...] = jnp.zeros_like(dbg_ref)
            dbv_ref[...] = jnp.zeros_like(dbv_ref)

        for t in range(CONV_WIDTH):
            dwg_ref[t:t + 1, :] += jnp.sum(dg * tg[t], axis=0, keepdims=True)
            dwv_ref[t:t + 1, :] += jnp.sum(dv * tv[t], axis=0, keepdims=True)
        dbg_ref[...] += jnp.sum(dg, axis=0, keepdims=True)
        dbv_ref[...] += jnp.sum(dv, axis=0, keepdims=True)

    def cur(off):
        return pl.BlockSpec((bt, cw), lambda j, i: (i, j + off))

    def prev(off):
        return pl.BlockSpec((8, cw), lambda j, i: (jnp.maximum(i * r8 - 1, 0), j + off))

    def nxt(off):
        return pl.BlockSpec((8, cw), lambda j, i: (jnp.minimum((i + 1) * r8, last_blk), j + off))

    def vec(rows, off):
        return pl.BlockSpec((rows, cw), lambda j, i: (0, j + off))

    half = jax.ShapeDtypeStruct((T, F), BF16)
    dag, dav, dwg, dwv, dbg, dbv = pl.pallas_call(
        body, name=name, grid=(nf, T // bt),
        in_specs=[cur(0), prev(0), nxt(0), cur(nf), prev(nf), nxt(nf), cur(0), nxt(0),
                  vec(3, 0), vec(3, nf), vec(1, 0), vec(1, nf)],
        out_specs=[cur(0), cur(0), vec(3, 0), vec(3, 0), vec(1, 0), vec(1, 0)],
        out_shape=[half, half, jax.ShapeDtypeStruct((3, F), F32), jax.ShapeDtypeStruct((3, F), F32),
                   jax.ShapeDtypeStruct((1, F), F32), jax.ShapeDtypeStruct((1, F), F32)],
        compiler_params=_cp("parallel", "arbitrary"))(a, a, a, a, a, a, du, du, w, w, b, b)
    return dag, dav, jnp.concatenate([dwg, dwv], axis=1), jnp.concatenate([dbg, dbv], axis=1)


def _ada_fwd(c_all, ada_w, ada_b):
    R, D = c_all.shape
    L, _, Ns = ada_w.shape
    tn = _pick(Ns, (512, 256, 128))

    def body(c_ref, w_ref, b_ref, o_ref):
        cv = c_ref[...]
        cond = cv * _sigmoid(cv)
        o_ref[...] = _dg(cond, w_ref[...], 1, 0) + b_ref[...]

    return pl.pallas_call(
        body, name="ada_fwd", grid=(L, Ns // tn),
        in_specs=[pl.BlockSpec((R, D), lambda l, j: (0, 0)), pl.BlockSpec((None, D, tn), lambda l, j: (l, 0, j)),
                  pl.BlockSpec((None, 1, tn), lambda l, j: (l, 0, j))],
        out_specs=pl.BlockSpec((None, R, tn), lambda l, j: (l, 0, j)),
        out_shape=jax.ShapeDtypeStruct((L, R, Ns), F32), compiler_params=_cp("parallel", "parallel"))(c_all, ada_w, ada_b)


def _adam_math(w, g, m, v):
    m2 = ADAM_B1 * m + (1.0 - ADAM_B1) * g
    v2 = ADAM_B2 * v + (1.0 - ADAM_B2) * (g * g)
    m_hat = m2 / (1.0 - ADAM_B1 ** ADAM_STEP)
    v_hat = v2 / (1.0 - ADAM_B2 ** ADAM_STEP)
    delta = -ADAM_LR * (m_hat / (jnp.sqrt(v_hat) + ADAM_EPS) + ADAM_WD * w)
    return delta, m2, v2


def _ada_grad_adam(c_all_t, dmod, w, m, v):
    D, R = c_all_t.shape
    L, _, Ns = dmod.shape
    tr = _rows_within(D, Ns * 4, 1 << 20)

    def body(c_ref, d_ref, w_ref, m_ref, v_ref, g_ref, dl_ref, m2_ref, v2_ref):
        cv = c_ref[...]
        g = _dg(cv * _sigmoid(cv), d_ref[...], 1, 0)
        g_ref[...] = g
        dl_ref[...], m2_ref[...], v2_ref[...] = _adam_math(w_ref[...], g, m_ref[...], v_ref[...])

    big = pl.BlockSpec((None, tr, Ns), lambda l, i: (l, i, 0))
    shp = jax.ShapeDtypeStruct((L, D, Ns), F32)
    return pl.pallas_call(
        body, name="ada_grad_adam", grid=(L, D // tr),
        in_specs=[pl.BlockSpec((tr, R), lambda l, i: (i, 0)), pl.BlockSpec((None, R, Ns), lambda l, i: (l, 0, 0)), big, big, big],
        out_specs=[big] * 4, out_shape=[shp] * 4, compiler_params=_cp("parallel", "parallel"))(c_all_t, dmod, w, m, v)


def _adam(name, w, g, m, v):
    R, C = w.shape
    tr = _rows_within(R, C * 4, 1 << 21)

    def body(w_ref, g_ref, m_ref, v_ref, dl_ref, m2_ref, v2_ref):
        dl_ref[...], m2_ref[...], v2_ref[...] = _adam_math(w_ref[...], g_ref[...], m_ref[...], v_ref[...])

    blk = pl.BlockSpec((tr, C), lambda i: (i, 0))
    shp = jax.ShapeDtypeStruct((R, C), F32)
    return pl.pallas_call(body, name=name, grid=(R // tr,), in_specs=[blk] * 4, out_specs=[blk] * 3,
                          out_shape=[shp] * 3, compiler_params=_cp("parallel"))(w, g, m, v)


def _cast_into_rows(name, w, chip):
    _, R, C = w.shape
    tr = _rows_within(R, C * 4, 1 << 22)

    def body(chip_ref, w_ref, o_ref):
        o_ref[...] = w_ref[...].astype(BF16)

    grid_spec = pltpu.PrefetchScalarGridSpec(
        num_scalar_prefetch=1, grid=(2, R // tr),
        in_specs=[pl.BlockSpec((None, tr, C), lambda h, i, s: (h, i, 0))],
        out_specs=pl.BlockSpec((None, tr, C), lambda h, i, s: (2 * s[0] + h, i, 0)))
    return pl.pallas_call(body, name=name, grid_spec=grid_spec, out_shape=jax.ShapeDtypeStruct((N_DEV, R, C), BF16),
                          compiler_params=_cp("arbitrary", "arbitrary"))(chip.reshape(1).astype(jnp.int32), w)


def _add_pairs(name, a, b):
    _, R, C = a.shape
    tr = _rows_within(R, C * 2, 1 << 21)

    def body(a_ref, b_ref, o_ref):
        o_ref[...] = (a_ref[...].astype(F32) + b_ref[...].astype(F32)).astype(o_ref.dtype)

    blk = pl.BlockSpec((None, tr, C), lambda j, i: (j, i, 0))
    return pl.pallas_call(body, name=name, grid=(4, R // tr), in_specs=[blk, blk], out_specs=blk,
                          out_shape=jax.ShapeDtypeStruct(a.shape, BF16), compiler_params=_cp("parallel", "parallel"))(a, b)


def _sum_into_pair(name, own, landed, slot, chip):
    n, R, C = landed.shape
    tr = _rows_within(R, (n + 1) * C * landed.dtype.itemsize, 1 << 23)

    def body(idx_ref, own_ref, x_ref, o_ref):
        mine = idx_ref[1]
        acc = None
        for j in range(n):
            part = jnp.where(mine == j, own_ref[...], x_ref[j]).astype(F32)
            acc = part if acc is None else acc + part
        o_ref[...] = acc

    grid_spec = pltpu.PrefetchScalarGridSpec(
        num_scalar_prefetch=1, grid=(R // tr,),
        in_specs=[pl.BlockSpec((None, tr, C), lambda i, s: (s[1], i, 0)), pl.BlockSpec((n, tr, C), lambda i, s: (0, i, 0))],
        out_specs=pl.BlockSpec((None, tr, C), lambda i, s: (s[0], i, 0)))
    idx = jnp.stack([slot, chip]).astype(jnp.int32)
    return pl.pallas_call(body, name=name, grid_spec=grid_spec, out_shape=jax.ShapeDtypeStruct((2, R, C), F32),
                          compiler_params=_cp("arbitrary"))(idx, own, landed)


def _sum_leading(name, a, out_dtype=F32):
    n, R, C = a.shape
    tr = _rows_within(R, n * C * a.dtype.itemsize, 1 << 23)

    def body(a_ref, o_ref):
        acc = a_ref[0].astype(F32)
        for j in range(1, n):
            acc = acc + a_ref[j].astype(F32)
        o_ref[...] = acc.astype(o_ref.dtype)

    return pl.pallas_call(body, name=name, grid=(R // tr,), in_specs=[pl.BlockSpec((n, tr, C), lambda i: (0, i, 0))],
                          out_specs=pl.BlockSpec((tr, C), lambda i: (i, 0)),
                          out_shape=jax.ShapeDtypeStruct((R, C), out_dtype), compiler_params=_cp("parallel"))(a)


def _place():
    return lax.axis_index("x"), lax.axis_index("y"), lax.axis_index("c")


def _all_gather(name, blocks, halves=False, after=None):
    n = len(blocks)
    shapes = [b.shape[1:] if halves else b.shape for b in blocks]
    extra = [] if after is None else [after]

    def body(*refs):
        ins, outs = refs[:n], refs[n + len(extra):2 * n + len(extra)]
        send_sems, recv_sems, local_sems = refs[2 * n + len(extra):]
        x, y, c = _place()
        me, sibling = (x, y, c), (x, y, 1 - c)
        chips = [(1 - x, y), (x, 1 - y), (1 - x, 1 - y)]

        def rows(a, px, py, pc):
            return outs[a].at[4 * px + 2 * py + pc]

        def copy(a, k, block, to, src=None):
            return pltpu.make_async_remote_copy(
                src_ref=rows(a, *block) if src is None else src, dst_ref=rows(a, *block),
                send_sem=send_sems.at[7 * a + k], recv_sem=recv_sems.at[7 * a + k],
                device_id=to, device_id_type=MESH)

        started = []
        mine = []
        for a in range(n):
            src = ins[a].at[c] if halves else ins[a]
            mine.append(pltpu.make_async_copy(src, rows(a, *me), local_sems.at[a]))
            mine[-1].start()
            first = [copy(a, 0, me, sibling, src=src)]
            first += [copy(a, 1 + j, me, (*chip, c), src=src) for j, chip in enumerate(chips)]
            for cp in first:
                cp.start()
            started += first
        for j, chip in enumerate(chips):
            for a in range(n):
                copy(a, 1 + j, (*chip, c), me).wait_recv()
                passed = copy(a, 4 + j, (*chip, c), sibling)
                passed.start()
                started.append(passed)
        for a in range(n):
            copy(a, 0, sibling, me).wait_recv()
            for j, chip in enumerate(chips):
                copy(a, 4 + j, (*chip, 1 - c), me).wait_recv()
        for cp in started:
            cp.wait_send()
        for cp in mine:
            cp.wait()

    return pl.pallas_call(
        body, name=name, in_specs=[ANY] * (n + len(extra)), out_specs=[ANY] * n,
        out_shape=[jax.ShapeDtypeStruct((N_DEV,) + tuple(s), b.dtype) for s, b in zip(shapes, blocks)],
        scratch_shapes=[pltpu.SemaphoreType.DMA((7 * n,)), pltpu.SemaphoreType.DMA((7 * n,)),
                        pltpu.SemaphoreType.DMA((n,))],
    )(*blocks, *extra)


def _to_sibling(name, arrays, pick):
    n = len(arrays)
    per = 4 if pick == "other_half" else 1

    def body(*refs):
        ins, outs = refs[:n], refs[n:2 * n]
        send_sems, recv_sems, local_sems = refs[2 * n:]
        x, y, c = _place()
        sibling = (x, y, 1 - c)
        started = []
        local = []
        for a in range(n):
            for j in range(per):
                if pick == "other_half":
                    src, dst = ins[a].at[2 * j + (1 - c)], outs[a].at[j]
                else:
                    src, dst = ins[a], outs[a].at[c]
                    local.append(pltpu.make_async_copy(src, dst, local_sems.at[a]))
                    local[-1].start()
                cp = pltpu.make_async_remote_copy(src_ref=src, dst_ref=dst, send_sem=send_sems.at[per * a + j],
                                                  recv_sem=recv_sems.at[per * a + j], device_id=sibling, device_id_type=MESH)
                cp.start()
                started.append(cp)
        for cp in started:
            cp.wait()
        for cp in local:
            cp.wait()

    if pick == "other_half":
        out_shape = [jax.ShapeDtypeStruct((4,) + a.shape[1:], a.dtype) for a in arrays]
    else:
        out_shape = [jax.ShapeDtypeStruct((2,) + a.shape, a.dtype) for a in arrays]
    return pl.pallas_call(
        body, name=name, in_specs=[ANY] * n, out_specs=[ANY] * n, out_shape=out_shape,
        scratch_shapes=[pltpu.SemaphoreType.DMA((per * n,)), pltpu.SemaphoreType.DMA((per * n,)),
                        pltpu.SemaphoreType.DMA((n,))],
    )(*arrays)


def _share_halves(name, arrays):
    n = len(arrays)

    def body(*refs):
        ins, outs = refs[:n], refs[n:2 * n]
        send_sems, recv_sems = refs[2 * n:]
        x, y, c = _place()
        started = []
        for a in range(n):
            cp = pltpu.make_async_remote_copy(src_ref=ins[a].at[c], dst_ref=outs[a].at[c], send_sem=send_sems.at[a],
                                              recv_sem=recv_sems.at[a], device_id=(x, y, 1 - c), device_id_type=MESH)
            cp.start()
            started.append(cp)
        for a in range(n):
            started[a].wait_send()
            pltpu.make_async_remote_copy(src_ref=ins[a].at[1 - c], dst_ref=outs[a].at[1 - c], send_sem=send_sems.at[a],
                                         recv_sem=recv_sems.at[a], device_id=(x, y, 1 - c), device_id_type=MESH).wait_recv()

    return pl.pallas_call(
        body, name=name, in_specs=[ANY] * n, out_specs=[ANY] * n,
        out_shape=[jax.ShapeDtypeStruct(a.shape, a.dtype) for a in arrays],
        input_output_aliases={a: a for a in range(n)},
        scratch_shapes=[pltpu.SemaphoreType.DMA((n,)), pltpu.SemaphoreType.DMA((n,))],
    )(*arrays)


HBM = pl.BlockSpec(memory_space=pltpu.HBM)
SEM = pl.BlockSpec(memory_space=pltpu.SEMAPHORE)
EFFECT = pltpu.SideEffectType.DATAFLOW_SIDE_EFFECTING


def _chip_copies(kind, srcs, dsts, send_sems, recv_sems):
    x, y, c = _place()
    mine = 2 * x + y
    sends, arrivals = [], []
    for a in range(len(srcs)):
        for k, (px, py) in enumerate([(1 - x, y), (x, 1 - y), (1 - x, 1 - y)]):
            other = 2 * px + py
            if kind == "rows":
                src, dst, lands = srcs[a].at[2 * mine + c], dsts[a].at[2 * mine + c], dsts[a].at[2 * other + c]
            else:
                src, dst, lands = srcs[a].at[other], dsts[a].at[mine], dsts[a].at[other]
            sem = dict(send_sem=send_sems.at[3 * a + k], recv_sem=recv_sems.at[3 * a + k], device_id=(px, py, c),
                       device_id_type=MESH)
            sends.append(pltpu.make_async_remote_copy(src_ref=src, dst_ref=dst, **sem))
            arrivals.append(pltpu.make_async_remote_copy(src_ref=src, dst_ref=lands, **sem))
    return sends, arrivals


def _chips_start(name, kind, srcs, dsts=None, after=None):
    n = len(srcs)
    bufs = list(srcs) + (list(dsts) if dsts is not None else [])
    nb = len(bufs)
    extra = [] if after is None else [after]

    def body(*refs):
        ins = refs[:nb]
        send_sems, recv_sems = refs[nb + len(extra)], refs[nb + len(extra) + 1]
        token = refs[-1]
        sends, _ = _chip_copies(kind, ins[:n], ins[n:] if dsts is not None else ins[:n], send_sems, recv_sems)
        for cp in sends:
            cp.start()
        token[...] = jnp.zeros_like(token)

    out = pl.pallas_call(
        body, name=name,
        out_shape=(pltpu.SemaphoreType.DMA((3 * n,)), pltpu.SemaphoreType.DMA((3 * n,)),
                   *[pltpu.HBM(b.shape, b.dtype) for b in bufs], jax.ShapeDtypeStruct((8, HEAD), F32)),
        in_specs=(HBM,) * nb + (ANY,) * len(extra),
        out_specs=(SEM, SEM) + (HBM,) * nb + (pl.BlockSpec(memory_space=pltpu.VMEM),),
        input_output_aliases={i: 2 + i for i in range(nb)},
        compiler_params=pltpu.CompilerParams(has_side_effects=EFFECT),
    )(*[pltpu.with_memory_space_constraint(b, pltpu.HBM) for b in bufs], *extra)
    return out[0], out[1], list(out[2:2 + nb]), out[-1]


def _chips_wait(name, kind, n, send_sems, recv_sems, bufs, after):
    nb = len(bufs)

    def body(*refs):
        ins = refs[:nb]
        s_sems, r_sems = refs[nb], refs[nb + 1]
        sends, arrivals = _chip_copies(kind, ins[:n], ins[n:] if nb > n else ins[:n], s_sems, r_sems)
        for cp in sends:
            cp.wait_send()
        for cp in arrivals:
            cp.wait_recv()

    return list(pl.pallas_call(
        body, name=name, out_shape=tuple(pltpu.HBM(b.shape, b.dtype) for b in bufs),
        in_specs=(HBM,) * nb + (SEM, SEM, pl.BlockSpec(memory_space=pl.ANY)), out_specs=(HBM,) * nb,
        input_output_aliases={i: i for i in range(nb)},
        compiler_params=pltpu.CompilerParams(has_side_effects=EFFECT),
    )(*bufs, send_sems, recv_sems, after))


def _fill_from_sibling(name, arrays):
    n = len(arrays)

    def body(*refs):
        ins, outs = refs[:n], refs[n:2 * n]
        send_sems, recv_sems = refs[2 * n:]
        x, y, c = _place()
        sends, arrivals = [], []
        for a in range(n):
            for k, (px, py) in enumerate([(1 - x, y), (x, 1 - y), (1 - x, 1 - y)]):
                sem = dict(send_sem=send_sems.at[3 * a + k], recv_sem=recv_sems.at[3 * a + k], device_id=(x, y, 1 - c),
                           device_id_type=MESH)
                row = 2 * (2 * px + py)
                sends.append(pltpu.make_async_remote_copy(src_ref=ins[a].at[row + c], dst_ref=outs[a].at[row + c], **sem))
                arrivals.append(pltpu.make_async_remote_copy(src_ref=ins[a].at[row + c], dst_ref=outs[a].at[row + 1 - c], **sem))
        for cp in sends:
            cp.start()
        for cp in sends:
            cp.wait_send()
        for cp in arrivals:
            cp.wait_recv()

    return pl.pallas_call(
        body, name=name, in_specs=[ANY] * n, out_specs=[ANY] * n,
        out_shape=[jax.ShapeDtypeStruct(a.shape, a.dtype) for a in arrays],
        input_output_aliases={a: a for a in range(n)},
        scratch_shapes=[pltpu.SemaphoreType.DMA((3 * n,)), pltpu.SemaphoreType.DMA((3 * n,))],
    )(*arrays)


def kernel(x, c, ada_w, ada_b, mix_norm, ffn_norm, par_w_in, par_w_out, hg_lb_logits, hg_out_norm, sg_w_in, sg_v_gain, sg_v_bias, sg_w_pos, sg_b_pos, sg_w_out, ffn_up, ffn_conv_w, ffn_conv_b, ffn_down, final_norm, loss_target, m_ada_w, m_ada_b, m_mix_norm, m_ffn_norm, m_par_w_in, m_par_w_out, m_hg_lb_logits, m_hg_out_norm, m_sg_w_in, m_sg_v_gain, m_sg_v_bias, m_sg_w_pos, m_sg_b_pos, m_sg_w_out, m_ffn_up, m_ffn_conv_w, m_ffn_conv_b, m_ffn_down, m_final_norm, v_ada_w, v_ada_b, v_mix_norm, v_ffn_norm, v_par_w_in, v_par_w_out, v_hg_lb_logits, v_hg_out_norm, v_sg_w_in, v_sg_v_gain, v_sg_v_bias, v_sg_w_pos, v_sg_b_pos, v_sg_w_out, v_ffn_up, v_ffn_conv_w, v_ffn_conv_b, v_ffn_down, v_final_norm):
    names = ["ada_w", "ada_b", "mix_norm", "ffn_norm", "par_w_in", "par_w_out", "hg_lb_logits", "hg_out_norm", "sg_w_in",
             "sg_v_gain", "sg_v_bias", "sg_w_pos", "sg_b_pos", "sg_w_out", "ffn_up", "ffn_conv_w", "ffn_conv_b",
             "ffn_down", "final_norm"]
    W = dict(zip(names, [ada_w, ada_b, mix_norm, ffn_norm, par_w_in, par_w_out, hg_lb_logits, hg_out_norm, sg_w_in,
                         sg_v_gain, sg_v_bias, sg_w_pos, sg_b_pos, sg_w_out, ffn_up, ffn_conv_w, ffn_conv_b, ffn_down,
                         final_norm]))
    M = dict(zip(names, [m_ada_w, m_ada_b, m_mix_norm, m_ffn_norm, m_par_w_in, m_par_w_out, m_hg_lb_logits, m_hg_out_norm,
                         m_sg_w_in, m_sg_v_gain, m_sg_v_bias, m_sg_w_pos, m_sg_b_pos, m_sg_w_out, m_ffn_up, m_ffn_conv_w,
                         m_ffn_conv_b, m_ffn_down, m_final_norm]))
    V = dict(zip(names, [v_ada_w, v_ada_b, v_mix_norm, v_ffn_norm, v_par_w_in, v_par_w_out, v_hg_lb_logits, v_hg_out_norm,
                         v_sg_w_in, v_sg_v_gain, v_sg_v_bias, v_sg_w_pos, v_sg_b_pos, v_sg_w_out, v_ffn_up, v_ffn_conv_w,
                         v_ffn_conv_b, v_ffn_down, v_final_norm]))

    x = x[0]
    target = loss_target[0]
    T, D = x.shape
    ix, iy, ic = _place()
    chip = 2 * ix + iy
    dev = 2 * chip + ic
    H = hg_out_norm.shape[1]
    SBW = H * HEAD
    NA = ada_w.shape[2]
    F2s = ffn_up.shape[2]
    F2 = N_CHIP * F2s
    SGW = sg_w_out.shape[1] * N_CHIP
    G = sg_w_pos.shape[1]

    shards = [par_w_in[0], par_w_out[0], sg_w_in[0], sg_w_out[0], ffn_up[0], ffn_up[1], ffn_down[0], ffn_down[1]]
    kinds = ["col", "row", "col", "row", "col", "col", "row", "row"]
    rows8 = [_cast_into_rows("cast_w", w.reshape(2, w.shape[0] // 2, w.shape[1]), chip) for w in shards]
    groups = {"a": [0], "b": [1, 4, 6], "c": [2, 3, 5, 7]}
    started = {"a": _chips_start("gather_start_a", "rows", [rows8[i] for i in groups["a"]])}

    def weights_of(g, after):
        send_sems, recv_sems, bufs, _ = started[g]
        bufs = _chips_wait("gather_wait_" + g, "rows", len(bufs), send_sems, recv_sems, bufs, after)
        out = {}
        for i, g8 in zip(groups[g], _fill_from_sibling("gather_fill_" + g, bufs)):
            K, N = shards[i].shape
            out[i] = g8.reshape(N_CHIP, K, N) if kinds[i] == "col" else g8.reshape(N_CHIP * K, N)
        return out

    n_cw = ffn_conv_w.size
    n_sv = sg_v_gain.size
    c_all, small_all = _all_gather("gather_small", [c, _pack_rows([ffn_conv_w, sg_v_gain, sg_v_bias])],
                                   after=started["a"][3])
    c_all = c_all.reshape(N_DEV, D)
    small_all = small_all.reshape(N_CHIP, 2, -1)[:, 0]
    conv_w_full = small_all[:, :n_cw].reshape(N_CHIP, 2, CONV_WIDTH, F2s).transpose(1, 2, 0, 3).reshape(2, CONV_WIDTH, F2)
    sg_gain_full = small_all[:, n_cw:n_cw + n_sv].reshape(1, SGW)
    sg_bias_full = small_all[:, n_cw + n_sv:n_cw + 2 * n_sv].reshape(1, SGW)

    c_pad = jnp.pad(c_all, ((0, 16 - N_DEV), (0, 0)))
    ada_b_sh = lax.dynamic_slice(ada_b, (0, chip * NA), (2, NA)).reshape(2, 1, NA)
    mod_sh = _ada_fwd(c_pad, ada_w, ada_b_sh)
    mod_all, = _all_gather("gather_mod", [mod_sh[:, :N_DEV]])
    mod_all = mod_all.reshape(N_CHIP, 2, 2, N_DEV, NA)[:, 0]
    mod = lax.dynamic_index_in_dim(mod_all, dev, axis=2, keepdims=False)
    mod = mod.transpose(1, 0, 2).reshape(2, 6, D)
    mods = [[mod[l, k].reshape(1, D) for k in range(6)] for l in range(2)]
    for g in ("b", "c"):
        started[g] = _chips_start("gather_start_" + g, "rows", [rows8[i] for i in groups[g]], after=mod)
    start_token = sum(st[3][0, 0] for st in started.values())

    vec = lambda a: a.reshape(1, -1)
    l0 = vec(hg_lb_logits[0])
    l1 = vec(hg_lb_logits[1])
    hg_gain = vec(hg_out_norm[0])
    wpos = sg_w_pos[0]
    bpos = sg_b_pos[0].reshape(G, SG_CHUNK, 1)
    conv_b = [vec(ffn_conv_b[l]) for l in range(2)]

    sh1, sc1, g1, sh2, sc2, g2 = mods[0]
    w_in = weights_of("a", mod)[0]
    h0 = _normmod_fwd("norm_mix0", x, vec(mix_norm[0]) + start_token, sc1, sh1)
    proj = _mm_nn("mm_par_in", h0, w_in)
    o_sb, sb_tot = _sb_fwd(proj, H)
    o_hg, hg_states = _hg_fwd(proj, l0, l1, hg_gain, H, 3 * H)
    o_cat = jnp.concatenate([o_sb, o_hg], axis=1)
    wb = weights_of("b", o_cat)
    w_out, wup, wdn = wb[1], [wb[4], None], [wb[6], None]
    y0 = _mm_nn("mm_par_out", o_cat, w_out)
    x1, h0f = _res_normmod_fwd("res_norm_ffn0", x, y0, g1, vec(ffn_norm[0]), sc2, sh2)
    a0 = _mm_nn("mm_up0", h0f, wup[0])
    u0 = _conv_fwd("conv_fwd0", a0, conv_w_full[0], conv_b[0])
    f0 = _mm_nn("mm_down0", u0, wdn[0])
    sh1b, sc1b, g1b, sh2b, sc2b, g2b = mods[1]
    x2, h1 = _res_normmod_fwd("res_norm_mix1", x1, f0, g2, vec(mix_norm[1]), sc1b, sh1b)
    wc = weights_of("c", h1)
    wsg_in, wsg_out, wup[1], wdn[1] = wc[2], wc[3], wc[5], wc[7]
    zpre = _mm_nn("mm_sg_in", h1, wsg_in)
    s1 = _sg_fwd(zpre, sg_gain_full, sg_bias_full, wpos, bpos)
    y1 = _mm_nn("mm_sg_out", s1, wsg_out)
    x3, h1f = _res_normmod_fwd("res_norm_ffn1", x2, y1, g1b, vec(ffn_norm[1]), sc2b, sh2b)
    a1 = _mm_nn("mm_up1", h1f, wup[1])
    u1 = _conv_fwd("conv_fwd1", a1, conv_w_full[1], conv_b[1])
    f1 = _mm_nn("mm_down1", u1, wdn[1])
    loss_sum, dx, df1, dg2b, d_final = _final_fwd_bwd(x3, f1, g2b, vec(final_norm), target)
    loss = lax.psum(loss_sum[0, 0], ("x", "y", "c"))

    def reduce_start(tag, idx, grads):
        eights = [g.reshape((N_DEV, -1, g.shape[-1])) for g in grads]
        from_sib = _to_sibling("grads_to_sibling_" + tag, eights, "other_half")
        own = [lax.dynamic_index_in_dim(e.reshape((N_CHIP, 2) + e.shape[1:]), ic, axis=1, keepdims=False) for e in eights]
        pair = [_add_pairs("add_pair", o, r) for o, r in zip(own, from_sib)]
        landing = [lax.empty(p.shape, p.dtype) for p in pair]
        send_sems, recv_sems, bufs, token = _chips_start("grads_start_" + tag, "parts", pair, landing)
        return (tag, idx, send_sems, recv_sems, bufs), token[0:1, 0:1]

    def reduce_finish(state, after):
        tag, idx, send_sems, recv_sems, bufs = state
        n = len(idx)
        bufs = _chips_wait("grads_wait_" + tag, "parts", n, send_sems, recv_sems, bufs, after)
        halves = [_sum_into_pair("sum_chips", p, x_, ic, chip) for p, x_ in zip(bufs[:n], bufs[n:])]
        both = _share_halves("grads_share_" + tag, halves)
        return {i: b.reshape(shards[i].shape) for i, b in zip(idx, both)}

    def ffn_bwd(l, dfl, u, a, hf):
        g_dn = _mm_tn("mm_g_down", u, dfl)
        du = _mm_nt("mm_d_u", dfl, wdn[l])
        da_g, da_v, dcw, dcb = _conv_bwd("conv_bwd", a, du, conv_w_full[l], conv_b[l])
        da = jnp.concatenate([da_g, da_v], axis=1)
        g_up = _mm_tn("mm_g_up", hf, da, chunks=N_CHIP)
        dh = _mm_nt("mm_d_hf", da, wup[l])
        return g_dn, g_up, dcw, dcb, dh

    g_dn1, g_up1, dcw1, dcb1, dh1f = ffn_bwd(1, df1, u1, a1, h1f)
    red1, tok = reduce_start("1", [5, 7], [g_up1, g_dn1])
    dx, dgn_f1, dsc2b, dsh2b, dy1, dg1b = _block_bwd("bwd_ffn1", dx, dh1f, x3, vec(ffn_norm[1]) + tok, sc2b, sh2b, y1, g1b)
    g_sg_out = _mm_tn("mm_g_sg_out", s1, dy1)
    ds1 = _mm_nt("mm_d_s", dy1, wsg_out)
    dzpre, dsg_gain, dsg_bias, dwpos, dbpos = _sg_bwd(zpre, ds1, sg_gain_full, sg_bias_full, wpos, bpos)
    g_sg_in = _mm_tn("mm_g_sg_in", h1, dzpre, chunks=N_CHIP)
    dh1 = _mm_nt("mm_d_h1", dzpre, wsg_in)
    red2, tok = reduce_start("2", [2, 3], [g_sg_in, g_sg_out])
    dx, dgn_m1, dsc1b, dsh1b, df0, dg2 = _block_bwd("bwd_mix1", dx, dh1, x2, vec(mix_norm[1]) + tok, sc1b, sh1b, f0, g2)
    g_dn0, g_up0, dcw0, dcb0, dh0f = ffn_bwd(0, df0, u0, a0, h0f)
    red3, tok = reduce_start("3", [4, 6], [g_up0, g_dn0])
    dx, dgn_f0, dsc2, dsh2, dy0, dg1 = _block_bwd("bwd_ffn0", dx, dh0f, x1, vec(ffn_norm[0]) + tok, sc2, sh2, y0, g1)
    g_out = _mm_tn("mm_g_par_out", o_cat, dy0)
    do = _mm_nt("mm_d_o", dy0, w_out)
    dq, dk, dv = _sb_bwd(proj, do, sb_tot, H)
    dhq, dhf, dhi, dhg, dl0, dl1, dhg_gain = _hg_bwd(proj, hg_states, do, l0, l1, hg_gain, H, 3 * H, H)
    dproj = jnp.concatenate([dq, dk, dv, dhq, dhf, dhi, dhg], axis=1).astype(BF16)
    g_in = _mm_tn("mm_g_par_in", h0, dproj, chunks=N_CHIP)
    dh0 = _mm_nt("mm_d_h0", dproj, w_in)
    red4, tok = reduce_start("4", [0, 1], [g_in, g_out])
    grad_x, dgn_m0, dsc1, dsh1 = _block_bwd("bwd_mix0", dx, dh0, x, vec(mix_norm[0]) + tok, sc1, sh1)

    G_, delta, new_m, new_v = {}, {}, {}, {}

    def adam_on(nme):
        shp = W[nme].shape
        r2 = lambda a: a.reshape(-1, shp[-1])
        d_, m_, v_ = _adam("adam_" + nme, r2(W[nme]), r2(G_[nme]), r2(M[nme]), r2(V[nme]))
        delta[nme], new_m[nme], new_v[nme] = d_.reshape(shp), m_.reshape(shp), v_.reshape(shp)

    g_shards = {}
    for state in (red1, red2, red3):
        g_shards.update(reduce_finish(state, grad_x))
    G_["sg_w_in"] = g_shards[2][None]
    G_["sg_w_out"] = g_shards[3][None]
    G_["ffn_up"] = jnp.stack([g_shards[4], g_shards[5]])
    G_["ffn_down"] = jnp.stack([g_shards[6], g_shards[7]])
    for nme in ["sg_w_in", "sg_w_out", "ffn_up", "ffn_down"]:
        adam_on(nme)

    dmod = jnp.concatenate([dsh1, dsc1, dg1, dsh2, dsc2, dg2, dsh1b, dsc1b, dg1b, dsh2b, dsc2b, dg2b], axis=1)
    parts = [dmod, dgn_m0, dgn_m1, dgn_f0, dgn_f1, dl0, dl1, dhg_gain, dsg_gain, dsg_bias, dwpos, dbpos,
             dcw0, dcw1, dcb0, dcb1, d_final]
    sizes = [p.size for p in parts]
    packed = _pack_rows(parts)
    packed_all, = _all_gather("gather_small_grads", [packed], after=new_v["ffn_down"])
    summed = _sum_leading("sum_small_grads", packed_all).reshape(-1)
    offs = [0]
    for s in sizes:
        offs.append(offs[-1] + s)
    red = [summed[offs[i]:offs[i + 1]] for i in range(len(parts))]
    (r_dmod, r_gm0, r_gm1, r_gf0, r_gf1, r_l0, r_l1, r_hgain, r_sgain, r_sbias, r_wpos, r_bpos,
     r_cw0, r_cw1, r_cb0, r_cb1, r_final) = red
    n_mod = sizes[0]
    dmod_all = packed_all.reshape(N_DEV, -1)[:, :n_mod].reshape(N_DEV, 2, 6 * D)

    G_["ada_b"] = r_dmod.reshape(2, 6 * D)
    G_["mix_norm"] = jnp.stack([r_gm0, r_gm1])
    G_["ffn_norm"] = jnp.stack([r_gf0, r_gf1])
    G_["hg_lb_logits"] = jnp.stack([r_l0, r_l1])
    G_["hg_out_norm"] = r_hgain.reshape(hg_out_norm.shape)
    G_["sg_v_gain"] = lax.dynamic_slice(r_sgain, (chip * n_sv,), (n_sv,)).reshape(sg_v_gain.shape)
    G_["sg_v_bias"] = lax.dynamic_slice(r_sbias, (chip * n_sv,), (n_sv,)).reshape(sg_v_bias.shape)
    G_["sg_w_pos"] = r_wpos.reshape(sg_w_pos.shape)
    G_["sg_b_pos"] = r_bpos.reshape(sg_b_pos.shape)
    cw_full = jnp.stack([r_cw0.reshape(CONV_WIDTH, F2), r_cw1.reshape(CONV_WIDTH, F2)])
    G_["ffn_conv_w"] = lax.dynamic_slice(cw_full, (0, 0, chip * F2s), (2, CONV_WIDTH, F2s))
    G_["ffn_conv_b"] = jnp.stack([r_cb0, r_cb1])
    G_["final_norm"] = r_final

    c_t = jnp.pad(c_all, ((0, HEAD - N_DEV), (0, 0))).T
    dmod_sh = lax.dynamic_slice(dmod_all.transpose(1, 0, 2), (0, 0, chip * NA), (2, N_DEV, NA))
    dmod_sh = jnp.pad(dmod_sh, ((0, 0), (0, HEAD - N_DEV), (0, 0)))
    G_["ada_w"], delta["ada_w"], new_m["ada_w"], new_v["ada_w"] = _ada_grad_adam(c_t, dmod_sh, ada_w, m_ada_w, v_ada_w)

    g_shards.update(reduce_finish(red4, G_["ada_w"]))
    G_["par_w_in"] = g_shards[0][None]
    G_["par_w_out"] = g_shards[1][None]
    for nme in ["par_w_in", "par_w_out"]:
        adam_on(nme)
    small = [n_ for n_ in names if n_ not in delta]
    pk = lambda dct: _pack_rows([dct[n_] for n_ in small])
    d_, m_, v_ = _adam("adam_small", pk(W), pk(G_), pk(M), pk(V))
    off = 0
    for n_ in small:
        sz = W[n_].size
        for dst, src in ((delta, d_), (new_m, m_), (new_v, v_)):
            dst[n_] = src.reshape(-1)[off:off + sz].reshape(W[n_].shape)
        off += sz

    return (loss, grad_x[None], *[G_[n_] for n_ in names], *[delta[n_] for n_ in names],
            *[new_m[n_] for n_ in names], *[new_v[n_] for n_ in names])
```

```python
import functools
import math

import jax
import jax.numpy as jnp
from jax import lax
from jax.experimental import pallas as pl
from jax.experimental.pallas import tpu as pltpu

F32 = jnp.float32
BF16 = jnp.bfloat16
MESH = pl.DeviceIdType.MESH
ANY = pl.BlockSpec(memory_space=pl.ANY)

NORM_EPS = 1e-6
ADAM_LR = 0.001
ADAM_B1 = 0.9
ADAM_B2 = 0.999
ADAM_EPS = 1e-08
ADAM_WD = 0.01
ADAM_STEP = 10
CONV_WIDTH = 3
HEAD = 128
HG_CHUNK = 64
SG_CHUNK = 128
N_DEV = 8
N_CHIP = 4
V7X_VMEM_LIMIT = 56 * 1024 * 1024


def _cp(*sem):
    return pltpu.CompilerParams(dimension_semantics=sem if sem else None, vmem_limit_bytes=V7X_VMEM_LIMIT)


def _pick(n, prefs):
    for p in prefs:
        if p <= n and n % p == 0:
            return p
    return n


def _iota(shape, axis):
    return lax.broadcasted_iota(jnp.int32, shape, axis)


def _rows_within(R, row_bytes, budget):
    if R * row_bytes <= budget:
        return R
    for t in (1024, 512, 256, 128, 64, 32, 16):
        if R % t == 0 and t * row_bytes <= budget:
            return t
    return _pick(R, (16, 8))


def _pack_rows(arrays):
    flat = jnp.concatenate([a.reshape(-1) for a in arrays])
    pad = (-flat.size) % (8 * HEAD)
    return jnp.pad(flat, (0, pad)).reshape(-1, HEAD)


def _dg(a, b, ca, cb):
    return lax.dot_general(a.astype(BF16), b.astype(BF16), (((ca,), (cb,)), ((), ())), preferred_element_type=F32)


@jax.custom_vjp
def mm_nn(a, b):
    return _dg(a, b, 1, 0)


mm_nn.defvjp(lambda a, b: (_dg(a, b, 1, 0), (a, b)),
             lambda r, g: (_dg(g, r[1], 1, 1), _dg(r[0], g, 0, 0)))


@jax.custom_vjp
def mm_nt(a, b):
    return _dg(a, b, 1, 1)


mm_nt.defvjp(lambda a, b: (_dg(a, b, 1, 1), (a, b)),
             lambda r, g: (_dg(g, r[1], 1, 0), _dg(g, r[0], 0, 0)))


@jax.custom_vjp
def mm_tn(a, b):
    return _dg(a, b, 0, 0)


mm_tn.defvjp(lambda a, b: (_dg(a, b, 0, 0), (a, b)),
             lambda r, g: (_dg(r[1], g, 1, 1), _dg(r[0], g, 1, 0)))


def _split(x):
    hi = x.astype(BF16)
    lo = (x - hi.astype(F32)).astype(BF16)
    return hi, lo


def _sum_right(x, m01):
    hi, lo = _split(x)
    return _dg(hi, m01, 1, 0) + _dg(lo, m01, 1, 0)


def _sum_left_impl(m01, x, ca):
    hi, lo = _split(x)
    return _dg(m01, hi, ca, 0) + _dg(m01, lo, ca, 0)


@jax.custom_vjp
def _sum_left(m01, x):
    return _sum_left_impl(m01, x, 1)


_sum_left.defvjp(lambda m, x: (_sum_left_impl(m, x, 1), m),
                 lambda m, g: (None, _sum_left_impl(m, g, 0)))


def _sigmoid(x):
    return 1.0 / (1.0 + jnp.exp(-x))


def _softplus(z):
    return jnp.maximum(z, 0.0) + jnp.log(1.0 + jnp.exp(-jnp.abs(z)))


_INV_SQRT2 = 1.0 / math.sqrt(2.0)
_INV_SQRT2PI = 1.0 / math.sqrt(2.0 * math.pi)


@jax.custom_vjp
def _gelu(x):
    return 0.5 * x * (1.0 + lax.erf(x * _INV_SQRT2))


_gelu.defvjp(lambda x: (0.5 * x * (1.0 + lax.erf(x * _INV_SQRT2)), x),
             lambda x, g: (g * (0.5 * (1.0 + lax.erf(x * _INV_SQRT2)) + x * jnp.exp(-0.5 * x * x) * _INV_SQRT2PI),))


def _rms(x, gain):
    r = lax.rsqrt(jnp.mean(x * x, axis=-1, keepdims=True) + NORM_EPS)
    return x * r * gain


def _normmod(x, gain, sc, sh):
    return _rms(x, gain) * (1.0 + sc) + sh


def _mm_call(name, a, b, out_shape, out_dtype, dims, grid, a_spec, b_spec, o_spec, acc_shape):
    nk = grid[2]

    def body(a_ref, b_ref, o_ref, *scratch):
        part = lax.dot_general(a_ref[...].astype(BF16), b_ref[...].astype(BF16), dims, preferred_element_type=F32)
        if nk == 1:
            o_ref[...] = part.astype(o_ref.dtype)
            return
        acc_ref, = scratch
        k = pl.program_id(2)

        @pl.when(k == 0)
        def _():
            acc_ref[...] = part

        @pl.when(k > 0)
        def _():
            acc_ref[...] += part

        @pl.when(k == nk - 1)
        def _():
            o_ref[...] = acc_ref[...].astype(o_ref.dtype)

    return pl.pallas_call(
        body, name=name, grid=grid, in_specs=[a_spec, b_spec], out_specs=o_spec,
        out_shape=jax.ShapeDtypeStruct(out_shape, out_dtype),
        scratch_shapes=[] if nk == 1 else [pltpu.VMEM(acc_shape, F32)],
        compiler_params=_cp("parallel", "parallel", "arbitrary"),
    )(a, b)


def _mm_nn(name, a, b, out_dtype=F32):
    M, K = a.shape
    chunked = b.ndim == 3
    Nc = b.shape[-1]
    N = Nc * (b.shape[0] if chunked else 1)
    tm = _pick(M, (1024, 512, 256, 128, 64, 32, 16, 8))
    tn = _pick(Nc, (1408, 1024, 896, 512, 256, 128))
    tk = _pick(K, (2048, 1408, 1024, 512, 256, 128))
    npc = Nc // tn
    if chunked:
        b_spec = pl.BlockSpec((None, tk, tn), lambda i, j, k: (j // npc, k, j % npc))
    else:
        b_spec = pl.BlockSpec((tk, tn), lambda i, j, k: (k, j))
    return _mm_call(name, a, b, (M, N), out_dtype, (((1,), (0,)), ((), ())), (M // tm, N // tn, K // tk),
                    pl.BlockSpec((tm, tk), lambda i, j, k: (i, k)), b_spec,
                    pl.BlockSpec((tm, tn), lambda i, j, k: (i, j)), (tm, tn))


def _mm_nt(name, a, b, out_dtype=F32):
    M, N = a.shape
    chunked = b.ndim == 3
    Nc = b.shape[-1]
    K = b.shape[-2]
    tm = _pick(M, (1024, 512, 256, 128, 64, 32, 16, 8))
    tn = _pick(K, (1408, 1024, 512, 256, 128))
    tk = _pick(Nc, (2048, 1792, 1408, 1024, 896, 512, 256, 128))
    npc = Nc // tk
    if chunked:
        b_spec = pl.BlockSpec((None, tn, tk), lambda i, j, k: (k // npc, j, k % npc))
    else:
        b_spec = pl.BlockSpec((tn, tk), lambda i, j, k: (j, k))
    return _mm_call(name, a, b, (M, K), out_dtype, (((1,), (1,)), ((), ())), (M // tm, K // tn, N // tk),
                    pl.BlockSpec((tm, tk), lambda i, j, k: (i, k)), b_spec,
                    pl.BlockSpec((tm, tn), lambda i, j, k: (i, j)), (tm, tn))


def _mm_tn(name, a, b, chunks=1, out_dtype=BF16):
    T, K = a.shape
    N = b.shape[1]
    Nc = N // chunks
    tm = _pick(K, (1408, 1024, 512, 256, 128))
    tn = _pick(Nc, (1408, 1024, 896, 512, 256, 128))
    tk = _pick(T, (1024, 512, 256, 128))
    npc = Nc // tn
    if chunks > 1:
        shape = (chunks, K, Nc)
        o_spec = pl.BlockSpec((None, tm, tn), lambda i, j, k: (j // npc, i, j % npc))
    else:
        shape = (K, N)
        o_spec = pl.BlockSpec((tm, tn), lambda i, j, k: (i, j))
    return _mm_call(name, a, b, shape, out_dtype, (((0,), (0,)), ((), ())), (K // tm, N // tn, T // tk),
                    pl.BlockSpec((tk, tm), lambda i, j, k: (k, i)),
                    pl.BlockSpec((tk, tn), lambda i, j, k: (k, j)), o_spec, (tm, tn))


def _row_tile(T):
    return _pick(T, (256, 128, 64, 32, 16, 8))


def _vec_spec(D):
    return pl.BlockSpec((1, D), lambda i: (0, 0))


def _normmod_fwd(name, x, gain, sc, sh):
    T, D = x.shape
    bt = _row_tile(T)

    def body(x_ref, g_ref, sc_ref, sh_ref, h_ref):
        h_ref[...] = _normmod(x_ref[...], g_ref[...], sc_ref[...], sh_ref[...]).astype(h_ref.dtype)

    rows = pl.BlockSpec((bt, D), lambda i: (i, 0))
    return pl.pallas_call(body, name=name, grid=(T // bt,), in_specs=[rows] + [_vec_spec(D)] * 3, out_specs=rows,
                          out_shape=jax.ShapeDtypeStruct((T, D), BF16), compiler_params=_cp("parallel"))(x, gain, sc, sh)


def _res_normmod_fwd(name, x, y, g, gain, sc, sh):
    T, D = x.shape
    bt = _row_tile(T)

    def body(x_ref, y_ref, gate_ref, g_ref, sc_ref, sh_ref, x1_ref, h_ref):
        x1 = x_ref[...] + gate_ref[...] * y_ref[...]
        x1_ref[...] = x1
        h_ref[...] = _normmod(x1, g_ref[...], sc_ref[...], sh_ref[...]).astype(h_ref.dtype)

    rows = pl.BlockSpec((bt, D), lambda i: (i, 0))
    return pl.pallas_call(body, name=name, grid=(T // bt,), in_specs=[rows, rows] + [_vec_spec(D)] * 4,
                          out_specs=[rows, rows],
                          out_shape=[jax.ShapeDtypeStruct((T, D), F32), jax.ShapeDtypeStruct((T, D), BF16)],
                          compiler_params=_cp("parallel"))(x, y, g, gain, sc, sh)


def _final_fwd_bwd(x, y, g, gain, target):
    T, D = x.shape
    bt = _row_tile(T)

    def body(x_ref, y_ref, gate_ref, g_ref, t_ref, loss_ref, dx_ref, dy_ref, dgate_ref, dgain_ref):
        i = pl.program_id(0)
        yv = y_ref[...]
        gate = gate_ref[...]
        x4 = x_ref[...] + gate * yv
        out, vjp = jax.vjp(_rms, x4, g_ref[...])
        err = out - t_ref[...]
        dx4, dgain = vjp(err * (1.0 / D))
        part = 0.5 * jnp.sum(jnp.mean(err * err, axis=-1, keepdims=True), axis=0, keepdims=True)

        @pl.when(i == 0)
        def _():
            loss_ref[...] = jnp.zeros_like(loss_ref)
            dgate_ref[...] = jnp.zeros_like(dgate_ref)
            dgain_ref[...] = jnp.zeros_like(dgain_ref)

        loss_ref[...] += jnp.broadcast_to(part, loss_ref.shape)
        dx_ref[...] = dx4
        dy_ref[...] = (gate * dx4).astype(dy_ref.dtype)
        dgate_ref[...] += jnp.sum(dx4 * yv, axis=0, keepdims=True)
        dgain_ref[...] += dgain

    rows = pl.BlockSpec((bt, D), lambda i: (i, 0))
    vec = _vec_spec(D)
    return pl.pallas_call(
        body, name="final_loss", grid=(T // bt,), in_specs=[rows, rows, vec, vec, rows],
        out_specs=[pl.BlockSpec((1, HEAD), lambda i: (0, 0)), rows, rows, vec, vec],
        out_shape=[jax.ShapeDtypeStruct((1, HEAD), F32), jax.ShapeDtypeStruct((T, D), F32),
                   jax.ShapeDtypeStruct((T, D), BF16), jax.ShapeDtypeStruct((1, D), F32),
                   jax.ShapeDtypeStruct((1, D), F32)],
        compiler_params=_cp("arbitrary"))(x, y, g, gain, target)


def _block_bwd(name, dx_out, dh, x_in, gain, sc, sh, y_prev=None, g_prev=None):
    T, D = x_in.shape
    bt = _row_tile(T)
    has_prev = y_prev is not None

    def body(*refs):
        if has_prev:
            dxo_ref, dh_ref, x_ref, g_ref, sc_ref, sh_ref, y_ref, gp_ref, dx_ref, dgain_ref, dsc_ref, dsh_ref, dy_ref, dgp_ref = refs
        else:
            dxo_ref, dh_ref, x_ref, g_ref, sc_ref, sh_ref, dx_ref, dgain_ref, dsc_ref, dsh_ref = refs
        i = pl.program_id(0)
        _, vjp = jax.vjp(_normmod, x_ref[...], g_ref[...], sc_ref[...], sh_ref[...])
        dxn, dgain, dsc, dsh = vjp(dh_ref[...])
        dx = dxo_ref[...] + dxn
        dx_ref[...] = dx

        @pl.when(i == 0)
        def _():
            dgain_ref[...] = jnp.zeros_like(dgain_ref)
            dsc_ref[...] = jnp.zeros_like(dsc_ref)
            dsh_ref[...] = jnp.zeros_like(dsh_ref)
            if has_prev:
                dgp_ref[...] = jnp.zeros_like(dgp_ref)

        dgain_ref[...] += dgain
        dsc_ref[...] += dsc
        dsh_ref[...] += dsh
        if has_prev:
            dy_ref[...] = (gp_ref[...] * dx).astype(dy_ref.dtype)
            dgp_ref[...] += jnp.sum(dx * y_ref[...], axis=0, keepdims=True)

    rows = pl.BlockSpec((bt, D), lambda i: (i, 0))
    vec = _vec_spec(D)
    ins = [dx_out, dh, x_in, gain, sc, sh]
    in_specs = [rows, rows, rows, vec, vec, vec]
    out_specs = [rows, vec, vec, vec]
    out_shape = [jax.ShapeDtypeStruct((T, D), F32)] + [jax.ShapeDtypeStruct((1, D), F32)] * 3
    if has_prev:
        ins += [y_prev, g_prev]
        in_specs += [rows, vec]
        out_specs += [rows, vec]
        out_shape += [jax.ShapeDtypeStruct((T, D), BF16), jax.ShapeDtypeStruct((1, D), F32)]
    return pl.pallas_call(body, name=name, grid=(T // bt,), in_specs=in_specs, out_specs=out_specs,
                          out_shape=out_shape, compiler_params=_cp("arbitrary"))(*ins)


def _sb_tiles(T):
    tq = _pick(T, (512, 256, 128))
    return tq, tq // HEAD


def _sb_fwd(proj, H):
    T = proj.shape[0]
    tq, nsub = _sb_tiles(T)
    scale = HEAD ** -0.5

    def body(q_ref, k_ref, v_ref, o_ref, l_ref, acc_ref):
        i = pl.program_id(1)
        q = q_ref[...].astype(BF16)
        later = (_iota((HEAD, HEAD), 0) > _iota((HEAD, HEAD), 1)).astype(BF16)
        row = _iota((tq, HEAD), 0)
        col = _iota((tq, HEAD), 1)

        def key_step(j, c, diagonal):
            off = pl.multiple_of(j * tq, tq)
            k = k_ref[pl.ds(off, tq), :].astype(BF16)
            v = v_ref[pl.ds(off, tq), :].astype(BF16)
            z = _dg(q, k, 1, 1) * scale
            ws = [None] * nsub
            for s in reversed(range(nsub)):
                zs = z[:, s * HEAD:(s + 1) * HEAD]
                sp = _softplus(zs)
                if diagonal:
                    strict = (s * HEAD + col) < row
                    lk = jnp.where(strict, -sp, 0.0)
                else:
                    lk = -sp
                w = jnp.exp(zs - sp + _sum_right(lk, later) + c)
                if diagonal:
                    w = jnp.where(strict, w, 0.0)
                ws[s] = w.astype(BF16)
                c = c + jnp.sum(lk, axis=1, keepdims=True)
            acc_ref[...] += _dg(jnp.concatenate(ws, axis=1), v, 1, 0)
            return c

        acc_ref[...] = jnp.zeros_like(acc_ref)
        c = key_step(i, jnp.zeros((tq, 1), F32), True)
        c = lax.fori_loop(0, i, lambda n, c: key_step(i - 1 - n, c, False), c)
        o_ref[...] = acc_ref[...].astype(o_ref.dtype)
        l_ref[...] = jnp.broadcast_to(c, (tq, HEAD))

    blk = pl.BlockSpec((tq, HEAD), lambda h, i: (i, h))
    return pl.pallas_call(
        body, name="sb_fwd", grid=(H, T // tq),
        in_specs=[blk, pl.BlockSpec((T, HEAD), lambda h, i: (0, H + h)), pl.BlockSpec((T, HEAD), lambda h, i: (0, 2 * H + h))],
        out_specs=[blk, blk],
        out_shape=[jax.ShapeDtypeStruct((T, H * HEAD), BF16), jax.ShapeDtypeStruct((T, H * HEAD), F32)],
        scratch_shapes=[pltpu.VMEM((tq, HEAD), F32)],
        compiler_params=_cp("parallel", "arbitrary"))(proj, proj, proj)


def _sb_bwd(proj, do, L, H):
    T = proj.shape[0]
    tq, nsub = _sb_tiles(T)
    scale = HEAD ** -0.5

    def body(q_ref, k_ref, v_ref, do_ref, l_ref, dq_ref, dk_ref, dv_ref):
        i = pl.program_id(1)

        @pl.when(i == 0)
        def _():
            dk_ref[...] = jnp.zeros_like(dk_ref)
            dv_ref[...] = jnp.zeros_like(dv_ref)

        dq_ref[...] = jnp.zeros_like(dq_ref)
        q = q_ref[...].astype(BF16)
        do_ = do_ref[...].astype(BF16)
        total = l_ref[...]
        upto = (_iota((HEAD, HEAD), 0) <= _iota((HEAD, HEAD), 1)).astype(BF16)
        before = (_iota((HEAD, HEAD), 0) < _iota((HEAD, HEAD), 1)).astype(BF16)
        row = _iota((tq, HEAD), 0)
        col = _iota((tq, HEAD), 1)

        def key_step(j, carry, diagonal):
            cp, ce = carry
            off = pl.multiple_of(j * tq, tq)
            k = k_ref[pl.ds(off, tq), :].astype(BF16)
            v = v_ref[pl.ds(off, tq), :].astype(BF16)
            z = _dg(q, k, 1, 1) * scale
            dw = _dg(do_, v, 1, 1)
            ws, dzs = [], []
            for s in range(nsub):
                zs = z[:, s * HEAD:(s + 1) * HEAD]
                sp = _softplus(zs)
                if diagonal:
                    strict = (s * HEAD + col) < row
                    lk = jnp.where(strict, -sp, 0.0)
                else:
                    lk = -sp
                tail = total - (_sum_right(lk, upto) + cp)
                w = jnp.exp(zs - sp + tail)
                if diagonal:
                    w = jnp.where(strict, w, 0.0)
                e = w * dw[:, s * HEAD:(s + 1) * HEAD]
                e_before = _sum_right(e, before) + ce
                sig = jnp.exp(zs - sp)
                dz = (e * (1.0 - sig) - e_before * sig) * scale
                if diagonal:
                    dz = jnp.where(strict, dz, 0.0)
                ws.append(w.astype(BF16))
                dzs.append(dz.astype(BF16))
                cp = cp + jnp.sum(lk, axis=1, keepdims=True)
                ce = ce + jnp.sum(e, axis=1, keepdims=True)
            w_all = jnp.concatenate(ws, axis=1)
            dz_all = jnp.concatenate(dzs, axis=1)
            dv_ref[pl.ds(off, tq), :] += _dg(w_all, do_, 0, 0)
            dk_ref[pl.ds(off, tq), :] += _dg(dz_all, q, 0, 0)
            dq_ref[...] += _dg(dz_all, k, 1, 0)
            return cp, ce

        zero = jnp.zeros((tq, 1), F32)
        carry = lax.fori_loop(0, i, lambda j, cr: key_step(j, cr, False), (zero, zero))
        key_step(i, carry, True)

    blk = pl.BlockSpec((tq, HEAD), lambda h, i: (i, h))
    full = pl.BlockSpec((T, HEAD), lambda h, i: (0, h))
    shp = jax.ShapeDtypeStruct((T, H * HEAD), F32)
    return pl.pallas_call(
        body, name="sb_bwd", grid=(H, T // tq),
        in_specs=[blk, pl.BlockSpec((T, HEAD), lambda h, i: (0, H + h)), pl.BlockSpec((T, HEAD), lambda h, i: (0, 2 * H + h)),
                  blk, blk],
        out_specs=[blk, full, full], out_shape=[shp, shp, shp],
        compiler_params=_cp("parallel", "arbitrary"))(proj, proj, proj, do, L)


def _hg_tile(q, fl, iv, g, st, l0, l1, gain):
    R = 2 * HG_CHUNK
    row = _iota((R, R), 0)
    col = _iota((R, R), 1)
    first = row < HG_CHUNK
    same = first == (col < HG_CHUNK)
    tri = (row >= col) & same
    lb = _sigmoid(l0 - l1)
    f = lb + (1.0 - lb) * _sigmoid(fl)
    logf = jnp.log(f)
    k = 1.0 - f
    qf = q * _sigmoid(q)
    G = _sum_left(tri.astype(BF16), logf)
    gl_a = jnp.sum(jnp.where(first, logf, 0.0), axis=0, keepdims=True)
    gl_b = jnp.sum(jnp.where(first, 0.0, logf), axis=0, keepdims=True)
    q_dec = qf * jnp.exp(G)
    k_inv = k * jnp.exp(-G)
    k_end = k * jnp.exp(jnp.where(first, gl_a, gl_b) - G)
    scores = jnp.where(tri, mm_nt(q_dec, k_inv), 0.0)
    o = mm_nn(scores, iv)
    o_a = mm_nt(q_dec, st)
    st_mid = st * jnp.exp(gl_a) + mm_tn(jnp.where(first, iv, 0.0), k_end)
    o_b = mm_nt(q_dec, st_mid)
    st_new = st_mid * jnp.exp(gl_b) + mm_tn(jnp.where(first, 0.0, iv), k_end)
    o = o + jnp.where(first, o_a, o_b)
    on = o * lax.rsqrt(jnp.mean(o * o, axis=-1, keepdims=True) + NORM_EPS) * gain
    return on * (g * _sigmoid(g)), st_new


def _hg_heads(H):
    return _pick(H, (4, 2, 1))


def _hg_specs(H, c0, rev, nt):
    hb = _hg_heads(H)
    w = hb * HEAD

    def at(base):
        if rev:
            return pl.BlockSpec((HEAD, w), lambda h, i: (nt - 1 - i, base // hb + h))
        return pl.BlockSpec((HEAD, w), lambda h, i: (i, base // hb + h))
    return [at(c0), at(c0 + H), at(c0 + 2 * H), at(c0 + 3 * H)]


def _hg_fwd(proj, l0, l1, gain, H, c0):
    T = proj.shape[0]
    nt = T // HEAD
    hb = _hg_heads(H)
    w = hb * HEAD

    def body(q_ref, f_ref, i_ref, g_ref, l0_ref, l1_ref, gain_ref, o_ref, st_out_ref, st_ref):
        @pl.when(pl.program_id(1) == 0)
        def _():
            st_ref[...] = jnp.zeros_like(st_ref)

        for j in range(hb):
            s = slice(j * HEAD, (j + 1) * HEAD)
            st = st_ref[j]
            st_out_ref[j] = st
            out, st_new = _hg_tile(q_ref[:, s], f_ref[:, s], i_ref[:, s], g_ref[:, s], st, l0_ref[:, s], l1_ref[:, s],
                                   gain_ref[:, s])
            o_ref[:, s] = out.astype(o_ref.dtype)
            st_ref[j] = st_new

    vec = pl.BlockSpec((1, w), lambda h, i: (0, h))
    return pl.pallas_call(
        body, name="hg_fwd", grid=(H // hb, nt), in_specs=_hg_specs(H, c0, False, nt) + [vec, vec, vec],
        out_specs=[pl.BlockSpec((HEAD, w), lambda h, i: (i, h)),
                   pl.BlockSpec((hb, None, HEAD, HEAD), lambda h, i: (h, i, 0, 0))],
        out_shape=[jax.ShapeDtypeStruct((T, H * HEAD), BF16), jax.ShapeDtypeStruct((H, nt, HEAD, HEAD), F32)],
        scratch_shapes=[pltpu.VMEM((hb, HEAD, HEAD), F32)],
        compiler_params=_cp("parallel", "arbitrary"))(proj, proj, proj, proj, l0, l1, gain)


def _hg_bwd(proj, states, do, l0, l1, gain, H, c0, do_c0):
    T = proj.shape[0]
    nt = T // HEAD
    hb = _hg_heads(H)
    w = hb * HEAD

    def body(q_ref, f_ref, i_ref, g_ref, st_in_ref, do_ref, l0_ref, l1_ref, gain_ref,
             dq_ref, df_ref, di_ref, dg_ref, dl0_ref, dl1_ref, dgain_ref, dst_ref):
        @pl.when(pl.program_id(1) == 0)
        def _():
            dst_ref[...] = jnp.zeros_like(dst_ref)
            dl0_ref[...] = jnp.zeros_like(dl0_ref)
            dl1_ref[...] = jnp.zeros_like(dl1_ref)
            dgain_ref[...] = jnp.zeros_like(dgain_ref)

        for j in range(hb):
            s = slice(j * HEAD, (j + 1) * HEAD)
            _, vjp = jax.vjp(_hg_tile, q_ref[:, s], f_ref[:, s], i_ref[:, s], g_ref[:, s], st_in_ref[j],
                             l0_ref[:, s], l1_ref[:, s], gain_ref[:, s])
            dq, df, di, dg, dst, dl0, dl1, dgain = vjp((do_ref[:, s], dst_ref[j]))
            dq_ref[:, s] = dq
            df_ref[:, s] = df
            di_ref[:, s] = di
            dg_ref[:, s] = dg
            dst_ref[j] = dst
            dl0_ref[:, s] += dl0
            dl1_ref[:, s] += dl1
            dgain_ref[:, s] += dgain

    vec = pl.BlockSpec((1, w), lambda h, i: (0, h))
    rblk = pl.BlockSpec((HEAD, w), lambda h, i: (nt - 1 - i, h))
    shp = jax.ShapeDtypeStruct((T, H * HEAD), F32)
    vshp = jax.ShapeDtypeStruct((1, H * HEAD), F32)
    return pl.pallas_call(
        body, name="hg_bwd", grid=(H // hb, nt),
        in_specs=_hg_specs(H, c0, True, nt) + [
            pl.BlockSpec((hb, None, HEAD, HEAD), lambda h, i: (h, nt - 1 - i, 0, 0)),
            pl.BlockSpec((HEAD, w), lambda h, i: (nt - 1 - i, do_c0 // hb + h)), vec, vec, vec],
        out_specs=[rblk, rblk, rblk, rblk, vec, vec, vec],
        out_shape=[shp, shp, shp, shp, vshp, vshp, vshp],
        scratch_shapes=[pltpu.VMEM((hb, HEAD, HEAD), F32)],
        compiler_params=_cp("parallel", "arbitrary"))(proj, proj, proj, proj, states, do, l0, l1, gain)


def _sg_chunk(u_parts, v_parts, gains, biases, wpos, bpos):
    W = sum(p.shape[1] for p in v_parts)
    C = v_parts[0].shape[0]
    v = [_gelu(p) for p in v_parts]
    mu = sum(jnp.sum(p, axis=-1, keepdims=True) for p in v) * (1.0 / W)
    xc = [p - mu for p in v]
    r = lax.rsqrt(sum(jnp.sum(p * p, axis=-1, keepdims=True) for p in xc) * (1.0 / W) + NORM_EPS)
    causal = _iota((C, C), 0) >= _iota((C, C), 1)
    out = []
    for up, p, gn, bs, w, b in zip(u_parts, xc, gains, biases, wpos, bpos):
        vn = p * r * gn + bs
        mixed = mm_nn(jnp.where(causal, w, 0.0), vn) + b
        out.append(_gelu(up) * mixed)
    return out


def _sg_fwd(zpre, vgain, vbias, wpos, bpos):
    T, W2 = zpre.shape
    W = W2 // 2
    G = wpos.shape[0]
    cg = W // G
    C = SG_CHUNK

    def body(z_ref, gn_ref, bs_ref, w_ref, b_ref, s_ref):
        sl = [slice(g * cg, (g + 1) * cg) for g in range(G)]
        out = _sg_chunk([z_ref[:, s] for s in sl], [z_ref[:, W + s.start:W + s.stop] for s in sl],
                        [gn_ref[:, s] for s in sl], [bs_ref[:, s] for s in sl],
                        [w_ref[g] for g in range(G)], [b_ref[g] for g in range(G)])
        for s, o in zip(sl, out):
            s_ref[:, s] = o.astype(s_ref.dtype)

    return pl.pallas_call(
        body, name="sg_fwd", grid=(T // C,),
        in_specs=[pl.BlockSpec((C, W2), lambda i: (i, 0)), _vec_spec(W), _vec_spec(W),
                  pl.BlockSpec((G, C, C), lambda i: (0, 0, 0)), pl.BlockSpec((G, C, 1), lambda i: (0, 0, 0))],
        out_specs=pl.BlockSpec((C, W), lambda i: (i, 0)),
        out_shape=jax.ShapeDtypeStruct((T, W), BF16), compiler_params=_cp("parallel"))(zpre, vgain, vbias, wpos, bpos)


def _sg_bwd(zpre, ds, vgain, vbias, wpos, bpos):
    T, W2 = zpre.shape
    W = W2 // 2
    G = wpos.shape[0]
    cg = W // G
    C = SG_CHUNK

    def body(z_ref, ds_ref, gn_ref, bs_ref, w_ref, b_ref, dz_ref, dgn_ref, dbs_ref, dw_ref, db_ref):
        @pl.when(pl.program_id(0) == 0)
        def _():
            dgn_ref[...] = jnp.zeros_like(dgn_ref)
            dbs_ref[...] = jnp.zeros_like(dbs_ref)
            dw_ref[...] = jnp.zeros_like(dw_ref)
            db_ref[...] = jnp.zeros_like(db_ref)

        sl = [slice(g * cg, (g + 1) * cg) for g in range(G)]
        _, vjp = jax.vjp(_sg_chunk, [z_ref[:, s] for s in sl], [z_ref[:, W + s.start:W + s.stop] for s in sl],
                         [gn_ref[:, s] for s in sl], [bs_ref[:, s] for s in sl],
                         [w_ref[g] for g in range(G)], [b_ref[g] for g in range(G)])
        du, dv, dgn, dbs, dw, db = vjp([ds_ref[:, s] for s in sl])
        for g, s in enumerate(sl):
            dz_ref[:, s] = du[g].astype(dz_ref.dtype)
            dz_ref[:, W + s.start:W + s.stop] = dv[g].astype(dz_ref.dtype)
            dgn_ref[:, s] += dgn[g]
            dbs_ref[:, s] += dbs[g]
            dw_ref[g] += dw[g]
            db_ref[g] += db[g]

    wspec = pl.BlockSpec((G, C, C), lambda i: (0, 0, 0))
    bspec = pl.BlockSpec((G, C, 1), lambda i: (0, 0, 0))
    return pl.pallas_call(
        body, name="sg_bwd", grid=(T // C,),
        in_specs=[pl.BlockSpec((C, W2), lambda i: (i, 0)), pl.BlockSpec((C, W), lambda i: (i, 0)),
                  _vec_spec(W), _vec_spec(W), wspec, bspec],
        out_specs=[pl.BlockSpec((C, W2), lambda i: (i, 0)), _vec_spec(W), _vec_spec(W), wspec, bspec],
        out_shape=[jax.ShapeDtypeStruct((T, W2), BF16), jax.ShapeDtypeStruct((1, W), F32),
                   jax.ShapeDtypeStruct((1, W), F32), jax.ShapeDtypeStruct((G, C, C), F32),
                   jax.ShapeDtypeStruct((G, C, 1), F32)],
        compiler_params=_cp("arbitrary"))(zpre, ds, vgain, vbias, wpos, bpos)


def _conv_tiles(T, F):
    return _pick(T, (512, 256, 128, 64, 32, 16, 8)), _pick(F, (512, 256, 128))


def _shift_down(cur, prev8, n, first_tile):
    bt = cur.shape[0]
    r = pltpu.roll(cur, n, 0)
    p = pltpu.roll(prev8, n, 0)
    p = jnp.where(first_tile, 0.0, p)
    head = jnp.concatenate([p, r[8:]], axis=0) if bt > 8 else p
    return jnp.where(_iota(cur.shape, 0) < n, head, r)


def _shift_up(cur, next8, n, last_tile):
    bt = cur.shape[0]
    r = pltpu.roll(cur, bt - n, 0)
    p = pltpu.roll(next8, 8 - n, 0)
    p = jnp.where(last_tile, 0.0, p)
    tail = jnp.concatenate([r[:bt - 8], p], axis=0) if bt > 8 else p
    return jnp.where(_iota(cur.shape, 0) >= bt - n, tail, r)


def _conv_apply(cur, prev8, w_ref, b, first_tile):
    return (b + w_ref[0:1, :] * _shift_down(cur, prev8, 2, first_tile)
            + w_ref[1:2, :] * _shift_down(cur, prev8, 1, first_tile) + w_ref[2:3, :] * cur)


def _conv_fwd(name, a, w, b):
    T, F2 = a.shape
    F = F2 // 2
    bt, cw = _conv_tiles(T, F)
    nf = F // cw
    r8 = bt // 8

    def body(g_ref, gp_ref, v_ref, vp_ref, wg_ref, wv_ref, bg_ref, bv_ref, u_ref):
        first = pl.program_id(0) == 0
        gate = _conv_apply(g_ref[...], gp_ref[...], wg_ref, bg_ref[...], first)
        val = _conv_apply(v_ref[...], vp_ref[...], wv_ref, bv_ref[...], first)
        u_ref[...] = (gate * _sigmoid(gate) * val).astype(u_ref.dtype)

    def cur(off):
        return pl.BlockSpec((bt, cw), lambda i, j: (i, j + off))

    def prev(off):
        return pl.BlockSpec((8, cw), lambda i, j: (jnp.maximum(i * r8 - 1, 0), j + off))

    def vec(rows, off):
        return pl.BlockSpec((rows, cw), lambda i, j: (0, j + off))

    return pl.pallas_call(
        body, name=name, grid=(T // bt, nf),
        in_specs=[cur(0), prev(0), cur(nf), prev(nf), vec(3, 0), vec(3, nf), vec(1, 0), vec(1, nf)],
        out_specs=pl.BlockSpec((bt, cw), lambda i, j: (i, j)),
        out_shape=jax.ShapeDtypeStruct((T, F), BF16),
        compiler_params=_cp("parallel", "parallel"))(a, a, a, a, w, w, b, b)


def _conv_bwd(name, a, du, w, b):
    T, F2 = a.shape
    F = F2 // 2
    bt, cw = _conv_tiles(T, F)
    nf = F // cw
    r8 = bt // 8
    last_blk = T // 8 - 1

    def body(g_ref, gp_ref, gn_ref, v_ref, vp_ref, vn_ref, du_ref, dun_ref, wg_ref, wv_ref, bg_ref, bv_ref,
             dag_ref, dav_ref, dwg_ref, dwv_ref, dbg_ref, dbv_ref):
        i = pl.program_id(1)
        first = i == 0
        last = i == pl.num_programs(1) - 1

        def taps(cur, prev8, at_start):
            return _shift_down(cur, prev8, 2, at_start), _shift_down(cur, prev8, 1, at_start), cur

        def conv(t, w_ref, b_ref):
            return b_ref[...] + w_ref[0:1, :] * t[0] + w_ref[1:2, :] * t[1] + w_ref[2:3, :] * t[2]

        def act_bwd(gate, val, du_):
            sg = _sigmoid(gate)
            return du_ * val * (sg * (1.0 + gate * (1.0 - sg))), du_ * gate * sg

        g_cur, v_cur = g_ref[...], v_ref[...]
        tg = taps(g_cur, gp_ref[...], first)
        tv = taps(v_cur, vp_ref[...], first)
        dg, dv = act_bwd(conv(tg, wg_ref, bg_ref), conv(tv, wv_ref, bv_ref), du_ref[...])
        tgn = taps(gn_ref[...], g_cur[bt - 8:, :], False)
        tvn = taps(vn_ref[...], v_cur[bt - 8:, :], False)
        dgn, dvn = act_bwd(conv(tgn, wg_ref, bg_ref), conv(tvn, wv_ref, bv_ref), dun_ref[...])

        def conv_t(d, dn, w_ref):
            return w_ref[2:3, :] * d + w_ref[1:2, :] * _shift_up(d, dn, 1, last) + w_ref[0:1, :] * _shift_up(d, dn, 2, last)

        dag_ref[...] = conv_t(dg, dgn, wg_ref).astype(dag_ref.dtype)
        dav_ref[...] = conv_t(dv, dvn, wv_ref).astype(dav_ref.dtype)

        @pl.when(first)
        def _():
            dwg_ref[...] = jnp.zeros_like(dwg_ref)
            dwv_ref[...] = jnp.zeros_like(dwv_ref)
            dbg_ref[...] = jnp.zeros_like(dbg_ref)
            dbv_ref[...] = jnp.zeros_like(dbv_ref)

        for t in range(CONV_WIDTH):
            dwg_ref[t:t + 1, :] += jnp.sum(dg * tg[t], axis=0, keepdims=True)
            dwv_ref[t:t + 1, :] += jnp.sum(dv * tv[t], axis=0, keepdims=True)
        dbg_ref[...] += jnp.sum(dg, axis=0, keepdims=True)
        dbv_ref[...] += jnp.sum(dv, axis=0, keepdims=True)

    def cur(off):
        return pl.BlockSpec((bt, cw), lambda j, i: (i, j + off))

    def prev(off):
        return pl.BlockSpec((8, cw), lambda j, i: (jnp.maximum(i * r8 - 1, 0), j + off))

    def nxt(off):
        return pl.BlockSpec((8, cw), lambda j, i: (jnp.minimum((i + 1) * r8, last_blk), j + off))

    def vec(rows, off):
        return pl.BlockSpec((rows, cw), lambda j, i: (0, j + off))

    half = jax.ShapeDtypeStruct((T, F), BF16)
    dag, dav, dwg, dwv, dbg, dbv = pl.pallas_call(
        body, name=name, grid=(nf, T // bt),
        in_specs=[cur(0), prev(0), nxt(0), cur(nf), prev(nf), nxt(nf), cur(0), nxt(0),
                  vec(3, 0), vec(3, nf), vec(1, 0), vec(1, nf)],
        out_specs=[cur(0), cur(0), vec(3, 0), vec(3, 0), vec(1, 0), vec(1, 0)],
        out_shape=[half, half, jax.ShapeDtypeStruct((3, F), F32), jax.ShapeDtypeStruct((3, F), F32),
                   jax.ShapeDtypeStruct((1, F), F32), jax.ShapeDtypeStruct((1, F), F32)],
        compiler_params=_cp("parallel", "arbitrary"))(a, a, a, a, a, a, du, du, w, w, b, b)
    return dag, dav, jnp.concatenate([dwg, dwv], axis=1), jnp.concatenate([dbg, dbv], axis=1)


def _ada_fwd(c_all, ada_w, ada_b):
    R, D = c_all.shape
    L, _, Ns = ada_w.shape
    tn = _pick(Ns, (512, 256, 128))

    def body(c_ref, w_ref, b_ref, o_ref):
        cv = c_ref[...]
        cond = cv * _sigmoid(cv)
        o_ref[...] = _dg(cond, w_ref[...], 1, 0) + b_ref[...]

    return pl.pallas_call(
        body, name="ada_fwd", grid=(L, Ns // tn),
        in_specs=[pl.BlockSpec((R, D), lambda l, j: (0, 0)), pl.BlockSpec((None, D, tn), lambda l, j: (l, 0, j)),
                  pl.BlockSpec((None, 1, tn), lambda l, j: (l, 0, j))],
        out_specs=pl.BlockSpec((None, R, tn), lambda l, j: (l, 0, j)),
        out_shape=jax.ShapeDtypeStruct((L, R, Ns), F32), compiler_params=_cp("parallel", "parallel"))(c_all, ada_w, ada_b)


def _adam_math(w, g, m, v):
    m2 = ADAM_B1 * m + (1.0 - ADAM_B1) * g
    v2 = ADAM_B2 * v + (1.0 - ADAM_B2) * (g * g)
    m_hat = m2 / (1.0 - ADAM_B1 ** ADAM_STEP)
    v_hat = v2 / (1.0 - ADAM_B2 ** ADAM_STEP)
    delta = -ADAM_LR * (m_hat / (jnp.sqrt(v_hat) + ADAM_EPS) + ADAM_WD * w)
    return delta, m2, v2


def _ada_grad_adam(c_all_t, dmod, w, m, v):
    D, R = c_all_t.shape
    L, _, Ns = dmod.shape
    tr = _rows_within(D, Ns * 4, 1 << 20)

    def body(c_ref, d_ref, w_ref, m_ref, v_ref, g_ref, dl_ref, m2_ref, v2_ref):
        cv = c_ref[...]
        g = _dg(cv * _sigmoid(cv), d_ref[...], 1, 0)
        g_ref[...] = g
        dl_ref[...], m2_ref[...], v2_ref[...] = _adam_math(w_ref[...], g, m_ref[...], v_ref[...])

    big = pl.BlockSpec((None, tr, Ns), lambda l, i: (l, i, 0))
    shp = jax.ShapeDtypeStruct((L, D, Ns), F32)
    return pl.pallas_call(
        body, name="ada_grad_adam", grid=(L, D // tr),
        in_specs=[pl.BlockSpec((tr, R), lambda l, i: (i, 0)), pl.BlockSpec((None, R, Ns), lambda l, i: (l, 0, 0)), big, big, big],
        out_specs=[big] * 4, out_shape=[shp] * 4, compiler_params=_cp("parallel", "parallel"))(c_all_t, dmod, w, m, v)


def _adam(name, w, g, m, v):
    R, C = w.shape
    tr = _rows_within(R, C * 4, 1 << 21)

    def body(w_ref, g_ref, m_ref, v_ref, dl_ref, m2_ref, v2_ref):
        dl_ref[...], m2_ref[...], v2_ref[...] = _adam_math(w_ref[...], g_ref[...], m_ref[...], v_ref[...])

    blk = pl.BlockSpec((tr, C), lambda i: (i, 0))
    shp = jax.ShapeDtypeStruct((R, C), F32)
    return pl.pallas_call(body, name=name, grid=(R // tr,), in_specs=[blk] * 4, out_specs=[blk] * 3,
                          out_shape=[shp] * 3, compiler_params=_cp("parallel"))(w, g, m, v)


def _cast_into_rows(name, w, chip):
    _, R, C = w.shape
    tr = _rows_within(R, C * 4, 1 << 22)

    def body(chip_ref, w_ref, o_ref):
        o_ref[...] = w_ref[...].astype(BF16)

    grid_spec = pltpu.PrefetchScalarGridSpec(
        num_scalar_prefetch=1, grid=(2, R // tr),
        in_specs=[pl.BlockSpec((None, tr, C), lambda h, i, s: (h, i, 0))],
        out_specs=pl.BlockSpec((None, tr, C), lambda h, i, s: (2 * s[0] + h, i, 0)))
    return pl.pallas_call(body, name=name, grid_spec=grid_spec, out_shape=jax.ShapeDtypeStruct((N_DEV, R, C), BF16),
                          compiler_params=_cp("arbitrary", "arbitrary"))(chip.reshape(1).astype(jnp.int32), w)


def _add_pairs(name, eight, from_sib, c):
    _, R, C = from_sib.shape
    tr = _rows_within(R, C * 2, 1 << 21)

    def body(c_ref, a_ref, b_ref, o_ref):
        o_ref[...] = (a_ref[...].astype(F32) + b_ref[...].astype(F32)).astype(o_ref.dtype)

    blk = pl.BlockSpec((None, tr, C), lambda j, i, s: (j, i, 0))
    grid_spec = pltpu.PrefetchScalarGridSpec(
        num_scalar_prefetch=1, grid=(4, R // tr),
        in_specs=[pl.BlockSpec((None, tr, C), lambda j, i, s: (2 * j + s[0], i, 0)), blk], out_specs=blk)
    return pl.pallas_call(body, name=name, grid_spec=grid_spec, out_shape=jax.ShapeDtypeStruct(from_sib.shape, BF16),
                          compiler_params=_cp("arbitrary", "arbitrary"))(c.reshape(1).astype(jnp.int32), eight, from_sib)


def _sum_into_pair(name, own, landed, slot, chip):
    n, R, C = landed.shape
    tr = _rows_within(R, (n + 1) * C * landed.dtype.itemsize, 1 << 23)

    def body(idx_ref, own_ref, x_ref, o_ref):
        mine = idx_ref[1]
        acc = None
        for j in range(n):
            part = jnp.where(mine == j, own_ref[...], x_ref[j]).astype(F32)
            acc = part if acc is None else acc + part
        o_ref[...] = acc

    grid_spec = pltpu.PrefetchScalarGridSpec(
        num_scalar_prefetch=1, grid=(R // tr,),
        in_specs=[pl.BlockSpec((None, tr, C), lambda i, s: (s[1], i, 0)), pl.BlockSpec((n, tr, C), lambda i, s: (0, i, 0))],
        out_specs=pl.BlockSpec((None, tr, C), lambda i, s: (s[0], i, 0)))
    idx = jnp.stack([slot, chip]).astype(jnp.int32)
    return pl.pallas_call(body, name=name, grid_spec=grid_spec, out_shape=jax.ShapeDtypeStruct((2, R, C), F32),
                          compiler_params=_cp("arbitrary"))(idx, own, landed)


def _sum_leading(name, a, out_dtype=F32):
    n, R, C = a.shape
    tr = _rows_within(R, n * C * a.dtype.itemsize, 1 << 23)

    def body(a_ref, o_ref):
        acc = a_ref[0].astype(F32)
        for j in range(1, n):
            acc = acc + a_ref[j].astype(F32)
        o_ref[...] = acc.astype(o_ref.dtype)

    return pl.pallas_call(body, name=name, grid=(R // tr,), in_specs=[pl.BlockSpec((n, tr, C), lambda i: (0, i, 0))],
                          out_specs=pl.BlockSpec((tr, C), lambda i: (i, 0)),
                          out_shape=jax.ShapeDtypeStruct((R, C), out_dtype), compiler_params=_cp("parallel"))(a)


def _place():
    return lax.axis_index("x"), lax.axis_index("y"), lax.axis_index("c")


def _all_gather(name, blocks, halves=False, after=None):
    n = len(blocks)
    shapes = [b.shape[1:] if halves else b.shape for b in blocks]
    extra = [] if after is None else [after]

    def body(*refs):
        ins, outs = refs[:n], refs[n + len(extra):2 * n + len(extra)]
        send_sems, recv_sems, local_sems = refs[2 * n + len(extra):]
        x, y, c = _place()
        me, sibling = (x, y, c), (x, y, 1 - c)
        chips = [(1 - x, y), (x, 1 - y), (1 - x, 1 - y)]

        def rows(a, px, py, pc):
            return outs[a].at[4 * px + 2 * py + pc]

        def copy(a, k, block, to, src=None):
            return pltpu.make_async_remote_copy(
                src_ref=rows(a, *block) if src is None else src, dst_ref=rows(a, *block),
                send_sem=send_sems.at[7 * a + k], recv_sem=recv_sems.at[7 * a + k],
                device_id=to, device_id_type=MESH)

        started = []
        mine = []
        for a in range(n):
            src = ins[a].at[c] if halves else ins[a]
            mine.append(pltpu.make_async_copy(src, rows(a, *me), local_sems.at[a]))
            mine[-1].start()
            first = [copy(a, 0, me, sibling, src=src)]
            first += [copy(a, 1 + j, me, (*chip, c), src=src) for j, chip in enumerate(chips)]
            for cp in first:
                cp.start()
            started += first
        for j, chip in enumerate(chips):
            for a in range(n):
                copy(a, 1 + j, (*chip, c), me).wait_recv()
                passed = copy(a, 4 + j, (*chip, c), sibling)
                passed.start()
                started.append(passed)
        for a in range(n):
            copy(a, 0, sibling, me).wait_recv()
            for j, chip in enumerate(chips):
                copy(a, 4 + j, (*chip, 1 - c), me).wait_recv()
        for cp in started:
            cp.wait_send()
        for cp in mine:
            cp.wait()

    return pl.pallas_call(
        body, name=name, in_specs=[ANY] * (n + len(extra)), out_specs=[ANY] * n,
        out_shape=[jax.ShapeDtypeStruct((N_DEV,) + tuple(s), b.dtype) for s, b in zip(shapes, blocks)],
        scratch_shapes=[pltpu.SemaphoreType.DMA((7 * n,)), pltpu.SemaphoreType.DMA((7 * n,)),
                        pltpu.SemaphoreType.DMA((n,))],
    )(*blocks, *extra)


def _share_halves(name, arrays):
    n = len(arrays)

    def body(*refs):
        ins, outs = refs[:n], refs[n:2 * n]
        send_sems, recv_sems = refs[2 * n:]
        x, y, c = _place()
        started = []
        for a in range(n):
            cp = pltpu.make_async_remote_copy(src_ref=ins[a].at[c], dst_ref=outs[a].at[c], send_sem=send_sems.at[a],
                                              recv_sem=recv_sems.at[a], device_id=(x, y, 1 - c), device_id_type=MESH)
            cp.start()
            started.append(cp)
        for a in range(n):
            started[a].wait_send()
            pltpu.make_async_remote_copy(src_ref=ins[a].at[1 - c], dst_ref=outs[a].at[1 - c], send_sem=send_sems.at[a],
                                         recv_sem=recv_sems.at[a], device_id=(x, y, 1 - c), device_id_type=MESH).wait_recv()

    return pl.pallas_call(
        body, name=name, in_specs=[ANY] * n, out_specs=[ANY] * n,
        out_shape=[jax.ShapeDtypeStruct(a.shape, a.dtype) for a in arrays],
        input_output_aliases={a: a for a in range(n)},
        scratch_shapes=[pltpu.SemaphoreType.DMA((n,)), pltpu.SemaphoreType.DMA((n,))],
    )(*arrays)


HBM = pl.BlockSpec(memory_space=pltpu.HBM)
SEM = pl.BlockSpec(memory_space=pltpu.SEMAPHORE)
EFFECT = pltpu.SideEffectType.DATAFLOW_SIDE_EFFECTING


COPIES_PER_ARRAY = {"rows": 3, "parts": 3, "halves": 4}


def _chip_copies(kind, srcs, dsts, send_sems, recv_sems):
    x, y, c = _place()
    mine = 2 * x + y
    per = COPIES_PER_ARRAY[kind]
    sends, arrivals = [], []
    for a in range(len(srcs)):
        if kind == "halves":
            for j in range(N_CHIP):
                cp = pltpu.make_async_remote_copy(
                    src_ref=srcs[a].at[2 * j + 1 - c], dst_ref=dsts[a].at[j], send_sem=send_sems.at[per * a + j],
                    recv_sem=recv_sems.at[per * a + j], device_id=(x, y, 1 - c), device_id_type=MESH)
                sends.append(cp)
                arrivals.append(cp)
            continue
        for k, (px, py) in enumerate([(1 - x, y), (x, 1 - y), (1 - x, 1 - y)]):
            other = 2 * px + py
            if kind == "rows":
                src, dst, lands = srcs[a].at[2 * mine + c], dsts[a].at[2 * mine + c], dsts[a].at[2 * other + c]
            else:
                src, dst, lands = srcs[a].at[other], dsts[a].at[mine], dsts[a].at[other]
            sem = dict(send_sem=send_sems.at[per * a + k], recv_sem=recv_sems.at[per * a + k], device_id=(px, py, c),
                       device_id_type=MESH)
            sends.append(pltpu.make_async_remote_copy(src_ref=src, dst_ref=dst, **sem))
            arrivals.append(pltpu.make_async_remote_copy(src_ref=src, dst_ref=lands, **sem))
    return sends, arrivals


def _chips_start(name, kind, srcs, dsts=None, after=None):
    n = len(srcs)
    bufs = list(srcs) + (list(dsts) if dsts is not None else [])
    nb = len(bufs)
    extra = [] if after is None else [after]

    def body(*refs):
        ins = refs[:nb]
        send_sems, recv_sems = refs[nb + len(extra)], refs[nb + len(extra) + 1]
        token = refs[-1]
        sends, _ = _chip_copies(kind, ins[:n], ins[n:] if dsts is not None else ins[:n], send_sems, recv_sems)
        for cp in sends:
            cp.start()
        token[...] = jnp.zeros_like(token)

    out = pl.pallas_call(
        body, name=name,
        out_shape=(pltpu.SemaphoreType.DMA((COPIES_PER_ARRAY[kind] * n,)), pltpu.SemaphoreType.DMA((COPIES_PER_ARRAY[kind] * n,)),
                   *[pltpu.HBM(b.shape, b.dtype) for b in bufs], jax.ShapeDtypeStruct((8, HEAD), F32)),
        in_specs=(HBM,) * nb + (ANY,) * len(extra),
        out_specs=(SEM, SEM) + (HBM,) * nb + (pl.BlockSpec(memory_space=pltpu.VMEM),),
        input_output_aliases={i: 2 + i for i in range(nb)},
        compiler_params=pltpu.CompilerParams(has_side_effects=EFFECT),
    )(*[pltpu.with_memory_space_constraint(b, pltpu.HBM) for b in bufs], *extra)
    return out[0], out[1], list(out[2:2 + nb]), out[-1]


def _chips_wait(name, kind, n, send_sems, recv_sems, bufs, after):
    nb = len(bufs)

    def body(*refs):
        ins = refs[:nb]
        s_sems, r_sems = refs[nb], refs[nb + 1]
        sends, arrivals = _chip_copies(kind, ins[:n], ins[n:] if nb > n else ins[:n], s_sems, r_sems)
        for cp in sends:
            cp.wait_send()
        for cp in arrivals:
            cp.wait_recv()

    return list(pl.pallas_call(
        body, name=name, out_shape=tuple(pltpu.HBM(b.shape, b.dtype) for b in bufs),
        in_specs=(HBM,) * nb + (SEM, SEM, pl.BlockSpec(memory_space=pl.ANY)), out_specs=(HBM,) * nb,
        input_output_aliases={i: i for i in range(nb)},
        compiler_params=pltpu.CompilerParams(has_side_effects=EFFECT),
    )(*bufs, send_sems, recv_sems, after))


def _fill_from_sibling(name, arrays):
    n = len(arrays)

    def body(*refs):
        ins, outs = refs[:n], refs[n:2 * n]
        send_sems, recv_sems = refs[2 * n:]
        x, y, c = _place()
        sends, arrivals = [], []
        for a in range(n):
            for k, (px, py) in enumerate([(1 - x, y), (x, 1 - y), (1 - x, 1 - y)]):
                sem = dict(send_sem=send_sems.at[3 * a + k], recv_sem=recv_sems.at[3 * a + k], device_id=(x, y, 1 - c),
                           device_id_type=MESH)
                row = 2 * (2 * px + py)
                sends.append(pltpu.make_async_remote_copy(src_ref=ins[a].at[row + c], dst_ref=outs[a].at[row + c], **sem))
                arrivals.append(pltpu.make_async_remote_copy(src_ref=ins[a].at[row + c], dst_ref=outs[a].at[row + 1 - c], **sem))
        for cp in sends:
            cp.start()
        for cp in sends:
            cp.wait_send()
        for cp in arrivals:
            cp.wait_recv()

    return pl.pallas_call(
        body, name=name, in_specs=[ANY] * n, out_specs=[ANY] * n,
        out_shape=[jax.ShapeDtypeStruct(a.shape, a.dtype) for a in arrays],
        input_output_aliases={a: a for a in range(n)},
        scratch_shapes=[pltpu.SemaphoreType.DMA((3 * n,)), pltpu.SemaphoreType.DMA((3 * n,))],
    )(*arrays)


def kernel(x, c, ada_w, ada_b, mix_norm, ffn_norm, par_w_in, par_w_out, hg_lb_logits, hg_out_norm, sg_w_in, sg_v_gain, sg_v_bias, sg_w_pos, sg_b_pos, sg_w_out, ffn_up, ffn_conv_w, ffn_conv_b, ffn_down, final_norm, loss_target, m_ada_w, m_ada_b, m_mix_norm, m_ffn_norm, m_par_w_in, m_par_w_out, m_hg_lb_logits, m_hg_out_norm, m_sg_w_in, m_sg_v_gain, m_sg_v_bias, m_sg_w_pos, m_sg_b_pos, m_sg_w_out, m_ffn_up, m_ffn_conv_w, m_ffn_conv_b, m_ffn_down, m_final_norm, v_ada_w, v_ada_b, v_mix_norm, v_ffn_norm, v_par_w_in, v_par_w_out, v_hg_lb_logits, v_hg_out_norm, v_sg_w_in, v_sg_v_gain, v_sg_v_bias, v_sg_w_pos, v_sg_b_pos, v_sg_w_out, v_ffn_up, v_ffn_conv_w, v_ffn_conv_b, v_ffn_down, v_final_norm):
    names = ["ada_w", "ada_b", "mix_norm", "ffn_norm", "par_w_in", "par_w_out", "hg_lb_logits", "hg_out_norm", "sg_w_in",
             "sg_v_gain", "sg_v_bias", "sg_w_pos", "sg_b_pos", "sg_w_out", "ffn_up", "ffn_conv_w", "ffn_conv_b",
             "ffn_down", "final_norm"]
    W = dict(zip(names, [ada_w, ada_b, mix_norm, ffn_norm, par_w_in, par_w_out, hg_lb_logits, hg_out_norm, sg_w_in,
                         sg_v_gain, sg_v_bias, sg_w_pos, sg_b_pos, sg_w_out, ffn_up, ffn_conv_w, ffn_conv_b, ffn_down,
                         final_norm]))
    M = dict(zip(names, [m_ada_w, m_ada_b, m_mix_norm, m_ffn_norm, m_par_w_in, m_par_w_out, m_hg_lb_logits, m_hg_out_norm,
                         m_sg_w_in, m_sg_v_gain, m_sg_v_bias, m_sg_w_pos, m_sg_b_pos, m_sg_w_out, m_ffn_up, m_ffn_conv_w,
                         m_ffn_conv_b, m_ffn_down, m_final_norm]))
    V = dict(zip(names, [v_ada_w, v_ada_b, v_mix_norm, v_ffn_norm, v_par_w_in, v_par_w_out, v_hg_lb_logits, v_hg_out_norm,
                         v_sg_w_in, v_sg_v_gain, v_sg_v_bias, v_sg_w_pos, v_sg_b_pos, v_sg_w_out, v_ffn_up, v_ffn_conv_w,
                         v_ffn_conv_b, v_ffn_down, v_final_norm]))

    x = x[0]
    target = loss_target[0]
    T, D = x.shape
    ix, iy, ic = _place()
    chip = 2 * ix + iy
    dev = 2 * chip + ic
    H = hg_out_norm.shape[1]
    SBW = H * HEAD
    NA = ada_w.shape[2]
    F2s = ffn_up.shape[2]
    F2 = N_CHIP * F2s
    SGW = sg_w_out.shape[1] * N_CHIP
    G = sg_w_pos.shape[1]

    shards = [par_w_in[0], par_w_out[0], sg_w_in[0], sg_w_out[0], ffn_up[0], ffn_up[1], ffn_down[0], ffn_down[1]]
    kinds = ["col", "row", "col", "row", "col", "col", "row", "row"]
    rows8 = [_cast_into_rows("cast_w", w.reshape(2, w.shape[0] // 2, w.shape[1]), chip) for w in shards]
    groups = {"a": [0], "b": [1, 4, 6], "c": [2, 3, 5, 7]}
    started = {"a": _chips_start("gather_start_a", "rows", [rows8[i] for i in groups["a"]])}

    def weights_of(g, after):
        send_sems, recv_sems, bufs, _ = started[g]
        bufs = _chips_wait("gather_wait_" + g, "rows", len(bufs), send_sems, recv_sems, bufs, after)
        out = {}
        for i, g8 in zip(groups[g], _fill_from_sibling("gather_fill_" + g, bufs)):
            K, N = shards[i].shape
            out[i] = g8.reshape(N_CHIP, K, N) if kinds[i] == "col" else g8.reshape(N_CHIP * K, N)
        return out

    n_cw = ffn_conv_w.size
    n_sv = sg_v_gain.size
    c_all, small_all = _all_gather("gather_small", [c, _pack_rows([ffn_conv_w, sg_v_gain, sg_v_bias])],
                                   after=started["a"][3])
    c_all = c_all.reshape(N_DEV, D)
    small_all = small_all.reshape(N_CHIP, 2, -1)[:, 0]
    conv_w_full = small_all[:, :n_cw].reshape(N_CHIP, 2, CONV_WIDTH, F2s).transpose(1, 2, 0, 3).reshape(2, CONV_WIDTH, F2)
    sg_gain_full = small_all[:, n_cw:n_cw + n_sv].reshape(1, SGW)
    sg_bias_full = small_all[:, n_cw + n_sv:n_cw + 2 * n_sv].reshape(1, SGW)

    c_pad = jnp.pad(c_all, ((0, 16 - N_DEV), (0, 0)))
    ada_b_sh = lax.dynamic_slice(ada_b, (0, chip * NA), (2, NA)).reshape(2, 1, NA)
    mod_sh = _ada_fwd(c_pad, ada_w, ada_b_sh)
    mod_all, = _all_gather("gather_mod", [mod_sh[:, :N_DEV]])
    mod_all = mod_all.reshape(N_CHIP, 2, 2, N_DEV, NA)[:, 0]
    mod = lax.dynamic_index_in_dim(mod_all, dev, axis=2, keepdims=False)
    mod = mod.transpose(1, 0, 2).reshape(2, 6, D)
    mods = [[mod[l, k].reshape(1, D) for k in range(6)] for l in range(2)]
    for g in ("b", "c"):
        started[g] = _chips_start("gather_start_" + g, "rows", [rows8[i] for i in groups[g]], after=mod)
    start_token = sum(st[3][0, 0] for st in started.values())

    vec = lambda a: a.reshape(1, -1)
    l0 = vec(hg_lb_logits[0])
    l1 = vec(hg_lb_logits[1])
    hg_gain = vec(hg_out_norm[0])
    wpos = sg_w_pos[0]
    bpos = sg_b_pos[0].reshape(G, SG_CHUNK, 1)
    conv_b = [vec(ffn_conv_b[l]) for l in range(2)]

    sh1, sc1, g1, sh2, sc2, g2 = mods[0]
    w_in = weights_of("a", mod)[0]
    h0 = _normmod_fwd("norm_mix0", x, vec(mix_norm[0]) + start_token, sc1, sh1)
    proj = _mm_nn("mm_par_in", h0, w_in)
    o_sb, sb_tot = _sb_fwd(proj, H)
    o_hg, hg_states = _hg_fwd(proj, l0, l1, hg_gain, H, 3 * H)
    o_cat = jnp.concatenate([o_sb, o_hg], axis=1)
    wb = weights_of("b", o_cat)
    w_out, wup, wdn = wb[1], [wb[4], None], [wb[6], None]
    y0 = _mm_nn("mm_par_out", o_cat, w_out)
    x1, h0f = _res_normmod_fwd("res_norm_ffn0", x, y0, g1, vec(ffn_norm[0]), sc2, sh2)
    a0 = _mm_nn("mm_up0", h0f, wup[0])
    u0 = _conv_fwd("conv_fwd0", a0, conv_w_full[0], conv_b[0])
    f0 = _mm_nn("mm_down0", u0, wdn[0])
    sh1b, sc1b, g1b, sh2b, sc2b, g2b = mods[1]
    x2, h1 = _res_normmod_fwd("res_norm_mix1", x1, f0, g2, vec(mix_norm[1]), sc1b, sh1b)
    wc = weights_of("c", h1)
    wsg_in, wsg_out, wup[1], wdn[1] = wc[2], wc[3], wc[5], wc[7]
    zpre = _mm_nn("mm_sg_in", h1, wsg_in)
    s1 = _sg_fwd(zpre, sg_gain_full, sg_bias_full, wpos, bpos)
    y1 = _mm_nn("mm_sg_out", s1, wsg_out)
    x3, h1f = _res_normmod_fwd("res_norm_ffn1", x2, y1, g1b, vec(ffn_norm[1]), sc2b, sh2b)
    a1 = _mm_nn("mm_up1", h1f, wup[1])
    u1 = _conv_fwd("conv_fwd1", a1, conv_w_full[1], conv_b[1])
    f1 = _mm_nn("mm_down1", u1, wdn[1])
    loss_sum, dx, df1, dg2b, d_final = _final_fwd_bwd(x3, f1, g2b, vec(final_norm), target)
    loss = lax.psum(loss_sum[0, 0], ("x", "y", "c"))

    def reduce_start(tag, idx, grads):
        eights = [g.reshape((N_DEV, -1, g.shape[-1])) for g in grads]
        landing = [lax.empty((N_CHIP,) + e.shape[1:], e.dtype) for e in eights]
        send_sems, recv_sems, bufs, token = _chips_start("grads_sibling_start_" + tag, "halves", eights, landing)
        return (tag, idx, send_sems, recv_sems, bufs), token[0:1, 0:1]

    def reduce_cross(state, after):
        tag, idx, send_sems, recv_sems, bufs = state
        n = len(idx)
        bufs = _chips_wait("grads_sibling_wait_" + tag, "halves", n, send_sems, recv_sems, bufs, after)
        pair = [_add_pairs("add_pair", e, r, ic) for e, r in zip(bufs[:n], bufs[n:])]
        landing = [lax.empty(p.shape, p.dtype) for p in pair]
        send_sems, recv_sems, bufs, token = _chips_start("grads_start_" + tag, "parts", pair, landing)
        return (tag, idx, send_sems, recv_sems, bufs), token[0:1, 0:1]

    def reduce_finish(state, after):
        tag, idx, send_sems, recv_sems, bufs = state
        n = len(idx)
        bufs = _chips_wait("grads_wait_" + tag, "parts", n, send_sems, recv_sems, bufs, after)
        halves = [_sum_into_pair("sum_chips", p, x_, ic, chip) for p, x_ in zip(bufs[:n], bufs[n:])]
        both = _share_halves("grads_share_" + tag, halves)
        return {i: b.reshape(shards[i].shape) for i, b in zip(idx, both)}

    def ffn_bwd(l, dfl, u, a, hf):
        g_dn = _mm_tn("mm_g_down", u, dfl)
        du = _mm_nt("mm_d_u", dfl, wdn[l])
        da_g, da_v, dcw, dcb = _conv_bwd("conv_bwd", a, du, conv_w_full[l], conv_b[l])
        da = jnp.concatenate([da_g, da_v], axis=1)
        g_up = _mm_tn("mm_g_up", hf, da, chunks=N_CHIP)
        dh = _mm_nt("mm_d_hf", da, wup[l])
        return g_dn, g_up, dcw, dcb, dh

    g_dn1, g_up1, dcw1, dcb1, dh1f = ffn_bwd(1, df1, u1, a1, h1f)
    red1, tok = reduce_start("1", [5, 7], [g_up1, g_dn1])
    dx, dgn_f1, dsc2b, dsh2b, dy1, dg1b = _block_bwd("bwd_ffn1", dx, dh1f, x3, vec(ffn_norm[1]) + tok, sc2b, sh2b, y1, g1b)
    g_sg_out = _mm_tn("mm_g_sg_out", s1, dy1)
    ds1 = _mm_nt("mm_d_s", dy1, wsg_out)
    dzpre, dsg_gain, dsg_bias, dwpos, dbpos = _sg_bwd(zpre, ds1, sg_gain_full, sg_bias_full, wpos, bpos)
    red1, tok_x = reduce_cross(red1, dzpre)
    g_sg_in = _mm_tn("mm_g_sg_in", h1, dzpre, chunks=N_CHIP)
    dh1 = _mm_nt("mm_d_h1", dzpre, wsg_in)
    red2, tok = reduce_start("2", [2, 3], [g_sg_in, g_sg_out])
    dx, dgn_m1, dsc1b, dsh1b, df0, dg2 = _block_bwd("bwd_mix1", dx, dh1, x2, vec(mix_norm[1]) + tok + tok_x, sc1b, sh1b, f0, g2)
    g_dn0, g_up0, dcw0, dcb0, dh0f = ffn_bwd(0, df0, u0, a0, h0f)
    red2, tok_x = reduce_cross(red2, dh0f)
    red3, tok = reduce_start("3", [4, 6], [g_up0, g_dn0])
    dx, dgn_f0, dsc2, dsh2, dy0, dg1 = _block_bwd("bwd_ffn0", dx, dh0f, x1, vec(ffn_norm[0]) + tok + tok_x, sc2, sh2, y0, g1)
    g_out = _mm_tn("mm_g_par_out", o_cat, dy0)
    do = _mm_nt("mm_d_o", dy0, w_out)
    dq, dk, dv = _sb_bwd(proj, do, sb_tot, H)
    dhq, dhf, dhi, dhg, dl0, dl1, dhg_gain = _hg_bwd(proj, hg_states, do, l0, l1, hg_gain, H, 3 * H, H)
    dproj = jnp.concatenate([dq, dk, dv, dhq, dhf, dhi, dhg], axis=1).astype(BF16)
    red3, tok_x = reduce_cross(red3, dproj)
    g_in = _mm_tn("mm_g_par_in", h0, dproj, chunks=N_CHIP)
    red4, tok = reduce_start("4", [0, 1], [g_in, g_out])
    dh0 = _mm_nt("mm_d_h0", dproj, w_in)
    grad_x, dgn_m0, dsc1, dsh1 = _block_bwd("bwd_mix0", dx, dh0, x, vec(mix_norm[0]) + tok + tok_x, sc1, sh1)
    red4, tok_x = reduce_cross(red4, grad_x)

    G_, delta, new_m, new_v = {}, {}, {}, {}

    def adam_on(nme):
        shp = W[nme].shape
        r2 = lambda a: a.reshape(-1, shp[-1])
        d_, m_, v_ = _adam("adam_" + nme, r2(W[nme]), r2(G_[nme]), r2(M[nme]), r2(V[nme]))
        delta[nme], new_m[nme], new_v[nme] = d_.reshape(shp), m_.reshape(shp), v_.reshape(shp)

    g_shards = {}
    for state in (red1, red2, red3):
        g_shards.update(reduce_finish(state, red4[4][0]))
    G_["sg_w_in"] = g_shards[2][None]
    G_["sg_w_out"] = g_shards[3][None]
    G_["ffn_up"] = jnp.stack([g_shards[4], g_shards[5]])
    G_["ffn_down"] = jnp.stack([g_shards[6], g_shards[7]])
    for nme in ["sg_w_in", "sg_w_out", "ffn_up", "ffn_down"]:
        adam_on(nme)

    dmod = jnp.concatenate([dsh1, dsc1, dg1, dsh2, dsc2, dg2, dsh1b, dsc1b, dg1b, dsh2b, dsc2b, dg2b], axis=1)
    parts = [dmod, dgn_m0, dgn_m1, dgn_f0, dgn_f1, dl0, dl1, dhg_gain, dsg_gain, dsg_bias, dwpos, dbpos,
             dcw0, dcw1, dcb0, dcb1, d_final]
    sizes = [p.size for p in parts]
    packed = _pack_rows(parts)
    packed_all, = _all_gather("gather_small_grads", [packed], after=new_v["ffn_down"])
    summed = _sum_leading("sum_small_grads", packed_all).reshape(-1)
    offs = [0]
    for s in sizes:
        offs.append(offs[-1] + s)
    red = [summed[offs[i]:offs[i + 1]] for i in range(len(parts))]
    (r_dmod, r_gm0, r_gm1, r_gf0, r_gf1, r_l0, r_l1, r_hgain, r_sgain, r_sbias, r_wpos, r_bpos,
     r_cw0, r_cw1, r_cb0, r_cb1, r_final) = red
    n_mod = sizes[0]
    dmod_all = packed_all.reshape(N_DEV, -1)[:, :n_mod].reshape(N_DEV, 2, 6 * D)

    G_["ada_b"] = r_dmod.reshape(2, 6 * D)
    G_["mix_norm"] = jnp.stack([r_gm0, r_gm1])
    G_["ffn_norm"] = jnp.stack([r_gf0, r_gf1])
    G_["hg_lb_logits"] = jnp.stack([r_l0, r_l1])
    G_["hg_out_norm"] = r_hgain.reshape(hg_out_norm.shape)
    G_["sg_v_gain"] = lax.dynamic_slice(r_sgain, (chip * n_sv,), (n_sv,)).reshape(sg_v_gain.shape)
    G_["sg_v_bias"] = lax.dynamic_slice(r_sbias, (chip * n_sv,), (n_sv,)).reshape(sg_v_bias.shape)
    G_["sg_w_pos"] = r_wpos.reshape(sg_w_pos.shape)
    G_["sg_b_pos"] = r_bpos.reshape(sg_b_pos.shape)
    cw_full = jnp.stack([r_cw0.reshape(CONV_WIDTH, F2), r_cw1.reshape(CONV_WIDTH, F2)])
    G_["ffn_conv_w"] = lax.dynamic_slice(cw_full, (0, 0, chip * F2s), (2, CONV_WIDTH, F2s))
    G_["ffn_conv_b"] = jnp.stack([r_cb0, r_cb1])
    G_["final_norm"] = r_final

    c_t = jnp.pad(c_all, ((0, HEAD - N_DEV), (0, 0))).T
    dmod_sh = lax.dynamic_slice(dmod_all.transpose(1, 0, 2), (0, 0, chip * NA), (2, N_DEV, NA))
    dmod_sh = jnp.pad(dmod_sh, ((0, 0), (0, HEAD - N_DEV), (0, 0)))
    G_["ada_w"], delta["ada_w"], new_m["ada_w"], new_v["ada_w"] = _ada_grad_adam(c_t, dmod_sh, ada_w, m_ada_w, v_ada_w)

    g_shards.update(reduce_finish(red4, G_["ada_w"]))
    G_["par_w_in"] = g_shards[0][None]
    G_["par_w_out"] = g_shards[1][None]
    for nme in ["par_w_in", "par_w_out"]:
        adam_on(nme)
    small = [n_ for n_ in names if n_ not in delta]
    pk = lambda dct: _pack_rows([dct[n_] for n_ in small])
    d_, m_, v_ = _adam("adam_small", pk(W), pk(G_), pk(M), pk(V))
    off = 0
    for n_ in small:
        sz = W[n_].size
        for dst, src in ((delta, d_), (new_m, m_), (new_v, v_)):
            dst[n_] = src.reshape(-1)[off:off + sz].reshape(W[n_].shape)
        off += sz

    return (loss, grad_x[None], *[G_[n_] for n_ in names], *[delta[n_] for n_ in names],
            *[new_m[n_] for n_ in names], *[new_v[n_] for n_ in names])
```

```python
import functools
import math

import jax
import jax.numpy as jnp
from jax import lax
from jax.experimental import pallas as pl
from jax.experimental.pallas import tpu as pltpu

F32 = jnp.float32
BF16 = jnp.bfloat16
MESH = pl.DeviceIdType.MESH
ANY = pl.BlockSpec(memory_space=pl.ANY)

NORM_EPS = 1e-6
ADAM_LR = 0.001
ADAM_B1 = 0.9
ADAM_B2 = 0.999
ADAM_EPS = 1e-08
ADAM_WD = 0.01
ADAM_STEP = 10
CONV_WIDTH = 3
HEAD = 128
HG_CHUNK = 64
SG_CHUNK = 128
N_DEV = 8
N_CHIP = 4
V7X_VMEM_LIMIT = 56 * 1024 * 1024


def _cp(*sem):
    return pltpu.CompilerParams(dimension_semantics=sem if sem else None, vmem_limit_bytes=V7X_VMEM_LIMIT)


def _pick(n, prefs):
    for p in prefs:
        if p <= n and n % p == 0:
            return p
    return n


def _iota(shape, axis):
    return lax.broadcasted_iota(jnp.int32, shape, axis)


def _rows_within(R, row_bytes, budget):
    if R * row_bytes <= budget:
        return R
    for t in (1024, 512, 256, 128, 64, 32, 16):
        if R % t == 0 and t * row_bytes <= budget:
            return t
    return _pick(R, (16, 8))


def _pack_rows(arrays):
    flat = jnp.concatenate([a.reshape(-1) for a in arrays])
    pad = (-flat.size) % (8 * HEAD)
    return jnp.pad(flat, (0, pad)).reshape(-1, HEAD)


def _dg(a, b, ca, cb):
    return lax.dot_general(a.astype(BF16), b.astype(BF16), (((ca,), (cb,)), ((), ())), preferred_element_type=F32)


@jax.custom_vjp
def mm_nn(a, b):
    return _dg(a, b, 1, 0)


mm_nn.defvjp(lambda a, b: (_dg(a, b, 1, 0), (a, b)),
             lambda r, g: (_dg(g, r[1], 1, 1), _dg(r[0], g, 0, 0)))


@jax.custom_vjp
def mm_nt(a, b):
    return _dg(a, b, 1, 1)


mm_nt.defvjp(lambda a, b: (_dg(a, b, 1, 1), (a, b)),
             lambda r, g: (_dg(g, r[1], 1, 0), _dg(g, r[0], 0, 0)))


@jax.custom_vjp
def mm_tn(a, b):
    return _dg(a, b, 0, 0)


mm_tn.defvjp(lambda a, b: (_dg(a, b, 0, 0), (a, b)),
             lambda r, g: (_dg(r[1], g, 1, 1), _dg(r[0], g, 1, 0)))


def _split(x):
    hi = x.astype(BF16)
    lo = (x - hi.astype(F32)).astype(BF16)
    return hi, lo


def _sum_right(x, m01):
    hi, lo = _split(x)
    return _dg(hi, m01, 1, 0) + _dg(lo, m01, 1, 0)


def _sum_left_impl(m01, x, ca):
    hi, lo = _split(x)
    return _dg(m01, hi, ca, 0) + _dg(m01, lo, ca, 0)


@jax.custom_vjp
def _sum_left(m01, x):
    return _sum_left_impl(m01, x, 1)


_sum_left.defvjp(lambda m, x: (_sum_left_impl(m, x, 1), m),
                 lambda m, g: (None, _sum_left_impl(m, g, 0)))


def _sigmoid(x):
    return 1.0 / (1.0 + jnp.exp(-x))


def _softplus(z):
    return jnp.maximum(z, 0.0) + jnp.log(1.0 + jnp.exp(-jnp.abs(z)))


_INV_SQRT2 = 1.0 / math.sqrt(2.0)
_INV_SQRT2PI = 1.0 / math.sqrt(2.0 * math.pi)


@jax.custom_vjp
def _gelu(x):
    return 0.5 * x * (1.0 + lax.erf(x * _INV_SQRT2))


_gelu.defvjp(lambda x: (0.5 * x * (1.0 + lax.erf(x * _INV_SQRT2)), x),
             lambda x, g: (g * (0.5 * (1.0 + lax.erf(x * _INV_SQRT2)) + x * jnp.exp(-0.5 * x * x) * _INV_SQRT2PI),))


def _rms(x, gain):
    r = lax.rsqrt(jnp.mean(x * x, axis=-1, keepdims=True) + NORM_EPS)
    return x * r * gain


def _normmod(x, gain, sc, sh):
    return _rms(x, gain) * (1.0 + sc) + sh


def _mm_call(name, a, b, out_shape, out_dtype, dims, grid, a_spec, b_spec, o_spec, acc_shape):
    nk = grid[2]

    def body(a_ref, b_ref, o_ref, *scratch):
        part = lax.dot_general(a_ref[...].astype(BF16), b_ref[...].astype(BF16), dims, preferred_element_type=F32)
        if nk == 1:
            o_ref[...] = part.astype(o_ref.dtype)
            return
        acc_ref, = scratch
        k = pl.program_id(2)

        @pl.when(k == 0)
        def _():
            acc_ref[...] = part

        @pl.when(k > 0)
        def _():
            acc_ref[...] += part

        @pl.when(k == nk - 1)
        def _():
            o_ref[...] = acc_ref[...].astype(o_ref.dtype)

    return pl.pallas_call(
        body, name=name, grid=grid, in_specs=[a_spec, b_spec], out_specs=o_spec,
        out_shape=jax.ShapeDtypeStruct(out_shape, out_dtype),
        scratch_shapes=[] if nk == 1 else [pltpu.VMEM(acc_shape, F32)],
        compiler_params=_cp("parallel", "parallel", "arbitrary"),
    )(a, b)


def _mm_nn(name, a, b, out_dtype=F32):
    M, K = a.shape
    chunked = b.ndim == 3
    Nc = b.shape[-1]
    N = Nc * (b.shape[0] if chunked else 1)
    tm = _pick(M, (1024, 512, 256, 128, 64, 32, 16, 8))
    tn = _pick(Nc, (1408, 1024, 896, 512, 256, 128))
    tk = _pick(K, (2048, 1408, 1024, 512, 256, 128))
    npc = Nc // tn
    if chunked:
        b_spec = pl.BlockSpec((None, tk, tn), lambda i, j, k: (j // npc, k, j % npc))
    else:
        b_spec = pl.BlockSpec((tk, tn), lambda i, j, k: (k, j))
    return _mm_call(name, a, b, (M, N), out_dtype, (((1,), (0,)), ((), ())), (M // tm, N // tn, K // tk),
                    pl.BlockSpec((tm, tk), lambda i, j, k: (i, k)), b_spec,
                    pl.BlockSpec((tm, tn), lambda i, j, k: (i, j)), (tm, tn))


def _mm_nt(name, a, b, out_dtype=F32):
    M, N = a.shape
    chunked = b.ndim == 3
    Nc = b.shape[-1]
    K = b.shape[-2]
    tm = _pick(M, (1024, 512, 256, 128, 64, 32, 16, 8))
    tn = _pick(K, (1408, 1024, 512, 256, 128))
    tk = _pick(Nc, (2048, 1792, 1408, 1024, 896, 512, 256, 128))
    npc = Nc // tk
    if chunked:
        b_spec = pl.BlockSpec((None, tn, tk), lambda i, j, k: (k // npc, j, k % npc))
    else:
        b_spec = pl.BlockSpec((tn, tk), lambda i, j, k: (j, k))
    return _mm_call(name, a, b, (M, K), out_dtype, (((1,), (1,)), ((), ())), (M // tm, K // tn, N // tk),
                    pl.BlockSpec((tm, tk), lambda i, j, k: (i, k)), b_spec,
                    pl.BlockSpec((tm, tn), lambda i, j, k: (i, j)), (tm, tn))


def _mm_tn(name, a, b, chunks=1, out_dtype=BF16):
    T, K = a.shape
    N = b.shape[1]
    Nc = N // chunks
    tm = _pick(K, (1408, 1024, 512, 256, 128))
    tn = _pick(Nc, (1408, 1024, 896, 512, 256, 128))
    tk = _pick(T, (1024, 512, 256, 128))
    npc = Nc // tn
    if chunks > 1:
        shape = (chunks, K, Nc)
        o_spec = pl.BlockSpec((None, tm, tn), lambda i, j, k: (j // npc, i, j % npc))
    else:
        shape = (K, N)
        o_spec = pl.BlockSpec((tm, tn), lambda i, j, k: (i, j))
    return _mm_call(name, a, b, shape, out_dtype, (((0,), (0,)), ((), ())), (K // tm, N // tn, T // tk),
                    pl.BlockSpec((tk, tm), lambda i, j, k: (k, i)),
                    pl.BlockSpec((tk, tn), lambda i, j, k: (k, j)), o_spec, (tm, tn))


def _row_tile(T):
    return _pick(T, (256, 128, 64, 32, 16, 8))


def _vec_spec(D):
    return pl.BlockSpec((1, D), lambda i: (0, 0))


def _normmod_fwd(name, x, gain, sc, sh):
    T, D = x.shape
    bt = _row_tile(T)

    def body(x_ref, g_ref, sc_ref, sh_ref, h_ref):
        h_ref[...] = _normmod(x_ref[...], g_ref[...], sc_ref[...], sh_ref[...]).astype(h_ref.dtype)

    rows = pl.BlockSpec((bt, D), lambda i: (i, 0))
    return pl.pallas_call(body, name=name, grid=(T // bt,), in_specs=[rows] + [_vec_spec(D)] * 3, out_specs=rows,
                          out_shape=jax.ShapeDtypeStruct((T, D), BF16), compiler_params=_cp("parallel"))(x, gain, sc, sh)


def _res_normmod_fwd(name, x, y, g, gain, sc, sh):
    T, D = x.shape
    bt = _row_tile(T)

    def body(x_ref, y_ref, gate_ref, g_ref, sc_ref, sh_ref, x1_ref, h_ref):
        x1 = x_ref[...] + gate_ref[...] * y_ref[...]
        x1_ref[...] = x1
        h_ref[...] = _normmod(x1, g_ref[...], sc_ref[...], sh_ref[...]).astype(h_ref.dtype)

    rows = pl.BlockSpec((bt, D), lambda i: (i, 0))
    return pl.pallas_call(body, name=name, grid=(T // bt,), in_specs=[rows, rows] + [_vec_spec(D)] * 4,
                          out_specs=[rows, rows],
                          out_shape=[jax.ShapeDtypeStruct((T, D), F32), jax.ShapeDtypeStruct((T, D), BF16)],
                          compiler_params=_cp("parallel"))(x, y, g, gain, sc, sh)


def _final_fwd_bwd(x, y, g, gain, target):
    T, D = x.shape
    bt = _row_tile(T)

    def body(x_ref, y_ref, gate_ref, g_ref, t_ref, loss_ref, dx_ref, dy_ref, dgate_ref, dgain_ref):
        i = pl.program_id(0)
        yv = y_ref[...]
        gate = gate_ref[...]
        x4 = x_ref[...] + gate * yv
        out, vjp = jax.vjp(_rms, x4, g_ref[...])
        err = out - t_ref[...]
        dx4, dgain = vjp(err * (1.0 / D))
        part = 0.5 * jnp.sum(jnp.mean(err * err, axis=-1, keepdims=True), axis=0, keepdims=True)

        @pl.when(i == 0)
        def _():
            loss_ref[...] = jnp.zeros_like(loss_ref)
            dgate_ref[...] = jnp.zeros_like(dgate_ref)
            dgain_ref[...] = jnp.zeros_like(dgain_ref)

        loss_ref[...] += jnp.broadcast_to(part, loss_ref.shape)
        dx_ref[...] = dx4
        dy_ref[...] = (gate * dx4).astype(dy_ref.dtype)
        dgate_ref[...] += jnp.sum(dx4 * yv, axis=0, keepdims=True)
        dgain_ref[...] += dgain

    rows = pl.BlockSpec((bt, D), lambda i: (i, 0))
    vec = _vec_spec(D)
    return pl.pallas_call(
        body, name="final_loss", grid=(T // bt,), in_specs=[rows, rows, vec, vec, rows],
        out_specs=[pl.BlockSpec((1, HEAD), lambda i: (0, 0)), rows, rows, vec, vec],
        out_shape=[jax.ShapeDtypeStruct((1, HEAD), F32), jax.ShapeDtypeStruct((T, D), F32),
                   jax.ShapeDtypeStruct((T, D), BF16), jax.ShapeDtypeStruct((1, D), F32),
                   jax.ShapeDtypeStruct((1, D), F32)],
        compiler_params=_cp("arbitrary"))(x, y, g, gain, target)


def _block_bwd(name, dx_out, dh, x_in, gain, sc, sh, y_prev=None, g_prev=None):
    T, D = x_in.shape
    bt = _row_tile(T)
    has_prev = y_prev is not None

    def body(*refs):
        if has_prev:
            dxo_ref, dh_ref, x_ref, g_ref, sc_ref, sh_ref, y_ref, gp_ref, dx_ref, dgain_ref, dsc_ref, dsh_ref, dy_ref, dgp_ref = refs
        else:
            dxo_ref, dh_ref, x_ref, g_ref, sc_ref, sh_ref, dx_ref, dgain_ref, dsc_ref, dsh_ref = refs
        i = pl.program_id(0)
        _, vjp = jax.vjp(_normmod, x_ref[...], g_ref[...], sc_ref[...], sh_ref[...])
        dxn, dgain, dsc, dsh = vjp(dh_ref[...])
        dx = dxo_ref[...] + dxn
        dx_ref[...] = dx

        @pl.when(i == 0)
        def _():
            dgain_ref[...] = jnp.zeros_like(dgain_ref)
            dsc_ref[...] = jnp.zeros_like(dsc_ref)
            dsh_ref[...] = jnp.zeros_like(dsh_ref)
            if has_prev:
                dgp_ref[...] = jnp.zeros_like(dgp_ref)

        dgain_ref[...] += dgain
        dsc_ref[...] += dsc
        dsh_ref[...] += dsh
        if has_prev:
            dy_ref[...] = (gp_ref[...] * dx).astype(dy_ref.dtype)
            dgp_ref[...] += jnp.sum(dx * y_ref[...], axis=0, keepdims=True)

    rows = pl.BlockSpec((bt, D), lambda i: (i, 0))
    vec = _vec_spec(D)
    ins = [dx_out, dh, x_in, gain, sc, sh]
    in_specs = [rows, rows, rows, vec, vec, vec]
    out_specs = [rows, vec, vec, vec]
    out_shape = [jax.ShapeDtypeStruct((T, D), F32)] + [jax.ShapeDtypeStruct((1, D), F32)] * 3
    if has_prev:
        ins += [y_prev, g_prev]
        in_specs += [rows, vec]
        out_specs += [rows, vec]
        out_shape += [jax.ShapeDtypeStruct((T, D), BF16), jax.ShapeDtypeStruct((1, D), F32)]
    return pl.pallas_call(body, name=name, grid=(T // bt,), in_specs=in_specs, out_specs=out_specs,
                          out_shape=out_shape, compiler_params=_cp("arbitrary"))(*ins)


def _sb_tiles(T):
    tq = _pick(T, (512, 256, 128))
    return tq, tq // HEAD


def _sb_fwd(proj, H):
    T = proj.shape[0]
    tq, nsub = _sb_tiles(T)
    scale = HEAD ** -0.5

    def body(q_ref, k_ref, v_ref, o_ref, l_ref, acc_ref):
        i = pl.program_id(1)
        q = q_ref[...].astype(BF16)
        later = (_iota((HEAD, HEAD), 0) > _iota((HEAD, HEAD), 1)).astype(BF16)
        row = _iota((tq, HEAD), 0)
        col = _iota((tq, HEAD), 1)

        def key_step(j, c, diagonal):
            off = pl.multiple_of(j * tq, tq)
            k = k_ref[pl.ds(off, tq), :].astype(BF16)
            v = v_ref[pl.ds(off, tq), :].astype(BF16)
            z = _dg(q, k, 1, 1) * scale
            ws = [None] * nsub
            for s in reversed(range(nsub)):
                zs = z[:, s * HEAD:(s + 1) * HEAD]
                sp = _softplus(zs)
                if diagonal:
                    strict = (s * HEAD + col) < row
                    lk = jnp.where(strict, -sp, 0.0)
                else:
                    lk = -sp
                w = jnp.exp(zs - sp + _sum_right(lk, later) + c)
                if diagonal:
                    w = jnp.where(strict, w, 0.0)
                ws[s] = w.astype(BF16)
                c = c + jnp.sum(lk, axis=1, keepdims=True)
            acc_ref[...] += _dg(jnp.concatenate(ws, axis=1), v, 1, 0)
            return c

        acc_ref[...] = jnp.zeros_like(acc_ref)
        c = key_step(i, jnp.zeros((tq, 1), F32), True)
        c = lax.fori_loop(0, i, lambda n, c: key_step(i - 1 - n, c, False), c)
        o_ref[...] = acc_ref[...].astype(o_ref.dtype)
        l_ref[...] = jnp.broadcast_to(c, (tq, HEAD))

    blk = pl.BlockSpec((tq, HEAD), lambda h, i: (i, h))
    return pl.pallas_call(
        body, name="sb_fwd", grid=(H, T // tq),
        in_specs=[blk, pl.BlockSpec((T, HEAD), lambda h, i: (0, H + h)), pl.BlockSpec((T, HEAD), lambda h, i: (0, 2 * H + h))],
        out_specs=[blk, blk],
        out_shape=[jax.ShapeDtypeStruct((T, H * HEAD), BF16), jax.ShapeDtypeStruct((T, H * HEAD), F32)],
        scratch_shapes=[pltpu.VMEM((tq, HEAD), F32)],
        compiler_params=_cp("parallel", "arbitrary"))(proj, proj, proj)


def _sb_bwd(proj, do, L, H):
    T = proj.shape[0]
    tq, nsub = _sb_tiles(T)
    scale = HEAD ** -0.5

    def body(q_ref, k_ref, v_ref, do_ref, l_ref, dq_ref, dk_ref, dv_ref):
        i = pl.program_id(1)

        @pl.when(i == 0)
        def _():
            dk_ref[...] = jnp.zeros_like(dk_ref)
            dv_ref[...] = jnp.zeros_like(dv_ref)

        dq_ref[...] = jnp.zeros_like(dq_ref)
        q = q_ref[...].astype(BF16)
        do_ = do_ref[...].astype(BF16)
        total = l_ref[...]
        upto = (_iota((HEAD, HEAD), 0) <= _iota((HEAD, HEAD), 1)).astype(BF16)
        before = (_iota((HEAD, HEAD), 0) < _iota((HEAD, HEAD), 1)).astype(BF16)
        row = _iota((tq, HEAD), 0)
        col = _iota((tq, HEAD), 1)

        def key_step(j, carry, diagonal):
            cp, ce = carry
            off = pl.multiple_of(j * tq, tq)
            k = k_ref[pl.ds(off, tq), :].astype(BF16)
            v = v_ref[pl.ds(off, tq), :].astype(BF16)
            z = _dg(q, k, 1, 1) * scale
            dw = _dg(do_, v, 1, 1)
            ws, dzs = [], []
            for s in range(nsub):
                zs = z[:, s * HEAD:(s + 1) * HEAD]
                sp = _softplus(zs)
                if diagonal:
                    strict = (s * HEAD + col) < row
                    lk = jnp.where(strict, -sp, 0.0)
                else:
                    lk = -sp
                tail = total - (_sum_right(lk, upto) + cp)
                w = jnp.exp(zs - sp + tail)
                if diagonal:
                    w = jnp.where(strict, w, 0.0)
                e = w * dw[:, s * HEAD:(s + 1) * HEAD]
                e_before = _sum_right(e, before) + ce
                sig = jnp.exp(zs - sp)
                dz = (e * (1.0 - sig) - e_before * sig) * scale
                if diagonal:
                    dz = jnp.where(strict, dz, 0.0)
                ws.append(w.astype(BF16))
                dzs.append(dz.astype(BF16))
                cp = cp + jnp.sum(lk, axis=1, keepdims=True)
                ce = ce + jnp.sum(e, axis=1, keepdims=True)
            w_all = jnp.concatenate(ws, axis=1)
            dz_all = jnp.concatenate(dzs, axis=1)
            dv_ref[pl.ds(off, tq), :] += _dg(w_all, do_, 0, 0)
            dk_ref[pl.ds(off, tq), :] += _dg(dz_all, q, 0, 0)
            dq_ref[...] += _dg(dz_all, k, 1, 0)
            return cp, ce

        zero = jnp.zeros((tq, 1), F32)
        carry = lax.fori_loop(0, i, lambda j, cr: key_step(j, cr, False), (zero, zero))
        key_step(i, carry, True)

    blk = pl.BlockSpec((tq, HEAD), lambda h, i: (i, h))
    full = pl.BlockSpec((T, HEAD), lambda h, i: (0, h))
    shp = jax.ShapeDtypeStruct((T, H * HEAD), F32)
    return pl.pallas_call(
        body, name="sb_bwd", grid=(H, T // tq),
        in_specs=[blk, pl.BlockSpec((T, HEAD), lambda h, i: (0, H + h)), pl.BlockSpec((T, HEAD), lambda h, i: (0, 2 * H + h)),
                  blk, blk],
        out_specs=[blk, full, full], out_shape=[shp, shp, shp],
        compiler_params=_cp("parallel", "arbitrary"))(proj, proj, proj, do, L)


def _hg_tile(q, fl, iv, g, st, l0, l1, gain):
    R = 2 * HG_CHUNK
    row = _iota((R, R), 0)
    col = _iota((R, R), 1)
    first = row < HG_CHUNK
    same = first == (col < HG_CHUNK)
    tri = (row >= col) & same
    lb = _sigmoid(l0 - l1)
    f = lb + (1.0 - lb) * _sigmoid(fl)
    logf = jnp.log(f)
    k = 1.0 - f
    qf = q * _sigmoid(q)
    G = _sum_left(tri.astype(BF16), logf)
    gl_a = jnp.sum(jnp.where(first, logf, 0.0), axis=0, keepdims=True)
    gl_b = jnp.sum(jnp.where(first, 0.0, logf), axis=0, keepdims=True)
    q_dec = qf * jnp.exp(G)
    k_inv = k * jnp.exp(-G)
    k_end = k * jnp.exp(jnp.where(first, gl_a, gl_b) - G)
    scores = jnp.where(tri, mm_nt(q_dec, k_inv), 0.0)
    o = mm_nn(scores, iv)
    o_a = mm_nt(q_dec, st)
    st_mid = st * jnp.exp(gl_a) + mm_tn(jnp.where(first, iv, 0.0), k_end)
    o_b = mm_nt(q_dec, st_mid)
    st_new = st_mid * jnp.exp(gl_b) + mm_tn(jnp.where(first, 0.0, iv), k_end)
    o = o + jnp.where(first, o_a, o_b)
    on = o * lax.rsqrt(jnp.mean(o * o, axis=-1, keepdims=True) + NORM_EPS) * gain
    return on * (g * _sigmoid(g)), st_new


def _hg_heads(H):
    return _pick(H, (4, 2, 1))


def _hg_specs(H, c0, rev, nt):
    hb = _hg_heads(H)
    w = hb * HEAD

    def at(base):
        if rev:
            return pl.BlockSpec((HEAD, w), lambda h, i: (nt - 1 - i, base // hb + h))
        return pl.BlockSpec((HEAD, w), lambda h, i: (i, base // hb + h))
    return [at(c0), at(c0 + H), at(c0 + 2 * H), at(c0 + 3 * H)]


def _hg_fwd(proj, l0, l1, gain, H, c0):
    T = proj.shape[0]
    nt = T // HEAD
    hb = _hg_heads(H)
    w = hb * HEAD

    def body(q_ref, f_ref, i_ref, g_ref, l0_ref, l1_ref, gain_ref, o_ref, st_out_ref, st_ref):
        @pl.when(pl.program_id(1) == 0)
        def _():
            st_ref[...] = jnp.zeros_like(st_ref)

        for j in range(hb):
            s = slice(j * HEAD, (j + 1) * HEAD)
            st = st_ref[j]
            st_out_ref[j] = st
            out, st_new = _hg_tile(q_ref[:, s], f_ref[:, s], i_ref[:, s], g_ref[:, s], st, l0_ref[:, s], l1_ref[:, s],
                                   gain_ref[:, s])
            o_ref[:, s] = out.astype(o_ref.dtype)
            st_ref[j] = st_new

    vec = pl.BlockSpec((1, w), lambda h, i: (0, h))
    return pl.pallas_call(
        body, name="hg_fwd", grid=(H // hb, nt), in_specs=_hg_specs(H, c0, False, nt) + [vec, vec, vec],
        out_specs=[pl.BlockSpec((HEAD, w), lambda h, i: (i, h)),
                   pl.BlockSpec((hb, None, HEAD, HEAD), lambda h, i: (h, i, 0, 0))],
        out_shape=[jax.ShapeDtypeStruct((T, H * HEAD), BF16), jax.ShapeDtypeStruct((H, nt, HEAD, HEAD), F32)],
        scratch_shapes=[pltpu.VMEM((hb, HEAD, HEAD), F32)],
        compiler_params=_cp("parallel", "arbitrary"))(proj, proj, proj, proj, l0, l1, gain)


def _hg_bwd(proj, states, do, l0, l1, gain, H, c0, do_c0):
    T = proj.shape[0]
    nt = T // HEAD
    hb = _hg_heads(H)
    w = hb * HEAD

    def body(q_ref, f_ref, i_ref, g_ref, st_in_ref, do_ref, l0_ref, l1_ref, gain_ref,
             dq_ref, df_ref, di_ref, dg_ref, dl0_ref, dl1_ref, dgain_ref, dst_ref):
        @pl.when(pl.program_id(1) == 0)
        def _():
            dst_ref[...] = jnp.zeros_like(dst_ref)
            dl0_ref[...] = jnp.zeros_like(dl0_ref)
            dl1_ref[...] = jnp.zeros_like(dl1_ref)
            dgain_ref[...] = jnp.zeros_like(dgain_ref)

        for j in range(hb):
            s = slice(j * HEAD, (j + 1) * HEAD)
            _, vjp = jax.vjp(_hg_tile, q_ref[:, s], f_ref[:, s], i_ref[:, s], g_ref[:, s], st_in_ref[j],
                             l0_ref[:, s], l1_ref[:, s], gain_ref[:, s])
            dq, df, di, dg, dst, dl0, dl1, dgain = vjp((do_ref[:, s], dst_ref[j]))
            dq_ref[:, s] = dq
            df_ref[:, s] = df
            di_ref[:, s] = di
            dg_ref[:, s] = dg
            dst_ref[j] = dst
            dl0_ref[:, s] += dl0
            dl1_ref[:, s] += dl1
            dgain_ref[:, s] += dgain

    vec = pl.BlockSpec((1, w), lambda h, i: (0, h))
    rblk = pl.BlockSpec((HEAD, w), lambda h, i: (nt - 1 - i, h))
    shp = jax.ShapeDtypeStruct((T, H * HEAD), F32)
    vshp = jax.ShapeDtypeStruct((1, H * HEAD), F32)
    return pl.pallas_call(
        body, name="hg_bwd", grid=(H // hb, nt),
        in_specs=_hg_specs(H, c0, True, nt) + [
            pl.BlockSpec((hb, None, HEAD, HEAD), lambda h, i: (h, nt - 1 - i, 0, 0)),
            pl.BlockSpec((HEAD, w), lambda h, i: (nt - 1 - i, do_c0 // hb + h)), vec, vec, vec],
        out_specs=[rblk, rblk, rblk, rblk, vec, vec, vec],
        out_shape=[shp, shp, shp, shp, vshp, vshp, vshp],
        scratch_shapes=[pltpu.VMEM((hb, HEAD, HEAD), F32)],
        compiler_params=_cp("parallel", "arbitrary"))(proj, proj, proj, proj, states, do, l0, l1, gain)


def _sg_chunk(u_parts, v_parts, gains, biases, wpos, bpos):
    W = sum(p.shape[1] for p in v_parts)
    C = v_parts[0].shape[0]
    v = [_gelu(p) for p in v_parts]
    mu = sum(jnp.sum(p, axis=-1, keepdims=True) for p in v) * (1.0 / W)
    xc = [p - mu for p in v]
    r = lax.rsqrt(sum(jnp.sum(p * p, axis=-1, keepdims=True) for p in xc) * (1.0 / W) + NORM_EPS)
    causal = _iota((C, C), 0) >= _iota((C, C), 1)
    out = []
    for up, p, gn, bs, w, b in zip(u_parts, xc, gains, biases, wpos, bpos):
        vn = p * r * gn + bs
        mixed = mm_nn(jnp.where(causal, w, 0.0), vn) + b
        out.append(_gelu(up) * mixed)
    return out


def _sg_fwd(zpre, vgain, vbias, wpos, bpos):
    T, W2 = zpre.shape
    W = W2 // 2
    G = wpos.shape[0]
    cg = W // G
    C = SG_CHUNK

    def body(z_ref, gn_ref, bs_ref, w_ref, b_ref, s_ref):
        sl = [slice(g * cg, (g + 1) * cg) for g in range(G)]
        out = _sg_chunk([z_ref[:, s] for s in sl], [z_ref[:, W + s.start:W + s.stop] for s in sl],
                        [gn_ref[:, s] for s in sl], [bs_ref[:, s] for s in sl],
                        [w_ref[g] for g in range(G)], [b_ref[g] for g in range(G)])
        for s, o in zip(sl, out):
            s_ref[:, s] = o.astype(s_ref.dtype)

    return pl.pallas_call(
        body, name="sg_fwd", grid=(T // C,),
        in_specs=[pl.BlockSpec((C, W2), lambda i: (i, 0)), _vec_spec(W), _vec_spec(W),
                  pl.BlockSpec((G, C, C), lambda i: (0, 0, 0)), pl.BlockSpec((G, C, 1), lambda i: (0, 0, 0))],
        out_specs=pl.BlockSpec((C, W), lambda i: (i, 0)),
        out_shape=jax.ShapeDtypeStruct((T, W), BF16), compiler_params=_cp("parallel"))(zpre, vgain, vbias, wpos, bpos)


def _sg_bwd(zpre, ds, vgain, vbias, wpos, bpos):
    T, W2 = zpre.shape
    W = W2 // 2
    G = wpos.shape[0]
    cg = W // G
    C = SG_CHUNK

    def body(z_ref, ds_ref, gn_ref, bs_ref, w_ref, b_ref, dz_ref, dgn_ref, dbs_ref, dw_ref, db_ref):
        @pl.when(pl.program_id(0) == 0)
        def _():
            dgn_ref[...] = jnp.zeros_like(dgn_ref)
            dbs_ref[...] = jnp.zeros_like(dbs_ref)
            dw_ref[...] = jnp.zeros_like(dw_ref)
            db_ref[...] = jnp.zeros_like(db_ref)

        sl = [slice(g * cg, (g + 1) * cg) for g in range(G)]
        _, vjp = jax.vjp(_sg_chunk, [z_ref[:, s] for s in sl], [z_ref[:, W + s.start:W + s.stop] for s in sl],
                         [gn_ref[:, s] for s in sl], [bs_ref[:, s] for s in sl],
                         [w_ref[g] for g in range(G)], [b_ref[g] for g in range(G)])
        du, dv, dgn, dbs, dw, db = vjp([ds_ref[:, s] for s in sl])
        for g, s in enumerate(sl):
            dz_ref[:, s] = du[g].astype(dz_ref.dtype)
            dz_ref[:, W + s.start:W + s.stop] = dv[g].astype(dz_ref.dtype)
            dgn_ref[:, s] += dgn[g]
            dbs_ref[:, s] += dbs[g]
            dw_ref[g] += dw[g]
            db_ref[g] += db[g]

    wspec = pl.BlockSpec((G, C, C), lambda i: (0, 0, 0))
    bspec = pl.BlockSpec((G, C, 1), lambda i: (0, 0, 0))
    return pl.pallas_call(
        body, name="sg_bwd", grid=(T // C,),
        in_specs=[pl.BlockSpec((C, W2), lambda i: (i, 0)), pl.BlockSpec((C, W), lambda i: (i, 0)),
                  _vec_spec(W), _vec_spec(W), wspec, bspec],
        out_specs=[pl.BlockSpec((C, W2), lambda i: (i, 0)), _vec_spec(W), _vec_spec(W), wspec, bspec],
        out_shape=[jax.ShapeDtypeStruct((T, W2), BF16), jax.ShapeDtypeStruct((1, W), F32),
                   jax.ShapeDtypeStruct((1, W), F32), jax.ShapeDtypeStruct((G, C, C), F32),
                   jax.ShapeDtypeStruct((G, C, 1), F32)],
        compiler_params=_cp("arbitrary"))(zpre, ds, vgain, vbias, wpos, bpos)


def _conv_tiles(T, F):
    return _pick(T, (512, 256, 128, 64, 32, 16, 8)), _pick(F, (512, 256, 128))


def _shift_down(cur, prev8, n, first_tile):
    bt = cur.shape[0]
    r = pltpu.roll(cur, n, 0)
    p = pltpu.roll(prev8, n, 0)
    p = jnp.where(first_tile, 0.0, p)
    head = jnp.concatenate([p, r[8:]], axis=0) if bt > 8 else p
    return jnp.where(_iota(cur.shape, 0) < n, head, r)


def _shift_up(cur, next8, n, last_tile):
    bt = cur.shape[0]
    r = pltpu.roll(cur, bt - n, 0)
    p = pltpu.roll(next8, 8 - n, 0)
    p = jnp.where(last_tile, 0.0, p)
    tail = jnp.concatenate([r[:bt - 8], p], axis=0) if bt > 8 else p
    return jnp.where(_iota(cur.shape, 0) >= bt - n, tail, r)


def _conv_apply(cur, prev8, w_ref, b, first_tile):
    return (b + w_ref[0:1, :] * _shift_down(cur, prev8, 2, first_tile)
            + w_ref[1:2, :] * _shift_down(cur, prev8, 1, first_tile) + w_ref[2:3, :] * cur)


def _conv_fwd(name, a, w, b):
    T, F2 = a.shape
    F = F2 // 2
    bt, cw = _conv_tiles(T, F)
    nf = F // cw
    r8 = bt // 8

    def body(g_ref, gp_ref, v_ref, vp_ref, wg_ref, wv_ref, bg_ref, bv_ref, u_ref):
        first = pl.program_id(0) == 0
        gate = _conv_apply(g_ref[...], gp_ref[...], wg_ref, bg_ref[...], first)
        val = _conv_apply(v_ref[...], vp_ref[...], wv_ref, bv_ref[...], first)
        u_ref[...] = (gate * _sigmoid(gate) * val).astype(u_ref.dtype)

    def cur(off):
        return pl.BlockSpec((bt, cw), lambda i, j: (i, j + off))

    def prev(off):
        return pl.BlockSpec((8, cw), lambda i, j: (jnp.maximum(i * r8 - 1, 0), j + off))

    def vec(rows, off):
        return pl.BlockSpec((rows, cw), lambda i, j: (0, j + off))

    return pl.pallas_call(
        body, name=name, grid=(T // bt, nf),
        in_specs=[cur(0), prev(0), cur(nf), prev(nf), vec(3, 0), vec(3, nf), vec(1, 0), vec(1, nf)],
        out_specs=pl.BlockSpec((bt, cw), lambda i, j: (i, j)),
        out_shape=jax.ShapeDtypeStruct((T, F), BF16),
        compiler_params=_cp("parallel", "parallel"))(a, a, a, a, w, w, b, b)


def _conv_bwd(name, a, du, w, b):
    T, F2 = a.shape
    F = F2 // 2
    bt, cw = _conv_tiles(T, F)
    nf = F // cw
    r8 = bt // 8
    last_blk = T // 8 - 1

    def body(g_ref, gp_ref, gn_ref, v_ref, vp_ref, vn_ref, du_ref, dun_ref, wg_ref, wv_ref, bg_ref, bv_ref,
             dag_ref, dav_ref, dwg_ref, dwv_ref, dbg_ref, dbv_ref):
        i = pl.program_id(1)
        first = i == 0
        last = i == pl.num_programs(1) - 1

        def taps(cur, prev8, at_start):
            return _shift_down(cur, prev8, 2, at_start), _shift_down(cur, prev8, 1, at_start), cur

        def conv(t, w_ref, b_ref):
            return b_ref[...] + w_ref[0:1, :] * t[0] + w_ref[1:2, :] * t[1] + w_ref[2:3, :] * t[2]

        def act_bwd(gate, val, du_):
            sg = _sigmoid(gate)
            return du_ * val * (sg * (1.0 + gate * (1.0 - sg))), du_ * gate * sg

        g_cur, v_cur = g_ref[...], v_ref[...]
        tg = taps(g_cur, gp_ref[...], first)
        tv = taps(v_cur, vp_ref[...], first)
        dg, dv = act_bwd(conv(tg, wg_ref, bg_ref), conv(tv, wv_ref, bv_ref), du_ref[...])
        tgn = taps(gn_ref[...], g_cur[bt - 8:, :], False)
        tvn = taps(vn_ref[...], v_cur[bt - 8:, :], False)
        dgn, dvn = act_bwd(conv(tgn, wg_ref, bg_ref), conv(tvn, wv_ref, bv_ref), dun_ref[...])

        def conv_t(d, dn, w_ref):
            return w_ref[2:3, :] * d + w_ref[1:2, :] * _shift_up(d, dn, 1, last) + w_ref[0:1, :] * _shift_up(d, dn, 2, last)

        dag_ref[...] = conv_t(dg, dgn, wg_ref).astype(dag_ref.dtype)
        dav_ref[...] = conv_t(dv, dvn, wv_ref).astype(dav_ref.dtype)

        @pl.when(first)
        def _():
            dwg_ref[...] = jnp.zeros_like(dwg_ref)
            dwv_ref[...] = jnp.zeros_like(dwv_ref)
            dbg_ref[...] = jnp.zeros_like(dbg_ref)
            dbv_ref[...] = jnp.zeros_like(dbv_ref)

        for t in range(CONV_WIDTH):
            dwg_ref[t:t + 1, :] += jnp.sum(dg * tg[t], axis=0, keepdims=True)
            dwv_ref[t:t + 1, :] += jnp.sum(dv * tv[t], axis=0, keepdims=True)
        dbg_ref[...] += jnp.sum(dg, axis=0, keepdims=True)
        dbv_ref[...] += jnp.sum(dv, axis=0, keepdims=True)

    def cur(off):
        return pl.BlockSpec((bt, cw), lambda j, i: (i, j + off))

    def prev(off):
        return pl.BlockSpec((8, cw), lambda j, i: (jnp.maximum(i * r8 - 1, 0), j + off))

    def nxt(off):
        return pl.BlockSpec((8, cw), lambda j, i: (jnp.minimum((i + 1) * r8, last_blk), j + off))

    def vec(rows, off):
        return pl.BlockSpec((rows, cw), lambda j, i: (0, j + off))

    half = jax.ShapeDtypeStruct((T, F), BF16)
    dag, dav, dwg, dwv, dbg, dbv = pl.pallas_call(
        body, name=name, grid=(nf, T // bt),
        in_specs=[cur(0), prev(0), nxt(0), cur(nf), prev(nf), nxt(nf), cur(0), nxt(0),
                  vec(3, 0), vec(3, nf), vec(1, 0), vec(1, nf)],
        out_specs=[cur(0), cur(0), vec(3, 0), vec(3, 0), vec(1, 0), vec(1, 0)],
        out_shape=[half, half, jax.ShapeDtypeStruct((3, F), F32), jax.ShapeDtypeStruct((3, F), F32),
                   jax.ShapeDtypeStruct((1, F), F32), jax.ShapeDtypeStruct((1, F), F32)],
        compiler_params=_cp("parallel", "arbitrary"))(a, a, a, a, a, a, du, du, w, w, b, b)
    return dag, dav, jnp.concatenate([dwg, dwv], axis=1), jnp.concatenate([dbg, dbv], axis=1)


def _ada_fwd(c_all, ada_w, ada_b):
    R, D = c_all.shape
    L, _, Ns = ada_w.shape
    tn = _pick(Ns, (512, 256, 128))

    def body(c_ref, w_ref, b_ref, o_ref):
        cv = c_ref[...]
        cond = cv * _sigmoid(cv)
        o_ref[...] = _dg(cond, w_ref[...], 1, 0) + b_ref[...]

    return pl.pallas_call(
        body, name="ada_fwd", grid=(L, Ns // tn),
        in_specs=[pl.BlockSpec((R, D), lambda l, j: (0, 0)), pl.BlockSpec((None, D, tn), lambda l, j: (l, 0, j)),
                  pl.BlockSpec((None, 1, tn), lambda l, j: (l, 0, j))],
        out_specs=pl.BlockSpec((None, R, tn), lambda l, j: (l, 0, j)),
        out_shape=jax.ShapeDtypeStruct((L, R, Ns), F32), compiler_params=_cp("parallel", "parallel"))(c_all, ada_w, ada_b)


def _adam_math(w, g, m, v):
    m2 = ADAM_B1 * m + (1.0 - ADAM_B1) * g
    v2 = ADAM_B2 * v + (1.0 - ADAM_B2) * (g * g)
    m_hat = m2 / (1.0 - ADAM_B1 ** ADAM_STEP)
    v_hat = v2 / (1.0 - ADAM_B2 ** ADAM_STEP)
    delta = -ADAM_LR * (m_hat / (jnp.sqrt(v_hat) + ADAM_EPS) + ADAM_WD * w)
    return delta, m2, v2


def _ada_grad_adam(c_all_t, dmod, w, m, v):
    D, R = c_all_t.shape
    L, _, Ns = dmod.shape
    tr = _rows_within(D, Ns * 4, 1 << 20)

    def body(c_ref, d_ref, w_ref, m_ref, v_ref, g_ref, dl_ref, m2_ref, v2_ref):
        cv = c_ref[...]
        g = _dg(cv * _sigmoid(cv), d_ref[...], 1, 0)
        g_ref[...] = g
        dl_ref[...], m2_ref[...], v2_ref[...] = _adam_math(w_ref[...], g, m_ref[...], v_ref[...])

    big = pl.BlockSpec((None, tr, Ns), lambda l, i: (l, i, 0))
    shp = jax.ShapeDtypeStruct((L, D, Ns), F32)
    return pl.pallas_call(
        body, name="ada_grad_adam", grid=(L, D // tr),
        in_specs=[pl.BlockSpec((tr, R), lambda l, i: (i, 0)), pl.BlockSpec((None, R, Ns), lambda l, i: (l, 0, 0)), big, big, big],
        out_specs=[big] * 4, out_shape=[shp] * 4, compiler_params=_cp("parallel", "parallel"))(c_all_t, dmod, w, m, v)


def _adam(name, w, g, m, v):
    R, C = w.shape
    tr = _rows_within(R, C * 4, 1 << 21)

    def body(w_ref, g_ref, m_ref, v_ref, dl_ref, m2_ref, v2_ref):
        dl_ref[...], m2_ref[...], v2_ref[...] = _adam_math(w_ref[...], g_ref[...], m_ref[...], v_ref[...])

    blk = pl.BlockSpec((tr, C), lambda i: (i, 0))
    shp = jax.ShapeDtypeStruct((R, C), F32)
    return pl.pallas_call(body, name=name, grid=(R // tr,), in_specs=[blk] * 4, out_specs=[blk] * 3,
                          out_shape=[shp] * 3, compiler_params=_cp("parallel"))(w, g, m, v)


def _cast_into_rows(name, w, chip, after=None):
    _, R, C = w.shape
    tr = _rows_within(R, C * 4, 1 << 22)
    extra = [] if after is None else [after]

    def body(chip_ref, w_ref, *rest):
        o_ref = rest[-1]
        o_ref[...] = w_ref[...].astype(BF16)

    grid_spec = pltpu.PrefetchScalarGridSpec(
        num_scalar_prefetch=1, grid=(2, R // tr),
        in_specs=[pl.BlockSpec((None, tr, C), lambda h, i, s: (h, i, 0))] + [ANY] * len(extra),
        out_specs=pl.BlockSpec((None, tr, C), lambda h, i, s: (2 * s[0] + h, i, 0)))
    return pl.pallas_call(body, name=name, grid_spec=grid_spec, out_shape=jax.ShapeDtypeStruct((N_DEV, R, C), BF16),
                          compiler_params=_cp("arbitrary", "arbitrary"))(chip.reshape(1).astype(jnp.int32), w, *extra)


def _add_pairs(name, eight, from_sib, c):
    _, R, C = from_sib.shape
    tr = _rows_within(R, C * 2, 1 << 21)

    def body(c_ref, a_ref, b_ref, o_ref):
        o_ref[...] = (a_ref[...].astype(F32) + b_ref[...].astype(F32)).astype(o_ref.dtype)

    blk = pl.BlockSpec((None, tr, C), lambda j, i, s: (j, i, 0))
    grid_spec = pltpu.PrefetchScalarGridSpec(
        num_scalar_prefetch=1, grid=(4, R // tr),
        in_specs=[pl.BlockSpec((None, tr, C), lambda j, i, s: (2 * j + s[0], i, 0)), blk], out_specs=blk)
    return pl.pallas_call(body, name=name, grid_spec=grid_spec, out_shape=jax.ShapeDtypeStruct(from_sib.shape, BF16),
                          compiler_params=_cp("arbitrary", "arbitrary"))(c.reshape(1).astype(jnp.int32), eight, from_sib)


def _sum_into_pair(name, own, landed, slot, chip):
    n, R, C = landed.shape
    tr = _rows_within(R, (n + 1) * C * landed.dtype.itemsize, 1 << 23)

    def body(idx_ref, own_ref, x_ref, o_ref):
        mine = idx_ref[1]
        acc = None
        for j in range(n):
            part = jnp.where(mine == j, own_ref[...], x_ref[j]).astype(F32)
            acc = part if acc is None else acc + part
        o_ref[...] = acc

    grid_spec = pltpu.PrefetchScalarGridSpec(
        num_scalar_prefetch=1, grid=(R // tr,),
        in_specs=[pl.BlockSpec((None, tr, C), lambda i, s: (s[1], i, 0)), pl.BlockSpec((n, tr, C), lambda i, s: (0, i, 0))],
        out_specs=pl.BlockSpec((None, tr, C), lambda i, s: (s[0], i, 0)))
    idx = jnp.stack([slot, chip]).astype(jnp.int32)
    return pl.pallas_call(body, name=name, grid_spec=grid_spec, out_shape=jax.ShapeDtypeStruct((2, R, C), F32),
                          compiler_params=_cp("arbitrary"))(idx, own, landed)


def _sum_leading(name, a, out_dtype=F32):
    n, R, C = a.shape
    tr = _rows_within(R, n * C * a.dtype.itemsize, 1 << 23)

    def body(a_ref, o_ref):
        acc = a_ref[0].astype(F32)
        for j in range(1, n):
            acc = acc + a_ref[j].astype(F32)
        o_ref[...] = acc.astype(o_ref.dtype)

    return pl.pallas_call(body, name=name, grid=(R // tr,), in_specs=[pl.BlockSpec((n, tr, C), lambda i: (0, i, 0))],
                          out_specs=pl.BlockSpec((tr, C), lambda i: (i, 0)),
                          out_shape=jax.ShapeDtypeStruct((R, C), out_dtype), compiler_params=_cp("parallel"))(a)


def _place():
    return lax.axis_index("x"), lax.axis_index("y"), lax.axis_index("c")


def _all_gather(name, blocks, halves=False, after=None):
    n = len(blocks)
    shapes = [b.shape[1:] if halves else b.shape for b in blocks]
    extra = [] if after is None else [after]

    def body(*refs):
        ins, outs = refs[:n], refs[n + len(extra):2 * n + len(extra)]
        send_sems, recv_sems, local_sems = refs[2 * n + len(extra):]
        x, y, c = _place()
        me, sibling = (x, y, c), (x, y, 1 - c)
        chips = [(1 - x, y), (x, 1 - y), (1 - x, 1 - y)]

        def rows(a, px, py, pc):
            return outs[a].at[4 * px + 2 * py + pc]

        def copy(a, k, block, to, src=None):
            return pltpu.make_async_remote_copy(
                src_ref=rows(a, *block) if src is None else src, dst_ref=rows(a, *block),
                send_sem=send_sems.at[7 * a + k], recv_sem=recv_sems.at[7 * a + k],
                device_id=to, device_id_type=MESH)

        started = []
        mine = []
        for a in range(n):
            src = ins[a].at[c] if halves else ins[a]
            mine.append(pltpu.make_async_copy(src, rows(a, *me), local_sems.at[a]))
            mine[-1].start()
            first = [copy(a, 0, me, sibling, src=src)]
            first += [copy(a, 1 + j, me, (*chip, c), src=src) for j, chip in enumerate(chips)]
            for cp in first:
                cp.start()
            started += first
        for j, chip in enumerate(chips):
            for a in range(n):
                copy(a, 1 + j, (*chip, c), me).wait_recv()
                passed = copy(a, 4 + j, (*chip, c), sibling)
                passed.start()
                started.append(passed)
        for a in range(n):
            copy(a, 0, sibling, me).wait_recv()
            for j, chip in enumerate(chips):
                copy(a, 4 + j, (*chip, 1 - c), me).wait_recv()
        for cp in started:
            cp.wait_send()
        for cp in mine:
            cp.wait()

    return pl.pallas_call(
        body, name=name, in_specs=[ANY] * (n + len(extra)), out_specs=[ANY] * n,
        out_shape=[jax.ShapeDtypeStruct((N_DEV,) + tuple(s), b.dtype) for s, b in zip(shapes, blocks)],
        scratch_shapes=[pltpu.SemaphoreType.DMA((7 * n,)), pltpu.SemaphoreType.DMA((7 * n,)),
                        pltpu.SemaphoreType.DMA((n,))],
    )(*blocks, *extra)


def _share_halves(name, arrays):
    n = len(arrays)

    def body(*refs):
        ins, outs = refs[:n], refs[n:2 * n]
        send_sems, recv_sems = refs[2 * n:]
        x, y, c = _place()
        started = []
        for a in range(n):
            cp = pltpu.make_async_remote_copy(src_ref=ins[a].at[c], dst_ref=outs[a].at[c], send_sem=send_sems.at[a],
                                              recv_sem=recv_sems.at[a], device_id=(x, y, 1 - c), device_id_type=MESH)
            cp.start()
            started.append(cp)
        for a in range(n):
            started[a].wait_send()
            pltpu.make_async_remote_copy(src_ref=ins[a].at[1 - c], dst_ref=outs[a].at[1 - c], send_sem=send_sems.at[a],
                                         recv_sem=recv_sems.at[a], device_id=(x, y, 1 - c), device_id_type=MESH).wait_recv()

    return pl.pallas_call(
        body, name=name, in_specs=[ANY] * n, out_specs=[ANY] * n,
        out_shape=[jax.ShapeDtypeStruct(a.shape, a.dtype) for a in arrays],
        input_output_aliases={a: a for a in range(n)},
        scratch_shapes=[pltpu.SemaphoreType.DMA((n,)), pltpu.SemaphoreType.DMA((n,))],
    )(*arrays)


HBM = pl.BlockSpec(memory_space=pltpu.HBM)
SEM = pl.BlockSpec(memory_space=pltpu.SEMAPHORE)
EFFECT = pltpu.SideEffectType.DATAFLOW_SIDE_EFFECTING


COPIES_PER_ARRAY = {"rows": 3, "fill": 3, "parts": 3, "halves": 4}


def _chip_copies(kind, srcs, dsts, send_sems, recv_sems):
    x, y, c = _place()
    mine = 2 * x + y
    per = COPIES_PER_ARRAY[kind]
    sends, arrivals = [], []
    for a in range(len(srcs)):
        if kind == "halves":
            for j in range(N_CHIP):
                cp = pltpu.make_async_remote_copy(
                    src_ref=srcs[a].at[2 * j + 1 - c], dst_ref=dsts[a].at[j], send_sem=send_sems.at[per * a + j],
                    recv_sem=recv_sems.at[per * a + j], device_id=(x, y, 1 - c), device_id_type=MESH)
                sends.append(cp)
                arrivals.append(cp)
            continue
        for k, (px, py) in enumerate([(1 - x, y), (x, 1 - y), (1 - x, 1 - y)]):
            other = 2 * px + py
            if kind == "fill":
                cp = dict(send_sem=send_sems.at[per * a + k], recv_sem=recv_sems.at[per * a + k], device_id=(x, y, 1 - c),
                          device_id_type=MESH)
                sends.append(pltpu.make_async_remote_copy(src_ref=srcs[a].at[2 * other + c], dst_ref=dsts[a].at[2 * other + c], **cp))
                arrivals.append(pltpu.make_async_remote_copy(src_ref=srcs[a].at[2 * other + c],
                                                             dst_ref=dsts[a].at[2 * other + 1 - c], **cp))
                continue
            if kind == "rows":
                src, dst, lands = srcs[a].at[2 * mine + c], dsts[a].at[2 * mine + c], dsts[a].at[2 * other + c]
            else:
                src, dst, lands = srcs[a].at[other], dsts[a].at[mine], dsts[a].at[other]
            sem = dict(send_sem=send_sems.at[per * a + k], recv_sem=recv_sems.at[per * a + k], device_id=(px, py, c),
                       device_id_type=MESH)
            sends.append(pltpu.make_async_remote_copy(src_ref=src, dst_ref=dst, **sem))
            arrivals.append(pltpu.make_async_remote_copy(src_ref=src, dst_ref=lands, **sem))
    return sends, arrivals


def _chips_start(name, kind, srcs, dsts=None, after=None):
    n = len(srcs)
    bufs = list(srcs) + (list(dsts) if dsts is not None else [])
    nb = len(bufs)
    extra = [] if after is None else [after]

    def body(*refs):
        ins = refs[:nb]
        send_sems, recv_sems = refs[nb + len(extra)], refs[nb + len(extra) + 1]
        token = refs[-1]
        sends, _ = _chip_copies(kind, ins[:n], ins[n:] if dsts is not None else ins[:n], send_sems, recv_sems)
        for cp in sends:
            cp.start()
        token[...] = jnp.zeros_like(token)

    out = pl.pallas_call(
        body, name=name,
        out_shape=(pltpu.SemaphoreType.DMA((COPIES_PER_ARRAY[kind] * n,)), pltpu.SemaphoreType.DMA((COPIES_PER_ARRAY[kind] * n,)),
                   *[pltpu.HBM(b.shape, b.dtype) for b in bufs], jax.ShapeDtypeStruct((8, HEAD), F32)),
        in_specs=(HBM,) * nb + (ANY,) * len(extra),
        out_specs=(SEM, SEM) + (HBM,) * nb + (pl.BlockSpec(memory_space=pltpu.VMEM),),
        input_output_aliases={i: 2 + i for i in range(nb)},
        compiler_params=pltpu.CompilerParams(has_side_effects=EFFECT),
    )(*[pltpu.with_memory_space_constraint(b, pltpu.HBM) for b in bufs], *extra)
    return out[0], out[1], list(out[2:2 + nb]), out[-1]


def _chips_wait(name, kind, n, send_sems, recv_sems, bufs, after):
    nb = len(bufs)

    def body(*refs):
        ins = refs[:nb]
        s_sems, r_sems = refs[nb], refs[nb + 1]
        sends, arrivals = _chip_copies(kind, ins[:n], ins[n:] if nb > n else ins[:n], s_sems, r_sems)
        for cp in sends:
            cp.wait_send()
        for cp in arrivals:
            cp.wait_recv()

    return list(pl.pallas_call(
        body, name=name, out_shape=tuple(pltpu.HBM(b.shape, b.dtype) for b in bufs),
        in_specs=(HBM,) * nb + (SEM, SEM, pl.BlockSpec(memory_space=pl.ANY)), out_specs=(HBM,) * nb,
        input_output_aliases={i: i for i in range(nb)},
        compiler_params=pltpu.CompilerParams(has_side_effects=EFFECT),
    )(*bufs, send_sems, recv_sems, after))


def _fill_from_sibling(name, arrays):
    n = len(arrays)

    def body(*refs):
        ins, outs = refs[:n], refs[n:2 * n]
        send_sems, recv_sems = refs[2 * n:]
        x, y, c = _place()
        sends, arrivals = [], []
        for a in range(n):
            for k, (px, py) in enumerate([(1 - x, y), (x, 1 - y), (1 - x, 1 - y)]):
                sem = dict(send_sem=send_sems.at[3 * a + k], recv_sem=recv_sems.at[3 * a + k], device_id=(x, y, 1 - c),
                           device_id_type=MESH)
                row = 2 * (2 * px + py)
                sends.append(pltpu.make_async_remote_copy(src_ref=ins[a].at[row + c], dst_ref=outs[a].at[row + c], **sem))
                arrivals.append(pltpu.make_async_remote_copy(src_ref=ins[a].at[row + c], dst_ref=outs[a].at[row + 1 - c], **sem))
        for cp in sends:
            cp.start()
        for cp in sends:
            cp.wait_send()
        for cp in arrivals:
            cp.wait_recv()

    return pl.pallas_call(
        body, name=name, in_specs=[ANY] * n, out_specs=[ANY] * n,
        out_shape=[jax.ShapeDtypeStruct(a.shape, a.dtype) for a in arrays],
        input_output_aliases={a: a for a in range(n)},
        scratch_shapes=[pltpu.SemaphoreType.DMA((3 * n,)), pltpu.SemaphoreType.DMA((3 * n,))],
    )(*arrays)


def kernel(x, c, ada_w, ada_b, mix_norm, ffn_norm, par_w_in, par_w_out, hg_lb_logits, hg_out_norm, sg_w_in, sg_v_gain, sg_v_bias, sg_w_pos, sg_b_pos, sg_w_out, ffn_up, ffn_conv_w, ffn_conv_b, ffn_down, final_norm, loss_target, m_ada_w, m_ada_b, m_mix_norm, m_ffn_norm, m_par_w_in, m_par_w_out, m_hg_lb_logits, m_hg_out_norm, m_sg_w_in, m_sg_v_gain, m_sg_v_bias, m_sg_w_pos, m_sg_b_pos, m_sg_w_out, m_ffn_up, m_ffn_conv_w, m_ffn_conv_b, m_ffn_down, m_final_norm, v_ada_w, v_ada_b, v_mix_norm, v_ffn_norm, v_par_w_in, v_par_w_out, v_hg_lb_logits, v_hg_out_norm, v_sg_w_in, v_sg_v_gain, v_sg_v_bias, v_sg_w_pos, v_sg_b_pos, v_sg_w_out, v_ffn_up, v_ffn_conv_w, v_ffn_conv_b, v_ffn_down, v_final_norm):
    names = ["ada_w", "ada_b", "mix_norm", "ffn_norm", "par_w_in", "par_w_out", "hg_lb_logits", "hg_out_norm", "sg_w_in",
             "sg_v_gain", "sg_v_bias", "sg_w_pos", "sg_b_pos", "sg_w_out", "ffn_up", "ffn_conv_w", "ffn_conv_b",
             "ffn_down", "final_norm"]
    W = dict(zip(names, [ada_w, ada_b, mix_norm, ffn_norm, par_w_in, par_w_out, hg_lb_logits, hg_out_norm, sg_w_in,
                         sg_v_gain, sg_v_bias, sg_w_pos, sg_b_pos, sg_w_out, ffn_up, ffn_conv_w, ffn_conv_b, ffn_down,
                         final_norm]))
    M = dict(zip(names, [m_ada_w, m_ada_b, m_mix_norm, m_ffn_norm, m_par_w_in, m_par_w_out, m_hg_lb_logits, m_hg_out_norm,
                         m_sg_w_in, m_sg_v_gain, m_sg_v_bias, m_sg_w_pos, m_sg_b_pos, m_sg_w_out, m_ffn_up, m_ffn_conv_w,
                         m_ffn_conv_b, m_ffn_down, m_final_norm]))
    V = dict(zip(names, [v_ada_w, v_ada_b, v_mix_norm, v_ffn_norm, v_par_w_in, v_par_w_out, v_hg_lb_logits, v_hg_out_norm,
                         v_sg_w_in, v_sg_v_gain, v_sg_v_bias, v_sg_w_pos, v_sg_b_pos, v_sg_w_out, v_ffn_up, v_ffn_conv_w,
                         v_ffn_conv_b, v_ffn_down, v_final_norm]))

    x = x[0]
    target = loss_target[0]
    T, D = x.shape
    ix, iy, ic = _place()
    chip = 2 * ix + iy
    dev = 2 * chip + ic
    H = hg_out_norm.shape[1]
    SBW = H * HEAD
    NA = ada_w.shape[2]
    F2s = ffn_up.shape[2]
    F2 = N_CHIP * F2s
    SGW = sg_w_out.shape[1] * N_CHIP
    G = sg_w_pos.shape[1]

    shards = [par_w_in[0], par_w_out[0], sg_w_in[0], sg_w_out[0], ffn_up[0], ffn_up[1], ffn_down[0], ffn_down[1]]
    kinds = ["col", "row", "col", "row", "col", "col", "row", "row"]
    halves = [w.reshape(2, w.shape[0] // 2, w.shape[1]) for w in shards]
    groups = {"a": [0], "b": [1, 4, 6], "c": [2, 3, 5, 7]}
    rows8 = {0: _cast_into_rows("cast_w", halves[0], chip)}
    started = {"a": _chips_start("gather_start_a", "rows", [rows8[0]])}
    for i in range(1, len(shards)):
        rows8[i] = _cast_into_rows("cast_w", halves[i], chip, after=started["a"][2][0])

    def as_weights(g, bufs):
        out = {}
        for i, g8 in zip(groups[g], bufs):
            K, N = shards[i].shape
            out[i] = g8.reshape(N_CHIP, K, N) if kinds[i] == "col" else g8.reshape(N_CHIP * K, N)
        return out

    def weights_landed(g, after):
        send_sems, recv_sems, bufs, _ = started[g]
        bufs = _chips_wait("gather_wait_" + g, "rows", len(bufs), send_sems, recv_sems, bufs, after)
        return _chips_start("gather_fill_start_" + g, "fill", bufs)

    def weights_of(g, filling, after):
        send_sems, recv_sems, bufs, _ = filling
        return as_weights(g, _chips_wait("gather_fill_wait_" + g, "fill", len(bufs), send_sems, recv_sems, bufs, after))

    n_cw = ffn_conv_w.size
    n_sv = sg_v_gain.size
    c_all, small_all = _all_gather("gather_small", [c, _pack_rows([ffn_conv_w, sg_v_gain, sg_v_bias])],
                                   after=started["a"][3])
    c_all = c_all.reshape(N_DEV, D)
    small_all = small_all.reshape(N_CHIP, 2, -1)[:, 0]
    conv_w_full = small_all[:, :n_cw].reshape(N_CHIP, 2, CONV_WIDTH, F2s).transpose(1, 2, 0, 3).reshape(2, CONV_WIDTH, F2)
    sg_gain_full = small_all[:, n_cw:n_cw + n_sv].reshape(1, SGW)
    sg_bias_full = small_all[:, n_cw + n_sv:n_cw + 2 * n_sv].reshape(1, SGW)

    c_pad = jnp.pad(c_all, ((0, 16 - N_DEV), (0, 0)))
    ada_b_sh = lax.dynamic_slice(ada_b, (0, chip * NA), (2, NA)).reshape(2, 1, NA)
    mod_sh = _ada_fwd(c_pad, ada_w, ada_b_sh)
    mod_all, = _all_gather("gather_mod", [mod_sh[:, :N_DEV]])
    mod_all = mod_all.reshape(N_CHIP, 2, 2, N_DEV, NA)[:, 0]
    mod = lax.dynamic_index_in_dim(mod_all, dev, axis=2, keepdims=False)
    mod = mod.transpose(1, 0, 2).reshape(2, 6, D)
    mods = [[mod[l, k].reshape(1, D) for k in range(6)] for l in range(2)]
    for g in ("b", "c"):
        started[g] = _chips_start("gather_start_" + g, "rows", [rows8[i] for i in groups[g]], after=mod)
    start_token = sum(st[3][0, 0] for st in started.values())

    vec = lambda a: a.reshape(1, -1)
    l0 = vec(hg_lb_logits[0])
    l1 = vec(hg_lb_logits[1])
    hg_gain = vec(hg_out_norm[0])
    wpos = sg_w_pos[0]
    bpos = sg_b_pos[0].reshape(G, SG_CHUNK, 1)
    conv_b = [vec(ffn_conv_b[l]) for l in range(2)]

    sh1, sc1, g1, sh2, sc2, g2 = mods[0]
    send_a, recv_a, bufs_a, _ = started["a"]
    bufs_a = _chips_wait("gather_wait_a", "rows", 1, send_a, recv_a, bufs_a, mod)
    w_in = as_weights("a", _fill_from_sibling("gather_fill_a", bufs_a))[0]
    h0 = _normmod_fwd("norm_mix0", x, vec(mix_norm[0]) + start_token, sc1, sh1)
    proj = _mm_nn("mm_par_in", h0, w_in)
    o_sb, sb_tot = _sb_fwd(proj, H)
    filling_b = weights_landed("b", o_sb)
    o_hg, hg_states = _hg_fwd(proj, l0 + filling_b[3][0:1, 0:1], l1, hg_gain, H, 3 * H)
    o_cat = jnp.concatenate([o_sb, o_hg], axis=1)
    wb = weights_of("b", filling_b, o_cat)
    w_out, wup, wdn = wb[1], [wb[4], None], [wb[6], None]
    y0 = _mm_nn("mm_par_out", o_cat, w_out)
    x1, h0f = _res_normmod_fwd("res_norm_ffn0", x, y0, g1, vec(ffn_norm[0]), sc2, sh2)
    a0 = _mm_nn("mm_up0", h0f, wup[0])
    u0 = _conv_fwd("conv_fwd0", a0, conv_w_full[0], conv_b[0])
    filling_c = weights_landed("c", u0)
    f0 = _mm_nn("mm_down0", u0, wdn[0])
    sh1b, sc1b, g1b, sh2b, sc2b, g2b = mods[1]
    x2, h1 = _res_normmod_fwd("res_norm_mix1", x1, f0, g2, vec(mix_norm[1]) + filling_c[3][0:1, 0:1], sc1b, sh1b)
    wc = weights_of("c", filling_c, h1)
    wsg_in, wsg_out, wup[1], wdn[1] = wc[2], wc[3], wc[5], wc[7]
    zpre = _mm_nn("mm_sg_in", h1, wsg_in)
    s1 = _sg_fwd(zpre, sg_gain_full, sg_bias_full, wpos, bpos)
    y1 = _mm_nn("mm_sg_out", s1, wsg_out)
    x3, h1f = _res_normmod_fwd("res_norm_ffn1", x2, y1, g1b, vec(ffn_norm[1]), sc2b, sh2b)
    a1 = _mm_nn("mm_up1", h1f, wup[1])
    u1 = _conv_fwd("conv_fwd1", a1, conv_w_full[1], conv_b[1])
    f1 = _mm_nn("mm_down1", u1, wdn[1])
    loss_sum, dx, df1, dg2b, d_final = _final_fwd_bwd(x3, f1, g2b, vec(final_norm), target)
    loss = lax.psum(loss_sum[0, 0], ("x", "y", "c"))

    def reduce_start(tag, idx, grads):
        eights = [g.reshape((N_DEV, -1, g.shape[-1])) for g in grads]
        landing = [lax.empty((N_CHIP,) + e.shape[1:], e.dtype) for e in eights]
        send_sems, recv_sems, bufs, token = _chips_start("grads_sibling_start_" + tag, "halves", eights, landing)
        return (tag, idx, send_sems, recv_sems, bufs), token[0:1, 0:1]

    def reduce_cross(state, after):
        tag, idx, send_sems, recv_sems, bufs = state
        n = len(idx)
        bufs = _chips_wait("grads_sibling_wait_" + tag, "halves", n, send_sems, recv_sems, bufs, after)
        pair = [_add_pairs("add_pair", e, r, ic) for e, r in zip(bufs[:n], bufs[n:])]
        landing = [lax.empty(p.shape, p.dtype) for p in pair]
        send_sems, recv_sems, bufs, token = _chips_start("grads_start_" + tag, "parts", pair, landing)
        return (tag, idx, send_sems, recv_sems, bufs), token[0:1, 0:1]

    def reduce_finish(state, after):
        tag, idx, send_sems, recv_sems, bufs = state
        n = len(idx)
        bufs = _chips_wait("grads_wait_" + tag, "parts", n, send_sems, recv_sems, bufs, after)
        halves = [_sum_into_pair("sum_chips", p, x_, ic, chip) for p, x_ in zip(bufs[:n], bufs[n:])]
        both = _share_halves("grads_share_" + tag, halves)
        return {i: b.reshape(shards[i].shape) for i, b in zip(idx, both)}

    def ffn_bwd(l, dfl, u, a, hf):
        g_dn = _mm_tn("mm_g_down", u, dfl)
        du = _mm_nt("mm_d_u", dfl, wdn[l])
        da_g, da_v, dcw, dcb = _conv_bwd("conv_bwd", a, du, conv_w_full[l], conv_b[l])
        da = jnp.concatenate([da_g, da_v], axis=1)
        g_up = _mm_tn("mm_g_up", hf, da, chunks=N_CHIP)
        dh = _mm_nt("mm_d_hf", da, wup[l])
        return g_dn, g_up, dcw, dcb, dh

    g_dn1, g_up1, dcw1, dcb1, dh1f = ffn_bwd(1, df1, u1, a1, h1f)
    red1, tok = reduce_start("1", [5, 7], [g_up1, g_dn1])
    dx, dgn_f1, dsc2b, dsh2b, dy1, dg1b = _block_bwd("bwd_ffn1", dx, dh1f, x3, vec(ffn_norm[1]) + tok, sc2b, sh2b, y1, g1b)
    g_sg_out = _mm_tn("mm_g_sg_out", s1, dy1)
    ds1 = _mm_nt("mm_d_s", dy1, wsg_out)
    dzpre, dsg_gain, dsg_bias, dwpos, dbpos = _sg_bwd(zpre, ds1, sg_gain_full, sg_bias_full, wpos, bpos)
    red1, tok_x = reduce_cross(red1, dzpre)
    g_sg_in = _mm_tn("mm_g_sg_in", h1, dzpre, chunks=N_CHIP)
    dh1 = _mm_nt("mm_d_h1", dzpre, wsg_in)
    red2, tok = reduce_start("2", [2, 3], [g_sg_in, g_sg_out])
    dx, dgn_m1, dsc1b, dsh1b, df0, dg2 = _block_bwd("bwd_mix1", dx, dh1, x2, vec(mix_norm[1]) + tok + tok_x, sc1b, sh1b, f0, g2)
    g_dn0, g_up0, dcw0, dcb0, dh0f = ffn_bwd(0, df0, u0, a0, h0f)
    red2, tok_x = reduce_cross(red2, dh0f)
    red3, tok = reduce_start("3", [4, 6], [g_up0, g_dn0])
    dx, dgn_f0, dsc2, dsh2, dy0, dg1 = _block_bwd("bwd_ffn0", dx, dh0f, x1, vec(ffn_norm[0]) + tok + tok_x, sc2, sh2, y0, g1)
    g_out = _mm_tn("mm_g_par_out", o_cat, dy0)
    do = _mm_nt("mm_d_o", dy0, w_out)
    red3, tok_x = reduce_cross(red3, do)
    dhq, dhf, dhi, dhg, dl0, dl1, dhg_gain = _hg_bwd(proj, hg_states, do, l0 + tok_x, l1, hg_gain, H, 3 * H, H)
    dq, dk, dv = _sb_bwd(proj, do, sb_tot, H)
    dproj = jnp.concatenate([dq, dk, dv, dhq, dhf, dhi, dhg], axis=1).astype(BF16)
    g_in = _mm_tn("mm_g_par_in", h0, dproj, chunks=N_CHIP)
    red4, tok = reduce_start("4", [0, 1], [g_in, g_out])
    dh0 = _mm_nt("mm_d_h0", dproj, w_in)
    grad_x, dgn_m0, dsc1, dsh1 = _block_bwd("bwd_mix0", dx, dh0, x, vec(mix_norm[0]) + tok, sc1, sh1)
    red4, tok_x = reduce_cross(red4, grad_x)

    G_, delta, new_m, new_v = {}, {}, {}, {}

    def adam_on(nme):
        shp = W[nme].shape
        r2 = lambda a: a.reshape(-1, shp[-1])
        d_, m_, v_ = _adam("adam_" + nme, r2(W[nme]), r2(G_[nme]), r2(M[nme]), r2(V[nme]))
        delta[nme], new_m[nme], new_v[nme] = d_.reshape(shp), m_.reshape(shp), v_.reshape(shp)

    g_shards = {}
    for state in (red1, red2, red3):
        g_shards.update(reduce_finish(state, red4[4][0]))
    G_["sg_w_in"] = g_shards[2][None]
    G_["sg_w_out"] = g_shards[3][None]
    G_["ffn_up"] = jnp.stack([g_shards[4], g_shards[5]])
    G_["ffn_down"] = jnp.stack([g_shards[6], g_shards[7]])
    for nme in ["sg_w_in", "sg_w_out", "ffn_up", "ffn_down"]:
        adam_on(nme)

    dmod = jnp.concatenate([dsh1, dsc1, dg1, dsh2, dsc2, dg2, dsh1b, dsc1b, dg1b, dsh2b, dsc2b, dg2b], axis=1)
    parts = [dmod, dgn_m0, dgn_m1, dgn_f0, dgn_f1, dl0, dl1, dhg_gain, dsg_gain, dsg_bias, dwpos, dbpos,
             dcw0, dcw1, dcb0, dcb1, d_final]
    sizes = [p.size for p in parts]
    packed = _pack_rows(parts)
    packed_all, = _all_gather("gather_small_grads", [packed], after=new_v["ffn_down"])
    summed = _sum_leading("sum_small_grads", packed_all).reshape(-1)
    offs = [0]
    for s in sizes:
        offs.append(offs[-1] + s)
    red = [summed[offs[i]:offs[i + 1]] for i in range(len(parts))]
    (r_dmod, r_gm0, r_gm1, r_gf0, r_gf1, r_l0, r_l1, r_hgain, r_sgain, r_sbias, r_wpos, r_bpos,
     r_cw0, r_cw1, r_cb0, r_cb1, r_final) = red
    n_mod = sizes[0]
    dmod_all = packed_all.reshape(N_DEV, -1)[:, :n_mod].reshape(N_DEV, 2, 6 * D)

    G_["ada_b"] = r_dmod.reshape(2, 6 * D)
    G_["mix_norm"] = jnp.stack([r_gm0, r_gm1])
    G_["ffn_norm"] = jnp.stack([r_gf0, r_gf1])
    G_["hg_lb_logits"] = jnp.stack([r_l0, r_l1])
    G_["hg_out_norm"] = r_hgain.reshape(hg_out_norm.shape)
    G_["sg_v_gain"] = lax.dynamic_slice(r_sgain, (chip * n_sv,), (n_sv,)).reshape(sg_v_gain.shape)
    G_["sg_v_bias"] = lax.dynamic_slice(r_sbias, (chip * n_sv,), (n_sv,)).reshape(sg_v_bias.shape)
    G_["sg_w_pos"] = r_wpos.reshape(sg_w_pos.shape)
    G_["sg_b_pos"] = r_bpos.reshape(sg_b_pos.shape)
    cw_full = jnp.stack([r_cw0.reshape(CONV_WIDTH, F2), r_cw1.reshape(CONV_WIDTH, F2)])
    G_["ffn_conv_w"] = lax.dynamic_slice(cw_full, (0, 0, chip * F2s), (2, CONV_WIDTH, F2s))
    G_["ffn_conv_b"] = jnp.stack([r_cb0, r_cb1])
    G_["final_norm"] = r_final

    c_t = jnp.pad(c_all, ((0, HEAD - N_DEV), (0, 0))).T
    dmod_sh = lax.dynamic_slice(dmod_all.transpose(1, 0, 2), (0, 0, chip * NA), (2, N_DEV, NA))
    dmod_sh = jnp.pad(dmod_sh, ((0, 0), (0, HEAD - N_DEV), (0, 0)))
    G_["ada_w"], delta["ada_w"], new_m["ada_w"], new_v["ada_w"] = _ada_grad_adam(c_t, dmod_sh, ada_w, m_ada_w, v_ada_w)

    g_shards.update(reduce_finish(red4, G_["ada_w"]))
    G_["par_w_in"] = g_shards[0][None]
    G_["par_w_out"] = g_shards[1][None]
    for nme in ["par_w_in", "par_w_out"]:
        adam_on(nme)
    small = [n_ for n_ in names if n_ not in delta]
    pk = lambda dct: _pack_rows([dct[n_] for n_ in small])
    d_, m_, v_ = _adam("adam_small", pk(W), pk(G_), pk(M), pk(V))
    off = 0
    for n_ in small:
        sz = W[n_].size
        for dst, src in ((delta, d_), (new_m, m_), (new_v, v_)):
            dst[n_] = src.reshape(-1)[off:off + sz].reshape(W[n_].shape)
        off += sz

    return (loss, grad_x[None], *[G_[n_] for n_ in names], *[delta[n_] for n_ in names],
            *[new_m[n_] for n_ in names], *[new_v[n_] for n_ in names])
```

```python
import functools
import math

import jax
import jax.numpy as jnp
from jax import lax
from jax.experimental import pallas as pl
from jax.experimental.pallas import tpu as pltpu

F32 = jnp.float32
BF16 = jnp.bfloat16
MESH = pl.DeviceIdType.MESH
ANY = pl.BlockSpec(memory_space=pl.ANY)

NORM_EPS = 1e-6
ADAM_LR = 0.001
ADAM_B1 = 0.9
ADAM_B2 = 0.999
ADAM_EPS = 1e-08
ADAM_WD = 0.01
ADAM_STEP = 10
CONV_WIDTH = 3
HEAD = 128
HG_CHUNK = 64
SG_CHUNK = 128
N_DEV = 8
N_CHIP = 4
V7X_VMEM_LIMIT = 56 * 1024 * 1024


def _cp(*sem):
    return pltpu.CompilerParams(dimension_semantics=sem if sem else None, vmem_limit_bytes=V7X_VMEM_LIMIT)


def _pick(n, prefs):
    for p in prefs:
        if p <= n and n % p == 0:
            return p
    return n


def _iota(shape, axis):
    return lax.broadcasted_iota(jnp.int32, shape, axis)


def _rows_within(R, row_bytes, budget):
    if R * row_bytes <= budget:
        return R
    for t in (1024, 512, 256, 128, 64, 32, 16):
        if R % t == 0 and t * row_bytes <= budget:
            return t
    return _pick(R, (16, 8))


def _pack_rows(arrays):
    flat = jnp.concatenate([a.reshape(-1) for a in arrays])
    pad = (-flat.size) % (8 * HEAD)
    return jnp.pad(flat, (0, pad)).reshape(-1, HEAD)


def _dg(a, b, ca, cb):
    return lax.dot_general(a.astype(BF16), b.astype(BF16), (((ca,), (cb,)), ((), ())), preferred_element_type=F32)


@jax.custom_vjp
def mm_nn(a, b):
    return _dg(a, b, 1, 0)


mm_nn.defvjp(lambda a, b: (_dg(a, b, 1, 0), (a, b)),
             lambda r, g: (_dg(g, r[1], 1, 1), _dg(r[0], g, 0, 0)))


@jax.custom_vjp
def mm_nt(a, b):
    return _dg(a, b, 1, 1)


mm_nt.defvjp(lambda a, b: (_dg(a, b, 1, 1), (a, b)),
             lambda r, g: (_dg(g, r[1], 1, 0), _dg(g, r[0], 0, 0)))


@jax.custom_vjp
def mm_tn(a, b):
    return _dg(a, b, 0, 0)


mm_tn.defvjp(lambda a, b: (_dg(a, b, 0, 0), (a, b)),
             lambda r, g: (_dg(r[1], g, 1, 1), _dg(r[0], g, 1, 0)))


def _split(x):
    hi = x.astype(BF16)
    lo = (x - hi.astype(F32)).astype(BF16)
    return hi, lo


def _sum_right(x, m01):
    hi, lo = _split(x)
    return _dg(hi, m01, 1, 0) + _dg(lo, m01, 1, 0)


def _sum_left_impl(m01, x, ca):
    hi, lo = _split(x)
    return _dg(m01, hi, ca, 0) + _dg(m01, lo, ca, 0)


@jax.custom_vjp
def _sum_left(m01, x):
    return _sum_left_impl(m01, x, 1)


_sum_left.defvjp(lambda m, x: (_sum_left_impl(m, x, 1), m),
                 lambda m, g: (None, _sum_left_impl(m, g, 0)))


def _sigmoid(x):
    return 1.0 / (1.0 + jnp.exp(-x))


def _softplus(z):
    return jnp.maximum(z, 0.0) + jnp.log(1.0 + jnp.exp(-jnp.abs(z)))


_INV_SQRT2 = 1.0 / math.sqrt(2.0)
_INV_SQRT2PI = 1.0 / math.sqrt(2.0 * math.pi)


@jax.custom_vjp
def _gelu(x):
    return 0.5 * x * (1.0 + lax.erf(x * _INV_SQRT2))


_gelu.defvjp(lambda x: (0.5 * x * (1.0 + lax.erf(x * _INV_SQRT2)), x),
             lambda x, g: (g * (0.5 * (1.0 + lax.erf(x * _INV_SQRT2)) + x * jnp.exp(-0.5 * x * x) * _INV_SQRT2PI),))


def _rms(x, gain):
    r = lax.rsqrt(jnp.mean(x * x, axis=-1, keepdims=True) + NORM_EPS)
    return x * r * gain


def _normmod(x, gain, sc, sh):
    return _rms(x, gain) * (1.0 + sc) + sh


def _mm_call(name, a, b, out_shape, out_dtype, dims, grid, a_spec, b_spec, o_spec, acc_shape):
    nk = grid[2]

    def body(a_ref, b_ref, o_ref, *scratch):
        part = lax.dot_general(a_ref[...].astype(BF16), b_ref[...].astype(BF16), dims, preferred_element_type=F32)
        if nk == 1:
            o_ref[...] = part.astype(o_ref.dtype)
            return
        acc_ref, = scratch
        k = pl.program_id(2)

        @pl.when(k == 0)
        def _():
            acc_ref[...] = part

        @pl.when(k > 0)
        def _():
            acc_ref[...] += part

        @pl.when(k == nk - 1)
        def _():
            o_ref[...] = acc_ref[...].astype(o_ref.dtype)

    return pl.pallas_call(
        body, name=name, grid=grid, in_specs=[a_spec, b_spec], out_specs=o_spec,
        out_shape=jax.ShapeDtypeStruct(out_shape, out_dtype),
        scratch_shapes=[] if nk == 1 else [pltpu.VMEM(acc_shape, F32)],
        compiler_params=_cp("parallel", "parallel", "arbitrary"),
    )(a, b)


def _mm_nn(name, a, b, out_dtype=F32):
    M, K = a.shape
    chunked = b.ndim == 3
    Nc = b.shape[-1]
    N = Nc * (b.shape[0] if chunked else 1)
    tm = _pick(M, (1024, 512, 256, 128, 64, 32, 16, 8))
    tn = _pick(Nc, (1408, 1024, 896, 512, 256, 128))
    tk = _pick(K, (2048, 1408, 1024, 512, 256, 128))
    npc = Nc // tn
    if chunked:
        b_spec = pl.BlockSpec((None, tk, tn), lambda i, j, k: (j // npc, k, j % npc))
    else:
        b_spec = pl.BlockSpec((tk, tn), lambda i, j, k: (k, j))
    return _mm_call(name, a, b, (M, N), out_dtype, (((1,), (0,)), ((), ())), (M // tm, N // tn, K // tk),
                    pl.BlockSpec((tm, tk), lambda i, j, k: (i, k)), b_spec,
                    pl.BlockSpec((tm, tn), lambda i, j, k: (i, j)), (tm, tn))


def _mm_nt(name, a, b, out_dtype=F32):
    M, N = a.shape
    chunked = b.ndim == 3
    Nc = b.shape[-1]
    K = b.shape[-2]
    tm = _pick(M, (1024, 512, 256, 128, 64, 32, 16, 8))
    tn = _pick(K, (1408, 1024, 512, 256, 128))
    tk = _pick(Nc, (2048, 1792, 1408, 1024, 896, 512, 256, 128))
    npc = Nc // tk
    if chunked:
        b_spec = pl.BlockSpec((None, tn, tk), lambda i, j, k: (k // npc, j, k % npc))
    else:
        b_spec = pl.BlockSpec((tn, tk), lambda i, j, k: (j, k))
    return _mm_call(name, a, b, (M, K), out_dtype, (((1,), (1,)), ((), ())), (M // tm, K // tn, N // tk),
                    pl.BlockSpec((tm, tk), lambda i, j, k: (i, k)), b_spec,
                    pl.BlockSpec((tm, tn), lambda i, j, k: (i, j)), (tm, tn))


def _mm_tn(name, a, b, chunks=1, out_dtype=BF16):
    T, K = a.shape
    N = b.shape[1]
    Nc = N // chunks
    tm = _pick(K, (1408, 1024, 512, 256, 128))
    tn = _pick(Nc, (1408, 1024, 896, 512, 256, 128))
    tk = _pick(T, (1024, 512, 256, 128))
    npc = Nc // tn
    if chunks > 1:
        shape = (chunks, K, Nc)
        o_spec = pl.BlockSpec((None, tm, tn), lambda i, j, k: (j // npc, i, j % npc))
    else:
        shape = (K, N)
        o_spec = pl.BlockSpec((tm, tn), lambda i, j, k: (i, j))
    return _mm_call(name, a, b, shape, out_dtype, (((0,), (0,)), ((), ())), (K // tm, N // tn, T // tk),
                    pl.BlockSpec((tk, tm), lambda i, j, k: (k, i)),
                    pl.BlockSpec((tk, tn), lambda i, j, k: (k, j)), o_spec, (tm, tn))


def _row_tile(T):
    return _pick(T, (256, 128, 64, 32, 16, 8))


def _vec_spec(D):
    return pl.BlockSpec((1, D), lambda i: (0, 0))


def _normmod_fwd(name, x, gain, sc, sh):
    T, D = x.shape
    bt = _row_tile(T)

    def body(x_ref, g_ref, sc_ref, sh_ref, h_ref):
        h_ref[...] = _normmod(x_ref[...], g_ref[...], sc_ref[...], sh_ref[...]).astype(h_ref.dtype)

    rows = pl.BlockSpec((bt, D), lambda i: (i, 0))
    return pl.pallas_call(body, name=name, grid=(T // bt,), in_specs=[rows] + [_vec_spec(D)] * 3, out_specs=rows,
                          out_shape=jax.ShapeDtypeStruct((T, D), BF16), compiler_params=_cp("parallel"))(x, gain, sc, sh)


def _res_normmod_fwd(name, x, y, g, gain, sc, sh):
    T, D = x.shape
    bt = _row_tile(T)

    def body(x_ref, y_ref, gate_ref, g_ref, sc_ref, sh_ref, x1_ref, h_ref):
        x1 = x_ref[...] + gate_ref[...] * y_ref[...]
        x1_ref[...] = x1
        h_ref[...] = _normmod(x1, g_ref[...], sc_ref[...], sh_ref[...]).astype(h_ref.dtype)

    rows = pl.BlockSpec((bt, D), lambda i: (i, 0))
    return pl.pallas_call(body, name=name, grid=(T // bt,), in_specs=[rows, rows] + [_vec_spec(D)] * 4,
                          out_specs=[rows, rows],
                          out_shape=[jax.ShapeDtypeStruct((T, D), F32), jax.ShapeDtypeStruct((T, D), BF16)],
                          compiler_params=_cp("parallel"))(x, y, g, gain, sc, sh)


def _final_fwd_bwd(x, y, g, gain, target):
    T, D = x.shape
    bt = _row_tile(T)

    def body(x_ref, y_ref, gate_ref, g_ref, t_ref, loss_ref, dx_ref, dy_ref, dgate_ref, dgain_ref):
        i = pl.program_id(0)
        yv = y_ref[...]
        gate = gate_ref[...]
        x4 = x_ref[...] + gate * yv
        out, vjp = jax.vjp(_rms, x4, g_ref[...])
        err = out - t_ref[...]
        dx4, dgain = vjp(err * (1.0 / D))
        part = 0.5 * jnp.sum(jnp.mean(err * err, axis=-1, keepdims=True), axis=0, keepdims=True)

        @pl.when(i == 0)
        def _():
            loss_ref[...] = jnp.zeros_like(loss_ref)
            dgate_ref[...] = jnp.zeros_like(dgate_ref)
            dgain_ref[...] = jnp.zeros_like(dgain_ref)

        loss_ref[...] += jnp.broadcast_to(part, loss_ref.shape)
        dx_ref[...] = dx4
        dy_ref[...] = (gate * dx4).astype(dy_ref.dtype)
        dgate_ref[...] += jnp.sum(dx4 * yv, axis=0, keepdims=True)
        dgain_ref[...] += dgain

    rows = pl.BlockSpec((bt, D), lambda i: (i, 0))
    vec = _vec_spec(D)
    return pl.pallas_call(
        body, name="final_loss", grid=(T // bt,), in_specs=[rows, rows, vec, vec, rows],
        out_specs=[pl.BlockSpec((1, HEAD), lambda i: (0, 0)), rows, rows, vec, vec],
        out_shape=[jax.ShapeDtypeStruct((1, HEAD), F32), jax.ShapeDtypeStruct((T, D), F32),
                   jax.ShapeDtypeStruct((T, D), BF16), jax.ShapeDtypeStruct((1, D), F32),
                   jax.ShapeDtypeStruct((1, D), F32)],
        compiler_params=_cp("arbitrary"))(x, y, g, gain, target)


def _block_bwd(name, dx_out, dh, x_in, gain, sc, sh, y_prev=None, g_prev=None):
    T, D = x_in.shape
    bt = _row_tile(T)
    has_prev = y_prev is not None

    def body(*refs):
        if has_prev:
            dxo_ref, dh_ref, x_ref, g_ref, sc_ref, sh_ref, y_ref, gp_ref, dx_ref, dgain_ref, dsc_ref, dsh_ref, dy_ref, dgp_ref = refs
        else:
            dxo_ref, dh_ref, x_ref, g_ref, sc_ref, sh_ref, dx_ref, dgain_ref, dsc_ref, dsh_ref = refs
        i = pl.program_id(0)
        _, vjp = jax.vjp(_normmod, x_ref[...], g_ref[...], sc_ref[...], sh_ref[...])
        dxn, dgain, dsc, dsh = vjp(dh_ref[...])
        dx = dxo_ref[...] + dxn
        dx_ref[...] = dx

        @pl.when(i == 0)
        def _():
            dgain_ref[...] = jnp.zeros_like(dgain_ref)
            dsc_ref[...] = jnp.zeros_like(dsc_ref)
            dsh_ref[...] = jnp.zeros_like(dsh_ref)
            if has_prev:
                dgp_ref[...] = jnp.zeros_like(dgp_ref)

        dgain_ref[...] += dgain
        dsc_ref[...] += dsc
        dsh_ref[...] += dsh
        if has_prev:
            dy_ref[...] = (gp_ref[...] * dx).astype(dy_ref.dtype)
            dgp_ref[...] += jnp.sum(dx * y_ref[...], axis=0, keepdims=True)

    rows = pl.BlockSpec((bt, D), lambda i: (i, 0))
    vec = _vec_spec(D)
    ins = [dx_out, dh, x_in, gain, sc, sh]
    in_specs = [rows, rows, rows, vec, vec, vec]
    out_specs = [rows, vec, vec, vec]
    out_shape = [jax.ShapeDtypeStruct((T, D), F32)] + [jax.ShapeDtypeStruct((1, D), F32)] * 3
    if has_prev:
        ins += [y_prev, g_prev]
        in_specs += [rows, vec]
        out_specs += [rows, vec]
        out_shape += [jax.ShapeDtypeStruct((T, D), BF16), jax.ShapeDtypeStruct((1, D), F32)]
    return pl.pallas_call(body, name=name, grid=(T // bt,), in_specs=in_specs, out_specs=out_specs,
                          out_shape=out_shape, compiler_params=_cp("arbitrary"))(*ins)


def _sb_tiles(T):
    tq = _pick(T, (512, 256, 128))
    return tq, tq // HEAD


def _sb_fwd(proj, H):
    T = proj.shape[0]
    tq, nsub = _sb_tiles(T)
    scale = HEAD ** -0.5

    def body(q_ref, k_ref, v_ref, o_ref, l_ref, acc_ref):
        i = pl.program_id(1)
        q = q_ref[...].astype(BF16)
        later = (_iota((HEAD, HEAD), 0) > _iota((HEAD, HEAD), 1)).astype(BF16)
        row = _iota((tq, HEAD), 0)
        col = _iota((tq, HEAD), 1)

        def key_step(j, c, diagonal):
            off = pl.multiple_of(j * tq, tq)
            k = k_ref[pl.ds(off, tq), :].astype(BF16)
            v = v_ref[pl.ds(off, tq), :].astype(BF16)
            z = _dg(q, k, 1, 1) * scale
            ws = [None] * nsub
            for s in reversed(range(nsub)):
                zs = z[:, s * HEAD:(s + 1) * HEAD]
                sp = _softplus(zs)
                if diagonal:
                    strict = (s * HEAD + col) < row
                    lk = jnp.where(strict, -sp, 0.0)
                else:
                    lk = -sp
                w = jnp.exp(zs - sp + _sum_right(lk, later) + c)
                if diagonal:
                    w = jnp.where(strict, w, 0.0)
                ws[s] = w.astype(BF16)
                c = c + jnp.sum(lk, axis=1, keepdims=True)
            acc_ref[...] += _dg(jnp.concatenate(ws, axis=1), v, 1, 0)
            return c

        acc_ref[...] = jnp.zeros_like(acc_ref)
        c = key_step(i, jnp.zeros((tq, 1), F32), True)
        c = lax.fori_loop(0, i, lambda n, c: key_step(i - 1 - n, c, False), c)
        o_ref[...] = acc_ref[...].astype(o_ref.dtype)
        l_ref[...] = jnp.broadcast_to(c, (tq, HEAD))

    blk = pl.BlockSpec((tq, HEAD), lambda h, i: (i, h))
    return pl.pallas_call(
        body, name="sb_fwd", grid=(H, T // tq),
        in_specs=[blk, pl.BlockSpec((T, HEAD), lambda h, i: (0, H + h)), pl.BlockSpec((T, HEAD), lambda h, i: (0, 2 * H + h))],
        out_specs=[blk, blk],
        out_shape=[jax.ShapeDtypeStruct((T, H * HEAD), BF16), jax.ShapeDtypeStruct((T, H * HEAD), F32)],
        scratch_shapes=[pltpu.VMEM((tq, HEAD), F32)],
        compiler_params=_cp("parallel", "arbitrary"))(proj, proj, proj)


def _sb_bwd(proj, do, L, H):
    T = proj.shape[0]
    tq, nsub = _sb_tiles(T)
    scale = HEAD ** -0.5

    def body(q_ref, k_ref, v_ref, do_ref, l_ref, dq_ref, dk_ref, dv_ref):
        i = pl.program_id(1)

        @pl.when(i == 0)
        def _():
            dk_ref[...] = jnp.zeros_like(dk_ref)
            dv_ref[...] = jnp.zeros_like(dv_ref)

        dq_ref[...] = jnp.zeros_like(dq_ref)
        q = q_ref[...].astype(BF16)
        do_ = do_ref[...].astype(BF16)
        total = l_ref[...]
        upto = (_iota((HEAD, HEAD), 0) <= _iota((HEAD, HEAD), 1)).astype(BF16)
        before = (_iota((HEAD, HEAD), 0) < _iota((HEAD, HEAD), 1)).astype(BF16)
        row = _iota((tq, HEAD), 0)
        col = _iota((tq, HEAD), 1)

        def key_step(j, carry, diagonal):
            cp, ce = carry
            off = pl.multiple_of(j * tq, tq)
            k = k_ref[pl.ds(off, tq), :].astype(BF16)
            v = v_ref[pl.ds(off, tq), :].astype(BF16)
            z = _dg(q, k, 1, 1) * scale
            dw = _dg(do_, v, 1, 1)
            ws, dzs = [], []
            for s in range(nsub):
                zs = z[:, s * HEAD:(s + 1) * HEAD]
                sp = _softplus(zs)
                if diagonal:
                    strict = (s * HEAD + col) < row
                    lk = jnp.where(strict, -sp, 0.0)
                else:
                    lk = -sp
                tail = total - (_sum_right(lk, upto) + cp)
                w = jnp.exp(zs - sp + tail)
                if diagonal:
                    w = jnp.where(strict, w, 0.0)
                e = w * dw[:, s * HEAD:(s + 1) * HEAD]
                e_before = _sum_right(e, before) + ce
                sig = jnp.exp(zs - sp)
                dz = (e * (1.0 - sig) - e_before * sig) * scale
                if diagonal:
                    dz = jnp.where(strict, dz, 0.0)
                ws.append(w.astype(BF16))
                dzs.append(dz.astype(BF16))
                cp = cp + jnp.sum(lk, axis=1, keepdims=True)
                ce = ce + jnp.sum(e, axis=1, keepdims=True)
            w_all = jnp.concatenate(ws, axis=1)
            dz_all = jnp.concatenate(dzs, axis=1)
            dv_ref[pl.ds(off, tq), :] += _dg(w_all, do_, 0, 0)
            dk_ref[pl.ds(off, tq), :] += _dg(dz_all, q, 0, 0)
            dq_ref[...] += _dg(dz_all, k, 1, 0)
            return cp, ce

        zero = jnp.zeros((tq, 1), F32)
        carry = lax.fori_loop(0, i, lambda j, cr: key_step(j, cr, False), (zero, zero))
        key_step(i, carry, True)

    blk = pl.BlockSpec((tq, HEAD), lambda h, i: (i, h))
    full = pl.BlockSpec((T, HEAD), lambda h, i: (0, h))
    shp = jax.ShapeDtypeStruct((T, H * HEAD), F32)
    return pl.pallas_call(
        body, name="sb_bwd", grid=(H, T // tq),
        in_specs=[blk, pl.BlockSpec((T, HEAD), lambda h, i: (0, H + h)), pl.BlockSpec((T, HEAD), lambda h, i: (0, 2 * H + h)),
                  blk, blk],
        out_specs=[blk, full, full], out_shape=[shp, shp, shp],
        compiler_params=_cp("parallel", "arbitrary"))(proj, proj, proj, do, L)


def _hg_tile(q, fl, iv, g, st, l0, l1, gain):
    R = 2 * HG_CHUNK
    row = _iota((R, R), 0)
    col = _iota((R, R), 1)
    first = row < HG_CHUNK
    same = first == (col < HG_CHUNK)
    tri = (row >= col) & same
    lb = _sigmoid(l0 - l1)
    f = lb + (1.0 - lb) * _sigmoid(fl)
    logf = jnp.log(f)
    k = 1.0 - f
    qf = q * _sigmoid(q)
    G = _sum_left(tri.astype(BF16), logf)
    gl_a = jnp.sum(jnp.where(first, logf, 0.0), axis=0, keepdims=True)
    gl_b = jnp.sum(jnp.where(first, 0.0, logf), axis=0, keepdims=True)
    q_dec = qf * jnp.exp(G)
    k_inv = k * jnp.exp(-G)
    k_end = k * jnp.exp(jnp.where(first, gl_a, gl_b) - G)
    scores = jnp.where(tri, mm_nt(q_dec, k_inv), 0.0)
    o = mm_nn(scores, iv)
    o_a = mm_nt(q_dec, st)
    st_mid = st * jnp.exp(gl_a) + mm_tn(jnp.where(first, iv, 0.0), k_end)
    o_b = mm_nt(q_dec, st_mid)
    st_new = st_mid * jnp.exp(gl_b) + mm_tn(jnp.where(first, 0.0, iv), k_end)
    o = o + jnp.where(first, o_a, o_b)
    on = o * lax.rsqrt(jnp.mean(o * o, axis=-1, keepdims=True) + NORM_EPS) * gain
    return on * (g * _sigmoid(g)), st_new


def _hg_heads(H):
    return _pick(H, (4, 2, 1))


def _hg_specs(H, c0, rev, nt):
    hb = _hg_heads(H)
    w = hb * HEAD

    def at(base):
        if rev:
            return pl.BlockSpec((HEAD, w), lambda h, i: (nt - 1 - i, base // hb + h))
        return pl.BlockSpec((HEAD, w), lambda h, i: (i, base // hb + h))
    return [at(c0), at(c0 + H), at(c0 + 2 * H), at(c0 + 3 * H)]


def _hg_fwd(proj, l0, l1, gain, H, c0):
    T = proj.shape[0]
    nt = T // HEAD
    hb = _hg_heads(H)
    w = hb * HEAD

    def body(q_ref, f_ref, i_ref, g_ref, l0_ref, l1_ref, gain_ref, o_ref, st_out_ref, st_ref):
        @pl.when(pl.program_id(1) == 0)
        def _():
            st_ref[...] = jnp.zeros_like(st_ref)

        for j in range(hb):
            s = slice(j * HEAD, (j + 1) * HEAD)
            st = st_ref[j]
            st_out_ref[j] = st
            out, st_new = _hg_tile(q_ref[:, s], f_ref[:, s], i_ref[:, s], g_ref[:, s], st, l0_ref[:, s], l1_ref[:, s],
                                   gain_ref[:, s])
            o_ref[:, s] = out.astype(o_ref.dtype)
            st_ref[j] = st_new

    vec = pl.BlockSpec((1, w), lambda h, i: (0, h))
    return pl.pallas_call(
        body, name="hg_fwd", grid=(H // hb, nt), in_specs=_hg_specs(H, c0, False, nt) + [vec, vec, vec],
        out_specs=[pl.BlockSpec((HEAD, w), lambda h, i: (i, h)),
                   pl.BlockSpec((hb, None, HEAD, HEAD), lambda h, i: (h, i, 0, 0))],
        out_shape=[jax.ShapeDtypeStruct((T, H * HEAD), BF16), jax.ShapeDtypeStruct((H, nt, HEAD, HEAD), F32)],
        scratch_shapes=[pltpu.VMEM((hb, HEAD, HEAD), F32)],
        compiler_params=_cp("parallel", "arbitrary"))(proj, proj, proj, proj, l0, l1, gain)


def _hg_bwd(proj, states, do, l0, l1, gain, H, c0, do_c0):
    T = proj.shape[0]
    nt = T // HEAD
    hb = _hg_heads(H)
    w = hb * HEAD

    def body(q_ref, f_ref, i_ref, g_ref, st_in_ref, do_ref, l0_ref, l1_ref, gain_ref,
             dq_ref, df_ref, di_ref, dg_ref, dl0_ref, dl1_ref, dgain_ref, dst_ref):
        @pl.when(pl.program_id(1) == 0)
        def _():
            dst_ref[...] = jnp.zeros_like(dst_ref)
            dl0_ref[...] = jnp.zeros_like(dl0_ref)
            dl1_ref[...] = jnp.zeros_like(dl1_ref)
            dgain_ref[...] = jnp.zeros_like(dgain_ref)

        for j in range(hb):
            s = slice(j * HEAD, (j + 1) * HEAD)
            _, vjp = jax.vjp(_hg_tile, q_ref[:, s], f_ref[:, s], i_ref[:, s], g_ref[:, s], st_in_ref[j],
                             l0_ref[:, s], l1_ref[:, s], gain_ref[:, s])
            dq, df, di, dg, dst, dl0, dl1, dgain = vjp((do_ref[:, s], dst_ref[j]))
            dq_ref[:, s] = dq
            df_ref[:, s] = df
            di_ref[:, s] = di
            dg_ref[:, s] = dg
            dst_ref[j] = dst
            dl0_ref[:, s] += dl0
            dl1_ref[:, s] += dl1
            dgain_ref[:, s] += dgain

    vec = pl.BlockSpec((1, w), lambda h, i: (0, h))
    rblk = pl.BlockSpec((HEAD, w), lambda h, i: (nt - 1 - i, h))
    shp = jax.ShapeDtypeStruct((T, H * HEAD), F32)
    vshp = jax.ShapeDtypeStruct((1, H * HEAD), F32)
    return pl.pallas_call(
        body, name="hg_bwd", grid=(H // hb, nt),
        in_specs=_hg_specs(H, c0, True, nt) + [
            pl.BlockSpec((hb, None, HEAD, HEAD), lambda h, i: (h, nt - 1 - i, 0, 0)),
            pl.BlockSpec((HEAD, w), lambda h, i: (nt - 1 - i, do_c0 // hb + h)), vec, vec, vec],
        out_specs=[rblk, rblk, rblk, rblk, vec, vec, vec],
        out_shape=[shp, shp, shp, shp, vshp, vshp, vshp],
        scratch_shapes=[pltpu.VMEM((hb, HEAD, HEAD), F32)],
        compiler_params=_cp("parallel", "arbitrary"))(proj, proj, proj, proj, states, do, l0, l1, gain)


def _sg_chunk(u_parts, v_parts, gains, biases, wpos, bpos):
    W = sum(p.shape[1] for p in v_parts)
    C = v_parts[0].shape[0]
    v = [_gelu(p) for p in v_parts]
    mu = sum(jnp.sum(p, axis=-1, keepdims=True) for p in v) * (1.0 / W)
    xc = [p - mu for p in v]
    r = lax.rsqrt(sum(jnp.sum(p * p, axis=-1, keepdims=True) for p in xc) * (1.0 / W) + NORM_EPS)
    causal = _iota((C, C), 0) >= _iota((C, C), 1)
    out = []
    for up, p, gn, bs, w, b in zip(u_parts, xc, gains, biases, wpos, bpos):
        vn = p * r * gn + bs
        mixed = mm_nn(jnp.where(causal, w, 0.0), vn) + b
        out.append(_gelu(up) * mixed)
    return out


def _sg_fwd(zpre, vgain, vbias, wpos, bpos):
    T, W2 = zpre.shape
    W = W2 // 2
    G = wpos.shape[0]
    cg = W // G
    C = SG_CHUNK

    def body(z_ref, gn_ref, bs_ref, w_ref, b_ref, s_ref):
        sl = [slice(g * cg, (g + 1) * cg) for g in range(G)]
        out = _sg_chunk([z_ref[:, s] for s in sl], [z_ref[:, W + s.start:W + s.stop] for s in sl],
                        [gn_ref[:, s] for s in sl], [bs_ref[:, s] for s in sl],
                        [w_ref[g] for g in range(G)], [b_ref[g] for g in range(G)])
        for s, o in zip(sl, out):
            s_ref[:, s] = o.astype(s_ref.dtype)

    return pl.pallas_call(
        body, name="sg_fwd", grid=(T // C,),
        in_specs=[pl.BlockSpec((C, W2), lambda i: (i, 0)), _vec_spec(W), _vec_spec(W),
                  pl.BlockSpec((G, C, C), lambda i: (0, 0, 0)), pl.BlockSpec((G, C, 1), lambda i: (0, 0, 0))],
        out_specs=pl.BlockSpec((C, W), lambda i: (i, 0)),
        out_shape=jax.ShapeDtypeStruct((T, W), BF16), compiler_params=_cp("parallel"))(zpre, vgain, vbias, wpos, bpos)


def _sg_bwd(zpre, ds, vgain, vbias, wpos, bpos):
    T, W2 = zpre.shape
    W = W2 // 2
    G = wpos.shape[0]
    cg = W // G
    C = SG_CHUNK

    def body(z_ref, ds_ref, gn_ref, bs_ref, w_ref, b_ref, dz_ref, dgn_ref, dbs_ref, dw_ref, db_ref):
        @pl.when(pl.program_id(0) == 0)
        def _():
            dgn_ref[...] = jnp.zeros_like(dgn_ref)
            dbs_ref[...] = jnp.zeros_like(dbs_ref)
            dw_ref[...] = jnp.zeros_like(dw_ref)
            db_ref[...] = jnp.zeros_like(db_ref)

        sl = [slice(g * cg, (g + 1) * cg) for g in range(G)]
        _, vjp = jax.vjp(_sg_chunk, [z_ref[:, s] for s in sl], [z_ref[:, W + s.start:W + s.stop] for s in sl],
                         [gn_ref[:, s] for s in sl], [bs_ref[:, s] for s in sl],
                         [w_ref[g] for g in range(G)], [b_ref[g] for g in range(G)])
        du, dv, dgn, dbs, dw, db = vjp([ds_ref[:, s] for s in sl])
        for g, s in enumerate(sl):
            dz_ref[:, s] = du[g].astype(dz_ref.dtype)
            dz_ref[:, W + s.start:W + s.stop] = dv[g].astype(dz_ref.dtype)
            dgn_ref[:, s] += dgn[g]
            dbs_ref[:, s] += dbs[g]
            dw_ref[g] += dw[g]
            db_ref[g] += db[g]

    wspec = pl.BlockSpec((G, C, C), lambda i: (0, 0, 0))
    bspec = pl.BlockSpec((G, C, 1), lambda i: (0, 0, 0))
    return pl.pallas_call(
        body, name="sg_bwd", grid=(T // C,),
        in_specs=[pl.BlockSpec((C, W2), lambda i: (i, 0)), pl.BlockSpec((C, W), lambda i: (i, 0)),
                  _vec_spec(W), _vec_spec(W), wspec, bspec],
        out_specs=[pl.BlockSpec((C, W2), lambda i: (i, 0)), _vec_spec(W), _vec_spec(W), wspec, bspec],
        out_shape=[jax.ShapeDtypeStruct((T, W2), BF16), jax.ShapeDtypeStruct((1, W), F32),
                   jax.ShapeDtypeStruct((1, W), F32), jax.ShapeDtypeStruct((G, C, C), F32),
                   jax.ShapeDtypeStruct((G, C, 1), F32)],
        compiler_params=_cp("arbitrary"))(zpre, ds, vgain, vbias, wpos, bpos)


def _conv_tiles(T, F):
    return _pick(T, (512, 256, 128, 64, 32, 16, 8)), _pick(F, (512, 256, 128))


def _shift_down(cur, prev8, n, first_tile):
    bt = cur.shape[0]
    r = pltpu.roll(cur, n, 0)
    p = pltpu.roll(prev8, n, 0)
    p = jnp.where(first_tile, 0.0, p)
    head = jnp.concatenate([p, r[8:]], axis=0) if bt > 8 else p
    return jnp.where(_iota(cur.shape, 0) < n, head, r)


def _shift_up(cur, next8, n, last_tile):
    bt = cur.shape[0]
    r = pltpu.roll(cur, bt - n, 0)
    p = pltpu.roll(next8, 8 - n, 0)
    p = jnp.where(last_tile, 0.0, p)
    tail = jnp.concatenate([r[:bt - 8], p], axis=0) if bt > 8 else p
    return jnp.where(_iota(cur.shape, 0) >= bt - n, tail, r)


def _conv_apply(cur, prev8, w_ref, b, first_tile):
    return (b + w_ref[0:1, :] * _shift_down(cur, prev8, 2, first_tile)
            + w_ref[1:2, :] * _shift_down(cur, prev8, 1, first_tile) + w_ref[2:3, :] * cur)


def _conv_fwd(name, a, w, b):
    T, F2 = a.shape
    F = F2 // 2
    bt, cw = _conv_tiles(T, F)
    nf = F // cw
    r8 = bt // 8

    def body(g_ref, gp_ref, v_ref, vp_ref, wg_ref, wv_ref, bg_ref, bv_ref, u_ref):
        first = pl.program_id(0) == 0
        gate = _conv_apply(g_ref[...], gp_ref[...], wg_ref, bg_ref[...], first)
        val = _conv_apply(v_ref[...], vp_ref[...], wv_ref, bv_ref[...], first)
        u_ref[...] = (gate * _sigmoid(gate) * val).astype(u_ref.dtype)

    def cur(off):
        return pl.BlockSpec((bt, cw), lambda i, j: (i, j + off))

    def prev(off):
        return pl.BlockSpec((8, cw), lambda i, j: (jnp.maximum(i * r8 - 1, 0), j + off))

    def vec(rows, off):
        return pl.BlockSpec((rows, cw), lambda i, j: (0, j + off))

    return pl.pallas_call(
        body, name=name, grid=(T // bt, nf),
        in_specs=[cur(0), prev(0), cur(nf), prev(nf), vec(3, 0), vec(3, nf), vec(1, 0), vec(1, nf)],
        out_specs=pl.BlockSpec((bt, cw), lambda i, j: (i, j)),
        out_shape=jax.ShapeDtypeStruct((T, F), BF16),
        compiler_params=_cp("parallel", "parallel"))(a, a, a, a, w, w, b, b)


def _conv_bwd(name, a, du, w, b):
    T, F2 = a.shape
    F = F2 // 2
    bt, cw = _conv_tiles(T, F)
    nf = F // cw
    r8 = bt // 8
    last_blk = T // 8 - 1

    def body(g_ref, gp_ref, gn_ref, v_ref, vp_ref, vn_ref, du_ref, dun_ref, wg_ref, wv_ref, bg_ref, bv_ref,
             dag_ref, dav_ref, dwg_ref, dwv_ref, dbg_ref, dbv_ref):
        i = pl.program_id(1)
        first = i == 0
        last = i == pl.num_programs(1) - 1

        def taps(cur, prev8, at_start):
            return _shift_down(cur, prev8, 2, at_start), _shift_down(cur, prev8, 1, at_start), cur

        def conv(t, w_ref, b_ref):
            return b_ref[...] + w_ref[0:1, :] * t[0] + w_ref[1:2, :] * t[1] + w_ref[2:3, :] * t[2]

        def act_bwd(gate, val, du_):
            sg = _sigmoid(gate)
            return du_ * val * (sg * (1.0 + gate * (1.0 - sg))), du_ * gate * sg

        g_cur, v_cur = g_ref[...], v_ref[...]
        tg = taps(g_cur, gp_ref[...], first)
        tv = taps(v_cur, vp_ref[...], first)
        dg, dv = act_bwd(conv(tg, wg_ref, bg_ref), conv(tv, wv_ref, bv_ref), du_ref[...])
        tgn = taps(gn_ref[...], g_cur[bt - 8:, :], False)
        tvn = taps(vn_ref[...], v_cur[bt - 8:, :], False)
        dgn, dvn = act_bwd(conv(tgn, wg_ref, bg_ref), conv(tvn, wv_ref, bv_ref), dun_ref[...])

        def conv_t(d, dn, w_ref):
            return w_ref[2:3, :] * d + w_ref[1:2, :] * _shift_up(d, dn, 1, last) + w_ref[0:1, :] * _shift_up(d, dn, 2, last)

        dag_ref[...] = conv_t(dg, dgn, wg_ref).astype(dag_ref.dtype)
        dav_ref[...] = conv_t(dv, dvn, wv_ref).astype(dav_ref.dtype)

        @pl.when(first)
        def _():
            dwg_ref[...] = jnp.zeros_like(dwg_ref)
            dwv_ref[...] = jnp.zeros_like(dwv_ref)
            dbg_ref[...] = jnp.zeros_like(dbg_ref)
            dbv_ref[...] = jnp.zeros_like(dbv_ref)

        for t in range(CONV_WIDTH):
            dwg_ref[t:t + 1, :] += jnp.sum(dg * tg[t], axis=0, keepdims=True)
            dwv_ref[t:t + 1, :] += jnp.sum(dv * tv[t], axis=0, keepdims=True)
        dbg_ref[...] += jnp.sum(dg, axis=0, keepdims=True)
        dbv_ref[...] += jnp.sum(dv, axis=0, keepdims=True)

    def cur(off):
        return pl.BlockSpec((bt, cw), lambda j, i: (i, j + off))

    def prev(off):
        return pl.BlockSpec((8, cw), lambda j, i: (jnp.maximum(i * r8 - 1, 0), j + off))

    def nxt(off):
        return pl.BlockSpec((8, cw), lambda j, i: (jnp.minimum((i + 1) * r8, last_blk), j + off))

    def vec(rows, off):
        return pl.BlockSpec((rows, cw), lambda j, i: (0, j + off))

    half = jax.ShapeDtypeStruct((T, F), BF16)
    dag, dav, dwg, dwv, dbg, dbv = pl.pallas_call(
        body, name=name, grid=(nf, T // bt),
        in_specs=[cur(0), prev(0), nxt(0), cur(nf), prev(nf), nxt(nf), cur(0), nxt(0),
                  vec(3, 0), vec(3, nf), vec(1, 0), vec(1, nf)],
        out_specs=[cur(0), cur(0), vec(3, 0), vec(3, 0), vec(1, 0), vec(1, 0)],
        out_shape=[half, half, jax.ShapeDtypeStruct((3, F), F32), jax.ShapeDtypeStruct((3, F), F32),
                   jax.ShapeDtypeStruct((1, F), F32), jax.ShapeDtypeStruct((1, F), F32)],
        compiler_params=_cp("parallel", "arbitrary"))(a, a, a, a, a, a, du, du, w, w, b, b)
    return dag, dav, jnp.concatenate([dwg, dwv], axis=1), jnp.concatenate([dbg, dbv], axis=1)


def _ada_fwd(c_all, ada_w, ada_b):
    R, D = c_all.shape
    L, _, Ns = ada_w.shape
    tn = _pick(Ns, (512, 256, 128))

    def body(c_ref, w_ref, b_ref, o_ref):
        cv = c_ref[...]
        cond = cv * _sigmoid(cv)
        o_ref[...] = _dg(cond, w_ref[...], 1, 0) + b_ref[...]

    return pl.pallas_call(
        body, name="ada_fwd", grid=(L, Ns // tn),
        in_specs=[pl.BlockSpec((R, D), lambda l, j: (0, 0)), pl.BlockSpec((None, D, tn), lambda l, j: (l, 0, j)),
                  pl.BlockSpec((None, 1, tn), lambda l, j: (l, 0, j))],
        out_specs=pl.BlockSpec((None, R, tn), lambda l, j: (l, 0, j)),
        out_shape=jax.ShapeDtypeStruct((L, R, Ns), F32), compiler_params=_cp("parallel", "parallel"))(c_all, ada_w, ada_b)


def _adam_math(w, g, m, v):
    m2 = ADAM_B1 * m + (1.0 - ADAM_B1) * g
    v2 = ADAM_B2 * v + (1.0 - ADAM_B2) * (g * g)
    m_hat = m2 / (1.0 - ADAM_B1 ** ADAM_STEP)
    v_hat = v2 / (1.0 - ADAM_B2 ** ADAM_STEP)
    delta = -ADAM_LR * (m_hat / (jnp.sqrt(v_hat) + ADAM_EPS) + ADAM_WD * w)
    return delta, m2, v2


def _ada_grad_adam(c_all_t, dmod, w, m, v):
    D, R = c_all_t.shape
    L, _, Ns = dmod.shape
    tr = _rows_within(D, Ns * 4, 1 << 20)

    def body(c_ref, d_ref, w_ref, m_ref, v_ref, g_ref, dl_ref, m2_ref, v2_ref):
        cv = c_ref[...]
        g = _dg(cv * _sigmoid(cv), d_ref[...], 1, 0)
        g_ref[...] = g
        dl_ref[...], m2_ref[...], v2_ref[...] = _adam_math(w_ref[...], g, m_ref[...], v_ref[...])

    big = pl.BlockSpec((None, tr, Ns), lambda l, i: (l, i, 0))
    shp = jax.ShapeDtypeStruct((L, D, Ns), F32)
    return pl.pallas_call(
        body, name="ada_grad_adam", grid=(L, D // tr),
        in_specs=[pl.BlockSpec((tr, R), lambda l, i: (i, 0)), pl.BlockSpec((None, R, Ns), lambda l, i: (l, 0, 0)), big, big, big],
        out_specs=[big] * 4, out_shape=[shp] * 4, compiler_params=_cp("parallel", "parallel"))(c_all_t, dmod, w, m, v)


def _adam(name, w, g, m, v):
    R, C = w.shape
    tr = _rows_within(R, C * 4, 1 << 21)

    def body(w_ref, g_ref, m_ref, v_ref, dl_ref, m2_ref, v2_ref):
        dl_ref[...], m2_ref[...], v2_ref[...] = _adam_math(w_ref[...], g_ref[...], m_ref[...], v_ref[...])

    blk = pl.BlockSpec((tr, C), lambda i: (i, 0))
    shp = jax.ShapeDtypeStruct((R, C), F32)
    return pl.pallas_call(body, name=name, grid=(R // tr,), in_specs=[blk] * 4, out_specs=[blk] * 3,
                          out_shape=[shp] * 3, compiler_params=_cp("parallel"))(w, g, m, v)


def _cast_into_rows(name, w, chip, after=None):
    _, R, C = w.shape
    tr = _rows_within(R, C * 4, 1 << 22)
    extra = [] if after is None else [after]

    def body(chip_ref, w_ref, *rest):
        o_ref = rest[-1]
        o_ref[...] = w_ref[...].astype(BF16)

    grid_spec = pltpu.PrefetchScalarGridSpec(
        num_scalar_prefetch=1, grid=(2, R // tr),
        in_specs=[pl.BlockSpec((None, tr, C), lambda h, i, s: (h, i, 0))] + [ANY] * len(extra),
        out_specs=pl.BlockSpec((None, tr, C), lambda h, i, s: (2 * s[0] + h, i, 0)))
    return pl.pallas_call(body, name=name, grid_spec=grid_spec, out_shape=jax.ShapeDtypeStruct((N_DEV, R, C), BF16),
                          compiler_params=_cp("arbitrary", "arbitrary"))(chip.reshape(1).astype(jnp.int32), w, *extra)


def _add_pairs(name, eight, from_sib, c):
    _, R, C = from_sib.shape
    tr = _rows_within(R, C * 2, 1 << 21)

    def body(c_ref, a_ref, b_ref, o_ref):
        o_ref[...] = (a_ref[...].astype(F32) + b_ref[...].astype(F32)).astype(o_ref.dtype)

    blk = pl.BlockSpec((None, tr, C), lambda j, i, s: (j, i, 0))
    grid_spec = pltpu.PrefetchScalarGridSpec(
        num_scalar_prefetch=1, grid=(4, R // tr),
        in_specs=[pl.BlockSpec((None, tr, C), lambda j, i, s: (2 * j + s[0], i, 0)), blk], out_specs=blk)
    return pl.pallas_call(body, name=name, grid_spec=grid_spec, out_shape=jax.ShapeDtypeStruct(from_sib.shape, BF16),
                          compiler_params=_cp("arbitrary", "arbitrary"))(c.reshape(1).astype(jnp.int32), eight, from_sib)


def _sum_into_pair(name, own, landed, slot, chip):
    n, R, C = landed.shape
    tr = _rows_within(R, (n + 1) * C * landed.dtype.itemsize, 1 << 23)

    def body(idx_ref, own_ref, x_ref, o_ref):
        mine = idx_ref[1]
        acc = None
        for j in range(n):
            part = jnp.where(mine == j, own_ref[...], x_ref[j]).astype(F32)
            acc = part if acc is None else acc + part
        o_ref[...] = acc

    grid_spec = pltpu.PrefetchScalarGridSpec(
        num_scalar_prefetch=1, grid=(R // tr,),
        in_specs=[pl.BlockSpec((None, tr, C), lambda i, s: (s[1], i, 0)), pl.BlockSpec((n, tr, C), lambda i, s: (0, i, 0))],
        out_specs=pl.BlockSpec((None, tr, C), lambda i, s: (s[0], i, 0)))
    idx = jnp.stack([slot, chip]).astype(jnp.int32)
    return pl.pallas_call(body, name=name, grid_spec=grid_spec, out_shape=jax.ShapeDtypeStruct((2, R, C), F32),
                          compiler_params=_cp("arbitrary"))(idx, own, landed)


def _sum_leading(name, a, out_dtype=F32):
    n, R, C = a.shape
    tr = _rows_within(R, n * C * a.dtype.itemsize, 1 << 23)

    def body(a_ref, o_ref):
        acc = a_ref[0].astype(F32)
        for j in range(1, n):
            acc = acc + a_ref[j].astype(F32)
        o_ref[...] = acc.astype(o_ref.dtype)

    return pl.pallas_call(body, name=name, grid=(R // tr,), in_specs=[pl.BlockSpec((n, tr, C), lambda i: (0, i, 0))],
                          out_specs=pl.BlockSpec((tr, C), lambda i: (i, 0)),
                          out_shape=jax.ShapeDtypeStruct((R, C), out_dtype), compiler_params=_cp("parallel"))(a)


def _place():
    return lax.axis_index("x"), lax.axis_index("y"), lax.axis_index("c")


def _all_gather(name, blocks, halves=False, after=None):
    n = len(blocks)
    shapes = [b.shape[1:] if halves else b.shape for b in blocks]
    extra = [] if after is None else [after]

    def body(*refs):
        ins, outs = refs[:n], refs[n + len(extra):2 * n + len(extra)]
        send_sems, recv_sems, local_sems = refs[2 * n + len(extra):]
        x, y, c = _place()
        me, sibling = (x, y, c), (x, y, 1 - c)
        chips = [(1 - x, y), (x, 1 - y), (1 - x, 1 - y)]

        def rows(a, px, py, pc):
            return outs[a].at[4 * px + 2 * py + pc]

        def copy(a, k, block, to, src=None):
            return pltpu.make_async_remote_copy(
                src_ref=rows(a, *block) if src is None else src, dst_ref=rows(a, *block),
                send_sem=send_sems.at[7 * a + k], recv_sem=recv_sems.at[7 * a + k],
                device_id=to, device_id_type=MESH)

        started = []
        mine = []
        for a in range(n):
            src = ins[a].at[c] if halves else ins[a]
            mine.append(pltpu.make_async_copy(src, rows(a, *me), local_sems.at[a]))
            mine[-1].start()
            first = [copy(a, 0, me, sibling, src=src)]
            first += [copy(a, 1 + j, me, (*chip, c), src=src) for j, chip in enumerate(chips)]
            for cp in first:
                cp.start()
            started += first
        for j, chip in enumerate(chips):
            for a in range(n):
                copy(a, 1 + j, (*chip, c), me).wait_recv()
                passed = copy(a, 4 + j, (*chip, c), sibling)
                passed.start()
                started.append(passed)
        for a in range(n):
            copy(a, 0, sibling, me).wait_recv()
            for j, chip in enumerate(chips):
                copy(a, 4 + j, (*chip, 1 - c), me).wait_recv()
        for cp in started:
            cp.wait_send()
        for cp in mine:
            cp.wait()

    return pl.pallas_call(
        body, name=name, in_specs=[ANY] * (n + len(extra)), out_specs=[ANY] * n,
        out_shape=[jax.ShapeDtypeStruct((N_DEV,) + tuple(s), b.dtype) for s, b in zip(shapes, blocks)],
        scratch_shapes=[pltpu.SemaphoreType.DMA((7 * n,)), pltpu.SemaphoreType.DMA((7 * n,)),
                        pltpu.SemaphoreType.DMA((n,))],
    )(*blocks, *extra)


def _share_halves(name, arrays):
    n = len(arrays)

    def body(*refs):
        ins, outs = refs[:n], refs[n:2 * n]
        send_sems, recv_sems = refs[2 * n:]
        x, y, c = _place()
        started = []
        for a in range(n):
            cp = pltpu.make_async_remote_copy(src_ref=ins[a].at[c], dst_ref=outs[a].at[c], send_sem=send_sems.at[a],
                                              recv_sem=recv_sems.at[a], device_id=(x, y, 1 - c), device_id_type=MESH)
            cp.start()
            started.append(cp)
        for a in range(n):
            started[a].wait_send()
            pltpu.make_async_remote_copy(src_ref=ins[a].at[1 - c], dst_ref=outs[a].at[1 - c], send_sem=send_sems.at[a],
                                         recv_sem=recv_sems.at[a], device_id=(x, y, 1 - c), device_id_type=MESH).wait_recv()

    return pl.pallas_call(
        body, name=name, in_specs=[ANY] * n, out_specs=[ANY] * n,
        out_shape=[jax.ShapeDtypeStruct(a.shape, a.dtype) for a in arrays],
        input_output_aliases={a: a for a in range(n)},
        scratch_shapes=[pltpu.SemaphoreType.DMA((n,)), pltpu.SemaphoreType.DMA((n,))],
    )(*arrays)


HBM = pl.BlockSpec(memory_space=pltpu.HBM)
SEM = pl.BlockSpec(memory_space=pltpu.SEMAPHORE)
EFFECT = pltpu.SideEffectType.DATAFLOW_SIDE_EFFECTING


COPIES_PER_ARRAY = {"rows": 3, "fill": 3, "parts": 3, "halves": 4}


def _chip_copies(kind, srcs, dsts, send_sems, recv_sems):
    x, y, c = _place()
    mine = 2 * x + y
    per = COPIES_PER_ARRAY[kind]
    sends, arrivals = [], []
    for a in range(len(srcs)):
        if kind == "halves":
            for j in range(N_CHIP):
                cp = pltpu.make_async_remote_copy(
                    src_ref=srcs[a].at[2 * j + 1 - c], dst_ref=dsts[a].at[j], send_sem=send_sems.at[per * a + j],
                    recv_sem=recv_sems.at[per * a + j], device_id=(x, y, 1 - c), device_id_type=MESH)
                sends.append(cp)
                arrivals.append(cp)
            continue
        for k, (px, py) in enumerate([(1 - x, y), (x, 1 - y), (1 - x, 1 - y)]):
            other = 2 * px + py
            if kind == "fill":
                cp = dict(send_sem=send_sems.at[per * a + k], recv_sem=recv_sems.at[per * a + k], device_id=(x, y, 1 - c),
                          device_id_type=MESH)
                sends.append(pltpu.make_async_remote_copy(src_ref=srcs[a].at[2 * other + c], dst_ref=dsts[a].at[2 * other + c], **cp))
                arrivals.append(pltpu.make_async_remote_copy(src_ref=srcs[a].at[2 * other + c],
                                                             dst_ref=dsts[a].at[2 * other + 1 - c], **cp))
                continue
            if kind == "rows":
                src, dst, lands = srcs[a].at[2 * mine + c], dsts[a].at[2 * mine + c], dsts[a].at[2 * other + c]
            else:
                src, dst, lands = srcs[a].at[other], dsts[a].at[mine], dsts[a].at[other]
            sem = dict(send_sem=send_sems.at[per * a + k], recv_sem=recv_sems.at[per * a + k], device_id=(px, py, c),
                       device_id_type=MESH)
            sends.append(pltpu.make_async_remote_copy(src_ref=src, dst_ref=dst, **sem))
            arrivals.append(pltpu.make_async_remote_copy(src_ref=src, dst_ref=lands, **sem))
    return sends, arrivals


def _chips_start(name, kind, srcs, dsts=None, after=None):
    n = len(srcs)
    bufs = list(srcs) + (list(dsts) if dsts is not None else [])
    nb = len(bufs)
    extra = [] if after is None else [after]

    def body(*refs):
        ins = refs[:nb]
        send_sems, recv_sems = refs[nb + len(extra)], refs[nb + len(extra) + 1]
        token = refs[-1]
        sends, _ = _chip_copies(kind, ins[:n], ins[n:] if dsts is not None else ins[:n], send_sems, recv_sems)
        for cp in sends:
            cp.start()
        token[...] = jnp.zeros_like(token)

    out = pl.pallas_call(
        body, name=name,
        out_shape=(pltpu.SemaphoreType.DMA((COPIES_PER_ARRAY[kind] * n,)), pltpu.SemaphoreType.DMA((COPIES_PER_ARRAY[kind] * n,)),
                   *[pltpu.HBM(b.shape, b.dtype) for b in bufs], jax.ShapeDtypeStruct((8, HEAD), F32)),
        in_specs=(HBM,) * nb + (ANY,) * len(extra),
        out_specs=(SEM, SEM) + (HBM,) * nb + (pl.BlockSpec(memory_space=pltpu.VMEM),),
        input_output_aliases={i: 2 + i for i in range(nb)},
        compiler_params=pltpu.CompilerParams(has_side_effects=EFFECT),
    )(*[pltpu.with_memory_space_constraint(b, pltpu.HBM) for b in bufs], *extra)
    return out[0], out[1], list(out[2:2 + nb]), out[-1]


def _chips_wait(name, kind, n, send_sems, recv_sems, bufs, after):
    nb = len(bufs)

    def body(*refs):
        ins = refs[:nb]
        s_sems, r_sems = refs[nb], refs[nb + 1]
        sends, arrivals = _chip_copies(kind, ins[:n], ins[n:] if nb > n else ins[:n], s_sems, r_sems)
        for cp in sends:
            cp.wait_send()
        for cp in arrivals:
            cp.wait_recv()

    return list(pl.pallas_call(
        body, name=name, out_shape=tuple(pltpu.HBM(b.shape, b.dtype) for b in bufs),
        in_specs=(HBM,) * nb + (SEM, SEM, pl.BlockSpec(memory_space=pl.ANY)), out_specs=(HBM,) * nb,
        input_output_aliases={i: i for i in range(nb)},
        compiler_params=pltpu.CompilerParams(has_side_effects=EFFECT),
    )(*bufs, send_sems, recv_sems, after))


def _fill_from_sibling(name, arrays):
    n = len(arrays)

    def body(*refs):
        ins, outs = refs[:n], refs[n:2 * n]
        send_sems, recv_sems = refs[2 * n:]
        x, y, c = _place()
        sends, arrivals = [], []
        for a in range(n):
            for k, (px, py) in enumerate([(1 - x, y), (x, 1 - y), (1 - x, 1 - y)]):
                sem = dict(send_sem=send_sems.at[3 * a + k], recv_sem=recv_sems.at[3 * a + k], device_id=(x, y, 1 - c),
                           device_id_type=MESH)
                row = 2 * (2 * px + py)
                sends.append(pltpu.make_async_remote_copy(src_ref=ins[a].at[row + c], dst_ref=outs[a].at[row + c], **sem))
                arrivals.append(pltpu.make_async_remote_copy(src_ref=ins[a].at[row + c], dst_ref=outs[a].at[row + 1 - c], **sem))
        for cp in sends:
            cp.start()
        for cp in sends:
            cp.wait_send()
        for cp in arrivals:
            cp.wait_recv()

    return pl.pallas_call(
        body, name=name, in_specs=[ANY] * n, out_specs=[ANY] * n,
        out_shape=[jax.ShapeDtypeStruct(a.shape, a.dtype) for a in arrays],
        input_output_aliases={a: a for a in range(n)},
        scratch_shapes=[pltpu.SemaphoreType.DMA((3 * n,)), pltpu.SemaphoreType.DMA((3 * n,))],
    )(*arrays)


def kernel(x, c, ada_w, ada_b, mix_norm, ffn_norm, par_w_in, par_w_out, hg_lb_logits, hg_out_norm, sg_w_in, sg_v_gain, sg_v_bias, sg_w_pos, sg_b_pos, sg_w_out, ffn_up, ffn_conv_w, ffn_conv_b, ffn_down, final_norm, loss_target, m_ada_w, m_ada_b, m_mix_norm, m_ffn_norm, m_par_w_in, m_par_w_out, m_hg_lb_logits, m_hg_out_norm, m_sg_w_in, m_sg_v_gain, m_sg_v_bias, m_sg_w_pos, m_sg_b_pos, m_sg_w_out, m_ffn_up, m_ffn_conv_w, m_ffn_conv_b, m_ffn_down, m_final_norm, v_ada_w, v_ada_b, v_mix_norm, v_ffn_norm, v_par_w_in, v_par_w_out, v_hg_lb_logits, v_hg_out_norm, v_sg_w_in, v_sg_v_gain, v_sg_v_bias, v_sg_w_pos, v_sg_b_pos, v_sg_w_out, v_ffn_up, v_ffn_conv_w, v_ffn_conv_b, v_ffn_down, v_final_norm):
    names = ["ada_w", "ada_b", "mix_norm", "ffn_norm", "par_w_in", "par_w_out", "hg_lb_logits", "hg_out_norm", "sg_w_in",
             "sg_v_gain", "sg_v_bias", "sg_w_pos", "sg_b_pos", "sg_w_out", "ffn_up", "ffn_conv_w", "ffn_conv_b",
             "ffn_down", "final_norm"]
    W = dict(zip(names, [ada_w, ada_b, mix_norm, ffn_norm, par_w_in, par_w_out, hg_lb_logits, hg_out_norm, sg_w_in,
                         sg_v_gain, sg_v_bias, sg_w_pos, sg_b_pos, sg_w_out, ffn_up, ffn_conv_w, ffn_conv_b, ffn_down,
                         final_norm]))
    M = dict(zip(names, [m_ada_w, m_ada_b, m_mix_norm, m_ffn_norm, m_par_w_in, m_par_w_out, m_hg_lb_logits, m_hg_out_norm,
                         m_sg_w_in, m_sg_v_gain, m_sg_v_bias, m_sg_w_pos, m_sg_b_pos, m_sg_w_out, m_ffn_up, m_ffn_conv_w,
                         m_ffn_conv_b, m_ffn_down, m_final_norm]))
    V = dict(zip(names, [v_ada_w, v_ada_b, v_mix_norm, v_ffn_norm, v_par_w_in, v_par_w_out, v_hg_lb_logits, v_hg_out_norm,
                         v_sg_w_in, v_sg_v_gain, v_sg_v_bias, v_sg_w_pos, v_sg_b_pos, v_sg_w_out, v_ffn_up, v_ffn_conv_w,
                         v_ffn_conv_b, v_ffn_down, v_final_norm]))

    x = x[0]
    target = loss_target[0]
    T, D = x.shape
    ix, iy, ic = _place()
    chip = 2 * ix + iy
    dev = 2 * chip + ic
    H = hg_out_norm.shape[1]
    SBW = H * HEAD
    NA = ada_w.shape[2]
    F2s = ffn_up.shape[2]
    F2 = N_CHIP * F2s
    SGW = sg_w_out.shape[1] * N_CHIP
    G = sg_w_pos.shape[1]

    shards = [par_w_in[0], par_w_out[0], sg_w_in[0], sg_w_out[0], ffn_up[0], ffn_up[1], ffn_down[0], ffn_down[1]]
    kinds = ["col", "row", "col", "row", "col", "col", "row", "row"]
    halves = [w.reshape(2, w.shape[0] // 2, w.shape[1]) for w in shards]
    groups = {"a": [0], "b": [1, 4, 6], "c": [2, 3, 5, 7]}
    rows8 = {0: _cast_into_rows("cast_w", halves[0], chip)}
    started = {}

    def as_weights(g, bufs):
        out = {}
        for i, g8 in zip(groups[g], bufs):
            K, N = shards[i].shape
            out[i] = g8.reshape(N_CHIP, K, N) if kinds[i] == "col" else g8.reshape(N_CHIP * K, N)
        return out

    def weights_landed(g, after):
        send_sems, recv_sems, bufs, _ = started[g]
        bufs = _chips_wait("gather_wait_" + g, "rows", len(bufs), send_sems, recv_sems, bufs, after)
        return _chips_start("gather_fill_start_" + g, "fill", bufs)

    def weights_of(g, filling, after):
        send_sems, recv_sems, bufs, _ = filling
        return as_weights(g, _chips_wait("gather_fill_wait_" + g, "fill", len(bufs), send_sems, recv_sems, bufs, after))

    n_cw = ffn_conv_w.size
    n_sv = sg_v_gain.size
    c_all, small_all = _all_gather("gather_small", [c, _pack_rows([ffn_conv_w, sg_v_gain, sg_v_bias])])
    c_all = c_all.reshape(N_DEV, D)
    small_all = small_all.reshape(N_CHIP, 2, -1)[:, 0]
    conv_w_full = small_all[:, :n_cw].reshape(N_CHIP, 2, CONV_WIDTH, F2s).transpose(1, 2, 0, 3).reshape(2, CONV_WIDTH, F2)
    sg_gain_full = small_all[:, n_cw:n_cw + n_sv].reshape(1, SGW)
    sg_bias_full = small_all[:, n_cw + n_sv:n_cw + 2 * n_sv].reshape(1, SGW)

    c_pad = jnp.pad(c_all, ((0, 16 - N_DEV), (0, 0)))
    ada_b_sh = lax.dynamic_slice(ada_b, (0, chip * NA), (2, NA)).reshape(2, 1, NA)
    mod_sh = _ada_fwd(c_pad, ada_w, ada_b_sh)
    mod_all, = _all_gather("gather_mod", [mod_sh[:, :N_DEV]])
    mod_all = mod_all.reshape(N_CHIP, 2, 2, N_DEV, NA)[:, 0]
    mod = lax.dynamic_index_in_dim(mod_all, dev, axis=2, keepdims=False)
    mod = mod.transpose(1, 0, 2).reshape(2, 6, D)
    mods = [[mod[l, k].reshape(1, D) for k in range(6)] for l in range(2)]
    started["a"] = _chips_start("gather_start_a", "rows", [rows8[0]], after=mod)
    for i in range(1, len(shards)):
        rows8[i] = _cast_into_rows("cast_w", halves[i], chip, after=started["a"][3])

    vec = lambda a: a.reshape(1, -1)
    l0 = vec(hg_lb_logits[0])
    l1 = vec(hg_lb_logits[1])
    hg_gain = vec(hg_out_norm[0])
    wpos = sg_w_pos[0]
    bpos = sg_b_pos[0].reshape(G, SG_CHUNK, 1)
    conv_b = [vec(ffn_conv_b[l]) for l in range(2)]

    sh1, sc1, g1, sh2, sc2, g2 = mods[0]
    send_a, recv_a, bufs_a, _ = started["a"]
    bufs_a = _chips_wait("gather_wait_a", "rows", 1, send_a, recv_a, bufs_a, rows8[len(shards) - 1])
    bufs_a = _fill_from_sibling("gather_fill_a", bufs_a)
    started["b"] = _chips_start("gather_start_b", "rows", [rows8[i] for i in groups["b"]], after=bufs_a[0])
    start_token = started["a"][3][0, 0] + started["b"][3][0, 0]
    w_in = as_weights("a", bufs_a)[0]
    h0 = _normmod_fwd("norm_mix0", x, vec(mix_norm[0]) + start_token, sc1, sh1)
    proj = _mm_nn("mm_par_in", h0, w_in)
    o_sb, sb_tot = _sb_fwd(proj, H)
    filling_b = weights_landed("b", o_sb)
    started["c"] = _chips_start("gather_start_c", "rows", [rows8[i] for i in groups["c"]], after=o_sb)
    o_hg, hg_states = _hg_fwd(proj, l0 + filling_b[3][0:1, 0:1] + started["c"][3][0:1, 0:1], l1, hg_gain, H, 3 * H)
    o_cat = jnp.concatenate([o_sb, o_hg], axis=1)
    wb = weights_of("b", filling_b, o_cat)
    w_out, wup, wdn = wb[1], [wb[4], None], [wb[6], None]
    y0 = _mm_nn("mm_par_out", o_cat, w_out)
    x1, h0f = _res_normmod_fwd("res_norm_ffn0", x, y0, g1, vec(ffn_norm[0]), sc2, sh2)
    a0 = _mm_nn("mm_up0", h0f, wup[0])
    u0 = _conv_fwd("conv_fwd0", a0, conv_w_full[0], conv_b[0])
    filling_c = weights_landed("c", u0)
    f0 = _mm_nn("mm_down0", u0, wdn[0])
    sh1b, sc1b, g1b, sh2b, sc2b, g2b = mods[1]
    x2, h1 = _res_normmod_fwd("res_norm_mix1", x1, f0, g2, vec(mix_norm[1]) + filling_c[3][0:1, 0:1], sc1b, sh1b)
    wc = weights_of("c", filling_c, h1)
    wsg_in, wsg_out, wup[1], wdn[1] = wc[2], wc[3], wc[5], wc[7]
    zpre = _mm_nn("mm_sg_in", h1, wsg_in)
    s1 = _sg_fwd(zpre, sg_gain_full, sg_bias_full, wpos, bpos)
    y1 = _mm_nn("mm_sg_out", s1, wsg_out)
    x3, h1f = _res_normmod_fwd("res_norm_ffn1", x2, y1, g1b, vec(ffn_norm[1]), sc2b, sh2b)
    a1 = _mm_nn("mm_up1", h1f, wup[1])
    u1 = _conv_fwd("conv_fwd1", a1, conv_w_full[1], conv_b[1])
    f1 = _mm_nn("mm_down1", u1, wdn[1])
    loss_sum, dx, df1, dg2b, d_final = _final_fwd_bwd(x3, f1, g2b, vec(final_norm), target)
    loss = lax.psum(loss_sum[0, 0], ("x", "y", "c"))

    def reduce_start(tag, idx, grads):
        eights = [g.reshape((N_DEV, -1, g.shape[-1])) for g in grads]
        landing = [lax.empty((N_CHIP,) + e.shape[1:], e.dtype) for e in eights]
        send_sems, recv_sems, bufs, token = _chips_start("grads_sibling_start_" + tag, "halves", eights, landing)
        return (tag, idx, send_sems, recv_sems, bufs), token[0:1, 0:1]

    def reduce_cross(state, after):
        tag, idx, send_sems, recv_sems, bufs = state
        n = len(idx)
        bufs = _chips_wait("grads_sibling_wait_" + tag, "halves", n, send_sems, recv_sems, bufs, after)
        pair = [_add_pairs("add_pair", e, r, ic) for e, r in zip(bufs[:n], bufs[n:])]
        landing = [lax.empty(p.shape, p.dtype) for p in pair]
        send_sems, recv_sems, bufs, token = _chips_start("grads_start_" + tag, "parts", pair, landing)
        return (tag, idx, send_sems, recv_sems, bufs), token[0:1, 0:1]

    def reduce_finish(state, after):
        tag, idx, send_sems, recv_sems, bufs = state
        n = len(idx)
        bufs = _chips_wait("grads_wait_" + tag, "parts", n, send_sems, recv_sems, bufs, after)
        halves = [_sum_into_pair("sum_chips", p, x_, ic, chip) for p, x_ in zip(bufs[:n], bufs[n:])]
        both = _share_halves("grads_share_" + tag, halves)
        return {i: b.reshape(shards[i].shape) for i, b in zip(idx, both)}

    def ffn_bwd(l, dfl, u, a, hf):
        g_dn = _mm_tn("mm_g_down", u, dfl)
        du = _mm_nt("mm_d_u", dfl, wdn[l])
        da_g, da_v, dcw, dcb = _conv_bwd("conv_bwd", a, du, conv_w_full[l], conv_b[l])
        da = jnp.concatenate([da_g, da_v], axis=1)
        g_up = _mm_tn("mm_g_up", hf, da, chunks=N_CHIP)
        dh = _mm_nt("mm_d_hf", da, wup[l])
        return g_dn, g_up, dcw, dcb, dh

    g_dn1, g_up1, dcw1, dcb1, dh1f = ffn_bwd(1, df1, u1, a1, h1f)
    red1, tok = reduce_start("1", [5, 7], [g_up1, g_dn1])
    dx, dgn_f1, dsc2b, dsh2b, dy1, dg1b = _block_bwd("bwd_ffn1", dx, dh1f, x3, vec(ffn_norm[1]) + tok, sc2b, sh2b, y1, g1b)
    g_sg_out = _mm_tn("mm_g_sg_out", s1, dy1)
    ds1 = _mm_nt("mm_d_s", dy1, wsg_out)
    dzpre, dsg_gain, dsg_bias, dwpos, dbpos = _sg_bwd(zpre, ds1, sg_gain_full, sg_bias_full, wpos, bpos)
    red1, tok_x = reduce_cross(red1, dzpre)
    g_sg_in = _mm_tn("mm_g_sg_in", h1, dzpre, chunks=N_CHIP)
    dh1 = _mm_nt("mm_d_h1", dzpre, wsg_in)
    red2, tok = reduce_start("2", [2, 3], [g_sg_in, g_sg_out])
    dx, dgn_m1, dsc1b, dsh1b, df0, dg2 = _block_bwd("bwd_mix1", dx, dh1, x2, vec(mix_norm[1]) + tok + tok_x, sc1b, sh1b, f0, g2)
    g_dn0, g_up0, dcw0, dcb0, dh0f = ffn_bwd(0, df0, u0, a0, h0f)
    red2, tok_x = reduce_cross(red2, dh0f)
    red3, tok = reduce_start("3", [4, 6], [g_up0, g_dn0])
    dx, dgn_f0, dsc2, dsh2, dy0, dg1 = _block_bwd("bwd_ffn0", dx, dh0f, x1, vec(ffn_norm[0]) + tok + tok_x, sc2, sh2, y0, g1)
    g_out = _mm_tn("mm_g_par_out", o_cat, dy0)
    do = _mm_nt("mm_d_o", dy0, w_out)
    red3, tok_x = reduce_cross(red3, do)
    dhq, dhf, dhi, dhg, dl0, dl1, dhg_gain = _hg_bwd(proj, hg_states, do, l0 + tok_x, l1, hg_gain, H, 3 * H, H)
    dq, dk, dv = _sb_bwd(proj, do, sb_tot, H)
    dproj = jnp.concatenate([dq, dk, dv, dhq, dhf, dhi, dhg], axis=1).astype(BF16)
    g_in = _mm_tn("mm_g_par_in", h0, dproj, chunks=N_CHIP)
    red4, tok = reduce_start("4", [0, 1], [g_in, g_out])
    dh0 = _mm_nt("mm_d_h0", dproj, w_in)
    grad_x, dgn_m0, dsc1, dsh1 = _block_bwd("bwd_mix0", dx, dh0, x, vec(mix_norm[0]) + tok, sc1, sh1)
    red4, tok_x = reduce_cross(red4, grad_x)

    G_, delta, new_m, new_v = {}, {}, {}, {}

    def adam_on(nme):
        shp = W[nme].shape
        r2 = lambda a: a.reshape(-1, shp[-1])
        d_, m_, v_ = _adam("adam_" + nme, r2(W[nme]), r2(G_[nme]), r2(M[nme]), r2(V[nme]))
        delta[nme], new_m[nme], new_v[nme] = d_.reshape(shp), m_.reshape(shp), v_.reshape(shp)

    g_shards = {}
    for state in (red1, red2, red3):
        g_shards.update(reduce_finish(state, red4[4][0]))
    G_["sg_w_in"] = g_shards[2][None]
    G_["sg_w_out"] = g_shards[3][None]
    G_["ffn_up"] = jnp.stack([g_shards[4], g_shards[5]])
    G_["ffn_down"] = jnp.stack([g_shards[6], g_shards[7]])
    for nme in ["sg_w_in", "sg_w_out", "ffn_up", "ffn_down"]:
        adam_on(nme)

    dmod = jnp.concatenate([dsh1, dsc1, dg1, dsh2, dsc2, dg2, dsh1b, dsc1b, dg1b, dsh2b, dsc2b, dg2b], axis=1)
    parts = [dmod, dgn_m0, dgn_m1, dgn_f0, dgn_f1, dl0, dl1, dhg_gain, dsg_gain, dsg_bias, dwpos, dbpos,
             dcw0, dcw1, dcb0, dcb1, d_final]
    sizes = [p.size for p in parts]
    packed = _pack_rows(parts)
    packed_all, = _all_gather("gather_small_grads", [packed], after=new_v["ffn_down"])
    summed = _sum_leading("sum_small_grads", packed_all).reshape(-1)
    offs = [0]
    for s in sizes:
        offs.append(offs[-1] + s)
    red = [summed[offs[i]:offs[i + 1]] for i in range(len(parts))]
    (r_dmod, r_gm0, r_gm1, r_gf0, r_gf1, r_l0, r_l1, r_hgain, r_sgain, r_sbias, r_wpos, r_bpos,
     r_cw0, r_cw1, r_cb0, r_cb1, r_final) = red
    n_mod = sizes[0]
    dmod_all = packed_all.reshape(N_DEV, -1)[:, :n_mod].reshape(N_DEV, 2, 6 * D)

    G_["ada_b"] = r_dmod.reshape(2, 6 * D)
    G_["mix_norm"] = jnp.stack([r_gm0, r_gm1])
    G_["ffn_norm"] = jnp.stack([r_gf0, r_gf1])
    G_["hg_lb_logits"] = jnp.stack([r_l0, r_l1])
    G_["hg_out_norm"] = r_hgain.reshape(hg_out_norm.shape)
    G_["sg_v_gain"] = lax.dynamic_slice(r_sgain, (chip * n_sv,), (n_sv,)).reshape(sg_v_gain.shape)
    G_["sg_v_bias"] = lax.dynamic_slice(r_sbias, (chip * n_sv,), (n_sv,)).reshape(sg_v_bias.shape)
    G_["sg_w_pos"] = r_wpos.reshape(sg_w_pos.shape)
    G_["sg_b_pos"] = r_bpos.reshape(sg_b_pos.shape)
    cw_full = jnp.stack([r_cw0.reshape(CONV_WIDTH, F2), r_cw1.reshape(CONV_WIDTH, F2)])
    G_["ffn_conv_w"] = lax.dynamic_slice(cw_full, (0, 0, chip * F2s), (2, CONV_WIDTH, F2s))
    G_["ffn_conv_b"] = jnp.stack([r_cb0, r_cb1])
    G_["final_norm"] = r_final

    c_t = jnp.pad(c_all, ((0, HEAD - N_DEV), (0, 0))).T
    dmod_sh = lax.dynamic_slice(dmod_all.transpose(1, 0, 2), (0, 0, chip * NA), (2, N_DEV, NA))
    dmod_sh = jnp.pad(dmod_sh, ((0, 0), (0, HEAD - N_DEV), (0, 0)))
    G_["ada_w"], delta["ada_w"], new_m["ada_w"], new_v["ada_w"] = _ada_grad_adam(c_t, dmod_sh, ada_w, m_ada_w, v_ada_w)

    g_shards.update(reduce_finish(red4, G_["ada_w"]))
    G_["par_w_in"] = g_shards[0][None]
    G_["par_w_out"] = g_shards[1][None]
    for nme in ["par_w_in", "par_w_out"]:
        adam_on(nme)
    small = [n_ for n_ in names if n_ not in delta]
    pk = lambda dct: _pack_rows([dct[n_] for n_ in small])
    d_, m_, v_ = _adam("adam_small", pk(W), pk(G_), pk(M), pk(V))
    off = 0
    for n_ in small:
        sz = W[n_].size
        for dst, src in ((delta, d_), (new_m, m_), (new_v, v_)):
            dst[n_] = src.reshape(-1)[off:off + sz].reshape(W[n_].shape)
        off += sz

    return (loss, grad_x[None], *[G_[n_] for n_ in names], *[delta[n_] for n_ in names],
            *[new_m[n_] for n_ in names], *[new_v[n_] for n_ in names])
```

```python
import functools
import math

import jax
import jax.numpy as jnp
from jax import lax
from jax.experimental import pallas as pl
from jax.experimental.pallas import tpu as pltpu

F32 = jnp.float32
BF16 = jnp.bfloat16
MESH = pl.DeviceIdType.MESH
ANY = pl.BlockSpec(memory_space=pl.ANY)

NORM_EPS = 1e-6
ADAM_LR = 0.001
ADAM_B1 = 0.9
ADAM_B2 = 0.999
ADAM_EPS = 1e-08
ADAM_WD = 0.01
ADAM_STEP = 10
CONV_WIDTH = 3
HEAD = 128
HG_CHUNK = 64
SG_CHUNK = 128
N_DEV = 8
N_CHIP = 4
V7X_VMEM_LIMIT = 56 * 1024 * 1024


def _cp(*sem):
    return pltpu.CompilerParams(dimension_semantics=sem if sem else None, vmem_limit_bytes=V7X_VMEM_LIMIT)


def _pick(n, prefs):
    for p in prefs:
        if p <= n and n % p == 0:
            return p
    return n


def _iota(shape, axis):
    return lax.broadcasted_iota(jnp.int32, shape, axis)


def _rows_within(R, row_bytes, budget):
    if R * row_bytes <= budget:
        return R
    for t in (1024, 512, 256, 128, 64, 32, 16):
        if R % t == 0 and t * row_bytes <= budget:
            return t
    return _pick(R, (16, 8))


def _pack_rows(arrays):
    flat = jnp.concatenate([a.reshape(-1) for a in arrays])
    pad = (-flat.size) % (8 * HEAD)
    return jnp.pad(flat, (0, pad)).reshape(-1, HEAD)


def _dg(a, b, ca, cb):
    return lax.dot_general(a.astype(BF16), b.astype(BF16), (((ca,), (cb,)), ((), ())), preferred_element_type=F32)


@jax.custom_vjp
def mm_nn(a, b):
    return _dg(a, b, 1, 0)


mm_nn.defvjp(lambda a, b: (_dg(a, b, 1, 0), (a, b)),
             lambda r, g: (_dg(g, r[1], 1, 1), _dg(r[0], g, 0, 0)))


@jax.custom_vjp
def mm_nt(a, b):
    return _dg(a, b, 1, 1)


mm_nt.defvjp(lambda a, b: (_dg(a, b, 1, 1), (a, b)),
             lambda r, g: (_dg(g, r[1], 1, 0), _dg(g, r[0], 0, 0)))


@jax.custom_vjp
def mm_tn(a, b):
    return _dg(a, b, 0, 0)


mm_tn.defvjp(lambda a, b: (_dg(a, b, 0, 0), (a, b)),
             lambda r, g: (_dg(r[1], g, 1, 1), _dg(r[0], g, 1, 0)))


def _split(x):
    hi = x.astype(BF16)
    lo = (x - hi.astype(F32)).astype(BF16)
    return hi, lo


def _sum_right(x, m01):
    hi, lo = _split(x)
    return _dg(hi, m01, 1, 0) + _dg(lo, m01, 1, 0)


def _sum_left_impl(m01, x, ca):
    hi, lo = _split(x)
    return _dg(m01, hi, ca, 0) + _dg(m01, lo, ca, 0)


@jax.custom_vjp
def _sum_left(m01, x):
    return _sum_left_impl(m01, x, 1)


_sum_left.defvjp(lambda m, x: (_sum_left_impl(m, x, 1), m),
                 lambda m, g: (None, _sum_left_impl(m, g, 0)))


def _sigmoid(x):
    return 1.0 / (1.0 + jnp.exp(-x))


def _softplus(z):
    return jnp.maximum(z, 0.0) + jnp.log(1.0 + jnp.exp(-jnp.abs(z)))


_INV_SQRT2 = 1.0 / math.sqrt(2.0)
_INV_SQRT2PI = 1.0 / math.sqrt(2.0 * math.pi)


@jax.custom_vjp
def _gelu(x):
    return 0.5 * x * (1.0 + lax.erf(x * _INV_SQRT2))


_gelu.defvjp(lambda x: (0.5 * x * (1.0 + lax.erf(x * _INV_SQRT2)), x),
             lambda x, g: (g * (0.5 * (1.0 + lax.erf(x * _INV_SQRT2)) + x * jnp.exp(-0.5 * x * x) * _INV_SQRT2PI),))


def _rms(x, gain):
    r = lax.rsqrt(jnp.mean(x * x, axis=-1, keepdims=True) + NORM_EPS)
    return x * r * gain


def _normmod(x, gain, sc, sh):
    return _rms(x, gain) * (1.0 + sc) + sh


def _mm_call(name, a, b, out_shape, out_dtype, dims, grid, a_spec, b_spec, o_spec, acc_shape):
    nk = grid[2]

    def body(a_ref, b_ref, o_ref, *scratch):
        part = lax.dot_general(a_ref[...].astype(BF16), b_ref[...].astype(BF16), dims, preferred_element_type=F32)
        if nk == 1:
            o_ref[...] = part.astype(o_ref.dtype)
            return
        acc_ref, = scratch
        k = pl.program_id(2)

        @pl.when(k == 0)
        def _():
            acc_ref[...] = part

        @pl.when(k > 0)
        def _():
            acc_ref[...] += part

        @pl.when(k == nk - 1)
        def _():
            o_ref[...] = acc_ref[...].astype(o_ref.dtype)

    return pl.pallas_call(
        body, name=name, grid=grid, in_specs=[a_spec, b_spec], out_specs=o_spec,
        out_shape=jax.ShapeDtypeStruct(out_shape, out_dtype),
        scratch_shapes=[] if nk == 1 else [pltpu.VMEM(acc_shape, F32)],
        compiler_params=_cp("parallel", "parallel", "arbitrary"),
    )(a, b)


def _mm_nn(name, a, b, out_dtype=F32):
    M, K = a.shape
    chunked = b.ndim == 3
    Nc = b.shape[-1]
    N = Nc * (b.shape[0] if chunked else 1)
    tm = _pick(M, (1024, 512, 256, 128, 64, 32, 16, 8))
    tn = _pick(Nc, (1408, 1024, 896, 512, 256, 128))
    tk = _pick(K, (2048, 1408, 1024, 512, 256, 128))
    npc = Nc // tn
    if chunked:
        b_spec = pl.BlockSpec((None, tk, tn), lambda i, j, k: (j // npc, k, j % npc))
    else:
        b_spec = pl.BlockSpec((tk, tn), lambda i, j, k: (k, j))
    return _mm_call(name, a, b, (M, N), out_dtype, (((1,), (0,)), ((), ())), (M // tm, N // tn, K // tk),
                    pl.BlockSpec((tm, tk), lambda i, j, k: (i, k)), b_spec,
                    pl.BlockSpec((tm, tn), lambda i, j, k: (i, j)), (tm, tn))


def _mm_nt(name, a, b, out_dtype=F32):
    planar = a.ndim == 3
    M, Np = a.shape[-2:]
    N = Np * (a.shape[0] if planar else 1)
    chunked = b.ndim == 3
    Nc = b.shape[-1]
    K = b.shape[-2]
    tm = _pick(M, (1024, 512, 256, 128, 64, 32, 16, 8))
    tn = _pick(K, (1408, 1024, 512, 256, 128))
    tk = _pick(Nc, (2048, 1792, 1408, 1024, 896, 512, 256, 128))
    assert Np % tk == 0
    npc = Nc // tk
    npp = Np // tk
    if chunked:
        b_spec = pl.BlockSpec((None, tn, tk), lambda i, j, k: (k // npc, j, k % npc))
    else:
        b_spec = pl.BlockSpec((tn, tk), lambda i, j, k: (j, k))
    if planar:
        a_spec = pl.BlockSpec((None, tm, tk), lambda i, j, k: (k // npp, i, k % npp))
    else:
        a_spec = pl.BlockSpec((tm, tk), lambda i, j, k: (i, k))
    return _mm_call(name, a, b, (M, K), out_dtype, (((1,), (1,)), ((), ())), (M // tm, K // tn, N // tk),
                    a_spec, b_spec, pl.BlockSpec((tm, tn), lambda i, j, k: (i, j)), (tm, tn))


def _mm_tn(name, a, b, chunks=1, out_dtype=BF16):
    T, K = a.shape
    planar = b.ndim == 3
    Np = b.shape[-1]
    N = Np * (b.shape[0] if planar else 1)
    Nc = N // chunks
    tm = _pick(K, (1408, 1024, 512, 256, 128))
    tn = _pick(Nc, (1408, 1024, 896, 512, 256, 128))
    tk = _pick(T, (1024, 512, 256, 128))
    assert Np % tn == 0
    npc = Nc // tn
    npp = Np // tn
    if planar:
        b_spec = pl.BlockSpec((None, tk, tn), lambda i, j, k: (j // npp, k, j % npp))
    else:
        b_spec = pl.BlockSpec((tk, tn), lambda i, j, k: (k, j))
    if chunks > 1:
        shape = (chunks, K, Nc)
        o_spec = pl.BlockSpec((None, tm, tn), lambda i, j, k: (j // npc, i, j % npc))
    else:
        shape = (K, N)
        o_spec = pl.BlockSpec((tm, tn), lambda i, j, k: (i, j))
    return _mm_call(name, a, b, shape, out_dtype, (((0,), (0,)), ((), ())), (K // tm, N // tn, T // tk),
                    pl.BlockSpec((tk, tm), lambda i, j, k: (k, i)), b_spec, o_spec, (tm, tn))


def _row_tile(T):
    return _pick(T, (256, 128, 64, 32, 16, 8))


def _vec_spec(D):
    return pl.BlockSpec((1, D), lambda i: (0, 0))


def _normmod_fwd(name, x, gain, sc, sh):
    T, D = x.shape
    bt = _row_tile(T)

    def body(x_ref, g_ref, sc_ref, sh_ref, h_ref):
        h_ref[...] = _normmod(x_ref[...], g_ref[...], sc_ref[...], sh_ref[...]).astype(h_ref.dtype)

    rows = pl.BlockSpec((bt, D), lambda i: (i, 0))
    return pl.pallas_call(body, name=name, grid=(T // bt,), in_specs=[rows] + [_vec_spec(D)] * 3, out_specs=rows,
                          out_shape=jax.ShapeDtypeStruct((T, D), BF16), compiler_params=_cp("parallel"))(x, gain, sc, sh)


def _res_normmod_fwd(name, x, y, g, gain, sc, sh):
    T, D = x.shape
    bt = _row_tile(T)

    def body(x_ref, y_ref, gate_ref, g_ref, sc_ref, sh_ref, x1_ref, h_ref):
        x1 = x_ref[...] + gate_ref[...] * y_ref[...]
        x1_ref[...] = x1
        h_ref[...] = _normmod(x1, g_ref[...], sc_ref[...], sh_ref[...]).astype(h_ref.dtype)

    rows = pl.BlockSpec((bt, D), lambda i: (i, 0))
    return pl.pallas_call(body, name=name, grid=(T // bt,), in_specs=[rows, rows] + [_vec_spec(D)] * 4,
                          out_specs=[rows, rows],
                          out_shape=[jax.ShapeDtypeStruct((T, D), F32), jax.ShapeDtypeStruct((T, D), BF16)],
                          compiler_params=_cp("parallel"))(x, y, g, gain, sc, sh)


def _final_fwd_bwd(x, y, g, gain, target):
    T, D = x.shape
    bt = _row_tile(T)

    def body(x_ref, y_ref, gate_ref, g_ref, t_ref, loss_ref, dx_ref, dy_ref, dgate_ref, dgain_ref):
        i = pl.program_id(0)
        yv = y_ref[...]
        gate = gate_ref[...]
        x4 = x_ref[...] + gate * yv
        out, vjp = jax.vjp(_rms, x4, g_ref[...])
        err = out - t_ref[...]
        dx4, dgain = vjp(err * (1.0 / D))
        part = 0.5 * jnp.sum(jnp.mean(err * err, axis=-1, keepdims=True), axis=0, keepdims=True)

        @pl.when(i == 0)
        def _():
            loss_ref[...] = jnp.zeros_like(loss_ref)
            dgate_ref[...] = jnp.zeros_like(dgate_ref)
            dgain_ref[...] = jnp.zeros_like(dgain_ref)

        loss_ref[...] += jnp.broadcast_to(part, loss_ref.shape)
        dx_ref[...] = dx4
        dy_ref[...] = (gate * dx4).astype(dy_ref.dtype)
        dgate_ref[...] += jnp.sum(dx4 * yv, axis=0, keepdims=True)
        dgain_ref[...] += dgain

    rows = pl.BlockSpec((bt, D), lambda i: (i, 0))
    vec = _vec_spec(D)
    return pl.pallas_call(
        body, name="final_loss", grid=(T // bt,), in_specs=[rows, rows, vec, vec, rows],
        out_specs=[pl.BlockSpec((1, HEAD), lambda i: (0, 0)), rows, rows, vec, vec],
        out_shape=[jax.ShapeDtypeStruct((1, HEAD), F32), jax.ShapeDtypeStruct((T, D), F32),
                   jax.ShapeDtypeStruct((T, D), BF16), jax.ShapeDtypeStruct((1, D), F32),
                   jax.ShapeDtypeStruct((1, D), F32)],
        compiler_params=_cp("arbitrary"))(x, y, g, gain, target)


def _block_bwd(name, dx_out, dh, x_in, gain, sc, sh, y_prev=None, g_prev=None):
    T, D = x_in.shape
    bt = _row_tile(T)
    has_prev = y_prev is not None

    def body(*refs):
        if has_prev:
            dxo_ref, dh_ref, x_ref, g_ref, sc_ref, sh_ref, y_ref, gp_ref, dx_ref, dgain_ref, dsc_ref, dsh_ref, dy_ref, dgp_ref = refs
        else:
            dxo_ref, dh_ref, x_ref, g_ref, sc_ref, sh_ref, dx_ref, dgain_ref, dsc_ref, dsh_ref = refs
        i = pl.program_id(0)
        _, vjp = jax.vjp(_normmod, x_ref[...], g_ref[...], sc_ref[...], sh_ref[...])
        dxn, dgain, dsc, dsh = vjp(dh_ref[...])
        dx = dxo_ref[...] + dxn
        dx_ref[...] = dx

        @pl.when(i == 0)
        def _():
            dgain_ref[...] = jnp.zeros_like(dgain_ref)
            dsc_ref[...] = jnp.zeros_like(dsc_ref)
            dsh_ref[...] = jnp.zeros_like(dsh_ref)
            if has_prev:
                dgp_ref[...] = jnp.zeros_like(dgp_ref)

        dgain_ref[...] += dgain
        dsc_ref[...] += dsc
        dsh_ref[...] += dsh
        if has_prev:
            dy_ref[...] = (gp_ref[...] * dx).astype(dy_ref.dtype)
            dgp_ref[...] += jnp.sum(dx * y_ref[...], axis=0, keepdims=True)

    rows = pl.BlockSpec((bt, D), lambda i: (i, 0))
    vec = _vec_spec(D)
    ins = [dx_out, dh, x_in, gain, sc, sh]
    in_specs = [rows, rows, rows, vec, vec, vec]
    out_specs = [rows, vec, vec, vec]
    out_shape = [jax.ShapeDtypeStruct((T, D), F32)] + [jax.ShapeDtypeStruct((1, D), F32)] * 3
    if has_prev:
        ins += [y_prev, g_prev]
        in_specs += [rows, vec]
        out_specs += [rows, vec]
        out_shape += [jax.ShapeDtypeStruct((T, D), BF16), jax.ShapeDtypeStruct((1, D), F32)]
    return pl.pallas_call(body, name=name, grid=(T // bt,), in_specs=in_specs, out_specs=out_specs,
                          out_shape=out_shape, compiler_params=_cp("arbitrary"))(*ins)


def _sb_tiles(T):
    tq = _pick(T, (512, 256, 128))
    return tq, tq // HEAD


def _sb_fwd(proj, H):
    T = proj.shape[0]
    tq, nsub = _sb_tiles(T)
    scale = HEAD ** -0.5

    def body(q_ref, k_ref, v_ref, o_ref, l_ref, acc_ref):
        i = pl.program_id(1)
        q = q_ref[...].astype(BF16)
        later = (_iota((HEAD, HEAD), 0) > _iota((HEAD, HEAD), 1)).astype(BF16)
        row = _iota((tq, HEAD), 0)
        col = _iota((tq, HEAD), 1)

        def key_step(j, c, diagonal):
            off = pl.multiple_of(j * tq, tq)
            k = k_ref[pl.ds(off, tq), :].astype(BF16)
            v = v_ref[pl.ds(off, tq), :].astype(BF16)
            z = _dg(q, k, 1, 1) * scale
            ws = [None] * nsub
            for s in reversed(range(nsub)):
                zs = z[:, s * HEAD:(s + 1) * HEAD]
                sp = _softplus(zs)
                if diagonal:
                    strict = (s * HEAD + col) < row
                    lk = jnp.where(strict, -sp, 0.0)
                else:
                    lk = -sp
                w = jnp.exp(zs - sp + _sum_right(lk, later) + c)
                if diagonal:
                    w = jnp.where(strict, w, 0.0)
                ws[s] = w.astype(BF16)
                c = c + jnp.sum(lk, axis=1, keepdims=True)
            acc_ref[...] += _dg(jnp.concatenate(ws, axis=1), v, 1, 0)
            return c

        acc_ref[...] = jnp.zeros_like(acc_ref)
        c = key_step(i, jnp.zeros((tq, 1), F32), True)
        c = lax.fori_loop(0, i, lambda n, c: key_step(i - 1 - n, c, False), c)
        o_ref[...] = acc_ref[...].astype(o_ref.dtype)
        l_ref[...] = jnp.broadcast_to(c, (tq, HEAD))

    blk = pl.BlockSpec((tq, HEAD), lambda h, i: (i, h))
    return pl.pallas_call(
        body, name="sb_fwd", grid=(H, T // tq),
        in_specs=[blk, pl.BlockSpec((T, HEAD), lambda h, i: (0, H + h)), pl.BlockSpec((T, HEAD), lambda h, i: (0, 2 * H + h))],
        out_specs=[blk, blk],
        out_shape=[jax.ShapeDtypeStruct((T, H * HEAD), BF16), jax.ShapeDtypeStruct((T, H * HEAD), F32)],
        scratch_shapes=[pltpu.VMEM((tq, HEAD), F32)],
        compiler_params=_cp("parallel", "arbitrary"))(proj, proj, proj)


def _sb_bwd(proj, do, L, H):
    T = proj.shape[0]
    tq, nsub = _sb_tiles(T)
    scale = HEAD ** -0.5

    def body(q_ref, k_ref, v_ref, do_ref, l_ref, dq_ref, dk_ref, dv_ref):
        i = pl.program_id(1)

        @pl.when(i == 0)
        def _():
            dk_ref[...] = jnp.zeros_like(dk_ref)
            dv_ref[...] = jnp.zeros_like(dv_ref)

        dq_ref[...] = jnp.zeros_like(dq_ref)
        q = q_ref[...].astype(BF16)
        do_ = do_ref[...].astype(BF16)
        total = l_ref[...]
        upto = (_iota((HEAD, HEAD), 0) <= _iota((HEAD, HEAD), 1)).astype(BF16)
        before = (_iota((HEAD, HEAD), 0) < _iota((HEAD, HEAD), 1)).astype(BF16)
        row = _iota((tq, HEAD), 0)
        col = _iota((tq, HEAD), 1)

        def key_step(j, carry, diagonal):
            cp, ce = carry
            off = pl.multiple_of(j * tq, tq)
            k = k_ref[pl.ds(off, tq), :].astype(BF16)
            v = v_ref[pl.ds(off, tq), :].astype(BF16)
            z = _dg(q, k, 1, 1) * scale
            dw = _dg(do_, v, 1, 1)
            ws, dzs = [], []
            for s in range(nsub):
                zs = z[:, s * HEAD:(s + 1) * HEAD]
                sp = _softplus(zs)
                if diagonal:
                    strict = (s * HEAD + col) < row
                    lk = jnp.where(strict, -sp, 0.0)
                else:
                    lk = -sp
                tail = total - (_sum_right(lk, upto) + cp)
                w = jnp.exp(zs - sp + tail)
                if diagonal:
                    w = jnp.where(strict, w, 0.0)
                e = w * dw[:, s * HEAD:(s + 1) * HEAD]
                e_before = _sum_right(e, before) + ce
                sig = jnp.exp(zs - sp)
                dz = (e * (1.0 - sig) - e_before * sig) * scale
                if diagonal:
                    dz = jnp.where(strict, dz, 0.0)
                ws.append(w.astype(BF16))
                dzs.append(dz.astype(BF16))
                cp = cp + jnp.sum(lk, axis=1, keepdims=True)
                ce = ce + jnp.sum(e, axis=1, keepdims=True)
            w_all = jnp.concatenate(ws, axis=1)
            dz_all = jnp.concatenate(dzs, axis=1)
            dv_ref[pl.ds(off, tq), :] += _dg(w_all, do_, 0, 0)
            dk_ref[pl.ds(off, tq), :] += _dg(dz_all, q, 0, 0)
            dq_ref[...] += _dg(dz_all, k, 1, 0)
            return cp, ce

        zero = jnp.zeros((tq, 1), F32)
        carry = lax.fori_loop(0, i, lambda j, cr: key_step(j, cr, False), (zero, zero))
        key_step(i, carry, True)

    blk = pl.BlockSpec((tq, HEAD), lambda h, i: (i, h))
    full = pl.BlockSpec((T, HEAD), lambda h, i: (0, h))
    shp = jax.ShapeDtypeStruct((T, H * HEAD), F32)
    return pl.pallas_call(
        body, name="sb_bwd", grid=(H, T // tq),
        in_specs=[blk, pl.BlockSpec((T, HEAD), lambda h, i: (0, H + h)), pl.BlockSpec((T, HEAD), lambda h, i: (0, 2 * H + h)),
                  blk, blk],
        out_specs=[blk, full, full], out_shape=[shp, shp, shp],
        compiler_params=_cp("parallel", "arbitrary"))(proj, proj, proj, do, L)


def _hg_tile(q, fl, iv, g, st, l0, l1, gain):
    R = 2 * HG_CHUNK
    row = _iota((R, R), 0)
    col = _iota((R, R), 1)
    first = row < HG_CHUNK
    same = first == (col < HG_CHUNK)
    tri = (row >= col) & same
    lb = _sigmoid(l0 - l1)
    f = lb + (1.0 - lb) * _sigmoid(fl)
    logf = jnp.log(f)
    k = 1.0 - f
    qf = q * _sigmoid(q)
    G = _sum_left(tri.astype(BF16), logf)
    gl_a = jnp.sum(jnp.where(first, logf, 0.0), axis=0, keepdims=True)
    gl_b = jnp.sum(jnp.where(first, 0.0, logf), axis=0, keepdims=True)
    q_dec = qf * jnp.exp(G)
    k_inv = k * jnp.exp(-G)
    k_end = k * jnp.exp(jnp.where(first, gl_a, gl_b) - G)
    scores = jnp.where(tri, mm_nt(q_dec, k_inv), 0.0)
    o = mm_nn(scores, iv)
    o_a = mm_nt(q_dec, st)
    st_mid = st * jnp.exp(gl_a) + mm_tn(jnp.where(first, iv, 0.0), k_end)
    o_b = mm_nt(q_dec, st_mid)
    st_new = st_mid * jnp.exp(gl_b) + mm_tn(jnp.where(first, 0.0, iv), k_end)
    o = o + jnp.where(first, o_a, o_b)
    on = o * lax.rsqrt(jnp.mean(o * o, axis=-1, keepdims=True) + NORM_EPS) * gain
    return on * (g * _sigmoid(g)), st_new


def _hg_heads(H):
    return _pick(H, (8, 4, 2, 1))


def _hg_specs(H, c0, rev, nt):
    hb = _hg_heads(H)
    w = hb * HEAD

    def at(base):
        if rev:
            return pl.BlockSpec((HEAD, w), lambda h, i: (nt - 1 - i, base // hb + h))
        return pl.BlockSpec((HEAD, w), lambda h, i: (i, base // hb + h))
    return [at(c0), at(c0 + H), at(c0 + 2 * H), at(c0 + 3 * H)]


def _hg_fwd(proj, l0, l1, gain, H, c0):
    T = proj.shape[0]
    nt = T // HEAD
    hb = _hg_heads(H)
    w = hb * HEAD

    def body(q_ref, f_ref, i_ref, g_ref, l0_ref, l1_ref, gain_ref, o_ref, st_out_ref, st_ref):
        @pl.when(pl.program_id(1) == 0)
        def _():
            st_ref[...] = jnp.zeros_like(st_ref)

        for j in range(hb):
            s = slice(j * HEAD, (j + 1) * HEAD)
            st = st_ref[j]
            st_out_ref[j] = st
            out, st_new = _hg_tile(q_ref[:, s], f_ref[:, s], i_ref[:, s], g_ref[:, s], st, l0_ref[:, s], l1_ref[:, s],
                                   gain_ref[:, s])
            o_ref[:, s] = out.astype(o_ref.dtype)
            st_ref[j] = st_new

    vec = pl.BlockSpec((1, w), lambda h, i: (0, h))
    return pl.pallas_call(
        body, name="hg_fwd", grid=(H // hb, nt), in_specs=_hg_specs(H, c0, False, nt) + [vec, vec, vec],
        out_specs=[pl.BlockSpec((HEAD, w), lambda h, i: (i, h)),
                   pl.BlockSpec((hb, None, HEAD, HEAD), lambda h, i: (h, i, 0, 0))],
        out_shape=[jax.ShapeDtypeStruct((T, H * HEAD), BF16), jax.ShapeDtypeStruct((H, nt, HEAD, HEAD), F32)],
        scratch_shapes=[pltpu.VMEM((hb, HEAD, HEAD), F32)],
        compiler_params=_cp("parallel", "arbitrary"))(proj, proj, proj, proj, l0, l1, gain)


def _hg_bwd(proj, states, do, l0, l1, gain, H, c0, do_c0):
    T = proj.shape[0]
    nt = T // HEAD
    hb = _hg_heads(H)
    w = hb * HEAD

    def body(q_ref, f_ref, i_ref, g_ref, st_in_ref, do_ref, l0_ref, l1_ref, gain_ref,
             dq_ref, df_ref, di_ref, dg_ref, dl0_ref, dl1_ref, dgain_ref, dst_ref):
        @pl.when(pl.program_id(1) == 0)
        def _():
            dst_ref[...] = jnp.zeros_like(dst_ref)
            dl0_ref[...] = jnp.zeros_like(dl0_ref)
            dl1_ref[...] = jnp.zeros_like(dl1_ref)
            dgain_ref[...] = jnp.zeros_like(dgain_ref)

        for j in range(hb):
            s = slice(j * HEAD, (j + 1) * HEAD)
            _, vjp = jax.vjp(_hg_tile, q_ref[:, s], f_ref[:, s], i_ref[:, s], g_ref[:, s], st_in_ref[j],
                             l0_ref[:, s], l1_ref[:, s], gain_ref[:, s])
            dq, df, di, dg, dst, dl0, dl1, dgain = vjp((do_ref[:, s], dst_ref[j]))
            dq_ref[:, s] = dq
            df_ref[:, s] = df
            di_ref[:, s] = di
            dg_ref[:, s] = dg
            dst_ref[j] = dst
            dl0_ref[:, s] += dl0
            dl1_ref[:, s] += dl1
            dgain_ref[:, s] += dgain

    vec = pl.BlockSpec((1, w), lambda h, i: (0, h))
    rblk = pl.BlockSpec((HEAD, w), lambda h, i: (nt - 1 - i, h))
    shp = jax.ShapeDtypeStruct((T, H * HEAD), F32)
    vshp = jax.ShapeDtypeStruct((1, H * HEAD), F32)
    return pl.pallas_call(
        body, name="hg_bwd", grid=(H // hb, nt),
        in_specs=_hg_specs(H, c0, True, nt) + [
            pl.BlockSpec((hb, None, HEAD, HEAD), lambda h, i: (h, nt - 1 - i, 0, 0)),
            pl.BlockSpec((HEAD, w), lambda h, i: (nt - 1 - i, do_c0 // hb + h)), vec, vec, vec],
        out_specs=[rblk, rblk, rblk, rblk, vec, vec, vec],
        out_shape=[shp, shp, shp, shp, vshp, vshp, vshp],
        scratch_shapes=[pltpu.VMEM((hb, HEAD, HEAD), F32)],
        compiler_params=_cp("parallel", "arbitrary"))(proj, proj, proj, proj, states, do, l0, l1, gain)


def _sg_chunk(u_parts, v_parts, gains, biases, wpos, bpos):
    W = sum(p.shape[1] for p in v_parts)
    C = v_parts[0].shape[0]
    v = [_gelu(p) for p in v_parts]
    mu = sum(jnp.sum(p, axis=-1, keepdims=True) for p in v) * (1.0 / W)
    xc = [p - mu for p in v]
    r = lax.rsqrt(sum(jnp.sum(p * p, axis=-1, keepdims=True) for p in xc) * (1.0 / W) + NORM_EPS)
    causal = _iota((C, C), 0) >= _iota((C, C), 1)
    out = []
    for up, p, gn, bs, w, b in zip(u_parts, xc, gains, biases, wpos, bpos):
        vn = p * r * gn + bs
        mixed = mm_nn(jnp.where(causal, w, 0.0), vn) + b
        out.append(_gelu(up) * mixed)
    return out


def _sg_fwd(zpre, vgain, vbias, wpos, bpos):
    T, W2 = zpre.shape
    W = W2 // 2
    G = wpos.shape[0]
    cg = W // G
    C = SG_CHUNK

    def body(z_ref, gn_ref, bs_ref, w_ref, b_ref, s_ref):
        sl = [slice(g * cg, (g + 1) * cg) for g in range(G)]
        out = _sg_chunk([z_ref[:, s] for s in sl], [z_ref[:, W + s.start:W + s.stop] for s in sl],
                        [gn_ref[:, s] for s in sl], [bs_ref[:, s] for s in sl],
                        [w_ref[g] for g in range(G)], [b_ref[g] for g in range(G)])
        for s, o in zip(sl, out):
            s_ref[:, s] = o.astype(s_ref.dtype)

    return pl.pallas_call(
        body, name="sg_fwd", grid=(T // C,),
        in_specs=[pl.BlockSpec((C, W2), lambda i: (i, 0)), _vec_spec(W), _vec_spec(W),
                  pl.BlockSpec((G, C, C), lambda i: (0, 0, 0)), pl.BlockSpec((G, C, 1), lambda i: (0, 0, 0))],
        out_specs=pl.BlockSpec((C, W), lambda i: (i, 0)),
        out_shape=jax.ShapeDtypeStruct((T, W), BF16), compiler_params=_cp("parallel"))(zpre, vgain, vbias, wpos, bpos)


def _sg_bwd(zpre, ds, vgain, vbias, wpos, bpos):
    T, W2 = zpre.shape
    W = W2 // 2
    G = wpos.shape[0]
    cg = W // G
    C = SG_CHUNK

    def body(z_ref, ds_ref, gn_ref, bs_ref, w_ref, b_ref, dz_ref, dgn_ref, dbs_ref, dw_ref, db_ref):
        @pl.when(pl.program_id(0) == 0)
        def _():
            dgn_ref[...] = jnp.zeros_like(dgn_ref)
            dbs_ref[...] = jnp.zeros_like(dbs_ref)
            dw_ref[...] = jnp.zeros_like(dw_ref)
            db_ref[...] = jnp.zeros_like(db_ref)

        sl = [slice(g * cg, (g + 1) * cg) for g in range(G)]
        _, vjp = jax.vjp(_sg_chunk, [z_ref[:, s] for s in sl], [z_ref[:, W + s.start:W + s.stop] for s in sl],
                         [gn_ref[:, s] for s in sl], [bs_ref[:, s] for s in sl],
                         [w_ref[g] for g in range(G)], [b_ref[g] for g in range(G)])
        du, dv, dgn, dbs, dw, db = vjp([ds_ref[:, s] for s in sl])
        for g, s in enumerate(sl):
            dz_ref[:, s] = du[g].astype(dz_ref.dtype)
            dz_ref[:, W + s.start:W + s.stop] = dv[g].astype(dz_ref.dtype)
            dgn_ref[:, s] += dgn[g]
            dbs_ref[:, s] += dbs[g]
            dw_ref[g] += dw[g]
            db_ref[g] += db[g]

    wspec = pl.BlockSpec((G, C, C), lambda i: (0, 0, 0))
    bspec = pl.BlockSpec((G, C, 1), lambda i: (0, 0, 0))
    return pl.pallas_call(
        body, name="sg_bwd", grid=(T // C,),
        in_specs=[pl.BlockSpec((C, W2), lambda i: (i, 0)), pl.BlockSpec((C, W), lambda i: (i, 0)),
                  _vec_spec(W), _vec_spec(W), wspec, bspec],
        out_specs=[pl.BlockSpec((C, W2), lambda i: (i, 0)), _vec_spec(W), _vec_spec(W), wspec, bspec],
        out_shape=[jax.ShapeDtypeStruct((T, W2), BF16), jax.ShapeDtypeStruct((1, W), F32),
                   jax.ShapeDtypeStruct((1, W), F32), jax.ShapeDtypeStruct((G, C, C), F32),
                   jax.ShapeDtypeStruct((G, C, 1), F32)],
        compiler_params=_cp("arbitrary"))(zpre, ds, vgain, vbias, wpos, bpos)


def _conv_tiles(T, F):
    return _pick(T, (512, 256, 128, 64, 32, 16, 8)), _pick(F, (512, 256, 128))


def _shift_down(cur, prev8, n, first_tile):
    bt = cur.shape[0]
    r = pltpu.roll(cur, n, 0)
    p = pltpu.roll(prev8, n, 0)
    p = jnp.where(first_tile, 0.0, p)
    head = jnp.where(_iota(p.shape, 0) < n, p, r[:8])
    return jnp.concatenate([head, r[8:]], axis=0) if bt > 8 else head


def _shift_up(cur, next8, n, last_tile):
    bt = cur.shape[0]
    r = pltpu.roll(cur, bt - n, 0)
    p = pltpu.roll(next8, 8 - n, 0)
    p = jnp.where(last_tile, 0.0, p)
    tail = jnp.where(_iota(p.shape, 0) >= 8 - n, p, r[bt - 8:])
    return jnp.concatenate([r[:bt - 8], tail], axis=0) if bt > 8 else tail


def _conv_apply(cur, prev8, w_ref, b, first_tile):
    return (b + w_ref[0:1, :] * _shift_down(cur, prev8, 2, first_tile)
            + w_ref[1:2, :] * _shift_down(cur, prev8, 1, first_tile) + w_ref[2:3, :] * cur)


def _conv_fwd(name, a, w, b):
    T, F2 = a.shape
    F = F2 // 2
    bt, cw = _conv_tiles(T, F)
    nf = F // cw
    r8 = bt // 8

    def body(g_ref, gp_ref, v_ref, vp_ref, wg_ref, wv_ref, bg_ref, bv_ref, u_ref):
        first = pl.program_id(0) == 0
        gate = _conv_apply(g_ref[...], gp_ref[...], wg_ref, bg_ref[...], first)
        val = _conv_apply(v_ref[...], vp_ref[...], wv_ref, bv_ref[...], first)
        u_ref[...] = (gate * _sigmoid(gate) * val).astype(u_ref.dtype)

    def cur(off):
        return pl.BlockSpec((bt, cw), lambda i, j: (i, j + off))

    def prev(off):
        return pl.BlockSpec((8, cw), lambda i, j: (jnp.maximum(i * r8 - 1, 0), j + off))

    def vec(rows, off):
        return pl.BlockSpec((rows, cw), lambda i, j: (0, j + off))

    return pl.pallas_call(
        body, name=name, grid=(T // bt, nf),
        in_specs=[cur(0), prev(0), cur(nf), prev(nf), vec(3, 0), vec(3, nf), vec(1, 0), vec(1, nf)],
        out_specs=pl.BlockSpec((bt, cw), lambda i, j: (i, j)),
        out_shape=jax.ShapeDtypeStruct((T, F), BF16),
        compiler_params=_cp("parallel", "parallel"))(a, a, a, a, w, w, b, b)


def _conv_bwd(name, a, du, w, b):
    T, F2 = a.shape
    F = F2 // 2
    bt, cw = _conv_tiles(T, F)
    nf = F // cw
    r8 = bt // 8
    last_blk = T // 8 - 1

    def body(g_ref, gp_ref, gn_ref, v_ref, vp_ref, vn_ref, du_ref, dun_ref, wg_ref, wv_ref, bg_ref, bv_ref,
             da_ref, dwg_ref, dwv_ref, dbg_ref, dbv_ref):
        i = pl.program_id(1)
        first = i == 0
        last = i == pl.num_programs(1) - 1

        def taps(cur, prev8, at_start):
            return _shift_down(cur, prev8, 2, at_start), _shift_down(cur, prev8, 1, at_start), cur

        def conv(t, w_ref, b_ref):
            return b_ref[...] + w_ref[0:1, :] * t[0] + w_ref[1:2, :] * t[1] + w_ref[2:3, :] * t[2]

        def act_bwd(gate, val, du_):
            sg = _sigmoid(gate)
            return du_ * val * (sg * (1.0 + gate * (1.0 - sg))), du_ * gate * sg

        g_cur, v_cur = g_ref[...], v_ref[...]
        tg = taps(g_cur, gp_ref[...], first)
        tv = taps(v_cur, vp_ref[...], first)
        dg, dv = act_bwd(conv(tg, wg_ref, bg_ref), conv(tv, wv_ref, bv_ref), du_ref[...])
        tgn = taps(gn_ref[...], g_cur[bt - 8:, :], False)
        tvn = taps(vn_ref[...], v_cur[bt - 8:, :], False)
        dgn, dvn = act_bwd(conv(tgn, wg_ref, bg_ref), conv(tvn, wv_ref, bv_ref), dun_ref[...])

        def conv_t(d, dn, w_ref):
            return w_ref[2:3, :] * d + w_ref[1:2, :] * _shift_up(d, dn, 1, last) + w_ref[0:1, :] * _shift_up(d, dn, 2, last)

        da_ref[0] = conv_t(dg, dgn, wg_ref).astype(da_ref.dtype)
        da_ref[1] = conv_t(dv, dvn, wv_ref).astype(da_ref.dtype)

        @pl.when(first)
        def _():
            dwg_ref[...] = jnp.zeros_like(dwg_ref)
            dwv_ref[...] = jnp.zeros_like(dwv_ref)
            dbg_ref[...] = jnp.zeros_like(dbg_ref)
            dbv_ref[...] = jnp.zeros_like(dbv_ref)

        for t in range(CONV_WIDTH):
            dwg_ref[t:t + 1, :] += jnp.sum(dg * tg[t], axis=0, keepdims=True)
            dwv_ref[t:t + 1, :] += jnp.sum(dv * tv[t], axis=0, keepdims=True)
        dbg_ref[...] += jnp.sum(dg, axis=0, keepdims=True)
        dbv_ref[...] += jnp.sum(dv, axis=0, keepdims=True)

    def cur(off):
        return pl.BlockSpec((bt, cw), lambda j, i: (i, j + off))

    def prev(off):
        return pl.BlockSpec((8, cw), lambda j, i: (jnp.maximum(i * r8 - 1, 0), j + off))

    def nxt(off):
        return pl.BlockSpec((8, cw), lambda j, i: (jnp.minimum((i + 1) * r8, last_blk), j + off))

    def vec(rows, off):
        return pl.BlockSpec((rows, cw), lambda j, i: (0, j + off))

    da, dwg, dwv, dbg, dbv = pl.pallas_call(
        body, name=name, grid=(nf, T // bt),
        in_specs=[cur(0), prev(0), nxt(0), cur(nf), prev(nf), nxt(nf), cur(0), nxt(0),
                  vec(3, 0), vec(3, nf), vec(1, 0), vec(1, nf)],
        out_specs=[pl.BlockSpec((2, bt, cw), lambda j, i: (0, i, j)), vec(3, 0), vec(3, 0), vec(1, 0), vec(1, 0)],
        out_shape=[jax.ShapeDtypeStruct((2, T, F), BF16), jax.ShapeDtypeStruct((3, F), F32), jax.ShapeDtypeStruct((3, F), F32),
                   jax.ShapeDtypeStruct((1, F), F32), jax.ShapeDtypeStruct((1, F), F32)],
        compiler_params=_cp("parallel", "arbitrary"))(a, a, a, a, a, a, du, du, w, w, b, b)
    return da, jnp.concatenate([dwg, dwv], axis=1), jnp.concatenate([dbg, dbv], axis=1)


def _ada_fwd(c_all, ada_w, ada_b):
    R, D = c_all.shape
    L, _, Ns = ada_w.shape
    tn = _pick(Ns, (512, 256, 128))

    def body(c_ref, w_ref, b_ref, o_ref):
        cv = c_ref[...]
        cond = cv * _sigmoid(cv)
        o_ref[...] = _dg(cond, w_ref[...], 1, 0) + b_ref[...]

    return pl.pallas_call(
        body, name="ada_fwd", grid=(L, Ns // tn),
        in_specs=[pl.BlockSpec((R, D), lambda l, j: (0, 0)), pl.BlockSpec((None, D, tn), lambda l, j: (l, 0, j)),
                  pl.BlockSpec((None, 1, tn), lambda l, j: (l, 0, j))],
        out_specs=pl.BlockSpec((None, R, tn), lambda l, j: (l, 0, j)),
        out_shape=jax.ShapeDtypeStruct((L, R, Ns), F32), compiler_params=_cp("parallel", "parallel"))(c_all, ada_w, ada_b)


def _adam_math(w, g, m, v):
    m2 = ADAM_B1 * m + (1.0 - ADAM_B1) * g
    v2 = ADAM_B2 * v + (1.0 - ADAM_B2) * (g * g)
    m_hat = m2 / (1.0 - ADAM_B1 ** ADAM_STEP)
    v_hat = v2 / (1.0 - ADAM_B2 ** ADAM_STEP)
    delta = -ADAM_LR * (m_hat / (jnp.sqrt(v_hat) + ADAM_EPS) + ADAM_WD * w)
    return delta, m2, v2


def _ada_grad_adam(c_all_t, dmod, w, m, v):
    D, R = c_all_t.shape
    L, _, Ns = dmod.shape
    tr = _rows_within(D, Ns * 4, 1 << 20)

    def body(c_ref, d_ref, w_ref, m_ref, v_ref, g_ref, dl_ref, m2_ref, v2_ref):
        cv = c_ref[...]
        g = _dg(cv * _sigmoid(cv), d_ref[...], 1, 0)
        g_ref[...] = g
        dl_ref[...], m2_ref[...], v2_ref[...] = _adam_math(w_ref[...], g, m_ref[...], v_ref[...])

    big = pl.BlockSpec((None, tr, Ns), lambda l, i: (l, i, 0))
    shp = jax.ShapeDtypeStruct((L, D, Ns), F32)
    return pl.pallas_call(
        body, name="ada_grad_adam", grid=(L, D // tr),
        in_specs=[pl.BlockSpec((tr, R), lambda l, i: (i, 0)), pl.BlockSpec((None, R, Ns), lambda l, i: (l, 0, 0)), big, big, big],
        out_specs=[big] * 4, out_shape=[shp] * 4, compiler_params=_cp("parallel", "parallel"))(c_all_t, dmod, w, m, v)


def _adam(name, w, g, m, v):
    R, C = w.shape
    tr = _rows_within(R, C * 4, 1 << 21)

    def body(w_ref, g_ref, m_ref, v_ref, dl_ref, m2_ref, v2_ref):
        dl_ref[...], m2_ref[...], v2_ref[...] = _adam_math(w_ref[...], g_ref[...], m_ref[...], v_ref[...])

    blk = pl.BlockSpec((tr, C), lambda i: (i, 0))
    shp = jax.ShapeDtypeStruct((R, C), F32)
    return pl.pallas_call(body, name=name, grid=(R // tr,), in_specs=[blk] * 4, out_specs=[blk] * 3,
                          out_shape=[shp] * 3, compiler_params=_cp("parallel"))(w, g, m, v)


def _cast_into_rows(name, w, chip, after=None):
    _, R, C = w.shape
    tr = _rows_within(R, C * 4, 1 << 22)
    extra = [] if after is None else [after]

    def body(chip_ref, w_ref, *rest):
        o_ref = rest[-1]
        o_ref[...] = w_ref[...].astype(BF16)

    grid_spec = pltpu.PrefetchScalarGridSpec(
        num_scalar_prefetch=1, grid=(2, R // tr),
        in_specs=[pl.BlockSpec((None, tr, C), lambda h, i, s: (h, i, 0))] + [ANY] * len(extra),
        out_specs=pl.BlockSpec((None, tr, C), lambda h, i, s: (2 * s[0] + h, i, 0)))
    return pl.pallas_call(body, name=name, grid_spec=grid_spec, out_shape=jax.ShapeDtypeStruct((N_DEV, R, C), BF16),
                          compiler_params=_cp("arbitrary", "arbitrary"))(chip.reshape(1).astype(jnp.int32), w, *extra)


def _add_pairs(name, eight, from_sib, c):
    _, R, C = from_sib.shape
    tr = _rows_within(R, C * 2, 1 << 21)

    def body(c_ref, a_ref, b_ref, o_ref):
        o_ref[...] = (a_ref[...].astype(F32) + b_ref[...].astype(F32)).astype(o_ref.dtype)

    blk = pl.BlockSpec((None, tr, C), lambda j, i, s: (j, i, 0))
    grid_spec = pltpu.PrefetchScalarGridSpec(
        num_scalar_prefetch=1, grid=(4, R // tr),
        in_specs=[pl.BlockSpec((None, tr, C), lambda j, i, s: (2 * j + s[0], i, 0)), blk], out_specs=blk)
    return pl.pallas_call(body, name=name, grid_spec=grid_spec, out_shape=jax.ShapeDtypeStruct(from_sib.shape, BF16),
                          compiler_params=_cp("arbitrary", "arbitrary"))(c.reshape(1).astype(jnp.int32), eight, from_sib)


def _sum_into_pair(name, own, landed, slot, chip):
    n, R, C = landed.shape
    tr = _rows_within(R, (n + 1) * C * landed.dtype.itemsize, 1 << 23)

    def body(idx_ref, own_ref, x_ref, o_ref):
        mine = idx_ref[1]
        acc = None
        for j in range(n):
            part = jnp.where(mine == j, own_ref[...], x_ref[j]).astype(F32)
            acc = part if acc is None else acc + part
        o_ref[...] = acc

    grid_spec = pltpu.PrefetchScalarGridSpec(
        num_scalar_prefetch=1, grid=(R // tr,),
        in_specs=[pl.BlockSpec((None, tr, C), lambda i, s: (s[1], i, 0)), pl.BlockSpec((n, tr, C), lambda i, s: (0, i, 0))],
        out_specs=pl.BlockSpec((None, tr, C), lambda i, s: (s[0], i, 0)))
    idx = jnp.stack([slot, chip]).astype(jnp.int32)
    return pl.pallas_call(body, name=name, grid_spec=grid_spec, out_shape=jax.ShapeDtypeStruct((2, R, C), F32),
                          compiler_params=_cp("arbitrary"))(idx, own, landed)


def _sum_leading(name, a, out_dtype=F32):
    n, R, C = a.shape
    tr = _rows_within(R, n * C * a.dtype.itemsize, 1 << 23)

    def body(a_ref, o_ref):
        acc = a_ref[0].astype(F32)
        for j in range(1, n):
            acc = acc + a_ref[j].astype(F32)
        o_ref[...] = acc.astype(o_ref.dtype)

    return pl.pallas_call(body, name=name, grid=(R // tr,), in_specs=[pl.BlockSpec((n, tr, C), lambda i: (0, i, 0))],
                          out_specs=pl.BlockSpec((tr, C), lambda i: (i, 0)),
                          out_shape=jax.ShapeDtypeStruct((R, C), out_dtype), compiler_params=_cp("parallel"))(a)


def _place():
    return lax.axis_index("x"), lax.axis_index("y"), lax.axis_index("c")


def _all_gather(name, blocks, halves=False, after=None):
    n = len(blocks)
    shapes = [b.shape[1:] if halves else b.shape for b in blocks]
    extra = [] if after is None else [after]

    def body(*refs):
        ins, outs = refs[:n], refs[n + len(extra):2 * n + len(extra)]
        send_sems, recv_sems, local_sems = refs[2 * n + len(extra):]
        x, y, c = _place()
        me, sibling = (x, y, c), (x, y, 1 - c)
        chips = [(1 - x, y), (x, 1 - y), (1 - x, 1 - y)]

        def rows(a, px, py, pc):
            return outs[a].at[4 * px + 2 * py + pc]

        def copy(a, k, block, to, src=None):
            return pltpu.make_async_remote_copy(
                src_ref=rows(a, *block) if src is None else src, dst_ref=rows(a, *block),
                send_sem=send_sems.at[7 * a + k], recv_sem=recv_sems.at[7 * a + k],
                device_id=to, device_id_type=MESH)

        started = []
        mine = []
        for a in range(n):
            src = ins[a].at[c] if halves else ins[a]
            mine.append(pltpu.make_async_copy(src, rows(a, *me), local_sems.at[a]))
            mine[-1].start()
            first = [copy(a, 0, me, sibling, src=src)]
            first += [copy(a, 1 + j, me, (*chip, c), src=src) for j, chip in enumerate(chips)]
            for cp in first:
                cp.start()
            started += first
        for j, chip in enumerate(chips):
            for a in range(n):
                copy(a, 1 + j, (*chip, c), me).wait_recv()
                passed = copy(a, 4 + j, (*chip, c), sibling)
                passed.start()
                started.append(passed)
        for a in range(n):
            copy(a, 0, sibling, me).wait_recv()
            for j, chip in enumerate(chips):
                copy(a, 4 + j, (*chip, 1 - c), me).wait_recv()
        for cp in started:
            cp.wait_send()
        for cp in mine:
            cp.wait()

    return pl.pallas_call(
        body, name=name, in_specs=[ANY] * (n + len(extra)), out_specs=[ANY] * n,
        out_shape=[jax.ShapeDtypeStruct((N_DEV,) + tuple(s), b.dtype) for s, b in zip(shapes, blocks)],
        scratch_shapes=[pltpu.SemaphoreType.DMA((7 * n,)), pltpu.SemaphoreType.DMA((7 * n,)),
                        pltpu.SemaphoreType.DMA((n,))],
    )(*blocks, *extra)


def _share_halves(name, arrays):
    n = len(arrays)

    def body(*refs):
        ins, outs = refs[:n], refs[n:2 * n]
        send_sems, recv_sems = refs[2 * n:]
        x, y, c = _place()
        started = []
        for a in range(n):
            cp = pltpu.make_async_remote_copy(src_ref=ins[a].at[c], dst_ref=outs[a].at[c], send_sem=send_sems.at[a],
                                              recv_sem=recv_sems.at[a], device_id=(x, y, 1 - c), device_id_type=MESH)
            cp.start()
            started.append(cp)
        for a in range(n):
            started[a].wait_send()
            pltpu.make_async_remote_copy(src_ref=ins[a].at[1 - c], dst_ref=outs[a].at[1 - c], send_sem=send_sems.at[a],
                                         recv_sem=recv_sems.at[a], device_id=(x, y, 1 - c), device_id_type=MESH).wait_recv()

    return pl.pallas_call(
        body, name=name, in_specs=[ANY] * n, out_specs=[ANY] * n,
        out_shape=[jax.ShapeDtypeStruct(a.shape, a.dtype) for a in arrays],
        input_output_aliases={a: a for a in range(n)},
        scratch_shapes=[pltpu.SemaphoreType.DMA((n,)), pltpu.SemaphoreType.DMA((n,))],
    )(*arrays)


HBM = pl.BlockSpec(memory_space=pltpu.HBM)
SEM = pl.BlockSpec(memory_space=pltpu.SEMAPHORE)
EFFECT = pltpu.SideEffectType.DATAFLOW_SIDE_EFFECTING


COPIES_PER_ARRAY = {"rows": 3, "fill": 3, "parts": 3, "halves": 4}


def _chip_copies(kind, srcs, dsts, send_sems, recv_sems):
    x, y, c = _place()
    mine = 2 * x + y
    per = COPIES_PER_ARRAY[kind]
    sends, arrivals = [], []
    for a in range(len(srcs)):
        if kind == "halves":
            for j in range(N_CHIP):
                cp = pltpu.make_async_remote_copy(
                    src_ref=srcs[a].at[2 * j + 1 - c], dst_ref=dsts[a].at[j], send_sem=send_sems.at[per * a + j],
                    recv_sem=recv_sems.at[per * a + j], device_id=(x, y, 1 - c), device_id_type=MESH)
                sends.append(cp)
                arrivals.append(cp)
            continue
        for k, (px, py) in enumerate([(1 - x, y), (x, 1 - y), (1 - x, 1 - y)]):
            other = 2 * px + py
            if kind == "fill":
                cp = dict(send_sem=send_sems.at[per * a + k], recv_sem=recv_sems.at[per * a + k], device_id=(x, y, 1 - c),
                          device_id_type=MESH)
                sends.append(pltpu.make_async_remote_copy(src_ref=srcs[a].at[2 * other + c], dst_ref=dsts[a].at[2 * other + c], **cp))
                arrivals.append(pltpu.make_async_remote_copy(src_ref=srcs[a].at[2 * other + c],
                                                             dst_ref=dsts[a].at[2 * other + 1 - c], **cp))
                continue
            if kind == "rows":
                src, dst, lands = srcs[a].at[2 * mine + c], dsts[a].at[2 * mine + c], dsts[a].at[2 * other + c]
            else:
                src, dst, lands = srcs[a].at[other], dsts[a].at[mine], dsts[a].at[other]
            sem = dict(send_sem=send_sems.at[per * a + k], recv_sem=recv_sems.at[per * a + k], device_id=(px, py, c),
                       device_id_type=MESH)
            sends.append(pltpu.make_async_remote_copy(src_ref=src, dst_ref=dst, **sem))
            arrivals.append(pltpu.make_async_remote_copy(src_ref=src, dst_ref=lands, **sem))
    return sends, arrivals


def _chips_start(name, kind, srcs, dsts=None, after=None):
    n = len(srcs)
    bufs = list(srcs) + (list(dsts) if dsts is not None else [])
    nb = len(bufs)
    extra = [] if after is None else [after]

    def body(*refs):
        ins = refs[:nb]
        send_sems, recv_sems = refs[nb + len(extra)], refs[nb + len(extra) + 1]
        token = refs[-1]
        sends, _ = _chip_copies(kind, ins[:n], ins[n:] if dsts is not None else ins[:n], send_sems, recv_sems)
        for cp in sends:
            cp.start()
        token[...] = jnp.zeros_like(token)

    out = pl.pallas_call(
        body, name=name,
        out_shape=(pltpu.SemaphoreType.DMA((COPIES_PER_ARRAY[kind] * n,)), pltpu.SemaphoreType.DMA((COPIES_PER_ARRAY[kind] * n,)),
                   *[pltpu.HBM(b.shape, b.dtype) for b in bufs], jax.ShapeDtypeStruct((8, HEAD), F32)),
        in_specs=(HBM,) * nb + (ANY,) * len(extra),
        out_specs=(SEM, SEM) + (HBM,) * nb + (pl.BlockSpec(memory_space=pltpu.VMEM),),
        input_output_aliases={i: 2 + i for i in range(nb)},
        compiler_params=pltpu.CompilerParams(has_side_effects=EFFECT),
    )(*[pltpu.with_memory_space_constraint(b, pltpu.HBM) for b in bufs], *extra)
    return out[0], out[1], list(out[2:2 + nb]), out[-1]


def _chips_wait(name, kind, n, send_sems, recv_sems, bufs, after):
    nb = len(bufs)

    def body(*refs):
        ins = refs[:nb]
        s_sems, r_sems = refs[nb], refs[nb + 1]
        sends, arrivals = _chip_copies(kind, ins[:n], ins[n:] if nb > n else ins[:n], s_sems, r_sems)
        for cp in sends:
            cp.wait_send()
        for cp in arrivals:
            cp.wait_recv()

    return list(pl.pallas_call(
        body, name=name, out_shape=tuple(pltpu.HBM(b.shape, b.dtype) for b in bufs),
        in_specs=(HBM,) * nb + (SEM, SEM, pl.BlockSpec(memory_space=pl.ANY)), out_specs=(HBM,) * nb,
        input_output_aliases={i: i for i in range(nb)},
        compiler_params=pltpu.CompilerParams(has_side_effects=EFFECT),
    )(*bufs, send_sems, recv_sems, after))


def _fill_from_sibling(name, arrays):
    n = len(arrays)

    def body(*refs):
        ins, outs = refs[:n], refs[n:2 * n]
        send_sems, recv_sems = refs[2 * n:]
        x, y, c = _place()
        sends, arrivals = [], []
        for a in range(n):
            for k, (px, py) in enumerate([(1 - x, y), (x, 1 - y), (1 - x, 1 - y)]):
                sem = dict(send_sem=send_sems.at[3 * a + k], recv_sem=recv_sems.at[3 * a + k], device_id=(x, y, 1 - c),
                           device_id_type=MESH)
                row = 2 * (2 * px + py)
                sends.append(pltpu.make_async_remote_copy(src_ref=ins[a].at[row + c], dst_ref=outs[a].at[row + c], **sem))
                arrivals.append(pltpu.make_async_remote_copy(src_ref=ins[a].at[row + c], dst_ref=outs[a].at[row + 1 - c], **sem))
        for cp in sends:
            cp.start()
        for cp in sends:
            cp.wait_send()
        for cp in arrivals:
            cp.wait_recv()

    return pl.pallas_call(
        body, name=name, in_specs=[ANY] * n, out_specs=[ANY] * n,
        out_shape=[jax.ShapeDtypeStruct(a.shape, a.dtype) for a in arrays],
        input_output_aliases={a: a for a in range(n)},
        scratch_shapes=[pltpu.SemaphoreType.DMA((3 * n,)), pltpu.SemaphoreType.DMA((3 * n,))],
    )(*arrays)


def kernel(x, c, ada_w, ada_b, mix_norm, ffn_norm, par_w_in, par_w_out, hg_lb_logits, hg_out_norm, sg_w_in, sg_v_gain, sg_v_bias, sg_w_pos, sg_b_pos, sg_w_out, ffn_up, ffn_conv_w, ffn_conv_b, ffn_down, final_norm, loss_target, m_ada_w, m_ada_b, m_mix_norm, m_ffn_norm, m_par_w_in, m_par_w_out, m_hg_lb_logits, m_hg_out_norm, m_sg_w_in, m_sg_v_gain, m_sg_v_bias, m_sg_w_pos, m_sg_b_pos, m_sg_w_out, m_ffn_up, m_ffn_conv_w, m_ffn_conv_b, m_ffn_down, m_final_norm, v_ada_w, v_ada_b, v_mix_norm, v_ffn_norm, v_par_w_in, v_par_w_out, v_hg_lb_logits, v_hg_out_norm, v_sg_w_in, v_sg_v_gain, v_sg_v_bias, v_sg_w_pos, v_sg_b_pos, v_sg_w_out, v_ffn_up, v_ffn_conv_w, v_ffn_conv_b, v_ffn_down, v_final_norm):
    names = ["ada_w", "ada_b", "mix_norm", "ffn_norm", "par_w_in", "par_w_out", "hg_lb_logits", "hg_out_norm", "sg_w_in",
             "sg_v_gain", "sg_v_bias", "sg_w_pos", "sg_b_pos", "sg_w_out", "ffn_up", "ffn_conv_w", "ffn_conv_b",
             "ffn_down", "final_norm"]
    W = dict(zip(names, [ada_w, ada_b, mix_norm, ffn_norm, par_w_in, par_w_out, hg_lb_logits, hg_out_norm, sg_w_in,
                         sg_v_gain, sg_v_bias, sg_w_pos, sg_b_pos, sg_w_out, ffn_up, ffn_conv_w, ffn_conv_b, ffn_down,
                         final_norm]))
    M = dict(zip(names, [m_ada_w, m_ada_b, m_mix_norm, m_ffn_norm, m_par_w_in, m_par_w_out, m_hg_lb_logits, m_hg_out_norm,
                         m_sg_w_in, m_sg_v_gain, m_sg_v_bias, m_sg_w_pos, m_sg_b_pos, m_sg_w_out, m_ffn_up, m_ffn_conv_w,
                         m_ffn_conv_b, m_ffn_down, m_final_norm]))
    V = dict(zip(names, [v_ada_w, v_ada_b, v_mix_norm, v_ffn_norm, v_par_w_in, v_par_w_out, v_hg_lb_logits, v_hg_out_norm,
                         v_sg_w_in, v_sg_v_gain, v_sg_v_bias, v_sg_w_pos, v_sg_b_pos, v_sg_w_out, v_ffn_up, v_ffn_conv_w,
                         v_ffn_conv_b, v_ffn_down, v_final_norm]))

    x = x[0]
    target = loss_target[0]
    T, D = x.shape
    ix, iy, ic = _place()
    chip = 2 * ix + iy
    dev = 2 * chip + ic
    H = hg_out_norm.shape[1]
    SBW = H * HEAD
    NA = ada_w.shape[2]
    F2s = ffn_up.shape[2]
    F2 = N_CHIP * F2s
    SGW = sg_w_out.shape[1] * N_CHIP
    G = sg_w_pos.shape[1]

    shards = [par_w_in[0], par_w_out[0], sg_w_in[0], sg_w_out[0], ffn_up[0], ffn_up[1], ffn_down[0], ffn_down[1]]
    kinds = ["col", "row", "col", "row", "col", "col", "row", "row"]
    halves = [w.reshape(2, w.shape[0] // 2, w.shape[1]) for w in shards]
    groups = {"a": [0], "b": [1, 4, 6], "c": [2, 3, 5, 7]}
    rows8 = {0: _cast_into_rows("cast_w", halves[0], chip)}
    started = {}

    def as_weights(g, bufs):
        out = {}
        for i, g8 in zip(groups[g], bufs):
            K, N = shards[i].shape
            out[i] = g8.reshape(N_CHIP, K, N) if kinds[i] == "col" else g8.reshape(N_CHIP * K, N)
        return out

    def weights_landed(g, after):
        send_sems, recv_sems, bufs, _ = started[g]
        bufs = _chips_wait("gather_wait_" + g, "rows", len(bufs), send_sems, recv_sems, bufs, after)
        return _chips_start("gather_fill_start_" + g, "fill", bufs)

    def weights_of(g, filling, after):
        send_sems, recv_sems, bufs, _ = filling
        return as_weights(g, _chips_wait("gather_fill_wait_" + g, "fill", len(bufs), send_sems, recv_sems, bufs, after))

    n_cw = ffn_conv_w.size
    n_sv = sg_v_gain.size
    c_all, small_all = _all_gather("gather_small", [c, _pack_rows([ffn_conv_w, sg_v_gain, sg_v_bias])])
    c_all = c_all.reshape(N_DEV, D)
    small_all = small_all.reshape(N_CHIP, 2, -1)[:, 0]
    conv_w_full = small_all[:, :n_cw].reshape(N_CHIP, 2, CONV_WIDTH, F2s).transpose(1, 2, 0, 3).reshape(2, CONV_WIDTH, F2)
    sg_gain_full = small_all[:, n_cw:n_cw + n_sv].reshape(1, SGW)
    sg_bias_full = small_all[:, n_cw + n_sv:n_cw + 2 * n_sv].reshape(1, SGW)

    c_pad = jnp.pad(c_all, ((0, 16 - N_DEV), (0, 0)))
    ada_b_sh = lax.dynamic_slice(ada_b, (0, chip * NA), (2, NA)).reshape(2, 1, NA)
    mod_sh = _ada_fwd(c_pad, ada_w, ada_b_sh)
    mod_all, = _all_gather("gather_mod", [mod_sh[:, :N_DEV]])
    mod_all = mod_all.reshape(N_CHIP, 2, 2, N_DEV, NA)[:, 0]
    mod = lax.dynamic_index_in_dim(mod_all, dev, axis=2, keepdims=False)
    mod = mod.transpose(1, 0, 2).reshape(2, 6, D)
    mods = [[mod[l, k].reshape(1, D) for k in range(6)] for l in range(2)]
    started["a"] = _chips_start("gather_start_a", "rows", [rows8[0]], after=mod)
    for i in range(1, len(shards)):
        rows8[i] = _cast_into_rows("cast_w", halves[i], chip, after=started["a"][3])

    vec = lambda a: a.reshape(1, -1)
    l0 = vec(hg_lb_logits[0])
    l1 = vec(hg_lb_logits[1])
    hg_gain = vec(hg_out_norm[0])
    wpos = sg_w_pos[0]
    bpos = sg_b_pos[0].reshape(G, SG_CHUNK, 1)
    conv_b = [vec(ffn_conv_b[l]) for l in range(2)]

    sh1, sc1, g1, sh2, sc2, g2 = mods[0]
    send_a, recv_a, bufs_a, _ = started["a"]
    bufs_a = _chips_wait("gather_wait_a", "rows", 1, send_a, recv_a, bufs_a, rows8[len(shards) - 1])
    bufs_a = _fill_from_sibling("gather_fill_a", bufs_a)
    started["b"] = _chips_start("gather_start_b", "rows", [rows8[i] for i in groups["b"]], after=bufs_a[0])
    start_token = started["a"][3][0, 0] + started["b"][3][0, 0]
    w_in = as_weights("a", bufs_a)[0]
    h0 = _normmod_fwd("norm_mix0", x, vec(mix_norm[0]) + start_token, sc1, sh1)
    proj = _mm_nn("mm_par_in", h0, w_in)
    o_sb, sb_tot = _sb_fwd(proj, H)
    filling_b = weights_landed("b", o_sb)
    started["c"] = _chips_start("gather_start_c", "rows", [rows8[i] for i in groups["c"]], after=o_sb)
    o_hg, hg_states = _hg_fwd(proj, l0 + filling_b[3][0:1, 0:1] + started["c"][3][0:1, 0:1], l1, hg_gain, H, 3 * H)
    o_cat = jnp.concatenate([o_sb, o_hg], axis=1)
    wb = weights_of("b", filling_b, o_cat)
    w_out, wup, wdn = wb[1], [wb[4], None], [wb[6], None]
    y0 = _mm_nn("mm_par_out", o_cat, w_out)
    x1, h0f = _res_normmod_fwd("res_norm_ffn0", x, y0, g1, vec(ffn_norm[0]), sc2, sh2)
    a0 = _mm_nn("mm_up0", h0f, wup[0])
    u0 = _conv_fwd("conv_fwd0", a0, conv_w_full[0], conv_b[0])
    filling_c = weights_landed("c", u0)
    f0 = _mm_nn("mm_down0", u0, wdn[0])
    sh1b, sc1b, g1b, sh2b, sc2b, g2b = mods[1]
    x2, h1 = _res_normmod_fwd("res_norm_mix1", x1, f0, g2, vec(mix_norm[1]) + filling_c[3][0:1, 0:1], sc1b, sh1b)
    wc = weights_of("c", filling_c, h1)
    wsg_in, wsg_out, wup[1], wdn[1] = wc[2], wc[3], wc[5], wc[7]
    zpre = _mm_nn("mm_sg_in", h1, wsg_in)
    s1 = _sg_fwd(zpre, sg_gain_full, sg_bias_full, wpos, bpos)
    y1 = _mm_nn("mm_sg_out", s1, wsg_out)
    x3, h1f = _res_normmod_fwd("res_norm_ffn1", x2, y1, g1b, vec(ffn_norm[1]), sc2b, sh2b)
    a1 = _mm_nn("mm_up1", h1f, wup[1])
    u1 = _conv_fwd("conv_fwd1", a1, conv_w_full[1], conv_b[1])
    f1 = _mm_nn("mm_down1", u1, wdn[1])
    loss_sum, dx, df1, dg2b, d_final = _final_fwd_bwd(x3, f1, g2b, vec(final_norm), target)
    loss = lax.psum(loss_sum[0, 0], ("x", "y", "c"))

    def reduce_start(tag, idx, grads):
        eights = [g.reshape((N_DEV, -1, g.shape[-1])) for g in grads]
        landing = [lax.empty((N_CHIP,) + e.shape[1:], e.dtype) for e in eights]
        send_sems, recv_sems, bufs, token = _chips_start("grads_sibling_start_" + tag, "halves", eights, landing)
        return (tag, idx, send_sems, recv_sems, bufs), token[0:1, 0:1]

    def reduce_cross(state, after):
        tag, idx, send_sems, recv_sems, bufs = state
        n = len(idx)
        bufs = _chips_wait("grads_sibling_wait_" + tag, "halves", n, send_sems, recv_sems, bufs, after)
        pair = [_add_pairs("add_pair", e, r, ic) for e, r in zip(bufs[:n], bufs[n:])]
        landing = [lax.empty(p.shape, p.dtype) for p in pair]
        send_sems, recv_sems, bufs, token = _chips_start("grads_start_" + tag, "parts", pair, landing)
        return (tag, idx, send_sems, recv_sems, bufs), token[0:1, 0:1]

    def reduce_finish(state, after):
        tag, idx, send_sems, recv_sems, bufs = state
        n = len(idx)
        bufs = _chips_wait("grads_wait_" + tag, "parts", n, send_sems, recv_sems, bufs, after)
        halves = [_sum_into_pair("sum_chips", p, x_, ic, chip) for p, x_ in zip(bufs[:n], bufs[n:])]
        both = _share_halves("grads_share_" + tag, halves)
        return {i: b.reshape(shards[i].shape) for i, b in zip(idx, both)}

    def ffn_bwd(l, dfl, u, a, hf):
        g_dn = _mm_tn("mm_g_down", u, dfl)
        du = _mm_nt("mm_d_u", dfl, wdn[l])
        da, dcw, dcb = _conv_bwd("conv_bwd", a, du, conv_w_full[l], conv_b[l])
        g_up = _mm_tn("mm_g_up", hf, da, chunks=N_CHIP)
        dh = _mm_nt("mm_d_hf", da, wup[l])
        return g_dn, g_up, dcw, dcb, dh

    g_dn1, g_up1, dcw1, dcb1, dh1f = ffn_bwd(1, df1, u1, a1, h1f)
    red1, tok = reduce_start("1", [5, 7], [g_up1, g_dn1])
    dx, dgn_f1, dsc2b, dsh2b, dy1, dg1b = _block_bwd("bwd_ffn1", dx, dh1f, x3, vec(ffn_norm[1]) + tok, sc2b, sh2b, y1, g1b)
    g_sg_out = _mm_tn("mm_g_sg_out", s1, dy1)
    ds1 = _mm_nt("mm_d_s", dy1, wsg_out)
    dzpre, dsg_gain, dsg_bias, dwpos, dbpos = _sg_bwd(zpre, ds1, sg_gain_full, sg_bias_full, wpos, bpos)
    red1, tok_x = reduce_cross(red1, dzpre)
    g_sg_in = _mm_tn("mm_g_sg_in", h1, dzpre, chunks=N_CHIP)
    dh1 = _mm_nt("mm_d_h1", dzpre, wsg_in)
    red2, tok = reduce_start("2", [2, 3], [g_sg_in, g_sg_out])
    dx, dgn_m1, dsc1b, dsh1b, df0, dg2 = _block_bwd("bwd_mix1", dx, dh1, x2, vec(mix_norm[1]) + tok + tok_x, sc1b, sh1b, f0, g2)
    g_dn0, g_up0, dcw0, dcb0, dh0f = ffn_bwd(0, df0, u0, a0, h0f)
    red2, tok_x = reduce_cross(red2, dh0f)
    red3, tok = reduce_start("3", [4, 6], [g_up0, g_dn0])
    dx, dgn_f0, dsc2, dsh2, dy0, dg1 = _block_bwd("bwd_ffn0", dx, dh0f, x1, vec(ffn_norm[0]) + tok + tok_x, sc2, sh2, y0, g1)
    g_out = _mm_tn("mm_g_par_out", o_cat, dy0)
    do = _mm_nt("mm_d_o", dy0, w_out)
    red3, tok_x = reduce_cross(red3, do)
    dhq, dhf, dhi, dhg, dl0, dl1, dhg_gain = _hg_bwd(proj, hg_states, do, l0 + tok_x, l1, hg_gain, H, 3 * H, H)
    dq, dk, dv = _sb_bwd(proj, do, sb_tot, H)
    dproj = jnp.concatenate([dq, dk, dv, dhq, dhf, dhi, dhg], axis=1).astype(BF16)
    g_in = _mm_tn("mm_g_par_in", h0, dproj, chunks=N_CHIP)
    red4, tok = reduce_start("4", [0, 1], [g_in, g_out])
    dh0 = _mm_nt("mm_d_h0", dproj, w_in)
    grad_x, dgn_m0, dsc1, dsh1 = _block_bwd("bwd_mix0", dx, dh0, x, vec(mix_norm[0]) + tok, sc1, sh1)
    red4, tok_x = reduce_cross(red4, grad_x)

    G_, delta, new_m, new_v = {}, {}, {}, {}

    def adam_on(nme):
        shp = W[nme].shape
        r2 = lambda a: a.reshape(-1, shp[-1])
        d_, m_, v_ = _adam("adam_" + nme, r2(W[nme]), r2(G_[nme]), r2(M[nme]), r2(V[nme]))
        delta[nme], new_m[nme], new_v[nme] = d_.reshape(shp), m_.reshape(shp), v_.reshape(shp)

    g_shards = {}
    for state in (red1, red2, red3):
        g_shards.update(reduce_finish(state, red4[4][0]))
    G_["sg_w_in"] = g_shards[2][None]
    G_["sg_w_out"] = g_shards[3][None]
    G_["ffn_up"] = jnp.stack([g_shards[4], g_shards[5]])
    G_["ffn_down"] = jnp.stack([g_shards[6], g_shards[7]])
    for nme in ["sg_w_in", "sg_w_out", "ffn_up", "ffn_down"]:
        adam_on(nme)

    dmod = jnp.concatenate([dsh1, dsc1, dg1, dsh2, dsc2, dg2, dsh1b, dsc1b, dg1b, dsh2b, dsc2b, dg2b], axis=1)
    parts = [dmod, dgn_m0, dgn_m1, dgn_f0, dgn_f1, dl0, dl1, dhg_gain, dsg_gain, dsg_bias, dwpos, dbpos,
             dcw0, dcw1, dcb0, dcb1, d_final]
    sizes = [p.size for p in parts]
    packed = _pack_rows(parts)
    packed_all, = _all_gather("gather_small_grads", [packed], after=new_v["ffn_down"])
    summed = _sum_leading("sum_small_grads", packed_all).reshape(-1)
    offs = [0]
    for s in sizes:
        offs.append(offs[-1] + s)
    red = [summed[offs[i]:offs[i + 1]] for i in range(len(parts))]
    (r_dmod, r_gm0, r_gm1, r_gf0, r_gf1, r_l0, r_l1, r_hgain, r_sgain, r_sbias, r_wpos, r_bpos,
     r_cw0, r_cw1, r_cb0, r_cb1, r_final) = red
    n_mod = sizes[0]
    dmod_all = packed_all.reshape(N_DEV, -1)[:, :n_mod].reshape(N_DEV, 2, 6 * D)

    G_["ada_b"] = r_dmod.reshape(2, 6 * D)
    G_["mix_norm"] = jnp.stack([r_gm0, r_gm1])
    G_["ffn_norm"] = jnp.stack([r_gf0, r_gf1])
    G_["hg_lb_logits"] = jnp.stack([r_l0, r_l1])
    G_["hg_out_norm"] = r_hgain.reshape(hg_out_norm.shape)
    G_["sg_v_gain"] = lax.dynamic_slice(r_sgain, (chip * n_sv,), (n_sv,)).reshape(sg_v_gain.shape)
    G_["sg_v_bias"] = lax.dynamic_slice(r_sbias, (chip * n_sv,), (n_sv,)).reshape(sg_v_bias.shape)
    G_["sg_w_pos"] = r_wpos.reshape(sg_w_pos.shape)
    G_["sg_b_pos"] = r_bpos.reshape(sg_b_pos.shape)
    cw_full = jnp.stack([r_cw0.reshape(CONV_WIDTH, F2), r_cw1.reshape(CONV_WIDTH, F2)])
    G_["ffn_conv_w"] = lax.dynamic_slice(cw_full, (0, 0, chip * F2s), (2, CONV_WIDTH, F2s))
    G_["ffn_conv_b"] = jnp.stack([r_cb0, r_cb1])
    G_["final_norm"] = r_final

    c_t = jnp.pad(c_all, ((0, HEAD - N_DEV), (0, 0))).T
    dmod_sh = lax.dynamic_slice(dmod_all.transpose(1, 0, 2), (0, 0, chip * NA), (2, N_DEV, NA))
    dmod_sh = jnp.pad(dmod_sh, ((0, 0), (0, HEAD - N_DEV), (0, 0)))
    G_["ada_w"], delta["ada_w"], new_m["ada_w"], new_v["ada_w"] = _ada_grad_adam(c_t, dmod_sh, ada_w, m_ada_w, v_ada_w)

    g_shards.update(reduce_finish(red4, G_["ada_w"]))
    G_["par_w_in"] = g_shards[0][None]
    G_["par_w_out"] = g_shards[1][None]
    for nme in ["par_w_in", "par_w_out"]:
        adam_on(nme)
    small = [n_ for n_ in names if n_ not in delta]
    pk = lambda dct: _pack_rows([dct[n_] for n_ in small])
    d_, m_, v_ = _adam("adam_small", pk(W), pk(G_), pk(M), pk(V))
    off = 0
    for n_ in small:
        sz = W[n_].size
        for dst, src in ((delta, d_), (new_m, m_), (new_v, v_)):
            dst[n_] = src.reshape(-1)[off:off + sz].reshape(W[n_].shape)
        off += sz

    return (loss, grad_x[None], *[G_[n_] for n_ in names], *[delta[n_] for n_ in names],
            *[new_m[n_] for n_ in names], *[new_v[n_] for n_ in names])
```

```python
import functools
import math

import jax
import jax.numpy as jnp
from jax import lax
from jax.experimental import pallas as pl
from jax.experimental.pallas import tpu as pltpu

F32 = jnp.float32
BF16 = jnp.bfloat16
MESH = pl.DeviceIdType.MESH
ANY = pl.BlockSpec(memory_space=pl.ANY)

NORM_EPS = 1e-6
ADAM_LR = 0.001
ADAM_B1 = 0.9
ADAM_B2 = 0.999
ADAM_EPS = 1e-08
ADAM_WD = 0.01
ADAM_STEP = 10
CONV_WIDTH = 3
HEAD = 128
HG_CHUNK = 64
SG_CHUNK = 128
N_DEV = 8
N_CHIP = 4
V7X_VMEM_LIMIT = 56 * 1024 * 1024


def _cp(*sem):
    return pltpu.CompilerParams(dimension_semantics=sem if sem else None, vmem_limit_bytes=V7X_VMEM_LIMIT)


def _pick(n, prefs):
    for p in prefs:
        if p <= n and n % p == 0:
            return p
    return n


def _iota(shape, axis):
    return lax.broadcasted_iota(jnp.int32, shape, axis)


def _rows_within(R, row_bytes, budget):
    if R * row_bytes <= budget:
        return R
    for t in (1024, 512, 256, 128, 64, 32, 16):
        if R % t == 0 and t * row_bytes <= budget:
            return t
    return _pick(R, (16, 8))


def _pack_rows(arrays):
    flat = jnp.concatenate([a.reshape(-1) for a in arrays])
    pad = (-flat.size) % (8 * HEAD)
    return jnp.pad(flat, (0, pad)).reshape(-1, HEAD)


def _dg(a, b, ca, cb):
    if a.ndim == 3:
        dims = (((ca + 1,), (cb + 1,)), ((0,), (0,)))
    else:
        dims = (((ca,), (cb,)), ((), ()))
    return lax.dot_general(a.astype(BF16), b.astype(BF16), dims, preferred_element_type=F32)


@jax.custom_vjp
def mm_nn(a, b):
    return _dg(a, b, 1, 0)


mm_nn.defvjp(lambda a, b: (_dg(a, b, 1, 0), (a, b)),
             lambda r, g: (_dg(g, r[1], 1, 1), _dg(r[0], g, 0, 0)))


@jax.custom_vjp
def mm_nt(a, b):
    return _dg(a, b, 1, 1)


mm_nt.defvjp(lambda a, b: (_dg(a, b, 1, 1), (a, b)),
             lambda r, g: (_dg(g, r[1], 1, 0), _dg(g, r[0], 0, 0)))


@jax.custom_vjp
def mm_tn(a, b):
    return _dg(a, b, 0, 0)


mm_tn.defvjp(lambda a, b: (_dg(a, b, 0, 0), (a, b)),
             lambda r, g: (_dg(r[1], g, 1, 1), _dg(r[0], g, 1, 0)))


def _split(x):
    hi = x.astype(BF16)
    lo = (x - hi.astype(F32)).astype(BF16)
    return hi, lo


def _sum_right(x, m01):
    hi, lo = _split(x)
    return _dg(hi, m01, 1, 0) + _dg(lo, m01, 1, 0)


def _sum_left_impl(m01, x, ca):
    if x.ndim == 3:
        m01 = jnp.broadcast_to(m01, (x.shape[0],) + m01.shape)
    hi, lo = _split(x)
    return _dg(m01, hi, ca, 0) + _dg(m01, lo, ca, 0)


@jax.custom_vjp
def _sum_left(m01, x):
    return _sum_left_impl(m01, x, 1)


_sum_left.defvjp(lambda m, x: (_sum_left_impl(m, x, 1), m),
                 lambda m, g: (None, _sum_left_impl(m, g, 0)))


def _sigmoid(x):
    return 1.0 / (1.0 + jnp.exp(-x))


def _softplus(z):
    return jnp.maximum(z, 0.0) + jnp.log(1.0 + jnp.exp(-jnp.abs(z)))


_INV_SQRT2 = 1.0 / math.sqrt(2.0)
_INV_SQRT2PI = 1.0 / math.sqrt(2.0 * math.pi)


@jax.custom_vjp
def _gelu(x):
    return 0.5 * x * (1.0 + lax.erf(x * _INV_SQRT2))


_gelu.defvjp(lambda x: (0.5 * x * (1.0 + lax.erf(x * _INV_SQRT2)), x),
             lambda x, g: (g * (0.5 * (1.0 + lax.erf(x * _INV_SQRT2)) + x * jnp.exp(-0.5 * x * x) * _INV_SQRT2PI),))


def _rms(x, gain):
    r = lax.rsqrt(jnp.mean(x * x, axis=-1, keepdims=True) + NORM_EPS)
    return x * r * gain


def _normmod(x, gain, sc, sh):
    return _rms(x, gain) * (1.0 + sc) + sh


def _mm_call(name, a, b, out_shape, out_dtype, dims, grid, a_spec, b_spec, o_spec, acc_shape):
    nk = grid[2]

    def body(a_ref, b_ref, o_ref, *scratch):
        part = lax.dot_general(a_ref[...].astype(BF16), b_ref[...].astype(BF16), dims, preferred_element_type=F32)
        if nk == 1:
            o_ref[...] = part.astype(o_ref.dtype)
            return
        acc_ref, = scratch
        k = pl.program_id(2)

        @pl.when(k == 0)
        def _():
            acc_ref[...] = part

        @pl.when(k > 0)
        def _():
            acc_ref[...] += part

        @pl.when(k == nk - 1)
        def _():
            o_ref[...] = acc_ref[...].astype(o_ref.dtype)

    return pl.pallas_call(
        body, name=name, grid=grid, in_specs=[a_spec, b_spec], out_specs=o_spec,
        out_shape=jax.ShapeDtypeStruct(out_shape, out_dtype),
        scratch_shapes=[] if nk == 1 else [pltpu.VMEM(acc_shape, F32)],
        compiler_params=_cp("parallel", "parallel", "arbitrary"),
    )(a, b)


def _mm_nn(name, a, b, out_dtype=F32):
    M, K = a.shape
    chunked = b.ndim == 3
    Nc = b.shape[-1]
    N = Nc * (b.shape[0] if chunked else 1)
    tm = _pick(M, (1024, 512, 256, 128, 64, 32, 16, 8))
    tn = _pick(Nc, (1408, 1024, 896, 512, 256, 128))
    tk = _pick(K, (2048, 1408, 1024, 512, 256, 128))
    npc = Nc // tn
    if chunked:
        b_spec = pl.BlockSpec((None, tk, tn), lambda i, j, k: (j // npc, k, j % npc))
    else:
        b_spec = pl.BlockSpec((tk, tn), lambda i, j, k: (k, j))
    return _mm_call(name, a, b, (M, N), out_dtype, (((1,), (0,)), ((), ())), (M // tm, N // tn, K // tk),
                    pl.BlockSpec((tm, tk), lambda i, j, k: (i, k)), b_spec,
                    pl.BlockSpec((tm, tn), lambda i, j, k: (i, j)), (tm, tn))


def _mm_nt(name, a, b, out_dtype=F32):
    planar = a.ndim == 3
    M, Np = a.shape[-2:]
    N = Np * (a.shape[0] if planar else 1)
    chunked = b.ndim == 3
    Nc = b.shape[-1]
    K = b.shape[-2]
    tm = _pick(M, (1024, 512, 256, 128, 64, 32, 16, 8))
    tn = _pick(K, (1408, 1024, 512, 256, 128))
    tk = _pick(Nc, (2048, 1792, 1408, 1024, 896, 512, 256, 128))
    assert Np % tk == 0
    npc = Nc // tk
    npp = Np // tk
    if chunked:
        b_spec = pl.BlockSpec((None, tn, tk), lambda i, j, k: (k // npc, j, k % npc))
    else:
        b_spec = pl.BlockSpec((tn, tk), lambda i, j, k: (j, k))
    if planar:
        a_spec = pl.BlockSpec((None, tm, tk), lambda i, j, k: (k // npp, i, k % npp))
    else:
        a_spec = pl.BlockSpec((tm, tk), lambda i, j, k: (i, k))
    return _mm_call(name, a, b, (M, K), out_dtype, (((1,), (1,)), ((), ())), (M // tm, K // tn, N // tk),
                    a_spec, b_spec, pl.BlockSpec((tm, tn), lambda i, j, k: (i, j)), (tm, tn))


def _mm_tn(name, a, b, chunks=1, out_dtype=BF16):
    T, K = a.shape
    planar = b.ndim == 3
    Np = b.shape[-1]
    N = Np * (b.shape[0] if planar else 1)
    Nc = N // chunks
    tm = _pick(K, (1408, 1024, 512, 256, 128))
    tn = _pick(Nc, (1408, 1024, 896, 512, 256, 128))
    tk = _pick(T, (1024, 512, 256, 128))
    assert Np % tn == 0
    npc = Nc // tn
    npp = Np // tn
    if planar:
        b_spec = pl.BlockSpec((None, tk, tn), lambda i, j, k: (j // npp, k, j % npp))
    else:
        b_spec = pl.BlockSpec((tk, tn), lambda i, j, k: (k, j))
    if chunks > 1:
        shape = (chunks, K, Nc)
        o_spec = pl.BlockSpec((None, tm, tn), lambda i, j, k: (j // npc, i, j % npc))
    else:
        shape = (K, N)
        o_spec = pl.BlockSpec((tm, tn), lambda i, j, k: (i, j))
    return _mm_call(name, a, b, shape, out_dtype, (((0,), (0,)), ((), ())), (K // tm, N // tn, T // tk),
                    pl.BlockSpec((tk, tm), lambda i, j, k: (k, i)), b_spec, o_spec, (tm, tn))


def _row_tile(T):
    return _pick(T, (256, 128, 64, 32, 16, 8))


def _vec_spec(D):
    return pl.BlockSpec((1, D), lambda i: (0, 0))


def _normmod_fwd(name, x, gain, sc, sh):
    T, D = x.shape
    bt = _row_tile(T)

    def body(x_ref, g_ref, sc_ref, sh_ref, h_ref):
        h_ref[...] = _normmod(x_ref[...], g_ref[...], sc_ref[...], sh_ref[...]).astype(h_ref.dtype)

    rows = pl.BlockSpec((bt, D), lambda i: (i, 0))
    return pl.pallas_call(body, name=name, grid=(T // bt,), in_specs=[rows] + [_vec_spec(D)] * 3, out_specs=rows,
                          out_shape=jax.ShapeDtypeStruct((T, D), BF16), compiler_params=_cp("parallel"))(x, gain, sc, sh)


def _res_normmod_fwd(name, x, y, g, gain, sc, sh):
    T, D = x.shape
    bt = _row_tile(T)

    def body(x_ref, y_ref, gate_ref, g_ref, sc_ref, sh_ref, x1_ref, h_ref):
        x1 = x_ref[...] + gate_ref[...] * y_ref[...]
        x1_ref[...] = x1
        h_ref[...] = _normmod(x1, g_ref[...], sc_ref[...], sh_ref[...]).astype(h_ref.dtype)

    rows = pl.BlockSpec((bt, D), lambda i: (i, 0))
    return pl.pallas_call(body, name=name, grid=(T // bt,), in_specs=[rows, rows] + [_vec_spec(D)] * 4,
                          out_specs=[rows, rows],
                          out_shape=[jax.ShapeDtypeStruct((T, D), F32), jax.ShapeDtypeStruct((T, D), BF16)],
                          compiler_params=_cp("parallel"))(x, y, g, gain, sc, sh)


def _final_fwd_bwd(x, y, g, gain, target):
    T, D = x.shape
    bt = _row_tile(T)

    def body(x_ref, y_ref, gate_ref, g_ref, t_ref, loss_ref, dx_ref, dy_ref, dgate_ref, dgain_ref):
        i = pl.program_id(0)
        yv = y_ref[...]
        gate = gate_ref[...]
        x4 = x_ref[...] + gate * yv
        out, vjp = jax.vjp(_rms, x4, g_ref[...])
        err = out - t_ref[...]
        dx4, dgain = vjp(err * (1.0 / D))
        part = 0.5 * jnp.sum(jnp.mean(err * err, axis=-1, keepdims=True), axis=0, keepdims=True)

        @pl.when(i == 0)
        def _():
            loss_ref[...] = jnp.zeros_like(loss_ref)
            dgate_ref[...] = jnp.zeros_like(dgate_ref)
            dgain_ref[...] = jnp.zeros_like(dgain_ref)

        loss_ref[...] += jnp.broadcast_to(part, loss_ref.shape)
        dx_ref[...] = dx4
        dy_ref[...] = (gate * dx4).astype(dy_ref.dtype)
        dgate_ref[...] += jnp.sum(dx4 * yv, axis=0, keepdims=True)
        dgain_ref[...] += dgain

    rows = pl.BlockSpec((bt, D), lambda i: (i, 0))
    vec = _vec_spec(D)
    return pl.pallas_call(
        body, name="final_loss", grid=(T // bt,), in_specs=[rows, rows, vec, vec, rows],
        out_specs=[pl.BlockSpec((1, HEAD), lambda i: (0, 0)), rows, rows, vec, vec],
        out_shape=[jax.ShapeDtypeStruct((1, HEAD), F32), jax.ShapeDtypeStruct((T, D), F32),
                   jax.ShapeDtypeStruct((T, D), BF16), jax.ShapeDtypeStruct((1, D), F32),
                   jax.ShapeDtypeStruct((1, D), F32)],
        compiler_params=_cp("arbitrary"))(x, y, g, gain, target)


def _block_bwd(name, dx_out, dh, x_in, gain, sc, sh, y_prev=None, g_prev=None):
    T, D = x_in.shape
    bt = _row_tile(T)
    has_prev = y_prev is not None

    def body(*refs):
        if has_prev:
            dxo_ref, dh_ref, x_ref, g_ref, sc_ref, sh_ref, y_ref, gp_ref, dx_ref, dgain_ref, dsc_ref, dsh_ref, dy_ref, dgp_ref = refs
        else:
            dxo_ref, dh_ref, x_ref, g_ref, sc_ref, sh_ref, dx_ref, dgain_ref, dsc_ref, dsh_ref = refs
        i = pl.program_id(0)
        _, vjp = jax.vjp(_normmod, x_ref[...], g_ref[...], sc_ref[...], sh_ref[...])
        dxn, dgain, dsc, dsh = vjp(dh_ref[...])
        dx = dxo_ref[...] + dxn
        dx_ref[...] = dx

        @pl.when(i == 0)
        def _():
            dgain_ref[...] = jnp.zeros_like(dgain_ref)
            dsc_ref[...] = jnp.zeros_like(dsc_ref)
            dsh_ref[...] = jnp.zeros_like(dsh_ref)
            if has_prev:
                dgp_ref[...] = jnp.zeros_like(dgp_ref)

        dgain_ref[...] += dgain
        dsc_ref[...] += dsc
        dsh_ref[...] += dsh
        if has_prev:
            dy_ref[...] = (gp_ref[...] * dx).astype(dy_ref.dtype)
            dgp_ref[...] += jnp.sum(dx * y_ref[...], axis=0, keepdims=True)

    rows = pl.BlockSpec((bt, D), lambda i: (i, 0))
    vec = _vec_spec(D)
    ins = [dx_out, dh, x_in, gain, sc, sh]
    in_specs = [rows, rows, rows, vec, vec, vec]
    out_specs = [rows, vec, vec, vec]
    out_shape = [jax.ShapeDtypeStruct((T, D), F32)] + [jax.ShapeDtypeStruct((1, D), F32)] * 3
    if has_prev:
        ins += [y_prev, g_prev]
        in_specs += [rows, vec]
        out_specs += [rows, vec]
        out_shape += [jax.ShapeDtypeStruct((T, D), BF16), jax.ShapeDtypeStruct((1, D), F32)]
    return pl.pallas_call(body, name=name, grid=(T // bt,), in_specs=in_specs, out_specs=out_specs,
                          out_shape=out_shape, compiler_params=_cp("arbitrary"))(*ins)


def _sb_tiles(T):
    tq = _pick(T, (512, 256, 128))
    return tq, tq // HEAD


def _sb_fwd(proj, H):
    T = proj.shape[0]
    tq, nsub = _sb_tiles(T)
    scale = HEAD ** -0.5

    def body(q_ref, k_ref, v_ref, o_ref, l_ref, acc_ref):
        i = pl.program_id(1)
        q = q_ref[...].astype(BF16)
        later = (_iota((HEAD, HEAD), 0) > _iota((HEAD, HEAD), 1)).astype(BF16)
        row = _iota((tq, HEAD), 0)
        col = _iota((tq, HEAD), 1)

        def key_step(j, c, diagonal):
            off = pl.multiple_of(j * tq, tq)
            k = k_ref[pl.ds(off, tq), :].astype(BF16)
            v = v_ref[pl.ds(off, tq), :].astype(BF16)
            z = _dg(q, k, 1, 1) * scale
            ws = [None] * nsub
            for s in reversed(range(nsub)):
                zs = z[:, s * HEAD:(s + 1) * HEAD]
                sp = _softplus(zs)
                if diagonal:
                    strict = (s * HEAD + col) < row
                    lk = jnp.where(strict, -sp, 0.0)
                else:
                    lk = -sp
                w = jnp.exp(zs - sp + _sum_right(lk, later) + c)
                if diagonal:
                    w = jnp.where(strict, w, 0.0)
                ws[s] = w.astype(BF16)
                c = c + jnp.sum(lk, axis=1, keepdims=True)
            acc_ref[...] += _dg(jnp.concatenate(ws, axis=1), v, 1, 0)
            return c

        acc_ref[...] = jnp.zeros_like(acc_ref)
        c = key_step(i, jnp.zeros((tq, 1), F32), True)
        c = lax.fori_loop(0, i, lambda n, c: key_step(i - 1 - n, c, False), c)
        o_ref[...] = acc_ref[...].astype(o_ref.dtype)
        l_ref[...] = jnp.broadcast_to(c, (tq, HEAD))

    blk = pl.BlockSpec((tq, HEAD), lambda h, i: (i, h))
    return pl.pallas_call(
        body, name="sb_fwd", grid=(H, T // tq),
        in_specs=[blk, pl.BlockSpec((T, HEAD), lambda h, i: (0, H + h)), pl.BlockSpec((T, HEAD), lambda h, i: (0, 2 * H + h))],
        out_specs=[blk, blk],
        out_shape=[jax.ShapeDtypeStruct((T, H * HEAD), BF16), jax.ShapeDtypeStruct((T, H * HEAD), F32)],
        scratch_shapes=[pltpu.VMEM((tq, HEAD), F32)],
        compiler_params=_cp("parallel", "arbitrary"))(proj, proj, proj)


def _sb_bwd(proj, do, L, H):
    T = proj.shape[0]
    tq, nsub = _sb_tiles(T)
    scale = HEAD ** -0.5

    def body(q_ref, k_ref, v_ref, do_ref, l_ref, dq_ref, dk_ref, dv_ref):
        i = pl.program_id(1)

        @pl.when(i == 0)
        def _():
            dk_ref[...] = jnp.zeros_like(dk_ref)
            dv_ref[...] = jnp.zeros_like(dv_ref)

        dq_ref[...] = jnp.zeros_like(dq_ref)
        q = q_ref[...].astype(BF16)
        do_ = do_ref[...].astype(BF16)
        total = l_ref[...]
        upto = (_iota((HEAD, HEAD), 0) <= _iota((HEAD, HEAD), 1)).astype(BF16)
        before = (_iota((HEAD, HEAD), 0) < _iota((HEAD, HEAD), 1)).astype(BF16)
        row = _iota((tq, HEAD), 0)
        col = _iota((tq, HEAD), 1)

        def key_step(j, carry, diagonal):
            cp, ce = carry
            off = pl.multiple_of(j * tq, tq)
            k = k_ref[pl.ds(off, tq), :].astype(BF16)
            v = v_ref[pl.ds(off, tq), :].astype(BF16)
            z = _dg(q, k, 1, 1) * scale
            dw = _dg(do_, v, 1, 1)
            ws, dzs = [], []
            for s in range(nsub):
                zs = z[:, s * HEAD:(s + 1) * HEAD]
                sp = _softplus(zs)
                if diagonal:
                    strict = (s * HEAD + col) < row
                    lk = jnp.where(strict, -sp, 0.0)
                else:
                    lk = -sp
                tail = total - (_sum_right(lk, upto) + cp)
                w = jnp.exp(zs - sp + tail)
                if diagonal:
                    w = jnp.where(strict, w, 0.0)
                e = w * dw[:, s * HEAD:(s + 1) * HEAD]
                e_before = _sum_right(e, before) + ce
                sig = jnp.exp(zs - sp)
                dz = (e * (1.0 - sig) - e_before * sig) * scale
                if diagonal:
                    dz = jnp.where(strict, dz, 0.0)
                ws.append(w.astype(BF16))
                dzs.append(dz.astype(BF16))
                cp = cp + jnp.sum(lk, axis=1, keepdims=True)
                ce = ce + jnp.sum(e, axis=1, keepdims=True)
            w_all = jnp.concatenate(ws, axis=1)
            dz_all = jnp.concatenate(dzs, axis=1)
            dv_ref[pl.ds(off, tq), :] += _dg(w_all, do_, 0, 0)
            dk_ref[pl.ds(off, tq), :] += _dg(dz_all, q, 0, 0)
            dq_ref[...] += _dg(dz_all, k, 1, 0)
            return cp, ce

        zero = jnp.zeros((tq, 1), F32)
        carry = lax.fori_loop(0, i, lambda j, cr: key_step(j, cr, False), (zero, zero))
        key_step(i, carry, True)

    blk = pl.BlockSpec((tq, HEAD), lambda h, i: (i, h))
    full = pl.BlockSpec((T, HEAD), lambda h, i: (0, h))
    shp = jax.ShapeDtypeStruct((T, H * HEAD), F32)
    return pl.pallas_call(
        body, name="sb_bwd", grid=(H, T // tq),
        in_specs=[blk, pl.BlockSpec((T, HEAD), lambda h, i: (0, H + h)), pl.BlockSpec((T, HEAD), lambda h, i: (0, 2 * H + h)),
                  blk, blk],
        out_specs=[blk, full, full], out_shape=[shp, shp, shp],
        compiler_params=_cp("parallel", "arbitrary"))(proj, proj, proj, do, L)


def _hg_tile(q, fl, iv, g, st, l0, l1, gain):
    R = 2 * HG_CHUNK
    row = _iota((R, R), 0)
    col = _iota((R, R), 1)
    first = row < HG_CHUNK
    same = first == (col < HG_CHUNK)
    tri = (row >= col) & same
    lb = _sigmoid(l0 - l1)
    f = lb + (1.0 - lb) * _sigmoid(fl)
    logf = jnp.log(f)
    k = 1.0 - f
    qf = q * _sigmoid(q)
    G = _sum_left(tri.astype(BF16), logf)
    gl_a = jnp.sum(jnp.where(first, logf, 0.0), axis=-2, keepdims=True)
    gl_b = jnp.sum(jnp.where(first, 0.0, logf), axis=-2, keepdims=True)
    q_dec = qf * jnp.exp(G)
    k_inv = k * jnp.exp(-G)
    k_end = k * jnp.exp(jnp.where(first, gl_a, gl_b) - G)
    scores = jnp.where(tri, mm_nt(q_dec, k_inv), 0.0)
    o = mm_nn(scores, iv)
    o_a = mm_nt(q_dec, st)
    st_mid = st * jnp.exp(gl_a) + mm_tn(jnp.where(first, iv, 0.0), k_end)
    o_b = mm_nt(q_dec, st_mid)
    st_new = st_mid * jnp.exp(gl_b) + mm_tn(jnp.where(first, 0.0, iv), k_end)
    o = o + jnp.where(first, o_a, o_b)
    on = o * lax.rsqrt(jnp.mean(o * o, axis=-1, keepdims=True) + NORM_EPS) * gain
    return on * (g * _sigmoid(g)), st_new


def _hg_heads(H):
    return _pick(H, (8, 4, 2, 1))


def _hg_specs(H, c0, rev, nt):
    hb = _hg_heads(H)
    w = hb * HEAD

    def at(base):
        if rev:
            return pl.BlockSpec((HEAD, w), lambda h, i: (nt - 1 - i, base // hb + h))
        return pl.BlockSpec((HEAD, w), lambda h, i: (i, base // hb + h))
    return [at(c0), at(c0 + H), at(c0 + 2 * H), at(c0 + 3 * H)]


def _hg_fwd(proj, l0, l1, gain, H, c0):
    T = proj.shape[0]
    nt = T // HEAD
    hb = _hg_heads(H)
    w = hb * HEAD

    def body(q_ref, f_ref, i_ref, g_ref, l0_ref, l1_ref, gain_ref, o_ref, st_out_ref, st_ref):
        @pl.when(pl.program_id(1) == 0)
        def _():
            st_ref[...] = jnp.zeros_like(st_ref)

        sl = [slice(j * HEAD, (j + 1) * HEAD) for j in range(hb)]
        heads = lambda ref: jnp.stack([ref[:, s] for s in sl])
        st = st_ref[...]
        st_out_ref[...] = st
        out, st_new = _hg_tile(heads(q_ref), heads(f_ref), heads(i_ref), heads(g_ref), st, heads(l0_ref), heads(l1_ref),
                               heads(gain_ref))
        for j, s in enumerate(sl):
            o_ref[:, s] = out[j].astype(o_ref.dtype)
        st_ref[...] = st_new

    vec = pl.BlockSpec((1, w), lambda h, i: (0, h))
    return pl.pallas_call(
        body, name="hg_fwd", grid=(H // hb, nt), in_specs=_hg_specs(H, c0, False, nt) + [vec, vec, vec],
        out_specs=[pl.BlockSpec((HEAD, w), lambda h, i: (i, h)),
                   pl.BlockSpec((hb, None, HEAD, HEAD), lambda h, i: (h, i, 0, 0))],
        out_shape=[jax.ShapeDtypeStruct((T, H * HEAD), BF16), jax.ShapeDtypeStruct((H, nt, HEAD, HEAD), F32)],
        scratch_shapes=[pltpu.VMEM((hb, HEAD, HEAD), F32)],
        compiler_params=_cp("parallel", "arbitrary"))(proj, proj, proj, proj, l0, l1, gain)


def _hg_bwd(proj, states, do, l0, l1, gain, H, c0, do_c0):
    T = proj.shape[0]
    nt = T // HEAD
    hb = _hg_heads(H)
    w = hb * HEAD

    def body(q_ref, f_ref, i_ref, g_ref, st_in_ref, do_ref, l0_ref, l1_ref, gain_ref,
             dq_ref, df_ref, di_ref, dg_ref, dl0_ref, dl1_ref, dgain_ref, dst_ref):
        @pl.when(pl.program_id(1) == 0)
        def _():
            dst_ref[...] = jnp.zeros_like(dst_ref)
            dl0_ref[...] = jnp.zeros_like(dl0_ref)
            dl1_ref[...] = jnp.zeros_like(dl1_ref)
            dgain_ref[...] = jnp.zeros_like(dgain_ref)

        sl = [slice(j * HEAD, (j + 1) * HEAD) for j in range(hb)]
        heads = lambda ref: jnp.stack([ref[:, s] for s in sl])
        _, vjp = jax.vjp(_hg_tile, heads(q_ref), heads(f_ref), heads(i_ref), heads(g_ref), st_in_ref[...],
                         heads(l0_ref), heads(l1_ref), heads(gain_ref))
        dq, df, di, dg, dst, dl0, dl1, dgain = vjp((heads(do_ref), dst_ref[...]))
        dst_ref[...] = dst
        for j, s in enumerate(sl):
            dq_ref[:, s] = dq[j]
            df_ref[:, s] = df[j]
            di_ref[:, s] = di[j]
            dg_ref[:, s] = dg[j]
            dl0_ref[:, s] += dl0[j]
            dl1_ref[:, s] += dl1[j]
            dgain_ref[:, s] += dgain[j]

    vec = pl.BlockSpec((1, w), lambda h, i: (0, h))
    rblk = pl.BlockSpec((HEAD, w), lambda h, i: (nt - 1 - i, h))
    shp = jax.ShapeDtypeStruct((T, H * HEAD), F32)
    vshp = jax.ShapeDtypeStruct((1, H * HEAD), F32)
    return pl.pallas_call(
        body, name="hg_bwd", grid=(H // hb, nt),
        in_specs=_hg_specs(H, c0, True, nt) + [
            pl.BlockSpec((hb, None, HEAD, HEAD), lambda h, i: (h, nt - 1 - i, 0, 0)),
            pl.BlockSpec((HEAD, w), lambda h, i: (nt - 1 - i, do_c0 // hb + h)), vec, vec, vec],
        out_specs=[rblk, rblk, rblk, rblk, vec, vec, vec],
        out_shape=[shp, shp, shp, shp, vshp, vshp, vshp],
        scratch_shapes=[pltpu.VMEM((hb, HEAD, HEAD), F32)],
        compiler_params=_cp("parallel", "arbitrary"))(proj, proj, proj, proj, states, do, l0, l1, gain)


def _sg_chunk(u_parts, v_parts, gains, biases, wpos, bpos):
    W = sum(p.shape[1] for p in v_parts)
    C = v_parts[0].shape[0]
    v = [_gelu(p) for p in v_parts]
    mu = sum(jnp.sum(p, axis=-1, keepdims=True) for p in v) * (1.0 / W)
    xc = [p - mu for p in v]
    r = lax.rsqrt(sum(jnp.sum(p * p, axis=-1, keepdims=True) for p in xc) * (1.0 / W) + NORM_EPS)
    causal = _iota((C, C), 0) >= _iota((C, C), 1)
    out = []
    for up, p, gn, bs, w, b in zip(u_parts, xc, gains, biases, wpos, bpos):
        vn = p * r * gn + bs
        mixed = mm_nn(jnp.where(causal, w, 0.0), vn) + b
        out.append(_gelu(up) * mixed)
    return out


def _sg_fwd(zpre, vgain, vbias, wpos, bpos):
    T, W2 = zpre.shape
    W = W2 // 2
    G = wpos.shape[0]
    cg = W // G
    C = SG_CHUNK

    def body(z_ref, gn_ref, bs_ref, w_ref, b_ref, s_ref):
        sl = [slice(g * cg, (g + 1) * cg) for g in range(G)]
        out = _sg_chunk([z_ref[:, s] for s in sl], [z_ref[:, W + s.start:W + s.stop] for s in sl],
                        [gn_ref[:, s] for s in sl], [bs_ref[:, s] for s in sl],
                        [w_ref[g] for g in range(G)], [b_ref[g] for g in range(G)])
        for s, o in zip(sl, out):
            s_ref[:, s] = o.astype(s_ref.dtype)

    return pl.pallas_call(
        body, name="sg_fwd", grid=(T // C,),
        in_specs=[pl.BlockSpec((C, W2), lambda i: (i, 0)), _vec_spec(W), _vec_spec(W),
                  pl.BlockSpec((G, C, C), lambda i: (0, 0, 0)), pl.BlockSpec((G, C, 1), lambda i: (0, 0, 0))],
        out_specs=pl.BlockSpec((C, W), lambda i: (i, 0)),
        out_shape=jax.ShapeDtypeStruct((T, W), BF16), compiler_params=_cp("parallel"))(zpre, vgain, vbias, wpos, bpos)


def _sg_bwd(zpre, ds, vgain, vbias, wpos, bpos):
    T, W2 = zpre.shape
    W = W2 // 2
    G = wpos.shape[0]
    cg = W // G
    C = SG_CHUNK

    def body(z_ref, ds_ref, gn_ref, bs_ref, w_ref, b_ref, dz_ref, dgn_ref, dbs_ref, dw_ref, db_ref):
        @pl.when(pl.program_id(0) == 0)
        def _():
            dgn_ref[...] = jnp.zeros_like(dgn_ref)
            dbs_ref[...] = jnp.zeros_like(dbs_ref)
            dw_ref[...] = jnp.zeros_like(dw_ref)
            db_ref[...] = jnp.zeros_like(db_ref)

        sl = [slice(g * cg, (g + 1) * cg) for g in range(G)]
        _, vjp = jax.vjp(_sg_chunk, [z_ref[:, s] for s in sl], [z_ref[:, W + s.start:W + s.stop] for s in sl],
                         [gn_ref[:, s] for s in sl], [bs_ref[:, s] for s in sl],
                         [w_ref[g] for g in range(G)], [b_ref[g] for g in range(G)])
        du, dv, dgn, dbs, dw, db = vjp([ds_ref[:, s] for s in sl])
        for g, s in enumerate(sl):
            dz_ref[:, s] = du[g].astype(dz_ref.dtype)
            dz_ref[:, W + s.start:W + s.stop] = dv[g].astype(dz_ref.dtype)
            dgn_ref[:, s] += dgn[g]
            dbs_ref[:, s] += dbs[g]
            dw_ref[g] += dw[g]
            db_ref[g] += db[g]

    wspec = pl.BlockSpec((G, C, C), lambda i: (0, 0, 0))
    bspec = pl.BlockSpec((G, C, 1), lambda i: (0, 0, 0))
    return pl.pallas_call(
        body, name="sg_bwd", grid=(T // C,),
        in_specs=[pl.BlockSpec((C, W2), lambda i: (i, 0)), pl.BlockSpec((C, W), lambda i: (i, 0)),
                  _vec_spec(W), _vec_spec(W), wspec, bspec],
        out_specs=[pl.BlockSpec((C, W2), lambda i: (i, 0)), _vec_spec(W), _vec_spec(W), wspec, bspec],
        out_shape=[jax.ShapeDtypeStruct((T, W2), BF16), jax.ShapeDtypeStruct((1, W), F32),
                   jax.ShapeDtypeStruct((1, W), F32), jax.ShapeDtypeStruct((G, C, C), F32),
                   jax.ShapeDtypeStruct((G, C, 1), F32)],
        compiler_params=_cp("arbitrary"))(zpre, ds, vgain, vbias, wpos, bpos)


def _conv_tiles(T, F):
    return _pick(T, (512, 256, 128, 64, 32, 16, 8)), _pick(F, (512, 256, 128))


def _shift_down(cur, prev8, n, first_tile):
    bt = cur.shape[0]
    r = pltpu.roll(cur, n, 0)
    p = pltpu.roll(prev8, n, 0)
    p = jnp.where(first_tile, 0.0, p)
    head = jnp.where(_iota(p.shape, 0) < n, p, r[:8])
    return jnp.concatenate([head, r[8:]], axis=0) if bt > 8 else head


def _shift_up(cur, next8, n, last_tile):
    bt = cur.shape[0]
    r = pltpu.roll(cur, bt - n, 0)
    p = pltpu.roll(next8, 8 - n, 0)
    p = jnp.where(last_tile, 0.0, p)
    tail = jnp.where(_iota(p.shape, 0) >= 8 - n, p, r[bt - 8:])
    return jnp.concatenate([r[:bt - 8], tail], axis=0) if bt > 8 else tail


def _conv_apply(cur, prev8, w_ref, b, first_tile):
    return (b + w_ref[0:1, :] * _shift_down(cur, prev8, 2, first_tile)
            + w_ref[1:2, :] * _shift_down(cur, prev8, 1, first_tile) + w_ref[2:3, :] * cur)


def _conv_fwd(name, a, w, b):
    T, F2 = a.shape
    F = F2 // 2
    bt, cw = _conv_tiles(T, F)
    nf = F // cw
    r8 = bt // 8

    def body(g_ref, gp_ref, v_ref, vp_ref, wg_ref, wv_ref, bg_ref, bv_ref, u_ref):
        first = pl.program_id(0) == 0
        gate = _conv_apply(g_ref[...], gp_ref[...], wg_ref, bg_ref[...], first)
        val = _conv_apply(v_ref[...], vp_ref[...], wv_ref, bv_ref[...], first)
        u_ref[...] = (gate * _sigmoid(gate) * val).astype(u_ref.dtype)

    def cur(off):
        return pl.BlockSpec((bt, cw), lambda i, j: (i, j + off))

    def prev(off):
        return pl.BlockSpec((8, cw), lambda i, j: (jnp.maximum(i * r8 - 1, 0), j + off))

    def vec(rows, off):
        return pl.BlockSpec((rows, cw), lambda i, j: (0, j + off))

    return pl.pallas_call(
        body, name=name, grid=(T // bt, nf),
        in_specs=[cur(0), prev(0), cur(nf), prev(nf), vec(3, 0), vec(3, nf), vec(1, 0), vec(1, nf)],
        out_specs=pl.BlockSpec((bt, cw), lambda i, j: (i, j)),
        out_shape=jax.ShapeDtypeStruct((T, F), BF16),
        compiler_params=_cp("parallel", "parallel"))(a, a, a, a, w, w, b, b)


def _conv_bwd(name, a, du, w, b):
    T, F2 = a.shape
    F = F2 // 2
    bt, cw = _conv_tiles(T, F)
    nf = F // cw
    r8 = bt // 8
    last_blk = T // 8 - 1

    def body(g_ref, gp_ref, gn_ref, v_ref, vp_ref, vn_ref, du_ref, dun_ref, wg_ref, wv_ref, bg_ref, bv_ref,
             da_ref, dwg_ref, dwv_ref, dbg_ref, dbv_ref):
        i = pl.program_id(1)
        first = i == 0
        last = i == pl.num_programs(1) - 1

        def taps(cur, prev8, at_start):
            return _shift_down(cur, prev8, 2, at_start), _shift_down(cur, prev8, 1, at_start), cur

        def conv(t, w_ref, b_ref):
            return b_ref[...] + w_ref[0:1, :] * t[0] + w_ref[1:2, :] * t[1] + w_ref[2:3, :] * t[2]

        def act_bwd(gate, val, du_):
            sg = _sigmoid(gate)
            return du_ * val * (sg * (1.0 + gate * (1.0 - sg))), du_ * gate * sg

        g_cur, v_cur = g_ref[...], v_ref[...]
        tg = taps(g_cur, gp_ref[...], first)
        tv = taps(v_cur, vp_ref[...], first)
        dg, dv = act_bwd(conv(tg, wg_ref, bg_ref), conv(tv, wv_ref, bv_ref), du_ref[...])
        tgn = taps(gn_ref[...], g_cur[bt - 8:, :], False)
        tvn = taps(vn_ref[...], v_cur[bt - 8:, :], False)
        dgn, dvn = act_bwd(conv(tgn, wg_ref, bg_ref), conv(tvn, wv_ref, bv_ref), dun_ref[...])

        def conv_t(d, dn, w_ref):
            return w_ref[2:3, :] * d + w_ref[1:2, :] * _shift_up(d, dn, 1, last) + w_ref[0:1, :] * _shift_up(d, dn, 2, last)

        da_ref[0] = conv_t(dg, dgn, wg_ref).astype(da_ref.dtype)
        da_ref[1] = conv_t(dv, dvn, wv_ref).astype(da_ref.dtype)

        @pl.when(first)
        def _():
            dwg_ref[...] = jnp.zeros_like(dwg_ref)
            dwv_ref[...] = jnp.zeros_like(dwv_ref)
            dbg_ref[...] = jnp.zeros_like(dbg_ref)
            dbv_ref[...] = jnp.zeros_like(dbv_ref)

        for t in range(CONV_WIDTH):
            dwg_ref[t:t + 1, :] += jnp.sum(dg * tg[t], axis=0, keepdims=True)
            dwv_ref[t:t + 1, :] += jnp.sum(dv * tv[t], axis=0, keepdims=True)
        dbg_ref[...] += jnp.sum(dg, axis=0, keepdims=True)
        dbv_ref[...] += jnp.sum(dv, axis=0, keepdims=True)

    def cur(off):
        return pl.BlockSpec((bt, cw), lambda j, i: (i, j + off))

    def prev(off):
        return pl.BlockSpec((8, cw), lambda j, i: (jnp.maximum(i * r8 - 1, 0), j + off))

    def nxt(off):
        return pl.BlockSpec((8, cw), lambda j, i: (jnp.minimum((i + 1) * r8, last_blk), j + off))

    def vec(rows, off):
        return pl.BlockSpec((rows, cw), lambda j, i: (0, j + off))

    da, dwg, dwv, dbg, dbv = pl.pallas_call(
        body, name=name, grid=(nf, T // bt),
        in_specs=[cur(0), prev(0), nxt(0), cur(nf), prev(nf), nxt(nf), cur(0), nxt(0),
                  vec(3, 0), vec(3, nf), vec(1, 0), vec(1, nf)],
        out_specs=[pl.BlockSpec((2, bt, cw), lambda j, i: (0, i, j)), vec(3, 0), vec(3, 0), vec(1, 0), vec(1, 0)],
        out_shape=[jax.ShapeDtypeStruct((2, T, F), BF16), jax.ShapeDtypeStruct((3, F), F32), jax.ShapeDtypeStruct((3, F), F32),
                   jax.ShapeDtypeStruct((1, F), F32), jax.ShapeDtypeStruct((1, F), F32)],
        compiler_params=_cp("parallel", "arbitrary"))(a, a, a, a, a, a, du, du, w, w, b, b)
    return da, jnp.concatenate([dwg, dwv], axis=1), jnp.concatenate([dbg, dbv], axis=1)


def _ada_fwd(c_all, ada_w, ada_b):
    R, D = c_all.shape
    L, _, Ns = ada_w.shape
    tn = _pick(Ns, (512, 256, 128))

    def body(c_ref, w_ref, b_ref, o_ref):
        cv = c_ref[...]
        cond = cv * _sigmoid(cv)
        o_ref[...] = _dg(cond, w_ref[...], 1, 0) + b_ref[...]

    return pl.pallas_call(
        body, name="ada_fwd", grid=(L, Ns // tn),
        in_specs=[pl.BlockSpec((R, D), lambda l, j: (0, 0)), pl.BlockSpec((None, D, tn), lambda l, j: (l, 0, j)),
                  pl.BlockSpec((None, 1, tn), lambda l, j: (l, 0, j))],
        out_specs=pl.BlockSpec((None, R, tn), lambda l, j: (l, 0, j)),
        out_shape=jax.ShapeDtypeStruct((L, R, Ns), F32), compiler_params=_cp("parallel", "parallel"))(c_all, ada_w, ada_b)


def _adam_math(w, g, m, v):
    m2 = ADAM_B1 * m + (1.0 - ADAM_B1) * g
    v2 = ADAM_B2 * v + (1.0 - ADAM_B2) * (g * g)
    m_hat = m2 / (1.0 - ADAM_B1 ** ADAM_STEP)
    v_hat = v2 / (1.0 - ADAM_B2 ** ADAM_STEP)
    delta = -ADAM_LR * (m_hat / (jnp.sqrt(v_hat) + ADAM_EPS) + ADAM_WD * w)
    return delta, m2, v2


def _ada_grad_adam(c_all_t, dmod, w, m, v):
    D, R = c_all_t.shape
    L, _, Ns = dmod.shape
    tr = _rows_within(D, Ns * 4, 1 << 20)

    def body(c_ref, d_ref, w_ref, m_ref, v_ref, g_ref, dl_ref, m2_ref, v2_ref):
        cv = c_ref[...]
        g = _dg(cv * _sigmoid(cv), d_ref[...], 1, 0)
        g_ref[...] = g
        dl_ref[...], m2_ref[...], v2_ref[...] = _adam_math(w_ref[...], g, m_ref[...], v_ref[...])

    big = pl.BlockSpec((None, tr, Ns), lambda l, i: (l, i, 0))
    shp = jax.ShapeDtypeStruct((L, D, Ns), F32)
    return pl.pallas_call(
        body, name="ada_grad_adam", grid=(L, D // tr),
        in_specs=[pl.BlockSpec((tr, R), lambda l, i: (i, 0)), pl.BlockSpec((None, R, Ns), lambda l, i: (l, 0, 0)), big, big, big],
        out_specs=[big] * 4, out_shape=[shp] * 4, compiler_params=_cp("parallel", "parallel"))(c_all_t, dmod, w, m, v)


def _adam(name, w, g, m, v):
    R, C = w.shape
    tr = _rows_within(R, C * 4, 1 << 21)

    def body(w_ref, g_ref, m_ref, v_ref, dl_ref, m2_ref, v2_ref):
        dl_ref[...], m2_ref[...], v2_ref[...] = _adam_math(w_ref[...], g_ref[...], m_ref[...], v_ref[...])

    blk = pl.BlockSpec((tr, C), lambda i: (i, 0))
    shp = jax.ShapeDtypeStruct((R, C), F32)
    return pl.pallas_call(body, name=name, grid=(R // tr,), in_specs=[blk] * 4, out_specs=[blk] * 3,
                          out_shape=[shp] * 3, compiler_params=_cp("parallel"))(w, g, m, v)


def _cast_into_rows(name, w, chip, after=None):
    _, R, C = w.shape
    tr = _rows_within(R, C * 4, 1 << 22)
    extra = [] if after is None else [after]

    def body(chip_ref, w_ref, *rest):
        o_ref = rest[-1]
        o_ref[...] = w_ref[...].astype(BF16)

    grid_spec = pltpu.PrefetchScalarGridSpec(
        num_scalar_prefetch=1, grid=(2, R // tr),
        in_specs=[pl.BlockSpec((None, tr, C), lambda h, i, s: (h, i, 0))] + [ANY] * len(extra),
        out_specs=pl.BlockSpec((None, tr, C), lambda h, i, s: (2 * s[0] + h, i, 0)))
    return pl.pallas_call(body, name=name, grid_spec=grid_spec, out_shape=jax.ShapeDtypeStruct((N_DEV, R, C), BF16),
                          compiler_params=_cp("arbitrary", "arbitrary"))(chip.reshape(1).astype(jnp.int32), w, *extra)


def _add_pairs(name, eight, from_sib, c):
    _, R, C = from_sib.shape
    tr = _rows_within(R, C * 2, 1 << 21)

    def body(c_ref, a_ref, b_ref, o_ref):
        o_ref[...] = (a_ref[...].astype(F32) + b_ref[...].astype(F32)).astype(o_ref.dtype)

    blk = pl.BlockSpec((None, tr, C), lambda j, i, s: (j, i, 0))
    grid_spec = pltpu.PrefetchScalarGridSpec(
        num_scalar_prefetch=1, grid=(4, R // tr),
        in_specs=[pl.BlockSpec((None, tr, C), lambda j, i, s: (2 * j + s[0], i, 0)), blk], out_specs=blk)
    return pl.pallas_call(body, name=name, grid_spec=grid_spec, out_shape=jax.ShapeDtypeStruct(from_sib.shape, BF16),
                          compiler_params=_cp("arbitrary", "arbitrary"))(c.reshape(1).astype(jnp.int32), eight, from_sib)


def _sum_into_pair(name, own, landed, slot, chip):
    n, R, C = landed.shape
    tr = _rows_within(R, (n + 1) * C * landed.dtype.itemsize, 1 << 23)

    def body(idx_ref, own_ref, x_ref, o_ref):
        mine = idx_ref[1]
        acc = None
        for j in range(n):
            part = jnp.where(mine == j, own_ref[...], x_ref[j]).astype(F32)
            acc = part if acc is None else acc + part
        o_ref[...] = acc

    grid_spec = pltpu.PrefetchScalarGridSpec(
        num_scalar_prefetch=1, grid=(R // tr,),
        in_specs=[pl.BlockSpec((None, tr, C), lambda i, s: (s[1], i, 0)), pl.BlockSpec((n, tr, C), lambda i, s: (0, i, 0))],
        out_specs=pl.BlockSpec((None, tr, C), lambda i, s: (s[0], i, 0)))
    idx = jnp.stack([slot, chip]).astype(jnp.int32)
    return pl.pallas_call(body, name=name, grid_spec=grid_spec, out_shape=jax.ShapeDtypeStruct((2, R, C), F32),
                          compiler_params=_cp("arbitrary"))(idx, own, landed)


def _sum_leading(name, a, out_dtype=F32):
    n, R, C = a.shape
    tr = _rows_within(R, n * C * a.dtype.itemsize, 1 << 23)

    def body(a_ref, o_ref):
        acc = a_ref[0].astype(F32)
        for j in range(1, n):
            acc = acc + a_ref[j].astype(F32)
        o_ref[...] = acc.astype(o_ref.dtype)

    return pl.pallas_call(body, name=name, grid=(R // tr,), in_specs=[pl.BlockSpec((n, tr, C), lambda i: (0, i, 0))],
                          out_specs=pl.BlockSpec((tr, C), lambda i: (i, 0)),
                          out_shape=jax.ShapeDtypeStruct((R, C), out_dtype), compiler_params=_cp("parallel"))(a)


def _place():
    return lax.axis_index("x"), lax.axis_index("y"), lax.axis_index("c")


def _all_gather(name, blocks, halves=False, after=None):
    n = len(blocks)
    shapes = [b.shape[1:] if halves else b.shape for b in blocks]
    extra = [] if after is None else [after]

    def body(*refs):
        ins, outs = refs[:n], refs[n + len(extra):2 * n + len(extra)]
        send_sems, recv_sems, local_sems = refs[2 * n + len(extra):]
        x, y, c = _place()
        me, sibling = (x, y, c), (x, y, 1 - c)
        chips = [(1 - x, y), (x, 1 - y), (1 - x, 1 - y)]

        def rows(a, px, py, pc):
            return outs[a].at[4 * px + 2 * py + pc]

        def copy(a, k, block, to, src=None):
            return pltpu.make_async_remote_copy(
                src_ref=rows(a, *block) if src is None else src, dst_ref=rows(a, *block),
                send_sem=send_sems.at[7 * a + k], recv_sem=recv_sems.at[7 * a + k],
                device_id=to, device_id_type=MESH)

        started = []
        mine = []
        for a in range(n):
            src = ins[a].at[c] if halves else ins[a]
            mine.append(pltpu.make_async_copy(src, rows(a, *me), local_sems.at[a]))
            mine[-1].start()
            first = [copy(a, 0, me, sibling, src=src)]
            first += [copy(a, 1 + j, me, (*chip, c), src=src) for j, chip in enumerate(chips)]
            for cp in first:
                cp.start()
            started += first
        for j, chip in enumerate(chips):
            for a in range(n):
                copy(a, 1 + j, (*chip, c), me).wait_recv()
                passed = copy(a, 4 + j, (*chip, c), sibling)
                passed.start()
                started.append(passed)
        for a in range(n):
            copy(a, 0, sibling, me).wait_recv()
            for j, chip in enumerate(chips):
                copy(a, 4 + j, (*chip, 1 - c), me).wait_recv()
        for cp in started:
            cp.wait_send()
        for cp in mine:
            cp.wait()

    return pl.pallas_call(
        body, name=name, in_specs=[ANY] * (n + len(extra)), out_specs=[ANY] * n,
        out_shape=[jax.ShapeDtypeStruct((N_DEV,) + tuple(s), b.dtype) for s, b in zip(shapes, blocks)],
        scratch_shapes=[pltpu.SemaphoreType.DMA((7 * n,)), pltpu.SemaphoreType.DMA((7 * n,)),
                        pltpu.SemaphoreType.DMA((n,))],
    )(*blocks, *extra)


def _share_halves(name, arrays):
    n = len(arrays)

    def body(*refs):
        ins, outs = refs[:n], refs[n:2 * n]
        send_sems, recv_sems = refs[2 * n:]
        x, y, c = _place()
        started = []
        for a in range(n):
            cp = pltpu.make_async_remote_copy(src_ref=ins[a].at[c], dst_ref=outs[a].at[c], send_sem=send_sems.at[a],
                                              recv_sem=recv_sems.at[a], device_id=(x, y, 1 - c), device_id_type=MESH)
            cp.start()
            started.append(cp)
        for a in range(n):
            started[a].wait_send()
            pltpu.make_async_remote_copy(src_ref=ins[a].at[1 - c], dst_ref=outs[a].at[1 - c], send_sem=send_sems.at[a],
                                         recv_sem=recv_sems.at[a], device_id=(x, y, 1 - c), device_id_type=MESH).wait_recv()

    return pl.pallas_call(
        body, name=name, in_specs=[ANY] * n, out_specs=[ANY] * n,
        out_shape=[jax.ShapeDtypeStruct(a.shape, a.dtype) for a in arrays],
        input_output_aliases={a: a for a in range(n)},
        scratch_shapes=[pltpu.SemaphoreType.DMA((n,)), pltpu.SemaphoreType.DMA((n,))],
    )(*arrays)


HBM = pl.BlockSpec(memory_space=pltpu.HBM)
SEM = pl.BlockSpec(memory_space=pltpu.SEMAPHORE)
EFFECT = pltpu.SideEffectType.DATAFLOW_SIDE_EFFECTING


COPIES_PER_ARRAY = {"rows": 3, "fill": 3, "parts": 3, "halves": 4}


def _chip_copies(kind, srcs, dsts, send_sems, recv_sems):
    x, y, c = _place()
    mine = 2 * x + y
    per = COPIES_PER_ARRAY[kind]
    sends, arrivals = [], []
    for a in range(len(srcs)):
        if kind == "halves":
            for j in range(N_CHIP):
                cp = pltpu.make_async_remote_copy(
                    src_ref=srcs[a].at[2 * j + 1 - c], dst_ref=dsts[a].at[j], send_sem=send_sems.at[per * a + j],
                    recv_sem=recv_sems.at[per * a + j], device_id=(x, y, 1 - c), device_id_type=MESH)
                sends.append(cp)
                arrivals.append(cp)
            continue
        for k, (px, py) in enumerate([(1 - x, y), (x, 1 - y), (1 - x, 1 - y)]):
            other = 2 * px + py
            if kind == "fill":
                cp = dict(send_sem=send_sems.at[per * a + k], recv_sem=recv_sems.at[per * a + k], device_id=(x, y, 1 - c),
                          device_id_type=MESH)
                sends.append(pltpu.make_async_remote_copy(src_ref=srcs[a].at[2 * other + c], dst_ref=dsts[a].at[2 * other + c], **cp))
                arrivals.append(pltpu.make_async_remote_copy(src_ref=srcs[a].at[2 * other + c],
                                                             dst_ref=dsts[a].at[2 * other + 1 - c], **cp))
                continue
            if kind == "rows":
                src, dst, lands = srcs[a].at[2 * mine + c], dsts[a].at[2 * mine + c], dsts[a].at[2 * other + c]
            else:
                src, dst, lands = srcs[a].at[other], dsts[a].at[mine], dsts[a].at[other]
            sem = dict(send_sem=send_sems.at[per * a + k], recv_sem=recv_sems.at[per * a + k], device_id=(px, py, c),
                       device_id_type=MESH)
            sends.append(pltpu.make_async_remote_copy(src_ref=src, dst_ref=dst, **sem))
            arrivals.append(pltpu.make_async_remote_copy(src_ref=src, dst_ref=lands, **sem))
    return sends, arrivals


def _chips_start(name, kind, srcs, dsts=None, after=None):
    n = len(srcs)
    bufs = list(srcs) + (list(dsts) if dsts is not None else [])
    nb = len(bufs)
    extra = [] if after is None else [after]

    def body(*refs):
        ins = refs[:nb]
        send_sems, recv_sems = refs[nb + len(extra)], refs[nb + len(extra) + 1]
        token = refs[-1]
        sends, _ = _chip_copies(kind, ins[:n], ins[n:] if dsts is not None else ins[:n], send_sems, recv_sems)
        for cp in sends:
            cp.start()
        token[...] = jnp.zeros_like(token)

    out = pl.pallas_call(
        body, name=name,
        out_shape=(pltpu.SemaphoreType.DMA((COPIES_PER_ARRAY[kind] * n,)), pltpu.SemaphoreType.DMA((COPIES_PER_ARRAY[kind] * n,)),
                   *[pltpu.HBM(b.shape, b.dtype) for b in bufs], jax.ShapeDtypeStruct((8, HEAD), F32)),
        in_specs=(HBM,) * nb + (ANY,) * len(extra),
        out_specs=(SEM, SEM) + (HBM,) * nb + (pl.BlockSpec(memory_space=pltpu.VMEM),),
        input_output_aliases={i: 2 + i for i in range(nb)},
        compiler_params=pltpu.CompilerParams(has_side_effects=EFFECT),
    )(*[pltpu.with_memory_space_constraint(b, pltpu.HBM) for b in bufs], *extra)
    return out[0], out[1], list(out[2:2 + nb]), out[-1]


def _chips_wait(name, kind, n, send_sems, recv_sems, bufs, after):
    nb = len(bufs)

    def body(*refs):
        ins = refs[:nb]
        s_sems, r_sems = refs[nb], refs[nb + 1]
        sends, arrivals = _chip_copies(kind, ins[:n], ins[n:] if nb > n else ins[:n], s_sems, r_sems)
        for cp in sends:
            cp.wait_send()
        for cp in arrivals:
            cp.wait_recv()

    return list(pl.pallas_call(
        body, name=name, out_shape=tuple(pltpu.HBM(b.shape, b.dtype) for b in bufs),
        in_specs=(HBM,) * nb + (SEM, SEM, pl.BlockSpec(memory_space=pl.ANY)), out_specs=(HBM,) * nb,
        input_output_aliases={i: i for i in range(nb)},
        compiler_params=pltpu.CompilerParams(has_side_effects=EFFECT),
    )(*bufs, send_sems, recv_sems, after))


def _fill_from_sibling(name, arrays):
    n = len(arrays)

    def body(*refs):
        ins, outs = refs[:n], refs[n:2 * n]
        send_sems, recv_sems = refs[2 * n:]
        x, y, c = _place()
        sends, arrivals = [], []
        for a in range(n):
            for k, (px, py) in enumerate([(1 - x, y), (x, 1 - y), (1 - x, 1 - y)]):
                sem = dict(send_sem=send_sems.at[3 * a + k], recv_sem=recv_sems.at[3 * a + k], device_id=(x, y, 1 - c),
                           device_id_type=MESH)
                row = 2 * (2 * px + py)
                sends.append(pltpu.make_async_remote_copy(src_ref=ins[a].at[row + c], dst_ref=outs[a].at[row + c], **sem))
                arrivals.append(pltpu.make_async_remote_copy(src_ref=ins[a].at[row + c], dst_ref=outs[a].at[row + 1 - c], **sem))
        for cp in sends:
            cp.start()
        for cp in sends:
            cp.wait_send()
        for cp in arrivals:
            cp.wait_recv()

    return pl.pallas_call(
        body, name=name, in_specs=[ANY] * n, out_specs=[ANY] * n,
        out_shape=[jax.ShapeDtypeStruct(a.shape, a.dtype) for a in arrays],
        input_output_aliases={a: a for a in range(n)},
        scratch_shapes=[pltpu.SemaphoreType.DMA((3 * n,)), pltpu.SemaphoreType.DMA((3 * n,))],
    )(*arrays)


def kernel(x, c, ada_w, ada_b, mix_norm, ffn_norm, par_w_in, par_w_out, hg_lb_logits, hg_out_norm, sg_w_in, sg_v_gain, sg_v_bias, sg_w_pos, sg_b_pos, sg_w_out, ffn_up, ffn_conv_w, ffn_conv_b, ffn_down, final_norm, loss_target, m_ada_w, m_ada_b, m_mix_norm, m_ffn_norm, m_par_w_in, m_par_w_out, m_hg_lb_logits, m_hg_out_norm, m_sg_w_in, m_sg_v_gain, m_sg_v_bias, m_sg_w_pos, m_sg_b_pos, m_sg_w_out, m_ffn_up, m_ffn_conv_w, m_ffn_conv_b, m_ffn_down, m_final_norm, v_ada_w, v_ada_b, v_mix_norm, v_ffn_norm, v_par_w_in, v_par_w_out, v_hg_lb_logits, v_hg_out_norm, v_sg_w_in, v_sg_v_gain, v_sg_v_bias, v_sg_w_pos, v_sg_b_pos, v_sg_w_out, v_ffn_up, v_ffn_conv_w, v_ffn_conv_b, v_ffn_down, v_final_norm):
    names = ["ada_w", "ada_b", "mix_norm", "ffn_norm", "par_w_in", "par_w_out", "hg_lb_logits", "hg_out_norm", "sg_w_in",
             "sg_v_gain", "sg_v_bias", "sg_w_pos", "sg_b_pos", "sg_w_out", "ffn_up", "ffn_conv_w", "ffn_conv_b",
             "ffn_down", "final_norm"]
    W = dict(zip(names, [ada_w, ada_b, mix_norm, ffn_norm, par_w_in, par_w_out, hg_lb_logits, hg_out_norm, sg_w_in,
                         sg_v_gain, sg_v_bias, sg_w_pos, sg_b_pos, sg_w_out, ffn_up, ffn_conv_w, ffn_conv_b, ffn_down,
                         final_norm]))
    M = dict(zip(names, [m_ada_w, m_ada_b, m_mix_norm, m_ffn_norm, m_par_w_in, m_par_w_out, m_hg_lb_logits, m_hg_out_norm,
                         m_sg_w_in, m_sg_v_gain, m_sg_v_bias, m_sg_w_pos, m_sg_b_pos, m_sg_w_out, m_ffn_up, m_ffn_conv_w,
                         m_ffn_conv_b, m_ffn_down, m_final_norm]))
    V = dict(zip(names, [v_ada_w, v_ada_b, v_mix_norm, v_ffn_norm, v_par_w_in, v_par_w_out, v_hg_lb_logits, v_hg_out_norm,
                         v_sg_w_in, v_sg_v_gain, v_sg_v_bias, v_sg_w_pos, v_sg_b_pos, v_sg_w_out, v_ffn_up, v_ffn_conv_w,
                         v_ffn_conv_b, v_ffn_down, v_final_norm]))

    x = x[0]
    target = loss_target[0]
    T, D = x.shape
    ix, iy, ic = _place()
    chip = 2 * ix + iy
    dev = 2 * chip + ic
    H = hg_out_norm.shape[1]
    SBW = H * HEAD
    NA = ada_w.shape[2]
    F2s = ffn_up.shape[2]
    F2 = N_CHIP * F2s
    SGW = sg_w_out.shape[1] * N_CHIP
    G = sg_w_pos.shape[1]

    shards = [par_w_in[0], par_w_out[0], sg_w_in[0], sg_w_out[0], ffn_up[0], ffn_up[1], ffn_down[0], ffn_down[1]]
    kinds = ["col", "row", "col", "row", "col", "col", "row", "row"]
    halves = [w.reshape(2, w.shape[0] // 2, w.shape[1]) for w in shards]
    groups = {"a": [0], "b": [1, 4, 6], "c": [2, 3, 5, 7]}
    rows8 = {0: _cast_into_rows("cast_w", halves[0], chip)}
    started = {}

    def as_weights(g, bufs):
        out = {}
        for i, g8 in zip(groups[g], bufs):
            K, N = shards[i].shape
            out[i] = g8.reshape(N_CHIP, K, N) if kinds[i] == "col" else g8.reshape(N_CHIP * K, N)
        return out

    def weights_landed(g, after):
        send_sems, recv_sems, bufs, _ = started[g]
        bufs = _chips_wait("gather_wait_" + g, "rows", len(bufs), send_sems, recv_sems, bufs, after)
        return _chips_start("gather_fill_start_" + g, "fill", bufs)

    def weights_of(g, filling, after):
        send_sems, recv_sems, bufs, _ = filling
        return as_weights(g, _chips_wait("gather_fill_wait_" + g, "fill", len(bufs), send_sems, recv_sems, bufs, after))

    n_cw = ffn_conv_w.size
    n_sv = sg_v_gain.size
    c_all, small_all = _all_gather("gather_small", [c, _pack_rows([ffn_conv_w, sg_v_gain, sg_v_bias])])
    c_all = c_all.reshape(N_DEV, D)
    small_all = small_all.reshape(N_CHIP, 2, -1)[:, 0]
    conv_w_full = small_all[:, :n_cw].reshape(N_CHIP, 2, CONV_WIDTH, F2s).transpose(1, 2, 0, 3).reshape(2, CONV_WIDTH, F2)
    sg_gain_full = small_all[:, n_cw:n_cw + n_sv].reshape(1, SGW)
    sg_bias_full = small_all[:, n_cw + n_sv:n_cw + 2 * n_sv].reshape(1, SGW)

    c_pad = jnp.pad(c_all, ((0, 16 - N_DEV), (0, 0)))
    ada_b_sh = lax.dynamic_slice(ada_b, (0, chip * NA), (2, NA)).reshape(2, 1, NA)
    mod_sh = _ada_fwd(c_pad, ada_w, ada_b_sh)
    mod_all, = _all_gather("gather_mod", [mod_sh[:, :N_DEV]])
    mod_all = mod_all.reshape(N_CHIP, 2, 2, N_DEV, NA)[:, 0]
    mod = lax.dynamic_index_in_dim(mod_all, dev, axis=2, keepdims=False)
    mod = mod.transpose(1, 0, 2).reshape(2, 6, D)
    mods = [[mod[l, k].reshape(1, D) for k in range(6)] for l in range(2)]
    started["a"] = _chips_start("gather_start_a", "rows", [rows8[0]], after=mod)
    for i in range(1, len(shards)):
        rows8[i] = _cast_into_rows("cast_w", halves[i], chip, after=started["a"][3])

    vec = lambda a: a.reshape(1, -1)
    l0 = vec(hg_lb_logits[0])
    l1 = vec(hg_lb_logits[1])
    hg_gain = vec(hg_out_norm[0])
    wpos = sg_w_pos[0]
    bpos = sg_b_pos[0].reshape(G, SG_CHUNK, 1)
    conv_b = [vec(ffn_conv_b[l]) for l in range(2)]

    sh1, sc1, g1, sh2, sc2, g2 = mods[0]
    send_a, recv_a, bufs_a, _ = started["a"]
    bufs_a = _chips_wait("gather_wait_a", "rows", 1, send_a, recv_a, bufs_a, rows8[len(shards) - 1])
    bufs_a = _fill_from_sibling("gather_fill_a", bufs_a)
    started["b"] = _chips_start("gather_start_b", "rows", [rows8[i] for i in groups["b"]], after=bufs_a[0])
    start_token = started["a"][3][0, 0] + started["b"][3][0, 0]
    w_in = as_weights("a", bufs_a)[0]
    h0 = _normmod_fwd("norm_mix0", x, vec(mix_norm[0]) + start_token, sc1, sh1)
    proj = _mm_nn("mm_par_in", h0, w_in)
    o_sb, sb_tot = _sb_fwd(proj, H)
    filling_b = weights_landed("b", o_sb)
    started["c"] = _chips_start("gather_start_c", "rows", [rows8[i] for i in groups["c"]], after=o_sb)
    o_hg, hg_states = _hg_fwd(proj, l0 + filling_b[3][0:1, 0:1] + started["c"][3][0:1, 0:1], l1, hg_gain, H, 3 * H)
    o_cat = jnp.concatenate([o_sb, o_hg], axis=1)
    wb = weights_of("b", filling_b, o_cat)
    w_out, wup, wdn = wb[1], [wb[4], None], [wb[6], None]
    y0 = _mm_nn("mm_par_out", o_cat, w_out)
    x1, h0f = _res_normmod_fwd("res_norm_ffn0", x, y0, g1, vec(ffn_norm[0]), sc2, sh2)
    a0 = _mm_nn("mm_up0", h0f, wup[0])
    u0 = _conv_fwd("conv_fwd0", a0, conv_w_full[0], conv_b[0])
    filling_c = weights_landed("c", u0)
    f0 = _mm_nn("mm_down0", u0, wdn[0])
    sh1b, sc1b, g1b, sh2b, sc2b, g2b = mods[1]
    x2, h1 = _res_normmod_fwd("res_norm_mix1", x1, f0, g2, vec(mix_norm[1]) + filling_c[3][0:1, 0:1], sc1b, sh1b)
    wc = weights_of("c", filling_c, h1)
    wsg_in, wsg_out, wup[1], wdn[1] = wc[2], wc[3], wc[5], wc[7]
    zpre = _mm_nn("mm_sg_in", h1, wsg_in)
    s1 = _sg_fwd(zpre, sg_gain_full, sg_bias_full, wpos, bpos)
    y1 = _mm_nn("mm_sg_out", s1, wsg_out)
    x3, h1f = _res_normmod_fwd("res_norm_ffn1", x2, y1, g1b, vec(ffn_norm[1]), sc2b, sh2b)
    a1 = _mm_nn("mm_up1", h1f, wup[1])
    u1 = _conv_fwd("conv_fwd1", a1, conv_w_full[1], conv_b[1])
    f1 = _mm_nn("mm_down1", u1, wdn[1])
    loss_sum, dx, df1, dg2b, d_final = _final_fwd_bwd(x3, f1, g2b, vec(final_norm), target)
    loss = lax.psum(loss_sum[0, 0], ("x", "y", "c"))

    def reduce_start(tag, idx, grads):
        eights = [g.reshape((N_DEV, -1, g.shape[-1])) for g in grads]
        landing = [lax.empty((N_CHIP,) + e.shape[1:], e.dtype) for e in eights]
        send_sems, recv_sems, bufs, token = _chips_start("grads_sibling_start_" + tag, "halves", eights, landing)
        return (tag, idx, send_sems, recv_sems, bufs), token[0:1, 0:1]

    def reduce_cross(state, after):
        tag, idx, send_sems, recv_sems, bufs = state
        n = len(idx)
        bufs = _chips_wait("grads_sibling_wait_" + tag, "halves", n, send_sems, recv_sems, bufs, after)
        pair = [_add_pairs("add_pair", e, r, ic) for e, r in zip(bufs[:n], bufs[n:])]
        landing = [lax.empty(p.shape, p.dtype) for p in pair]
        send_sems, recv_sems, bufs, token = _chips_start("grads_start_" + tag, "parts", pair, landing)
        return (tag, idx, send_sems, recv_sems, bufs), token[0:1, 0:1]

    def reduce_finish(state, after):
        tag, idx, send_sems, recv_sems, bufs = state
        n = len(idx)
        bufs = _chips_wait("grads_wait_" + tag, "parts", n, send_sems, recv_sems, bufs, after)
        halves = [_sum_into_pair("sum_chips", p, x_, ic, chip) for p, x_ in zip(bufs[:n], bufs[n:])]
        both = _share_halves("grads_share_" + tag, halves)
        return {i: b.reshape(shards[i].shape) for i, b in zip(idx, both)}

    def ffn_bwd(l, dfl, u, a, hf):
        g_dn = _mm_tn("mm_g_down", u, dfl)
        du = _mm_nt("mm_d_u", dfl, wdn[l])
        da, dcw, dcb = _conv_bwd("conv_bwd", a, du, conv_w_full[l], conv_b[l])
        g_up = _mm_tn("mm_g_up", hf, da, chunks=N_CHIP)
        dh = _mm_nt("mm_d_hf", da, wup[l])
        return g_dn, g_up, dcw, dcb, dh

    g_dn1, g_up1, dcw1, dcb1, dh1f = ffn_bwd(1, df1, u1, a1, h1f)
    red1, tok = reduce_start("1", [5, 7], [g_up1, g_dn1])
    dx, dgn_f1, dsc2b, dsh2b, dy1, dg1b = _block_bwd("bwd_ffn1", dx, dh1f, x3, vec(ffn_norm[1]) + tok, sc2b, sh2b, y1, g1b)
    g_sg_out = _mm_tn("mm_g_sg_out", s1, dy1)
    ds1 = _mm_nt("mm_d_s", dy1, wsg_out)
    dzpre, dsg_gain, dsg_bias, dwpos, dbpos = _sg_bwd(zpre, ds1, sg_gain_full, sg_bias_full, wpos, bpos)
    red1, tok_x = reduce_cross(red1, dzpre)
    g_sg_in = _mm_tn("mm_g_sg_in", h1, dzpre, chunks=N_CHIP)
    dh1 = _mm_nt("mm_d_h1", dzpre, wsg_in)
    red2, tok = reduce_start("2", [2, 3], [g_sg_in, g_sg_out])
    dx, dgn_m1, dsc1b, dsh1b, df0, dg2 = _block_bwd("bwd_mix1", dx, dh1, x2, vec(mix_norm[1]) + tok + tok_x, sc1b, sh1b, f0, g2)
    g_dn0, g_up0, dcw0, dcb0, dh0f = ffn_bwd(0, df0, u0, a0, h0f)
    red2, tok_x = reduce_cross(red2, dh0f)
    red3, tok = reduce_start("3", [4, 6], [g_up0, g_dn0])
    dx, dgn_f0, dsc2, dsh2, dy0, dg1 = _block_bwd("bwd_ffn0", dx, dh0f, x1, vec(ffn_norm[0]) + tok + tok_x, sc2, sh2, y0, g1)
    g_out = _mm_tn("mm_g_par_out", o_cat, dy0)
    do = _mm_nt("mm_d_o", dy0, w_out)
    red3, tok_x = reduce_cross(red3, do)
    dhq, dhf, dhi, dhg, dl0, dl1, dhg_gain = _hg_bwd(proj, hg_states, do, l0 + tok_x, l1, hg_gain, H, 3 * H, H)
    dq, dk, dv = _sb_bwd(proj, do, sb_tot, H)
    dproj = jnp.concatenate([dq, dk, dv, dhq, dhf, dhi, dhg], axis=1).astype(BF16)
    g_in = _mm_tn("mm_g_par_in", h0, dproj, chunks=N_CHIP)
    red4, tok = reduce_start("4", [0, 1], [g_in, g_out])
    dh0 = _mm_nt("mm_d_h0", dproj, w_in)
    grad_x, dgn_m0, dsc1, dsh1 = _block_bwd("bwd_mix0", dx, dh0, x, vec(mix_norm[0]) + tok, sc1, sh1)
    red4, tok_x = reduce_cross(red4, grad_x)

    G_, delta, new_m, new_v = {}, {}, {}, {}

    def adam_on(nme):
        shp = W[nme].shape
        r2 = lambda a: a.reshape(-1, shp[-1])
        d_, m_, v_ = _adam("adam_" + nme, r2(W[nme]), r2(G_[nme]), r2(M[nme]), r2(V[nme]))
        delta[nme], new_m[nme], new_v[nme] = d_.reshape(shp), m_.reshape(shp), v_.reshape(shp)

    g_shards = {}
    for state in (red1, red2, red3):
        g_shards.update(reduce_finish(state, red4[4][0]))
    G_["sg_w_in"] = g_shards[2][None]
    G_["sg_w_out"] = g_shards[3][None]
    G_["ffn_up"] = jnp.stack([g_shards[4], g_shards[5]])
    G_["ffn_down"] = jnp.stack([g_shards[6], g_shards[7]])
    for nme in ["sg_w_in", "sg_w_out", "ffn_up", "ffn_down"]:
        adam_on(nme)

    dmod = jnp.concatenate([dsh1, dsc1, dg1, dsh2, dsc2, dg2, dsh1b, dsc1b, dg1b, dsh2b, dsc2b, dg2b], axis=1)
    parts = [dmod, dgn_m0, dgn_m1, dgn_f0, dgn_f1, dl0, dl1, dhg_gain, dsg_gain, dsg_bias, dwpos, dbpos,
             dcw0, dcw1, dcb0, dcb1, d_final]
    sizes = [p.size for p in parts]
    packed = _pack_rows(parts)
    packed_all, = _all_gather("gather_small_grads", [packed], after=new_v["ffn_down"])
    summed = _sum_leading("sum_small_grads", packed_all).reshape(-1)
    offs = [0]
    for s in sizes:
        offs.append(offs[-1] + s)
    red = [summed[offs[i]:offs[i + 1]] for i in range(len(parts))]
    (r_dmod, r_gm0, r_gm1, r_gf0, r_gf1, r_l0, r_l1, r_hgain, r_sgain, r_sbias, r_wpos, r_bpos,
     r_cw0, r_cw1, r_cb0, r_cb1, r_final) = red
    n_mod = sizes[0]
    dmod_all = packed_all.reshape(N_DEV, -1)[:, :n_mod].reshape(N_DEV, 2, 6 * D)

    G_["ada_b"] = r_dmod.reshape(2, 6 * D)
    G_["mix_norm"] = jnp.stack([r_gm0, r_gm1])
    G_["ffn_norm"] = jnp.stack([r_gf0, r_gf1])
    G_["hg_lb_logits"] = jnp.stack([r_l0, r_l1])
    G_["hg_out_norm"] = r_hgain.reshape(hg_out_norm.shape)
    G_["sg_v_gain"] = lax.dynamic_slice(r_sgain, (chip * n_sv,), (n_sv,)).reshape(sg_v_gain.shape)
    G_["sg_v_bias"] = lax.dynamic_slice(r_sbias, (chip * n_sv,), (n_sv,)).reshape(sg_v_bias.shape)
    G_["sg_w_pos"] = r_wpos.reshape(sg_w_pos.shape)
    G_["sg_b_pos"] = r_bpos.reshape(sg_b_pos.shape)
    cw_full = jnp.stack([r_cw0.reshape(CONV_WIDTH, F2), r_cw1.reshape(CONV_WIDTH, F2)])
    G_["ffn_conv_w"] = lax.dynamic_slice(cw_full, (0, 0, chip * F2s), (2, CONV_WIDTH, F2s))
    G_["ffn_conv_b"] = jnp.stack([r_cb0, r_cb1])
    G_["final_norm"] = r_final

    c_t = jnp.pad(c_all, ((0, HEAD - N_DEV), (0, 0))).T
    dmod_sh = lax.dynamic_slice(dmod_all.transpose(1, 0, 2), (0, 0, chip * NA), (2, N_DEV, NA))
    dmod_sh = jnp.pad(dmod_sh, ((0, 0), (0, HEAD - N_DEV), (0, 0)))
    G_["ada_w"], delta["ada_w"], new_m["ada_w"], new_v["ada_w"] = _ada_grad_adam(c_t, dmod_sh, ada_w, m_ada_w, v_ada_w)

    g_shards.update(reduce_finish(red4, G_["ada_w"]))
    G_["par_w_in"] = g_shards[0][None]
    G_["par_w_out"] = g_shards[1][None]
    for nme in ["par_w_in", "par_w_out"]:
        adam_on(nme)
    small = [n_ for n_ in names if n_ not in delta]
    pk = lambda dct: _pack_rows([dct[n_] for n_ in small])
    d_, m_, v_ = _adam("adam_small", pk(W), pk(G_), pk(M), pk(V))
    off = 0
    for n_ in small:
        sz = W[n_].size
        for dst, src in ((delta, d_), (new_m, m_), (new_v, v_)):
            dst[n_] = src.reshape(-1)[off:off + sz].reshape(W[n_].shape)
        off += sz

    return (loss, grad_x[None], *[G_[n_] for n_ in names], *[delta[n_] for n_ in names],
            *[new_m[n_] for n_ in names], *[new_v[n_] for n_ in names])
```

```python
import functools
import math

import jax
import jax.numpy as jnp
from jax import lax
from jax.experimental import pallas as pl
from jax.experimental.pallas import tpu as pltpu

F32 = jnp.float32
BF16 = jnp.bfloat16
MESH = pl.DeviceIdType.MESH
ANY = pl.BlockSpec(memory_space=pl.ANY)

NORM_EPS = 1e-6
ADAM_LR = 0.001
ADAM_B1 = 0.9
ADAM_B2 = 0.999
ADAM_EPS = 1e-08
ADAM_WD = 0.01
ADAM_STEP = 10
CONV_WIDTH = 3
HEAD = 128
HG_CHUNK = 64
SG_CHUNK = 128
N_DEV = 8
N_CHIP = 4
V7X_VMEM_LIMIT = 56 * 1024 * 1024


def _cp(*sem):
    return pltpu.CompilerParams(dimension_semantics=sem if sem else None, vmem_limit_bytes=V7X_VMEM_LIMIT)


def _pick(n, prefs):
    for p in prefs:
        if p <= n and n % p == 0:
            return p
    return n


def _iota(shape, axis):
    return lax.broadcasted_iota(jnp.int32, shape, axis)


def _rows_within(R, row_bytes, budget):
    if R * row_bytes <= budget:
        return R
    for t in (1024, 512, 256, 128, 64, 32, 16):
        if R % t == 0 and t * row_bytes <= budget:
            return t
    return _pick(R, (16, 8))


def _pack_rows(arrays):
    flat = jnp.concatenate([a.reshape(-1) for a in arrays])
    pad = (-flat.size) % (8 * HEAD)
    return jnp.pad(flat, (0, pad)).reshape(-1, HEAD)


def _dg(a, b, ca, cb):
    if a.ndim == 3:
        dims = (((ca + 1,), (cb + 1,)), ((0,), (0,)))
    else:
        dims = (((ca,), (cb,)), ((), ()))
    return lax.dot_general(a.astype(BF16), b.astype(BF16), dims, preferred_element_type=F32)


@jax.custom_vjp
def mm_nn(a, b):
    return _dg(a, b, 1, 0)


mm_nn.defvjp(lambda a, b: (_dg(a, b, 1, 0), (a, b)),
             lambda r, g: (_dg(g, r[1], 1, 1), _dg(r[0], g, 0, 0)))


@jax.custom_vjp
def mm_nt(a, b):
    return _dg(a, b, 1, 1)


mm_nt.defvjp(lambda a, b: (_dg(a, b, 1, 1), (a, b)),
             lambda r, g: (_dg(g, r[1], 1, 0), _dg(g, r[0], 0, 0)))


@jax.custom_vjp
def mm_tn(a, b):
    return _dg(a, b, 0, 0)


mm_tn.defvjp(lambda a, b: (_dg(a, b, 0, 0), (a, b)),
             lambda r, g: (_dg(r[1], g, 1, 1), _dg(r[0], g, 1, 0)))


def _split(x):
    hi = x.astype(BF16)
    lo = (x - hi.astype(F32)).astype(BF16)
    return hi, lo


def _sum_right(x, m01):
    hi, lo = _split(x)
    return _dg(hi, m01, 1, 0) + _dg(lo, m01, 1, 0)


def _sum_left_impl(m01, x, ca):
    if x.ndim == 3:
        m01 = jnp.broadcast_to(m01, (x.shape[0],) + m01.shape)
    hi, lo = _split(x)
    return _dg(m01, hi, ca, 0) + _dg(m01, lo, ca, 0)


@jax.custom_vjp
def _sum_left(m01, x):
    return _sum_left_impl(m01, x, 1)


_sum_left.defvjp(lambda m, x: (_sum_left_impl(m, x, 1), m),
                 lambda m, g: (None, _sum_left_impl(m, g, 0)))


def _sigmoid(x):
    return 1.0 / (1.0 + jnp.exp(-x))


def _softplus(z):
    return jnp.maximum(z, 0.0) + jnp.log(1.0 + jnp.exp(-jnp.abs(z)))


_INV_SQRT2 = 1.0 / math.sqrt(2.0)
_INV_SQRT2PI = 1.0 / math.sqrt(2.0 * math.pi)


@jax.custom_vjp
def _gelu(x):
    return 0.5 * x * (1.0 + lax.erf(x * _INV_SQRT2))


_gelu.defvjp(lambda x: (0.5 * x * (1.0 + lax.erf(x * _INV_SQRT2)), x),
             lambda x, g: (g * (0.5 * (1.0 + lax.erf(x * _INV_SQRT2)) + x * jnp.exp(-0.5 * x * x) * _INV_SQRT2PI),))


def _rms(x, gain):
    r = lax.rsqrt(jnp.mean(x * x, axis=-1, keepdims=True) + NORM_EPS)
    return x * r * gain


def _normmod(x, gain, sc, sh):
    return _rms(x, gain) * (1.0 + sc) + sh


def _mm_call(name, a, b, out_shape, out_dtype, dims, grid, a_spec, b_spec, o_spec, acc_shape):
    nk = grid[2]

    def body(a_ref, b_ref, o_ref, *scratch):
        part = lax.dot_general(a_ref[...].astype(BF16), b_ref[...].astype(BF16), dims, preferred_element_type=F32)
        if nk == 1:
            o_ref[...] = part.astype(o_ref.dtype)
            return
        acc_ref, = scratch
        k = pl.program_id(2)

        @pl.when(k == 0)
        def _():
            acc_ref[...] = part

        @pl.when(k > 0)
        def _():
            acc_ref[...] += part

        @pl.when(k == nk - 1)
        def _():
            o_ref[...] = acc_ref[...].astype(o_ref.dtype)

    return pl.pallas_call(
        body, name=name, grid=grid, in_specs=[a_spec, b_spec], out_specs=o_spec,
        out_shape=jax.ShapeDtypeStruct(out_shape, out_dtype),
        scratch_shapes=[] if nk == 1 else [pltpu.VMEM(acc_shape, F32)],
        compiler_params=_cp("parallel", "parallel", "arbitrary"),
    )(a, b)


def _mm_nn(name, a, b, out_dtype=F32):
    M, K = a.shape
    chunked = b.ndim == 3
    Nc = b.shape[-1]
    N = Nc * (b.shape[0] if chunked else 1)
    tm = _pick(M, (1024, 512, 256, 128, 64, 32, 16, 8))
    tn = _pick(Nc, (1408, 1024, 896, 512, 256, 128))
    tk = _pick(K, (2048, 1408, 1024, 512, 256, 128))
    npc = Nc // tn
    if chunked:
        b_spec = pl.BlockSpec((None, tk, tn), lambda i, j, k: (j // npc, k, j % npc))
    else:
        b_spec = pl.BlockSpec((tk, tn), lambda i, j, k: (k, j))
    return _mm_call(name, a, b, (M, N), out_dtype, (((1,), (0,)), ((), ())), (M // tm, N // tn, K // tk),
                    pl.BlockSpec((tm, tk), lambda i, j, k: (i, k)), b_spec,
                    pl.BlockSpec((tm, tn), lambda i, j, k: (i, j)), (tm, tn))


def _mm_nt(name, a, b, out_dtype=F32):
    planar = a.ndim == 3
    M, Np = a.shape[-2:]
    N = Np * (a.shape[0] if planar else 1)
    chunked = b.ndim == 3
    Nc = b.shape[-1]
    K = b.shape[-2]
    tm = _pick(M, (1024, 512, 256, 128, 64, 32, 16, 8))
    tn = _pick(K, (1408, 1024, 512, 256, 128))
    tk = _pick(Nc, (2048, 1792, 1408, 1024, 896, 512, 256, 128))
    assert Np % tk == 0
    npc = Nc // tk
    npp = Np // tk
    if chunked:
        b_spec = pl.BlockSpec((None, tn, tk), lambda i, j, k: (k // npc, j, k % npc))
    else:
        b_spec = pl.BlockSpec((tn, tk), lambda i, j, k: (j, k))
    if planar:
        a_spec = pl.BlockSpec((None, tm, tk), lambda i, j, k: (k // npp, i, k % npp))
    else:
        a_spec = pl.BlockSpec((tm, tk), lambda i, j, k: (i, k))
    return _mm_call(name, a, b, (M, K), out_dtype, (((1,), (1,)), ((), ())), (M // tm, K // tn, N // tk),
                    a_spec, b_spec, pl.BlockSpec((tm, tn), lambda i, j, k: (i, j)), (tm, tn))


def _mm_tn(name, a, b, chunks=1, out_dtype=BF16):
    T, K = a.shape
    planar = b.ndim == 3
    Np = b.shape[-1]
    N = Np * (b.shape[0] if planar else 1)
    Nc = N // chunks
    tm = _pick(K, (1408, 1024, 512, 256, 128))
    tn = _pick(Nc, (1408, 1024, 896, 512, 256, 128))
    tk = _pick(T, (1024, 512, 256, 128))
    assert Np % tn == 0
    npc = Nc // tn
    npp = Np // tn
    if planar:
        b_spec = pl.BlockSpec((None, tk, tn), lambda i, j, k: (j // npp, k, j % npp))
    else:
        b_spec = pl.BlockSpec((tk, tn), lambda i, j, k: (k, j))
    if chunks > 1:
        shape = (chunks, K, Nc)
        o_spec = pl.BlockSpec((None, tm, tn), lambda i, j, k: (j // npc, i, j % npc))
    else:
        shape = (K, N)
        o_spec = pl.BlockSpec((tm, tn), lambda i, j, k: (i, j))
    return _mm_call(name, a, b, shape, out_dtype, (((0,), (0,)), ((), ())), (K // tm, N // tn, T // tk),
                    pl.BlockSpec((tk, tm), lambda i, j, k: (k, i)), b_spec, o_spec, (tm, tn))


def _row_tile(T):
    return _pick(T, (256, 128, 64, 32, 16, 8))


def _vec_spec(D):
    return pl.BlockSpec((1, D), lambda i: (0, 0))


def _normmod_fwd(name, x, gain, sc, sh):
    T, D = x.shape
    bt = _row_tile(T)

    def body(x_ref, g_ref, sc_ref, sh_ref, h_ref):
        h_ref[...] = _normmod(x_ref[...], g_ref[...], sc_ref[...], sh_ref[...]).astype(h_ref.dtype)

    rows = pl.BlockSpec((bt, D), lambda i: (i, 0))
    return pl.pallas_call(body, name=name, grid=(T // bt,), in_specs=[rows] + [_vec_spec(D)] * 3, out_specs=rows,
                          out_shape=jax.ShapeDtypeStruct((T, D), BF16), compiler_params=_cp("parallel"))(x, gain, sc, sh)


def _res_normmod_fwd(name, x, y, g, gain, sc, sh):
    T, D = x.shape
    bt = _row_tile(T)

    def body(x_ref, y_ref, gate_ref, g_ref, sc_ref, sh_ref, x1_ref, h_ref):
        x1 = x_ref[...] + gate_ref[...] * y_ref[...]
        x1_ref[...] = x1
        h_ref[...] = _normmod(x1, g_ref[...], sc_ref[...], sh_ref[...]).astype(h_ref.dtype)

    rows = pl.BlockSpec((bt, D), lambda i: (i, 0))
    return pl.pallas_call(body, name=name, grid=(T // bt,), in_specs=[rows, rows] + [_vec_spec(D)] * 4,
                          out_specs=[rows, rows],
                          out_shape=[jax.ShapeDtypeStruct((T, D), F32), jax.ShapeDtypeStruct((T, D), BF16)],
                          compiler_params=_cp("parallel"))(x, y, g, gain, sc, sh)


def _final_fwd_bwd(x, y, g, gain, target):
    T, D = x.shape
    bt = _row_tile(T)

    def body(x_ref, y_ref, gate_ref, g_ref, t_ref, loss_ref, dx_ref, dy_ref, dgate_ref, dgain_ref):
        i = pl.program_id(0)
        yv = y_ref[...]
        gate = gate_ref[...]
        x4 = x_ref[...] + gate * yv
        out, vjp = jax.vjp(_rms, x4, g_ref[...])
        err = out - t_ref[...]
        dx4, dgain = vjp(err * (1.0 / D))
        part = 0.5 * jnp.sum(jnp.mean(err * err, axis=-1, keepdims=True), axis=0, keepdims=True)

        @pl.when(i == 0)
        def _():
            loss_ref[...] = jnp.zeros_like(loss_ref)
            dgate_ref[...] = jnp.zeros_like(dgate_ref)
            dgain_ref[...] = jnp.zeros_like(dgain_ref)

        loss_ref[...] += jnp.broadcast_to(part, loss_ref.shape)
        dx_ref[...] = dx4
        dy_ref[...] = (gate * dx4).astype(dy_ref.dtype)
        dgate_ref[...] += jnp.sum(dx4 * yv, axis=0, keepdims=True)
        dgain_ref[...] += dgain

    rows = pl.BlockSpec((bt, D), lambda i: (i, 0))
    vec = _vec_spec(D)
    return pl.pallas_call(
        body, name="final_loss", grid=(T // bt,), in_specs=[rows, rows, vec, vec, rows],
        out_specs=[pl.BlockSpec((1, HEAD), lambda i: (0, 0)), rows, rows, vec, vec],
        out_shape=[jax.ShapeDtypeStruct((1, HEAD), F32), jax.ShapeDtypeStruct((T, D), F32),
                   jax.ShapeDtypeStruct((T, D), BF16), jax.ShapeDtypeStruct((1, D), F32),
                   jax.ShapeDtypeStruct((1, D), F32)],
        compiler_params=_cp("arbitrary"))(x, y, g, gain, target)


def _block_bwd(name, dx_out, dh, x_in, gain, sc, sh, y_prev=None, g_prev=None):
    T, D = x_in.shape
    bt = _row_tile(T)
    has_prev = y_prev is not None

    def body(*refs):
        if has_prev:
            dxo_ref, dh_ref, x_ref, g_ref, sc_ref, sh_ref, y_ref, gp_ref, dx_ref, dgain_ref, dsc_ref, dsh_ref, dy_ref, dgp_ref = refs
        else:
            dxo_ref, dh_ref, x_ref, g_ref, sc_ref, sh_ref, dx_ref, dgain_ref, dsc_ref, dsh_ref = refs
        i = pl.program_id(0)
        _, vjp = jax.vjp(_normmod, x_ref[...], g_ref[...], sc_ref[...], sh_ref[...])
        dxn, dgain, dsc, dsh = vjp(dh_ref[...])
        dx = dxo_ref[...] + dxn
        dx_ref[...] = dx

        @pl.when(i == 0)
        def _():
            dgain_ref[...] = jnp.zeros_like(dgain_ref)
            dsc_ref[...] = jnp.zeros_like(dsc_ref)
            dsh_ref[...] = jnp.zeros_like(dsh_ref)
            if has_prev:
                dgp_ref[...] = jnp.zeros_like(dgp_ref)

        dgain_ref[...] += dgain
        dsc_ref[...] += dsc
        dsh_ref[...] += dsh
        if has_prev:
            dy_ref[...] = (gp_ref[...] * dx).astype(dy_ref.dtype)
            dgp_ref[...] += jnp.sum(dx * y_ref[...], axis=0, keepdims=True)

    rows = pl.BlockSpec((bt, D), lambda i: (i, 0))
    vec = _vec_spec(D)
    ins = [dx_out, dh, x_in, gain, sc, sh]
    in_specs = [rows, rows, rows, vec, vec, vec]
    out_specs = [rows, vec, vec, vec]
    out_shape = [jax.ShapeDtypeStruct((T, D), F32)] + [jax.ShapeDtypeStruct((1, D), F32)] * 3
    if has_prev:
        ins += [y_prev, g_prev]
        in_specs += [rows, vec]
        out_specs += [rows, vec]
        out_shape += [jax.ShapeDtypeStruct((T, D), BF16), jax.ShapeDtypeStruct((1, D), F32)]
    return pl.pallas_call(body, name=name, grid=(T // bt,), in_specs=in_specs, out_specs=out_specs,
                          out_shape=out_shape, compiler_params=_cp("arbitrary"))(*ins)


def _sb_tiles(T):
    tq = _pick(T, (512, 256, 128))
    return tq, tq // HEAD


def _sb_fwd(proj, H):
    T = proj.shape[0]
    tq, nsub = _sb_tiles(T)
    scale = HEAD ** -0.5

    def body(q_ref, k_ref, v_ref, o_ref, l_ref, acc_ref):
        i = pl.program_id(1)
        q = q_ref[...].astype(BF16)
        later = (_iota((HEAD, HEAD), 0) > _iota((HEAD, HEAD), 1)).astype(BF16)
        row = _iota((tq, HEAD), 0)
        col = _iota((tq, HEAD), 1)

        def key_step(j, c, diagonal):
            off = pl.multiple_of(j * tq, tq)
            k = k_ref[pl.ds(off, tq), :].astype(BF16)
            v = v_ref[pl.ds(off, tq), :].astype(BF16)
            z = _dg(q, k, 1, 1) * scale
            ws = [None] * nsub
            for s in reversed(range(nsub)):
                zs = z[:, s * HEAD:(s + 1) * HEAD]
                sp = _softplus(zs)
                if diagonal:
                    strict = (s * HEAD + col) < row
                    lk = jnp.where(strict, -sp, 0.0)
                else:
                    lk = -sp
                w = jnp.exp(zs - sp + _sum_right(lk, later) + c)
                if diagonal:
                    w = jnp.where(strict, w, 0.0)
                ws[s] = w.astype(BF16)
                c = c + jnp.sum(lk, axis=1, keepdims=True)
            acc_ref[...] += _dg(jnp.concatenate(ws, axis=1), v, 1, 0)
            return c

        acc_ref[...] = jnp.zeros_like(acc_ref)
        c = key_step(i, jnp.zeros((tq, 1), F32), True)
        c = lax.fori_loop(0, i, lambda n, c: key_step(i - 1 - n, c, False), c)
        o_ref[...] = acc_ref[...].astype(o_ref.dtype)
        l_ref[...] = jnp.broadcast_to(c, (tq, HEAD))

    blk = pl.BlockSpec((tq, HEAD), lambda h, i: (i, h))
    return pl.pallas_call(
        body, name="sb_fwd", grid=(H, T // tq),
        in_specs=[blk, pl.BlockSpec((T, HEAD), lambda h, i: (0, H + h)), pl.BlockSpec((T, HEAD), lambda h, i: (0, 2 * H + h))],
        out_specs=[blk, blk],
        out_shape=[jax.ShapeDtypeStruct((T, H * HEAD), BF16), jax.ShapeDtypeStruct((T, H * HEAD), F32)],
        scratch_shapes=[pltpu.VMEM((tq, HEAD), F32)],
        compiler_params=_cp("parallel", "arbitrary"))(proj, proj, proj)


def _sb_bwd(proj, do, L, H):
    T = proj.shape[0]
    tq, nsub = _sb_tiles(T)
    scale = HEAD ** -0.5

    def body(q_ref, k_ref, v_ref, do_ref, l_ref, dq_ref, dk_ref, dv_ref):
        i = pl.program_id(1)

        @pl.when(i == 0)
        def _():
            dk_ref[...] = jnp.zeros_like(dk_ref)
            dv_ref[...] = jnp.zeros_like(dv_ref)

        dq_ref[...] = jnp.zeros_like(dq_ref)
        q = q_ref[...].astype(BF16)
        do_ = do_ref[...].astype(BF16)
        total = l_ref[...]
        upto = (_iota((HEAD, HEAD), 0) <= _iota((HEAD, HEAD), 1)).astype(BF16)
        before = (_iota((HEAD, HEAD), 0) < _iota((HEAD, HEAD), 1)).astype(BF16)
        row = _iota((tq, HEAD), 0)
        col = _iota((tq, HEAD), 1)

        def key_step(j, carry, diagonal):
            cp, ce = carry
            off = pl.multiple_of(j * tq, tq)
            k = k_ref[pl.ds(off, tq), :].astype(BF16)
            v = v_ref[pl.ds(off, tq), :].astype(BF16)
            z = _dg(q, k, 1, 1) * scale
            dw = _dg(do_, v, 1, 1)
            ws, dzs = [], []
            for s in range(nsub):
                zs = z[:, s * HEAD:(s + 1) * HEAD]
                sp = _softplus(zs)
                if diagonal:
                    strict = (s * HEAD + col) < row
                    lk = jnp.where(strict, -sp, 0.0)
                else:
                    lk = -sp
                tail = total - (_sum_right(lk, upto) + cp)
                w = jnp.exp(zs - sp + tail)
                if diagonal:
                    w = jnp.where(strict, w, 0.0)
                e = w * dw[:, s * HEAD:(s + 1) * HEAD]
                e_before = _sum_right(e, before) + ce
                sig = jnp.exp(zs - sp)
                dz = (e * (1.0 - sig) - e_before * sig) * scale
                if diagonal:
                    dz = jnp.where(strict, dz, 0.0)
                ws.append(w.astype(BF16))
                dzs.append(dz.astype(BF16))
                cp = cp + jnp.sum(lk, axis=1, keepdims=True)
                ce = ce + jnp.sum(e, axis=1, keepdims=True)
            w_all = jnp.concatenate(ws, axis=1)
            dz_all = jnp.concatenate(dzs, axis=1)
            dv_ref[pl.ds(off, tq), :] += _dg(w_all, do_, 0, 0)
            dk_ref[pl.ds(off, tq), :] += _dg(dz_all, q, 0, 0)
            dq_ref[...] += _dg(dz_all, k, 1, 0)
            return cp, ce

        zero = jnp.zeros((tq, 1), F32)
        carry = lax.fori_loop(0, i, lambda j, cr: key_step(j, cr, False), (zero, zero))
        key_step(i, carry, True)

    blk = pl.BlockSpec((tq, HEAD), lambda h, i: (i, h))
    full = pl.BlockSpec((T, HEAD), lambda h, i: (0, h))
    shp = jax.ShapeDtypeStruct((T, H * HEAD), F32)
    return pl.pallas_call(
        body, name="sb_bwd", grid=(H, T // tq),
        in_specs=[blk, pl.BlockSpec((T, HEAD), lambda h, i: (0, H + h)), pl.BlockSpec((T, HEAD), lambda h, i: (0, 2 * H + h)),
                  blk, blk],
        out_specs=[blk, full, full], out_shape=[shp, shp, shp],
        compiler_params=_cp("parallel", "arbitrary"))(proj, proj, proj, do, L)


def _hg_tile(q, fl, iv, g, st, l0, l1, gain):
    R = 2 * HG_CHUNK
    row = _iota((R, R), 0)
    col = _iota((R, R), 1)
    first = row < HG_CHUNK
    same = first == (col < HG_CHUNK)
    tri = (row >= col) & same
    lb = _sigmoid(l0 - l1)
    f = lb + (1.0 - lb) * _sigmoid(fl)
    logf = jnp.log(f)
    k = 1.0 - f
    qf = q * _sigmoid(q)
    G = _sum_left(tri.astype(BF16), logf)
    gl_a = jnp.sum(jnp.where(first, logf, 0.0), axis=-2, keepdims=True)
    gl_b = jnp.sum(jnp.where(first, 0.0, logf), axis=-2, keepdims=True)
    q_dec = qf * jnp.exp(G)
    k_inv = k * jnp.exp(-G)
    k_end = k * jnp.exp(jnp.where(first, gl_a, gl_b) - G)
    scores = jnp.where(tri, mm_nt(q_dec, k_inv), 0.0)
    o = mm_nn(scores, iv)
    o_a = mm_nt(q_dec, st)
    st_mid = st * jnp.exp(gl_a) + mm_tn(jnp.where(first, iv, 0.0), k_end)
    o_b = mm_nt(q_dec, st_mid)
    st_new = st_mid * jnp.exp(gl_b) + mm_tn(jnp.where(first, 0.0, iv), k_end)
    o = o + jnp.where(first, o_a, o_b)
    on = o * lax.rsqrt(jnp.mean(o * o, axis=-1, keepdims=True) + NORM_EPS) * gain
    return on * (g * _sigmoid(g)), st_new


def _hg_heads(H):
    return _pick(H, (8, 4, 2, 1))


def _hg_specs(H, c0, rev, nt):
    hb = _hg_heads(H)
    w = hb * HEAD

    def at(base):
        if rev:
            return pl.BlockSpec((HEAD, w), lambda h, i: (nt - 1 - i, base // hb + h))
        return pl.BlockSpec((HEAD, w), lambda h, i: (i, base // hb + h))
    return [at(c0), at(c0 + H), at(c0 + 2 * H), at(c0 + 3 * H)]


def _hg_fwd(proj, l0, l1, gain, H, c0):
    T = proj.shape[0]
    nt = T // HEAD
    hb = _hg_heads(H)
    w = hb * HEAD

    def body(q_ref, f_ref, i_ref, g_ref, l0_ref, l1_ref, gain_ref, o_ref, st_out_ref, st_ref):
        @pl.when(pl.program_id(1) == 0)
        def _():
            st_ref[...] = jnp.zeros_like(st_ref)

        sl = [slice(j * HEAD, (j + 1) * HEAD) for j in range(hb)]
        heads = lambda ref: jnp.stack([ref[:, s] for s in sl])
        st = st_ref[...]
        st_out_ref[...] = st
        out, st_new = _hg_tile(heads(q_ref), heads(f_ref), heads(i_ref), heads(g_ref), st, heads(l0_ref), heads(l1_ref),
                               heads(gain_ref))
        for j, s in enumerate(sl):
            o_ref[:, s] = out[j].astype(o_ref.dtype)
        st_ref[...] = st_new

    vec = pl.BlockSpec((1, w), lambda h, i: (0, h))
    return pl.pallas_call(
        body, name="hg_fwd", grid=(H // hb, nt), in_specs=_hg_specs(H, c0, False, nt) + [vec, vec, vec],
        out_specs=[pl.BlockSpec((HEAD, w), lambda h, i: (i, h)),
                   pl.BlockSpec((hb, None, HEAD, HEAD), lambda h, i: (h, i, 0, 0))],
        out_shape=[jax.ShapeDtypeStruct((T, H * HEAD), BF16), jax.ShapeDtypeStruct((H, nt, HEAD, HEAD), F32)],
        scratch_shapes=[pltpu.VMEM((hb, HEAD, HEAD), F32)],
        compiler_params=_cp("parallel", "arbitrary"))(proj, proj, proj, proj, l0, l1, gain)


def _hg_bwd(proj, states, do, l0, l1, gain, H, c0, do_c0):
    T = proj.shape[0]
    nt = T // HEAD
    hb = _hg_heads(H)
    w = hb * HEAD

    def body(q_ref, f_ref, i_ref, g_ref, st_in_ref, do_ref, l0_ref, l1_ref, gain_ref,
             dq_ref, df_ref, di_ref, dg_ref, dl0_ref, dl1_ref, dgain_ref, dst_ref):
        @pl.when(pl.program_id(1) == 0)
        def _():
            dst_ref[...] = jnp.zeros_like(dst_ref)
            dl0_ref[...] = jnp.zeros_like(dl0_ref)
            dl1_ref[...] = jnp.zeros_like(dl1_ref)
            dgain_ref[...] = jnp.zeros_like(dgain_ref)

        sl = [slice(j * HEAD, (j + 1) * HEAD) for j in range(hb)]
        heads = lambda ref: jnp.stack([ref[:, s] for s in sl])
        _, vjp = jax.vjp(_hg_tile, heads(q_ref), heads(f_ref), heads(i_ref), heads(g_ref), st_in_ref[...],
                         heads(l0_ref), heads(l1_ref), heads(gain_ref))
        dq, df, di, dg, dst, dl0, dl1, dgain = vjp((heads(do_ref), dst_ref[...]))
        dst_ref[...] = dst
        for j, s in enumerate(sl):
            dq_ref[:, s] = dq[j]
            df_ref[:, s] = df[j]
            di_ref[:, s] = di[j]
            dg_ref[:, s] = dg[j]
            dl0_ref[:, s] += dl0[j]
            dl1_ref[:, s] += dl1[j]
            dgain_ref[:, s] += dgain[j]

    vec = pl.BlockSpec((1, w), lambda h, i: (0, h))
    rblk = pl.BlockSpec((HEAD, w), lambda h, i: (nt - 1 - i, h))
    shp = jax.ShapeDtypeStruct((T, H * HEAD), F32)
    vshp = jax.ShapeDtypeStruct((1, H * HEAD), F32)
    return pl.pallas_call(
        body, name="hg_bwd", grid=(H // hb, nt),
        in_specs=_hg_specs(H, c0, True, nt) + [
            pl.BlockSpec((hb, None, HEAD, HEAD), lambda h, i: (h, nt - 1 - i, 0, 0)),
            pl.BlockSpec((HEAD, w), lambda h, i: (nt - 1 - i, do_c0 // hb + h)), vec, vec, vec],
        out_specs=[rblk, rblk, rblk, rblk, vec, vec, vec],
        out_shape=[shp, shp, shp, shp, vshp, vshp, vshp],
        scratch_shapes=[pltpu.VMEM((hb, HEAD, HEAD), F32)],
        compiler_params=_cp("parallel", "arbitrary"))(proj, proj, proj, proj, states, do, l0, l1, gain)


def _sg_chunk(u_parts, v_parts, gains, biases, wpos, bpos):
    W = sum(p.shape[1] for p in v_parts)
    C = v_parts[0].shape[0]
    v = [_gelu(p) for p in v_parts]
    mu = sum(jnp.sum(p, axis=-1, keepdims=True) for p in v) * (1.0 / W)
    xc = [p - mu for p in v]
    r = lax.rsqrt(sum(jnp.sum(p * p, axis=-1, keepdims=True) for p in xc) * (1.0 / W) + NORM_EPS)
    causal = _iota((C, C), 0) >= _iota((C, C), 1)
    out = []
    for up, p, gn, bs, w, b in zip(u_parts, xc, gains, biases, wpos, bpos):
        vn = p * r * gn + bs
        mixed = mm_nn(jnp.where(causal, w, 0.0), vn) + b
        out.append(_gelu(up) * mixed)
    return out


def _sg_fwd(zpre, vgain, vbias, wpos, bpos):
    T, W2 = zpre.shape
    W = W2 // 2
    G = wpos.shape[0]
    cg = W // G
    C = SG_CHUNK

    def body(z_ref, gn_ref, bs_ref, w_ref, b_ref, s_ref):
        sl = [slice(g * cg, (g + 1) * cg) for g in range(G)]
        out = _sg_chunk([z_ref[:, s] for s in sl], [z_ref[:, W + s.start:W + s.stop] for s in sl],
                        [gn_ref[:, s] for s in sl], [bs_ref[:, s] for s in sl],
                        [w_ref[g] for g in range(G)], [b_ref[g] for g in range(G)])
        for s, o in zip(sl, out):
            s_ref[:, s] = o.astype(s_ref.dtype)

    return pl.pallas_call(
        body, name="sg_fwd", grid=(T // C,),
        in_specs=[pl.BlockSpec((C, W2), lambda i: (i, 0)), _vec_spec(W), _vec_spec(W),
                  pl.BlockSpec((G, C, C), lambda i: (0, 0, 0)), pl.BlockSpec((G, C, 1), lambda i: (0, 0, 0))],
        out_specs=pl.BlockSpec((C, W), lambda i: (i, 0)),
        out_shape=jax.ShapeDtypeStruct((T, W), BF16), compiler_params=_cp("parallel"))(zpre, vgain, vbias, wpos, bpos)


def _sg_bwd(zpre, ds, vgain, vbias, wpos, bpos):
    T, W2 = zpre.shape
    W = W2 // 2
    G = wpos.shape[0]
    cg = W // G
    C = SG_CHUNK

    def body(z_ref, ds_ref, gn_ref, bs_ref, w_ref, b_ref, dz_ref, dgn_ref, dbs_ref, dw_ref, db_ref):
        @pl.when(pl.program_id(0) == 0)
        def _():
            dgn_ref[...] = jnp.zeros_like(dgn_ref)
            dbs_ref[...] = jnp.zeros_like(dbs_ref)
            dw_ref[...] = jnp.zeros_like(dw_ref)
            db_ref[...] = jnp.zeros_like(db_ref)

        sl = [slice(g * cg, (g + 1) * cg) for g in range(G)]
        _, vjp = jax.vjp(_sg_chunk, [z_ref[:, s] for s in sl], [z_ref[:, W + s.start:W + s.stop] for s in sl],
                         [gn_ref[:, s] for s in sl], [bs_ref[:, s] for s in sl],
                         [w_ref[g] for g in range(G)], [b_ref[g] for g in range(G)])
        du, dv, dgn, dbs, dw, db = vjp([ds_ref[:, s] for s in sl])
        for g, s in enumerate(sl):
            dz_ref[:, s] = du[g].astype(dz_ref.dtype)
            dz_ref[:, W + s.start:W + s.stop] = dv[g].astype(dz_ref.dtype)
            dgn_ref[:, s] += dgn[g]
            dbs_ref[:, s] += dbs[g]
            dw_ref[g] += dw[g]
            db_ref[g] += db[g]

    wspec = pl.BlockSpec((G, C, C), lambda i: (0, 0, 0))
    bspec = pl.BlockSpec((G, C, 1), lambda i: (0, 0, 0))
    return pl.pallas_call(
        body, name="sg_bwd", grid=(T // C,),
        in_specs=[pl.BlockSpec((C, W2), lambda i: (i, 0)), pl.BlockSpec((C, W), lambda i: (i, 0)),
                  _vec_spec(W), _vec_spec(W), wspec, bspec],
        out_specs=[pl.BlockSpec((C, W2), lambda i: (i, 0)), _vec_spec(W), _vec_spec(W), wspec, bspec],
        out_shape=[jax.ShapeDtypeStruct((T, W2), BF16), jax.ShapeDtypeStruct((1, W), F32),
                   jax.ShapeDtypeStruct((1, W), F32), jax.ShapeDtypeStruct((G, C, C), F32),
                   jax.ShapeDtypeStruct((G, C, 1), F32)],
        compiler_params=_cp("arbitrary"))(zpre, ds, vgain, vbias, wpos, bpos)


def _conv_tiles(T, F):
    return _pick(T, (512, 256, 128, 64, 32, 16, 8)), _pick(F, (512, 256, 128))


def _shift_down(cur, prev8, n, first_tile):
    bt = cur.shape[0]
    r = pltpu.roll(cur, n, 0)
    p = pltpu.roll(prev8, n, 0)
    p = jnp.where(first_tile, 0.0, p)
    head = jnp.where(_iota(p.shape, 0) < n, p, r[:8])
    return jnp.concatenate([head, r[8:]], axis=0) if bt > 8 else head


def _shift_up(cur, next8, n, last_tile):
    bt = cur.shape[0]
    r = pltpu.roll(cur, bt - n, 0)
    p = pltpu.roll(next8, 8 - n, 0)
    p = jnp.where(last_tile, 0.0, p)
    tail = jnp.where(_iota(p.shape, 0) >= 8 - n, p, r[bt - 8:])
    return jnp.concatenate([r[:bt - 8], tail], axis=0) if bt > 8 else tail


def _conv_apply(cur, prev8, w_ref, b, first_tile):
    return (b + w_ref[0:1, :] * _shift_down(cur, prev8, 2, first_tile)
            + w_ref[1:2, :] * _shift_down(cur, prev8, 1, first_tile) + w_ref[2:3, :] * cur)


def _conv_fwd(name, a, w, b):
    T, F2 = a.shape
    F = F2 // 2
    bt, cw = _conv_tiles(T, F)
    nf = F // cw
    r8 = bt // 8

    def body(g_ref, gp_ref, v_ref, vp_ref, wg_ref, wv_ref, bg_ref, bv_ref, u_ref):
        first = pl.program_id(0) == 0
        gate = _conv_apply(g_ref[...], gp_ref[...], wg_ref, bg_ref[...], first)
        val = _conv_apply(v_ref[...], vp_ref[...], wv_ref, bv_ref[...], first)
        u_ref[...] = (gate * _sigmoid(gate) * val).astype(u_ref.dtype)

    def cur(off):
        return pl.BlockSpec((bt, cw), lambda i, j: (i, j + off))

    def prev(off):
        return pl.BlockSpec((8, cw), lambda i, j: (jnp.maximum(i * r8 - 1, 0), j + off))

    def vec(rows, off):
        return pl.BlockSpec((rows, cw), lambda i, j: (0, j + off))

    return pl.pallas_call(
        body, name=name, grid=(T // bt, nf),
        in_specs=[cur(0), prev(0), cur(nf), prev(nf), vec(3, 0), vec(3, nf), vec(1, 0), vec(1, nf)],
        out_specs=pl.BlockSpec((bt, cw), lambda i, j: (i, j)),
        out_shape=jax.ShapeDtypeStruct((T, F), BF16),
        compiler_params=_cp("parallel", "parallel"))(a, a, a, a, w, w, b, b)


def _conv_bwd(name, a, du, w, b):
    T, F2 = a.shape
    F = F2 // 2
    bt, cw = _conv_tiles(T, F)
    nf = F // cw
    r8 = bt // 8
    last_blk = T // 8 - 1

    def body(g_ref, gp_ref, gn_ref, v_ref, vp_ref, vn_ref, du_ref, dun_ref, wg_ref, wv_ref, bg_ref, bv_ref,
             da_ref, dwg_ref, dwv_ref, dbg_ref, dbv_ref):
        i = pl.program_id(1)
        first = i == 0
        last = i == pl.num_programs(1) - 1

        def taps(cur, prev8, at_start):
            return _shift_down(cur, prev8, 2, at_start), _shift_down(cur, prev8, 1, at_start), cur

        def conv(t, w_ref, b_ref):
            return b_ref[...] + w_ref[0:1, :] * t[0] + w_ref[1:2, :] * t[1] + w_ref[2:3, :] * t[2]

        def act_bwd(gate, val, du_):
            sg = _sigmoid(gate)
            return du_ * val * (sg * (1.0 + gate * (1.0 - sg))), du_ * gate * sg

        g_cur, v_cur = g_ref[...], v_ref[...]
        tg = taps(g_cur, gp_ref[...], first)
        tv = taps(v_cur, vp_ref[...], first)
        dg, dv = act_bwd(conv(tg, wg_ref, bg_ref), conv(tv, wv_ref, bv_ref), du_ref[...])
        tgn = taps(gn_ref[...], g_cur[bt - 8:, :], False)
        tvn = taps(vn_ref[...], v_cur[bt - 8:, :], False)
        dgn, dvn = act_bwd(conv(tgn, wg_ref, bg_ref), conv(tvn, wv_ref, bv_ref), dun_ref[...])

        def conv_t(d, dn, w_ref):
            return w_ref[2:3, :] * d + w_ref[1:2, :] * _shift_up(d, dn, 1, last) + w_ref[0:1, :] * _shift_up(d, dn, 2, last)

        da_ref[0] = conv_t(dg, dgn, wg_ref).astype(da_ref.dtype)
        da_ref[1] = conv_t(dv, dvn, wv_ref).astype(da_ref.dtype)

        @pl.when(first)
        def _():
            dwg_ref[...] = jnp.zeros_like(dwg_ref)
            dwv_ref[...] = jnp.zeros_like(dwv_ref)
            dbg_ref[...] = jnp.zeros_like(dbg_ref)
            dbv_ref[...] = jnp.zeros_like(dbv_ref)

        for t in range(CONV_WIDTH):
            dwg_ref[t:t + 1, :] += jnp.sum(dg * tg[t], axis=0, keepdims=True)
            dwv_ref[t:t + 1, :] += jnp.sum(dv * tv[t], axis=0, keepdims=True)
        dbg_ref[...] += jnp.sum(dg, axis=0, keepdims=True)
        dbv_ref[...] += jnp.sum(dv, axis=0, keepdims=True)

    def cur(off):
        return pl.BlockSpec((bt, cw), lambda j, i: (i, j + off))

    def prev(off):
        return pl.BlockSpec((8, cw), lambda j, i: (jnp.maximum(i * r8 - 1, 0), j + off))

    def nxt(off):
        return pl.BlockSpec((8, cw), lambda j, i: (jnp.minimum((i + 1) * r8, last_blk), j + off))

    def vec(rows, off):
        return pl.BlockSpec((rows, cw), lambda j, i: (0, j + off))

    da, dwg, dwv, dbg, dbv = pl.pallas_call(
        body, name=name, grid=(nf, T // bt),
        in_specs=[cur(0), prev(0), nxt(0), cur(nf), prev(nf), nxt(nf), cur(0), nxt(0),
                  vec(3, 0), vec(3, nf), vec(1, 0), vec(1, nf)],
        out_specs=[pl.BlockSpec((2, bt, cw), lambda j, i: (0, i, j)), vec(3, 0), vec(3, 0), vec(1, 0), vec(1, 0)],
        out_shape=[jax.ShapeDtypeStruct((2, T, F), BF16), jax.ShapeDtypeStruct((3, F), F32), jax.ShapeDtypeStruct((3, F), F32),
                   jax.ShapeDtypeStruct((1, F), F32), jax.ShapeDtypeStruct((1, F), F32)],
        compiler_params=_cp("parallel", "arbitrary"))(a, a, a, a, a, a, du, du, w, w, b, b)
    return da, jnp.concatenate([dwg, dwv], axis=1), jnp.concatenate([dbg, dbv], axis=1)


def _ada_fwd(c_all, ada_w, ada_b):
    R, D = c_all.shape
    L, _, Ns = ada_w.shape
    tn = _pick(Ns, (512, 256, 128))

    def body(c_ref, w_ref, b_ref, o_ref):
        cv = c_ref[...]
        cond = cv * _sigmoid(cv)
        o_ref[...] = _dg(cond, w_ref[...], 1, 0) + b_ref[...]

    return pl.pallas_call(
        body, name="ada_fwd", grid=(L, Ns // tn),
        in_specs=[pl.BlockSpec((R, D), lambda l, j: (0, 0)), pl.BlockSpec((None, D, tn), lambda l, j: (l, 0, j)),
                  pl.BlockSpec((None, 1, tn), lambda l, j: (l, 0, j))],
        out_specs=pl.BlockSpec((None, R, tn), lambda l, j: (l, 0, j)),
        out_shape=jax.ShapeDtypeStruct((L, R, Ns), F32), compiler_params=_cp("parallel", "parallel"))(c_all, ada_w, ada_b)


def _adam_math(w, g, m, v):
    m2 = ADAM_B1 * m + (1.0 - ADAM_B1) * g
    v2 = ADAM_B2 * v + (1.0 - ADAM_B2) * (g * g)
    m_hat = m2 / (1.0 - ADAM_B1 ** ADAM_STEP)
    v_hat = v2 / (1.0 - ADAM_B2 ** ADAM_STEP)
    delta = -ADAM_LR * (m_hat / (jnp.sqrt(v_hat) + ADAM_EPS) + ADAM_WD * w)
    return delta, m2, v2


def _ada_grad_adam(c_all_t, dmod, w, m, v):
    D, R = c_all_t.shape
    L, _, Ns = dmod.shape
    tr = _rows_within(D, Ns * 4, 1 << 21)

    def body(c_ref, d_ref, w_ref, m_ref, v_ref, g_ref, dl_ref, m2_ref, v2_ref):
        cv = c_ref[...]
        g = _dg(cv * _sigmoid(cv), d_ref[...], 1, 0)
        g_ref[...] = g
        dl_ref[...], m2_ref[...], v2_ref[...] = _adam_math(w_ref[...], g, m_ref[...], v_ref[...])

    big = pl.BlockSpec((None, tr, Ns), lambda l, i: (l, i, 0))
    shp = jax.ShapeDtypeStruct((L, D, Ns), F32)
    return pl.pallas_call(
        body, name="ada_grad_adam", grid=(L, D // tr),
        in_specs=[pl.BlockSpec((tr, R), lambda l, i: (i, 0)), pl.BlockSpec((None, R, Ns), lambda l, i: (l, 0, 0)), big, big, big],
        out_specs=[big] * 4, out_shape=[shp] * 4, compiler_params=_cp("parallel", "parallel"))(c_all_t, dmod, w, m, v)


def _adam(name, w, g, m, v):
    R, C = w.shape
    tr = _rows_within(R, C * 4, 3 << 20)

    def body(w_ref, g_ref, m_ref, v_ref, dl_ref, m2_ref, v2_ref):
        dl_ref[...], m2_ref[...], v2_ref[...] = _adam_math(w_ref[...], g_ref[...], m_ref[...], v_ref[...])

    blk = pl.BlockSpec((tr, C), lambda i: (i, 0))
    shp = jax.ShapeDtypeStruct((R, C), F32)
    return pl.pallas_call(body, name=name, grid=(R // tr,), in_specs=[blk] * 4, out_specs=[blk] * 3,
                          out_shape=[shp] * 3, compiler_params=_cp("parallel"))(w, g, m, v)


def _cast_into_rows(name, w, chip, after=None):
    _, R, C = w.shape
    tr = _rows_within(R, C * 4, 1 << 22)
    extra = [] if after is None else [after]

    def body(chip_ref, w_ref, *rest):
        o_ref = rest[-1]
        o_ref[...] = w_ref[...].astype(BF16)

    grid_spec = pltpu.PrefetchScalarGridSpec(
        num_scalar_prefetch=1, grid=(2, R // tr),
        in_specs=[pl.BlockSpec((None, tr, C), lambda h, i, s: (h, i, 0))] + [ANY] * len(extra),
        out_specs=pl.BlockSpec((None, tr, C), lambda h, i, s: (2 * s[0] + h, i, 0)))
    return pl.pallas_call(body, name=name, grid_spec=grid_spec, out_shape=jax.ShapeDtypeStruct((N_DEV, R, C), BF16),
                          compiler_params=_cp("arbitrary", "arbitrary"))(chip.reshape(1).astype(jnp.int32), w, *extra)


def _add_pairs(name, eight, from_sib, c):
    _, R, C = from_sib.shape
    tr = _rows_within(R, C * 2, 1 << 22)

    def body(c_ref, a_ref, b_ref, o_ref):
        o_ref[...] = (a_ref[...].astype(F32) + b_ref[...].astype(F32)).astype(o_ref.dtype)

    blk = pl.BlockSpec((None, tr, C), lambda j, i, s: (j, i, 0))
    grid_spec = pltpu.PrefetchScalarGridSpec(
        num_scalar_prefetch=1, grid=(4, R // tr),
        in_specs=[pl.BlockSpec((None, tr, C), lambda j, i, s: (2 * j + s[0], i, 0)), blk], out_specs=blk)
    return pl.pallas_call(body, name=name, grid_spec=grid_spec, out_shape=jax.ShapeDtypeStruct(from_sib.shape, BF16),
                          compiler_params=_cp("arbitrary", "arbitrary"))(c.reshape(1).astype(jnp.int32), eight, from_sib)


def _sum_into_pair(name, own, landed, slot, chip):
    n, R, C = landed.shape
    tr = _rows_within(R, (n + 1) * C * landed.dtype.itemsize, 1 << 23)

    def body(idx_ref, own_ref, x_ref, o_ref):
        mine = idx_ref[1]
        acc = None
        for j in range(n):
            part = jnp.where(mine == j, own_ref[...], x_ref[j]).astype(F32)
            acc = part if acc is None else acc + part
        o_ref[...] = acc

    grid_spec = pltpu.PrefetchScalarGridSpec(
        num_scalar_prefetch=1, grid=(R // tr,),
        in_specs=[pl.BlockSpec((None, tr, C), lambda i, s: (s[1], i, 0)), pl.BlockSpec((n, tr, C), lambda i, s: (0, i, 0))],
        out_specs=pl.BlockSpec((None, tr, C), lambda i, s: (s[0], i, 0)))
    idx = jnp.stack([slot, chip]).astype(jnp.int32)
    return pl.pallas_call(body, name=name, grid_spec=grid_spec, out_shape=jax.ShapeDtypeStruct((2, R, C), F32),
                          compiler_params=_cp("arbitrary"))(idx, own, landed)


def _sum_leading(name, a, out_dtype=F32):
    n, R, C = a.shape
    tr = _rows_within(R, n * C * a.dtype.itemsize, 1 << 24)

    def body(a_ref, o_ref):
        acc = a_ref[0].astype(F32)
        for j in range(1, n):
            acc = acc + a_ref[j].astype(F32)
        o_ref[...] = acc.astype(o_ref.dtype)

    return pl.pallas_call(body, name=name, grid=(R // tr,), in_specs=[pl.BlockSpec((n, tr, C), lambda i: (0, i, 0))],
                          out_specs=pl.BlockSpec((tr, C), lambda i: (i, 0)),
                          out_shape=jax.ShapeDtypeStruct((R, C), out_dtype), compiler_params=_cp("parallel"))(a)


def _place():
    return lax.axis_index("x"), lax.axis_index("y"), lax.axis_index("c")


def _all_gather(name, blocks, halves=False, after=None):
    n = len(blocks)
    shapes = [b.shape[1:] if halves else b.shape for b in blocks]
    extra = [] if after is None else [after]

    def body(*refs):
        ins, outs = refs[:n], refs[n + len(extra):2 * n + len(extra)]
        send_sems, recv_sems, local_sems = refs[2 * n + len(extra):]
        x, y, c = _place()
        me, sibling = (x, y, c), (x, y, 1 - c)
        chips = [(1 - x, y), (x, 1 - y), (1 - x, 1 - y)]

        def rows(a, px, py, pc):
            return outs[a].at[4 * px + 2 * py + pc]

        def copy(a, k, block, to, src=None):
            return pltpu.make_async_remote_copy(
                src_ref=rows(a, *block) if src is None else src, dst_ref=rows(a, *block),
                send_sem=send_sems.at[7 * a + k], recv_sem=recv_sems.at[7 * a + k],
                device_id=to, device_id_type=MESH)

        started = []
        mine = []
        for a in range(n):
            src = ins[a].at[c] if halves else ins[a]
            mine.append(pltpu.make_async_copy(src, rows(a, *me), local_sems.at[a]))
            mine[-1].start()
            first = [copy(a, 0, me, sibling, src=src)]
            first += [copy(a, 1 + j, me, (*chip, c), src=src) for j, chip in enumerate(chips)]
            for cp in first:
                cp.start()
            started += first
        for j, chip in enumerate(chips):
            for a in range(n):
                copy(a, 1 + j, (*chip, c), me).wait_recv()
                passed = copy(a, 4 + j, (*chip, c), sibling)
                passed.start()
                started.append(passed)
        for a in range(n):
            copy(a, 0, sibling, me).wait_recv()
            for j, chip in enumerate(chips):
                copy(a, 4 + j, (*chip, 1 - c), me).wait_recv()
        for cp in started:
            cp.wait_send()
        for cp in mine:
            cp.wait()

    return pl.pallas_call(
        body, name=name, in_specs=[ANY] * (n + len(extra)), out_specs=[ANY] * n,
        out_shape=[jax.ShapeDtypeStruct((N_DEV,) + tuple(s), b.dtype) for s, b in zip(shapes, blocks)],
        scratch_shapes=[pltpu.SemaphoreType.DMA((7 * n,)), pltpu.SemaphoreType.DMA((7 * n,)),
                        pltpu.SemaphoreType.DMA((n,))],
    )(*blocks, *extra)


def _share_halves(name, arrays):
    n = len(arrays)

    def body(*refs):
        ins, outs = refs[:n], refs[n:2 * n]
        send_sems, recv_sems = refs[2 * n:]
        x, y, c = _place()
        started = []
        for a in range(n):
            cp = pltpu.make_async_remote_copy(src_ref=ins[a].at[c], dst_ref=outs[a].at[c], send_sem=send_sems.at[a],
                                              recv_sem=recv_sems.at[a], device_id=(x, y, 1 - c), device_id_type=MESH)
            cp.start()
            started.append(cp)
        for a in range(n):
            started[a].wait_send()
            pltpu.make_async_remote_copy(src_ref=ins[a].at[1 - c], dst_ref=outs[a].at[1 - c], send_sem=send_sems.at[a],
                                         recv_sem=recv_sems.at[a], device_id=(x, y, 1 - c), device_id_type=MESH).wait_recv()

    return pl.pallas_call(
        body, name=name, in_specs=[ANY] * n, out_specs=[ANY] * n,
        out_shape=[jax.ShapeDtypeStruct(a.shape, a.dtype) for a in arrays],
        input_output_aliases={a: a for a in range(n)},
        scratch_shapes=[pltpu.SemaphoreType.DMA((n,)), pltpu.SemaphoreType.DMA((n,))],
    )(*arrays)


HBM = pl.BlockSpec(memory_space=pltpu.HBM)
SEM = pl.BlockSpec(memory_space=pltpu.SEMAPHORE)
EFFECT = pltpu.SideEffectType.DATAFLOW_SIDE_EFFECTING


COPIES_PER_ARRAY = {"rows": 3, "fill": 3, "parts": 3, "halves": 4}


def _chip_copies(kind, srcs, dsts, send_sems, recv_sems):
    x, y, c = _place()
    mine = 2 * x + y
    per = COPIES_PER_ARRAY[kind]
    sends, arrivals = [], []
    for a in range(len(srcs)):
        if kind == "halves":
            for j in range(N_CHIP):
                cp = pltpu.make_async_remote_copy(
                    src_ref=srcs[a].at[2 * j + 1 - c], dst_ref=dsts[a].at[j], send_sem=send_sems.at[per * a + j],
                    recv_sem=recv_sems.at[per * a + j], device_id=(x, y, 1 - c), device_id_type=MESH)
                sends.append(cp)
                arrivals.append(cp)
            continue
        for k, (px, py) in enumerate([(1 - x, y), (x, 1 - y), (1 - x, 1 - y)]):
            other = 2 * px + py
            if kind == "fill":
                cp = dict(send_sem=send_sems.at[per * a + k], recv_sem=recv_sems.at[per * a + k], device_id=(x, y, 1 - c),
                          device_id_type=MESH)
                sends.append(pltpu.make_async_remote_copy(src_ref=srcs[a].at[2 * other + c], dst_ref=dsts[a].at[2 * other + c], **cp))
                arrivals.append(pltpu.make_async_remote_copy(src_ref=srcs[a].at[2 * other + c],
                                                             dst_ref=dsts[a].at[2 * other + 1 - c], **cp))
                continue
            if kind == "rows":
                src, dst, lands = srcs[a].at[2 * mine + c], dsts[a].at[2 * mine + c], dsts[a].at[2 * other + c]
            else:
                src, dst, lands = srcs[a].at[other], dsts[a].at[mine], dsts[a].at[other]
            sem = dict(send_sem=send_sems.at[per * a + k], recv_sem=recv_sems.at[per * a + k], device_id=(px, py, c),
                       device_id_type=MESH)
            sends.append(pltpu.make_async_remote_copy(src_ref=src, dst_ref=dst, **sem))
            arrivals.append(pltpu.make_async_remote_copy(src_ref=src, dst_ref=lands, **sem))
    return sends, arrivals


def _chips_start(name, kind, srcs, dsts=None, after=None):
    n = len(srcs)
    bufs = list(srcs) + (list(dsts) if dsts is not None else [])
    nb = len(bufs)
    extra = [] if after is None else [after]

    def body(*refs):
        ins = refs[:nb]
        send_sems, recv_sems = refs[nb + len(extra)], refs[nb + len(extra) + 1]
        token = refs[-1]
        sends, _ = _chip_copies(kind, ins[:n], ins[n:] if dsts is not None else ins[:n], send_sems, recv_sems)
        for cp in sends:
            cp.start()
        token[...] = jnp.zeros_like(token)

    out = pl.pallas_call(
        body, name=name,
        out_shape=(pltpu.SemaphoreType.DMA((COPIES_PER_ARRAY[kind] * n,)), pltpu.SemaphoreType.DMA((COPIES_PER_ARRAY[kind] * n,)),
                   *[pltpu.HBM(b.shape, b.dtype) for b in bufs], jax.ShapeDtypeStruct((8, HEAD), F32)),
        in_specs=(HBM,) * nb + (ANY,) * len(extra),
        out_specs=(SEM, SEM) + (HBM,) * nb + (pl.BlockSpec(memory_space=pltpu.VMEM),),
        input_output_aliases={i: 2 + i for i in range(nb)},
        compiler_params=pltpu.CompilerParams(has_side_effects=EFFECT),
    )(*[pltpu.with_memory_space_constraint(b, pltpu.HBM) for b in bufs], *extra)
    return out[0], out[1], list(out[2:2 + nb]), out[-1]


def _chips_wait(name, kind, n, send_sems, recv_sems, bufs, after):
    nb = len(bufs)

    def body(*refs):
        ins = refs[:nb]
        s_sems, r_sems = refs[nb], refs[nb + 1]
        sends, arrivals = _chip_copies(kind, ins[:n], ins[n:] if nb > n else ins[:n], s_sems, r_sems)
        for cp in sends:
            cp.wait_send()
        for cp in arrivals:
            cp.wait_recv()

    return list(pl.pallas_call(
        body, name=name, out_shape=tuple(pltpu.HBM(b.shape, b.dtype) for b in bufs),
        in_specs=(HBM,) * nb + (SEM, SEM, pl.BlockSpec(memory_space=pl.ANY)), out_specs=(HBM,) * nb,
        input_output_aliases={i: i for i in range(nb)},
        compiler_params=pltpu.CompilerParams(has_side_effects=EFFECT),
    )(*bufs, send_sems, recv_sems, after))


def _fill_from_sibling(name, arrays):
    n = len(arrays)

    def body(*refs):
        ins, outs = refs[:n], refs[n:2 * n]
        send_sems, recv_sems = refs[2 * n:]
        x, y, c = _place()
        sends, arrivals = [], []
        for a in range(n):
            for k, (px, py) in enumerate([(1 - x, y), (x, 1 - y), (1 - x, 1 - y)]):
                sem = dict(send_sem=send_sems.at[3 * a + k], recv_sem=recv_sems.at[3 * a + k], device_id=(x, y, 1 - c),
                           device_id_type=MESH)
                row = 2 * (2 * px + py)
                sends.append(pltpu.make_async_remote_copy(src_ref=ins[a].at[row + c], dst_ref=outs[a].at[row + c], **sem))
                arrivals.append(pltpu.make_async_remote_copy(src_ref=ins[a].at[row + c], dst_ref=outs[a].at[row + 1 - c], **sem))
        for cp in sends:
            cp.start()
        for cp in sends:
            cp.wait_send()
        for cp in arrivals:
            cp.wait_recv()

    return pl.pallas_call(
        body, name=name, in_specs=[ANY] * n, out_specs=[ANY] * n,
        out_shape=[jax.ShapeDtypeStruct(a.shape, a.dtype) for a in arrays],
        input_output_aliases={a: a for a in range(n)},
        scratch_shapes=[pltpu.SemaphoreType.DMA((3 * n,)), pltpu.SemaphoreType.DMA((3 * n,))],
    )(*arrays)


def kernel(x, c, ada_w, ada_b, mix_norm, ffn_norm, par_w_in, par_w_out, hg_lb_logits, hg_out_norm, sg_w_in, sg_v_gain, sg_v_bias, sg_w_pos, sg_b_pos, sg_w_out, ffn_up, ffn_conv_w, ffn_conv_b, ffn_down, final_norm, loss_target, m_ada_w, m_ada_b, m_mix_norm, m_ffn_norm, m_par_w_in, m_par_w_out, m_hg_lb_logits, m_hg_out_norm, m_sg_w_in, m_sg_v_gain, m_sg_v_bias, m_sg_w_pos, m_sg_b_pos, m_sg_w_out, m_ffn_up, m_ffn_conv_w, m_ffn_conv_b, m_ffn_down, m_final_norm, v_ada_w, v_ada_b, v_mix_norm, v_ffn_norm, v_par_w_in, v_par_w_out, v_hg_lb_logits, v_hg_out_norm, v_sg_w_in, v_sg_v_gain, v_sg_v_bias, v_sg_w_pos, v_sg_b_pos, v_sg_w_out, v_ffn_up, v_ffn_conv_w, v_ffn_conv_b, v_ffn_down, v_final_norm):
    names = ["ada_w", "ada_b", "mix_norm", "ffn_norm", "par_w_in", "par_w_out", "hg_lb_logits", "hg_out_norm", "sg_w_in",
             "sg_v_gain", "sg_v_bias", "sg_w_pos", "sg_b_pos", "sg_w_out", "ffn_up", "ffn_conv_w", "ffn_conv_b",
             "ffn_down", "final_norm"]
    W = dict(zip(names, [ada_w, ada_b, mix_norm, ffn_norm, par_w_in, par_w_out, hg_lb_logits, hg_out_norm, sg_w_in,
                         sg_v_gain, sg_v_bias, sg_w_pos, sg_b_pos, sg_w_out, ffn_up, ffn_conv_w, ffn_conv_b, ffn_down,
                         final_norm]))
    M = dict(zip(names, [m_ada_w, m_ada_b, m_mix_norm, m_ffn_norm, m_par_w_in, m_par_w_out, m_hg_lb_logits, m_hg_out_norm,
                         m_sg_w_in, m_sg_v_gain, m_sg_v_bias, m_sg_w_pos, m_sg_b_pos, m_sg_w_out, m_ffn_up, m_ffn_conv_w,
                         m_ffn_conv_b, m_ffn_down, m_final_norm]))
    V = dict(zip(names, [v_ada_w, v_ada_b, v_mix_norm, v_ffn_norm, v_par_w_in, v_par_w_out, v_hg_lb_logits, v_hg_out_norm,
                         v_sg_w_in, v_sg_v_gain, v_sg_v_bias, v_sg_w_pos, v_sg_b_pos, v_sg_w_out, v_ffn_up, v_ffn_conv_w,
                         v_ffn_conv_b, v_ffn_down, v_final_norm]))

    x = x[0]
    target = loss_target[0]
    T, D = x.shape
    ix, iy, ic = _place()
    chip = 2 * ix + iy
    dev = 2 * chip + ic
    H = hg_out_norm.shape[1]
    SBW = H * HEAD
    NA = ada_w.shape[2]
    F2s = ffn_up.shape[2]
    F2 = N_CHIP * F2s
    SGW = sg_w_out.shape[1] * N_CHIP
    G = sg_w_pos.shape[1]

    shards = [par_w_in[0], par_w_out[0], sg_w_in[0], sg_w_out[0], ffn_up[0], ffn_up[1], ffn_down[0], ffn_down[1]]
    kinds = ["col", "row", "col", "row", "col", "col", "row", "row"]
    halves = [w.reshape(2, w.shape[0] // 2, w.shape[1]) for w in shards]
    groups = {"a": [0], "b": [1, 4, 6], "c": [2, 3, 5, 7]}
    rows8 = {0: _cast_into_rows("cast_w", halves[0], chip)}
    started = {}

    def as_weights(g, bufs):
        out = {}
        for i, g8 in zip(groups[g], bufs):
            K, N = shards[i].shape
            out[i] = g8.reshape(N_CHIP, K, N) if kinds[i] == "col" else g8.reshape(N_CHIP * K, N)
        return out

    def weights_landed(g, after):
        send_sems, recv_sems, bufs, _ = started[g]
        bufs = _chips_wait("gather_wait_" + g, "rows", len(bufs), send_sems, recv_sems, bufs, after)
        return _chips_start("gather_fill_start_" + g, "fill", bufs)

    def weights_of(g, filling, after):
        send_sems, recv_sems, bufs, _ = filling
        return as_weights(g, _chips_wait("gather_fill_wait_" + g, "fill", len(bufs), send_sems, recv_sems, bufs, after))

    n_cw = ffn_conv_w.size
    n_sv = sg_v_gain.size
    c_all, small_all = _all_gather("gather_small", [c, _pack_rows([ffn_conv_w, sg_v_gain, sg_v_bias])])
    c_all = c_all.reshape(N_DEV, D)
    small_all = small_all.reshape(N_CHIP, 2, -1)[:, 0]
    conv_w_full = small_all[:, :n_cw].reshape(N_CHIP, 2, CONV_WIDTH, F2s).transpose(1, 2, 0, 3).reshape(2, CONV_WIDTH, F2)
    sg_gain_full = small_all[:, n_cw:n_cw + n_sv].reshape(1, SGW)
    sg_bias_full = small_all[:, n_cw + n_sv:n_cw + 2 * n_sv].reshape(1, SGW)

    c_pad = jnp.pad(c_all, ((0, 16 - N_DEV), (0, 0)))
    ada_b_sh = lax.dynamic_slice(ada_b, (0, chip * NA), (2, NA)).reshape(2, 1, NA)
    mod_sh = _ada_fwd(c_pad, ada_w, ada_b_sh)
    mod_all, = _all_gather("gather_mod", [mod_sh[:, :N_DEV]])
    mod_all = mod_all.reshape(N_CHIP, 2, 2, N_DEV, NA)[:, 0]
    mod = lax.dynamic_index_in_dim(mod_all, dev, axis=2, keepdims=False)
    mod = mod.transpose(1, 0, 2).reshape(2, 6, D)
    mods = [[mod[l, k].reshape(1, D) for k in range(6)] for l in range(2)]
    started["a"] = _chips_start("gather_start_a", "rows", [rows8[0]], after=mod)
    for i in range(1, len(shards)):
        rows8[i] = _cast_into_rows("cast_w", halves[i], chip, after=started["a"][3])

    vec = lambda a: a.reshape(1, -1)
    l0 = vec(hg_lb_logits[0])
    l1 = vec(hg_lb_logits[1])
    hg_gain = vec(hg_out_norm[0])
    wpos = sg_w_pos[0]
    bpos = sg_b_pos[0].reshape(G, SG_CHUNK, 1)
    conv_b = [vec(ffn_conv_b[l]) for l in range(2)]

    sh1, sc1, g1, sh2, sc2, g2 = mods[0]
    send_a, recv_a, bufs_a, _ = started["a"]
    bufs_a = _chips_wait("gather_wait_a", "rows", 1, send_a, recv_a, bufs_a, rows8[len(shards) - 1])
    bufs_a = _fill_from_sibling("gather_fill_a", bufs_a)
    started["b"] = _chips_start("gather_start_b", "rows", [rows8[i] for i in groups["b"]], after=bufs_a[0])
    start_token = started["a"][3][0, 0] + started["b"][3][0, 0]
    w_in = as_weights("a", bufs_a)[0]
    h0 = _normmod_fwd("norm_mix0", x, vec(mix_norm[0]) + start_token, sc1, sh1)
    proj = _mm_nn("mm_par_in", h0, w_in)
    o_sb, sb_tot = _sb_fwd(proj, H)
    filling_b = weights_landed("b", o_sb)
    started["c"] = _chips_start("gather_start_c", "rows", [rows8[i] for i in groups["c"]], after=o_sb)
    o_hg, hg_states = _hg_fwd(proj, l0 + filling_b[3][0:1, 0:1] + started["c"][3][0:1, 0:1], l1, hg_gain, H, 3 * H)
    o_cat = jnp.concatenate([o_sb, o_hg], axis=1)
    wb = weights_of("b", filling_b, o_cat)
    w_out, wup, wdn = wb[1], [wb[4], None], [wb[6], None]
    y0 = _mm_nn("mm_par_out", o_cat, w_out)
    x1, h0f = _res_normmod_fwd("res_norm_ffn0", x, y0, g1, vec(ffn_norm[0]), sc2, sh2)
    a0 = _mm_nn("mm_up0", h0f, wup[0])
    u0 = _conv_fwd("conv_fwd0", a0, conv_w_full[0], conv_b[0])
    filling_c = weights_landed("c", u0)
    f0 = _mm_nn("mm_down0", u0, wdn[0])
    sh1b, sc1b, g1b, sh2b, sc2b, g2b = mods[1]
    x2, h1 = _res_normmod_fwd("res_norm_mix1", x1, f0, g2, vec(mix_norm[1]) + filling_c[3][0:1, 0:1], sc1b, sh1b)
    wc = weights_of("c", filling_c, h1)
    wsg_in, wsg_out, wup[1], wdn[1] = wc[2], wc[3], wc[5], wc[7]
    zpre = _mm_nn("mm_sg_in", h1, wsg_in)
    s1 = _sg_fwd(zpre, sg_gain_full, sg_bias_full, wpos, bpos)
    y1 = _mm_nn("mm_sg_out", s1, wsg_out)
    x3, h1f = _res_normmod_fwd("res_norm_ffn1", x2, y1, g1b, vec(ffn_norm[1]), sc2b, sh2b)
    a1 = _mm_nn("mm_up1", h1f, wup[1])
    u1 = _conv_fwd("conv_fwd1", a1, conv_w_full[1], conv_b[1])
    f1 = _mm_nn("mm_down1", u1, wdn[1])
    loss_sum, dx, df1, dg2b, d_final = _final_fwd_bwd(x3, f1, g2b, vec(final_norm), target)
    loss = lax.psum(loss_sum[0, 0], ("x", "y", "c"))

    def reduce_start(tag, idx, grads):
        eights = [g.reshape((N_DEV, -1, g.shape[-1])) for g in grads]
        landing = [lax.empty((N_CHIP,) + e.shape[1:], e.dtype) for e in eights]
        send_sems, recv_sems, bufs, token = _chips_start("grads_sibling_start_" + tag, "halves", eights, landing)
        return (tag, idx, send_sems, recv_sems, bufs), token[0:1, 0:1]

    def reduce_cross(state, after):
        tag, idx, send_sems, recv_sems, bufs = state
        n = len(idx)
        bufs = _chips_wait("grads_sibling_wait_" + tag, "halves", n, send_sems, recv_sems, bufs, after)
        pair = [_add_pairs("add_pair", e, r, ic) for e, r in zip(bufs[:n], bufs[n:])]
        landing = [lax.empty(p.shape, p.dtype) for p in pair]
        send_sems, recv_sems, bufs, token = _chips_start("grads_start_" + tag, "parts", pair, landing)
        return (tag, idx, send_sems, recv_sems, bufs), token[0:1, 0:1]

    def reduce_finish(state, after):
        tag, idx, send_sems, recv_sems, bufs = state
        n = len(idx)
        bufs = _chips_wait("grads_wait_" + tag, "parts", n, send_sems, recv_sems, bufs, after)
        halves = [_sum_into_pair("sum_chips", p, x_, ic, chip) for p, x_ in zip(bufs[:n], bufs[n:])]
        both = _share_halves("grads_share_" + tag, halves)
        return {i: b.reshape(shards[i].shape) for i, b in zip(idx, both)}

    def ffn_bwd(l, dfl, u, a, hf):
        g_dn = _mm_tn("mm_g_down", u, dfl)
        du = _mm_nt("mm_d_u", dfl, wdn[l])
        da, dcw, dcb = _conv_bwd("conv_bwd", a, du, conv_w_full[l], conv_b[l])
        g_up = _mm_tn("mm_g_up", hf, da, chunks=N_CHIP)
        dh = _mm_nt("mm_d_hf", da, wup[l])
        return g_dn, g_up, dcw, dcb, dh

    g_dn1, g_up1, dcw1, dcb1, dh1f = ffn_bwd(1, df1, u1, a1, h1f)
    red1, tok = reduce_start("1", [5, 7], [g_up1, g_dn1])
    dx, dgn_f1, dsc2b, dsh2b, dy1, dg1b = _block_bwd("bwd_ffn1", dx, dh1f, x3, vec(ffn_norm[1]) + tok, sc2b, sh2b, y1, g1b)
    g_sg_out = _mm_tn("mm_g_sg_out", s1, dy1)
    ds1 = _mm_nt("mm_d_s", dy1, wsg_out)
    dzpre, dsg_gain, dsg_bias, dwpos, dbpos = _sg_bwd(zpre, ds1, sg_gain_full, sg_bias_full, wpos, bpos)
    red1, tok_x = reduce_cross(red1, dzpre)
    g_sg_in = _mm_tn("mm_g_sg_in", h1, dzpre, chunks=N_CHIP)
    dh1 = _mm_nt("mm_d_h1", dzpre, wsg_in)
    red2, tok = reduce_start("2", [2, 3], [g_sg_in, g_sg_out])
    dx, dgn_m1, dsc1b, dsh1b, df0, dg2 = _block_bwd("bwd_mix1", dx, dh1, x2, vec(mix_norm[1]) + tok + tok_x, sc1b, sh1b, f0, g2)
    g_dn0, g_up0, dcw0, dcb0, dh0f = ffn_bwd(0, df0, u0, a0, h0f)
    red2, tok_x = reduce_cross(red2, dh0f)
    red3, tok = reduce_start("3", [4, 6], [g_up0, g_dn0])
    dx, dgn_f0, dsc2, dsh2, dy0, dg1 = _block_bwd("bwd_ffn0", dx, dh0f, x1, vec(ffn_norm[0]) + tok + tok_x, sc2, sh2, y0, g1)
    g_out = _mm_tn("mm_g_par_out", o_cat, dy0)
    do = _mm_nt("mm_d_o", dy0, w_out)
    red3, tok_x = reduce_cross(red3, do)
    dhq, dhf, dhi, dhg, dl0, dl1, dhg_gain = _hg_bwd(proj, hg_states, do, l0 + tok_x, l1, hg_gain, H, 3 * H, H)
    dq, dk, dv = _sb_bwd(proj, do, sb_tot, H)
    dproj = jnp.concatenate([dq, dk, dv, dhq, dhf, dhi, dhg], axis=1).astype(BF16)
    g_in = _mm_tn("mm_g_par_in", h0, dproj, chunks=N_CHIP)
    red4, tok = reduce_start("4", [0, 1], [g_in, g_out])
    dh0 = _mm_nt("mm_d_h0", dproj, w_in)
    grad_x, dgn_m0, dsc1, dsh1 = _block_bwd("bwd_mix0", dx, dh0, x, vec(mix_norm[0]) + tok, sc1, sh1)
    red4, tok_x = reduce_cross(red4, grad_x)

    G_, delta, new_m, new_v = {}, {}, {}, {}

    def adam_on(nme):
        shp = W[nme].shape
        r2 = lambda a: a.reshape(-1, shp[-1])
        d_, m_, v_ = _adam("adam_" + nme, r2(W[nme]), r2(G_[nme]), r2(M[nme]), r2(V[nme]))
        delta[nme], new_m[nme], new_v[nme] = d_.reshape(shp), m_.reshape(shp), v_.reshape(shp)

    g_shards = {}
    for state in (red1, red2, red3):
        g_shards.update(reduce_finish(state, red4[4][0]))
    G_["sg_w_in"] = g_shards[2][None]
    G_["sg_w_out"] = g_shards[3][None]
    G_["ffn_up"] = jnp.stack([g_shards[4], g_shards[5]])
    G_["ffn_down"] = jnp.stack([g_shards[6], g_shards[7]])
    for nme in ["sg_w_in", "sg_w_out", "ffn_up", "ffn_down"]:
        adam_on(nme)

    dmod = jnp.concatenate([dsh1, dsc1, dg1, dsh2, dsc2, dg2, dsh1b, dsc1b, dg1b, dsh2b, dsc2b, dg2b], axis=1)
    parts = [dmod, dgn_m0, dgn_m1, dgn_f0, dgn_f1, dl0, dl1, dhg_gain, dsg_gain, dsg_bias, dwpos, dbpos,
             dcw0, dcw1, dcb0, dcb1, d_final]
    sizes = [p.size for p in parts]
    packed = _pack_rows(parts)
    packed_all, = _all_gather("gather_small_grads", [packed], after=new_v["ffn_down"])
    summed = _sum_leading("sum_small_grads", packed_all).reshape(-1)
    offs = [0]
    for s in sizes:
        offs.append(offs[-1] + s)
    red = [summed[offs[i]:offs[i + 1]] for i in range(len(parts))]
    (r_dmod, r_gm0, r_gm1, r_gf0, r_gf1, r_l0, r_l1, r_hgain, r_sgain, r_sbias, r_wpos, r_bpos,
     r_cw0, r_cw1, r_cb0, r_cb1, r_final) = red
    n_mod = sizes[0]
    dmod_all = packed_all.reshape(N_DEV, -1)[:, :n_mod].reshape(N_DEV, 2, 6 * D)

    G_["ada_b"] = r_dmod.reshape(2, 6 * D)
    G_["mix_norm"] = jnp.stack([r_gm0, r_gm1])
    G_["ffn_norm"] = jnp.stack([r_gf0, r_gf1])
    G_["hg_lb_logits"] = jnp.stack([r_l0, r_l1])
    G_["hg_out_norm"] = r_hgain.reshape(hg_out_norm.shape)
    G_["sg_v_gain"] = lax.dynamic_slice(r_sgain, (chip * n_sv,), (n_sv,)).reshape(sg_v_gain.shape)
    G_["sg_v_bias"] = lax.dynamic_slice(r_sbias, (chip * n_sv,), (n_sv,)).reshape(sg_v_bias.shape)
    G_["sg_w_pos"] = r_wpos.reshape(sg_w_pos.shape)
    G_["sg_b_pos"] = r_bpos.reshape(sg_b_pos.shape)
    cw_full = jnp.stack([r_cw0.reshape(CONV_WIDTH, F2), r_cw1.reshape(CONV_WIDTH, F2)])
    G_["ffn_conv_w"] = lax.dynamic_slice(cw_full, (0, 0, chip * F2s), (2, CONV_WIDTH, F2s))
    G_["ffn_conv_b"] = jnp.stack([r_cb0, r_cb1])
    G_["final_norm"] = r_final

    c_t = jnp.pad(c_all, ((0, HEAD - N_DEV), (0, 0))).T
    dmod_sh = lax.dynamic_slice(dmod_all.transpose(1, 0, 2), (0, 0, chip * NA), (2, N_DEV, NA))
    dmod_sh = jnp.pad(dmod_sh, ((0, 0), (0, HEAD - N_DEV), (0, 0)))
    G_["ada_w"], delta["ada_w"], new_m["ada_w"], new_v["ada_w"] = _ada_grad_adam(c_t, dmod_sh, ada_w, m_ada_w, v_ada_w)

    g_shards.update(reduce_finish(red4, G_["ada_w"]))
    G_["par_w_in"] = g_shards[0][None]
    G_["par_w_out"] = g_shards[1][None]
    for nme in ["par_w_in", "par_w_out"]:
        adam_on(nme)
    small = [n_ for n_ in names if n_ not in delta]
    pk = lambda dct: _pack_rows([dct[n_] for n_ in small])
    d_, m_, v_ = _adam("adam_small", pk(W), pk(G_), pk(M), pk(V))
    off = 0
    for n_ in small:
        sz = W[n_].size
        for dst, src in ((delta, d_), (new_m, m_), (new_v, v_)):
            dst[n_] = src.reshape(-1)[off:off + sz].reshape(W[n_].shape)
        off += sz

    return (loss, grad_x[None], *[G_[n_] for n_ in names], *[delta[n_] for n_ in names],
            *[new_m[n_] for n_ in names], *[new_v[n_] for n_ in names])
```

```python
import functools
import math

import jax
import jax.numpy as jnp
from jax import lax
from jax.experimental import pallas as pl
from jax.experimental.pallas import tpu as pltpu

F32 = jnp.float32
BF16 = jnp.bfloat16
MESH = pl.DeviceIdType.MESH
ANY = pl.BlockSpec(memory_space=pl.ANY)

NORM_EPS = 1e-6
ADAM_LR = 0.001
ADAM_B1 = 0.9
ADAM_B2 = 0.999
ADAM_EPS = 1e-08
ADAM_WD = 0.01
ADAM_STEP = 10
CONV_WIDTH = 3
HEAD = 128
HG_CHUNK = 64
SG_CHUNK = 128
N_DEV = 8
N_CHIP = 4
V7X_VMEM_LIMIT = 56 * 1024 * 1024


def _cp(*sem):
    return pltpu.CompilerParams(dimension_semantics=sem if sem else None, vmem_limit_bytes=V7X_VMEM_LIMIT)


def _pick(n, prefs):
    for p in prefs:
        if p <= n and n % p == 0:
            return p
    return n


def _iota(shape, axis):
    return lax.broadcasted_iota(jnp.int32, shape, axis)


def _rows_within(R, row_bytes, budget):
    if R * row_bytes <= budget:
        return R
    for t in (1024, 512, 256, 128, 64, 32, 16):
        if R % t == 0 and t * row_bytes <= budget:
            return t
    return _pick(R, (16, 8))


def _pack_rows(arrays):
    flat = jnp.concatenate([a.reshape(-1) for a in arrays])
    pad = (-flat.size) % (8 * HEAD)
    return jnp.pad(flat, (0, pad)).reshape(-1, HEAD)


def _dg(a, b, ca, cb):
    if a.ndim == 3:
        dims = (((ca + 1,), (cb + 1,)), ((0,), (0,)))
    else:
        dims = (((ca,), (cb,)), ((), ()))
    return lax.dot_general(a.astype(BF16), b.astype(BF16), dims, preferred_element_type=F32)


@jax.custom_vjp
def mm_nn(a, b):
    return _dg(a, b, 1, 0)


mm_nn.defvjp(lambda a, b: (_dg(a, b, 1, 0), (a, b)),
             lambda r, g: (_dg(g, r[1], 1, 1), _dg(r[0], g, 0, 0)))


@jax.custom_vjp
def mm_nt(a, b):
    return _dg(a, b, 1, 1)


mm_nt.defvjp(lambda a, b: (_dg(a, b, 1, 1), (a, b)),
             lambda r, g: (_dg(g, r[1], 1, 0), _dg(g, r[0], 0, 0)))


@jax.custom_vjp
def mm_tn(a, b):
    return _dg(a, b, 0, 0)


mm_tn.defvjp(lambda a, b: (_dg(a, b, 0, 0), (a, b)),
             lambda r, g: (_dg(r[1], g, 1, 1), _dg(r[0], g, 1, 0)))


def _split(x):
    hi = x.astype(BF16)
    lo = (x - hi.astype(F32)).astype(BF16)
    return hi, lo


def _sum_right(x, m01):
    hi, lo = _split(x)
    return _dg(hi, m01, 1, 0) + _dg(lo, m01, 1, 0)


def _sum_left_impl(m01, x, ca):
    if x.ndim == 3:
        m01 = jnp.broadcast_to(m01, (x.shape[0],) + m01.shape)
    hi, lo = _split(x)
    return _dg(m01, hi, ca, 0) + _dg(m01, lo, ca, 0)


@jax.custom_vjp
def _sum_left(m01, x):
    return _sum_left_impl(m01, x, 1)


_sum_left.defvjp(lambda m, x: (_sum_left_impl(m, x, 1), m),
                 lambda m, g: (None, _sum_left_impl(m, g, 0)))


def _sigmoid(x):
    return 1.0 / (1.0 + jnp.exp(-x))


def _softplus(z):
    return jnp.maximum(z, 0.0) + jnp.log(1.0 + jnp.exp(-jnp.abs(z)))


_INV_SQRT2 = 1.0 / math.sqrt(2.0)
_INV_SQRT2PI = 1.0 / math.sqrt(2.0 * math.pi)


@jax.custom_vjp
def _gelu(x):
    return 0.5 * x * (1.0 + lax.erf(x * _INV_SQRT2))


_gelu.defvjp(lambda x: (0.5 * x * (1.0 + lax.erf(x * _INV_SQRT2)), x),
             lambda x, g: (g * (0.5 * (1.0 + lax.erf(x * _INV_SQRT2)) + x * jnp.exp(-0.5 * x * x) * _INV_SQRT2PI),))


def _rms(x, gain):
    r = lax.rsqrt(jnp.mean(x * x, axis=-1, keepdims=True) + NORM_EPS)
    return x * r * gain


def _normmod(x, gain, sc, sh):
    return _rms(x, gain) * (1.0 + sc) + sh


def _mm_call(name, a, b, out_shape, out_dtype, dims, grid, a_spec, b_spec, o_spec, acc_shape):
    nk = grid[2]

    def body(a_ref, b_ref, o_ref, *scratch):
        part = lax.dot_general(a_ref[...].astype(BF16), b_ref[...].astype(BF16), dims, preferred_element_type=F32)
        if nk == 1:
            o_ref[...] = part.astype(o_ref.dtype)
            return
        acc_ref, = scratch
        k = pl.program_id(2)

        @pl.when(k == 0)
        def _():
            acc_ref[...] = part

        @pl.when(k > 0)
        def _():
            acc_ref[...] += part

        @pl.when(k == nk - 1)
        def _():
            o_ref[...] = acc_ref[...].astype(o_ref.dtype)

    return pl.pallas_call(
        body, name=name, grid=grid, in_specs=[a_spec, b_spec], out_specs=o_spec,
        out_shape=jax.ShapeDtypeStruct(out_shape, out_dtype),
        scratch_shapes=[] if nk == 1 else [pltpu.VMEM(acc_shape, F32)],
        compiler_params=_cp("parallel", "parallel", "arbitrary"),
    )(a, b)


def _mm_nn(name, a, b, out_dtype=F32):
    M, K = a.shape
    chunked = b.ndim == 3
    Nc = b.shape[-1]
    N = Nc * (b.shape[0] if chunked else 1)
    tm = _pick(M, (1024, 512, 256, 128, 64, 32, 16, 8))
    tn = _pick(Nc, (1408, 1024, 896, 512, 256, 128))
    tk = _pick(K, (2048, 1408, 1024, 512, 256, 128))
    npc = Nc // tn
    if chunked:
        b_spec = pl.BlockSpec((None, tk, tn), lambda i, j, k: (j // npc, k, j % npc))
    else:
        b_spec = pl.BlockSpec((tk, tn), lambda i, j, k: (k, j))
    return _mm_call(name, a, b, (M, N), out_dtype, (((1,), (0,)), ((), ())), (M // tm, N // tn, K // tk),
                    pl.BlockSpec((tm, tk), lambda i, j, k: (i, k)), b_spec,
                    pl.BlockSpec((tm, tn), lambda i, j, k: (i, j)), (tm, tn))


def _mm_nt(name, a, b, out_dtype=F32):
    planar = a.ndim == 3
    M, Np = a.shape[-2:]
    N = Np * (a.shape[0] if planar else 1)
    chunked = b.ndim == 3
    Nc = b.shape[-1]
    K = b.shape[-2]
    tm = _pick(M, (1024, 512, 256, 128, 64, 32, 16, 8))
    tn = _pick(K, (1408, 1024, 512, 256, 128))
    tk = _pick(Nc, (2048, 1792, 1408, 1024, 896, 512, 256, 128))
    assert Np % tk == 0
    npc = Nc // tk
    npp = Np // tk
    if chunked:
        b_spec = pl.BlockSpec((None, tn, tk), lambda i, j, k: (k // npc, j, k % npc))
    else:
        b_spec = pl.BlockSpec((tn, tk), lambda i, j, k: (j, k))
    if planar:
        a_spec = pl.BlockSpec((None, tm, tk), lambda i, j, k: (k // npp, i, k % npp))
    else:
        a_spec = pl.BlockSpec((tm, tk), lambda i, j, k: (i, k))
    return _mm_call(name, a, b, (M, K), out_dtype, (((1,), (1,)), ((), ())), (M // tm, K // tn, N // tk),
                    a_spec, b_spec, pl.BlockSpec((tm, tn), lambda i, j, k: (i, j)), (tm, tn))


def _mm_tn(name, a, b, chunks=1, out_dtype=BF16):
    T, K = a.shape
    planar = b.ndim == 3
    Np = b.shape[-1]
    N = Np * (b.shape[0] if planar else 1)
    Nc = N // chunks
    tm = _pick(K, (1408, 1024, 512, 256, 128))
    tn = _pick(Nc, (1408, 1024, 896, 512, 256, 128))
    tk = _pick(T, (1024, 512, 256, 128))
    assert Np % tn == 0
    npc = Nc // tn
    npp = Np // tn
    if planar:
        b_spec = pl.BlockSpec((None, tk, tn), lambda i, j, k: (j // npp, k, j % npp))
    else:
        b_spec = pl.BlockSpec((tk, tn), lambda i, j, k: (k, j))
    if chunks > 1:
        shape = (chunks, K, Nc)
        o_spec = pl.BlockSpec((None, tm, tn), lambda i, j, k: (j // npc, i, j % npc))
    else:
        shape = (K, N)
        o_spec = pl.BlockSpec((tm, tn), lambda i, j, k: (i, j))
    return _mm_call(name, a, b, shape, out_dtype, (((0,), (0,)), ((), ())), (K // tm, N // tn, T // tk),
                    pl.BlockSpec((tk, tm), lambda i, j, k: (k, i)), b_spec, o_spec, (tm, tn))


def _row_tile(T):
    return _pick(T, (256, 128, 64, 32, 16, 8))


def _vec_spec(D):
    return pl.BlockSpec((1, D), lambda i: (0, 0))


def _normmod_fwd(name, x, gain, sc, sh):
    T, D = x.shape
    bt = _row_tile(T)

    def body(x_ref, g_ref, sc_ref, sh_ref, h_ref):
        h_ref[...] = _normmod(x_ref[...], g_ref[...], sc_ref[...], sh_ref[...]).astype(h_ref.dtype)

    rows = pl.BlockSpec((bt, D), lambda i: (i, 0))
    return pl.pallas_call(body, name=name, grid=(T // bt,), in_specs=[rows] + [_vec_spec(D)] * 3, out_specs=rows,
                          out_shape=jax.ShapeDtypeStruct((T, D), BF16), compiler_params=_cp("parallel"))(x, gain, sc, sh)


def _res_normmod_fwd(name, x, y, g, gain, sc, sh):
    T, D = x.shape
    bt = _row_tile(T)

    def body(x_ref, y_ref, gate_ref, g_ref, sc_ref, sh_ref, x1_ref, h_ref):
        x1 = x_ref[...] + gate_ref[...] * y_ref[...]
        x1_ref[...] = x1
        h_ref[...] = _normmod(x1, g_ref[...], sc_ref[...], sh_ref[...]).astype(h_ref.dtype)

    rows = pl.BlockSpec((bt, D), lambda i: (i, 0))
    return pl.pallas_call(body, name=name, grid=(T // bt,), in_specs=[rows, rows] + [_vec_spec(D)] * 4,
                          out_specs=[rows, rows],
                          out_shape=[jax.ShapeDtypeStruct((T, D), F32), jax.ShapeDtypeStruct((T, D), BF16)],
                          compiler_params=_cp("parallel"))(x, y, g, gain, sc, sh)


def _final_fwd_bwd(x, y, g, gain, target):
    T, D = x.shape
    bt = _row_tile(T)

    def body(x_ref, y_ref, gate_ref, g_ref, t_ref, loss_ref, dx_ref, dy_ref, dgate_ref, dgain_ref):
        i = pl.program_id(0)
        yv = y_ref[...]
        gate = gate_ref[...]
        x4 = x_ref[...] + gate * yv
        out, vjp = jax.vjp(_rms, x4, g_ref[...])
        err = out - t_ref[...]
        dx4, dgain = vjp(err * (1.0 / D))
        part = 0.5 * jnp.sum(jnp.mean(err * err, axis=-1, keepdims=True), axis=0, keepdims=True)

        @pl.when(i == 0)
        def _():
            loss_ref[...] = jnp.zeros_like(loss_ref)
            dgate_ref[...] = jnp.zeros_like(dgate_ref)
            dgain_ref[...] = jnp.zeros_like(dgain_ref)

        loss_ref[...] += jnp.broadcast_to(part, loss_ref.shape)
        dx_ref[...] = dx4
        dy_ref[...] = (gate * dx4).astype(dy_ref.dtype)
        dgate_ref[...] += jnp.sum(dx4 * yv, axis=0, keepdims=True)
        dgain_ref[...] += dgain

    rows = pl.BlockSpec((bt, D), lambda i: (i, 0))
    vec = _vec_spec(D)
    return pl.pallas_call(
        body, name="final_loss", grid=(T // bt,), in_specs=[rows, rows, vec, vec, rows],
        out_specs=[pl.BlockSpec((1, HEAD), lambda i: (0, 0)), rows, rows, vec, vec],
        out_shape=[jax.ShapeDtypeStruct((1, HEAD), F32), jax.ShapeDtypeStruct((T, D), F32),
                   jax.ShapeDtypeStruct((T, D), BF16), jax.ShapeDtypeStruct((1, D), F32),
                   jax.ShapeDtypeStruct((1, D), F32)],
        compiler_params=_cp("arbitrary"))(x, y, g, gain, target)


def _block_bwd(name, dx_out, dh, x_in, gain, sc, sh, y_prev=None, g_prev=None):
    T, D = x_in.shape
    bt = _row_tile(T)
    has_prev = y_prev is not None

    def body(*refs):
        if has_prev:
            dxo_ref, dh_ref, x_ref, g_ref, sc_ref, sh_ref, y_ref, gp_ref, dx_ref, dgain_ref, dsc_ref, dsh_ref, dy_ref, dgp_ref = refs
        else:
            dxo_ref, dh_ref, x_ref, g_ref, sc_ref, sh_ref, dx_ref, dgain_ref, dsc_ref, dsh_ref = refs
        i = pl.program_id(0)
        _, vjp = jax.vjp(_normmod, x_ref[...], g_ref[...], sc_ref[...], sh_ref[...])
        dxn, dgain, dsc, dsh = vjp(dh_ref[...])
        dx = dxo_ref[...] + dxn
        dx_ref[...] = dx

        @pl.when(i == 0)
        def _():
            dgain_ref[...] = jnp.zeros_like(dgain_ref)
            dsc_ref[...] = jnp.zeros_like(dsc_ref)
            dsh_ref[...] = jnp.zeros_like(dsh_ref)
            if has_prev:
                dgp_ref[...] = jnp.zeros_like(dgp_ref)

        dgain_ref[...] += dgain
        dsc_ref[...] += dsc
        dsh_ref[...] += dsh
        if has_prev:
            dy_ref[...] = (gp_ref[...] * dx).astype(dy_ref.dtype)
            dgp_ref[...] += jnp.sum(dx * y_ref[...], axis=0, keepdims=True)

    rows = pl.BlockSpec((bt, D), lambda i: (i, 0))
    vec = _vec_spec(D)
    ins = [dx_out, dh, x_in, gain, sc, sh]
    in_specs = [rows, rows, rows, vec, vec, vec]
    out_specs = [rows, vec, vec, vec]
    out_shape = [jax.ShapeDtypeStruct((T, D), F32)] + [jax.ShapeDtypeStruct((1, D), F32)] * 3
    if has_prev:
        ins += [y_prev, g_prev]
        in_specs += [rows, vec]
        out_specs += [rows, vec]
        out_shape += [jax.ShapeDtypeStruct((T, D), BF16), jax.ShapeDtypeStruct((1, D), F32)]
    return pl.pallas_call(body, name=name, grid=(T // bt,), in_specs=in_specs, out_specs=out_specs,
                          out_shape=out_shape, compiler_params=_cp("arbitrary"))(*ins)


def _sb_tiles(T):
    tq = _pick(T, (512, 256, 128))
    return tq, tq // HEAD


def _sb_fwd(proj, H):
    T = proj.shape[0]
    tq, nsub = _sb_tiles(T)
    scale = HEAD ** -0.5

    def body(q_ref, k_ref, v_ref, o_ref, l_ref, acc_ref):
        i = pl.program_id(1)
        q = q_ref[...].astype(BF16)
        later = (_iota((HEAD, HEAD), 0) > _iota((HEAD, HEAD), 1)).astype(BF16)
        row = _iota((tq, HEAD), 0)
        col = _iota((tq, HEAD), 1)

        def key_step(j, c, diagonal):
            off = pl.multiple_of(j * tq, tq)
            k = k_ref[pl.ds(off, tq), :].astype(BF16)
            v = v_ref[pl.ds(off, tq), :].astype(BF16)
            z = _dg(q, k, 1, 1) * scale
            ws = [None] * nsub
            for s in reversed(range(nsub)):
                zs = z[:, s * HEAD:(s + 1) * HEAD]
                sp = _softplus(zs)
                if diagonal:
                    strict = (s * HEAD + col) < row
                    lk = jnp.where(strict, -sp, 0.0)
                else:
                    lk = -sp
                w = jnp.exp(zs - sp + _sum_right(lk, later) + c)
                if diagonal:
                    w = jnp.where(strict, w, 0.0)
                ws[s] = w.astype(BF16)
                c = c + jnp.sum(lk, axis=1, keepdims=True)
            acc_ref[...] += _dg(jnp.concatenate(ws, axis=1), v, 1, 0)
            return c

        acc_ref[...] = jnp.zeros_like(acc_ref)
        c = key_step(i, jnp.zeros((tq, 1), F32), True)
        c = lax.fori_loop(0, i, lambda n, c: key_step(i - 1 - n, c, False), c)
        o_ref[...] = acc_ref[...].astype(o_ref.dtype)
        l_ref[...] = jnp.broadcast_to(c, (tq, HEAD))

    blk = pl.BlockSpec((tq, HEAD), lambda h, i: (i, h))
    return pl.pallas_call(
        body, name="sb_fwd", grid=(H, T // tq),
        in_specs=[blk, pl.BlockSpec((T, HEAD), lambda h, i: (0, H + h)), pl.BlockSpec((T, HEAD), lambda h, i: (0, 2 * H + h))],
        out_specs=[blk, blk],
        out_shape=[jax.ShapeDtypeStruct((T, H * HEAD), BF16), jax.ShapeDtypeStruct((T, H * HEAD), F32)],
        scratch_shapes=[pltpu.VMEM((tq, HEAD), F32)],
        compiler_params=_cp("parallel", "arbitrary"))(proj, proj, proj)


def _sb_bwd(proj, do, L, H):
    T = proj.shape[0]
    tq, nsub = _sb_tiles(T)
    scale = HEAD ** -0.5

    def body(q_ref, k_ref, v_ref, do_ref, l_ref, dq_ref, dk_ref, dv_ref):
        i = pl.program_id(1)

        @pl.when(i == 0)
        def _():
            dk_ref[...] = jnp.zeros_like(dk_ref)
            dv_ref[...] = jnp.zeros_like(dv_ref)

        dq_ref[...] = jnp.zeros_like(dq_ref)
        q = q_ref[...].astype(BF16)
        do_ = do_ref[...].astype(BF16)
        total = l_ref[...]
        upto = (_iota((HEAD, HEAD), 0) <= _iota((HEAD, HEAD), 1)).astype(BF16)
        before = (_iota((HEAD, HEAD), 0) < _iota((HEAD, HEAD), 1)).astype(BF16)
        row = _iota((tq, HEAD), 0)
        col = _iota((tq, HEAD), 1)

        def key_step(j, carry, diagonal):
            cp, ce = carry
            off = pl.multiple_of(j * tq, tq)
            k = k_ref[pl.ds(off, tq), :].astype(BF16)
            v = v_ref[pl.ds(off, tq), :].astype(BF16)
            z = _dg(q, k, 1, 1) * scale
            dw = _dg(do_, v, 1, 1)
            ws, dzs = [], []
            for s in range(nsub):
                zs = z[:, s * HEAD:(s + 1) * HEAD]
                sp = _softplus(zs)
                if diagonal:
                    strict = (s * HEAD + col) < row
                    lk = jnp.where(strict, -sp, 0.0)
                else:
                    lk = -sp
                tail = total - (_sum_right(lk, upto) + cp)
                w = jnp.exp(zs - sp + tail)
                if diagonal:
                    w = jnp.where(strict, w, 0.0)
                e = w * dw[:, s * HEAD:(s + 1) * HEAD]
                e_before = _sum_right(e, before) + ce
                sig = jnp.exp(zs - sp)
                dz = (e * (1.0 - sig) - e_before * sig) * scale
                if diagonal:
                    dz = jnp.where(strict, dz, 0.0)
                ws.append(w.astype(BF16))
                dzs.append(dz.astype(BF16))
                cp = cp + jnp.sum(lk, axis=1, keepdims=True)
                ce = ce + jnp.sum(e, axis=1, keepdims=True)
            w_all = jnp.concatenate(ws, axis=1)
            dz_all = jnp.concatenate(dzs, axis=1)
            dv_ref[pl.ds(off, tq), :] += _dg(w_all, do_, 0, 0)
            dk_ref[pl.ds(off, tq), :] += _dg(dz_all, q, 0, 0)
            dq_ref[...] += _dg(dz_all, k, 1, 0)
            return cp, ce

        zero = jnp.zeros((tq, 1), F32)
        carry = lax.fori_loop(0, i, lambda j, cr: key_step(j, cr, False), (zero, zero))
        key_step(i, carry, True)

    blk = pl.BlockSpec((tq, HEAD), lambda h, i: (i, h))
    full = pl.BlockSpec((T, HEAD), lambda h, i: (0, h))
    shp = jax.ShapeDtypeStruct((T, H * HEAD), F32)
    return pl.pallas_call(
        body, name="sb_bwd", grid=(H, T // tq),
        in_specs=[blk, pl.BlockSpec((T, HEAD), lambda h, i: (0, H + h)), pl.BlockSpec((T, HEAD), lambda h, i: (0, 2 * H + h)),
                  blk, blk],
        out_specs=[blk, full, full], out_shape=[shp, shp, shp],
        compiler_params=_cp("parallel", "arbitrary"))(proj, proj, proj, do, L)


def _hg_tile(q, fl, iv, g, st, l0, l1, gain):
    R = 2 * HG_CHUNK
    row = _iota((R, R), 0)
    col = _iota((R, R), 1)
    first = row < HG_CHUNK
    same = first == (col < HG_CHUNK)
    tri = (row >= col) & same
    lb = _sigmoid(l0 - l1)
    f = lb + (1.0 - lb) * _sigmoid(fl)
    logf = jnp.log(f)
    k = 1.0 - f
    qf = q * _sigmoid(q)
    G = _sum_left(tri.astype(BF16), logf)
    gl_a = jnp.sum(jnp.where(first, logf, 0.0), axis=-2, keepdims=True)
    gl_b = jnp.sum(jnp.where(first, 0.0, logf), axis=-2, keepdims=True)
    q_dec = qf * jnp.exp(G)
    k_inv = k * jnp.exp(-G)
    k_end = k * jnp.exp(jnp.where(first, gl_a, gl_b) - G)
    scores = jnp.where(tri, mm_nt(q_dec, k_inv), 0.0)
    o = mm_nn(scores, iv)
    o_a = mm_nt(q_dec, st)
    st_mid = st * jnp.exp(gl_a) + mm_tn(jnp.where(first, iv, 0.0), k_end)
    o_b = mm_nt(q_dec, st_mid)
    st_new = st_mid * jnp.exp(gl_b) + mm_tn(jnp.where(first, 0.0, iv), k_end)
    o = o + jnp.where(first, o_a, o_b)
    on = o * lax.rsqrt(jnp.mean(o * o, axis=-1, keepdims=True) + NORM_EPS) * gain
    return on * (g * _sigmoid(g)), st_new


def _hg_heads(H):
    return _pick(H, (8, 4, 2, 1))


def _hg_specs(H, c0, rev, nt):
    hb = _hg_heads(H)
    w = hb * HEAD

    def at(base):
        if rev:
            return pl.BlockSpec((HEAD, w), lambda h, i: (nt - 1 - i, base // hb + h))
        return pl.BlockSpec((HEAD, w), lambda h, i: (i, base // hb + h))
    return [at(c0), at(c0 + H), at(c0 + 2 * H), at(c0 + 3 * H)]


def _hg_fwd(proj, l0, l1, gain, H, c0):
    T = proj.shape[0]
    nt = T // HEAD
    hb = _hg_heads(H)
    w = hb * HEAD

    def body(q_ref, f_ref, i_ref, g_ref, l0_ref, l1_ref, gain_ref, o_ref, st_out_ref, st_ref):
        @pl.when(pl.program_id(1) == 0)
        def _():
            st_ref[...] = jnp.zeros_like(st_ref)

        sl = [slice(j * HEAD, (j + 1) * HEAD) for j in range(hb)]
        heads = lambda ref: jnp.stack([ref[:, s] for s in sl])
        st = st_ref[...]
        st_out_ref[...] = st
        out, st_new = _hg_tile(heads(q_ref), heads(f_ref), heads(i_ref), heads(g_ref), st, heads(l0_ref), heads(l1_ref),
                               heads(gain_ref))
        for j, s in enumerate(sl):
            o_ref[:, s] = out[j].astype(o_ref.dtype)
        st_ref[...] = st_new

    vec = pl.BlockSpec((1, w), lambda h, i: (0, h))
    return pl.pallas_call(
        body, name="hg_fwd", grid=(H // hb, nt), in_specs=_hg_specs(H, c0, False, nt) + [vec, vec, vec],
        out_specs=[pl.BlockSpec((HEAD, w), lambda h, i: (i, h)),
                   pl.BlockSpec((hb, None, HEAD, HEAD), lambda h, i: (h, i, 0, 0))],
        out_shape=[jax.ShapeDtypeStruct((T, H * HEAD), BF16), jax.ShapeDtypeStruct((H, nt, HEAD, HEAD), F32)],
        scratch_shapes=[pltpu.VMEM((hb, HEAD, HEAD), F32)],
        compiler_params=_cp("parallel", "arbitrary"))(proj, proj, proj, proj, l0, l1, gain)


def _hg_bwd(proj, states, do, l0, l1, gain, H, c0, do_c0):
    T = proj.shape[0]
    nt = T // HEAD
    hb = _hg_heads(H)
    w = hb * HEAD

    def body(q_ref, f_ref, i_ref, g_ref, st_in_ref, do_ref, l0_ref, l1_ref, gain_ref,
             dq_ref, df_ref, di_ref, dg_ref, dl0_ref, dl1_ref, dgain_ref, dst_ref):
        @pl.when(pl.program_id(1) == 0)
        def _():
            dst_ref[...] = jnp.zeros_like(dst_ref)
            dl0_ref[...] = jnp.zeros_like(dl0_ref)
            dl1_ref[...] = jnp.zeros_like(dl1_ref)
            dgain_ref[...] = jnp.zeros_like(dgain_ref)

        sl = [slice(j * HEAD, (j + 1) * HEAD) for j in range(hb)]
        heads = lambda ref: jnp.stack([ref[:, s] for s in sl])
        _, vjp = jax.vjp(_hg_tile, heads(q_ref), heads(f_ref), heads(i_ref), heads(g_ref), st_in_ref[...],
                         heads(l0_ref), heads(l1_ref), heads(gain_ref))
        dq, df, di, dg, dst, dl0, dl1, dgain = vjp((heads(do_ref), dst_ref[...]))
        dst_ref[...] = dst
        for j, s in enumerate(sl):
            dq_ref[:, s] = dq[j]
            df_ref[:, s] = df[j]
            di_ref[:, s] = di[j]
            dg_ref[:, s] = dg[j]
            dl0_ref[:, s] += dl0[j]
            dl1_ref[:, s] += dl1[j]
            dgain_ref[:, s] += dgain[j]

    vec = pl.BlockSpec((1, w), lambda h, i: (0, h))
    rblk = pl.BlockSpec((HEAD, w), lambda h, i: (nt - 1 - i, h))
    shp = jax.ShapeDtypeStruct((T, H * HEAD), F32)
    vshp = jax.ShapeDtypeStruct((1, H * HEAD), F32)
    return pl.pallas_call(
        body, name="hg_bwd", grid=(H // hb, nt),
        in_specs=_hg_specs(H, c0, True, nt) + [
            pl.BlockSpec((hb, None, HEAD, HEAD), lambda h, i: (h, nt - 1 - i, 0, 0)),
            pl.BlockSpec((HEAD, w), lambda h, i: (nt - 1 - i, do_c0 // hb + h)), vec, vec, vec],
        out_specs=[rblk, rblk, rblk, rblk, vec, vec, vec],
        out_shape=[shp, shp, shp, shp, vshp, vshp, vshp],
        scratch_shapes=[pltpu.VMEM((hb, HEAD, HEAD), F32)],
        compiler_params=_cp("parallel", "arbitrary"))(proj, proj, proj, proj, states, do, l0, l1, gain)


def _sg_chunk(u_parts, v_parts, gains, biases, wpos, bpos):
    W = sum(p.shape[1] for p in v_parts)
    C = v_parts[0].shape[0]
    v = [_gelu(p) for p in v_parts]
    mu = sum(jnp.sum(p, axis=-1, keepdims=True) for p in v) * (1.0 / W)
    xc = [p - mu for p in v]
    r = lax.rsqrt(sum(jnp.sum(p * p, axis=-1, keepdims=True) for p in xc) * (1.0 / W) + NORM_EPS)
    causal = _iota((C, C), 0) >= _iota((C, C), 1)
    vn = jnp.stack([p * r * gn + bs for p, gn, bs in zip(xc, gains, biases)])
    mixed = mm_nn(jnp.stack([jnp.where(causal, w, 0.0) for w in wpos]), vn) + jnp.stack(bpos)
    return [_gelu(up) * mixed[n] for n, up in enumerate(u_parts)]


def _sg_fwd(zpre, vgain, vbias, wpos, bpos):
    T, W2 = zpre.shape
    W = W2 // 2
    G = wpos.shape[0]
    cg = W // G
    C = SG_CHUNK

    def body(z_ref, gn_ref, bs_ref, w_ref, b_ref, s_ref):
        sl = [slice(g * cg, (g + 1) * cg) for g in range(G)]
        out = _sg_chunk([z_ref[:, s] for s in sl], [z_ref[:, W + s.start:W + s.stop] for s in sl],
                        [gn_ref[:, s] for s in sl], [bs_ref[:, s] for s in sl],
                        [w_ref[g] for g in range(G)], [b_ref[g] for g in range(G)])
        for s, o in zip(sl, out):
            s_ref[:, s] = o.astype(s_ref.dtype)

    return pl.pallas_call(
        body, name="sg_fwd", grid=(T // C,),
        in_specs=[pl.BlockSpec((C, W2), lambda i: (i, 0)), _vec_spec(W), _vec_spec(W),
                  pl.BlockSpec((G, C, C), lambda i: (0, 0, 0)), pl.BlockSpec((G, C, 1), lambda i: (0, 0, 0))],
        out_specs=pl.BlockSpec((C, W), lambda i: (i, 0)),
        out_shape=jax.ShapeDtypeStruct((T, W), BF16), compiler_params=_cp("parallel"))(zpre, vgain, vbias, wpos, bpos)


def _sg_bwd(zpre, ds, vgain, vbias, wpos, bpos):
    T, W2 = zpre.shape
    W = W2 // 2
    G = wpos.shape[0]
    cg = W // G
    C = SG_CHUNK

    def body(z_ref, ds_ref, gn_ref, bs_ref, w_ref, b_ref, dz_ref, dgn_ref, dbs_ref, dw_ref, db_ref):
        @pl.when(pl.program_id(0) == 0)
        def _():
            dgn_ref[...] = jnp.zeros_like(dgn_ref)
            dbs_ref[...] = jnp.zeros_like(dbs_ref)
            dw_ref[...] = jnp.zeros_like(dw_ref)
            db_ref[...] = jnp.zeros_like(db_ref)

        sl = [slice(g * cg, (g + 1) * cg) for g in range(G)]
        _, vjp = jax.vjp(_sg_chunk, [z_ref[:, s] for s in sl], [z_ref[:, W + s.start:W + s.stop] for s in sl],
                         [gn_ref[:, s] for s in sl], [bs_ref[:, s] for s in sl],
                         [w_ref[g] for g in range(G)], [b_ref[g] for g in range(G)])
        du, dv, dgn, dbs, dw, db = vjp([ds_ref[:, s] for s in sl])
        for g, s in enumerate(sl):
            dz_ref[:, s] = du[g].astype(dz_ref.dtype)
            dz_ref[:, W + s.start:W + s.stop] = dv[g].astype(dz_ref.dtype)
            dgn_ref[:, s] += dgn[g]
            dbs_ref[:, s] += dbs[g]
            dw_ref[g] += dw[g]
            db_ref[g] += db[g]

    wspec = pl.BlockSpec((G, C, C), lambda i: (0, 0, 0))
    bspec = pl.BlockSpec((G, C, 1), lambda i: (0, 0, 0))
    return pl.pallas_call(
        body, name="sg_bwd", grid=(T // C,),
        in_specs=[pl.BlockSpec((C, W2), lambda i: (i, 0)), pl.BlockSpec((C, W), lambda i: (i, 0)),
                  _vec_spec(W), _vec_spec(W), wspec, bspec],
        out_specs=[pl.BlockSpec((C, W2), lambda i: (i, 0)), _vec_spec(W), _vec_spec(W), wspec, bspec],
        out_shape=[jax.ShapeDtypeStruct((T, W2), BF16), jax.ShapeDtypeStruct((1, W), F32),
                   jax.ShapeDtypeStruct((1, W), F32), jax.ShapeDtypeStruct((G, C, C), F32),
                   jax.ShapeDtypeStruct((G, C, 1), F32)],
        compiler_params=_cp("arbitrary"))(zpre, ds, vgain, vbias, wpos, bpos)


def _conv_tiles(T, F):
    return _pick(T, (512, 256, 128, 64, 32, 16, 8)), _pick(F, (512, 256, 128))


def _shift_down(cur, prev8, n, first_tile):
    bt = cur.shape[0]
    r = pltpu.roll(cur, n, 0)
    p = pltpu.roll(prev8, n, 0)
    p = jnp.where(first_tile, 0.0, p)
    head = jnp.where(_iota(p.shape, 0) < n, p, r[:8])
    return jnp.concatenate([head, r[8:]], axis=0) if bt > 8 else head


def _shift_up(cur, next8, n, last_tile):
    bt = cur.shape[0]
    r = pltpu.roll(cur, bt - n, 0)
    p = pltpu.roll(next8, 8 - n, 0)
    p = jnp.where(last_tile, 0.0, p)
    tail = jnp.where(_iota(p.shape, 0) >= 8 - n, p, r[bt - 8:])
    return jnp.concatenate([r[:bt - 8], tail], axis=0) if bt > 8 else tail


def _conv_apply(cur, prev8, w_ref, b, first_tile):
    return (b + w_ref[0:1, :] * _shift_down(cur, prev8, 2, first_tile)
            + w_ref[1:2, :] * _shift_down(cur, prev8, 1, first_tile) + w_ref[2:3, :] * cur)


def _conv_fwd(name, a, w, b):
    T, F2 = a.shape
    F = F2 // 2
    bt, cw = _conv_tiles(T, F)
    nf = F // cw
    r8 = bt // 8

    def body(g_ref, gp_ref, v_ref, vp_ref, wg_ref, wv_ref, bg_ref, bv_ref, u_ref):
        first = pl.program_id(0) == 0
        gate = _conv_apply(g_ref[...], gp_ref[...], wg_ref, bg_ref[...], first)
        val = _conv_apply(v_ref[...], vp_ref[...], wv_ref, bv_ref[...], first)
        u_ref[...] = (gate * _sigmoid(gate) * val).astype(u_ref.dtype)

    def cur(off):
        return pl.BlockSpec((bt, cw), lambda i, j: (i, j + off))

    def prev(off):
        return pl.BlockSpec((8, cw), lambda i, j: (jnp.maximum(i * r8 - 1, 0), j + off))

    def vec(rows, off):
        return pl.BlockSpec((rows, cw), lambda i, j: (0, j + off))

    return pl.pallas_call(
        body, name=name, grid=(T // bt, nf),
        in_specs=[cur(0), prev(0), cur(nf), prev(nf), vec(3, 0), vec(3, nf), vec(1, 0), vec(1, nf)],
        out_specs=pl.BlockSpec((bt, cw), lambda i, j: (i, j)),
        out_shape=jax.ShapeDtypeStruct((T, F), BF16),
        compiler_params=_cp("parallel", "parallel"))(a, a, a, a, w, w, b, b)


def _conv_bwd(name, a, du, w, b):
    T, F2 = a.shape
    F = F2 // 2
    bt, cw = _conv_tiles(T, F)
    nf = F // cw
    r8 = bt // 8
    last_blk = T // 8 - 1

    def body(g_ref, gp_ref, gn_ref, v_ref, vp_ref, vn_ref, du_ref, dun_ref, wg_ref, wv_ref, bg_ref, bv_ref,
             da_ref, dwg_ref, dwv_ref, dbg_ref, dbv_ref):
        i = pl.program_id(1)
        first = i == 0
        last = i == pl.num_programs(1) - 1

        def taps(cur, prev8, at_start):
            return _shift_down(cur, prev8, 2, at_start), _shift_down(cur, prev8, 1, at_start), cur

        def conv(t, w_ref, b_ref):
            return b_ref[...] + w_ref[0:1, :] * t[0] + w_ref[1:2, :] * t[1] + w_ref[2:3, :] * t[2]

        def act_bwd(gate, val, du_):
            sg = _sigmoid(gate)
            return du_ * val * (sg * (1.0 + gate * (1.0 - sg))), du_ * gate * sg

        g_cur, v_cur = g_ref[...], v_ref[...]
        tg = taps(g_cur, gp_ref[...], first)
        tv = taps(v_cur, vp_ref[...], first)
        dg, dv = act_bwd(conv(tg, wg_ref, bg_ref), conv(tv, wv_ref, bv_ref), du_ref[...])
        tgn = taps(gn_ref[...], g_cur[bt - 8:, :], False)
        tvn = taps(vn_ref[...], v_cur[bt - 8:, :], False)
        dgn, dvn = act_bwd(conv(tgn, wg_ref, bg_ref), conv(tvn, wv_ref, bv_ref), dun_ref[...])

        def conv_t(d, dn, w_ref):
            return w_ref[2:3, :] * d + w_ref[1:2, :] * _shift_up(d, dn, 1, last) + w_ref[0:1, :] * _shift_up(d, dn, 2, last)

        da_ref[0] = conv_t(dg, dgn, wg_ref).astype(da_ref.dtype)
        da_ref[1] = conv_t(dv, dvn, wv_ref).astype(da_ref.dtype)

        @pl.when(first)
        def _():
            dwg_ref[...] = jnp.zeros_like(dwg_ref)
            dwv_ref[...] = jnp.zeros_like(dwv_ref)
            dbg_ref[...] = jnp.zeros_like(dbg_ref)
            dbv_ref[...] = jnp.zeros_like(dbv_ref)

        for t in range(CONV_WIDTH):
            dwg_ref[t:t + 1, :] += jnp.sum(dg * tg[t], axis=0, keepdims=True)
            dwv_ref[t:t + 1, :] += jnp.sum(dv * tv[t], axis=0, keepdims=True)
        dbg_ref[...] += jnp.sum(dg, axis=0, keepdims=True)
        dbv_ref[...] += jnp.sum(dv, axis=0, keepdims=True)

    def cur(off):
        return pl.BlockSpec((bt, cw), lambda j, i: (i, j + off))

    def prev(off):
        return pl.BlockSpec((8, cw), lambda j, i: (jnp.maximum(i * r8 - 1, 0), j + off))

    def nxt(off):
        return pl.BlockSpec((8, cw), lambda j, i: (jnp.minimum((i + 1) * r8, last_blk), j + off))

    def vec(rows, off):
        return pl.BlockSpec((rows, cw), lambda j, i: (0, j + off))

    da, dwg, dwv, dbg, dbv = pl.pallas_call(
        body, name=name, grid=(nf, T // bt),
        in_specs=[cur(0), prev(0), nxt(0), cur(nf), prev(nf), nxt(nf), cur(0), nxt(0),
                  vec(3, 0), vec(3, nf), vec(1, 0), vec(1, nf)],
        out_specs=[pl.BlockSpec((2, bt, cw), lambda j, i: (0, i, j)), vec(3, 0), vec(3, 0), vec(1, 0), vec(1, 0)],
        out_shape=[jax.ShapeDtypeStruct((2, T, F), BF16), jax.ShapeDtypeStruct((3, F), F32), jax.ShapeDtypeStruct((3, F), F32),
                   jax.ShapeDtypeStruct((1, F), F32), jax.ShapeDtypeStruct((1, F), F32)],
        compiler_params=_cp("parallel", "arbitrary"))(a, a, a, a, a, a, du, du, w, w, b, b)
    return da, jnp.concatenate([dwg, dwv], axis=1), jnp.concatenate([dbg, dbv], axis=1)


def _ada_fwd(c_all, ada_w, ada_b):
    R, D = c_all.shape
    L, _, Ns = ada_w.shape
    tn = _pick(Ns, (512, 256, 128))

    def body(c_ref, w_ref, b_ref, o_ref):
        cv = c_ref[...]
        cond = cv * _sigmoid(cv)
        o_ref[...] = _dg(cond, w_ref[...], 1, 0) + b_ref[...]

    return pl.pallas_call(
        body, name="ada_fwd", grid=(L, Ns // tn),
        in_specs=[pl.BlockSpec((R, D), lambda l, j: (0, 0)), pl.BlockSpec((None, D, tn), lambda l, j: (l, 0, j)),
                  pl.BlockSpec((None, 1, tn), lambda l, j: (l, 0, j))],
        out_specs=pl.BlockSpec((None, R, tn), lambda l, j: (l, 0, j)),
        out_shape=jax.ShapeDtypeStruct((L, R, Ns), F32), compiler_params=_cp("parallel", "parallel"))(c_all, ada_w, ada_b)


def _adam_math(w, g, m, v):
    m2 = ADAM_B1 * m + (1.0 - ADAM_B1) * g
    v2 = ADAM_B2 * v + (1.0 - ADAM_B2) * (g * g)
    m_hat = m2 / (1.0 - ADAM_B1 ** ADAM_STEP)
    v_hat = v2 / (1.0 - ADAM_B2 ** ADAM_STEP)
    delta = -ADAM_LR * (m_hat / (jnp.sqrt(v_hat) + ADAM_EPS) + ADAM_WD * w)
    return delta, m2, v2


def _ada_grad_adam(c_all_t, dmod, w, m, v):
    D, R = c_all_t.shape
    L, _, Ns = dmod.shape
    tr = _rows_within(D, Ns * 4, 1 << 21)

    def body(c_ref, d_ref, w_ref, m_ref, v_ref, g_ref, dl_ref, m2_ref, v2_ref):
        cv = c_ref[...]
        g = _dg(cv * _sigmoid(cv), d_ref[...], 1, 0)
        g_ref[...] = g
        dl_ref[...], m2_ref[...], v2_ref[...] = _adam_math(w_ref[...], g, m_ref[...], v_ref[...])

    big = pl.BlockSpec((None, tr, Ns), lambda l, i: (l, i, 0))
    shp = jax.ShapeDtypeStruct((L, D, Ns), F32)
    return pl.pallas_call(
        body, name="ada_grad_adam", grid=(L, D // tr),
        in_specs=[pl.BlockSpec((tr, R), lambda l, i: (i, 0)), pl.BlockSpec((None, R, Ns), lambda l, i: (l, 0, 0)), big, big, big],
        out_specs=[big] * 4, out_shape=[shp] * 4, compiler_params=_cp("parallel", "parallel"))(c_all_t, dmod, w, m, v)


def _adam(name, w, g, m, v):
    R, C = w.shape
    tr = _rows_within(R, C * 4, 3 << 20)

    def body(w_ref, g_ref, m_ref, v_ref, dl_ref, m2_ref, v2_ref):
        dl_ref[...], m2_ref[...], v2_ref[...] = _adam_math(w_ref[...], g_ref[...], m_ref[...], v_ref[...])

    blk = pl.BlockSpec((tr, C), lambda i: (i, 0))
    shp = jax.ShapeDtypeStruct((R, C), F32)
    return pl.pallas_call(body, name=name, grid=(R // tr,), in_specs=[blk] * 4, out_specs=[blk] * 3,
                          out_shape=[shp] * 3, compiler_params=_cp("parallel"))(w, g, m, v)


def _cast_into_rows(name, w, chip, after=None):
    _, R, C = w.shape
    tr = _rows_within(R, C * 4, 1 << 22)
    extra = [] if after is None else [after]

    def body(chip_ref, w_ref, *rest):
        o_ref = rest[-1]
        o_ref[...] = w_ref[...].astype(BF16)

    grid_spec = pltpu.PrefetchScalarGridSpec(
        num_scalar_prefetch=1, grid=(2, R // tr),
        in_specs=[pl.BlockSpec((None, tr, C), lambda h, i, s: (h, i, 0))] + [ANY] * len(extra),
        out_specs=pl.BlockSpec((None, tr, C), lambda h, i, s: (2 * s[0] + h, i, 0)))
    return pl.pallas_call(body, name=name, grid_spec=grid_spec, out_shape=jax.ShapeDtypeStruct((N_DEV, R, C), BF16),
                          compiler_params=_cp("arbitrary", "arbitrary"))(chip.reshape(1).astype(jnp.int32), w, *extra)


def _add_pairs(name, eight, from_sib, c):
    _, R, C = from_sib.shape
    tr = _rows_within(R, C * 2, 1 << 22)

    def body(c_ref, a_ref, b_ref, o_ref):
        o_ref[...] = (a_ref[...].astype(F32) + b_ref[...].astype(F32)).astype(o_ref.dtype)

    blk = pl.BlockSpec((None, tr, C), lambda j, i, s: (j, i, 0))
    grid_spec = pltpu.PrefetchScalarGridSpec(
        num_scalar_prefetch=1, grid=(4, R // tr),
        in_specs=[pl.BlockSpec((None, tr, C), lambda j, i, s: (2 * j + s[0], i, 0)), blk], out_specs=blk)
    return pl.pallas_call(body, name=name, grid_spec=grid_spec, out_shape=jax.ShapeDtypeStruct(from_sib.shape, BF16),
                          compiler_params=_cp("arbitrary", "arbitrary"))(c.reshape(1).astype(jnp.int32), eight, from_sib)


def _sum_into_pair(name, own, landed, slot, chip):
    n, R, C = landed.shape
    tr = _rows_within(R, (n + 1) * C * landed.dtype.itemsize, 1 << 23)

    def body(idx_ref, own_ref, x_ref, o_ref):
        mine = idx_ref[1]
        acc = None
        for j in range(n):
            part = jnp.where(mine == j, own_ref[...], x_ref[j]).astype(F32)
            acc = part if acc is None else acc + part
        o_ref[...] = acc

    grid_spec = pltpu.PrefetchScalarGridSpec(
        num_scalar_prefetch=1, grid=(R // tr,),
        in_specs=[pl.BlockSpec((None, tr, C), lambda i, s: (s[1], i, 0)), pl.BlockSpec((n, tr, C), lambda i, s: (0, i, 0))],
        out_specs=pl.BlockSpec((None, tr, C), lambda i, s: (s[0], i, 0)))
    idx = jnp.stack([slot, chip]).astype(jnp.int32)
    return pl.pallas_call(body, name=name, grid_spec=grid_spec, out_shape=jax.ShapeDtypeStruct((2, R, C), F32),
                          compiler_params=_cp("arbitrary"))(idx, own, landed)


def _sum_leading(name, a, out_dtype=F32):
    n, R, C = a.shape
    tr = _rows_within(R, n * C * a.dtype.itemsize, 1 << 24)

    def body(a_ref, o_ref):
        acc = a_ref[0].astype(F32)
        for j in range(1, n):
            acc = acc + a_ref[j].astype(F32)
        o_ref[...] = acc.astype(o_ref.dtype)

    return pl.pallas_call(body, name=name, grid=(R // tr,), in_specs=[pl.BlockSpec((n, tr, C), lambda i: (0, i, 0))],
                          out_specs=pl.BlockSpec((tr, C), lambda i: (i, 0)),
                          out_shape=jax.ShapeDtypeStruct((R, C), out_dtype), compiler_params=_cp("parallel"))(a)


def _place():
    return lax.axis_index("x"), lax.axis_index("y"), lax.axis_index("c")


def _all_gather(name, blocks, halves=False, after=None):
    n = len(blocks)
    shapes = [b.shape[1:] if halves else b.shape for b in blocks]
    extra = [] if after is None else [after]

    def body(*refs):
        ins, outs = refs[:n], refs[n + len(extra):2 * n + len(extra)]
        send_sems, recv_sems, local_sems = refs[2 * n + len(extra):]
        x, y, c = _place()
        me, sibling = (x, y, c), (x, y, 1 - c)
        chips = [(1 - x, y), (x, 1 - y), (1 - x, 1 - y)]

        def rows(a, px, py, pc):
            return outs[a].at[4 * px + 2 * py + pc]

        def copy(a, k, block, to, src=None):
            return pltpu.make_async_remote_copy(
                src_ref=rows(a, *block) if src is None else src, dst_ref=rows(a, *block),
                send_sem=send_sems.at[7 * a + k], recv_sem=recv_sems.at[7 * a + k],
                device_id=to, device_id_type=MESH)

        started = []
        mine = []
        for a in range(n):
            src = ins[a].at[c] if halves else ins[a]
            mine.append(pltpu.make_async_copy(src, rows(a, *me), local_sems.at[a]))
            mine[-1].start()
            first = [copy(a, 0, me, sibling, src=src)]
            first += [copy(a, 1 + j, me, (*chip, c), src=src) for j, chip in enumerate(chips)]
            for cp in first:
                cp.start()
            started += first
        for j, chip in enumerate(chips):
            for a in range(n):
                copy(a, 1 + j, (*chip, c), me).wait_recv()
                passed = copy(a, 4 + j, (*chip, c), sibling)
                passed.start()
                started.append(passed)
        for a in range(n):
            copy(a, 0, sibling, me).wait_recv()
            for j, chip in enumerate(chips):
                copy(a, 4 + j, (*chip, 1 - c), me).wait_recv()
        for cp in started:
            cp.wait_send()
        for cp in mine:
            cp.wait()

    return pl.pallas_call(
        body, name=name, in_specs=[ANY] * (n + len(extra)), out_specs=[ANY] * n,
        out_shape=[jax.ShapeDtypeStruct((N_DEV,) + tuple(s), b.dtype) for s, b in zip(shapes, blocks)],
        scratch_shapes=[pltpu.SemaphoreType.DMA((7 * n,)), pltpu.SemaphoreType.DMA((7 * n,)),
                        pltpu.SemaphoreType.DMA((n,))],
    )(*blocks, *extra)


def _share_halves(name, arrays):
    n = len(arrays)

    def body(*refs):
        ins, outs = refs[:n], refs[n:2 * n]
        send_sems, recv_sems = refs[2 * n:]
        x, y, c = _place()
        started = []
        for a in range(n):
            cp = pltpu.make_async_remote_copy(src_ref=ins[a].at[c], dst_ref=outs[a].at[c], send_sem=send_sems.at[a],
                                              recv_sem=recv_sems.at[a], device_id=(x, y, 1 - c), device_id_type=MESH)
            cp.start()
            started.append(cp)
        for a in range(n):
            started[a].wait_send()
            pltpu.make_async_remote_copy(src_ref=ins[a].at[1 - c], dst_ref=outs[a].at[1 - c], send_sem=send_sems.at[a],
                                         recv_sem=recv_sems.at[a], device_id=(x, y, 1 - c), device_id_type=MESH).wait_recv()

    return pl.pallas_call(
        body, name=name, in_specs=[ANY] * n, out_specs=[ANY] * n,
        out_shape=[jax.ShapeDtypeStruct(a.shape, a.dtype) for a in arrays],
        input_output_aliases={a: a for a in range(n)},
        scratch_shapes=[pltpu.SemaphoreType.DMA((n,)), pltpu.SemaphoreType.DMA((n,))],
    )(*arrays)


HBM = pl.BlockSpec(memory_space=pltpu.HBM)
SEM = pl.BlockSpec(memory_space=pltpu.SEMAPHORE)
EFFECT = pltpu.SideEffectType.DATAFLOW_SIDE_EFFECTING


COPIES_PER_ARRAY = {"rows": 3, "fill": 3, "parts": 3, "halves": 4}


def _chip_copies(kind, srcs, dsts, send_sems, recv_sems):
    x, y, c = _place()
    mine = 2 * x + y
    per = COPIES_PER_ARRAY[kind]
    sends, arrivals = [], []
    for a in range(len(srcs)):
        if kind == "halves":
            for j in range(N_CHIP):
                cp = pltpu.make_async_remote_copy(
                    src_ref=srcs[a].at[2 * j + 1 - c], dst_ref=dsts[a].at[j], send_sem=send_sems.at[per * a + j],
                    recv_sem=recv_sems.at[per * a + j], device_id=(x, y, 1 - c), device_id_type=MESH)
                sends.append(cp)
                arrivals.append(cp)
            continue
        for k, (px, py) in enumerate([(1 - x, y), (x, 1 - y), (1 - x, 1 - y)]):
            other = 2 * px + py
            if kind == "fill":
                cp = dict(send_sem=send_sems.at[per * a + k], recv_sem=recv_sems.at[per * a + k], device_id=(x, y, 1 - c),
                          device_id_type=MESH)
                sends.append(pltpu.make_async_remote_copy(src_ref=srcs[a].at[2 * other + c], dst_ref=dsts[a].at[2 * other + c], **cp))
                arrivals.append(pltpu.make_async_remote_copy(src_ref=srcs[a].at[2 * other + c],
                                                             dst_ref=dsts[a].at[2 * other + 1 - c], **cp))
                continue
            if kind == "rows":
                src, dst, lands = srcs[a].at[2 * mine + c], dsts[a].at[2 * mine + c], dsts[a].at[2 * other + c]
            else:
                src, dst, lands = srcs[a].at[other], dsts[a].at[mine], dsts[a].at[other]
            sem = dict(send_sem=send_sems.at[per * a + k], recv_sem=recv_sems.at[per * a + k], device_id=(px, py, c),
                       device_id_type=MESH)
            sends.append(pltpu.make_async_remote_copy(src_ref=src, dst_ref=dst, **sem))
            arrivals.append(pltpu.make_async_remote_copy(src_ref=src, dst_ref=lands, **sem))
    return sends, arrivals


def _chips_start(name, kind, srcs, dsts=None, after=None):
    n = len(srcs)
    bufs = list(srcs) + (list(dsts) if dsts is not None else [])
    nb = len(bufs)
    extra = [] if after is None else [after]

    def body(*refs):
        ins = refs[:nb]
        send_sems, recv_sems = refs[nb + len(extra)], refs[nb + len(extra) + 1]
        token = refs[-1]
        sends, _ = _chip_copies(kind, ins[:n], ins[n:] if dsts is not None else ins[:n], send_sems, recv_sems)
        for cp in sends:
            cp.start()
        token[...] = jnp.zeros_like(token)

    out = pl.pallas_call(
        body, name=name,
        out_shape=(pltpu.SemaphoreType.DMA((COPIES_PER_ARRAY[kind] * n,)), pltpu.SemaphoreType.DMA((COPIES_PER_ARRAY[kind] * n,)),
                   *[pltpu.HBM(b.shape, b.dtype) for b in bufs], jax.ShapeDtypeStruct((8, HEAD), F32)),
        in_specs=(HBM,) * nb + (ANY,) * len(extra),
        out_specs=(SEM, SEM) + (HBM,) * nb + (pl.BlockSpec(memory_space=pltpu.VMEM),),
        input_output_aliases={i: 2 + i for i in range(nb)},
        compiler_params=pltpu.CompilerParams(has_side_effects=EFFECT),
    )(*[pltpu.with_memory_space_constraint(b, pltpu.HBM) for b in bufs], *extra)
    return out[0], out[1], list(out[2:2 + nb]), out[-1]


def _chips_wait(name, kind, n, send_sems, recv_sems, bufs, after):
    nb = len(bufs)

    def body(*refs):
        ins = refs[:nb]
        s_sems, r_sems = refs[nb], refs[nb + 1]
        sends, arrivals = _chip_copies(kind, ins[:n], ins[n:] if nb > n else ins[:n], s_sems, r_sems)
        for cp in sends:
            cp.wait_send()
        for cp in arrivals:
            cp.wait_recv()

    return list(pl.pallas_call(
        body, name=name, out_shape=tuple(pltpu.HBM(b.shape, b.dtype) for b in bufs),
        in_specs=(HBM,) * nb + (SEM, SEM, pl.BlockSpec(memory_space=pl.ANY)), out_specs=(HBM,) * nb,
        input_output_aliases={i: i for i in range(nb)},
        compiler_params=pltpu.CompilerParams(has_side_effects=EFFECT),
    )(*bufs, send_sems, recv_sems, after))


def _fill_from_sibling(name, arrays):
    n = len(arrays)

    def body(*refs):
        ins, outs = refs[:n], refs[n:2 * n]
        send_sems, recv_sems = refs[2 * n:]
        x, y, c = _place()
        sends, arrivals = [], []
        for a in range(n):
            for k, (px, py) in enumerate([(1 - x, y), (x, 1 - y), (1 - x, 1 - y)]):
                sem = dict(send_sem=send_sems.at[3 * a + k], recv_sem=recv_sems.at[3 * a + k], device_id=(x, y, 1 - c),
                           device_id_type=MESH)
                row = 2 * (2 * px + py)
                sends.append(pltpu.make_async_remote_copy(src_ref=ins[a].at[row + c], dst_ref=outs[a].at[row + c], **sem))
                arrivals.append(pltpu.make_async_remote_copy(src_ref=ins[a].at[row + c], dst_ref=outs[a].at[row + 1 - c], **sem))
        for cp in sends:
            cp.start()
        for cp in sends:
            cp.wait_send()
        for cp in arrivals:
            cp.wait_recv()

    return pl.pallas_call(
        body, name=name, in_specs=[ANY] * n, out_specs=[ANY] * n,
        out_shape=[jax.ShapeDtypeStruct(a.shape, a.dtype) for a in arrays],
        input_output_aliases={a: a for a in range(n)},
        scratch_shapes=[pltpu.SemaphoreType.DMA((3 * n,)), pltpu.SemaphoreType.DMA((3 * n,))],
    )(*arrays)


def kernel(x, c, ada_w, ada_b, mix_norm, ffn_norm, par_w_in, par_w_out, hg_lb_logits, hg_out_norm, sg_w_in, sg_v_gain, sg_v_bias, sg_w_pos, sg_b_pos, sg_w_out, ffn_up, ffn_conv_w, ffn_conv_b, ffn_down, final_norm, loss_target, m_ada_w, m_ada_b, m_mix_norm, m_ffn_norm, m_par_w_in, m_par_w_out, m_hg_lb_logits, m_hg_out_norm, m_sg_w_in, m_sg_v_gain, m_sg_v_bias, m_sg_w_pos, m_sg_b_pos, m_sg_w_out, m_ffn_up, m_ffn_conv_w, m_ffn_conv_b, m_ffn_down, m_final_norm, v_ada_w, v_ada_b, v_mix_norm, v_ffn_norm, v_par_w_in, v_par_w_out, v_hg_lb_logits, v_hg_out_norm, v_sg_w_in, v_sg_v_gain, v_sg_v_bias, v_sg_w_pos, v_sg_b_pos, v_sg_w_out, v_ffn_up, v_ffn_conv_w, v_ffn_conv_b, v_ffn_down, v_final_norm):
    names = ["ada_w", "ada_b", "mix_norm", "ffn_norm", "par_w_in", "par_w_out", "hg_lb_logits", "hg_out_norm", "sg_w_in",
             "sg_v_gain", "sg_v_bias", "sg_w_pos", "sg_b_pos", "sg_w_out", "ffn_up", "ffn_conv_w", "ffn_conv_b",
             "ffn_down", "final_norm"]
    W = dict(zip(names, [ada_w, ada_b, mix_norm, ffn_norm, par_w_in, par_w_out, hg_lb_logits, hg_out_norm, sg_w_in,
                         sg_v_gain, sg_v_bias, sg_w_pos, sg_b_pos, sg_w_out, ffn_up, ffn_conv_w, ffn_conv_b, ffn_down,
                         final_norm]))
    M = dict(zip(names, [m_ada_w, m_ada_b, m_mix_norm, m_ffn_norm, m_par_w_in, m_par_w_out, m_hg_lb_logits, m_hg_out_norm,
                         m_sg_w_in, m_sg_v_gain, m_sg_v_bias, m_sg_w_pos, m_sg_b_pos, m_sg_w_out, m_ffn_up, m_ffn_conv_w,
                         m_ffn_conv_b, m_ffn_down, m_final_norm]))
    V = dict(zip(names, [v_ada_w, v_ada_b, v_mix_norm, v_ffn_norm, v_par_w_in, v_par_w_out, v_hg_lb_logits, v_hg_out_norm,
                         v_sg_w_in, v_sg_v_gain, v_sg_v_bias, v_sg_w_pos, v_sg_b_pos, v_sg_w_out, v_ffn_up, v_ffn_conv_w,
                         v_ffn_conv_b, v_ffn_down, v_final_norm]))

    x = x[0]
    target = loss_target[0]
    T, D = x.shape
    ix, iy, ic = _place()
    chip = 2 * ix + iy
    dev = 2 * chip + ic
    H = hg_out_norm.shape[1]
    SBW = H * HEAD
    NA = ada_w.shape[2]
    F2s = ffn_up.shape[2]
    F2 = N_CHIP * F2s
    SGW = sg_w_out.shape[1] * N_CHIP
    G = sg_w_pos.shape[1]

    shards = [par_w_in[0], par_w_out[0], sg_w_in[0], sg_w_out[0], ffn_up[0], ffn_up[1], ffn_down[0], ffn_down[1]]
    kinds = ["col", "row", "col", "row", "col", "col", "row", "row"]
    halves = [w.reshape(2, w.shape[0] // 2, w.shape[1]) for w in shards]
    groups = {"a": [0], "b": [1, 4, 6], "c": [2, 3, 5, 7]}
    rows8 = {0: _cast_into_rows("cast_w", halves[0], chip)}
    started = {}

    def as_weights(g, bufs):
        out = {}
        for i, g8 in zip(groups[g], bufs):
            K, N = shards[i].shape
            out[i] = g8.reshape(N_CHIP, K, N) if kinds[i] == "col" else g8.reshape(N_CHIP * K, N)
        return out

    def weights_landed(g, after):
        send_sems, recv_sems, bufs, _ = started[g]
        bufs = _chips_wait("gather_wait_" + g, "rows", len(bufs), send_sems, recv_sems, bufs, after)
        return _chips_start("gather_fill_start_" + g, "fill", bufs)

    def weights_of(g, filling, after):
        send_sems, recv_sems, bufs, _ = filling
        return as_weights(g, _chips_wait("gather_fill_wait_" + g, "fill", len(bufs), send_sems, recv_sems, bufs, after))

    n_cw = ffn_conv_w.size
    n_sv = sg_v_gain.size
    c_all, small_all = _all_gather("gather_small", [c, _pack_rows([ffn_conv_w, sg_v_gain, sg_v_bias])])
    c_all = c_all.reshape(N_DEV, D)
    small_all = small_all.reshape(N_CHIP, 2, -1)[:, 0]
    conv_w_full = small_all[:, :n_cw].reshape(N_CHIP, 2, CONV_WIDTH, F2s).transpose(1, 2, 0, 3).reshape(2, CONV_WIDTH, F2)
    sg_gain_full = small_all[:, n_cw:n_cw + n_sv].reshape(1, SGW)
    sg_bias_full = small_all[:, n_cw + n_sv:n_cw + 2 * n_sv].reshape(1, SGW)

    c_pad = jnp.pad(c_all, ((0, 16 - N_DEV), (0, 0)))
    ada_b_sh = lax.dynamic_slice(ada_b, (0, chip * NA), (2, NA)).reshape(2, 1, NA)
    mod_sh = _ada_fwd(c_pad, ada_w, ada_b_sh)
    mod_all, = _all_gather("gather_mod", [mod_sh[:, :N_DEV]])
    mod_all = mod_all.reshape(N_CHIP, 2, 2, N_DEV, NA)[:, 0]
    mod = lax.dynamic_index_in_dim(mod_all, dev, axis=2, keepdims=False)
    mod = mod.transpose(1, 0, 2).reshape(2, 6, D)
    mods = [[mod[l, k].reshape(1, D) for k in range(6)] for l in range(2)]
    started["a"] = _chips_start("gather_start_a", "rows", [rows8[0]], after=mod)
    for i in range(1, len(shards)):
        rows8[i] = _cast_into_rows("cast_w", halves[i], chip, after=started["a"][3])

    vec = lambda a: a.reshape(1, -1)
    l0 = vec(hg_lb_logits[0])
    l1 = vec(hg_lb_logits[1])
    hg_gain = vec(hg_out_norm[0])
    wpos = sg_w_pos[0]
    bpos = sg_b_pos[0].reshape(G, SG_CHUNK, 1)
    conv_b = [vec(ffn_conv_b[l]) for l in range(2)]

    sh1, sc1, g1, sh2, sc2, g2 = mods[0]
    send_a, recv_a, bufs_a, _ = started["a"]
    bufs_a = _chips_wait("gather_wait_a", "rows", 1, send_a, recv_a, bufs_a, rows8[len(shards) - 1])
    bufs_a = _fill_from_sibling("gather_fill_a", bufs_a)
    started["b"] = _chips_start("gather_start_b", "rows", [rows8[i] for i in groups["b"]], after=bufs_a[0])
    start_token = started["a"][3][0, 0] + started["b"][3][0, 0]
    w_in = as_weights("a", bufs_a)[0]
    h0 = _normmod_fwd("norm_mix0", x, vec(mix_norm[0]) + start_token, sc1, sh1)
    proj = _mm_nn("mm_par_in", h0, w_in)
    o_sb, sb_tot = _sb_fwd(proj, H)
    filling_b = weights_landed("b", o_sb)
    started["c"] = _chips_start("gather_start_c", "rows", [rows8[i] for i in groups["c"]], after=o_sb)
    o_hg, hg_states = _hg_fwd(proj, l0 + filling_b[3][0:1, 0:1] + started["c"][3][0:1, 0:1], l1, hg_gain, H, 3 * H)
    o_cat = jnp.concatenate([o_sb, o_hg], axis=1)
    wb = weights_of("b", filling_b, o_cat)
    w_out, wup, wdn = wb[1], [wb[4], None], [wb[6], None]
    y0 = _mm_nn("mm_par_out", o_cat, w_out)
    x1, h0f = _res_normmod_fwd("res_norm_ffn0", x, y0, g1, vec(ffn_norm[0]), sc2, sh2)
    a0 = _mm_nn("mm_up0", h0f, wup[0])
    u0 = _conv_fwd("conv_fwd0", a0, conv_w_full[0], conv_b[0])
    filling_c = weights_landed("c", u0)
    f0 = _mm_nn("mm_down0", u0, wdn[0])
    sh1b, sc1b, g1b, sh2b, sc2b, g2b = mods[1]
    x2, h1 = _res_normmod_fwd("res_norm_mix1", x1, f0, g2, vec(mix_norm[1]) + filling_c[3][0:1, 0:1], sc1b, sh1b)
    wc = weights_of("c", filling_c, h1)
    wsg_in, wsg_out, wup[1], wdn[1] = wc[2], wc[3], wc[5], wc[7]
    zpre = _mm_nn("mm_sg_in", h1, wsg_in)
    s1 = _sg_fwd(zpre, sg_gain_full, sg_bias_full, wpos, bpos)
    y1 = _mm_nn("mm_sg_out", s1, wsg_out)
    x3, h1f = _res_normmod_fwd("res_norm_ffn1", x2, y1, g1b, vec(ffn_norm[1]), sc2b, sh2b)
    a1 = _mm_nn("mm_up1", h1f, wup[1])
    u1 = _conv_fwd("conv_fwd1", a1, conv_w_full[1], conv_b[1])
    f1 = _mm_nn("mm_down1", u1, wdn[1])
    loss_sum, dx, df1, dg2b, d_final = _final_fwd_bwd(x3, f1, g2b, vec(final_norm), target)
    loss = lax.psum(loss_sum[0, 0], ("x", "y", "c"))

    def reduce_start(tag, idx, grads):
        eights = [g.reshape((N_DEV, -1, g.shape[-1])) for g in grads]
        landing = [lax.empty((N_CHIP,) + e.shape[1:], e.dtype) for e in eights]
        send_sems, recv_sems, bufs, token = _chips_start("grads_sibling_start_" + tag, "halves", eights, landing)
        return (tag, idx, send_sems, recv_sems, bufs), token[0:1, 0:1]

    def reduce_cross(state, after):
        tag, idx, send_sems, recv_sems, bufs = state
        n = len(idx)
        bufs = _chips_wait("grads_sibling_wait_" + tag, "halves", n, send_sems, recv_sems, bufs, after)
        pair = [_add_pairs("add_pair", e, r, ic) for e, r in zip(bufs[:n], bufs[n:])]
        landing = [lax.empty(p.shape, p.dtype) for p in pair]
        send_sems, recv_sems, bufs, token = _chips_start("grads_start_" + tag, "parts", pair, landing)
        return (tag, idx, send_sems, recv_sems, bufs), token[0:1, 0:1]

    def reduce_finish(state, after):
        tag, idx, send_sems, recv_sems, bufs = state
        n = len(idx)
        bufs = _chips_wait("grads_wait_" + tag, "parts", n, send_sems, recv_sems, bufs, after)
        halves = [_sum_into_pair("sum_chips", p, x_, ic, chip) for p, x_ in zip(bufs[:n], bufs[n:])]
        both = _share_halves("grads_share_" + tag, halves)
        return {i: b.reshape(shards[i].shape) for i, b in zip(idx, both)}

    def ffn_bwd(l, dfl, u, a, hf):
        g_dn = _mm_tn("mm_g_down", u, dfl)
        du = _mm_nt("mm_d_u", dfl, wdn[l])
        da, dcw, dcb = _conv_bwd("conv_bwd", a, du, conv_w_full[l], conv_b[l])
        g_up = _mm_tn("mm_g_up", hf, da, chunks=N_CHIP)
        dh = _mm_nt("mm_d_hf", da, wup[l])
        return g_dn, g_up, dcw, dcb, dh

    g_dn1, g_up1, dcw1, dcb1, dh1f = ffn_bwd(1, df1, u1, a1, h1f)
    red1, tok = reduce_start("1", [5, 7], [g_up1, g_dn1])
    dx, dgn_f1, dsc2b, dsh2b, dy1, dg1b = _block_bwd("bwd_ffn1", dx, dh1f, x3, vec(ffn_norm[1]) + tok, sc2b, sh2b, y1, g1b)
    g_sg_out = _mm_tn("mm_g_sg_out", s1, dy1)
    ds1 = _mm_nt("mm_d_s", dy1, wsg_out)
    dzpre, dsg_gain, dsg_bias, dwpos, dbpos = _sg_bwd(zpre, ds1, sg_gain_full, sg_bias_full, wpos, bpos)
    red1, tok_x = reduce_cross(red1, dzpre)
    g_sg_in = _mm_tn("mm_g_sg_in", h1, dzpre, chunks=N_CHIP)
    dh1 = _mm_nt("mm_d_h1", dzpre, wsg_in)
    red2, tok = reduce_start("2", [2, 3], [g_sg_in, g_sg_out])
    dx, dgn_m1, dsc1b, dsh1b, df0, dg2 = _block_bwd("bwd_mix1", dx, dh1, x2, vec(mix_norm[1]) + tok + tok_x, sc1b, sh1b, f0, g2)
    g_dn0, g_up0, dcw0, dcb0, dh0f = ffn_bwd(0, df0, u0, a0, h0f)
    red2, tok_x = reduce_cross(red2, dh0f)
    red3, tok = reduce_start("3", [4, 6], [g_up0, g_dn0])
    dx, dgn_f0, dsc2, dsh2, dy0, dg1 = _block_bwd("bwd_ffn0", dx, dh0f, x1, vec(ffn_norm[0]) + tok + tok_x, sc2, sh2, y0, g1)
    g_out = _mm_tn("mm_g_par_out", o_cat, dy0)
    do = _mm_nt("mm_d_o", dy0, w_out)
    red3, tok_x = reduce_cross(red3, do)
    dhq, dhf, dhi, dhg, dl0, dl1, dhg_gain = _hg_bwd(proj, hg_states, do, l0 + tok_x, l1, hg_gain, H, 3 * H, H)
    dq, dk, dv = _sb_bwd(proj, do, sb_tot, H)
    dproj = jnp.concatenate([dq, dk, dv, dhq, dhf, dhi, dhg], axis=1).astype(BF16)
    g_in = _mm_tn("mm_g_par_in", h0, dproj, chunks=N_CHIP)
    red4, tok = reduce_start("4", [0, 1], [g_in, g_out])
    dh0 = _mm_nt("mm_d_h0", dproj, w_in)
    grad_x, dgn_m0, dsc1, dsh1 = _block_bwd("bwd_mix0", dx, dh0, x, vec(mix_norm[0]) + tok, sc1, sh1)
    red4, tok_x = reduce_cross(red4, grad_x)

    G_, delta, new_m, new_v = {}, {}, {}, {}

    def adam_on(nme):
        shp = W[nme].shape
        r2 = lambda a: a.reshape(-1, shp[-1])
        d_, m_, v_ = _adam("adam_" + nme, r2(W[nme]), r2(G_[nme]), r2(M[nme]), r2(V[nme]))
        delta[nme], new_m[nme], new_v[nme] = d_.reshape(shp), m_.reshape(shp), v_.reshape(shp)

    g_shards = {}
    for state in (red1, red2, red3):
        g_shards.update(reduce_finish(state, red4[4][0]))
    G_["sg_w_in"] = g_shards[2][None]
    G_["sg_w_out"] = g_shards[3][None]
    G_["ffn_up"] = jnp.stack([g_shards[4], g_shards[5]])
    G_["ffn_down"] = jnp.stack([g_shards[6], g_shards[7]])
    for nme in ["sg_w_in", "sg_w_out", "ffn_up", "ffn_down"]:
        adam_on(nme)

    dmod = jnp.concatenate([dsh1, dsc1, dg1, dsh2, dsc2, dg2, dsh1b, dsc1b, dg1b, dsh2b, dsc2b, dg2b], axis=1)
    parts = [dmod, dgn_m0, dgn_m1, dgn_f0, dgn_f1, dl0, dl1, dhg_gain, dsg_gain, dsg_bias, dwpos, dbpos,
             dcw0, dcw1, dcb0, dcb1, d_final]
    sizes = [p.size for p in parts]
    packed = _pack_rows(parts)
    packed_all, = _all_gather("gather_small_grads", [packed], after=new_v["ffn_down"])
    summed = _sum_leading("sum_small_grads", packed_all).reshape(-1)
    offs = [0]
    for s in sizes:
        offs.append(offs[-1] + s)
    red = [summed[offs[i]:offs[i + 1]] for i in range(len(parts))]
    (r_dmod, r_gm0, r_gm1, r_gf0, r_gf1, r_l0, r_l1, r_hgain, r_sgain, r_sbias, r_wpos, r_bpos,
     r_cw0, r_cw1, r_cb0, r_cb1, r_final) = red
    n_mod = sizes[0]
    dmod_all = packed_all.reshape(N_DEV, -1)[:, :n_mod].reshape(N_DEV, 2, 6 * D)

    G_["ada_b"] = r_dmod.reshape(2, 6 * D)
    G_["mix_norm"] = jnp.stack([r_gm0, r_gm1])
    G_["ffn_norm"] = jnp.stack([r_gf0, r_gf1])
    G_["hg_lb_logits"] = jnp.stack([r_l0, r_l1])
    G_["hg_out_norm"] = r_hgain.reshape(hg_out_norm.shape)
    G_["sg_v_gain"] = lax.dynamic_slice(r_sgain, (chip * n_sv,), (n_sv,)).reshape(sg_v_gain.shape)
    G_["sg_v_bias"] = lax.dynamic_slice(r_sbias, (chip * n_sv,), (n_sv,)).reshape(sg_v_bias.shape)
    G_["sg_w_pos"] = r_wpos.reshape(sg_w_pos.shape)
    G_["sg_b_pos"] = r_bpos.reshape(sg_b_pos.shape)
    cw_full = jnp.stack([r_cw0.reshape(CONV_WIDTH, F2), r_cw1.reshape(CONV_WIDTH, F2)])
    G_["ffn_conv_w"] = lax.dynamic_slice(cw_full, (0, 0, chip * F2s), (2, CONV_WIDTH, F2s))
    G_["ffn_conv_b"] = jnp.stack([r_cb0, r_cb1])
    G_["final_norm"] = r_final

    c_t = jnp.pad(c_all, ((0, HEAD - N_DEV), (0, 0))).T
    dmod_sh = lax.dynamic_slice(dmod_all.transpose(1, 0, 2), (0, 0, chip * NA), (2, N_DEV, NA))
    dmod_sh = jnp.pad(dmod_sh, ((0, 0), (0, HEAD - N_DEV), (0, 0)))
    G_["ada_w"], delta["ada_w"], new_m["ada_w"], new_v["ada_w"] = _ada_grad_adam(c_t, dmod_sh, ada_w, m_ada_w, v_ada_w)

    g_shards.update(reduce_finish(red4, G_["ada_w"]))
    G_["par_w_in"] = g_shards[0][None]
    G_["par_w_out"] = g_shards[1][None]
    for nme in ["par_w_in", "par_w_out"]:
        adam_on(nme)
    small = [n_ for n_ in names if n_ not in delta]
    pk = lambda dct: _pack_rows([dct[n_] for n_ in small])
    d_, m_, v_ = _adam("adam_small", pk(W), pk(G_), pk(M), pk(V))
    off = 0
    for n_ in small:
        sz = W[n_].size
        for dst, src in ((delta, d_), (new_m, m_), (new_v, v_)):
            dst[n_] = src.reshape(-1)[off:off + sz].reshape(W[n_].shape)
        off += sz

    return (loss, grad_x[None], *[G_[n_] for n_ in names], *[delta[n_] for n_ in names],
            *[new_m[n_] for n_ in names], *[new_v[n_] for n_ in names])
```

```python
import functools
import math

import jax
import jax.numpy as jnp
from jax import lax
from jax.experimental import pallas as pl
from jax.experimental.pallas import tpu as pltpu

F32 = jnp.float32
BF16 = jnp.bfloat16
MESH = pl.DeviceIdType.MESH
ANY = pl.BlockSpec(memory_space=pl.ANY)

NORM_EPS = 1e-6
ADAM_LR = 0.001
ADAM_B1 = 0.9
ADAM_B2 = 0.999
ADAM_EPS = 1e-08
ADAM_WD = 0.01
ADAM_STEP = 10
CONV_WIDTH = 3
HEAD = 128
HG_CHUNK = 64
SG_CHUNK = 128
N_DEV = 8
N_CHIP = 4
V7X_VMEM_LIMIT = 56 * 1024 * 1024


def _cp(*sem):
    return pltpu.CompilerParams(dimension_semantics=sem if sem else None, vmem_limit_bytes=V7X_VMEM_LIMIT)


def _pick(n, prefs):
    for p in prefs:
        if p <= n and n % p == 0:
            return p
    return n


def _iota(shape, axis):
    return lax.broadcasted_iota(jnp.int32, shape, axis)


def _rows_within(R, row_bytes, budget):
    if R * row_bytes <= budget:
        return R
    for t in (1024, 512, 256, 128, 64, 32, 16):
        if R % t == 0 and t * row_bytes <= budget:
            return t
    return _pick(R, (16, 8))


def _pack_rows(arrays):
    flat = jnp.concatenate([a.reshape(-1) for a in arrays])
    pad = (-flat.size) % (8 * HEAD)
    return jnp.pad(flat, (0, pad)).reshape(-1, HEAD)


def _dg(a, b, ca, cb):
    if a.ndim == 3:
        dims = (((ca + 1,), (cb + 1,)), ((0,), (0,)))
    else:
        dims = (((ca,), (cb,)), ((), ()))
    return lax.dot_general(a.astype(BF16), b.astype(BF16), dims, preferred_element_type=F32)


@jax.custom_vjp
def mm_nn(a, b):
    return _dg(a, b, 1, 0)


mm_nn.defvjp(lambda a, b: (_dg(a, b, 1, 0), (a, b)),
             lambda r, g: (_dg(g, r[1], 1, 1), _dg(r[0], g, 0, 0)))


@jax.custom_vjp
def mm_nt(a, b):
    return _dg(a, b, 1, 1)


mm_nt.defvjp(lambda a, b: (_dg(a, b, 1, 1), (a, b)),
             lambda r, g: (_dg(g, r[1], 1, 0), _dg(g, r[0], 0, 0)))


@jax.custom_vjp
def mm_tn(a, b):
    return _dg(a, b, 0, 0)


mm_tn.defvjp(lambda a, b: (_dg(a, b, 0, 0), (a, b)),
             lambda r, g: (_dg(r[1], g, 1, 1), _dg(r[0], g, 1, 0)))


def _split(x):
    hi = x.astype(BF16)
    lo = (x - hi.astype(F32)).astype(BF16)
    return hi, lo


def _sum_right(x, m01):
    hi, lo = _split(x)
    return _dg(hi, m01, 1, 0) + _dg(lo, m01, 1, 0)


def _sum_left_impl(m01, x, ca):
    if x.ndim == 3:
        m01 = jnp.broadcast_to(m01, (x.shape[0],) + m01.shape)
    hi, lo = _split(x)
    return _dg(m01, hi, ca, 0) + _dg(m01, lo, ca, 0)


@jax.custom_vjp
def _sum_left(m01, x):
    return _sum_left_impl(m01, x, 1)


_sum_left.defvjp(lambda m, x: (_sum_left_impl(m, x, 1), m),
                 lambda m, g: (None, _sum_left_impl(m, g, 0)))


def _sigmoid(x):
    return 1.0 / (1.0 + jnp.exp(-x))


def _softplus(z):
    return jnp.maximum(z, 0.0) + jnp.log(1.0 + jnp.exp(-jnp.abs(z)))


_INV_SQRT2 = 1.0 / math.sqrt(2.0)
_INV_SQRT2PI = 1.0 / math.sqrt(2.0 * math.pi)


@jax.custom_vjp
def _gelu(x):
    return 0.5 * x * (1.0 + lax.erf(x * _INV_SQRT2))


_gelu.defvjp(lambda x: (0.5 * x * (1.0 + lax.erf(x * _INV_SQRT2)), x),
             lambda x, g: (g * (0.5 * (1.0 + lax.erf(x * _INV_SQRT2)) + x * jnp.exp(-0.5 * x * x) * _INV_SQRT2PI),))


def _rms(x, gain):
    r = lax.rsqrt(jnp.mean(x * x, axis=-1, keepdims=True) + NORM_EPS)
    return x * r * gain


def _normmod(x, gain, sc, sh):
    return _rms(x, gain) * (1.0 + sc) + sh


def _mm_call(name, a, b, out_shape, out_dtype, dims, grid, a_spec, b_spec, o_spec, acc_shape):
    nk = grid[2]

    def body(a_ref, b_ref, o_ref, *scratch):
        part = lax.dot_general(a_ref[...].astype(BF16), b_ref[...].astype(BF16), dims, preferred_element_type=F32)
        if nk == 1:
            o_ref[...] = part.astype(o_ref.dtype)
            return
        acc_ref, = scratch
        k = pl.program_id(2)

        @pl.when(k == 0)
        def _():
            acc_ref[...] = part

        @pl.when(k > 0)
        def _():
            acc_ref[...] += part

        @pl.when(k == nk - 1)
        def _():
            o_ref[...] = acc_ref[...].astype(o_ref.dtype)

    return pl.pallas_call(
        body, name=name, grid=grid, in_specs=[a_spec, b_spec], out_specs=o_spec,
        out_shape=jax.ShapeDtypeStruct(out_shape, out_dtype),
        scratch_shapes=[] if nk == 1 else [pltpu.VMEM(acc_shape, F32)],
        compiler_params=_cp("parallel", "parallel", "arbitrary"),
    )(a, b)


def _mm_nn(name, a, b, out_dtype=F32):
    M, K = a.shape
    chunked = b.ndim == 3
    Nc = b.shape[-1]
    N = Nc * (b.shape[0] if chunked else 1)
    tm = _pick(M, (1024, 512, 256, 128, 64, 32, 16, 8))
    tn = _pick(Nc, (1408, 1024, 896, 512, 256, 128))
    tk = _pick(K, (2048, 1408, 1024, 512, 256, 128))
    npc = Nc // tn
    if chunked:
        b_spec = pl.BlockSpec((None, tk, tn), lambda i, j, k: (j // npc, k, j % npc))
    else:
        b_spec = pl.BlockSpec((tk, tn), lambda i, j, k: (k, j))
    return _mm_call(name, a, b, (M, N), out_dtype, (((1,), (0,)), ((), ())), (M // tm, N // tn, K // tk),
                    pl.BlockSpec((tm, tk), lambda i, j, k: (i, k)), b_spec,
                    pl.BlockSpec((tm, tn), lambda i, j, k: (i, j)), (tm, tn))


def _mm_nt(name, a, b, out_dtype=F32):
    planar = a.ndim == 3
    M, Np = a.shape[-2:]
    N = Np * (a.shape[0] if planar else 1)
    chunked = b.ndim == 3
    Nc = b.shape[-1]
    K = b.shape[-2]
    tm = _pick(M, (1024, 512, 256, 128, 64, 32, 16, 8))
    tn = _pick(K, (1408, 1024, 512, 256, 128))
    tk = _pick(Nc, (2048, 1792, 1408, 1024, 896, 512, 256, 128))
    assert Np % tk == 0
    npc = Nc // tk
    npp = Np // tk
    if chunked:
        b_spec = pl.BlockSpec((None, tn, tk), lambda i, j, k: (k // npc, j, k % npc))
    else:
        b_spec = pl.BlockSpec((tn, tk), lambda i, j, k: (j, k))
    if planar:
        a_spec = pl.BlockSpec((None, tm, tk), lambda i, j, k: (k // npp, i, k % npp))
    else:
        a_spec = pl.BlockSpec((tm, tk), lambda i, j, k: (i, k))
    return _mm_call(name, a, b, (M, K), out_dtype, (((1,), (1,)), ((), ())), (M // tm, K // tn, N // tk),
                    a_spec, b_spec, pl.BlockSpec((tm, tn), lambda i, j, k: (i, j)), (tm, tn))


def _mm_tn(name, a, b, chunks=1, out_dtype=BF16):
    T, K = a.shape
    planar = b.ndim == 3
    Np = b.shape[-1]
    N = Np * (b.shape[0] if planar else 1)
    Nc = N // chunks
    tm = _pick(K, (1408, 1024, 512, 256, 128))
    tn = _pick(Nc, (1408, 1024, 896, 512, 256, 128))
    tk = _pick(T, (1024, 512, 256, 128))
    assert Np % tn == 0
    npc = Nc // tn
    npp = Np // tn
    if planar:
        b_spec = pl.BlockSpec((None, tk, tn), lambda i, j, k: (j // npp, k, j % npp))
    else:
        b_spec = pl.BlockSpec((tk, tn), lambda i, j, k: (k, j))
    if chunks > 1:
        shape = (chunks, K, Nc)
        o_spec = pl.BlockSpec((None, tm, tn), lambda i, j, k: (j // npc, i, j % npc))
    else:
        shape = (K, N)
        o_spec = pl.BlockSpec((tm, tn), lambda i, j, k: (i, j))
    return _mm_call(name, a.T, b, shape, out_dtype, (((1,), (0,)), ((), ())), (K // tm, N // tn, T // tk),
                    pl.BlockSpec((tm, tk), lambda i, j, k: (i, k)), b_spec, o_spec, (tm, tn))


def _row_tile(T):
    return _pick(T, (256, 128, 64, 32, 16, 8))


def _vec_spec(D):
    return pl.BlockSpec((1, D), lambda i: (0, 0))


def _normmod_fwd(name, x, gain, sc, sh):
    T, D = x.shape
    bt = _row_tile(T)

    def body(x_ref, g_ref, sc_ref, sh_ref, h_ref):
        h_ref[...] = _normmod(x_ref[...], g_ref[...], sc_ref[...], sh_ref[...]).astype(h_ref.dtype)

    rows = pl.BlockSpec((bt, D), lambda i: (i, 0))
    return pl.pallas_call(body, name=name, grid=(T // bt,), in_specs=[rows] + [_vec_spec(D)] * 3, out_specs=rows,
                          out_shape=jax.ShapeDtypeStruct((T, D), BF16), compiler_params=_cp("parallel"))(x, gain, sc, sh)


def _res_normmod_fwd(name, x, y, g, gain, sc, sh):
    T, D = x.shape
    bt = _row_tile(T)

    def body(x_ref, y_ref, gate_ref, g_ref, sc_ref, sh_ref, x1_ref, h_ref):
        x1 = x_ref[...] + gate_ref[...] * y_ref[...]
        x1_ref[...] = x1
        h_ref[...] = _normmod(x1, g_ref[...], sc_ref[...], sh_ref[...]).astype(h_ref.dtype)

    rows = pl.BlockSpec((bt, D), lambda i: (i, 0))
    return pl.pallas_call(body, name=name, grid=(T // bt,), in_specs=[rows, rows] + [_vec_spec(D)] * 4,
                          out_specs=[rows, rows],
                          out_shape=[jax.ShapeDtypeStruct((T, D), F32), jax.ShapeDtypeStruct((T, D), BF16)],
                          compiler_params=_cp("parallel"))(x, y, g, gain, sc, sh)


def _final_fwd_bwd(x, y, g, gain, target):
    T, D = x.shape
    bt = _row_tile(T)

    def body(x_ref, y_ref, gate_ref, g_ref, t_ref, loss_ref, dx_ref, dy_ref, dgate_ref, dgain_ref):
        i = pl.program_id(0)
        yv = y_ref[...]
        gate = gate_ref[...]
        x4 = x_ref[...] + gate * yv
        out, vjp = jax.vjp(_rms, x4, g_ref[...])
        err = out - t_ref[...]
        dx4, dgain = vjp(err * (1.0 / D))
        part = 0.5 * jnp.sum(jnp.mean(err * err, axis=-1, keepdims=True), axis=0, keepdims=True)

        @pl.when(i == 0)
        def _():
            loss_ref[...] = jnp.zeros_like(loss_ref)
            dgate_ref[...] = jnp.zeros_like(dgate_ref)
            dgain_ref[...] = jnp.zeros_like(dgain_ref)

        loss_ref[...] += jnp.broadcast_to(part, loss_ref.shape)
        dx_ref[...] = dx4
        dy_ref[...] = (gate * dx4).astype(dy_ref.dtype)
        dgate_ref[...] += jnp.sum(dx4 * yv, axis=0, keepdims=True)
        dgain_ref[...] += dgain

    rows = pl.BlockSpec((bt, D), lambda i: (i, 0))
    vec = _vec_spec(D)
    return pl.pallas_call(
        body, name="final_loss", grid=(T // bt,), in_specs=[rows, rows, vec, vec, rows],
        out_specs=[pl.BlockSpec((1, HEAD), lambda i: (0, 0)), rows, rows, vec, vec],
        out_shape=[jax.ShapeDtypeStruct((1, HEAD), F32), jax.ShapeDtypeStruct((T, D), F32),
                   jax.ShapeDtypeStruct((T, D), BF16), jax.ShapeDtypeStruct((1, D), F32),
                   jax.ShapeDtypeStruct((1, D), F32)],
        compiler_params=_cp("arbitrary"))(x, y, g, gain, target)


def _block_bwd(name, dx_out, dh, x_in, gain, sc, sh, y_prev=None, g_prev=None):
    T, D = x_in.shape
    bt = _row_tile(T)
    has_prev = y_prev is not None

    def body(*refs):
        if has_prev:
            dxo_ref, dh_ref, x_ref, g_ref, sc_ref, sh_ref, y_ref, gp_ref, dx_ref, dgain_ref, dsc_ref, dsh_ref, dy_ref, dgp_ref = refs
        else:
            dxo_ref, dh_ref, x_ref, g_ref, sc_ref, sh_ref, dx_ref, dgain_ref, dsc_ref, dsh_ref = refs
        i = pl.program_id(0)
        _, vjp = jax.vjp(_normmod, x_ref[...], g_ref[...], sc_ref[...], sh_ref[...])
        dxn, dgain, dsc, dsh = vjp(dh_ref[...])
        dx = dxo_ref[...] + dxn
        dx_ref[...] = dx

        @pl.when(i == 0)
        def _():
            dgain_ref[...] = jnp.zeros_like(dgain_ref)
            dsc_ref[...] = jnp.zeros_like(dsc_ref)
            dsh_ref[...] = jnp.zeros_like(dsh_ref)
            if has_prev:
                dgp_ref[...] = jnp.zeros_like(dgp_ref)

        dgain_ref[...] += dgain
        dsc_ref[...] += dsc
        dsh_ref[...] += dsh
        if has_prev:
            dy_ref[...] = (gp_ref[...] * dx).astype(dy_ref.dtype)
            dgp_ref[...] += jnp.sum(dx * y_ref[...], axis=0, keepdims=True)

    rows = pl.BlockSpec((bt, D), lambda i: (i, 0))
    vec = _vec_spec(D)
    ins = [dx_out, dh, x_in, gain, sc, sh]
    in_specs = [rows, rows, rows, vec, vec, vec]
    out_specs = [rows, vec, vec, vec]
    out_shape = [jax.ShapeDtypeStruct((T, D), F32)] + [jax.ShapeDtypeStruct((1, D), F32)] * 3
    if has_prev:
        ins += [y_prev, g_prev]
        in_specs += [rows, vec]
        out_specs += [rows, vec]
        out_shape += [jax.ShapeDtypeStruct((T, D), BF16), jax.ShapeDtypeStruct((1, D), F32)]
    return pl.pallas_call(body, name=name, grid=(T // bt,), in_specs=in_specs, out_specs=out_specs,
                          out_shape=out_shape, compiler_params=_cp("arbitrary"))(*ins)


def _sb_tiles(T):
    tq = _pick(T, (512, 256, 128))
    return tq, tq // HEAD


def _sb_fwd(proj, H):
    T = proj.shape[0]
    tq, nsub = _sb_tiles(T)
    scale = HEAD ** -0.5

    def body(q_ref, k_ref, v_ref, o_ref, l_ref, acc_ref):
        i = pl.program_id(1)
        q = q_ref[...].astype(BF16)
        later = (_iota((HEAD, HEAD), 0) > _iota((HEAD, HEAD), 1)).astype(BF16)
        row = _iota((tq, HEAD), 0)
        col = _iota((tq, HEAD), 1)

        def key_step(j, c, diagonal):
            off = pl.multiple_of(j * tq, tq)
            k = k_ref[pl.ds(off, tq), :].astype(BF16)
            v = v_ref[pl.ds(off, tq), :].astype(BF16)
            z = _dg(q, k, 1, 1) * scale
            ws = [None] * nsub
            for s in reversed(range(nsub)):
                zs = z[:, s * HEAD:(s + 1) * HEAD]
                sp = _softplus(zs)
                if diagonal:
                    strict = (s * HEAD + col) < row
                    lk = jnp.where(strict, -sp, 0.0)
                else:
                    lk = -sp
                w = jnp.exp(zs - sp + _sum_right(lk, later) + c)
                if diagonal:
                    w = jnp.where(strict, w, 0.0)
                ws[s] = w.astype(BF16)
                c = c + jnp.sum(lk, axis=1, keepdims=True)
            acc_ref[...] += _dg(jnp.concatenate(ws, axis=1), v, 1, 0)
            return c

        acc_ref[...] = jnp.zeros_like(acc_ref)
        c = key_step(i, jnp.zeros((tq, 1), F32), True)
        c = lax.fori_loop(0, i, lambda n, c: key_step(i - 1 - n, c, False), c)
        o_ref[...] = acc_ref[...].astype(o_ref.dtype)
        l_ref[...] = jnp.broadcast_to(c, (tq, HEAD))

    blk = pl.BlockSpec((tq, HEAD), lambda h, i: (i, h))
    return pl.pallas_call(
        body, name="sb_fwd", grid=(H, T // tq),
        in_specs=[blk, pl.BlockSpec((T, HEAD), lambda h, i: (0, H + h)), pl.BlockSpec((T, HEAD), lambda h, i: (0, 2 * H + h))],
        out_specs=[blk, blk],
        out_shape=[jax.ShapeDtypeStruct((T, H * HEAD), BF16), jax.ShapeDtypeStruct((T, H * HEAD), F32)],
        scratch_shapes=[pltpu.VMEM((tq, HEAD), F32)],
        compiler_params=_cp("parallel", "arbitrary"))(proj, proj, proj)


def _sb_bwd(proj, do, L, H):
    T = proj.shape[0]
    tq, nsub = _sb_tiles(T)
    scale = HEAD ** -0.5

    def body(q_ref, k_ref, v_ref, do_ref, l_ref, dq_ref, dk_ref, dv_ref):
        i = pl.program_id(1)

        @pl.when(i == 0)
        def _():
            dk_ref[...] = jnp.zeros_like(dk_ref)
            dv_ref[...] = jnp.zeros_like(dv_ref)

        dq_ref[...] = jnp.zeros_like(dq_ref)
        q = q_ref[...].astype(BF16)
        do_ = do_ref[...].astype(BF16)
        total = l_ref[...]
        upto = (_iota((HEAD, HEAD), 0) <= _iota((HEAD, HEAD), 1)).astype(BF16)
        before = (_iota((HEAD, HEAD), 0) < _iota((HEAD, HEAD), 1)).astype(BF16)
        row = _iota((tq, HEAD), 0)
        col = _iota((tq, HEAD), 1)

        def key_step(j, carry, diagonal):
            cp, ce = carry
            off = pl.multiple_of(j * tq, tq)
            k = k_ref[pl.ds(off, tq), :].astype(BF16)
            v = v_ref[pl.ds(off, tq), :].astype(BF16)
            z = _dg(q, k, 1, 1) * scale
            dw = _dg(do_, v, 1, 1)
            ws, dzs = [], []
            for s in range(nsub):
                zs = z[:, s * HEAD:(s + 1) * HEAD]
                sp = _softplus(zs)
                if diagonal:
                    strict = (s * HEAD + col) < row
                    lk = jnp.where(strict, -sp, 0.0)
                else:
                    lk = -sp
                tail = total - (_sum_right(lk, upto) + cp)
                w = jnp.exp(zs - sp + tail)
                if diagonal:
                    w = jnp.where(strict, w, 0.0)
                e = w * dw[:, s * HEAD:(s + 1) * HEAD]
                e_before = _sum_right(e, before) + ce
                sig = jnp.exp(zs - sp)
                dz = (e * (1.0 - sig) - e_before * sig) * scale
                if diagonal:
                    dz = jnp.where(strict, dz, 0.0)
                ws.append(w.astype(BF16))
                dzs.append(dz.astype(BF16))
                cp = cp + jnp.sum(lk, axis=1, keepdims=True)
                ce = ce + jnp.sum(e, axis=1, keepdims=True)
            w_all = jnp.concatenate(ws, axis=1)
            dz_all = jnp.concatenate(dzs, axis=1)
            dv_ref[pl.ds(off, tq), :] += _dg(w_all, do_, 0, 0)
            dk_ref[pl.ds(off, tq), :] += _dg(dz_all, q, 0, 0)
            dq_ref[...] += _dg(dz_all, k, 1, 0)
            return cp, ce

        zero = jnp.zeros((tq, 1), F32)
        carry = lax.fori_loop(0, i, lambda j, cr: key_step(j, cr, False), (zero, zero))
        key_step(i, carry, True)

    blk = pl.BlockSpec((tq, HEAD), lambda h, i: (i, h))
    full = pl.BlockSpec((T, HEAD), lambda h, i: (0, h))
    shp = jax.ShapeDtypeStruct((T, H * HEAD), F32)
    return pl.pallas_call(
        body, name="sb_bwd", grid=(H, T // tq),
        in_specs=[blk, pl.BlockSpec((T, HEAD), lambda h, i: (0, H + h)), pl.BlockSpec((T, HEAD), lambda h, i: (0, 2 * H + h)),
                  blk, blk],
        out_specs=[blk, full, full], out_shape=[shp, shp, shp],
        compiler_params=_cp("parallel", "arbitrary"))(proj, proj, proj, do, L)


def _hg_tile(q, fl, iv, g, st, l0, l1, gain):
    R = 2 * HG_CHUNK
    row = _iota((R, R), 0)
    col = _iota((R, R), 1)
    first = row < HG_CHUNK
    same = first == (col < HG_CHUNK)
    tri = (row >= col) & same
    lb = _sigmoid(l0 - l1)
    f = lb + (1.0 - lb) * _sigmoid(fl)
    logf = jnp.log(f)
    k = 1.0 - f
    qf = q * _sigmoid(q)
    G = _sum_left(tri.astype(BF16), logf)
    gl_a = jnp.sum(jnp.where(first, logf, 0.0), axis=-2, keepdims=True)
    gl_b = jnp.sum(jnp.where(first, 0.0, logf), axis=-2, keepdims=True)
    q_dec = qf * jnp.exp(G)
    k_inv = k * jnp.exp(-G)
    k_end = k * jnp.exp(jnp.where(first, gl_a, gl_b) - G)
    scores = jnp.where(tri, mm_nt(q_dec, k_inv), 0.0)
    o = mm_nn(scores, iv)
    o_a = mm_nt(q_dec, st)
    st_mid = st * jnp.exp(gl_a) + mm_tn(jnp.where(first, iv, 0.0), k_end)
    o_b = mm_nt(q_dec, st_mid)
    st_new = st_mid * jnp.exp(gl_b) + mm_tn(jnp.where(first, 0.0, iv), k_end)
    o = o + jnp.where(first, o_a, o_b)
    on = o * lax.rsqrt(jnp.mean(o * o, axis=-1, keepdims=True) + NORM_EPS) * gain
    return on * (g * _sigmoid(g)), st_new


def _hg_heads(H):
    return _pick(H, (8, 4, 2, 1))


def _hg_specs(H, c0, rev, nt):
    hb = _hg_heads(H)
    w = hb * HEAD

    def at(base):
        if rev:
            return pl.BlockSpec((HEAD, w), lambda h, i: (nt - 1 - i, base // hb + h))
        return pl.BlockSpec((HEAD, w), lambda h, i: (i, base // hb + h))
    return [at(c0), at(c0 + H), at(c0 + 2 * H), at(c0 + 3 * H)]


def _hg_fwd(proj, l0, l1, gain, H, c0):
    T = proj.shape[0]
    nt = T // HEAD
    hb = _hg_heads(H)
    w = hb * HEAD

    def body(q_ref, f_ref, i_ref, g_ref, l0_ref, l1_ref, gain_ref, o_ref, st_out_ref, st_ref):
        @pl.when(pl.program_id(1) == 0)
        def _():
            st_ref[...] = jnp.zeros_like(st_ref)

        sl = [slice(j * HEAD, (j + 1) * HEAD) for j in range(hb)]
        heads = lambda ref: jnp.stack([ref[:, s] for s in sl])
        st = st_ref[...]
        st_out_ref[...] = st
        out, st_new = _hg_tile(heads(q_ref), heads(f_ref), heads(i_ref), heads(g_ref), st, heads(l0_ref), heads(l1_ref),
                               heads(gain_ref))
        for j, s in enumerate(sl):
            o_ref[:, s] = out[j].astype(o_ref.dtype)
        st_ref[...] = st_new

    vec = pl.BlockSpec((1, w), lambda h, i: (0, h))
    return pl.pallas_call(
        body, name="hg_fwd", grid=(H // hb, nt), in_specs=_hg_specs(H, c0, False, nt) + [vec, vec, vec],
        out_specs=[pl.BlockSpec((HEAD, w), lambda h, i: (i, h)),
                   pl.BlockSpec((hb, None, HEAD, HEAD), lambda h, i: (h, i, 0, 0))],
        out_shape=[jax.ShapeDtypeStruct((T, H * HEAD), BF16), jax.ShapeDtypeStruct((H, nt, HEAD, HEAD), F32)],
        scratch_shapes=[pltpu.VMEM((hb, HEAD, HEAD), F32)],
        compiler_params=_cp("parallel", "arbitrary"))(proj, proj, proj, proj, l0, l1, gain)


def _hg_bwd(proj, states, do, l0, l1, gain, H, c0, do_c0):
    T = proj.shape[0]
    nt = T // HEAD
    hb = _hg_heads(H)
    w = hb * HEAD

    def body(q_ref, f_ref, i_ref, g_ref, st_in_ref, do_ref, l0_ref, l1_ref, gain_ref,
             dq_ref, df_ref, di_ref, dg_ref, dl0_ref, dl1_ref, dgain_ref, dst_ref):
        @pl.when(pl.program_id(1) == 0)
        def _():
            dst_ref[...] = jnp.zeros_like(dst_ref)
            dl0_ref[...] = jnp.zeros_like(dl0_ref)
            dl1_ref[...] = jnp.zeros_like(dl1_ref)
            dgain_ref[...] = jnp.zeros_like(dgain_ref)

        sl = [slice(j * HEAD, (j + 1) * HEAD) for j in range(hb)]
        heads = lambda ref: jnp.stack([ref[:, s] for s in sl])
        _, vjp = jax.vjp(_hg_tile, heads(q_ref), heads(f_ref), heads(i_ref), heads(g_ref), st_in_ref[...],
                         heads(l0_ref), heads(l1_ref), heads(gain_ref))
        dq, df, di, dg, dst, dl0, dl1, dgain = vjp((heads(do_ref), dst_ref[...]))
        dst_ref[...] = dst
        for j, s in enumerate(sl):
            dq_ref[:, s] = dq[j]
            df_ref[:, s] = df[j]
            di_ref[:, s] = di[j]
            dg_ref[:, s] = dg[j]
            dl0_ref[:, s] += dl0[j]
            dl1_ref[:, s] += dl1[j]
            dgain_ref[:, s] += dgain[j]

    vec = pl.BlockSpec((1, w), lambda h, i: (0, h))
    rblk = pl.BlockSpec((HEAD, w), lambda h, i: (nt - 1 - i, h))
    shp = jax.ShapeDtypeStruct((T, H * HEAD), F32)
    vshp = jax.ShapeDtypeStruct((1, H * HEAD), F32)
    return pl.pallas_call(
        body, name="hg_bwd", grid=(H // hb, nt),
        in_specs=_hg_specs(H, c0, True, nt) + [
            pl.BlockSpec((hb, None, HEAD, HEAD), lambda h, i: (h, nt - 1 - i, 0, 0)),
            pl.BlockSpec((HEAD, w), lambda h, i: (nt - 1 - i, do_c0 // hb + h)), vec, vec, vec],
        out_specs=[rblk, rblk, rblk, rblk, vec, vec, vec],
        out_shape=[shp, shp, shp, shp, vshp, vshp, vshp],
        scratch_shapes=[pltpu.VMEM((hb, HEAD, HEAD), F32)],
        compiler_params=_cp("parallel", "arbitrary"))(proj, proj, proj, proj, states, do, l0, l1, gain)


def _sg_chunk(u_parts, v_parts, gains, biases, wpos, bpos):
    W = sum(p.shape[1] for p in v_parts)
    C = v_parts[0].shape[0]
    v = [_gelu(p) for p in v_parts]
    mu = sum(jnp.sum(p, axis=-1, keepdims=True) for p in v) * (1.0 / W)
    xc = [p - mu for p in v]
    r = lax.rsqrt(sum(jnp.sum(p * p, axis=-1, keepdims=True) for p in xc) * (1.0 / W) + NORM_EPS)
    causal = _iota((C, C), 0) >= _iota((C, C), 1)
    vn = jnp.stack([p * r * gn + bs for p, gn, bs in zip(xc, gains, biases)])
    mixed = mm_nn(jnp.stack([jnp.where(causal, w, 0.0) for w in wpos]), vn) + jnp.stack(bpos)
    return [_gelu(up) * mixed[n] for n, up in enumerate(u_parts)]


def _sg_fwd(zpre, vgain, vbias, wpos, bpos):
    T, W2 = zpre.shape
    W = W2 // 2
    G = wpos.shape[0]
    cg = W // G
    C = SG_CHUNK

    def body(z_ref, gn_ref, bs_ref, w_ref, b_ref, s_ref):
        sl = [slice(g * cg, (g + 1) * cg) for g in range(G)]
        out = _sg_chunk([z_ref[:, s] for s in sl], [z_ref[:, W + s.start:W + s.stop] for s in sl],
                        [gn_ref[:, s] for s in sl], [bs_ref[:, s] for s in sl],
                        [w_ref[g] for g in range(G)], [b_ref[g] for g in range(G)])
        for s, o in zip(sl, out):
            s_ref[:, s] = o.astype(s_ref.dtype)

    return pl.pallas_call(
        body, name="sg_fwd", grid=(T // C,),
        in_specs=[pl.BlockSpec((C, W2), lambda i: (i, 0)), _vec_spec(W), _vec_spec(W),
                  pl.BlockSpec((G, C, C), lambda i: (0, 0, 0)), pl.BlockSpec((G, C, 1), lambda i: (0, 0, 0))],
        out_specs=pl.BlockSpec((C, W), lambda i: (i, 0)),
        out_shape=jax.ShapeDtypeStruct((T, W), BF16), compiler_params=_cp("parallel"))(zpre, vgain, vbias, wpos, bpos)


def _sg_bwd(zpre, ds, vgain, vbias, wpos, bpos):
    T, W2 = zpre.shape
    W = W2 // 2
    G = wpos.shape[0]
    cg = W // G
    C = SG_CHUNK

    def body(z_ref, ds_ref, gn_ref, bs_ref, w_ref, b_ref, dz_ref, dgn_ref, dbs_ref, dw_ref, db_ref):
        @pl.when(pl.program_id(0) == 0)
        def _():
            dgn_ref[...] = jnp.zeros_like(dgn_ref)
            dbs_ref[...] = jnp.zeros_like(dbs_ref)
            dw_ref[...] = jnp.zeros_like(dw_ref)
            db_ref[...] = jnp.zeros_like(db_ref)

        sl = [slice(g * cg, (g + 1) * cg) for g in range(G)]
        _, vjp = jax.vjp(_sg_chunk, [z_ref[:, s] for s in sl], [z_ref[:, W + s.start:W + s.stop] for s in sl],
                         [gn_ref[:, s] for s in sl], [bs_ref[:, s] for s in sl],
                         [w_ref[g] for g in range(G)], [b_ref[g] for g in range(G)])
        du, dv, dgn, dbs, dw, db = vjp([ds_ref[:, s] for s in sl])
        for g, s in enumerate(sl):
            dz_ref[:, s] = du[g].astype(dz_ref.dtype)
            dz_ref[:, W + s.start:W + s.stop] = dv[g].astype(dz_ref.dtype)
            dgn_ref[:, s] += dgn[g]
            dbs_ref[:, s] += dbs[g]
            dw_ref[g] += dw[g]
            db_ref[g] += db[g]

    wspec = pl.BlockSpec((G, C, C), lambda i: (0, 0, 0))
    bspec = pl.BlockSpec((G, C, 1), lambda i: (0, 0, 0))
    return pl.pallas_call(
        body, name="sg_bwd", grid=(T // C,),
        in_specs=[pl.BlockSpec((C, W2), lambda i: (i, 0)), pl.BlockSpec((C, W), lambda i: (i, 0)),
                  _vec_spec(W), _vec_spec(W), wspec, bspec],
        out_specs=[pl.BlockSpec((C, W2), lambda i: (i, 0)), _vec_spec(W), _vec_spec(W), wspec, bspec],
        out_shape=[jax.ShapeDtypeStruct((T, W2), BF16), jax.ShapeDtypeStruct((1, W), F32),
                   jax.ShapeDtypeStruct((1, W), F32), jax.ShapeDtypeStruct((G, C, C), F32),
                   jax.ShapeDtypeStruct((G, C, 1), F32)],
        compiler_params=_cp("arbitrary"))(zpre, ds, vgain, vbias, wpos, bpos)


def _conv_tiles(T, F):
    return _pick(T, (512, 256, 128, 64, 32, 16, 8)), _pick(F, (512, 256, 128))


def _shift_down(cur, prev8, n, first_tile):
    bt = cur.shape[0]
    r = pltpu.roll(cur, n, 0)
    p = pltpu.roll(prev8, n, 0)
    p = jnp.where(first_tile, 0.0, p)
    head = jnp.where(_iota(p.shape, 0) < n, p, r[:8])
    return jnp.concatenate([head, r[8:]], axis=0) if bt > 8 else head


def _shift_up(cur, next8, n, last_tile):
    bt = cur.shape[0]
    r = pltpu.roll(cur, bt - n, 0)
    p = pltpu.roll(next8, 8 - n, 0)
    p = jnp.where(last_tile, 0.0, p)
    tail = jnp.where(_iota(p.shape, 0) >= 8 - n, p, r[bt - 8:])
    return jnp.concatenate([r[:bt - 8], tail], axis=0) if bt > 8 else tail


def _conv_apply(cur, prev8, w_ref, b, first_tile):
    return (b + w_ref[0:1, :] * _shift_down(cur, prev8, 2, first_tile)
            + w_ref[1:2, :] * _shift_down(cur, prev8, 1, first_tile) + w_ref[2:3, :] * cur)


def _conv_fwd(name, a, w, b):
    T, F2 = a.shape
    F = F2 // 2
    bt, cw = _conv_tiles(T, F)
    nf = F // cw
    r8 = bt // 8

    def body(g_ref, gp_ref, v_ref, vp_ref, wg_ref, wv_ref, bg_ref, bv_ref, u_ref):
        first = pl.program_id(0) == 0
        gate = _conv_apply(g_ref[...], gp_ref[...], wg_ref, bg_ref[...], first)
        val = _conv_apply(v_ref[...], vp_ref[...], wv_ref, bv_ref[...], first)
        u_ref[...] = (gate * _sigmoid(gate) * val).astype(u_ref.dtype)

    def cur(off):
        return pl.BlockSpec((bt, cw), lambda i, j: (i, j + off))

    def prev(off):
        return pl.BlockSpec((8, cw), lambda i, j: (jnp.maximum(i * r8 - 1, 0), j + off))

    def vec(rows, off):
        return pl.BlockSpec((rows, cw), lambda i, j: (0, j + off))

    return pl.pallas_call(
        body, name=name, grid=(T // bt, nf),
        in_specs=[cur(0), prev(0), cur(nf), prev(nf), vec(3, 0), vec(3, nf), vec(1, 0), vec(1, nf)],
        out_specs=pl.BlockSpec((bt, cw), lambda i, j: (i, j)),
        out_shape=jax.ShapeDtypeStruct((T, F), BF16),
        compiler_params=_cp("parallel", "parallel"))(a, a, a, a, w, w, b, b)


def _conv_bwd(name, a, du, w, b):
    T, F2 = a.shape
    F = F2 // 2
    bt, cw = _conv_tiles(T, F)
    nf = F // cw
    r8 = bt // 8
    last_blk = T // 8 - 1

    def body(g_ref, gp_ref, gn_ref, v_ref, vp_ref, vn_ref, du_ref, dun_ref, wg_ref, wv_ref, bg_ref, bv_ref,
             da_ref, dwg_ref, dwv_ref, dbg_ref, dbv_ref):
        i = pl.program_id(1)
        first = i == 0
        last = i == pl.num_programs(1) - 1

        def taps(cur, prev8, at_start):
            return _shift_down(cur, prev8, 2, at_start), _shift_down(cur, prev8, 1, at_start), cur

        def conv(t, w_ref, b_ref):
            return b_ref[...] + w_ref[0:1, :] * t[0] + w_ref[1:2, :] * t[1] + w_ref[2:3, :] * t[2]

        def act_bwd(gate, val, du_):
            sg = _sigmoid(gate)
            return du_ * val * (sg * (1.0 + gate * (1.0 - sg))), du_ * gate * sg

        g_cur, v_cur = g_ref[...], v_ref[...]
        tg = taps(g_cur, gp_ref[...], first)
        tv = taps(v_cur, vp_ref[...], first)
        dg, dv = act_bwd(conv(tg, wg_ref, bg_ref), conv(tv, wv_ref, bv_ref), du_ref[...])
        tgn = taps(gn_ref[...], g_cur[bt - 8:, :], False)
        tvn = taps(vn_ref[...], v_cur[bt - 8:, :], False)
        dgn, dvn = act_bwd(conv(tgn, wg_ref, bg_ref), conv(tvn, wv_ref, bv_ref), dun_ref[...])

        def conv_t(d, dn, w_ref):
            return w_ref[2:3, :] * d + w_ref[1:2, :] * _shift_up(d, dn, 1, last) + w_ref[0:1, :] * _shift_up(d, dn, 2, last)

        da_ref[0] = conv_t(dg, dgn, wg_ref).astype(da_ref.dtype)
        da_ref[1] = conv_t(dv, dvn, wv_ref).astype(da_ref.dtype)

        @pl.when(first)
        def _():
            dwg_ref[...] = jnp.zeros_like(dwg_ref)
            dwv_ref[...] = jnp.zeros_like(dwv_ref)
            dbg_ref[...] = jnp.zeros_like(dbg_ref)
            dbv_ref[...] = jnp.zeros_like(dbv_ref)

        for t in range(CONV_WIDTH):
            dwg_ref[t:t + 1, :] += jnp.sum(dg * tg[t], axis=0, keepdims=True)
            dwv_ref[t:t + 1, :] += jnp.sum(dv * tv[t], axis=0, keepdims=True)
        dbg_ref[...] += jnp.sum(dg, axis=0, keepdims=True)
        dbv_ref[...] += jnp.sum(dv, axis=0, keepdims=True)

    def cur(off):
        return pl.BlockSpec((bt, cw), lambda j, i: (i, j + off))

    def prev(off):
        return pl.BlockSpec((8, cw), lambda j, i: (jnp.maximum(i * r8 - 1, 0), j + off))

    def nxt(off):
        return pl.BlockSpec((8, cw), lambda j, i: (jnp.minimum((i + 1) * r8, last_blk), j + off))

    def vec(rows, off):
        return pl.BlockSpec((rows, cw), lambda j, i: (0, j + off))

    da, dwg, dwv, dbg, dbv = pl.pallas_call(
        body, name=name, grid=(nf, T // bt),
        in_specs=[cur(0), prev(0), nxt(0), cur(nf), prev(nf), nxt(nf), cur(0), nxt(0),
                  vec(3, 0), vec(3, nf), vec(1, 0), vec(1, nf)],
        out_specs=[pl.BlockSpec((2, bt, cw), lambda j, i: (0, i, j)), vec(3, 0), vec(3, 0), vec(1, 0), vec(1, 0)],
        out_shape=[jax.ShapeDtypeStruct((2, T, F), BF16), jax.ShapeDtypeStruct((3, F), F32), jax.ShapeDtypeStruct((3, F), F32),
                   jax.ShapeDtypeStruct((1, F), F32), jax.ShapeDtypeStruct((1, F), F32)],
        compiler_params=_cp("parallel", "arbitrary"))(a, a, a, a, a, a, du, du, w, w, b, b)
    return da, jnp.concatenate([dwg, dwv], axis=1), jnp.concatenate([dbg, dbv], axis=1)


def _ada_fwd(c_all, ada_w, ada_b):
    R, D = c_all.shape
    L, _, Ns = ada_w.shape
    tn = _pick(Ns, (512, 256, 128))

    def body(c_ref, w_ref, b_ref, o_ref):
        cv = c_ref[...]
        cond = cv * _sigmoid(cv)
        o_ref[...] = _dg(cond, w_ref[...], 1, 0) + b_ref[...]

    return pl.pallas_call(
        body, name="ada_fwd", grid=(L, Ns // tn),
        in_specs=[pl.BlockSpec((R, D), lambda l, j: (0, 0)), pl.BlockSpec((None, D, tn), lambda l, j: (l, 0, j)),
                  pl.BlockSpec((None, 1, tn), lambda l, j: (l, 0, j))],
        out_specs=pl.BlockSpec((None, R, tn), lambda l, j: (l, 0, j)),
        out_shape=jax.ShapeDtypeStruct((L, R, Ns), F32), compiler_params=_cp("parallel", "parallel"))(c_all, ada_w, ada_b)


def _adam_math(w, g, m, v):
    m2 = ADAM_B1 * m + (1.0 - ADAM_B1) * g
    v2 = ADAM_B2 * v + (1.0 - ADAM_B2) * (g * g)
    m_hat = m2 / (1.0 - ADAM_B1 ** ADAM_STEP)
    v_hat = v2 / (1.0 - ADAM_B2 ** ADAM_STEP)
    delta = -ADAM_LR * (m_hat / (jnp.sqrt(v_hat) + ADAM_EPS) + ADAM_WD * w)
    return delta, m2, v2


def _ada_grad_adam(c_all_t, dmod, w, m, v):
    D, R = c_all_t.shape
    L, _, Ns = dmod.shape
    tr = _rows_within(D, Ns * 4, 1 << 21)

    def body(c_ref, d_ref, w_ref, m_ref, v_ref, g_ref, dl_ref, m2_ref, v2_ref):
        cv = c_ref[...]
        g = _dg(cv * _sigmoid(cv), d_ref[...], 1, 0)
        g_ref[...] = g
        dl_ref[...], m2_ref[...], v2_ref[...] = _adam_math(w_ref[...], g, m_ref[...], v_ref[...])

    big = pl.BlockSpec((None, tr, Ns), lambda l, i: (l, i, 0))
    shp = jax.ShapeDtypeStruct((L, D, Ns), F32)
    return pl.pallas_call(
        body, name="ada_grad_adam", grid=(L, D // tr),
        in_specs=[pl.BlockSpec((tr, R), lambda l, i: (i, 0)), pl.BlockSpec((None, R, Ns), lambda l, i: (l, 0, 0)), big, big, big],
        out_specs=[big] * 4, out_shape=[shp] * 4, compiler_params=_cp("parallel", "parallel"))(c_all_t, dmod, w, m, v)


def _adam(name, w, g, m, v):
    R, C = w.shape
    tr = _rows_within(R, C * 4, 3 << 20)

    def body(w_ref, g_ref, m_ref, v_ref, dl_ref, m2_ref, v2_ref):
        dl_ref[...], m2_ref[...], v2_ref[...] = _adam_math(w_ref[...], g_ref[...], m_ref[...], v_ref[...])

    blk = pl.BlockSpec((tr, C), lambda i: (i, 0))
    shp = jax.ShapeDtypeStruct((R, C), F32)
    return pl.pallas_call(body, name=name, grid=(R // tr,), in_specs=[blk] * 4, out_specs=[blk] * 3,
                          out_shape=[shp] * 3, compiler_params=_cp("parallel"))(w, g, m, v)


def _cast_into_rows(name, w, chip, after=None):
    _, R, C = w.shape
    tr = _rows_within(R, C * 4, 1 << 22)
    extra = [] if after is None else [after]

    def body(chip_ref, w_ref, *rest):
        o_ref = rest[-1]
        o_ref[...] = w_ref[...].astype(BF16)

    grid_spec = pltpu.PrefetchScalarGridSpec(
        num_scalar_prefetch=1, grid=(2, R // tr),
        in_specs=[pl.BlockSpec((None, tr, C), lambda h, i, s: (h, i, 0))] + [ANY] * len(extra),
        out_specs=pl.BlockSpec((None, tr, C), lambda h, i, s: (2 * s[0] + h, i, 0)))
    return pl.pallas_call(body, name=name, grid_spec=grid_spec, out_shape=jax.ShapeDtypeStruct((N_DEV, R, C), BF16),
                          compiler_params=_cp("arbitrary", "arbitrary"))(chip.reshape(1).astype(jnp.int32), w, *extra)


def _add_pairs(name, eight, from_sib, c):
    _, R, C = from_sib.shape
    tr = _rows_within(R, C * 2, 1 << 22)

    def body(c_ref, a_ref, b_ref, o_ref):
        o_ref[...] = (a_ref[...].astype(F32) + b_ref[...].astype(F32)).astype(o_ref.dtype)

    blk = pl.BlockSpec((None, tr, C), lambda j, i, s: (j, i, 0))
    grid_spec = pltpu.PrefetchScalarGridSpec(
        num_scalar_prefetch=1, grid=(4, R // tr),
        in_specs=[pl.BlockSpec((None, tr, C), lambda j, i, s: (2 * j + s[0], i, 0)), blk], out_specs=blk)
    return pl.pallas_call(body, name=name, grid_spec=grid_spec, out_shape=jax.ShapeDtypeStruct(from_sib.shape, BF16),
                          compiler_params=_cp("arbitrary", "arbitrary"))(c.reshape(1).astype(jnp.int32), eight, from_sib)


def _sum_into_pair(name, own, landed, slot, chip):
    n, R, C = landed.shape
    tr = _rows_within(R, (n + 1) * C * landed.dtype.itemsize, 1 << 23)

    def body(idx_ref, own_ref, x_ref, o_ref):
        mine = idx_ref[1]
        acc = None
        for j in range(n):
            part = jnp.where(mine == j, own_ref[...], x_ref[j]).astype(F32)
            acc = part if acc is None else acc + part
        o_ref[...] = acc

    grid_spec = pltpu.PrefetchScalarGridSpec(
        num_scalar_prefetch=1, grid=(R // tr,),
        in_specs=[pl.BlockSpec((None, tr, C), lambda i, s: (s[1], i, 0)), pl.BlockSpec((n, tr, C), lambda i, s: (0, i, 0))],
        out_specs=pl.BlockSpec((None, tr, C), lambda i, s: (s[0], i, 0)))
    idx = jnp.stack([slot, chip]).astype(jnp.int32)
    return pl.pallas_call(body, name=name, grid_spec=grid_spec, out_shape=jax.ShapeDtypeStruct((2, R, C), F32),
                          compiler_params=_cp("arbitrary"))(idx, own, landed)


def _sum_leading(name, a, out_dtype=F32):
    n, R, C = a.shape
    tr = _rows_within(R, n * C * a.dtype.itemsize, 1 << 24)

    def body(a_ref, o_ref):
        acc = a_ref[0].astype(F32)
        for j in range(1, n):
            acc = acc + a_ref[j].astype(F32)
        o_ref[...] = acc.astype(o_ref.dtype)

    return pl.pallas_call(body, name=name, grid=(R // tr,), in_specs=[pl.BlockSpec((n, tr, C), lambda i: (0, i, 0))],
                          out_specs=pl.BlockSpec((tr, C), lambda i: (i, 0)),
                          out_shape=jax.ShapeDtypeStruct((R, C), out_dtype), compiler_params=_cp("parallel"))(a)


def _place():
    return lax.axis_index("x"), lax.axis_index("y"), lax.axis_index("c")


def _all_gather(name, blocks, halves=False, after=None):
    n = len(blocks)
    shapes = [b.shape[1:] if halves else b.shape for b in blocks]
    extra = [] if after is None else [after]

    def body(*refs):
        ins, outs = refs[:n], refs[n + len(extra):2 * n + len(extra)]
        send_sems, recv_sems, local_sems = refs[2 * n + len(extra):]
        x, y, c = _place()
        me, sibling = (x, y, c), (x, y, 1 - c)
        chips = [(1 - x, y), (x, 1 - y), (1 - x, 1 - y)]

        def rows(a, px, py, pc):
            return outs[a].at[4 * px + 2 * py + pc]

        def copy(a, k, block, to, src=None):
            return pltpu.make_async_remote_copy(
                src_ref=rows(a, *block) if src is None else src, dst_ref=rows(a, *block),
                send_sem=send_sems.at[7 * a + k], recv_sem=recv_sems.at[7 * a + k],
                device_id=to, device_id_type=MESH)

        started = []
        mine = []
        for a in range(n):
            src = ins[a].at[c] if halves else ins[a]
            mine.append(pltpu.make_async_copy(src, rows(a, *me), local_sems.at[a]))
            mine[-1].start()
            first = [copy(a, 0, me, sibling, src=src)]
            first += [copy(a, 1 + j, me, (*chip, c), src=src) for j, chip in enumerate(chips)]
            for cp in first:
                cp.start()
            started += first
        for j, chip in enumerate(chips):
            for a in range(n):
                copy(a, 1 + j, (*chip, c), me).wait_recv()
                passed = copy(a, 4 + j, (*chip, c), sibling)
                passed.start()
                started.append(passed)
        for a in range(n):
            copy(a, 0, sibling, me).wait_recv()
            for j, chip in enumerate(chips):
                copy(a, 4 + j, (*chip, 1 - c), me).wait_recv()
        for cp in started:
            cp.wait_send()
        for cp in mine:
            cp.wait()

    return pl.pallas_call(
        body, name=name, in_specs=[ANY] * (n + len(extra)), out_specs=[ANY] * n,
        out_shape=[jax.ShapeDtypeStruct((N_DEV,) + tuple(s), b.dtype) for s, b in zip(shapes, blocks)],
        scratch_shapes=[pltpu.SemaphoreType.DMA((7 * n,)), pltpu.SemaphoreType.DMA((7 * n,)),
                        pltpu.SemaphoreType.DMA((n,))],
    )(*blocks, *extra)


def _share_halves(name, arrays):
    n = len(arrays)

    def body(*refs):
        ins, outs = refs[:n], refs[n:2 * n]
        send_sems, recv_sems = refs[2 * n:]
        x, y, c = _place()
        started = []
        for a in range(n):
            cp = pltpu.make_async_remote_copy(src_ref=ins[a].at[c], dst_ref=outs[a].at[c], send_sem=send_sems.at[a],
                                              recv_sem=recv_sems.at[a], device_id=(x, y, 1 - c), device_id_type=MESH)
            cp.start()
            started.append(cp)
        for a in range(n):
            started[a].wait_send()
            pltpu.make_async_remote_copy(src_ref=ins[a].at[1 - c], dst_ref=outs[a].at[1 - c], send_sem=send_sems.at[a],
                                         recv_sem=recv_sems.at[a], device_id=(x, y, 1 - c), device_id_type=MESH).wait_recv()

    return pl.pallas_call(
        body, name=name, in_specs=[ANY] * n, out_specs=[ANY] * n,
        out_shape=[jax.ShapeDtypeStruct(a.shape, a.dtype) for a in arrays],
        input_output_aliases={a: a for a in range(n)},
        scratch_shapes=[pltpu.SemaphoreType.DMA((n,)), pltpu.SemaphoreType.DMA((n,))],
    )(*arrays)


HBM = pl.BlockSpec(memory_space=pltpu.HBM)
SEM = pl.BlockSpec(memory_space=pltpu.SEMAPHORE)
EFFECT = pltpu.SideEffectType.DATAFLOW_SIDE_EFFECTING


COPIES_PER_ARRAY = {"rows": 3, "fill": 3, "parts": 3, "halves": 4}


def _chip_copies(kind, srcs, dsts, send_sems, recv_sems):
    x, y, c = _place()
    mine = 2 * x + y
    per = COPIES_PER_ARRAY[kind]
    sends, arrivals = [], []
    for a in range(len(srcs)):
        if kind == "halves":
            for j in range(N_CHIP):
                cp = pltpu.make_async_remote_copy(
                    src_ref=srcs[a].at[2 * j + 1 - c], dst_ref=dsts[a].at[j], send_sem=send_sems.at[per * a + j],
                    recv_sem=recv_sems.at[per * a + j], device_id=(x, y, 1 - c), device_id_type=MESH)
                sends.append(cp)
                arrivals.append(cp)
            continue
        for k, (px, py) in enumerate([(1 - x, y), (x, 1 - y), (1 - x, 1 - y)]):
            other = 2 * px + py
            if kind == "fill":
                cp = dict(send_sem=send_sems.at[per * a + k], recv_sem=recv_sems.at[per * a + k], device_id=(x, y, 1 - c),
                          device_id_type=MESH)
                sends.append(pltpu.make_async_remote_copy(src_ref=srcs[a].at[2 * other + c], dst_ref=dsts[a].at[2 * other + c], **cp))
                arrivals.append(pltpu.make_async_remote_copy(src_ref=srcs[a].at[2 * other + c],
                                                             dst_ref=dsts[a].at[2 * other + 1 - c], **cp))
                continue
            if kind == "rows":
                src, dst, lands = srcs[a].at[2 * mine + c], dsts[a].at[2 * mine + c], dsts[a].at[2 * other + c]
            else:
                src, dst, lands = srcs[a].at[other], dsts[a].at[mine], dsts[a].at[other]
            sem = dict(send_sem=send_sems.at[per * a + k], recv_sem=recv_sems.at[per * a + k], device_id=(px, py, c),
                       device_id_type=MESH)
            sends.append(pltpu.make_async_remote_copy(src_ref=src, dst_ref=dst, **sem))
            arrivals.append(pltpu.make_async_remote_copy(src_ref=src, dst_ref=lands, **sem))
    return sends, arrivals


def _chips_start(name, kind, srcs, dsts=None, after=None):
    n = len(srcs)
    bufs = list(srcs) + (list(dsts) if dsts is not None else [])
    nb = len(bufs)
    extra = [] if after is None else [after]

    def body(*refs):
        ins = refs[:nb]
        send_sems, recv_sems = refs[nb + len(extra)], refs[nb + len(extra) + 1]
        token = refs[-1]
        sends, _ = _chip_copies(kind, ins[:n], ins[n:] if dsts is not None else ins[:n], send_sems, recv_sems)
        for cp in sends:
            cp.start()
        token[...] = jnp.zeros_like(token)

    out = pl.pallas_call(
        body, name=name,
        out_shape=(pltpu.SemaphoreType.DMA((COPIES_PER_ARRAY[kind] * n,)), pltpu.SemaphoreType.DMA((COPIES_PER_ARRAY[kind] * n,)),
                   *[pltpu.HBM(b.shape, b.dtype) for b in bufs], jax.ShapeDtypeStruct((8, HEAD), F32)),
        in_specs=(HBM,) * nb + (ANY,) * len(extra),
        out_specs=(SEM, SEM) + (HBM,) * nb + (pl.BlockSpec(memory_space=pltpu.VMEM),),
        input_output_aliases={i: 2 + i for i in range(nb)},
        compiler_params=pltpu.CompilerParams(has_side_effects=EFFECT),
    )(*[pltpu.with_memory_space_constraint(b, pltpu.HBM) for b in bufs], *extra)
    return out[0], out[1], list(out[2:2 + nb]), out[-1]


def _chips_wait(name, kind, n, send_sems, recv_sems, bufs, after):
    nb = len(bufs)

    def body(*refs):
        ins = refs[:nb]
        s_sems, r_sems = refs[nb], refs[nb + 1]
        sends, arrivals = _chip_copies(kind, ins[:n], ins[n:] if nb > n else ins[:n], s_sems, r_sems)
        for cp in sends:
            cp.wait_send()
        for cp in arrivals:
            cp.wait_recv()

    return list(pl.pallas_call(
        body, name=name, out_shape=tuple(pltpu.HBM(b.shape, b.dtype) for b in bufs),
        in_specs=(HBM,) * nb + (SEM, SEM, pl.BlockSpec(memory_space=pl.ANY)), out_specs=(HBM,) * nb,
        input_output_aliases={i: i for i in range(nb)},
        compiler_params=pltpu.CompilerParams(has_side_effects=EFFECT),
    )(*bufs, send_sems, recv_sems, after))


def _fill_from_sibling(name, arrays):
    n = len(arrays)

    def body(*refs):
        ins, outs = refs[:n], refs[n:2 * n]
        send_sems, recv_sems = refs[2 * n:]
        x, y, c = _place()
        sends, arrivals = [], []
        for a in range(n):
            for k, (px, py) in enumerate([(1 - x, y), (x, 1 - y), (1 - x, 1 - y)]):
                sem = dict(send_sem=send_sems.at[3 * a + k], recv_sem=recv_sems.at[3 * a + k], device_id=(x, y, 1 - c),
                           device_id_type=MESH)
                row = 2 * (2 * px + py)
                sends.append(pltpu.make_async_remote_copy(src_ref=ins[a].at[row + c], dst_ref=outs[a].at[row + c], **sem))
                arrivals.append(pltpu.make_async_remote_copy(src_ref=ins[a].at[row + c], dst_ref=outs[a].at[row + 1 - c], **sem))
        for cp in sends:
            cp.start()
        for cp in sends:
            cp.wait_send()
        for cp in arrivals:
            cp.wait_recv()

    return pl.pallas_call(
        body, name=name, in_specs=[ANY] * n, out_specs=[ANY] * n,
        out_shape=[jax.ShapeDtypeStruct(a.shape, a.dtype) for a in arrays],
        input_output_aliases={a: a for a in range(n)},
        scratch_shapes=[pltpu.SemaphoreType.DMA((3 * n,)), pltpu.SemaphoreType.DMA((3 * n,))],
    )(*arrays)


def kernel(x, c, ada_w, ada_b, mix_norm, ffn_norm, par_w_in, par_w_out, hg_lb_logits, hg_out_norm, sg_w_in, sg_v_gain, sg_v_bias, sg_w_pos, sg_b_pos, sg_w_out, ffn_up, ffn_conv_w, ffn_conv_b, ffn_down, final_norm, loss_target, m_ada_w, m_ada_b, m_mix_norm, m_ffn_norm, m_par_w_in, m_par_w_out, m_hg_lb_logits, m_hg_out_norm, m_sg_w_in, m_sg_v_gain, m_sg_v_bias, m_sg_w_pos, m_sg_b_pos, m_sg_w_out, m_ffn_up, m_ffn_conv_w, m_ffn_conv_b, m_ffn_down, m_final_norm, v_ada_w, v_ada_b, v_mix_norm, v_ffn_norm, v_par_w_in, v_par_w_out, v_hg_lb_logits, v_hg_out_norm, v_sg_w_in, v_sg_v_gain, v_sg_v_bias, v_sg_w_pos, v_sg_b_pos, v_sg_w_out, v_ffn_up, v_ffn_conv_w, v_ffn_conv_b, v_ffn_down, v_final_norm):
    names = ["ada_w", "ada_b", "mix_norm", "ffn_norm", "par_w_in", "par_w_out", "hg_lb_logits", "hg_out_norm", "sg_w_in",
             "sg_v_gain", "sg_v_bias", "sg_w_pos", "sg_b_pos", "sg_w_out", "ffn_up", "ffn_conv_w", "ffn_conv_b",
             "ffn_down", "final_norm"]
    W = dict(zip(names, [ada_w, ada_b, mix_norm, ffn_norm, par_w_in, par_w_out, hg_lb_logits, hg_out_norm, sg_w_in,
                         sg_v_gain, sg_v_bias, sg_w_pos, sg_b_pos, sg_w_out, ffn_up, ffn_conv_w, ffn_conv_b, ffn_down,
                         final_norm]))
    M = dict(zip(names, [m_ada_w, m_ada_b, m_mix_norm, m_ffn_norm, m_par_w_in, m_par_w_out, m_hg_lb_logits, m_hg_out_norm,
                         m_sg_w_in, m_sg_v_gain, m_sg_v_bias, m_sg_w_pos, m_sg_b_pos, m_sg_w_out, m_ffn_up, m_ffn_conv_w,
                         m_ffn_conv_b, m_ffn_down, m_final_norm]))
    V = dict(zip(names, [v_ada_w, v_ada_b, v_mix_norm, v_ffn_norm, v_par_w_in, v_par_w_out, v_hg_lb_logits, v_hg_out_norm,
                         v_sg_w_in, v_sg_v_gain, v_sg_v_bias, v_sg_w_pos, v_sg_b_pos, v_sg_w_out, v_ffn_up, v_ffn_conv_w,
                         v_ffn_conv_b, v_ffn_down, v_final_norm]))

    x = x[0]
    target = loss_target[0]
    T, D = x.shape
    ix, iy, ic = _place()
    chip = 2 * ix + iy
    dev = 2 * chip + ic
    H = hg_out_norm.shape[1]
    SBW = H * HEAD
    NA = ada_w.shape[2]
    F2s = ffn_up.shape[2]
    F2 = N_CHIP * F2s
    SGW = sg_w_out.shape[1] * N_CHIP
    G = sg_w_pos.shape[1]

    shards = [par_w_in[0], par_w_out[0], sg_w_in[0], sg_w_out[0], ffn_up[0], ffn_up[1], ffn_down[0], ffn_down[1]]
    kinds = ["col", "row", "col", "row", "col", "col", "row", "row"]
    halves = [w.reshape(2, w.shape[0] // 2, w.shape[1]) for w in shards]
    groups = {"a": [0], "b": [1, 4, 6], "c": [2, 3, 5, 7]}
    rows8 = {0: _cast_into_rows("cast_w", halves[0], chip)}
    started = {}

    def as_weights(g, bufs):
        out = {}
        for i, g8 in zip(groups[g], bufs):
            K, N = shards[i].shape
            out[i] = g8.reshape(N_CHIP, K, N) if kinds[i] == "col" else g8.reshape(N_CHIP * K, N)
        return out

    def weights_landed(g, after):
        send_sems, recv_sems, bufs, _ = started[g]
        bufs = _chips_wait("gather_wait_" + g, "rows", len(bufs), send_sems, recv_sems, bufs, after)
        return _chips_start("gather_fill_start_" + g, "fill", bufs)

    def weights_of(g, filling, after):
        send_sems, recv_sems, bufs, _ = filling
        return as_weights(g, _chips_wait("gather_fill_wait_" + g, "fill", len(bufs), send_sems, recv_sems, bufs, after))

    n_cw = ffn_conv_w.size
    n_sv = sg_v_gain.size
    c_all, small_all = _all_gather("gather_small", [c, _pack_rows([ffn_conv_w, sg_v_gain, sg_v_bias])])
    c_all = c_all.reshape(N_DEV, D)
    small_all = small_all.reshape(N_CHIP, 2, -1)[:, 0]
    conv_w_full = small_all[:, :n_cw].reshape(N_CHIP, 2, CONV_WIDTH, F2s).transpose(1, 2, 0, 3).reshape(2, CONV_WIDTH, F2)
    sg_gain_full = small_all[:, n_cw:n_cw + n_sv].reshape(1, SGW)
    sg_bias_full = small_all[:, n_cw + n_sv:n_cw + 2 * n_sv].reshape(1, SGW)

    c_pad = jnp.pad(c_all, ((0, 16 - N_DEV), (0, 0)))
    ada_b_sh = lax.dynamic_slice(ada_b, (0, chip * NA), (2, NA)).reshape(2, 1, NA)
    mod_sh = _ada_fwd(c_pad, ada_w, ada_b_sh)
    mod_all, = _all_gather("gather_mod", [mod_sh[:, :N_DEV]])
    mod_all = mod_all.reshape(N_CHIP, 2, 2, N_DEV, NA)[:, 0]
    mod = lax.dynamic_index_in_dim(mod_all, dev, axis=2, keepdims=False)
    mod = mod.transpose(1, 0, 2).reshape(2, 6, D)
    mods = [[mod[l, k].reshape(1, D) for k in range(6)] for l in range(2)]
    started["a"] = _chips_start("gather_start_a", "rows", [rows8[0]], after=mod)
    for i in range(1, len(shards)):
        rows8[i] = _cast_into_rows("cast_w", halves[i], chip, after=started["a"][3])

    vec = lambda a: a.reshape(1, -1)
    l0 = vec(hg_lb_logits[0])
    l1 = vec(hg_lb_logits[1])
    hg_gain = vec(hg_out_norm[0])
    wpos = sg_w_pos[0]
    bpos = sg_b_pos[0].reshape(G, SG_CHUNK, 1)
    conv_b = [vec(ffn_conv_b[l]) for l in range(2)]

    sh1, sc1, g1, sh2, sc2, g2 = mods[0]
    send_a, recv_a, bufs_a, _ = started["a"]
    bufs_a = _chips_wait("gather_wait_a", "rows", 1, send_a, recv_a, bufs_a, rows8[len(shards) - 1])
    bufs_a = _fill_from_sibling("gather_fill_a", bufs_a)
    started["b"] = _chips_start("gather_start_b", "rows", [rows8[i] for i in groups["b"]], after=bufs_a[0])
    start_token = started["a"][3][0, 0] + started["b"][3][0, 0]
    w_in = as_weights("a", bufs_a)[0]
    h0 = _normmod_fwd("norm_mix0", x, vec(mix_norm[0]) + start_token, sc1, sh1)
    proj = _mm_nn("mm_par_in", h0, w_in)
    o_sb, sb_tot = _sb_fwd(proj, H)
    filling_b = weights_landed("b", o_sb)
    started["c"] = _chips_start("gather_start_c", "rows", [rows8[i] for i in groups["c"]], after=o_sb)
    o_hg, hg_states = _hg_fwd(proj, l0 + filling_b[3][0:1, 0:1] + started["c"][3][0:1, 0:1], l1, hg_gain, H, 3 * H)
    o_cat = jnp.concatenate([o_sb, o_hg], axis=1)
    wb = weights_of("b", filling_b, o_cat)
    w_out, wup, wdn = wb[1], [wb[4], None], [wb[6], None]
    y0 = _mm_nn("mm_par_out", o_cat, w_out)
    x1, h0f = _res_normmod_fwd("res_norm_ffn0", x, y0, g1, vec(ffn_norm[0]), sc2, sh2)
    a0 = _mm_nn("mm_up0", h0f, wup[0])
    u0 = _conv_fwd("conv_fwd0", a0, conv_w_full[0], conv_b[0])
    filling_c = weights_landed("c", u0)
    f0 = _mm_nn("mm_down0", u0, wdn[0])
    sh1b, sc1b, g1b, sh2b, sc2b, g2b = mods[1]
    x2, h1 = _res_normmod_fwd("res_norm_mix1", x1, f0, g2, vec(mix_norm[1]) + filling_c[3][0:1, 0:1], sc1b, sh1b)
    wc = weights_of("c", filling_c, h1)
    wsg_in, wsg_out, wup[1], wdn[1] = wc[2], wc[3], wc[5], wc[7]
    zpre = _mm_nn("mm_sg_in", h1, wsg_in)
    s1 = _sg_fwd(zpre, sg_gain_full, sg_bias_full, wpos, bpos)
    y1 = _mm_nn("mm_sg_out", s1, wsg_out)
    x3, h1f = _res_normmod_fwd("res_norm_ffn1", x2, y1, g1b, vec(ffn_norm[1]), sc2b, sh2b)
    a1 = _mm_nn("mm_up1", h1f, wup[1])
    u1 = _conv_fwd("conv_fwd1", a1, conv_w_full[1], conv_b[1])
    f1 = _mm_nn("mm_down1", u1, wdn[1])
    loss_sum, dx, df1, dg2b, d_final = _final_fwd_bwd(x3, f1, g2b, vec(final_norm), target)
    loss = lax.psum(loss_sum[0, 0], ("x", "y", "c"))

    def reduce_start(tag, idx, grads):
        eights = [g.reshape((N_DEV, -1, g.shape[-1])) for g in grads]
        landing = [lax.empty((N_CHIP,) + e.shape[1:], e.dtype) for e in eights]
        send_sems, recv_sems, bufs, token = _chips_start("grads_sibling_start_" + tag, "halves", eights, landing)
        return (tag, idx, send_sems, recv_sems, bufs), token[0:1, 0:1]

    def reduce_cross(state, after):
        tag, idx, send_sems, recv_sems, bufs = state
        n = len(idx)
        bufs = _chips_wait("grads_sibling_wait_" + tag, "halves", n, send_sems, recv_sems, bufs, after)
        pair = [_add_pairs("add_pair", e, r, ic) for e, r in zip(bufs[:n], bufs[n:])]
        landing = [lax.empty(p.shape, p.dtype) for p in pair]
        send_sems, recv_sems, bufs, token = _chips_start("grads_start_" + tag, "parts", pair, landing)
        return (tag, idx, send_sems, recv_sems, bufs), token[0:1, 0:1]

    def reduce_finish(state, after):
        tag, idx, send_sems, recv_sems, bufs = state
        n = len(idx)
        bufs = _chips_wait("grads_wait_" + tag, "parts", n, send_sems, recv_sems, bufs, after)
        halves = [_sum_into_pair("sum_chips", p, x_, ic, chip) for p, x_ in zip(bufs[:n], bufs[n:])]
        both = _share_halves("grads_share_" + tag, halves)
        return {i: b.reshape(shards[i].shape) for i, b in zip(idx, both)}

    def ffn_bwd(l, dfl, u, a, hf):
        g_dn = _mm_tn("mm_g_down", u, dfl)
        du = _mm_nt("mm_d_u", dfl, wdn[l])
        da, dcw, dcb = _conv_bwd("conv_bwd", a, du, conv_w_full[l], conv_b[l])
        g_up = _mm_tn("mm_g_up", hf, da, chunks=N_CHIP)
        dh = _mm_nt("mm_d_hf", da, wup[l])
        return g_dn, g_up, dcw, dcb, dh

    g_dn1, g_up1, dcw1, dcb1, dh1f = ffn_bwd(1, df1, u1, a1, h1f)
    red1, tok = reduce_start("1", [5, 7], [g_up1, g_dn1])
    dx, dgn_f1, dsc2b, dsh2b, dy1, dg1b = _block_bwd("bwd_ffn1", dx, dh1f, x3, vec(ffn_norm[1]) + tok, sc2b, sh2b, y1, g1b)
    g_sg_out = _mm_tn("mm_g_sg_out", s1, dy1)
    ds1 = _mm_nt("mm_d_s", dy1, wsg_out)
    dzpre, dsg_gain, dsg_bias, dwpos, dbpos = _sg_bwd(zpre, ds1, sg_gain_full, sg_bias_full, wpos, bpos)
    red1, tok_x = reduce_cross(red1, dzpre)
    g_sg_in = _mm_tn("mm_g_sg_in", h1, dzpre, chunks=N_CHIP)
    dh1 = _mm_nt("mm_d_h1", dzpre, wsg_in)
    red2, tok = reduce_start("2", [2, 3], [g_sg_in, g_sg_out])
    dx, dgn_m1, dsc1b, dsh1b, df0, dg2 = _block_bwd("bwd_mix1", dx, dh1, x2, vec(mix_norm[1]) + tok + tok_x, sc1b, sh1b, f0, g2)
    g_dn0, g_up0, dcw0, dcb0, dh0f = ffn_bwd(0, df0, u0, a0, h0f)
    red2, tok_x = reduce_cross(red2, dh0f)
    red3, tok = reduce_start("3", [4, 6], [g_up0, g_dn0])
    dx, dgn_f0, dsc2, dsh2, dy0, dg1 = _block_bwd("bwd_ffn0", dx, dh0f, x1, vec(ffn_norm[0]) + tok + tok_x, sc2, sh2, y0, g1)
    g_out = _mm_tn("mm_g_par_out", o_cat, dy0)
    do = _mm_nt("mm_d_o", dy0, w_out)
    red3, tok_x = reduce_cross(red3, do)
    dhq, dhf, dhi, dhg, dl0, dl1, dhg_gain = _hg_bwd(proj, hg_states, do, l0 + tok_x, l1, hg_gain, H, 3 * H, H)
    dq, dk, dv = _sb_bwd(proj, do, sb_tot, H)
    dproj = jnp.concatenate([dq, dk, dv, dhq, dhf, dhi, dhg], axis=1).astype(BF16)
    g_in = _mm_tn("mm_g_par_in", h0, dproj, chunks=N_CHIP)
    red4, tok = reduce_start("4", [0, 1], [g_in, g_out])
    dh0 = _mm_nt("mm_d_h0", dproj, w_in)
    grad_x, dgn_m0, dsc1, dsh1 = _block_bwd("bwd_mix0", dx, dh0, x, vec(mix_norm[0]) + tok, sc1, sh1)
    red4, tok_x = reduce_cross(red4, grad_x)

    G_, delta, new_m, new_v = {}, {}, {}, {}

    def adam_on(nme):
        shp = W[nme].shape
        r2 = lambda a: a.reshape(-1, shp[-1])
        d_, m_, v_ = _adam("adam_" + nme, r2(W[nme]), r2(G_[nme]), r2(M[nme]), r2(V[nme]))
        delta[nme], new_m[nme], new_v[nme] = d_.reshape(shp), m_.reshape(shp), v_.reshape(shp)

    g_shards = {}
    for state in (red1, red2, red3):
        g_shards.update(reduce_finish(state, red4[4][0]))
    G_["sg_w_in"] = g_shards[2][None]
    G_["sg_w_out"] = g_shards[3][None]
    G_["ffn_up"] = jnp.stack([g_shards[4], g_shards[5]])
    G_["ffn_down"] = jnp.stack([g_shards[6], g_shards[7]])
    for nme in ["sg_w_in", "sg_w_out", "ffn_up", "ffn_down"]:
        adam_on(nme)

    dmod = jnp.concatenate([dsh1, dsc1, dg1, dsh2, dsc2, dg2, dsh1b, dsc1b, dg1b, dsh2b, dsc2b, dg2b], axis=1)
    parts = [dmod, dgn_m0, dgn_m1, dgn_f0, dgn_f1, dl0, dl1, dhg_gain, dsg_gain, dsg_bias, dwpos, dbpos,
             dcw0, dcw1, dcb0, dcb1, d_final]
    sizes = [p.size for p in parts]
    packed = _pack_rows(parts)
    packed_all, = _all_gather("gather_small_grads", [packed], after=new_v["ffn_down"])
    summed = _sum_leading("sum_small_grads", packed_all).reshape(-1)
    offs = [0]
    for s in sizes:
        offs.append(offs[-1] + s)
    red = [summed[offs[i]:offs[i + 1]] for i in range(len(parts))]
    (r_dmod, r_gm0, r_gm1, r_gf0, r_gf1, r_l0, r_l1, r_hgain, r_sgain, r_sbias, r_wpos, r_bpos,
     r_cw0, r_cw1, r_cb0, r_cb1, r_final) = red
    n_mod = sizes[0]
    dmod_all = packed_all.reshape(N_DEV, -1)[:, :n_mod].reshape(N_DEV, 2, 6 * D)

    G_["ada_b"] = r_dmod.reshape(2, 6 * D)
    G_["mix_norm"] = jnp.stack([r_gm0, r_gm1])
    G_["ffn_norm"] = jnp.stack([r_gf0, r_gf1])
    G_["hg_lb_logits"] = jnp.stack([r_l0, r_l1])
    G_["hg_out_norm"] = r_hgain.reshape(hg_out_norm.shape)
    G_["sg_v_gain"] = lax.dynamic_slice(r_sgain, (chip * n_sv,), (n_sv,)).reshape(sg_v_gain.shape)
    G_["sg_v_bias"] = lax.dynamic_slice(r_sbias, (chip * n_sv,), (n_sv,)).reshape(sg_v_bias.shape)
    G_["sg_w_pos"] = r_wpos.reshape(sg_w_pos.shape)
    G_["sg_b_pos"] = r_bpos.reshape(sg_b_pos.shape)
    cw_full = jnp.stack([r_cw0.reshape(CONV_WIDTH, F2), r_cw1.reshape(CONV_WIDTH, F2)])
    G_["ffn_conv_w"] = lax.dynamic_slice(cw_full, (0, 0, chip * F2s), (2, CONV_WIDTH, F2s))
    G_["ffn_conv_b"] = jnp.stack([r_cb0, r_cb1])
    G_["final_norm"] = r_final

    c_t = jnp.pad(c_all, ((0, HEAD - N_DEV), (0, 0))).T
    dmod_sh = lax.dynamic_slice(dmod_all.transpose(1, 0, 2), (0, 0, chip * NA), (2, N_DEV, NA))
    dmod_sh = jnp.pad(dmod_sh, ((0, 0), (0, HEAD - N_DEV), (0, 0)))
    G_["ada_w"], delta["ada_w"], new_m["ada_w"], new_v["ada_w"] = _ada_grad_adam(c_t, dmod_sh, ada_w, m_ada_w, v_ada_w)

    g_shards.update(reduce_finish(red4, G_["ada_w"]))
    G_["par_w_in"] = g_shards[0][None]
    G_["par_w_out"] = g_shards[1][None]
    for nme in ["par_w_in", "par_w_out"]:
        adam_on(nme)
    small = [n_ for n_ in names if n_ not in delta]
    pk = lambda dct: _pack_rows([dct[n_] for n_ in small])
    d_, m_, v_ = _adam("adam_small", pk(W), pk(G_), pk(M), pk(V))
    off = 0
    for n_ in small:
        sz = W[n_].size
        for dst, src in ((delta, d_), (new_m, m_), (new_v, v_)):
            dst[n_] = src.reshape(-1)[off:off + sz].reshape(W[n_].shape)
        off += sz

    return (loss, grad_x[None], *[G_[n_] for n_ in names], *[delta[n_] for n_ in names],
            *[new_m[n_] for n_ in names], *[new_v[n_] for n_ in names])
```
